```python
import math
import jax, jax.numpy as jnp
from jax import lax
import numpy as np

D_MODEL = 1024
BATCH = 8
SEQ = 16384
DEPTH = 1

N_MEM = 256
D_MIX = D_MODEL
D_A = D_MIX // 2
D_B = D_MIX - D_A
CONV_A_W = 3
CONV_B_W = 31
D_IN_ALL = 3 * D_A + 2 * D_B
XA_HEADS = 4
XA_HEAD_DIM = D_MODEL // XA_HEADS
D_FF = int(math.ceil((8 * D_MODEL / 3) / 256) * 256)
RMS_EPS = 1e-6
LN_EPS = 1e-5

kernel_name = "hybrid_parallel_conv_groups_xattn_swiglu"


def rmsnorm(x, g):
    xf = x.astype(jnp.float32)
    y = xf * lax.rsqrt(jnp.mean(xf * xf, axis=-1, keepdims=True) + RMS_EPS)
    return (y * g.astype(jnp.float32)).astype(x.dtype)


def layernorm(x, g, b):
    xf = x.astype(jnp.float32)
    mu = jnp.mean(xf, axis=-1, keepdims=True)
    var = jnp.mean(jnp.square(xf - mu), axis=-1, keepdims=True)
    y = (xf - mu) * lax.rsqrt(var + LN_EPS)
    return (y * g.astype(jnp.float32) + b.astype(jnp.float32)).astype(x.dtype)


def causal_dwconv(u, w):
    k = w.shape[0]
    return lax.conv_general_dilated(
        u, w[:, None, :].astype(u.dtype),
        window_strides=(1,), padding=((k - 1, 0),),
        dimension_numbers=("NWC", "WIO", "NWC"),
        feature_group_count=u.shape[-1])


def _fwd_setup_inputs(seed: int = 0) -> dict:
    key = jax.random.key(seed)
    ks = jax.random.split(key, 24)
    f32 = jnp.float32

    def w(k, shape, fan_in):
        return jax.random.normal(k, shape, f32) * (fan_in ** -0.5)

    def gain(k, n):
        return jnp.ones((n,), f32) + 0.05 * jax.random.normal(k, (n,), f32)

    return {
        "x": jax.random.normal(ks[0], (BATCH, SEQ, D_MODEL), f32),
        "mem": jax.random.normal(ks[1], (BATCH, N_MEM, D_MODEL), f32),
        "mix_pre_g": gain(ks[2], D_MODEL),
        "w_mix_in": w(ks[3], (D_MODEL, D_IN_ALL), D_MODEL),
        "conv_a_w": w(ks[4], (CONV_A_W, D_A), CONV_A_W),
        "conv_b_w": w(ks[5], (CONV_B_W, D_B), CONV_B_W),
        "conv_b_b": 0.02 * jax.random.normal(ks[6], (D_B,), f32),
        "ln_b_g": gain(ks[7], D_B),
        "ln_b_b": 0.02 * jax.random.normal(ks[8], (D_B,), f32),
        "w_mix_out": w(ks[9], (D_MIX, D_MODEL), D_MIX),
        "mix_post_g": gain(ks[10], D_MODEL),
        "xa_pre_g": gain(ks[11], D_MODEL),
        "mem_norm_g": gain(ks[12], D_MODEL),
        "w_q": w(ks[13], (D_MODEL, XA_HEADS * XA_HEAD_DIM), D_MODEL),
        "w_k": w(ks[14], (D_MODEL, XA_HEADS * XA_HEAD_DIM), D_MODEL),
        "w_v": w(ks[15], (D_MODEL, XA_HEADS * XA_HEAD_DIM), D_MODEL),
        "w_o": w(ks[16], (XA_HEADS * XA_HEAD_DIM, D_MODEL), XA_HEADS * XA_HEAD_DIM),
        "xa_post_g": gain(ks[17], D_MODEL),
        "ffn_pre_g": gain(ks[18], D_MODEL),
        "w_gate": w(ks[19], (D_MODEL, D_FF), D_MODEL),
        "w_up": w(ks[20], (D_MODEL, D_FF), D_MODEL),
        "w_down": w(ks[21], (D_FF, D_MODEL), D_FF),
        "ffn_post_g": gain(ks[22], D_MODEL),
    }


def parallel_conv_mixer(h, w_mix_in, conv_a_w, conv_b_w, conv_b_b, ln_b_g, ln_b_b, w_mix_out):
    u = jnp.einsum("bsd,dc->bsc", h, w_mix_in.astype(h.dtype))
    b_a, c_a, v_a, glu_v, glu_g = jnp.split(
        u, [D_A, 2 * D_A, 3 * D_A, 3 * D_A + D_B], axis=-1)
    y_a = b_a * causal_dwconv(c_a * v_a, conv_a_w)
    z = glu_v * jax.nn.sigmoid(glu_g)
    z = causal_dwconv(z, conv_b_w) + conv_b_b.astype(z.dtype)
    y_b = jax.nn.silu(layernorm(z, ln_b_g, ln_b_b))
    y = jnp.concatenate([y_a, y_b], axis=-1)
    return jnp.einsum("bsc,cd->bsd", y, w_mix_out.astype(y.dtype))


def memory_cross_attention(h, mem_n, w_q, w_k, w_v, w_o):
    bsz, s, _ = h.shape
    m = mem_n.shape[1]
    q = jnp.einsum("bsd,de->bse", h, w_q.astype(h.dtype)).reshape(bsz, s, XA_HEADS, XA_HEAD_DIM)
    k = jnp.einsum("bmd,de->bme", mem_n, w_k.astype(h.dtype)).reshape(bsz, m, XA_HEADS, XA_HEAD_DIM)
    v = jnp.einsum("bmd,de->bme", mem_n, w_v.astype(h.dtype)).reshape(bsz, m, XA_HEADS, XA_HEAD_DIM)
    scores = jnp.einsum("bshd,bmhd->bhsm", q.astype(jnp.float32), k.astype(jnp.float32))
    p = jax.nn.softmax(scores * (XA_HEAD_DIM ** -0.5), axis=-1).astype(h.dtype)
    o = jnp.einsum("bhsm,bmhd->bshd", p, v).reshape(bsz, s, XA_HEADS * XA_HEAD_DIM)
    return jnp.einsum("bse,ed->bsd", o, w_o.astype(h.dtype))


def swiglu_ffn(h, w_gate, w_up, w_down):
    g = jnp.einsum("bsd,df->bsf", h, w_gate.astype(h.dtype))
    u = jnp.einsum("bsd,df->bsf", h, w_up.astype(h.dtype))
    return jnp.einsum("bsf,fd->bsd", jax.nn.silu(g) * u, w_down.astype(h.dtype))


def _fwd_reference(x, mem, mix_pre_g, w_mix_in, conv_a_w, conv_b_w, conv_b_b, ln_b_g, ln_b_b,
              w_mix_out, mix_post_g, xa_pre_g, mem_norm_g, w_q, w_k, w_v, w_o, xa_post_g,
              ffn_pre_g, w_gate, w_up, w_down, ffn_post_g):
    mem_n = rmsnorm(mem, mem_norm_g)
    for _ in range(DEPTH):
        x = x + rmsnorm(parallel_conv_mixer(rmsnorm(x, mix_pre_g), w_mix_in, conv_a_w,
                                            conv_b_w, conv_b_b, ln_b_g, ln_b_b, w_mix_out),
                        mix_post_g)
        x = x + rmsnorm(memory_cross_attention(rmsnorm(x, xa_pre_g), mem_n, w_q, w_k, w_v, w_o),
                        xa_post_g)
        x = x + rmsnorm(swiglu_ffn(rmsnorm(x, ffn_pre_g), w_gate, w_up, w_down), ffn_post_g)
    return x


import jax as _jax
import jax.numpy as _jnp

TWIN_FORMAT = 'train_step'
FWD_PARAMS = ['x', 'mem', 'mix_pre_g', 'w_mix_in', 'conv_a_w', 'conv_b_w', 'conv_b_b', 'ln_b_g', 'ln_b_b', 'w_mix_out', 'mix_post_g', 'xa_pre_g', 'mem_norm_g', 'w_q', 'w_k', 'w_v', 'w_o', 'xa_post_g', 'ffn_pre_g', 'w_gate', 'w_up', 'w_down', 'ffn_post_g']
TWIN_WEIGHTS = ['mix_pre_g', 'w_mix_in', 'conv_a_w', 'conv_b_w', 'conv_b_b', 'ln_b_g', 'ln_b_b', 'w_mix_out', 'mix_post_g', 'xa_pre_g', 'mem_norm_g', 'w_q', 'w_k', 'w_v', 'w_o', 'xa_post_g', 'ffn_pre_g', 'w_gate', 'w_up', 'w_down', 'ffn_post_g']
TWIN_DIFF_INPUT = 'x'
TWIN_INPUTS = ['x', 'mem', 'mix_pre_g', 'w_mix_in', 'conv_a_w', 'conv_b_w', 'conv_b_b', 'ln_b_g', 'ln_b_b', 'w_mix_out', 'mix_post_g', 'xa_pre_g', 'mem_norm_g', 'w_q', 'w_k', 'w_v', 'w_o', 'xa_post_g', 'ffn_pre_g', 'w_gate', 'w_up', 'w_down', 'ffn_post_g', 'loss_target', 'm_mix_pre_g', 'm_w_mix_in', 'm_conv_a_w', 'm_conv_b_w', 'm_conv_b_b', 'm_ln_b_g', 'm_ln_b_b', 'm_w_mix_out', 'm_mix_post_g', 'm_xa_pre_g', 'm_mem_norm_g', 'm_w_q', 'm_w_k', 'm_w_v', 'm_w_o', 'm_xa_post_g', 'm_ffn_pre_g', 'm_w_gate', 'm_w_up', 'm_w_down', 'm_ffn_post_g', 'v_mix_pre_g', 'v_w_mix_in', 'v_conv_a_w', 'v_conv_b_w', 'v_conv_b_b', 'v_ln_b_g', 'v_ln_b_b', 'v_w_mix_out', 'v_mix_post_g', 'v_xa_pre_g', 'v_mem_norm_g', 'v_w_q', 'v_w_k', 'v_w_v', 'v_w_o', 'v_xa_post_g', 'v_ffn_pre_g', 'v_w_gate', 'v_w_up', 'v_w_down', 'v_ffn_post_g']
TWIN_OUTPUTS = ['loss', 'grad_x', 'grad_mix_pre_g', 'grad_w_mix_in', 'grad_conv_a_w', 'grad_conv_b_w', 'grad_conv_b_b', 'grad_ln_b_g', 'grad_ln_b_b', 'grad_w_mix_out', 'grad_mix_post_g', 'grad_xa_pre_g', 'grad_mem_norm_g', 'grad_w_q', 'grad_w_k', 'grad_w_v', 'grad_w_o', 'grad_xa_post_g', 'grad_ffn_pre_g', 'grad_w_gate', 'grad_w_up', 'grad_w_down', 'grad_ffn_post_g', 'delta_mix_pre_g', 'delta_w_mix_in', 'delta_conv_a_w', 'delta_conv_b_w', 'delta_conv_b_b', 'delta_ln_b_g', 'delta_ln_b_b', 'delta_w_mix_out', 'delta_mix_post_g', 'delta_xa_pre_g', 'delta_mem_norm_g', 'delta_w_q', 'delta_w_k', 'delta_w_v', 'delta_w_o', 'delta_xa_post_g', 'delta_ffn_pre_g', 'delta_w_gate', 'delta_w_up', 'delta_w_down', 'delta_ffn_post_g', 'new_m_mix_pre_g', 'new_m_w_mix_in', 'new_m_conv_a_w', 'new_m_conv_b_w', 'new_m_conv_b_b', 'new_m_ln_b_g', 'new_m_ln_b_b', 'new_m_w_mix_out', 'new_m_mix_post_g', 'new_m_xa_pre_g', 'new_m_mem_norm_g', 'new_m_w_q', 'new_m_w_k', 'new_m_w_v', 'new_m_w_o', 'new_m_xa_post_g', 'new_m_ffn_pre_g', 'new_m_w_gate', 'new_m_w_up', 'new_m_w_down', 'new_m_ffn_post_g', 'new_v_mix_pre_g', 'new_v_w_mix_in', 'new_v_conv_a_w', 'new_v_conv_b_w', 'new_v_conv_b_b', 'new_v_ln_b_g', 'new_v_ln_b_b', 'new_v_w_mix_out', 'new_v_mix_post_g', 'new_v_xa_pre_g', 'new_v_mem_norm_g', 'new_v_w_q', 'new_v_w_k', 'new_v_w_v', 'new_v_w_o', 'new_v_xa_post_g', 'new_v_ffn_pre_g', 'new_v_w_gate', 'new_v_w_up', 'new_v_w_down', 'new_v_ffn_post_g']
TWIN_LEAF_KINDS = {'loss': 'loss', 'grad_x': 'grad_x', 'grad_mix_pre_g': 'grad_w', 'grad_w_mix_in': 'grad_w', 'grad_conv_a_w': 'grad_w', 'grad_conv_b_w': 'grad_w', 'grad_conv_b_b': 'grad_w', 'grad_ln_b_g': 'grad_w', 'grad_ln_b_b': 'grad_w', 'grad_w_mix_out': 'grad_w', 'grad_mix_post_g': 'grad_w', 'grad_xa_pre_g': 'grad_w', 'grad_mem_norm_g': 'grad_w', 'grad_w_q': 'grad_w', 'grad_w_k': 'grad_w', 'grad_w_v': 'grad_w', 'grad_w_o': 'grad_w', 'grad_xa_post_g': 'grad_w', 'grad_ffn_pre_g': 'grad_w', 'grad_w_gate': 'grad_w', 'grad_w_up': 'grad_w', 'grad_w_down': 'grad_w', 'grad_ffn_post_g': 'grad_w', 'delta_mix_pre_g': 'delta_w', 'delta_w_mix_in': 'delta_w', 'delta_conv_a_w': 'delta_w', 'delta_conv_b_w': 'delta_w', 'delta_conv_b_b': 'delta_w', 'delta_ln_b_g': 'delta_w', 'delta_ln_b_b': 'delta_w', 'delta_w_mix_out': 'delta_w', 'delta_mix_post_g': 'delta_w', 'delta_xa_pre_g': 'delta_w', 'delta_mem_norm_g': 'delta_w', 'delta_w_q': 'delta_w', 'delta_w_k': 'delta_w', 'delta_w_v': 'delta_w', 'delta_w_o': 'delta_w', 'delta_xa_post_g': 'delta_w', 'delta_ffn_pre_g': 'delta_w', 'delta_w_gate': 'delta_w', 'delta_w_up': 'delta_w', 'delta_w_down': 'delta_w', 'delta_ffn_post_g': 'delta_w', 'new_m_mix_pre_g': 'new_m', 'new_m_w_mix_in': 'new_m', 'new_m_conv_a_w': 'new_m', 'new_m_conv_b_w': 'new_m', 'new_m_conv_b_b': 'new_m', 'new_m_ln_b_g': 'new_m', 'new_m_ln_b_b': 'new_m', 'new_m_w_mix_out': 'new_m', 'new_m_mix_post_g': 'new_m', 'new_m_xa_pre_g': 'new_m', 'new_m_mem_norm_g': 'new_m', 'new_m_w_q': 'new_m', 'new_m_w_k': 'new_m', 'new_m_w_v': 'new_m', 'new_m_w_o': 'new_m', 'new_m_xa_post_g': 'new_m', 'new_m_ffn_pre_g': 'new_m', 'new_m_w_gate': 'new_m', 'new_m_w_up': 'new_m', 'new_m_w_down': 'new_m', 'new_m_ffn_post_g': 'new_m', 'new_v_mix_pre_g': 'new_v', 'new_v_w_mix_in': 'new_v', 'new_v_conv_a_w': 'new_v', 'new_v_conv_b_w': 'new_v', 'new_v_conv_b_b': 'new_v', 'new_v_ln_b_g': 'new_v', 'new_v_ln_b_b': 'new_v', 'new_v_w_mix_out': 'new_v', 'new_v_mix_post_g': 'new_v', 'new_v_xa_pre_g': 'new_v', 'new_v_mem_norm_g': 'new_v', 'new_v_w_q': 'new_v', 'new_v_w_k': 'new_v', 'new_v_w_v': 'new_v', 'new_v_w_o': 'new_v', 'new_v_xa_post_g': 'new_v', 'new_v_ffn_pre_g': 'new_v', 'new_v_w_gate': 'new_v', 'new_v_w_up': 'new_v', 'new_v_w_down': 'new_v', 'new_v_ffn_post_g': 'new_v'}


def _forward(args):
    return _fwd_reference(*[args[k] for k in FWD_PARAMS])


def _output_shape():
    def fwd():
        inp = _fwd_setup_inputs(0)
        return _fwd_reference(*[inp[k] for k in FWD_PARAMS])
    out = _jax.eval_shape(fwd)
    return out.shape, out.dtype

N_MICROBATCH = 1
ADAM_LR = 0.001
ADAM_B1 = 0.9
ADAM_B2 = 0.999
ADAM_EPS = 1e-08
ADAM_WD = 0.01
ADAM_STEP = 10
PER_EXAMPLE_BATCH_AXIS = {'x': 0, 'mem': 0, 'loss_target': 0}
SHARED_INPUTS = []
_WEIGHT_DTYPES = {'mix_pre_g': _jnp.float32, 'w_mix_in': _jnp.float32, 'conv_a_w': _jnp.float32, 'conv_b_w': _jnp.float32, 'conv_b_b': _jnp.float32, 'ln_b_g': _jnp.float32, 'ln_b_b': _jnp.float32, 'w_mix_out': _jnp.float32, 'mix_post_g': _jnp.float32, 'xa_pre_g': _jnp.float32, 'mem_norm_g': _jnp.float32, 'w_q': _jnp.float32, 'w_k': _jnp.float32, 'w_v': _jnp.float32, 'w_o': _jnp.float32, 'xa_post_g': _jnp.float32, 'ffn_pre_g': _jnp.float32, 'w_gate': _jnp.float32, 'w_up': _jnp.float32, 'w_down': _jnp.float32, 'ffn_post_g': _jnp.float32}
MOMENT_SCALE = {'mix_pre_g': 1.953352e+00, 'w_mix_in': 1.143556e+00, 'conv_a_w': 1.590669e+00, 'conv_b_w': 1.385385e+00, 'conv_b_b': 2.927625e+01, 'ln_b_g': 1.136529e+01, 'ln_b_b': 1.824680e+01, 'w_mix_out': 4.039391e+00, 'mix_post_g': 1.284164e+02, 'xa_pre_g': 1.831752e+00, 'mem_norm_g': 8.399044e+00, 'w_q': 1.732554e+00, 'w_k': 1.748525e+00, 'w_v': 6.917167e+00, 'w_o': 7.090727e+00, 'xa_post_g': 1.298235e+02, 'ffn_pre_g': 4.725067e+00, 'w_gate': 1.399661e+00, 'w_up': 2.559402e+00, 'w_down': 4.324510e+00, 'ffn_post_g': 1.271392e+02}


def _to_microbatches(a, axis):
    t = _jnp.moveaxis(a, axis, 0)
    t = t.reshape((N_MICROBATCH, t.shape[0] // N_MICROBATCH) + t.shape[1:])
    return _jnp.moveaxis(t, 1, axis + 1)


def setup_inputs(seed: int = 0) -> dict:
    inp = _fwd_setup_inputs(seed)
    key = _jax.random.fold_in(_jax.random.key(seed), 7919)
    shape, _ = _output_shape()
    out = dict(inp)
    out["loss_target"] = _jax.random.normal(_jax.random.fold_in(key, 0), shape, _jnp.float32)
    for i, name in enumerate(TWIN_WEIGHTS):
        w = inp[name].astype(_jnp.float32)
        if MOMENT_SCALE is None:
            s = _jnp.sqrt(_jnp.mean(_jnp.square(w)) + 1e-30)
        else:
            s = MOMENT_SCALE[name]
        km, kv = _jax.random.split(_jax.random.fold_in(key, i + 1))
        out[name] = w
        out["m_" + name] = s * _jax.random.normal(km, w.shape, _jnp.float32)
        out["v_" + name] = (s * s) * _jax.random.uniform(kv, w.shape, _jnp.float32, 0.5, 1.5)
    if N_MICROBATCH > 1:
        for name, axis in PER_EXAMPLE_BATCH_AXIS.items():
            out[name] = _to_microbatches(out[name], axis)
    return {'x': out['x'], 'mem': out['mem'], 'mix_pre_g': out['mix_pre_g'], 'w_mix_in': out['w_mix_in'], 'conv_a_w': out['conv_a_w'], 'conv_b_w': out['conv_b_w'], 'conv_b_b': out['conv_b_b'], 'ln_b_g': out['ln_b_g'], 'ln_b_b': out['ln_b_b'], 'w_mix_out': out['w_mix_out'], 'mix_post_g': out['mix_post_g'], 'xa_pre_g': out['xa_pre_g'], 'mem_norm_g': out['mem_norm_g'], 'w_q': out['w_q'], 'w_k': out['w_k'], 'w_v': out['w_v'], 'w_o': out['w_o'], 'xa_post_g': out['xa_post_g'], 'ffn_pre_g': out['ffn_pre_g'], 'w_gate': out['w_gate'], 'w_up': out['w_up'], 'w_down': out['w_down'], 'ffn_post_g': out['ffn_post_g'], 'loss_target': out['loss_target'], 'm_mix_pre_g': out['m_mix_pre_g'], 'm_w_mix_in': out['m_w_mix_in'], 'm_conv_a_w': out['m_conv_a_w'], 'm_conv_b_w': out['m_conv_b_w'], 'm_conv_b_b': out['m_conv_b_b'], 'm_ln_b_g': out['m_ln_b_g'], 'm_ln_b_b': out['m_ln_b_b'], 'm_w_mix_out': out['m_w_mix_out'], 'm_mix_post_g': out['m_mix_post_g'], 'm_xa_pre_g': out['m_xa_pre_g'], 'm_mem_norm_g': out['m_mem_norm_g'], 'm_w_q': out['m_w_q'], 'm_w_k': out['m_w_k'], 'm_w_v': out['m_w_v'], 'm_w_o': out['m_w_o'], 'm_xa_post_g': out['m_xa_post_g'], 'm_ffn_pre_g': out['m_ffn_pre_g'], 'm_w_gate': out['m_w_gate'], 'm_w_up': out['m_w_up'], 'm_w_down': out['m_w_down'], 'm_ffn_post_g': out['m_ffn_post_g'], 'v_mix_pre_g': out['v_mix_pre_g'], 'v_w_mix_in': out['v_w_mix_in'], 'v_conv_a_w': out['v_conv_a_w'], 'v_conv_b_w': out['v_conv_b_w'], 'v_conv_b_b': out['v_conv_b_b'], 'v_ln_b_g': out['v_ln_b_g'], 'v_ln_b_b': out['v_ln_b_b'], 'v_w_mix_out': out['v_w_mix_out'], 'v_mix_post_g': out['v_mix_post_g'], 'v_xa_pre_g': out['v_xa_pre_g'], 'v_mem_norm_g': out['v_mem_norm_g'], 'v_w_q': out['v_w_q'], 'v_w_k': out['v_w_k'], 'v_w_v': out['v_w_v'], 'v_w_o': out['v_w_o'], 'v_xa_post_g': out['v_xa_post_g'], 'v_ffn_pre_g': out['v_ffn_pre_g'], 'v_w_gate': out['v_w_gate'], 'v_w_up': out['v_w_up'], 'v_w_down': out['v_w_down'], 'v_ffn_post_g': out['v_ffn_post_g']}


def _loss(weights, diff, rest, loss_target):
    with _jax.named_scope("forward"):
        args = {**rest, TWIN_DIFF_INPUT: diff, **{k: w.astype(_WEIGHT_DTYPES[k]) for k, w in weights.items()}}
        y = _forward(args)
    with _jax.named_scope("loss_head"):
        err = _jnp.square(y.astype(_jnp.float32) - loss_target)
        return 0.5 * _jnp.sum(_jnp.mean(err, axis=-1)) if err.ndim else 0.5 * err


def _adamw(w, g, m, v):
    m = ADAM_B1 * m + (1.0 - ADAM_B1) * g
    v = ADAM_B2 * v + (1.0 - ADAM_B2) * _jnp.square(g)
    m_hat = m / (1.0 - ADAM_B1 ** ADAM_STEP)
    v_hat = v / (1.0 - ADAM_B2 ** ADAM_STEP)
    delta = -ADAM_LR * (m_hat / (_jnp.sqrt(v_hat) + ADAM_EPS) + ADAM_WD * w)
    return delta, m, v


def reference(x, mem, mix_pre_g, w_mix_in, conv_a_w, conv_b_w, conv_b_b, ln_b_g, ln_b_b, w_mix_out, mix_post_g, xa_pre_g, mem_norm_g, w_q, w_k, w_v, w_o, xa_post_g, ffn_pre_g, w_gate, w_up, w_down, ffn_post_g, loss_target, m_mix_pre_g, m_w_mix_in, m_conv_a_w, m_conv_b_w, m_conv_b_b, m_ln_b_g, m_ln_b_b, m_w_mix_out, m_mix_post_g, m_xa_pre_g, m_mem_norm_g, m_w_q, m_w_k, m_w_v, m_w_o, m_xa_post_g, m_ffn_pre_g, m_w_gate, m_w_up, m_w_down, m_ffn_post_g, v_mix_pre_g, v_w_mix_in, v_conv_a_w, v_conv_b_w, v_conv_b_b, v_ln_b_g, v_ln_b_b, v_w_mix_out, v_mix_post_g, v_xa_pre_g, v_mem_norm_g, v_w_q, v_w_k, v_w_v, v_w_o, v_xa_post_g, v_ffn_pre_g, v_w_gate, v_w_up, v_w_down, v_ffn_post_g):
    given = dict(x=x, mem=mem, mix_pre_g=mix_pre_g, w_mix_in=w_mix_in, conv_a_w=conv_a_w, conv_b_w=conv_b_w, conv_b_b=conv_b_b, ln_b_g=ln_b_g, ln_b_b=ln_b_b, w_mix_out=w_mix_out, mix_post_g=mix_post_g, xa_pre_g=xa_pre_g, mem_norm_g=mem_norm_g, w_q=w_q, w_k=w_k, w_v=w_v, w_o=w_o, xa_post_g=xa_post_g, ffn_pre_g=ffn_pre_g, w_gate=w_gate, w_up=w_up, w_down=w_down, ffn_post_g=ffn_post_g, loss_target=loss_target, m_mix_pre_g=m_mix_pre_g, m_w_mix_in=m_w_mix_in, m_conv_a_w=m_conv_a_w, m_conv_b_w=m_conv_b_w, m_conv_b_b=m_conv_b_b, m_ln_b_g=m_ln_b_g, m_ln_b_b=m_ln_b_b, m_w_mix_out=m_w_mix_out, m_mix_post_g=m_mix_post_g, m_xa_pre_g=m_xa_pre_g, m_mem_norm_g=m_mem_norm_g, m_w_q=m_w_q, m_w_k=m_w_k, m_w_v=m_w_v, m_w_o=m_w_o, m_xa_post_g=m_xa_post_g, m_ffn_pre_g=m_ffn_pre_g, m_w_gate=m_w_gate, m_w_up=m_w_up, m_w_down=m_w_down, m_ffn_post_g=m_ffn_post_g, v_mix_pre_g=v_mix_pre_g, v_w_mix_in=v_w_mix_in, v_conv_a_w=v_conv_a_w, v_conv_b_w=v_conv_b_w, v_conv_b_b=v_conv_b_b, v_ln_b_g=v_ln_b_g, v_ln_b_b=v_ln_b_b, v_w_mix_out=v_w_mix_out, v_mix_post_g=v_mix_post_g, v_xa_pre_g=v_xa_pre_g, v_mem_norm_g=v_mem_norm_g, v_w_q=v_w_q, v_w_k=v_w_k, v_w_v=v_w_v, v_w_o=v_w_o, v_xa_post_g=v_xa_post_g, v_ffn_pre_g=v_ffn_pre_g, v_w_gate=v_w_gate, v_w_up=v_w_up, v_w_down=v_w_down, v_ffn_post_g=v_ffn_post_g)
    weights = {n: given[n] for n in TWIN_WEIGHTS}
    shared = {n: given[n] for n in SHARED_INPUTS}
    per_example = {n: given[n] for n in ['x', 'mem']}
    grad_fn = _jax.value_and_grad(_loss, argnums=(0, 1))

    def one_microbatch(ex, loss_target):
        ex = dict(ex)
        diff = ex.pop(TWIN_DIFF_INPUT)
        return grad_fn(weights, diff, {**shared, **ex}, loss_target)

    if N_MICROBATCH == 1:
        loss, (grad_w, grad_x) = one_microbatch(per_example, given["loss_target"])
    else:
        def body(carry, xs):
            loss_sum, grad_sum = carry
            l_k, (gw_k, gx_k) = one_microbatch(xs[0], xs[1])
            with _jax.named_scope("update"):
                return (loss_sum + l_k, _jax.tree.map(_jnp.add, grad_sum, gw_k)), gx_k

        init = (_jnp.zeros((), _jnp.float32), _jax.tree.map(_jnp.zeros_like, weights))
        (loss, grad_w), grad_x = _jax.lax.scan(body, init, (per_example, given["loss_target"]))
    with _jax.named_scope("update"):
        delta_w, new_m, new_v = {}, {}, {}
        for n in TWIN_WEIGHTS:
            delta_w[n], new_m[n], new_v[n] = _adamw(weights[n], grad_w[n], given["m_" + n], given["v_" + n])
    return (loss, grad_x, *[grad_w[n] for n in TWIN_WEIGHTS], *[delta_w[n] for n in TWIN_WEIGHTS],
            *[new_m[n] for n in TWIN_WEIGHTS], *[new_v[n] for n in TWIN_WEIGHTS])
```

```python
import functools

import jax
import jax.numpy as jnp
from jax import lax
from jax.experimental import pallas as pl
from jax.experimental.pallas import tpu as pltpu

F32 = jnp.float32
BF16 = jnp.bfloat16
MESH = pl.DeviceIdType.MESH

RMS_EPS = 1e-6
LN_EPS = 1e-5
D_MODEL = 1024
D_A = 512
D_B = 512
D_IN_ALL = 3 * D_A + 2 * D_B
CONV_A_W = 3
CONV_B_W = 31
HALO = 32
XA_HEADS = 4
HEAD_DIM = 256
D_FF = 2816
N_CHIPS = 4
FF_BLK = D_FF // N_CHIPS
IN_BLK = D_IN_ALL // N_CHIPS
ROW_BLK = D_MODEL // N_CHIPS

ADAM_LR = 0.001
ADAM_B1 = 0.9
ADAM_B2 = 0.999
ADAM_EPS = 1e-08
ADAM_WD = 0.01
ADAM_STEP = 10

TILE_FWD = 512
TILE_BWD = 256
CONV_ROWS = 64
V7X_VMEM_LIMIT = 56 * 1024 * 1024

PR_OUT, PR_Q, PR_K, PR_V, PR_O, PR_DOWN = 0, 256, 512, 768, 1024, 1280
PR_ROWS = PR_DOWN + FF_BLK
G2_ROWS = 2 * D_MODEL
G1_ROWS = D_MODEL

ANY = pl.BlockSpec(memory_space=pl.ANY)


def _mm(a, b):
    return lax.dot_general(a, b, (((1,), (0,)), ((), ())), preferred_element_type=F32)


def _mm_nt(a, b):
    return lax.dot_general(a, b, (((1,), (1,)), ((), ())), preferred_element_type=F32)


def _mm_tn(a, b):
    return lax.dot_general(a, b, (((0,), (0,)), ((), ())), preferred_element_type=F32)


def _sigmoid(x):
    return 1.0 / (1.0 + jnp.exp(-x))


def _rms(x):
    r = lax.rsqrt(jnp.mean(x * x, axis=-1, keepdims=True) + RMS_EPS)
    return x * r, r


def _rms_bwd(dy, xn, r, g):
    gdy = dy * g
    dx = r * (gdy - xn * jnp.mean(gdy * xn, axis=-1, keepdims=True))
    return dx, jnp.sum(dy * xn, axis=0, keepdims=True)


def _full(shape):
    return pl.BlockSpec(shape, lambda *_: (0,) * len(shape))


def _params(sem=None):
    return pltpu.CompilerParams(dimension_semantics=sem, vmem_limit_bytes=V7X_VMEM_LIMIT)


def _load_rows(dst, src_hbm, row0, rows):
    for k in range(N_CHIPS):
        pltpu.sync_copy(src_hbm.at[k, pl.ds(row0, rows), :], dst.at[pl.ds(rows * k, rows), :])


def _load_cols(dst, src_hbm, row0, rows, cols):
    for k in range(N_CHIPS):
        pltpu.sync_copy(src_hbm.at[k, pl.ds(row0, rows), :], dst.at[:, pl.ds(cols * k, cols)])


def _fill_phases(src, sh, nrows):
    for r in range(1, 8):
        sh[r, pl.ds(0, nrows), :] = src[pl.ds(r, nrows), pl.ds(D_A, D_B)]


def _phase_rows(src, sh, off, start, size):
    r = off % 8
    if r == 0:
        return src[pl.ds(off + start, size), pl.ds(D_A, D_B)]
    return sh[r, pl.ds(off - r + start, size), :]


def _mix_fwd(x, gpre, gpost, wa, wb, bb, lng, lnb, g1g, prg):
    S, D = x.shape
    T = min(TILE_FWD, S)
    n = S // T

    def body(x_ref, gpre_ref, gpost_ref, wa_ref, wb_ref, bb_ref, lng_ref, lnb_ref, g1g_hbm, prg_hbm,
             x1_ref, u_ref, o1_ref, z1_ref, win_v, wout_v, ext, sh, z1buf):
        i = pl.program_id(0)

        @pl.when(i == 0)
        def _():
            _load_cols(win_v, g1g_hbm, 0, D, IN_BLK)
            _load_rows(wout_v, prg_hbm, PR_OUT, ROW_BLK)
            ext[pl.ds(0, HALO), :] = jnp.zeros((HALO, D_A + D_B), F32)

        xv = x_ref[...]
        xn, _ = _rms(xv)
        h = (xn * gpre_ref[...]).astype(BF16)
        u = _mm(h, win_v[...])
        u_ref[...] = u.astype(BF16)
        b_a = u[:, 0:D_A]
        cv = u[:, D_A:2 * D_A] * u[:, 2 * D_A:3 * D_A]
        z0 = u[:, 3 * D_A:3 * D_A + D_B] * _sigmoid(u[:, 3 * D_A + D_B:])
        ext[pl.ds(HALO, T), pl.ds(0, D_A)] = cv
        ext[pl.ds(HALO, T), pl.ds(D_A, D_B)] = z0

        conv_a = ext[pl.ds(HALO - 2, T), pl.ds(0, D_A)] * wa_ref[0:1, :]
        for k in range(1, CONV_A_W):
            conv_a = conv_a + ext[pl.ds(HALO - 2 + k, T), pl.ds(0, D_A)] * wa_ref[k:k + 1, :]
        y_a = b_a * conv_a

        _fill_phases(ext, sh, T + HALO - 8)
        base = HALO - (CONV_B_W - 1)

        def chunk(ci, carry):
            start = pl.multiple_of(ci * CONV_ROWS, 8)
            acc = jnp.broadcast_to(bb_ref[...], (CONV_ROWS, D_B))
            for k in range(CONV_B_W):
                acc = acc + _phase_rows(ext, sh, base + k, start, CONV_ROWS) * wb_ref[k:k + 1, :]
            z1buf[pl.ds(start, CONV_ROWS), :] = acc
            return carry

        lax.fori_loop(0, T // CONV_ROWS, chunk, 0)
        z1 = z1buf[...]
        z1_ref[...] = z1.astype(BF16)
        mu = jnp.mean(z1, axis=-1, keepdims=True)
        zc = z1 - mu
        rstd = lax.rsqrt(jnp.mean(zc * zc, axis=-1, keepdims=True) + LN_EPS)
        l = zc * rstd * lng_ref[...] + lnb_ref[...]
        y_b = l * _sigmoid(l)
        y = jnp.concatenate([y_a, y_b], axis=-1).astype(BF16)
        o1 = _mm(y, wout_v[...])
        o1_ref[...] = o1.astype(BF16)
        o1n, _ = _rms(o1)
        x1_ref[...] = xv + o1n * gpost_ref[...]
        ext[pl.ds(0, HALO), :] = ext[pl.ds(T, HALO), :]

    tok = lambda w: pl.BlockSpec((T, w), lambda i: (i, 0))
    return pl.pallas_call(
        body,
        name="mix_fwd",
        grid=(n,),
        in_specs=[tok(D), _full((1, D)), _full((1, D)), _full((CONV_A_W, D_A)), _full((CONV_B_W, D_B)),
                  _full((1, D_B)), _full((1, D_B)), _full((1, D_B)), ANY, ANY],
        out_specs=[tok(D), tok(D_IN_ALL), tok(D), tok(D_B)],
        out_shape=[jax.ShapeDtypeStruct((S, D), F32), jax.ShapeDtypeStruct((S, D_IN_ALL), BF16),
                   jax.ShapeDtypeStruct((S, D), BF16), jax.ShapeDtypeStruct((S, D_B), BF16)],
        scratch_shapes=[pltpu.VMEM((D, D_IN_ALL), BF16), pltpu.VMEM((D_A + D_B, D), BF16),
                        pltpu.VMEM((HALO + T, D_A + D_B), F32), pltpu.VMEM((8, HALO + T, D_B), F32),
                        pltpu.VMEM((T, D_B), F32)],
        compiler_params=_params(("arbitrary",)),
    )(x, gpre, gpost, wa, wb, bb, lng, lnb, g1g, prg)


def _mix_bwd(dx1, x, o1, u, z1s, gpre, gpost, wa, wb, lng, lnb, g1g, prg, gpr_in):
    S, D = x.shape
    T = min(TILE_BWD, S)
    n = S // T
    hb = T // HALO

    def body(dx1_ref, x_ref, o1_ref, u_ref, uh_ref, z1_ref, gpre_ref, gpost_ref, wa_ref, wb_ref, lng_ref, lnb_ref,
             g1g_hbm, prg_hbm, gpr_hbm,
             dx_ref, dgpre_ref, dgpost_ref, dwa_ref, dwb_ref, dbb_ref, dlng_ref, dlnb_ref, gg1_hbm, gpr_out,
             win_v, wout_v, dwin_acc, dwout_acc, ext, ext2, shf, shb, dz0buf, dwb_acc):
        del gpr_hbm
        i = pl.program_id(0)

        @pl.when(i == 0)
        def _():
            _load_cols(win_v, g1g_hbm, 0, D, IN_BLK)
            _load_rows(wout_v, prg_hbm, PR_OUT, ROW_BLK)
            dwin_acc[...] = jnp.zeros_like(dwin_acc)
            dwout_acc[...] = jnp.zeros_like(dwout_acc)
            dwb_acc[...] = jnp.zeros_like(dwb_acc)
            ext2[pl.ds(T, HALO), :] = jnp.zeros((HALO, D_A + D_B), F32)
            for ref in (dgpre_ref, dgpost_ref, dwa_ref, dbb_ref, dlng_ref, dlnb_ref):
                ref[...] = jnp.zeros_like(ref)

        o1n, r1 = _rms(o1_ref[...].astype(F32))
        dx1v = dx1_ref[...]
        d_o1, dgp = _rms_bwd(dx1v, o1n, r1, gpost_ref[...])
        dgpost_ref[...] += dgp
        d_o1b = d_o1.astype(BF16)
        dy = _mm_nt(d_o1b, wout_v[...])

        first = (i == n - 1).astype(F32)
        uh = uh_ref[...].astype(F32) * (1.0 - first)
        ext[pl.ds(0, HALO), pl.ds(0, D_A)] = uh[:, D_A:2 * D_A] * uh[:, 2 * D_A:3 * D_A]
        ext[pl.ds(0, HALO), pl.ds(D_A, D_B)] = uh[:, 3 * D_A:3 * D_A + D_B] * _sigmoid(uh[:, 3 * D_A + D_B:])
        uf = u_ref[...].astype(F32)
        b_a = uf[:, 0:D_A]
        c_a = uf[:, D_A:2 * D_A]
        v_a = uf[:, 2 * D_A:3 * D_A]
        gv = uf[:, 3 * D_A:3 * D_A + D_B]
        sg = _sigmoid(uf[:, 3 * D_A + D_B:])
        ext[pl.ds(HALO, T), pl.ds(0, D_A)] = c_a * v_a
        ext[pl.ds(HALO, T), pl.ds(D_A, D_B)] = gv * sg
        conv_a = ext[pl.ds(HALO - 2, T), pl.ds(0, D_A)] * wa_ref[0:1, :]
        for k in range(1, CONV_A_W):
            conv_a = conv_a + ext[pl.ds(HALO - 2 + k, T), pl.ds(0, D_A)] * wa_ref[k:k + 1, :]
        z1 = z1_ref[...].astype(F32)
        mu = jnp.mean(z1, axis=-1, keepdims=True)
        zc = z1 - mu
        rstd = lax.rsqrt(jnp.mean(zc * zc, axis=-1, keepdims=True) + LN_EPS)
        zn = zc * rstd
        l = zn * lng_ref[...] + lnb_ref[...]
        sl = _sigmoid(l)
        y = jnp.concatenate([b_a * conv_a, l * sl], axis=-1).astype(BF16)
        dwout_acc[...] += _mm_tn(y, d_o1b)

        dy_a = dy[:, 0:D_A]
        dl = dy[:, D_A:] * (sl * (1.0 + l * (1.0 - sl)))
        dlng_ref[...] += jnp.sum(dl * zn, axis=0, keepdims=True)
        dlnb_ref[...] += jnp.sum(dl, axis=0, keepdims=True)
        dzn = dl * lng_ref[...]
        dz1 = rstd * (dzn - jnp.mean(dzn, axis=-1, keepdims=True) - zn * jnp.mean(dzn * zn, axis=-1, keepdims=True))
        dbb_ref[...] += jnp.sum(dz1, axis=0, keepdims=True)
        d_conv = dy_a * b_a
        ext2[pl.ds(0, T), pl.ds(0, D_A)] = d_conv
        ext2[pl.ds(0, T), pl.ds(D_A, D_B)] = dz1

        d_cv = ext2[pl.ds(CONV_A_W - 1, T), pl.ds(0, D_A)] * wa_ref[0:1, :]
        for k in range(1, CONV_A_W):
            d_cv = d_cv + ext2[pl.ds(CONV_A_W - 1 - k, T), pl.ds(0, D_A)] * wa_ref[k:k + 1, :]
        for k in range(CONV_A_W):
            dwa_ref[k:k + 1, :] += jnp.sum(d_conv * ext[pl.ds(HALO - 2 + k, T), pl.ds(0, D_A)], axis=0, keepdims=True)

        _fill_phases(ext, shf, T + HALO - 8)
        _fill_phases(ext2, shb, T + HALO - 8)
        base = HALO - (CONV_B_W - 1)

        def chunk(ci, carry):
            start = pl.multiple_of(ci * CONV_ROWS, 8)
            dzc = ext2[pl.ds(start, CONV_ROWS), pl.ds(D_A, D_B)]
            acc = jnp.zeros((CONV_ROWS, D_B), F32)
            for k in range(CONV_B_W):
                wk = wb_ref[k:k + 1, :]
                acc = acc + _phase_rows(ext2, shb, CONV_B_W - 1 - k, start, CONV_ROWS) * wk
                prod = dzc * _phase_rows(ext, shf, base + k, start, CONV_ROWS)
                part = prod[0:8, :]
                for m in range(1, CONV_ROWS // 8):
                    part = part + prod[8 * m:8 * m + 8, :]
                dwb_acc[k] += part
            dz0buf[pl.ds(start, CONV_ROWS), :] = acc
            return carry

        lax.fori_loop(0, T // CONV_ROWS, chunk, 0)
        dz0 = dz0buf[...]
        du = jnp.concatenate([dy_a * conv_a, d_cv * v_a, d_cv * c_a, dz0 * sg, dz0 * gv * sg * (1.0 - sg)],
                             axis=-1).astype(BF16)
        dh = _mm_nt(du, win_v[...])
        xv = x_ref[...]
        xn, r0 = _rms(xv)
        dwin_acc[...] += _mm_tn((xn * gpre_ref[...]).astype(BF16), du)
        dxp, dg0 = _rms_bwd(dh, xn, r0, gpre_ref[...])
        dgpre_ref[...] += dg0
        dx_ref[...] = dx1v + dxp
        ext2[pl.ds(T, HALO), :] = ext2[pl.ds(0, HALO), :]

        @pl.when(i == n - 1)
        def _():
            for k in range(CONV_B_W):
                dwb_ref[k:k + 1, :] = jnp.sum(dwb_acc[k], axis=0, keepdims=True)
            for k in range(N_CHIPS):
                pltpu.sync_copy(dwin_acc.at[:, pl.ds(IN_BLK * k, IN_BLK)], gg1_hbm.at[k])
                pltpu.sync_copy(dwout_acc.at[pl.ds(ROW_BLK * k, ROW_BLK), :], gpr_out.at[k, pl.ds(PR_OUT, ROW_BLK), :])

    rev = lambda w: pl.BlockSpec((T, w), lambda i: (n - 1 - i, 0))
    halo = pl.BlockSpec((HALO, D_IN_ALL), lambda i: (jnp.maximum((n - 1 - i) * hb - 1, 0), 0))
    outs = pl.pallas_call(
        body,
        name="mix_bwd",
        grid=(n,),
        in_specs=[rev(D), rev(D), rev(D), rev(D_IN_ALL), halo, rev(D_B), _full((1, D)), _full((1, D)),
                  _full((CONV_A_W, D_A)), _full((CONV_B_W, D_B)), _full((1, D_B)), _full((1, D_B)), ANY, ANY, ANY],
        out_specs=[rev(D), _full((1, D)), _full((1, D)), _full((CONV_A_W, D_A)), _full((CONV_B_W, D_B)),
                   _full((1, D_B)), _full((1, D_B)), _full((1, D_B)), ANY, ANY],
        out_shape=[jax.ShapeDtypeStruct((S, D), F32), jax.ShapeDtypeStruct((1, D), F32),
                   jax.ShapeDtypeStruct((1, D), F32), jax.ShapeDtypeStruct((CONV_A_W, D_A), F32),
                   jax.ShapeDtypeStruct((CONV_B_W, D_B), F32), jax.ShapeDtypeStruct((1, D_B), F32),
                   jax.ShapeDtypeStruct((1, D_B), F32), jax.ShapeDtypeStruct((1, D_B), F32),
                   jax.ShapeDtypeStruct((N_CHIPS, G1_ROWS, IN_BLK), F32),
                   jax.ShapeDtypeStruct(gpr_in.shape, F32)],
        input_output_aliases={14: 9},
        scratch_shapes=[pltpu.VMEM((D, D_IN_ALL), BF16), pltpu.VMEM((D_A + D_B, D), BF16),
                        pltpu.VMEM((D, D_IN_ALL), F32), pltpu.VMEM((D_A + D_B, D), F32),
                        pltpu.VMEM((HALO + T, D_A + D_B), F32), pltpu.VMEM((HALO + T, D_A + D_B), F32),
                        pltpu.VMEM((8, HALO + T, D_B), F32), pltpu.VMEM((8, HALO + T, D_B), F32),
                        pltpu.VMEM((T, D_B), F32), pltpu.VMEM((CONV_B_W, 8, D_B), F32)],
        compiler_params=_params(("arbitrary",)),
    )(dx1, x, o1, u, u, z1s, gpre, gpost, wa, wb, lng, lnb, g1g, prg, gpr_in)
    return outs


def _mem_kv(mem, gmem, prg):
    M, D = mem.shape

    def body(mem_ref, g_ref, prg_hbm, memn_ref, k_ref, v_ref, wk_v, wv_v):
        _load_rows(wk_v, prg_hbm, PR_K, ROW_BLK)
        _load_rows(wv_v, prg_hbm, PR_V, ROW_BLK)
        mn, _ = _rms(mem_ref[...])
        mb = (mn * g_ref[...]).astype(BF16)
        memn_ref[...] = mb
        k_ref[...] = _mm(mb, wk_v[...]).astype(BF16)
        v_ref[...] = _mm(mb, wv_v[...]).astype(BF16)

    return pl.pallas_call(
        body,
        name="mem_kv",
        grid=(1,),
        in_specs=[_full((M, D)), _full((1, D)), ANY],
        out_specs=[_full((M, D))] * 3,
        out_shape=[jax.ShapeDtypeStruct((M, D), BF16)] * 3,
        scratch_shapes=[pltpu.VMEM((D, D), BF16), pltpu.VMEM((D, D), BF16)],
        compiler_params=_params(("arbitrary",)),
    )(mem, gmem, prg)


def _attend(qb, kb, vb):
    scale = HEAD_DIM ** -0.5
    ps, os_ = [], []
    for hd in range(XA_HEADS):
        cols = slice(HEAD_DIM * hd, HEAD_DIM * (hd + 1))
        s = _mm_nt(qb[:, cols], kb[:, cols]) * scale
        e = jnp.exp(s - jnp.max(s, axis=-1, keepdims=True))
        p = e / jnp.sum(e, axis=-1, keepdims=True)
        ps.append(p)
        os_.append(_mm(p.astype(BF16), vb[:, cols]))
    return ps, jnp.concatenate(os_, axis=-1).astype(BF16)


def _xattn_fwd(x1, gpre, gpost, kb, vb, prg):
    S, D = x1.shape
    M = kb.shape[0]
    T = min(TILE_FWD, S)
    n = S // T

    def body(x1_ref, gpre_ref, gpost_ref, k_ref, v_ref, prg_hbm, x2_ref, q_ref, o2_ref, wq_v, wo_v):
        @pl.when(pl.program_id(0) == 0)
        def _():
            _load_rows(wq_v, prg_hbm, PR_Q, ROW_BLK)
            _load_rows(wo_v, prg_hbm, PR_O, ROW_BLK)

        xv = x1_ref[...]
        xn, _ = _rms(xv)
        qb = _mm((xn * gpre_ref[...]).astype(BF16), wq_v[...]).astype(BF16)
        q_ref[...] = qb
        _, ob = _attend(qb, k_ref[...], v_ref[...])
        o2 = _mm(ob, wo_v[...])
        o2_ref[...] = o2.astype(BF16)
        o2n, _ = _rms(o2)
        x2_ref[...] = xv + o2n * gpost_ref[...]

    tok = lambda w: pl.BlockSpec((T, w), lambda i: (i, 0))
    return pl.pallas_call(
        body,
        name="xattn_fwd",
        grid=(n,),
        in_specs=[tok(D), _full((1, D)), _full((1, D)), _full((M, D)), _full((M, D)), ANY],
        out_specs=[tok(D), tok(D), tok(D)],
        out_shape=[jax.ShapeDtypeStruct((S, D), F32), jax.ShapeDtypeStruct((S, D), BF16),
                   jax.ShapeDtypeStruct((S, D), BF16)],
        scratch_shapes=[pltpu.VMEM((D, D), BF16), pltpu.VMEM((D, D), BF16)],
        compiler_params=_params(("arbitrary",)),
    )(x1, gpre, gpost, kb, vb, prg)


def _xattn_bwd(dx3, dh3p, x2, x1, o2, q, kb, vb, gffn, gpost, gpre, prg, gpr_in):
    S, D = x1.shape
    M = kb.shape[0]
    T = min(TILE_BWD, S)
    n = S // T
    scale = HEAD_DIM ** -0.5

    def body(dx3_ref, dh3a_ref, dh3b_ref, x2_ref, x1_ref, o2_ref, q_ref, k_ref, v_ref, gffn_ref, gpost_ref, gpre_ref,
             prg_hbm, gpr_hbm,
             dx1_ref, dgffn_ref, dgpost_ref, dgpre_ref, dk_ref, dv_ref, gpr_out,
             wq_v, wo_v, dwq_acc, dwo_acc):
        del gpr_hbm
        i = pl.program_id(0)

        @pl.when(i == 0)
        def _():
            _load_rows(wq_v, prg_hbm, PR_Q, ROW_BLK)
            _load_rows(wo_v, prg_hbm, PR_O, ROW_BLK)
            dwq_acc[...] = jnp.zeros_like(dwq_acc)
            dwo_acc[...] = jnp.zeros_like(dwo_acc)
            for ref in (dgffn_ref, dgpost_ref, dgpre_ref, dk_ref, dv_ref):
                ref[...] = jnp.zeros_like(ref)

        x2n, r2 = _rms(x2_ref[...])
        dxp, dg = _rms_bwd(dh3a_ref[...] + dh3b_ref[...], x2n, r2, gffn_ref[...])
        dgffn_ref[...] += dg
        dx2 = dx3_ref[...] + dxp
        o2n, ro = _rms(o2_ref[...].astype(F32))
        d_o2, dg = _rms_bwd(dx2, o2n, ro, gpost_ref[...])
        dgpost_ref[...] += dg
        d_o2b = d_o2.astype(BF16)
        d_o = _mm_nt(d_o2b, wo_v[...]).astype(BF16)
        qb = q_ref[...]
        kv = k_ref[...]
        vv = v_ref[...]
        ps, ob = _attend(qb, kv, vv)
        dwo_acc[...] += _mm_tn(ob, d_o2b)
        dqs = []
        for hd in range(XA_HEADS):
            cols = slice(HEAD_DIM * hd, HEAD_DIM * (hd + 1))
            p = ps[hd]
            dp = _mm_nt(d_o[:, cols], vv[:, cols])
            dv_ref[:, cols] += _mm_tn(p.astype(BF16), d_o[:, cols])
            ds = (p * (dp - jnp.sum(p * dp, axis=-1, keepdims=True)) * scale).astype(BF16)
            dqs.append(_mm(ds, kv[:, cols]))
            dk_ref[:, cols] += _mm_tn(ds, qb[:, cols])
        dq = jnp.concatenate(dqs, axis=-1).astype(BF16)
        dh2 = _mm_nt(dq, wq_v[...])
        x1n, r1 = _rms(x1_ref[...])
        dwq_acc[...] += _mm_tn((x1n * gpre_ref[...]).astype(BF16), dq)
        dxp, dg = _rms_bwd(dh2, x1n, r1, gpre_ref[...])
        dgpre_ref[...] += dg
        dx1_ref[...] = dx2 + dxp

        @pl.when(i == n - 1)
        def _():
            for k in range(N_CHIPS):
                rows = pl.ds(ROW_BLK * k, ROW_BLK)
                pltpu.sync_copy(dwq_acc.at[rows, :], gpr_out.at[k, pl.ds(PR_Q, ROW_BLK), :])
                pltpu.sync_copy(dwo_acc.at[rows, :], gpr_out.at[k, pl.ds(PR_O, ROW_BLK), :])

    tok = lambda w: pl.BlockSpec((T, w), lambda i: (i, 0))
    part = lambda j: pl.BlockSpec((None, T, D), lambda i: (j, i, 0))
    return pl.pallas_call(
        body,
        name="xattn_bwd",
        grid=(n,),
        in_specs=[tok(D), part(0), part(1), tok(D), tok(D), tok(D), tok(D), _full((M, D)), _full((M, D)),
                  _full((1, D)), _full((1, D)), _full((1, D)), ANY, ANY],
        out_specs=[tok(D), _full((1, D)), _full((1, D)), _full((1, D)), _full((M, D)), _full((M, D)), ANY],
        out_shape=[jax.ShapeDtypeStruct((S, D), F32), jax.ShapeDtypeStruct((1, D), F32),
                   jax.ShapeDtypeStruct((1, D), F32), jax.ShapeDtypeStruct((1, D), F32),
                   jax.ShapeDtypeStruct((M, D), F32), jax.ShapeDtypeStruct((M, D), F32),
                   jax.ShapeDtypeStruct(gpr_in.shape, F32)],
        input_output_aliases={13: 6},
        scratch_shapes=[pltpu.VMEM((D, D), BF16), pltpu.VMEM((D, D), BF16),
                        pltpu.VMEM((D, D), F32), pltpu.VMEM((D, D), F32)],
        compiler_params=_params(("arbitrary",)),
    )(dx3, dh3p, dh3p, x2, x1, o2, q, kb, vb, gffn, gpost, gpre, prg, gpr_in)


def _mem_bwd(dk, dv, mem, memn, gmem, prg, gpr_in):
    M, D = mem.shape

    def body(dk_ref, dv_ref, mem_ref, memn_ref, g_ref, prg_hbm, gpr_hbm, dg_ref, gpr_out, wk_v, wv_v, dwk_v, dwv_v):
        del gpr_hbm
        _load_rows(wk_v, prg_hbm, PR_K, ROW_BLK)
        _load_rows(wv_v, prg_hbm, PR_V, ROW_BLK)
        dkb = dk_ref[...].astype(BF16)
        dvb = dv_ref[...].astype(BF16)
        mb = memn_ref[...]
        dwk_v[...] = _mm_tn(mb, dkb)
        dwv_v[...] = _mm_tn(mb, dvb)
        dmn = _mm_nt(dkb, wk_v[...]) + _mm_nt(dvb, wv_v[...])
        mn, _ = _rms(mem_ref[...])
        dg_ref[...] = jnp.sum(dmn * mn, axis=0, keepdims=True)
        for k in range(N_CHIPS):
            rows = pl.ds(ROW_BLK * k, ROW_BLK)
            pltpu.sync_copy(dwk_v.at[rows, :], gpr_out.at[k, pl.ds(PR_K, ROW_BLK), :])
            pltpu.sync_copy(dwv_v.at[rows, :], gpr_out.at[k, pl.ds(PR_V, ROW_BLK), :])

    return pl.pallas_call(
        body,
        name="mem_bwd",
        grid=(1,),
        in_specs=[_full((M, D)), _full((M, D)), _full((M, D)), _full((M, D)), _full((1, D)), ANY, ANY],
        out_specs=[_full((1, D)), ANY],
        out_shape=[jax.ShapeDtypeStruct((1, D), F32), jax.ShapeDtypeStruct(gpr_in.shape, F32)],
        input_output_aliases={6: 1},
        scratch_shapes=[pltpu.VMEM((D, D), BF16), pltpu.VMEM((D, D), BF16),
                        pltpu.VMEM((D, D), F32), pltpu.VMEM((D, D), F32)],
        compiler_params=_params(("arbitrary",)),
    )(dk, dv, mem, memn, gmem, prg, gpr_in)


def _ffn_fwd(x2, target, gpre, gpost, g2g, prg):
    S, D = x2.shape
    T = min(TILE_BWD, S)
    n = S // T

    def body(x2_ref, t_ref, gpre_ref, gpost_ref, g2g_hbm, prg_hbm,
             h3_ref, g_ref, u_ref, do3_ref, dx3_ref, loss_ref, dgpost_ref, wg_v, wu_v, wd_v):
        @pl.when(pl.program_id(0) == 0)
        def _():
            pltpu.sync_copy(g2g_hbm.at[:, pl.ds(0, D), :], wg_v)
            pltpu.sync_copy(g2g_hbm.at[:, pl.ds(D, D), :], wu_v)
            pltpu.sync_copy(prg_hbm.at[:, pl.ds(PR_DOWN, FF_BLK), :], wd_v)
            loss_ref[...] = jnp.zeros_like(loss_ref)
            dgpost_ref[...] = jnp.zeros_like(dgpost_ref)

        xv = x2_ref[...]
        xn, _ = _rms(xv)
        hb = (xn * gpre_ref[...]).astype(BF16)
        h3_ref[...] = hb
        o3 = jnp.zeros((T, D), F32)
        for c in range(N_CHIPS):
            g = _mm(hb, wg_v[c])
            u = _mm(hb, wu_v[c])
            g_ref[c] = g.astype(BF16)
            u_ref[c] = u.astype(BF16)
            o3 = o3 + _mm((g * _sigmoid(g) * u).astype(BF16), wd_v[c])
        o3n, r3 = _rms(o3)
        diff = xv + o3n * gpost_ref[...] - t_ref[...]
        sq = jnp.sum(jnp.sum(diff * diff, axis=-1, keepdims=True), axis=0, keepdims=True)
        loss_ref[...] += sq * (0.5 / D)
        dx3 = diff * (1.0 / D)
        dx3_ref[...] = dx3
        d_o3, dg = _rms_bwd(dx3, o3n, r3, gpost_ref[...])
        dgpost_ref[...] += dg
        do3_ref[...] = d_o3.astype(BF16)

    tok = lambda w: pl.BlockSpec((T, w), lambda i: (i, 0))
    blk = pl.BlockSpec((N_CHIPS, T, FF_BLK), lambda i: (0, i, 0))
    return pl.pallas_call(
        body,
        name="ffn_fwd",
        grid=(n,),
        in_specs=[tok(D), tok(D), _full((1, D)), _full((1, D)), ANY, ANY],
        out_specs=[tok(D), blk, blk, tok(D), tok(D), _full((1, 128)), _full((1, D))],
        out_shape=[jax.ShapeDtypeStruct((S, D), BF16), jax.ShapeDtypeStruct((N_CHIPS, S, FF_BLK), BF16),
                   jax.ShapeDtypeStruct((N_CHIPS, S, FF_BLK), BF16), jax.ShapeDtypeStruct((S, D), BF16),
                   jax.ShapeDtypeStruct((S, D), F32), jax.ShapeDtypeStruct((1, 128), F32),
                   jax.ShapeDtypeStruct((1, D), F32)],
        scratch_shapes=[pltpu.VMEM((N_CHIPS, D, FF_BLK), BF16), pltpu.VMEM((N_CHIPS, D, FF_BLK), BF16),
                        pltpu.VMEM((N_CHIPS, FF_BLK, D), BF16)],
        compiler_params=_params(("arbitrary",)),
    )(x2, target, gpre, gpost, g2g, prg)


def _ffn_bwd(h3, do3, gs, us, g2g, prg):
    S, D = h3.shape
    T = min(TILE_BWD, S)
    n = S // T
    NP = 2

    def body(h3_ref, do3_ref, g_ref, u_ref, g2g_hbm, prg_hbm, dh3_ref, gg2_hbm, gpr_hbm,
             wg_v, wu_v, wd_v, dwg_acc, dwu_acc, dwd_acc):
        jp = pl.program_id(0)
        i = pl.program_id(1)
        blocks = pl.ds(NP * jp, NP)

        @pl.when(i == 0)
        def _():
            pltpu.sync_copy(g2g_hbm.at[blocks, pl.ds(0, D), :], wg_v)
            pltpu.sync_copy(g2g_hbm.at[blocks, pl.ds(D, D), :], wu_v)
            pltpu.sync_copy(prg_hbm.at[blocks, pl.ds(PR_DOWN, FF_BLK), :], wd_v)
            dwg_acc[...] = jnp.zeros_like(dwg_acc)
            dwu_acc[...] = jnp.zeros_like(dwu_acc)
            dwd_acc[...] = jnp.zeros_like(dwd_acc)

        hb = h3_ref[...]
        d_o3 = do3_ref[...]
        dh = jnp.zeros((T, D), F32)
        for c in range(NP):
            g = g_ref[c].astype(F32)
            u = u_ref[c].astype(F32)
            sg = _sigmoid(g)
            sl = g * sg
            da = _mm_nt(d_o3, wd_v[c])
            dwd_acc[c] += _mm_tn((sl * u).astype(BF16), d_o3)
            dub = (da * sl).astype(BF16)
            dgb = (da * u * (sg * (1.0 + g * (1.0 - sg)))).astype(BF16)
            dwg_acc[c] += _mm_tn(hb, dgb)
            dwu_acc[c] += _mm_tn(hb, dub)
            dh = dh + _mm_nt(dgb, wg_v[c]) + _mm_nt(dub, wu_v[c])
        dh3_ref[...] = dh

        @pl.when(i == n - 1)
        def _():
            pltpu.sync_copy(dwg_acc, gg2_hbm.at[blocks, pl.ds(0, D), :])
            pltpu.sync_copy(dwu_acc, gg2_hbm.at[blocks, pl.ds(D, D), :])
            pltpu.sync_copy(dwd_acc, gpr_hbm.at[blocks, pl.ds(PR_DOWN, FF_BLK), :])

    tok = lambda w: pl.BlockSpec((T, w), lambda jp, i: (i, 0))
    blk = pl.BlockSpec((NP, T, FF_BLK), lambda jp, i: (jp, i, 0))
    return pl.pallas_call(
        body,
        name="ffn_bwd",
        grid=(N_CHIPS // NP, n),
        in_specs=[tok(D), tok(D), blk, blk, ANY, ANY],
        out_specs=[pl.BlockSpec((None, T, D), lambda jp, i: (jp, i, 0)), ANY, ANY],
        out_shape=[jax.ShapeDtypeStruct((N_CHIPS // NP, S, D), F32),
                   jax.ShapeDtypeStruct((N_CHIPS, G2_ROWS, FF_BLK), F32),
                   jax.ShapeDtypeStruct((N_CHIPS, PR_ROWS, D), F32)],
        scratch_shapes=[pltpu.VMEM((NP, D, FF_BLK), BF16), pltpu.VMEM((NP, D, FF_BLK), BF16),
                        pltpu.VMEM((NP, FF_BLK, D), BF16), pltpu.VMEM((NP, D, FF_BLK), F32),
                        pltpu.VMEM((NP, D, FF_BLK), F32), pltpu.VMEM((NP, FF_BLK, D), F32)],
        compiler_params=_params(("arbitrary", "arbitrary")),
    )(h3, do3, gs, us, g2g, prg)


PACK_TAGS = ("rows", "ffn", "mix_in")
SMALL_W_ROWS = 48


def _my_place():
    x, y, c = lax.axis_index("x"), lax.axis_index("y"), lax.axis_index("c")
    return x, y, c, ((1 - x, y), (x, 1 - y), (1 - x, 1 - y))


def _remote(src, dst, send_sem, recv_sem, to):
    return pltpu.make_async_remote_copy(src_ref=src, dst_ref=dst, send_sem=send_sem, recv_sem=recv_sem,
                                        device_id=to, device_id_type=MESH)


def _gather_weights(packs):
    np_ = len(packs)

    def body(*refs):
        srcs, dsts = refs[:np_], refs[np_:2 * np_]
        send, recv, fsend, frecv, lsem = refs[2 * np_:]
        x, y, c, chips = _my_place()
        j = 2 * x + y

        def half(p, cc):
            h = packs[p].shape[0] // 2
            return pl.ds(cc * h, h)

        local = [pltpu.make_async_copy(srcs[p], dsts[p].at[j], lsem.at[p]) for p in range(np_)]
        for cp in local:
            cp.start()
        sends = []
        for p in range(np_):
            for nn, (kx, ky) in enumerate(chips):
                cp = _remote(srcs[p].at[half(p, c)], dsts[p].at[j, half(p, c)], send.at[p, nn], recv.at[p, nn],
                             (kx, ky, c))
                cp.start()
                sends.append(cp)
        fwds = []
        for nn, (kx, ky) in enumerate(chips):
            jk = 2 * kx + ky
            for p in range(np_):
                blk = dsts[p].at[jk, half(p, c)]
                _remote(blk, blk, send.at[p, nn], recv.at[p, nn], (kx, ky, c)).wait_recv()
                cp = _remote(blk, blk, fsend.at[p, nn], frecv.at[p, nn], (x, y, 1 - c))
                cp.start()
                fwds.append(cp)
        for nn, (kx, ky) in enumerate(chips):
            jk = 2 * kx + ky
            for p in range(np_):
                blk = dsts[p].at[jk, half(p, 1 - c)]
                _remote(blk, blk, fsend.at[p, nn], frecv.at[p, nn], (x, y, 1 - c)).wait_recv()
        for cp in sends + fwds:
            cp.wait_send()
        for cp in local:
            cp.wait()

    return pl.pallas_call(
        body,
        name="gather_weights",
        in_specs=[ANY] * np_,
        out_specs=[ANY] * np_,
        out_shape=[jax.ShapeDtypeStruct((N_CHIPS,) + a.shape, a.dtype) for a in packs],
        scratch_shapes=[pltpu.SemaphoreType.DMA((np_, 3))] * 4 + [pltpu.SemaphoreType.DMA((np_,))],
    )(*packs)


def _pair_exchange(packs, small):
    np_ = len(packs)
    halves = [a.shape[1] // 2 for a in packs]

    def body(*refs):
        srcs, small_ref = refs[:np_], refs[np_]
        dsts, small_all = refs[np_ + 1:2 * np_ + 1], refs[2 * np_ + 1]
        send, recv, ssend, srecv, lsem = refs[2 * np_ + 2:]
        x, y, c, _ = _my_place()
        me = 4 * x + 2 * y + c
        cps = []
        for p in range(np_):
            cp = _remote(srcs[p].at[:, pl.ds((1 - c) * halves[p], halves[p]), :], dsts[p], send.at[p], recv.at[p],
                         (x, y, 1 - c))
            cp.start()
            cps.append(cp)
        loc = pltpu.make_async_copy(small_ref, small_all.at[me], lsem)
        loc.start()
        scps = []
        for d in range(1, 8):
            flip = lambda v, bit: 1 - v if bit else v
            peer = (flip(x, d & 4), flip(y, d & 2), flip(c, d & 1))
            cp = _remote(small_ref, small_all.at[me], ssend.at[d - 1], srecv.at[d - 1], peer)
            cp.start()
            scps.append((cp, 4 * peer[0] + 2 * peer[1] + peer[2], peer))
        for cp in cps:
            cp.wait()
        for d, (cp, slot, peer) in enumerate(scps):
            _remote(small_ref, small_all.at[slot], ssend.at[d], srecv.at[d], peer).wait_recv()
            cp.wait_send()
        loc.wait()

    outs = pl.pallas_call(
        body,
        name="pair_exchange",
        in_specs=[ANY] * (np_ + 1),
        out_specs=[ANY] * (np_ + 1),
        out_shape=[jax.ShapeDtypeStruct((N_CHIPS, h, a.shape[2]), F32) for a, h in zip(packs, halves)]
        + [jax.ShapeDtypeStruct((2 * N_CHIPS,) + small.shape, F32)],
        scratch_shapes=[pltpu.SemaphoreType.DMA((np_,)), pltpu.SemaphoreType.DMA((np_,)),
                        pltpu.SemaphoreType.DMA((7,)), pltpu.SemaphoreType.DMA((7,)), pltpu.SemaphoreType.DMA],
    )(*packs, small)
    return outs[:np_], outs[np_]


def _pair_sum(pack, got, cidx, tag):
    _, rows, w = pack.shape
    h = rows // 2

    def body(c_ref, p_ref, x_ref, o_ref):
        del c_ref
        o_ref[...] = (p_ref[...] + x_ref[...]).astype(BF16)

    return pl.pallas_call(
        body,
        name="pair_sum_" + tag,
        grid_spec=pltpu.PrefetchScalarGridSpec(
            num_scalar_prefetch=1,
            grid=(N_CHIPS,),
            in_specs=[pl.BlockSpec((None, None, h, w), lambda k, c: (k, c[0], 0, 0)),
                      pl.BlockSpec((None, h, w), lambda k, c: (k, 0, 0))],
            out_specs=pl.BlockSpec((None, h, w), lambda k, c: (k, 0, 0)),
        ),
        out_shape=jax.ShapeDtypeStruct((N_CHIPS, h, w), BF16),
        compiler_params=_params(("arbitrary",)),
    )(cidx, pack.reshape(N_CHIPS, 2, h, w), got)


def _chip_exchange(packs):
    np_ = len(packs)

    def body(*refs):
        srcs, dsts = refs[:np_], refs[np_:2 * np_]
        send, recv, lsem = refs[2 * np_:]
        x, y, c, chips = _my_place()
        j = 2 * x + y
        local = [pltpu.make_async_copy(srcs[p].at[j], dsts[p].at[j], lsem.at[p]) for p in range(np_)]
        for cp in local:
            cp.start()
        sends = []
        for p in range(np_):
            for nn, (kx, ky) in enumerate(chips):
                cp = _remote(srcs[p].at[2 * kx + ky], dsts[p].at[j], send.at[p, nn], recv.at[p, nn], (kx, ky, c))
                cp.start()
                sends.append(cp)
        for p in range(np_):
            for nn, (kx, ky) in enumerate(chips):
                blk = dsts[p].at[2 * kx + ky]
                _remote(blk, blk, send.at[p, nn], recv.at[p, nn], (kx, ky, c)).wait_recv()
        for cp in sends:
            cp.wait_send()
        for cp in local:
            cp.wait()

    return pl.pallas_call(
        body,
        name="chip_exchange",
        in_specs=[ANY] * np_,
        out_specs=[ANY] * np_,
        out_shape=[jax.ShapeDtypeStruct(a.shape, a.dtype) for a in packs],
        scratch_shapes=[pltpu.SemaphoreType.DMA((np_, 3)), pltpu.SemaphoreType.DMA((np_, 3)),
                        pltpu.SemaphoreType.DMA((np_,))],
    )(*packs)


def _sum_slots(parts, steps, tag):
    nslot, rows, w = parts.shape
    tr = rows // steps

    def body(p_ref, o_ref):
        acc = p_ref[0].astype(F32)
        for k in range(1, nslot):
            acc = acc + p_ref[k].astype(F32)
        o_ref[...] = acc

    return pl.pallas_call(
        body,
        name="sum_slots_" + tag,
        grid=(steps,),
        in_specs=[pl.BlockSpec((nslot, tr, w), lambda i: (0, i, 0))],
        out_specs=pl.BlockSpec((tr, w), lambda i: (i, 0)),
        out_shape=jax.ShapeDtypeStruct((rows, w), F32),
        compiler_params=_params(("arbitrary",)),
    )(parts)


def _pair_gather(halves):
    np_ = len(halves)

    def body(*refs):
        srcs, dsts = refs[:np_], refs[np_:2 * np_]
        send, recv, lsem = refs[2 * np_:]
        x, y, c, _ = _my_place()
        cps = []
        for p in range(np_):
            h = halves[p].shape[0]
            loc = pltpu.make_async_copy(srcs[p], dsts[p].at[pl.ds(c * h, h), :], lsem.at[p])
            loc.start()
            cp = _remote(srcs[p], dsts[p].at[pl.ds(c * h, h), :], send.at[p], recv.at[p], (x, y, 1 - c))
            cp.start()
            cps.append((loc, cp))
        for p, (loc, cp) in enumerate(cps):
            h = halves[p].shape[0]
            other = dsts[p].at[pl.ds((1 - c) * h, h), :]
            _remote(other, other, send.at[p], recv.at[p], (x, y, 1 - c)).wait_recv()
            cp.wait_send()
            loc.wait()

    return pl.pallas_call(
        body,
        name="pair_gather",
        in_specs=[ANY] * np_,
        out_specs=[ANY] * np_,
        out_shape=[jax.ShapeDtypeStruct((2 * a.shape[0], a.shape[1]), F32) for a in halves],
        scratch_shapes=[pltpu.SemaphoreType.DMA((np_,))] * 3,
    )(*halves)


def _adamw(gsrc, row0, w, m, v, tr, tag):
    rows, width = w.shape
    off = row0 // tr
    bc1 = 1.0 - ADAM_B1 ** ADAM_STEP
    bc2 = 1.0 - ADAM_B2 ** ADAM_STEP

    def body(g_ref, w_ref, m_ref, v_ref, go_ref, d_ref, mo_ref, vo_ref):
        g = g_ref[...]
        m2 = ADAM_B1 * m_ref[...] + (1.0 - ADAM_B1) * g
        v2 = ADAM_B2 * v_ref[...] + (1.0 - ADAM_B2) * (g * g)
        go_ref[...] = g
        mo_ref[...] = m2
        vo_ref[...] = v2
        d_ref[...] = -ADAM_LR * ((m2 / bc1) / (jnp.sqrt(v2 / bc2) + ADAM_EPS) + ADAM_WD * w_ref[...])

    here = pl.BlockSpec((tr, width), lambda i: (i, 0))
    return pl.pallas_call(
        body,
        name="adamw_" + tag,
        grid=(rows // tr,),
        in_specs=[pl.BlockSpec((tr, width), lambda i: (off + i, 0)), here, here, here],
        out_specs=[here] * 4,
        out_shape=[jax.ShapeDtypeStruct((rows, width), F32)] * 4,
        compiler_params=_params(("arbitrary",)),
    )(gsrc, w, m, v)


def _flat_rows(a):
    return a.reshape(-1, 128)


def kernel(x, mem, mix_pre_g, w_mix_in, conv_a_w, conv_b_w, conv_b_b, ln_b_g, ln_b_b, w_mix_out, mix_post_g, xa_pre_g, mem_norm_g, w_q, w_k, w_v, w_o, xa_post_g, ffn_pre_g, w_gate, w_up, w_down, ffn_post_g, loss_target, m_mix_pre_g, m_w_mix_in, m_conv_a_w, m_conv_b_w, m_conv_b_b, m_ln_b_g, m_ln_b_b, m_w_mix_out, m_mix_post_g, m_xa_pre_g, m_mem_norm_g, m_w_q, m_w_k, m_w_v, m_w_o, m_xa_post_g, m_ffn_pre_g, m_w_gate, m_w_up, m_w_down, m_ffn_post_g, v_mix_pre_g, v_w_mix_in, v_conv_a_w, v_conv_b_w, v_conv_b_b, v_ln_b_g, v_ln_b_b, v_w_mix_out, v_mix_post_g, v_xa_pre_g, v_mem_norm_g, v_w_q, v_w_k, v_w_v, v_w_o, v_xa_post_g, v_ffn_pre_g, v_w_gate, v_w_up, v_w_down, v_ffn_post_g):
    given = dict(locals())
    names = ["mix_pre_g", "w_mix_in", "conv_a_w", "conv_b_w", "conv_b_b", "ln_b_g", "ln_b_b", "w_mix_out",
             "mix_post_g", "xa_pre_g", "mem_norm_g", "w_q", "w_k", "w_v", "w_o", "xa_post_g", "ffn_pre_g",
             "w_gate", "w_up", "w_down", "ffn_post_g"]
    row = lambda a: a.reshape(1, -1)
    cx, cy, cc = lax.axis_index("x"), lax.axis_index("y"), lax.axis_index("c")
    chip = 2 * cx + cy
    ca_blk = conv_a_w.shape[1]

    pr = jnp.concatenate([w_mix_out, w_q, w_k, w_v, w_o, w_down], axis=0).astype(BF16)
    g2 = jnp.concatenate([w_gate, w_up], axis=0).astype(BF16)
    g1 = w_mix_in.astype(BF16)
    conv_rows = CONV_A_W + CONV_B_W
    sw = jnp.concatenate([conv_a_w, conv_b_w, jnp.zeros((SMALL_W_ROWS - conv_rows, ca_blk), F32)], axis=0)
    prg, g2g, g1g, swg = _gather_weights([pr, g2, g1, sw])
    conv_full = jnp.transpose(swg[:, :conv_rows, :], (1, 0, 2)).reshape(conv_rows, N_CHIPS * ca_blk)
    wa, wb = conv_full[:CONV_A_W], conv_full[CONV_A_W:]

    xs, ms, tgt = x[0], mem[0], loss_target[0]
    x1, u, o1, z1 = _mix_fwd(xs, row(mix_pre_g), row(mix_post_g), wa, wb, row(conv_b_b), row(ln_b_g), row(ln_b_b),
                             g1g, prg)
    memn, kb, vb = _mem_kv(ms, row(mem_norm_g), prg)
    x2, q, o2 = _xattn_fwd(x1, row(xa_pre_g), row(xa_post_g), kb, vb, prg)
    h3, gs, us, do3, dx3, loss_part, d_ffn_post = _ffn_fwd(x2, tgt, row(ffn_pre_g), row(ffn_post_g), g2g, prg)

    dh3p, gg2, gpr = _ffn_bwd(h3, do3, gs, us, g2g, prg)
    dx1, d_ffn_pre, d_xa_post, d_xa_pre, dk, dv, gpr = _xattn_bwd(
        dx3, dh3p, x2, x1, o2, q, kb, vb, row(ffn_pre_g), row(xa_post_g), row(xa_pre_g), prg, gpr)
    d_mem_g, gpr = _mem_bwd(dk, dv, ms, memn, row(mem_norm_g), prg, gpr)
    dx, d_mix_pre, d_mix_post, dwa, dwb, dbb, dlng, dlnb, gg1, gpr = _mix_bwd(
        dx1, xs, o1, u, z1, row(mix_pre_g), row(mix_post_g), wa, wb, row(ln_b_g), row(ln_b_b), g1g, prg, gpr)

    small_parts = [d_mix_pre, dwa, dwb, dbb, dlng, dlnb, d_mix_post, d_xa_pre, d_mem_g, d_xa_post, d_ffn_pre,
                   d_ffn_post, loss_part]
    sizes = [p.size for p in small_parts]
    small = jnp.concatenate([p.reshape(-1) for p in small_parts])
    small_rows = -(-small.size // (8 * 128)) * 8
    small = jnp.pad(small, (0, small_rows * 128 - small.size)).reshape(small_rows, 128)
    got, small_all = _pair_exchange([gpr, gg2, gg1], small)
    cidx = cc.astype(jnp.int32).reshape(1)
    sums = [_pair_sum(p, g, cidx, t) for p, g, t in zip([gpr, gg2, gg1], got, PACK_TAGS)]
    parts = _chip_exchange(sums)
    halves = [_sum_slots(p, 2, t) for p, t in zip(parts, PACK_TAGS)]
    rpr, rg2, rg1 = _pair_gather(halves)
    small_sum = _sum_slots(small_all, 1, "small").reshape(-1)
    red, pos = [], 0
    for p, sz in zip(small_parts, sizes):
        red.append(small_sum[pos:pos + sz].reshape(p.shape))
        pos += sz
    (r_mix_pre, r_wa, r_wb, r_bb, r_lng, r_lnb, r_mix_post, r_xa_pre, r_mem_g, r_xa_post, r_ffn_pre, r_ffn_post,
     r_loss) = red
    loss = r_loss[0, 0]

    res = {}
    big = {"w_mix_out": (rpr, PR_OUT, 256), "w_q": (rpr, PR_Q, 256), "w_k": (rpr, PR_K, 256),
           "w_v": (rpr, PR_V, 256), "w_o": (rpr, PR_O, 256), "w_down": (rpr, PR_DOWN, 64),
           "w_gate": (rg2, 0, 256), "w_up": (rg2, D_MODEL, 256), "w_mix_in": (rg1, 0, 256)}
    for nm, (src, row0, tr) in big.items():
        res[nm] = _adamw(src, row0, given[nm], given["m_" + nm], given["v_" + nm], tr, nm)
    small_grads = {"mix_pre_g": r_mix_pre, "conv_b_b": r_bb, "ln_b_g": r_lng, "ln_b_b": r_lnb,
                   "mix_post_g": r_mix_post, "xa_pre_g": r_xa_pre, "mem_norm_g": r_mem_g, "xa_post_g": r_xa_post,
                   "ffn_pre_g": r_ffn_pre, "ffn_post_g": r_ffn_post,
                   "conv_a_w": lax.dynamic_slice_in_dim(r_wa, chip * ca_blk, ca_blk, axis=1),
                   "conv_b_w": lax.dynamic_slice_in_dim(r_wb, chip * ca_blk, ca_blk, axis=1)}
    small_names = list(small_grads)

    def packed(prefix, grads=None):
        flat = jnp.concatenate([(grads[nm] if grads else given[prefix + nm]).reshape(-1) for nm in small_names])
        rows8 = -(-flat.size // (8 * 128)) * 8
        return jnp.pad(flat, (0, rows8 * 128 - flat.size)).reshape(rows8, 128)

    gp = packed("", small_grads)
    outs = _adamw(gp, 0, packed(""), packed("m_"), packed("v_"), gp.shape[0], "small")
    pos = 0
    for nm in small_names:
        shape = given[nm].shape
        sz = given[nm].size
        res[nm] = [o.reshape(-1)[pos:pos + sz].reshape(shape) for o in outs]
        pos += sz

    return (loss, dx[None], *[res[nm][0] for nm in names], *[res[nm][1] for nm in names],
            *[res[nm][2] for nm in names], *[res[nm][3] for nm in names])
```

```python
import functools

import jax
import jax.numpy as jnp
from jax import lax
from jax.experimental import pallas as pl
from jax.experimental.pallas import tpu as pltpu

F32 = jnp.float32
BF16 = jnp.bfloat16
MESH = pl.DeviceIdType.MESH

RMS_EPS = 1e-6
LN_EPS = 1e-5
D_MODEL = 1024
D_A = 512
D_B = 512
D_IN_ALL = 3 * D_A + 2 * D_B
CONV_A_W = 3
CONV_B_W = 31
HALO = 32
XA_HEADS = 4
HEAD_DIM = 256
D_FF = 2816
N_CHIPS = 4
FF_BLK = D_FF // N_CHIPS
IN_BLK = D_IN_ALL // N_CHIPS
ROW_BLK = D_MODEL // N_CHIPS

ADAM_LR = 0.001
ADAM_B1 = 0.9
ADAM_B2 = 0.999
ADAM_EPS = 1e-08
ADAM_WD = 0.01
ADAM_STEP = 10

TILE_FWD = 512
TILE_FFN = 512
TILE_BWD = 256
FFN_BWD_BLOCKS = 1
CONV_ROWS = 64
V7X_VMEM_LIMIT = 56 * 1024 * 1024

PR_OUT, PR_Q, PR_K, PR_V, PR_O, PR_DOWN = 0, 256, 512, 768, 1024, 1280
PR_ROWS = PR_DOWN + FF_BLK
G2_ROWS = 2 * D_MODEL
G1_ROWS = D_MODEL

ANY = pl.BlockSpec(memory_space=pl.ANY)


def _mm(a, b):
    return lax.dot_general(a, b, (((1,), (0,)), ((), ())), preferred_element_type=F32)


def _mm_nt(a, b):
    return lax.dot_general(a, b, (((1,), (1,)), ((), ())), preferred_element_type=F32)


def _mm_tn(a, b):
    return lax.dot_general(a, b, (((0,), (0,)), ((), ())), preferred_element_type=F32)


def _sigmoid(x):
    return 1.0 / (1.0 + jnp.exp(-x))


def _rms(x):
    r = lax.rsqrt(jnp.mean(x * x, axis=-1, keepdims=True) + RMS_EPS)
    return x * r, r


def _rms_bwd(dy, xn, r, g):
    gdy = dy * g
    dx = r * (gdy - xn * jnp.mean(gdy * xn, axis=-1, keepdims=True))
    return dx, jnp.sum(dy * xn, axis=0, keepdims=True)


def _full(shape):
    return pl.BlockSpec(shape, lambda *_: (0,) * len(shape))


def _params(sem=None):
    return pltpu.CompilerParams(dimension_semantics=sem, vmem_limit_bytes=V7X_VMEM_LIMIT)


def _load_rows(dst, src_hbm, row0, rows):
    for k in range(N_CHIPS):
        pltpu.sync_copy(src_hbm.at[k, pl.ds(row0, rows), :], dst.at[pl.ds(rows * k, rows), :])


def _load_cols(dst, src_hbm, row0, rows, cols):
    for k in range(N_CHIPS):
        pltpu.sync_copy(src_hbm.at[k, pl.ds(row0, rows), :], dst.at[:, pl.ds(cols * k, cols)])


def _fill_phases(src, sh, nrows):
    for r in range(1, 8):
        sh[r, pl.ds(0, nrows), :] = src[pl.ds(r, nrows), pl.ds(D_A, D_B)]


def _phase_rows(src, sh, off, start, size):
    r = off % 8
    if r == 0:
        return src[pl.ds(off + start, size), pl.ds(D_A, D_B)]
    return sh[r, pl.ds(off - r + start, size), :]


def _mix_fwd(x, gpre, gpost, wa, wb, bb, lng, lnb, g1g, prg):
    S, D = x.shape
    T = min(TILE_FWD, S)
    n = S // T

    def body(x_ref, gpre_ref, gpost_ref, wa_ref, wb_ref, bb_ref, lng_ref, lnb_ref, g1g_hbm, prg_hbm,
             x1_ref, u_ref, o1_ref, z1_ref, win_v, wout_v, ext, sh, z1buf):
        i = pl.program_id(0)

        @pl.when(i == 0)
        def _():
            _load_cols(win_v, g1g_hbm, 0, D, IN_BLK)
            _load_rows(wout_v, prg_hbm, PR_OUT, ROW_BLK)
            ext[pl.ds(0, HALO), :] = jnp.zeros((HALO, D_A + D_B), F32)

        xv = x_ref[...]
        xn, _ = _rms(xv)
        h = (xn * gpre_ref[...]).astype(BF16)
        u = _mm(h, win_v[...])
        u_ref[...] = u.astype(BF16)
        b_a = u[:, 0:D_A]
        cv = u[:, D_A:2 * D_A] * u[:, 2 * D_A:3 * D_A]
        z0 = u[:, 3 * D_A:3 * D_A + D_B] * _sigmoid(u[:, 3 * D_A + D_B:])
        ext[pl.ds(HALO, T), pl.ds(0, D_A)] = cv
        ext[pl.ds(HALO, T), pl.ds(D_A, D_B)] = z0

        conv_a = ext[pl.ds(HALO - 2, T), pl.ds(0, D_A)] * wa_ref[0:1, :]
        for k in range(1, CONV_A_W):
            conv_a = conv_a + ext[pl.ds(HALO - 2 + k, T), pl.ds(0, D_A)] * wa_ref[k:k + 1, :]
        y_a = b_a * conv_a

        _fill_phases(ext, sh, T + HALO - 8)
        base = HALO - (CONV_B_W - 1)

        def chunk(ci, carry):
            start = pl.multiple_of(ci * CONV_ROWS, 8)
            acc = jnp.broadcast_to(bb_ref[...], (CONV_ROWS, D_B))
            for k in range(CONV_B_W):
                acc = acc + _phase_rows(ext, sh, base + k, start, CONV_ROWS) * wb_ref[k:k + 1, :]
            z1buf[pl.ds(start, CONV_ROWS), :] = acc
            return carry

        lax.fori_loop(0, T // CONV_ROWS, chunk, 0)
        z1 = z1buf[...]
        z1_ref[...] = z1.astype(BF16)
        mu = jnp.mean(z1, axis=-1, keepdims=True)
        zc = z1 - mu
        rstd = lax.rsqrt(jnp.mean(zc * zc, axis=-1, keepdims=True) + LN_EPS)
        l = zc * rstd * lng_ref[...] + lnb_ref[...]
        y_b = l * _sigmoid(l)
        y = jnp.concatenate([y_a, y_b], axis=-1).astype(BF16)
        o1 = _mm(y, wout_v[...])
        o1_ref[...] = o1.astype(BF16)
        o1n, _ = _rms(o1)
        x1_ref[...] = xv + o1n * gpost_ref[...]
        ext[pl.ds(0, HALO), :] = ext[pl.ds(T, HALO), :]

    tok = lambda w: pl.BlockSpec((T, w), lambda i: (i, 0))
    return pl.pallas_call(
        body,
        name="mix_fwd",
        grid=(n,),
        in_specs=[tok(D), _full((1, D)), _full((1, D)), _full((CONV_A_W, D_A)), _full((CONV_B_W, D_B)),
                  _full((1, D_B)), _full((1, D_B)), _full((1, D_B)), ANY, ANY],
        out_specs=[tok(D), tok(D_IN_ALL), tok(D), tok(D_B)],
        out_shape=[jax.ShapeDtypeStruct((S, D), F32), jax.ShapeDtypeStruct((S, D_IN_ALL), BF16),
                   jax.ShapeDtypeStruct((S, D), BF16), jax.ShapeDtypeStruct((S, D_B), BF16)],
        scratch_shapes=[pltpu.VMEM((D, D_IN_ALL), BF16), pltpu.VMEM((D_A + D_B, D), BF16),
                        pltpu.VMEM((HALO + T, D_A + D_B), F32), pltpu.VMEM((8, HALO + T, D_B), F32),
                        pltpu.VMEM((T, D_B), F32)],
        compiler_params=_params(("arbitrary",)),
    )(x, gpre, gpost, wa, wb, bb, lng, lnb, g1g, prg)


def _mix_bwd(dx1, x, o1, u, z1s, gpre, gpost, wa, wb, lng, lnb, g1g, prg, gpr_in):
    S, D = x.shape
    T = min(TILE_BWD, S)
    n = S // T
    hb = T // HALO

    def body(dx1_ref, x_ref, o1_ref, u_ref, uh_ref, z1_ref, gpre_ref, gpost_ref, wa_ref, wb_ref, lng_ref, lnb_ref,
             g1g_hbm, prg_hbm, gpr_hbm,
             dx_ref, dgpre_ref, dgpost_ref, dwa_ref, dwb_ref, dbb_ref, dlng_ref, dlnb_ref, gg1_hbm, gpr_out,
             win_v, wout_v, dwin_acc, dwout_acc, ext, ext2, shf, shb, dz0buf, dwb_acc):
        del gpr_hbm
        i = pl.program_id(0)

        @pl.when(i == 0)
        def _():
            _load_cols(win_v, g1g_hbm, 0, D, IN_BLK)
            _load_rows(wout_v, prg_hbm, PR_OUT, ROW_BLK)
            dwin_acc[...] = jnp.zeros_like(dwin_acc)
            dwout_acc[...] = jnp.zeros_like(dwout_acc)
            dwb_acc[...] = jnp.zeros_like(dwb_acc)
            ext2[pl.ds(T, HALO), :] = jnp.zeros((HALO, D_A + D_B), F32)
            for ref in (dgpre_ref, dgpost_ref, dwa_ref, dbb_ref, dlng_ref, dlnb_ref):
                ref[...] = jnp.zeros_like(ref)

        o1n, r1 = _rms(o1_ref[...].astype(F32))
        dx1v = dx1_ref[...]
        d_o1, dgp = _rms_bwd(dx1v, o1n, r1, gpost_ref[...])
        dgpost_ref[...] += dgp
        d_o1b = d_o1.astype(BF16)
        dy = _mm_nt(d_o1b, wout_v[...])

        first = (i == n - 1).astype(F32)
        uh = uh_ref[...].astype(F32) * (1.0 - first)
        ext[pl.ds(0, HALO), pl.ds(0, D_A)] = uh[:, D_A:2 * D_A] * uh[:, 2 * D_A:3 * D_A]
        ext[pl.ds(0, HALO), pl.ds(D_A, D_B)] = uh[:, 3 * D_A:3 * D_A + D_B] * _sigmoid(uh[:, 3 * D_A + D_B:])
        uf = u_ref[...].astype(F32)
        b_a = uf[:, 0:D_A]
        c_a = uf[:, D_A:2 * D_A]
        v_a = uf[:, 2 * D_A:3 * D_A]
        gv = uf[:, 3 * D_A:3 * D_A + D_B]
        sg = _sigmoid(uf[:, 3 * D_A + D_B:])
        ext[pl.ds(HALO, T), pl.ds(0, D_A)] = c_a * v_a
        ext[pl.ds(HALO, T), pl.ds(D_A, D_B)] = gv * sg
        conv_a = ext[pl.ds(HALO - 2, T), pl.ds(0, D_A)] * wa_ref[0:1, :]
        for k in range(1, CONV_A_W):
            conv_a = conv_a + ext[pl.ds(HALO - 2 + k, T), pl.ds(0, D_A)] * wa_ref[k:k + 1, :]
        z1 = z1_ref[...].astype(F32)
        mu = jnp.mean(z1, axis=-1, keepdims=True)
        zc = z1 - mu
        rstd = lax.rsqrt(jnp.mean(zc * zc, axis=-1, keepdims=True) + LN_EPS)
        zn = zc * rstd
        l = zn * lng_ref[...] + lnb_ref[...]
        sl = _sigmoid(l)
        y = jnp.concatenate([b_a * conv_a, l * sl], axis=-1).astype(BF16)
        dwout_acc[...] += _mm_tn(y, d_o1b)

        dy_a = dy[:, 0:D_A]
        dl = dy[:, D_A:] * (sl * (1.0 + l * (1.0 - sl)))
        dlng_ref[...] += jnp.sum(dl * zn, axis=0, keepdims=True)
        dlnb_ref[...] += jnp.sum(dl, axis=0, keepdims=True)
        dzn = dl * lng_ref[...]
        dz1 = rstd * (dzn - jnp.mean(dzn, axis=-1, keepdims=True) - zn * jnp.mean(dzn * zn, axis=-1, keepdims=True))
        dbb_ref[...] += jnp.sum(dz1, axis=0, keepdims=True)
        d_conv = dy_a * b_a
        ext2[pl.ds(0, T), pl.ds(0, D_A)] = d_conv
        ext2[pl.ds(0, T), pl.ds(D_A, D_B)] = dz1

        d_cv = ext2[pl.ds(CONV_A_W - 1, T), pl.ds(0, D_A)] * wa_ref[0:1, :]
        for k in range(1, CONV_A_W):
            d_cv = d_cv + ext2[pl.ds(CONV_A_W - 1 - k, T), pl.ds(0, D_A)] * wa_ref[k:k + 1, :]
        for k in range(CONV_A_W):
            dwa_ref[k:k + 1, :] += jnp.sum(d_conv * ext[pl.ds(HALO - 2 + k, T), pl.ds(0, D_A)], axis=0, keepdims=True)

        _fill_phases(ext, shf, T + HALO - 8)
        _fill_phases(ext2, shb, T + HALO - 8)
        base = HALO - (CONV_B_W - 1)

        def chunk(ci, carry):
            start = pl.multiple_of(ci * CONV_ROWS, 8)
            dzc = ext2[pl.ds(start, CONV_ROWS), pl.ds(D_A, D_B)]
            acc = jnp.zeros((CONV_ROWS, D_B), F32)
            for k in range(CONV_B_W):
                wk = wb_ref[k:k + 1, :]
                acc = acc + _phase_rows(ext2, shb, CONV_B_W - 1 - k, start, CONV_ROWS) * wk
                prod = dzc * _phase_rows(ext, shf, base + k, start, CONV_ROWS)
                part = prod[0:8, :]
                for m in range(1, CONV_ROWS // 8):
                    part = part + prod[8 * m:8 * m + 8, :]
                dwb_acc[k] += part
            dz0buf[pl.ds(start, CONV_ROWS), :] = acc
            return carry

        lax.fori_loop(0, T // CONV_ROWS, chunk, 0)
        dz0 = dz0buf[...]
        du = jnp.concatenate([dy_a * conv_a, d_cv * v_a, d_cv * c_a, dz0 * sg, dz0 * gv * sg * (1.0 - sg)],
                             axis=-1).astype(BF16)
        dh = _mm_nt(du, win_v[...])
        xv = x_ref[...]
        xn, r0 = _rms(xv)
        dwin_acc[...] += _mm_tn((xn * gpre_ref[...]).astype(BF16), du)
        dxp, dg0 = _rms_bwd(dh, xn, r0, gpre_ref[...])
        dgpre_ref[...] += dg0
        dx_ref[...] = dx1v + dxp
        ext2[pl.ds(T, HALO), :] = ext2[pl.ds(0, HALO), :]

        @pl.when(i == n - 1)
        def _():
            for k in range(CONV_B_W):
                dwb_ref[k:k + 1, :] = jnp.sum(dwb_acc[k], axis=0, keepdims=True)
            for k in range(N_CHIPS):
                pltpu.sync_copy(dwin_acc.at[:, pl.ds(IN_BLK * k, IN_BLK)], gg1_hbm.at[k])
                pltpu.sync_copy(dwout_acc.at[pl.ds(ROW_BLK * k, ROW_BLK), :], gpr_out.at[k, pl.ds(PR_OUT, ROW_BLK), :])

    rev = lambda w: pl.BlockSpec((T, w), lambda i: (n - 1 - i, 0))
    halo = pl.BlockSpec((HALO, D_IN_ALL), lambda i: (jnp.maximum((n - 1 - i) * hb - 1, 0), 0))
    outs = pl.pallas_call(
        body,
        name="mix_bwd",
        grid=(n,),
        in_specs=[rev(D), rev(D), rev(D), rev(D_IN_ALL), halo, rev(D_B), _full((1, D)), _full((1, D)),
                  _full((CONV_A_W, D_A)), _full((CONV_B_W, D_B)), _full((1, D_B)), _full((1, D_B)), ANY, ANY, ANY],
        out_specs=[rev(D), _full((1, D)), _full((1, D)), _full((CONV_A_W, D_A)), _full((CONV_B_W, D_B)),
                   _full((1, D_B)), _full((1, D_B)), _full((1, D_B)), ANY, ANY],
        out_shape=[jax.ShapeDtypeStruct((S, D), F32), jax.ShapeDtypeStruct((1, D), F32),
                   jax.ShapeDtypeStruct((1, D), F32), jax.ShapeDtypeStruct((CONV_A_W, D_A), F32),
                   jax.ShapeDtypeStruct((CONV_B_W, D_B), F32), jax.ShapeDtypeStruct((1, D_B), F32),
                   jax.ShapeDtypeStruct((1, D_B), F32), jax.ShapeDtypeStruct((1, D_B), F32),
                   jax.ShapeDtypeStruct((N_CHIPS, G1_ROWS, IN_BLK), F32),
                   jax.ShapeDtypeStruct(gpr_in.shape, F32)],
        input_output_aliases={14: 9},
        scratch_shapes=[pltpu.VMEM((D, D_IN_ALL), BF16), pltpu.VMEM((D_A + D_B, D), BF16),
                        pltpu.VMEM((D, D_IN_ALL), F32), pltpu.VMEM((D_A + D_B, D), F32),
                        pltpu.VMEM((HALO + T, D_A + D_B), F32), pltpu.VMEM((HALO + T, D_A + D_B), F32),
                        pltpu.VMEM((8, HALO + T, D_B), F32), pltpu.VMEM((8, HALO + T, D_B), F32),
                        pltpu.VMEM((T, D_B), F32), pltpu.VMEM((CONV_B_W, 8, D_B), F32)],
        compiler_params=_params(("arbitrary",)),
    )(dx1, x, o1, u, u, z1s, gpre, gpost, wa, wb, lng, lnb, g1g, prg, gpr_in)
    return outs


def _mem_kv(mem, gmem, prg):
    M, D = mem.shape

    def body(mem_ref, g_ref, prg_hbm, memn_ref, k_ref, v_ref, wk_v, wv_v):
        _load_rows(wk_v, prg_hbm, PR_K, ROW_BLK)
        _load_rows(wv_v, prg_hbm, PR_V, ROW_BLK)
        mn, _ = _rms(mem_ref[...])
        mb = (mn * g_ref[...]).astype(BF16)
        memn_ref[...] = mb
        k_ref[...] = _mm(mb, wk_v[...]).astype(BF16)
        v_ref[...] = _mm(mb, wv_v[...]).astype(BF16)

    return pl.pallas_call(
        body,
        name="mem_kv",
        grid=(1,),
        in_specs=[_full((M, D)), _full((1, D)), ANY],
        out_specs=[_full((M, D))] * 3,
        out_shape=[jax.ShapeDtypeStruct((M, D), BF16)] * 3,
        scratch_shapes=[pltpu.VMEM((D, D), BF16), pltpu.VMEM((D, D), BF16)],
        compiler_params=_params(("arbitrary",)),
    )(mem, gmem, prg)


def _attend(qb, kb, vb):
    scale = HEAD_DIM ** -0.5
    ps, os_ = [], []
    for hd in range(XA_HEADS):
        cols = slice(HEAD_DIM * hd, HEAD_DIM * (hd + 1))
        s = _mm_nt(qb[:, cols], kb[:, cols]) * scale
        e = jnp.exp(s - jnp.max(s, axis=-1, keepdims=True))
        p = e / jnp.sum(e, axis=-1, keepdims=True)
        ps.append(p)
        os_.append(_mm(p.astype(BF16), vb[:, cols]))
    return ps, jnp.concatenate(os_, axis=-1).astype(BF16)


def _xattn_fwd(x1, gpre, gpost, kb, vb, prg):
    S, D = x1.shape
    M = kb.shape[0]
    T = min(TILE_FWD, S)
    n = S // T

    def body(x1_ref, gpre_ref, gpost_ref, k_ref, v_ref, prg_hbm, x2_ref, q_ref, o2_ref, wq_v, wo_v):
        @pl.when(pl.program_id(0) == 0)
        def _():
            _load_rows(wq_v, prg_hbm, PR_Q, ROW_BLK)
            _load_rows(wo_v, prg_hbm, PR_O, ROW_BLK)

        xv = x1_ref[...]
        xn, _ = _rms(xv)
        qb = _mm((xn * gpre_ref[...]).astype(BF16), wq_v[...]).astype(BF16)
        q_ref[...] = qb
        _, ob = _attend(qb, k_ref[...], v_ref[...])
        o2 = _mm(ob, wo_v[...])
        o2_ref[...] = o2.astype(BF16)
        o2n, _ = _rms(o2)
        x2_ref[...] = xv + o2n * gpost_ref[...]

    tok = lambda w: pl.BlockSpec((T, w), lambda i: (i, 0))
    return pl.pallas_call(
        body,
        name="xattn_fwd",
        grid=(n,),
        in_specs=[tok(D), _full((1, D)), _full((1, D)), _full((M, D)), _full((M, D)), ANY],
        out_specs=[tok(D), tok(D), tok(D)],
        out_shape=[jax.ShapeDtypeStruct((S, D), F32), jax.ShapeDtypeStruct((S, D), BF16),
                   jax.ShapeDtypeStruct((S, D), BF16)],
        scratch_shapes=[pltpu.VMEM((D, D), BF16), pltpu.VMEM((D, D), BF16)],
        compiler_params=_params(("arbitrary",)),
    )(x1, gpre, gpost, kb, vb, prg)


def _xattn_bwd(dx3, dh3p, x2, x1, o2, q, kb, vb, gffn, gpost, gpre, prg, gpr_in):
    S, D = x1.shape
    M = kb.shape[0]
    T = min(TILE_BWD, S)
    n = S // T
    scale = HEAD_DIM ** -0.5

    nparts = dh3p.shape[0]

    def body(*refs):
        dx3_ref, dh3_refs = refs[0], refs[1:1 + nparts]
        (x2_ref, x1_ref, o2_ref, q_ref, k_ref, v_ref, gffn_ref, gpost_ref, gpre_ref, prg_hbm, _,
         dx1_ref, dgffn_ref, dgpost_ref, dgpre_ref, dk_ref, dv_ref, gpr_out,
         wq_v, wo_v, dwq_acc, dwo_acc) = refs[1 + nparts:]
        i = pl.program_id(0)

        @pl.when(i == 0)
        def _():
            _load_rows(wq_v, prg_hbm, PR_Q, ROW_BLK)
            _load_rows(wo_v, prg_hbm, PR_O, ROW_BLK)
            dwq_acc[...] = jnp.zeros_like(dwq_acc)
            dwo_acc[...] = jnp.zeros_like(dwo_acc)
            for ref in (dgffn_ref, dgpost_ref, dgpre_ref, dk_ref, dv_ref):
                ref[...] = jnp.zeros_like(ref)

        x2n, r2 = _rms(x2_ref[...])
        dh3 = dh3_refs[0][...].astype(F32)
        for ref in dh3_refs[1:]:
            dh3 = dh3 + ref[...].astype(F32)
        dxp, dg = _rms_bwd(dh3, x2n, r2, gffn_ref[...])
        dgffn_ref[...] += dg
        dx2 = dx3_ref[...] + dxp
        o2n, ro = _rms(o2_ref[...].astype(F32))
        d_o2, dg = _rms_bwd(dx2, o2n, ro, gpost_ref[...])
        dgpost_ref[...] += dg
        d_o2b = d_o2.astype(BF16)
        d_o = _mm_nt(d_o2b, wo_v[...]).astype(BF16)
        qb = q_ref[...]
        kv = k_ref[...]
        vv = v_ref[...]
        ps, ob = _attend(qb, kv, vv)
        dwo_acc[...] += _mm_tn(ob, d_o2b)
        dqs = []
        for hd in range(XA_HEADS):
            cols = slice(HEAD_DIM * hd, HEAD_DIM * (hd + 1))
            p = ps[hd]
            dp = _mm_nt(d_o[:, cols], vv[:, cols])
            dv_ref[:, cols] += _mm_tn(p.astype(BF16), d_o[:, cols])
            ds = (p * (dp - jnp.sum(p * dp, axis=-1, keepdims=True)) * scale).astype(BF16)
            dqs.append(_mm(ds, kv[:, cols]))
            dk_ref[:, cols] += _mm_tn(ds, qb[:, cols])
        dq = jnp.concatenate(dqs, axis=-1).astype(BF16)
        dh2 = _mm_nt(dq, wq_v[...])
        x1n, r1 = _rms(x1_ref[...])
        dwq_acc[...] += _mm_tn((x1n * gpre_ref[...]).astype(BF16), dq)
        dxp, dg = _rms_bwd(dh2, x1n, r1, gpre_ref[...])
        dgpre_ref[...] += dg
        dx1_ref[...] = dx2 + dxp

        @pl.when(i == n - 1)
        def _():
            for k in range(N_CHIPS):
                rows = pl.ds(ROW_BLK * k, ROW_BLK)
                pltpu.sync_copy(dwq_acc.at[rows, :], gpr_out.at[k, pl.ds(PR_Q, ROW_BLK), :])
                pltpu.sync_copy(dwo_acc.at[rows, :], gpr_out.at[k, pl.ds(PR_O, ROW_BLK), :])

    tok = lambda w: pl.BlockSpec((T, w), lambda i: (i, 0))
    part = lambda j: pl.BlockSpec((None, T, D), lambda i: (j, i, 0))
    return pl.pallas_call(
        body,
        name="xattn_bwd",
        grid=(n,),
        in_specs=[tok(D)] + [part(j) for j in range(nparts)] + [tok(D), tok(D), tok(D), tok(D), _full((M, D)),
                                                                 _full((M, D)), _full((1, D)), _full((1, D)),
                                                                 _full((1, D)), ANY, ANY],
        out_specs=[tok(D), _full((1, D)), _full((1, D)), _full((1, D)), _full((M, D)), _full((M, D)), ANY],
        out_shape=[jax.ShapeDtypeStruct((S, D), F32), jax.ShapeDtypeStruct((1, D), F32),
                   jax.ShapeDtypeStruct((1, D), F32), jax.ShapeDtypeStruct((1, D), F32),
                   jax.ShapeDtypeStruct((M, D), F32), jax.ShapeDtypeStruct((M, D), F32),
                   jax.ShapeDtypeStruct(gpr_in.shape, F32)],
        input_output_aliases={nparts + 11: 6},
        scratch_shapes=[pltpu.VMEM((D, D), BF16), pltpu.VMEM((D, D), BF16),
                        pltpu.VMEM((D, D), F32), pltpu.VMEM((D, D), F32)],
        compiler_params=_params(("arbitrary",)),
    )(dx3, *([dh3p] * nparts), x2, x1, o2, q, kb, vb, gffn, gpost, gpre, prg, gpr_in)


def _mem_bwd(dk, dv, mem, memn, gmem, prg, gpr_in):
    M, D = mem.shape

    def body(dk_ref, dv_ref, mem_ref, memn_ref, g_ref, prg_hbm, gpr_hbm, dg_ref, gpr_out, wk_v, wv_v, dwk_v, dwv_v):
        del gpr_hbm
        _load_rows(wk_v, prg_hbm, PR_K, ROW_BLK)
        _load_rows(wv_v, prg_hbm, PR_V, ROW_BLK)
        dkb = dk_ref[...].astype(BF16)
        dvb = dv_ref[...].astype(BF16)
        mb = memn_ref[...]
        dwk_v[...] = _mm_tn(mb, dkb)
        dwv_v[...] = _mm_tn(mb, dvb)
        dmn = _mm_nt(dkb, wk_v[...]) + _mm_nt(dvb, wv_v[...])
        mn, _ = _rms(mem_ref[...])
        dg_ref[...] = jnp.sum(dmn * mn, axis=0, keepdims=True)
        for k in range(N_CHIPS):
            rows = pl.ds(ROW_BLK * k, ROW_BLK)
            pltpu.sync_copy(dwk_v.at[rows, :], gpr_out.at[k, pl.ds(PR_K, ROW_BLK), :])
            pltpu.sync_copy(dwv_v.at[rows, :], gpr_out.at[k, pl.ds(PR_V, ROW_BLK), :])

    return pl.pallas_call(
        body,
        name="mem_bwd",
        grid=(1,),
        in_specs=[_full((M, D)), _full((M, D)), _full((M, D)), _full((M, D)), _full((1, D)), ANY, ANY],
        out_specs=[_full((1, D)), ANY],
        out_shape=[jax.ShapeDtypeStruct((1, D), F32), jax.ShapeDtypeStruct(gpr_in.shape, F32)],
        input_output_aliases={6: 1},
        scratch_shapes=[pltpu.VMEM((D, D), BF16), pltpu.VMEM((D, D), BF16),
                        pltpu.VMEM((D, D), F32), pltpu.VMEM((D, D), F32)],
        compiler_params=_params(("arbitrary",)),
    )(dk, dv, mem, memn, gmem, prg, gpr_in)


def _ffn_fwd(x2, target, gpre, gpost, g2g, prg):
    S, D = x2.shape
    T = min(TILE_FFN, S)
    n = S // T

    def body(x2_ref, t_ref, gpre_ref, gpost_ref, g2g_hbm, prg_hbm,
             h3_ref, g_hbm, u_hbm, do3_ref, dx3_ref, loss_ref, dgpost_ref, wg_v, wu_v, wd_v, gst, ust, sem):
        i = pl.program_id(0)

        @pl.when(i == 0)
        def _():
            pltpu.sync_copy(g2g_hbm.at[:, pl.ds(0, D), :], wg_v)
            pltpu.sync_copy(g2g_hbm.at[:, pl.ds(D, D), :], wu_v)
            pltpu.sync_copy(prg_hbm.at[:, pl.ds(PR_DOWN, FF_BLK), :], wd_v)
            loss_ref[...] = jnp.zeros_like(loss_ref)
            dgpost_ref[...] = jnp.zeros_like(dgpost_ref)

        xv = x2_ref[...]
        xn, _ = _rms(xv)
        hb = (xn * gpre_ref[...]).astype(BF16)
        h3_ref[...] = hb
        o3 = jnp.zeros((T, D), F32)
        out = [None, None]
        for c in range(N_CHIPS):
            slot = c % 2
            if out[slot] is not None:
                for cp in out[slot]:
                    cp.wait()
            g = _mm(hb, wg_v[c])
            u = _mm(hb, wu_v[c])
            gst[slot] = g.astype(BF16)
            ust[slot] = u.astype(BF16)
            out[slot] = (pltpu.make_async_copy(gst.at[slot], g_hbm.at[c, i], sem.at[0, slot]),
                         pltpu.make_async_copy(ust.at[slot], u_hbm.at[c, i], sem.at[1, slot]))
            for cp in out[slot]:
                cp.start()
            o3 = o3 + _mm((g * _sigmoid(g) * u).astype(BF16), wd_v[c])
        for pair in out:
            for cp in pair:
                cp.wait()
        o3n, r3 = _rms(o3)
        diff = xv + o3n * gpost_ref[...] - t_ref[...]
        sq = jnp.sum(jnp.sum(diff * diff, axis=-1, keepdims=True), axis=0, keepdims=True)
        loss_ref[...] += sq * (0.5 / D)
        dx3 = diff * (1.0 / D)
        dx3_ref[...] = dx3
        d_o3, dg = _rms_bwd(dx3, o3n, r3, gpost_ref[...])
        dgpost_ref[...] += dg
        do3_ref[...] = d_o3.astype(BF16)

    tok = lambda w: pl.BlockSpec((T, w), lambda i: (i, 0))
    h3, gs, us, do3, dx3, loss, dgpost = pl.pallas_call(
        body,
        name="ffn_fwd",
        grid=(n,),
        in_specs=[tok(D), tok(D), _full((1, D)), _full((1, D)), ANY, ANY],
        out_specs=[tok(D), ANY, ANY, tok(D), tok(D), _full((1, 128)), _full((1, D))],
        out_shape=[jax.ShapeDtypeStruct((S, D), BF16), jax.ShapeDtypeStruct((N_CHIPS, n, T, FF_BLK), BF16),
                   jax.ShapeDtypeStruct((N_CHIPS, n, T, FF_BLK), BF16), jax.ShapeDtypeStruct((S, D), BF16),
                   jax.ShapeDtypeStruct((S, D), F32), jax.ShapeDtypeStruct((1, 128), F32),
                   jax.ShapeDtypeStruct((1, D), F32)],
        scratch_shapes=[pltpu.VMEM((N_CHIPS, D, FF_BLK), BF16), pltpu.VMEM((N_CHIPS, D, FF_BLK), BF16),
                        pltpu.VMEM((N_CHIPS, FF_BLK, D), BF16), pltpu.VMEM((2, T, FF_BLK), BF16),
                        pltpu.VMEM((2, T, FF_BLK), BF16), pltpu.SemaphoreType.DMA((2, 2))],
        compiler_params=_params(("arbitrary",)),
    )(x2, target, gpre, gpost, g2g, prg)
    return h3, gs.reshape(N_CHIPS, S, FF_BLK), us.reshape(N_CHIPS, S, FF_BLK), do3, dx3, loss, dgpost


def _ffn_bwd(h3, do3, gs, us, g2g, prg):
    S, D = h3.shape
    T = min(TILE_FFN, S)
    n = S // T
    NP = FFN_BWD_BLOCKS

    def body(h3_ref, do3_ref, g_ref, u_ref, g2g_hbm, prg_hbm, dh3_ref, gg2_hbm, gpr_hbm,
             wg_v, wu_v, wd_v, dwg_acc, dwu_acc, dwd_acc):
        jp = pl.program_id(0)
        i = pl.program_id(1)
        blocks = pl.ds(NP * jp, NP)

        @pl.when(i == 0)
        def _():
            pltpu.sync_copy(g2g_hbm.at[blocks, pl.ds(0, D), :], wg_v)
            pltpu.sync_copy(g2g_hbm.at[blocks, pl.ds(D, D), :], wu_v)
            pltpu.sync_copy(prg_hbm.at[blocks, pl.ds(PR_DOWN, FF_BLK), :], wd_v)
            dwg_acc[...] = jnp.zeros_like(dwg_acc)
            dwu_acc[...] = jnp.zeros_like(dwu_acc)
            dwd_acc[...] = jnp.zeros_like(dwd_acc)

        hb = h3_ref[...]
        d_o3 = do3_ref[...]
        dh = jnp.zeros((T, D), F32)
        for c in range(NP):
            g = g_ref[c].astype(F32)
            u = u_ref[c].astype(F32)
            sg = _sigmoid(g)
            sl = g * sg
            da = _mm_nt(d_o3, wd_v[c])
            dwd_acc[c] += _mm_tn((sl * u).astype(BF16), d_o3)
            dub = (da * sl).astype(BF16)
            dgb = (da * u * (sg * (1.0 + g * (1.0 - sg)))).astype(BF16)
            dwg_acc[c] += _mm_tn(hb, dgb)
            dwu_acc[c] += _mm_tn(hb, dub)
            dh = dh + _mm_nt(dgb, wg_v[c]) + _mm_nt(dub, wu_v[c])
        dh3_ref[...] = dh.astype(BF16)

        @pl.when(i == n - 1)
        def _():
            pltpu.sync_copy(dwg_acc, gg2_hbm.at[blocks, pl.ds(0, D), :])
            pltpu.sync_copy(dwu_acc, gg2_hbm.at[blocks, pl.ds(D, D), :])
            pltpu.sync_copy(dwd_acc, gpr_hbm.at[blocks, pl.ds(PR_DOWN, FF_BLK), :])

    tok = lambda w: pl.BlockSpec((T, w), lambda jp, i: (i, 0))
    blk = pl.BlockSpec((NP, T, FF_BLK), lambda jp, i: (jp, i, 0))
    return pl.pallas_call(
        body,
        name="ffn_bwd",
        grid=(N_CHIPS // NP, n),
        in_specs=[tok(D), tok(D), blk, blk, ANY, ANY],
        out_specs=[pl.BlockSpec((None, T, D), lambda jp, i: (jp, i, 0)), ANY, ANY],
        out_shape=[jax.ShapeDtypeStruct((N_CHIPS // NP, S, D), BF16),
                   jax.ShapeDtypeStruct((N_CHIPS, G2_ROWS, FF_BLK), F32),
                   jax.ShapeDtypeStruct((N_CHIPS, PR_ROWS, D), F32)],
        scratch_shapes=[pltpu.VMEM((NP, D, FF_BLK), BF16), pltpu.VMEM((NP, D, FF_BLK), BF16),
                        pltpu.VMEM((NP, FF_BLK, D), BF16), pltpu.VMEM((NP, D, FF_BLK), F32),
                        pltpu.VMEM((NP, D, FF_BLK), F32), pltpu.VMEM((NP, FF_BLK, D), F32)],
        compiler_params=_params(("arbitrary", "arbitrary")),
    )(h3, do3, gs, us, g2g, prg)


PACK_TAGS = ("rows", "ffn", "mix_in")
SMALL_W_ROWS = 48


def _my_place():
    x, y, c = lax.axis_index("x"), lax.axis_index("y"), lax.axis_index("c")
    return x, y, c, ((1 - x, y), (x, 1 - y), (1 - x, 1 - y))


def _remote(src, dst, send_sem, recv_sem, to):
    return pltpu.make_async_remote_copy(src_ref=src, dst_ref=dst, send_sem=send_sem, recv_sem=recv_sem,
                                        device_id=to, device_id_type=MESH)


def _gather_weights(packs):
    np_ = len(packs)
    split = [a.reshape(2, a.shape[0] // 2, a.shape[1]) for a in packs]

    def body(*refs):
        srcs, dsts = refs[:np_], refs[np_:2 * np_]
        send, recv, fsend, frecv, lsem = refs[2 * np_:]
        x, y, c, chips = _my_place()
        j = 2 * x + y
        local = [pltpu.make_async_copy(srcs[p], dsts[p].at[j], lsem.at[p]) for p in range(np_)]
        for cp in local:
            cp.start()
        sends = []
        for p in range(np_):
            for nn, (kx, ky) in enumerate(chips):
                cp = _remote(srcs[p].at[c], dsts[p].at[j, c], send.at[p, nn], recv.at[p, nn], (kx, ky, c))
                cp.start()
                sends.append(cp)
        fwds = []
        for nn, (kx, ky) in enumerate(chips):
            jk = 2 * kx + ky
            for p in range(np_):
                blk = dsts[p].at[jk, c]
                _remote(blk, blk, send.at[p, nn], recv.at[p, nn], (kx, ky, c)).wait_recv()
                cp = _remote(blk, blk, fsend.at[p, nn], frecv.at[p, nn], (x, y, 1 - c))
                cp.start()
                fwds.append(cp)
        for nn, (kx, ky) in enumerate(chips):
            jk = 2 * kx + ky
            for p in range(np_):
                blk = dsts[p].at[jk, 1 - c]
                _remote(blk, blk, fsend.at[p, nn], frecv.at[p, nn], (x, y, 1 - c)).wait_recv()
        for cp in sends + fwds:
            cp.wait_send()
        for cp in local:
            cp.wait()

    outs = pl.pallas_call(
        body,
        name="gather_weights",
        in_specs=[ANY] * np_,
        out_specs=[ANY] * np_,
        out_shape=[jax.ShapeDtypeStruct((N_CHIPS,) + a.shape, a.dtype) for a in split],
        scratch_shapes=[pltpu.SemaphoreType.DMA((np_, 3))] * 4 + [pltpu.SemaphoreType.DMA((np_,))],
    )(*split)
    return [o.reshape((N_CHIPS,) + a.shape) for o, a in zip(outs, packs)]


def _pair_exchange(packs, small):
    np_ = len(packs)
    halves = [a.shape[1] // 2 for a in packs]
    split = [a.reshape(N_CHIPS, 2, h, a.shape[2]) for a, h in zip(packs, halves)]

    def body(*refs):
        srcs, small_ref = refs[:np_], refs[np_]
        dsts, small_all = refs[np_ + 1:2 * np_ + 1], refs[2 * np_ + 1]
        send, recv, ssend, srecv, lsem = refs[2 * np_ + 2:]
        x, y, c, _ = _my_place()
        me = 4 * x + 2 * y + c
        cps = []
        for p in range(np_):
            cp = _remote(srcs[p].at[:, 1 - c], dsts[p], send.at[p], recv.at[p], (x, y, 1 - c))
            cp.start()
            cps.append(cp)
        loc = pltpu.make_async_copy(small_ref, small_all.at[me], lsem)
        loc.start()
        scps = []
        for d in range(1, 8):
            flip = lambda v, bit: 1 - v if bit else v
            peer = (flip(x, d & 4), flip(y, d & 2), flip(c, d & 1))
            cp = _remote(small_ref, small_all.at[me], ssend.at[d - 1], srecv.at[d - 1], peer)
            cp.start()
            scps.append((cp, 4 * peer[0] + 2 * peer[1] + peer[2], peer))
        for cp in cps:
            cp.wait()
        for d, (cp, slot, peer) in enumerate(scps):
            _remote(small_ref, small_all.at[slot], ssend.at[d], srecv.at[d], peer).wait_recv()
            cp.wait_send()
        loc.wait()

    outs = pl.pallas_call(
        body,
        name="pair_exchange",
        in_specs=[ANY] * (np_ + 1),
        out_specs=[ANY] * (np_ + 1),
        out_shape=[jax.ShapeDtypeStruct((N_CHIPS, h, a.shape[2]), F32) for a, h in zip(packs, halves)]
        + [jax.ShapeDtypeStruct((2 * N_CHIPS,) + small.shape, F32)],
        scratch_shapes=[pltpu.SemaphoreType.DMA((np_,)), pltpu.SemaphoreType.DMA((np_,)),
                        pltpu.SemaphoreType.DMA((7,)), pltpu.SemaphoreType.DMA((7,)), pltpu.SemaphoreType.DMA],
    )(*split, small)
    return outs[:np_], outs[np_]


def _pair_sum(pack, got, cidx, tag):
    _, rows, w = pack.shape
    h = rows // 2

    def body(c_ref, p_ref, x_ref, o_ref):
        del c_ref
        o_ref[...] = (p_ref[...] + x_ref[...]).astype(BF16)

    return pl.pallas_call(
        body,
        name="pair_sum_" + tag,
        grid_spec=pltpu.PrefetchScalarGridSpec(
            num_scalar_prefetch=1,
            grid=(N_CHIPS,),
            in_specs=[pl.BlockSpec((None, None, h, w), lambda k, c: (k, c[0], 0, 0)),
                      pl.BlockSpec((None, h, w), lambda k, c: (k, 0, 0))],
            out_specs=pl.BlockSpec((None, h, w), lambda k, c: (k, 0, 0)),
        ),
        out_shape=jax.ShapeDtypeStruct((N_CHIPS, h, w), BF16),
        compiler_params=_params(("arbitrary",)),
    )(cidx, pack.reshape(N_CHIPS, 2, h, w), got)


def _chip_exchange(packs):
    np_ = len(packs)

    def body(*refs):
        srcs, dsts = refs[:np_], refs[np_:2 * np_]
        send, recv, lsem = refs[2 * np_:]
        x, y, c, chips = _my_place()
        j = 2 * x + y
        local = [pltpu.make_async_copy(srcs[p].at[j], dsts[p].at[j], lsem.at[p]) for p in range(np_)]
        for cp in local:
            cp.start()
        sends = []
        for p in range(np_):
            for nn, (kx, ky) in enumerate(chips):
                cp = _remote(srcs[p].at[2 * kx + ky], dsts[p].at[j], send.at[p, nn], recv.at[p, nn], (kx, ky, c))
                cp.start()
                sends.append(cp)
        for p in range(np_):
            for nn, (kx, ky) in enumerate(chips):
                blk = dsts[p].at[2 * kx + ky]
                _remote(blk, blk, send.at[p, nn], recv.at[p, nn], (kx, ky, c)).wait_recv()
        for cp in sends:
            cp.wait_send()
        for cp in local:
            cp.wait()

    return pl.pallas_call(
        body,
        name="chip_exchange",
        in_specs=[ANY] * np_,
        out_specs=[ANY] * np_,
        out_shape=[jax.ShapeDtypeStruct(a.shape, a.dtype) for a in packs],
        scratch_shapes=[pltpu.SemaphoreType.DMA((np_, 3)), pltpu.SemaphoreType.DMA((np_, 3)),
                        pltpu.SemaphoreType.DMA((np_,))],
    )(*packs)


def _sum_slots(parts, steps, tag):
    nslot, rows, w = parts.shape
    tr = rows // steps

    def body(p_ref, o_ref):
        acc = p_ref[0].astype(F32)
        for k in range(1, nslot):
            acc = acc + p_ref[k].astype(F32)
        o_ref[...] = acc

    return pl.pallas_call(
        body,
        name="sum_slots_" + tag,
        grid=(steps,),
        in_specs=[pl.BlockSpec((nslot, tr, w), lambda i: (0, i, 0))],
        out_specs=pl.BlockSpec((tr, w), lambda i: (i, 0)),
        out_shape=jax.ShapeDtypeStruct((rows, w), F32),
        compiler_params=_params(("arbitrary",)),
    )(parts)


def _pair_gather(halves):
    np_ = len(halves)

    def body(*refs):
        srcs, dsts = refs[:np_], refs[np_:2 * np_]
        send, recv, lsem = refs[2 * np_:]
        x, y, c, _ = _my_place()
        cps = []
        for p in range(np_):
            loc = pltpu.make_async_copy(srcs[p], dsts[p].at[c], lsem.at[p])
            loc.start()
            cp = _remote(srcs[p], dsts[p].at[c], send.at[p], recv.at[p], (x, y, 1 - c))
            cp.start()
            cps.append((loc, cp))
        for p, (loc, cp) in enumerate(cps):
            other = dsts[p].at[1 - c]
            _remote(other, other, send.at[p], recv.at[p], (x, y, 1 - c)).wait_recv()
            cp.wait_send()
            loc.wait()

    outs = pl.pallas_call(
        body,
        name="pair_gather",
        in_specs=[ANY] * np_,
        out_specs=[ANY] * np_,
        out_shape=[jax.ShapeDtypeStruct((2,) + a.shape, F32) for a in halves],
        scratch_shapes=[pltpu.SemaphoreType.DMA((np_,))] * 3,
    )(*halves)
    return [o.reshape(2 * a.shape[0], a.shape[1]) for o, a in zip(outs, halves)]


def _adamw(gsrc, row0, w, m, v, tr, tag):
    rows, width = w.shape
    off = row0 // tr
    bc1 = 1.0 - ADAM_B1 ** ADAM_STEP
    bc2 = 1.0 - ADAM_B2 ** ADAM_STEP

    def body(g_ref, w_ref, m_ref, v_ref, go_ref, d_ref, mo_ref, vo_ref):
        g = g_ref[...]
        m2 = ADAM_B1 * m_ref[...] + (1.0 - ADAM_B1) * g
        v2 = ADAM_B2 * v_ref[...] + (1.0 - ADAM_B2) * (g * g)
        go_ref[...] = g
        mo_ref[...] = m2
        vo_ref[...] = v2
        d_ref[...] = -ADAM_LR * ((m2 / bc1) / (jnp.sqrt(v2 / bc2) + ADAM_EPS) + ADAM_WD * w_ref[...])

    here = pl.BlockSpec((tr, width), lambda i: (i, 0))
    return pl.pallas_call(
        body,
        name="adamw_" + tag,
        grid=(rows // tr,),
        in_specs=[pl.BlockSpec((tr, width), lambda i: (off + i, 0)), here, here, here],
        out_specs=[here] * 4,
        out_shape=[jax.ShapeDtypeStruct((rows, width), F32)] * 4,
        compiler_params=_params(("arbitrary",)),
    )(gsrc, w, m, v)


def _flat_rows(a):
    return a.reshape(-1, 128)


def kernel(x, mem, mix_pre_g, w_mix_in, conv_a_w, conv_b_w, conv_b_b, ln_b_g, ln_b_b, w_mix_out, mix_post_g, xa_pre_g, mem_norm_g, w_q, w_k, w_v, w_o, xa_post_g, ffn_pre_g, w_gate, w_up, w_down, ffn_post_g, loss_target, m_mix_pre_g, m_w_mix_in, m_conv_a_w, m_conv_b_w, m_conv_b_b, m_ln_b_g, m_ln_b_b, m_w_mix_out, m_mix_post_g, m_xa_pre_g, m_mem_norm_g, m_w_q, m_w_k, m_w_v, m_w_o, m_xa_post_g, m_ffn_pre_g, m_w_gate, m_w_up, m_w_down, m_ffn_post_g, v_mix_pre_g, v_w_mix_in, v_conv_a_w, v_conv_b_w, v_conv_b_b, v_ln_b_g, v_ln_b_b, v_w_mix_out, v_mix_post_g, v_xa_pre_g, v_mem_norm_g, v_w_q, v_w_k, v_w_v, v_w_o, v_xa_post_g, v_ffn_pre_g, v_w_gate, v_w_up, v_w_down, v_ffn_post_g):
    given = dict(locals())
    names = ["mix_pre_g", "w_mix_in", "conv_a_w", "conv_b_w", "conv_b_b", "ln_b_g", "ln_b_b", "w_mix_out",
             "mix_post_g", "xa_pre_g", "mem_norm_g", "w_q", "w_k", "w_v", "w_o", "xa_post_g", "ffn_pre_g",
             "w_gate", "w_up", "w_down", "ffn_post_g"]
    row = lambda a: a.reshape(1, -1)
    cx, cy, cc = lax.axis_index("x"), lax.axis_index("y"), lax.axis_index("c")
    chip = 2 * cx + cy
    ca_blk = conv_a_w.shape[1]

    pr = jnp.concatenate([w_mix_out, w_q, w_k, w_v, w_o, w_down], axis=0).astype(BF16)
    g2 = jnp.concatenate([w_gate, w_up], axis=0).astype(BF16)
    g1 = w_mix_in.astype(BF16)
    conv_rows = CONV_A_W + CONV_B_W
    sw = jnp.concatenate([conv_a_w, conv_b_w, jnp.zeros((SMALL_W_ROWS - conv_rows, ca_blk), F32)], axis=0)
    prg, g2g, g1g, swg = _gather_weights([pr, g2, g1, sw])
    conv_full = jnp.transpose(swg[:, :conv_rows, :], (1, 0, 2)).reshape(conv_rows, N_CHIPS * ca_blk)
    wa, wb = conv_full[:CONV_A_W], conv_full[CONV_A_W:]

    xs, ms, tgt = x[0], mem[0], loss_target[0]
    x1, u, o1, z1 = _mix_fwd(xs, row(mix_pre_g), row(mix_post_g), wa, wb, row(conv_b_b), row(ln_b_g), row(ln_b_b),
                             g1g, prg)
    memn, kb, vb = _mem_kv(ms, row(mem_norm_g), prg)
    x2, q, o2 = _xattn_fwd(x1, row(xa_pre_g), row(xa_post_g), kb, vb, prg)
    h3, gs, us, do3, dx3, loss_part, d_ffn_post = _ffn_fwd(x2, tgt, row(ffn_pre_g), row(ffn_post_g), g2g, prg)

    dh3p, gg2, gpr = _ffn_bwd(h3, do3, gs, us, g2g, prg)
    dx1, d_ffn_pre, d_xa_post, d_xa_pre, dk, dv, gpr = _xattn_bwd(
        dx3, dh3p, x2, x1, o2, q, kb, vb, row(ffn_pre_g), row(xa_post_g), row(xa_pre_g), prg, gpr)
    d_mem_g, gpr = _mem_bwd(dk, dv, ms, memn, row(mem_norm_g), prg, gpr)
    dx, d_mix_pre, d_mix_post, dwa, dwb, dbb, dlng, dlnb, gg1, gpr = _mix_bwd(
        dx1, xs, o1, u, z1, row(mix_pre_g), row(mix_post_g), wa, wb, row(ln_b_g), row(ln_b_b), g1g, prg, gpr)

    small_parts = [d_mix_pre, dwa, dwb, dbb, dlng, dlnb, d_mix_post, d_xa_pre, d_mem_g, d_xa_post, d_ffn_pre,
                   d_ffn_post, loss_part]
    sizes = [p.size for p in small_parts]
    small = jnp.concatenate([p.reshape(-1) for p in small_parts])
    small_rows = -(-small.size // (8 * 128)) * 8
    small = jnp.pad(small, (0, small_rows * 128 - small.size)).reshape(small_rows, 128)
    got, small_all = _pair_exchange([gpr, gg2, gg1], small)
    cidx = cc.astype(jnp.int32).reshape(1)
    sums = [_pair_sum(p, g, cidx, t) for p, g, t in zip([gpr, gg2, gg1], got, PACK_TAGS)]
    parts = _chip_exchange(sums)
    halves = [_sum_slots(p, 2, t) for p, t in zip(parts, PACK_TAGS)]
    rpr, rg2, rg1 = _pair_gather(halves)
    small_sum = _sum_slots(small_all, 1, "small").reshape(-1)
    red, pos = [], 0
    for p, sz in zip(small_parts, sizes):
        red.append(small_sum[pos:pos + sz].reshape(p.shape))
        pos += sz
    (r_mix_pre, r_wa, r_wb, r_bb, r_lng, r_lnb, r_mix_post, r_xa_pre, r_mem_g, r_xa_post, r_ffn_pre, r_ffn_post,
     r_loss) = red
    loss = r_loss[0, 0]

    res = {}
    big = {"w_mix_out": (rpr, PR_OUT, 256), "w_q": (rpr, PR_Q, 256), "w_k": (rpr, PR_K, 256),
           "w_v": (rpr, PR_V, 256), "w_o": (rpr, PR_O, 256), "w_down": (rpr, PR_DOWN, 64),
           "w_gate": (rg2, 0, 256), "w_up": (rg2, D_MODEL, 256), "w_mix_in": (rg1, 0, 256)}
    for nm, (src, row0, tr) in big.items():
        res[nm] = _adamw(src, row0, given[nm], given["m_" + nm], given["v_" + nm], tr, nm)
    small_grads = {"mix_pre_g": r_mix_pre, "conv_b_b": r_bb, "ln_b_g": r_lng, "ln_b_b": r_lnb,
                   "mix_post_g": r_mix_post, "xa_pre_g": r_xa_pre, "mem_norm_g": r_mem_g, "xa_post_g": r_xa_post,
                   "ffn_pre_g": r_ffn_pre, "ffn_post_g": r_ffn_post,
                   "conv_a_w": lax.dynamic_slice_in_dim(r_wa, chip * ca_blk, ca_blk, axis=1),
                   "conv_b_w": lax.dynamic_slice_in_dim(r_wb, chip * ca_blk, ca_blk, axis=1)}
    small_names = list(small_grads)

    def packed(prefix, grads=None):
        flat = jnp.concatenate([(grads[nm] if grads else given[prefix + nm]).reshape(-1) for nm in small_names])
        rows8 = -(-flat.size // (8 * 128)) * 8
        return jnp.pad(flat, (0, rows8 * 128 - flat.size)).reshape(rows8, 128)

    gp = packed("", small_grads)
    outs = _adamw(gp, 0, packed(""), packed("m_"), packed("v_"), gp.shape[0], "small")
    pos = 0
    for nm in small_names:
        shape = given[nm].shape
        sz = given[nm].size
        res[nm] = [o.reshape(-1)[pos:pos + sz].reshape(shape) for o in outs]
        pos += sz

    return (loss, dx[None], *[res[nm][0] for nm in names], *[res[nm][1] for nm in names],
            *[res[nm][2] for nm in names], *[res[nm][3] for nm in names])
```

```python
import functools

import jax
import jax.numpy as jnp
from jax import lax
from jax.experimental import pallas as pl
from jax.experimental.pallas import tpu as pltpu

F32 = jnp.float32
BF16 = jnp.bfloat16
MESH = pl.DeviceIdType.MESH

RMS_EPS = 1e-6
LN_EPS = 1e-5
D_MODEL = 1024
D_A = 512
D_B = 512
D_IN_ALL = 3 * D_A + 2 * D_B
CONV_A_W = 3
CONV_B_W = 31
HALO = 32
XA_HEADS = 4
HEAD_DIM = 256
D_FF = 2816
N_CHIPS = 4
FF_BLK = D_FF // N_CHIPS
IN_BLK = D_IN_ALL // N_CHIPS
ROW_BLK = D_MODEL // N_CHIPS

ADAM_LR = 0.001
ADAM_B1 = 0.9
ADAM_B2 = 0.999
ADAM_EPS = 1e-08
ADAM_WD = 0.01
ADAM_STEP = 10

TILE_FWD = 512
TILE_FFN = 512
TILE_BWD = 256
FFN_BWD_BLOCKS = 1
CONV_ROWS = 64
V7X_VMEM_LIMIT = 56 * 1024 * 1024

PR_OUT, PR_Q, PR_K, PR_V, PR_O, PR_DOWN = 0, 256, 512, 768, 1024, 1280
PR_ROWS = PR_DOWN + FF_BLK
G2_ROWS = 2 * D_MODEL
G1_ROWS = D_MODEL

ANY = pl.BlockSpec(memory_space=pl.ANY)


def _mm(a, b):
    return lax.dot_general(a, b, (((1,), (0,)), ((), ())), preferred_element_type=F32)


def _mm_nt(a, b):
    return lax.dot_general(a, b, (((1,), (1,)), ((), ())), preferred_element_type=F32)


def _mm_tn(a, b):
    return lax.dot_general(a, b, (((0,), (0,)), ((), ())), preferred_element_type=F32)


def _sigmoid(x):
    return 1.0 / (1.0 + jnp.exp(-x))


def _rms(x):
    r = lax.rsqrt(jnp.mean(x * x, axis=-1, keepdims=True) + RMS_EPS)
    return x * r, r


def _rms_bwd(dy, xn, r, g):
    gdy = dy * g
    dx = r * (gdy - xn * jnp.mean(gdy * xn, axis=-1, keepdims=True))
    return dx, jnp.sum(dy * xn, axis=0, keepdims=True)


def _full(shape):
    return pl.BlockSpec(shape, lambda *_: (0,) * len(shape))


def _params(sem=None):
    return pltpu.CompilerParams(dimension_semantics=sem, vmem_limit_bytes=V7X_VMEM_LIMIT)


def _load_rows(dst, src_hbm, row0, rows):
    for k in range(N_CHIPS):
        pltpu.sync_copy(src_hbm.at[k, pl.ds(row0, rows), :], dst.at[pl.ds(rows * k, rows), :])


def _load_cols(dst, src_hbm, row0, rows, cols):
    for k in range(N_CHIPS):
        pltpu.sync_copy(src_hbm.at[k, pl.ds(row0, rows), :], dst.at[:, pl.ds(cols * k, cols)])


def _fill_phases(src, sh, nrows):
    for r in range(1, 8):
        sh[r, pl.ds(0, nrows), :] = src[pl.ds(r, nrows), pl.ds(D_A, D_B)]


def _phase_rows(src, sh, off, start, size):
    r = off % 8
    if r == 0:
        return src[pl.ds(off + start, size), pl.ds(D_A, D_B)]
    return sh[r, pl.ds(off - r + start, size), :]


def _mix_fwd(x, gpre, gpost, wa, wb, bb, lng, lnb, g1g, prg):
    S, D = x.shape
    T = min(TILE_FWD, S)
    n = S // T

    def body(x_ref, gpre_ref, gpost_ref, wa_ref, wb_ref, bb_ref, lng_ref, lnb_ref, g1g_hbm, prg_hbm,
             x1_ref, u_ref, o1_ref, z1_ref, win_v, wout_v, ext, sh, z1buf):
        i = pl.program_id(0)

        @pl.when(i == 0)
        def _():
            _load_cols(win_v, g1g_hbm, 0, D, IN_BLK)
            _load_rows(wout_v, prg_hbm, PR_OUT, ROW_BLK)
            ext[pl.ds(0, HALO), :] = jnp.zeros((HALO, D_A + D_B), F32)

        xv = x_ref[...]
        xn, _ = _rms(xv)
        h = (xn * gpre_ref[...]).astype(BF16)
        u = _mm(h, win_v[...])
        u_ref[...] = u.astype(BF16)
        b_a = u[:, 0:D_A]
        cv = u[:, D_A:2 * D_A] * u[:, 2 * D_A:3 * D_A]
        z0 = u[:, 3 * D_A:3 * D_A + D_B] * _sigmoid(u[:, 3 * D_A + D_B:])
        ext[pl.ds(HALO, T), pl.ds(0, D_A)] = cv
        ext[pl.ds(HALO, T), pl.ds(D_A, D_B)] = z0

        conv_a = ext[pl.ds(HALO - 2, T), pl.ds(0, D_A)] * wa_ref[0:1, :]
        for k in range(1, CONV_A_W):
            conv_a = conv_a + ext[pl.ds(HALO - 2 + k, T), pl.ds(0, D_A)] * wa_ref[k:k + 1, :]
        y_a = b_a * conv_a

        _fill_phases(ext, sh, T + HALO - 8)
        base = HALO - (CONV_B_W - 1)

        def chunk(ci, carry):
            start = pl.multiple_of(ci * CONV_ROWS, 8)
            acc = jnp.broadcast_to(bb_ref[...], (CONV_ROWS, D_B))
            for k in range(CONV_B_W):
                acc = acc + _phase_rows(ext, sh, base + k, start, CONV_ROWS) * wb_ref[k:k + 1, :]
            z1buf[pl.ds(start, CONV_ROWS), :] = acc
            return carry

        lax.fori_loop(0, T // CONV_ROWS, chunk, 0)
        z1 = z1buf[...]
        z1_ref[...] = z1.astype(BF16)
        mu = jnp.mean(z1, axis=-1, keepdims=True)
        zc = z1 - mu
        rstd = lax.rsqrt(jnp.mean(zc * zc, axis=-1, keepdims=True) + LN_EPS)
        l = zc * rstd * lng_ref[...] + lnb_ref[...]
        y_b = l * _sigmoid(l)
        y = jnp.concatenate([y_a, y_b], axis=-1).astype(BF16)
        o1 = _mm(y, wout_v[...])
        o1_ref[...] = o1.astype(BF16)
        o1n, _ = _rms(o1)
        x1_ref[...] = xv + o1n * gpost_ref[...]
        ext[pl.ds(0, HALO), :] = ext[pl.ds(T, HALO), :]

    tok = lambda w: pl.BlockSpec((T, w), lambda i: (i, 0))
    return pl.pallas_call(
        body,
        name="mix_fwd",
        grid=(n,),
        in_specs=[tok(D), _full((1, D)), _full((1, D)), _full((CONV_A_W, D_A)), _full((CONV_B_W, D_B)),
                  _full((1, D_B)), _full((1, D_B)), _full((1, D_B)), ANY, ANY],
        out_specs=[tok(D), tok(D_IN_ALL), tok(D), tok(D_B)],
        out_shape=[jax.ShapeDtypeStruct((S, D), F32), jax.ShapeDtypeStruct((S, D_IN_ALL), BF16),
                   jax.ShapeDtypeStruct((S, D), BF16), jax.ShapeDtypeStruct((S, D_B), BF16)],
        scratch_shapes=[pltpu.VMEM((D, D_IN_ALL), BF16), pltpu.VMEM((D_A + D_B, D), BF16),
                        pltpu.VMEM((HALO + T, D_A + D_B), F32), pltpu.VMEM((8, HALO + T, D_B), F32),
                        pltpu.VMEM((T, D_B), F32)],
        compiler_params=_params(("arbitrary",)),
    )(x, gpre, gpost, wa, wb, bb, lng, lnb, g1g, prg)


def _mix_bwd(dx1, x, o1, u, z1s, gpre, gpost, wa, wb, lng, lnb, g1g, prg, gpr_in):
    S, D = x.shape
    T = min(TILE_BWD, S)
    n = S // T
    hb = T // HALO

    def body(dx1_ref, x_ref, o1_ref, u_ref, uh_ref, z1_ref, gpre_ref, gpost_ref, wa_ref, wb_ref, lng_ref, lnb_ref,
             g1g_hbm, prg_hbm, gpr_hbm,
             dx_ref, dgpre_ref, dgpost_ref, dwa_ref, dwb_ref, dbb_ref, dlng_ref, dlnb_ref, gg1_hbm, gpr_out,
             win_v, wout_v, dwin_acc, dwout_acc, ext, ext2, shf, shb, dz0buf, dwb_acc):
        del gpr_hbm
        i = pl.program_id(0)

        @pl.when(i == 0)
        def _():
            _load_cols(win_v, g1g_hbm, 0, D, IN_BLK)
            _load_rows(wout_v, prg_hbm, PR_OUT, ROW_BLK)
            dwin_acc[...] = jnp.zeros_like(dwin_acc)
            dwout_acc[...] = jnp.zeros_like(dwout_acc)
            dwb_acc[...] = jnp.zeros_like(dwb_acc)
            ext2[pl.ds(T, HALO), :] = jnp.zeros((HALO, D_A + D_B), F32)
            for ref in (dgpre_ref, dgpost_ref, dwa_ref, dbb_ref, dlng_ref, dlnb_ref):
                ref[...] = jnp.zeros_like(ref)

        o1n, r1 = _rms(o1_ref[...].astype(F32))
        dx1v = dx1_ref[...]
        d_o1, dgp = _rms_bwd(dx1v, o1n, r1, gpost_ref[...])
        dgpost_ref[...] += dgp
        d_o1b = d_o1.astype(BF16)
        dy = _mm_nt(d_o1b, wout_v[...])

        first = (i == n - 1).astype(F32)
        uh = uh_ref[...].astype(F32) * (1.0 - first)
        ext[pl.ds(0, HALO), pl.ds(0, D_A)] = uh[:, D_A:2 * D_A] * uh[:, 2 * D_A:3 * D_A]
        ext[pl.ds(0, HALO), pl.ds(D_A, D_B)] = uh[:, 3 * D_A:3 * D_A + D_B] * _sigmoid(uh[:, 3 * D_A + D_B:])
        uf = u_ref[...].astype(F32)
        b_a = uf[:, 0:D_A]
        c_a = uf[:, D_A:2 * D_A]
        v_a = uf[:, 2 * D_A:3 * D_A]
        gv = uf[:, 3 * D_A:3 * D_A + D_B]
        sg = _sigmoid(uf[:, 3 * D_A + D_B:])
        ext[pl.ds(HALO, T), pl.ds(0, D_A)] = c_a * v_a
        ext[pl.ds(HALO, T), pl.ds(D_A, D_B)] = gv * sg
        conv_a = ext[pl.ds(HALO - 2, T), pl.ds(0, D_A)] * wa_ref[0:1, :]
        for k in range(1, CONV_A_W):
            conv_a = conv_a + ext[pl.ds(HALO - 2 + k, T), pl.ds(0, D_A)] * wa_ref[k:k + 1, :]
        z1 = z1_ref[...].astype(F32)
        mu = jnp.mean(z1, axis=-1, keepdims=True)
        zc = z1 - mu
        rstd = lax.rsqrt(jnp.mean(zc * zc, axis=-1, keepdims=True) + LN_EPS)
        zn = zc * rstd
        l = zn * lng_ref[...] + lnb_ref[...]
        sl = _sigmoid(l)
        y = jnp.concatenate([b_a * conv_a, l * sl], axis=-1).astype(BF16)
        dwout_acc[...] += _mm_tn(y, d_o1b)

        dy_a = dy[:, 0:D_A]
        dl = dy[:, D_A:] * (sl * (1.0 + l * (1.0 - sl)))
        dlng_ref[...] += jnp.sum(dl * zn, axis=0, keepdims=True)
        dlnb_ref[...] += jnp.sum(dl, axis=0, keepdims=True)
        dzn = dl * lng_ref[...]
        dz1 = rstd * (dzn - jnp.mean(dzn, axis=-1, keepdims=True) - zn * jnp.mean(dzn * zn, axis=-1, keepdims=True))
        dbb_ref[...] += jnp.sum(dz1, axis=0, keepdims=True)
        d_conv = dy_a * b_a
        ext2[pl.ds(0, T), pl.ds(0, D_A)] = d_conv
        ext2[pl.ds(0, T), pl.ds(D_A, D_B)] = dz1

        d_cv = ext2[pl.ds(CONV_A_W - 1, T), pl.ds(0, D_A)] * wa_ref[0:1, :]
        for k in range(1, CONV_A_W):
            d_cv = d_cv + ext2[pl.ds(CONV_A_W - 1 - k, T), pl.ds(0, D_A)] * wa_ref[k:k + 1, :]
        for k in range(CONV_A_W):
            dwa_ref[k:k + 1, :] += jnp.sum(d_conv * ext[pl.ds(HALO - 2 + k, T), pl.ds(0, D_A)], axis=0, keepdims=True)

        _fill_phases(ext, shf, T + HALO - 8)
        _fill_phases(ext2, shb, T + HALO - 8)
        base = HALO - (CONV_B_W - 1)

        def chunk(ci, carry):
            start = pl.multiple_of(ci * CONV_ROWS, 8)
            dzc = ext2[pl.ds(start, CONV_ROWS), pl.ds(D_A, D_B)]
            acc = jnp.zeros((CONV_ROWS, D_B), F32)
            for k in range(CONV_B_W):
                wk = wb_ref[k:k + 1, :]
                acc = acc + _phase_rows(ext2, shb, CONV_B_W - 1 - k, start, CONV_ROWS) * wk
                prod = dzc * _phase_rows(ext, shf, base + k, start, CONV_ROWS)
                part = prod[0:8, :]
                for m in range(1, CONV_ROWS // 8):
                    part = part + prod[8 * m:8 * m + 8, :]
                dwb_acc[k] += part
            dz0buf[pl.ds(start, CONV_ROWS), :] = acc
            return carry

        lax.fori_loop(0, T // CONV_ROWS, chunk, 0)
        dz0 = dz0buf[...]
        du = jnp.concatenate([dy_a * conv_a, d_cv * v_a, d_cv * c_a, dz0 * sg, dz0 * gv * sg * (1.0 - sg)],
                             axis=-1).astype(BF16)
        dh = _mm_nt(du, win_v[...])
        xv = x_ref[...]
        xn, r0 = _rms(xv)
        dwin_acc[...] += _mm_tn((xn * gpre_ref[...]).astype(BF16), du)
        dxp, dg0 = _rms_bwd(dh, xn, r0, gpre_ref[...])
        dgpre_ref[...] += dg0
        dx_ref[...] = dx1v + dxp
        ext2[pl.ds(T, HALO), :] = ext2[pl.ds(0, HALO), :]

        @pl.when(i == n - 1)
        def _():
            for k in range(CONV_B_W):
                dwb_ref[k:k + 1, :] = jnp.sum(dwb_acc[k], axis=0, keepdims=True)
            for k in range(N_CHIPS):
                pltpu.sync_copy(dwin_acc.at[:, pl.ds(IN_BLK * k, IN_BLK)], gg1_hbm.at[k])
                pltpu.sync_copy(dwout_acc.at[pl.ds(ROW_BLK * k, ROW_BLK), :], gpr_out.at[k, pl.ds(PR_OUT, ROW_BLK), :])

    rev = lambda w: pl.BlockSpec((T, w), lambda i: (n - 1 - i, 0))
    halo = pl.BlockSpec((HALO, D_IN_ALL), lambda i: (jnp.maximum((n - 1 - i) * hb - 1, 0), 0))
    outs = pl.pallas_call(
        body,
        name="mix_bwd",
        grid=(n,),
        in_specs=[rev(D), rev(D), rev(D), rev(D_IN_ALL), halo, rev(D_B), _full((1, D)), _full((1, D)),
                  _full((CONV_A_W, D_A)), _full((CONV_B_W, D_B)), _full((1, D_B)), _full((1, D_B)), ANY, ANY, ANY],
        out_specs=[rev(D), _full((1, D)), _full((1, D)), _full((CONV_A_W, D_A)), _full((CONV_B_W, D_B)),
                   _full((1, D_B)), _full((1, D_B)), _full((1, D_B)), ANY, ANY],
        out_shape=[jax.ShapeDtypeStruct((S, D), F32), jax.ShapeDtypeStruct((1, D), F32),
                   jax.ShapeDtypeStruct((1, D), F32), jax.ShapeDtypeStruct((CONV_A_W, D_A), F32),
                   jax.ShapeDtypeStruct((CONV_B_W, D_B), F32), jax.ShapeDtypeStruct((1, D_B), F32),
                   jax.ShapeDtypeStruct((1, D_B), F32), jax.ShapeDtypeStruct((1, D_B), F32),
                   jax.ShapeDtypeStruct((N_CHIPS, G1_ROWS, IN_BLK), F32),
                   jax.ShapeDtypeStruct(gpr_in.shape, F32)],
        input_output_aliases={14: 9},
        scratch_shapes=[pltpu.VMEM((D, D_IN_ALL), BF16), pltpu.VMEM((D_A + D_B, D), BF16),
                        pltpu.VMEM((D, D_IN_ALL), F32), pltpu.VMEM((D_A + D_B, D), F32),
                        pltpu.VMEM((HALO + T, D_A + D_B), F32), pltpu.VMEM((HALO + T, D_A + D_B), F32),
                        pltpu.VMEM((8, HALO + T, D_B), F32), pltpu.VMEM((8, HALO + T, D_B), F32),
                        pltpu.VMEM((T, D_B), F32), pltpu.VMEM((CONV_B_W, 8, D_B), F32)],
        compiler_params=_params(("arbitrary",)),
    )(dx1, x, o1, u, u, z1s, gpre, gpost, wa, wb, lng, lnb, g1g, prg, gpr_in)
    return outs


def _mem_kv(mem, gmem, prg):
    M, D = mem.shape

    def body(mem_ref, g_ref, prg_hbm, memn_ref, k_ref, v_ref, wk_v, wv_v):
        _load_rows(wk_v, prg_hbm, PR_K, ROW_BLK)
        _load_rows(wv_v, prg_hbm, PR_V, ROW_BLK)
        mn, _ = _rms(mem_ref[...])
        mb = (mn * g_ref[...]).astype(BF16)
        memn_ref[...] = mb
        k_ref[...] = _mm(mb, wk_v[...]).astype(BF16)
        v_ref[...] = _mm(mb, wv_v[...]).astype(BF16)

    return pl.pallas_call(
        body,
        name="mem_kv",
        grid=(1,),
        in_specs=[_full((M, D)), _full((1, D)), ANY],
        out_specs=[_full((M, D))] * 3,
        out_shape=[jax.ShapeDtypeStruct((M, D), BF16)] * 3,
        scratch_shapes=[pltpu.VMEM((D, D), BF16), pltpu.VMEM((D, D), BF16)],
        compiler_params=_params(("arbitrary",)),
    )(mem, gmem, prg)


def _attend(qb, kb, vb):
    scale = HEAD_DIM ** -0.5
    ps, os_ = [], []
    for hd in range(XA_HEADS):
        cols = slice(HEAD_DIM * hd, HEAD_DIM * (hd + 1))
        s = _mm_nt(qb[:, cols], kb[:, cols]) * scale
        e = jnp.exp(s - jnp.max(s, axis=-1, keepdims=True))
        p = e / jnp.sum(e, axis=-1, keepdims=True)
        ps.append(p)
        os_.append(_mm(p.astype(BF16), vb[:, cols]))
    return ps, jnp.concatenate(os_, axis=-1).astype(BF16)


def _xattn_fwd(x1, gpre, gpost, kb, vb, prg):
    S, D = x1.shape
    M = kb.shape[0]
    T = min(TILE_FWD, S)
    n = S // T

    def body(x1_ref, gpre_ref, gpost_ref, k_ref, v_ref, prg_hbm, x2_ref, q_ref, o2_ref, wq_v, wo_v):
        @pl.when(pl.program_id(0) == 0)
        def _():
            _load_rows(wq_v, prg_hbm, PR_Q, ROW_BLK)
            _load_rows(wo_v, prg_hbm, PR_O, ROW_BLK)

        xv = x1_ref[...]
        xn, _ = _rms(xv)
        qb = _mm((xn * gpre_ref[...]).astype(BF16), wq_v[...]).astype(BF16)
        q_ref[...] = qb
        _, ob = _attend(qb, k_ref[...], v_ref[...])
        o2 = _mm(ob, wo_v[...])
        o2_ref[...] = o2.astype(BF16)
        o2n, _ = _rms(o2)
        x2_ref[...] = xv + o2n * gpost_ref[...]

    tok = lambda w: pl.BlockSpec((T, w), lambda i: (i, 0))
    return pl.pallas_call(
        body,
        name="xattn_fwd",
        grid=(n,),
        in_specs=[tok(D), _full((1, D)), _full((1, D)), _full((M, D)), _full((M, D)), ANY],
        out_specs=[tok(D), tok(D), tok(D)],
        out_shape=[jax.ShapeDtypeStruct((S, D), F32), jax.ShapeDtypeStruct((S, D), BF16),
                   jax.ShapeDtypeStruct((S, D), BF16)],
        scratch_shapes=[pltpu.VMEM((D, D), BF16), pltpu.VMEM((D, D), BF16)],
        compiler_params=_params(("arbitrary",)),
    )(x1, gpre, gpost, kb, vb, prg)


def _xattn_bwd(dx3, dh3p, x2, x1, o2, q, kb, vb, gffn, gpost, gpre, prg, gpr_in):
    S, D = x1.shape
    M = kb.shape[0]
    T = min(TILE_BWD, S)
    n = S // T
    scale = HEAD_DIM ** -0.5

    nparts = dh3p.shape[0]

    def body(*refs):
        dx3_ref, dh3_refs = refs[0], refs[1:1 + nparts]
        (x2_ref, x1_ref, o2_ref, q_ref, k_ref, v_ref, gffn_ref, gpost_ref, gpre_ref, prg_hbm, _,
         dx1_ref, dgffn_ref, dgpost_ref, dgpre_ref, dk_ref, dv_ref, gpr_out,
         wq_v, wo_v, dwq_acc, dwo_acc) = refs[1 + nparts:]
        i = pl.program_id(0)

        @pl.when(i == 0)
        def _():
            _load_rows(wq_v, prg_hbm, PR_Q, ROW_BLK)
            _load_rows(wo_v, prg_hbm, PR_O, ROW_BLK)
            dwq_acc[...] = jnp.zeros_like(dwq_acc)
            dwo_acc[...] = jnp.zeros_like(dwo_acc)
            for ref in (dgffn_ref, dgpost_ref, dgpre_ref, dk_ref, dv_ref):
                ref[...] = jnp.zeros_like(ref)

        x2n, r2 = _rms(x2_ref[...])
        dh3 = dh3_refs[0][...].astype(F32)
        for ref in dh3_refs[1:]:
            dh3 = dh3 + ref[...].astype(F32)
        dxp, dg = _rms_bwd(dh3, x2n, r2, gffn_ref[...])
        dgffn_ref[...] += dg
        dx2 = dx3_ref[...] + dxp
        o2n, ro = _rms(o2_ref[...].astype(F32))
        d_o2, dg = _rms_bwd(dx2, o2n, ro, gpost_ref[...])
        dgpost_ref[...] += dg
        d_o2b = d_o2.astype(BF16)
        d_o = _mm_nt(d_o2b, wo_v[...]).astype(BF16)
        qb = q_ref[...]
        kv = k_ref[...]
        vv = v_ref[...]
        ps, ob = _attend(qb, kv, vv)
        dwo_acc[...] += _mm_tn(ob, d_o2b)
        dqs = []
        for hd in range(XA_HEADS):
            cols = slice(HEAD_DIM * hd, HEAD_DIM * (hd + 1))
            p = ps[hd]
            dp = _mm_nt(d_o[:, cols], vv[:, cols])
            dv_ref[:, cols] += _mm_tn(p.astype(BF16), d_o[:, cols])
            ds = (p * (dp - jnp.sum(p * dp, axis=-1, keepdims=True)) * scale).astype(BF16)
            dqs.append(_mm(ds, kv[:, cols]))
            dk_ref[:, cols] += _mm_tn(ds, qb[:, cols])
        dq = jnp.concatenate(dqs, axis=-1).astype(BF16)
        dh2 = _mm_nt(dq, wq_v[...])
        x1n, r1 = _rms(x1_ref[...])
        dwq_acc[...] += _mm_tn((x1n * gpre_ref[...]).astype(BF16), dq)
        dxp, dg = _rms_bwd(dh2, x1n, r1, gpre_ref[...])
        dgpre_ref[...] += dg
        dx1_ref[...] = dx2 + dxp

        @pl.when(i == n - 1)
        def _():
            for k in range(N_CHIPS):
                rows = pl.ds(ROW_BLK * k, ROW_BLK)
                pltpu.sync_copy(dwq_acc.at[rows, :], gpr_out.at[k, pl.ds(PR_Q, ROW_BLK), :])
                pltpu.sync_copy(dwo_acc.at[rows, :], gpr_out.at[k, pl.ds(PR_O, ROW_BLK), :])

    tok = lambda w: pl.BlockSpec((T, w), lambda i: (i, 0))
    part = lambda j: pl.BlockSpec((None, T, D), lambda i: (j, i, 0))
    return pl.pallas_call(
        body,
        name="xattn_bwd",
        grid=(n,),
        in_specs=[tok(D)] + [part(j) for j in range(nparts)] + [tok(D), tok(D), tok(D), tok(D), _full((M, D)),
                                                                 _full((M, D)), _full((1, D)), _full((1, D)),
                                                                 _full((1, D)), ANY, ANY],
        out_specs=[tok(D), _full((1, D)), _full((1, D)), _full((1, D)), _full((M, D)), _full((M, D)), ANY],
        out_shape=[jax.ShapeDtypeStruct((S, D), F32), jax.ShapeDtypeStruct((1, D), F32),
                   jax.ShapeDtypeStruct((1, D), F32), jax.ShapeDtypeStruct((1, D), F32),
                   jax.ShapeDtypeStruct((M, D), F32), jax.ShapeDtypeStruct((M, D), F32),
                   jax.ShapeDtypeStruct(gpr_in.shape, F32)],
        input_output_aliases={nparts + 11: 6},
        scratch_shapes=[pltpu.VMEM((D, D), BF16), pltpu.VMEM((D, D), BF16),
                        pltpu.VMEM((D, D), F32), pltpu.VMEM((D, D), F32)],
        compiler_params=_params(("arbitrary",)),
    )(dx3, *([dh3p] * nparts), x2, x1, o2, q, kb, vb, gffn, gpost, gpre, prg, gpr_in)


def _mem_bwd(dk, dv, mem, memn, gmem, prg, gpr_in):
    M, D = mem.shape

    def body(dk_ref, dv_ref, mem_ref, memn_ref, g_ref, prg_hbm, gpr_hbm, dg_ref, gpr_out, wk_v, wv_v, dwk_v, dwv_v):
        del gpr_hbm
        _load_rows(wk_v, prg_hbm, PR_K, ROW_BLK)
        _load_rows(wv_v, prg_hbm, PR_V, ROW_BLK)
        dkb = dk_ref[...].astype(BF16)
        dvb = dv_ref[...].astype(BF16)
        mb = memn_ref[...]
        dwk_v[...] = _mm_tn(mb, dkb)
        dwv_v[...] = _mm_tn(mb, dvb)
        dmn = _mm_nt(dkb, wk_v[...]) + _mm_nt(dvb, wv_v[...])
        mn, _ = _rms(mem_ref[...])
        dg_ref[...] = jnp.sum(dmn * mn, axis=0, keepdims=True)
        for k in range(N_CHIPS):
            rows = pl.ds(ROW_BLK * k, ROW_BLK)
            pltpu.sync_copy(dwk_v.at[rows, :], gpr_out.at[k, pl.ds(PR_K, ROW_BLK), :])
            pltpu.sync_copy(dwv_v.at[rows, :], gpr_out.at[k, pl.ds(PR_V, ROW_BLK), :])

    return pl.pallas_call(
        body,
        name="mem_bwd",
        grid=(1,),
        in_specs=[_full((M, D)), _full((M, D)), _full((M, D)), _full((M, D)), _full((1, D)), ANY, ANY],
        out_specs=[_full((1, D)), ANY],
        out_shape=[jax.ShapeDtypeStruct((1, D), F32), jax.ShapeDtypeStruct(gpr_in.shape, F32)],
        input_output_aliases={6: 1},
        scratch_shapes=[pltpu.VMEM((D, D), BF16), pltpu.VMEM((D, D), BF16),
                        pltpu.VMEM((D, D), F32), pltpu.VMEM((D, D), F32)],
        compiler_params=_params(("arbitrary",)),
    )(dk, dv, mem, memn, gmem, prg, gpr_in)


def _ffn_fwd(x2, target, gpre, gpost, g2g, prg):
    S, D = x2.shape
    T = min(TILE_FFN, S)
    n = S // T

    def body(x2_ref, t_ref, gpre_ref, gpost_ref, g2g_hbm, prg_hbm,
             h3_ref, g_hbm, u_hbm, do3_ref, dx3_ref, loss_ref, dgpost_ref, wg_v, wu_v, wd_v, gst, ust, sem):
        i = pl.program_id(0)

        @pl.when(i == 0)
        def _():
            pltpu.sync_copy(g2g_hbm.at[:, pl.ds(0, D), :], wg_v)
            pltpu.sync_copy(g2g_hbm.at[:, pl.ds(D, D), :], wu_v)
            pltpu.sync_copy(prg_hbm.at[:, pl.ds(PR_DOWN, FF_BLK), :], wd_v)
            loss_ref[...] = jnp.zeros_like(loss_ref)
            dgpost_ref[...] = jnp.zeros_like(dgpost_ref)

        xv = x2_ref[...]
        xn, _ = _rms(xv)
        hb = (xn * gpre_ref[...]).astype(BF16)
        h3_ref[...] = hb
        o3 = jnp.zeros((T, D), F32)
        out = [None, None]
        for c in range(N_CHIPS):
            slot = c % 2
            if out[slot] is not None:
                for cp in out[slot]:
                    cp.wait()
            g = _mm(hb, wg_v[c])
            u = _mm(hb, wu_v[c])
            gst[slot] = g.astype(BF16)
            ust[slot] = u.astype(BF16)
            out[slot] = (pltpu.make_async_copy(gst.at[slot], g_hbm.at[c, i], sem.at[0, slot]),
                         pltpu.make_async_copy(ust.at[slot], u_hbm.at[c, i], sem.at[1, slot]))
            for cp in out[slot]:
                cp.start()
            o3 = o3 + _mm((g * _sigmoid(g) * u).astype(BF16), wd_v[c])
        for pair in out:
            for cp in pair:
                cp.wait()
        o3n, r3 = _rms(o3)
        diff = xv + o3n * gpost_ref[...] - t_ref[...]
        sq = jnp.sum(jnp.sum(diff * diff, axis=-1, keepdims=True), axis=0, keepdims=True)
        loss_ref[...] += sq * (0.5 / D)
        dx3 = diff * (1.0 / D)
        dx3_ref[...] = dx3
        d_o3, dg = _rms_bwd(dx3, o3n, r3, gpost_ref[...])
        dgpost_ref[...] += dg
        do3_ref[...] = d_o3.astype(BF16)

    tok = lambda w: pl.BlockSpec((T, w), lambda i: (i, 0))
    h3, gs, us, do3, dx3, loss, dgpost = pl.pallas_call(
        body,
        name="ffn_fwd",
        grid=(n,),
        in_specs=[tok(D), tok(D), _full((1, D)), _full((1, D)), ANY, ANY],
        out_specs=[tok(D), ANY, ANY, tok(D), tok(D), _full((1, 128)), _full((1, D))],
        out_shape=[jax.ShapeDtypeStruct((S, D), BF16), jax.ShapeDtypeStruct((N_CHIPS, n, T, FF_BLK), BF16),
                   jax.ShapeDtypeStruct((N_CHIPS, n, T, FF_BLK), BF16), jax.ShapeDtypeStruct((S, D), BF16),
                   jax.ShapeDtypeStruct((S, D), F32), jax.ShapeDtypeStruct((1, 128), F32),
                   jax.ShapeDtypeStruct((1, D), F32)],
        scratch_shapes=[pltpu.VMEM((N_CHIPS, D, FF_BLK), BF16), pltpu.VMEM((N_CHIPS, D, FF_BLK), BF16),
                        pltpu.VMEM((N_CHIPS, FF_BLK, D), BF16), pltpu.VMEM((2, T, FF_BLK), BF16),
                        pltpu.VMEM((2, T, FF_BLK), BF16), pltpu.SemaphoreType.DMA((2, 2))],
        compiler_params=_params(("arbitrary",)),
    )(x2, target, gpre, gpost, g2g, prg)
    return h3, gs.reshape(N_CHIPS, S, FF_BLK), us.reshape(N_CHIPS, S, FF_BLK), do3, dx3, loss, dgpost


def _ffn_bwd(h3, do3, gs, us, g2g, prg):
    S, D = h3.shape
    T = min(TILE_FFN, S)
    n = S // T
    NP = FFN_BWD_BLOCKS

    def body(h3_ref, do3_ref, g_ref, u_ref, g2g_hbm, prg_hbm, dh3_ref, gg2_hbm, gpr_hbm,
             wg_v, wu_v, wd_v, dwg_acc, dwu_acc, dwd_acc):
        jp = pl.program_id(0)
        i = pl.program_id(1)
        blocks = pl.ds(NP * jp, NP)

        @pl.when(i == 0)
        def _():
            pltpu.sync_copy(g2g_hbm.at[blocks, pl.ds(0, D), :], wg_v)
            pltpu.sync_copy(g2g_hbm.at[blocks, pl.ds(D, D), :], wu_v)
            pltpu.sync_copy(prg_hbm.at[blocks, pl.ds(PR_DOWN, FF_BLK), :], wd_v)
            dwg_acc[...] = jnp.zeros_like(dwg_acc)
            dwu_acc[...] = jnp.zeros_like(dwu_acc)
            dwd_acc[...] = jnp.zeros_like(dwd_acc)

        hb = h3_ref[...]
        d_o3 = do3_ref[...]
        dh = jnp.zeros((T, D), F32)
        for c in range(NP):
            g = g_ref[c].astype(F32)
            u = u_ref[c].astype(F32)
            sg = _sigmoid(g)
            sl = g * sg
            da = _mm_nt(d_o3, wd_v[c])
            dwd_acc[c] += _mm_tn((sl * u).astype(BF16), d_o3)
            dub = (da * sl).astype(BF16)
            dgb = (da * u * (sg * (1.0 + g * (1.0 - sg)))).astype(BF16)
            dwg_acc[c] += _mm_tn(hb, dgb)
            dwu_acc[c] += _mm_tn(hb, dub)
            dh = dh + _mm_nt(dgb, wg_v[c]) + _mm_nt(dub, wu_v[c])
        dh3_ref[...] = dh.astype(BF16)

        @pl.when(i == n - 1)
        def _():
            pltpu.sync_copy(dwg_acc, gg2_hbm.at[blocks, pl.ds(0, D), :])
            pltpu.sync_copy(dwu_acc, gg2_hbm.at[blocks, pl.ds(D, D), :])
            pltpu.sync_copy(dwd_acc, gpr_hbm.at[blocks, pl.ds(PR_DOWN, FF_BLK), :])

    tok = lambda w: pl.BlockSpec((T, w), lambda jp, i: (i, 0))
    blk = pl.BlockSpec((NP, T, FF_BLK), lambda jp, i: (jp, i, 0))
    return pl.pallas_call(
        body,
        name="ffn_bwd",
        grid=(N_CHIPS // NP, n),
        in_specs=[tok(D), tok(D), blk, blk, ANY, ANY],
        out_specs=[pl.BlockSpec((None, T, D), lambda jp, i: (jp, i, 0)), ANY, ANY],
        out_shape=[jax.ShapeDtypeStruct((N_CHIPS // NP, S, D), BF16),
                   jax.ShapeDtypeStruct((N_CHIPS, G2_ROWS, FF_BLK), F32),
                   jax.ShapeDtypeStruct((N_CHIPS, PR_ROWS, D), F32)],
        scratch_shapes=[pltpu.VMEM((NP, D, FF_BLK), BF16), pltpu.VMEM((NP, D, FF_BLK), BF16),
                        pltpu.VMEM((NP, FF_BLK, D), BF16), pltpu.VMEM((NP, D, FF_BLK), F32),
                        pltpu.VMEM((NP, D, FF_BLK), F32), pltpu.VMEM((NP, FF_BLK, D), F32)],
        compiler_params=_params(("arbitrary", "arbitrary")),
    )(h3, do3, gs, us, g2g, prg)


PACK_TAGS = ("rows", "ffn", "mix_in")
SMALL_W_ROWS = 48


def _my_place():
    x, y, c = lax.axis_index("x"), lax.axis_index("y"), lax.axis_index("c")
    return x, y, c, ((1 - x, y), (x, 1 - y), (1 - x, 1 - y))


def _remote(src, dst, send_sem, recv_sem, to):
    return pltpu.make_async_remote_copy(src_ref=src, dst_ref=dst, send_sem=send_sem, recv_sem=recv_sem,
                                        device_id=to, device_id_type=MESH)


def _gather_weights(packs):
    np_ = len(packs)
    split = [a.reshape(2, a.shape[0] // 2, a.shape[1]) for a in packs]

    def body(*refs):
        srcs, dsts = refs[:np_], refs[np_:2 * np_]
        send, recv, fsend, frecv, osend, orecv = refs[2 * np_:]
        x, y, c, chips = _my_place()
        j = 2 * x + y
        own = [_remote(srcs[p], dsts[p].at[j], osend.at[p], orecv.at[p], (x, y, 1 - c)) for p in range(np_)]
        for cp in own:
            cp.start()
        sends = []
        for p in range(np_):
            for nn, (kx, ky) in enumerate(chips):
                cp = _remote(srcs[p].at[c], dsts[p].at[j, c], send.at[p, nn], recv.at[p, nn], (kx, ky, c))
                cp.start()
                sends.append(cp)
        fwds = []
        for nn, (kx, ky) in enumerate(chips):
            jk = 2 * kx + ky
            for p in range(np_):
                blk = dsts[p].at[jk, c]
                _remote(blk, blk, send.at[p, nn], recv.at[p, nn], (kx, ky, c)).wait_recv()
                cp = _remote(blk, blk, fsend.at[p, nn], frecv.at[p, nn], (x, y, 1 - c))
                cp.start()
                fwds.append(cp)
        for nn, (kx, ky) in enumerate(chips):
            jk = 2 * kx + ky
            for p in range(np_):
                blk = dsts[p].at[jk, 1 - c]
                _remote(blk, blk, fsend.at[p, nn], frecv.at[p, nn], (x, y, 1 - c)).wait_recv()
        for cp in sends + fwds:
            cp.wait_send()
        for cp in own:
            cp.wait()

    outs = pl.pallas_call(
        body,
        name="gather_weights",
        in_specs=[ANY] * np_,
        out_specs=[ANY] * np_,
        out_shape=[jax.ShapeDtypeStruct((N_CHIPS,) + a.shape, a.dtype) for a in split],
        scratch_shapes=[pltpu.SemaphoreType.DMA((np_, 3))] * 4 + [pltpu.SemaphoreType.DMA((np_,))] * 2,
    )(*split)
    return [o.reshape((N_CHIPS,) + a.shape) for o, a in zip(outs, packs)]


def _pair_exchange(packs, small):
    np_ = len(packs)
    halves = [a.shape[1] // 2 for a in packs]
    split = [a.reshape(N_CHIPS, 2, h, a.shape[2]) for a, h in zip(packs, halves)]

    def body(*refs):
        srcs, small_ref = refs[:np_], refs[np_]
        dsts, small_all = refs[np_ + 1:2 * np_ + 1], refs[2 * np_ + 1]
        send, recv, ssend, srecv, lsem = refs[2 * np_ + 2:]
        x, y, c, _ = _my_place()
        me = 4 * x + 2 * y + c
        cps = []
        for p in range(np_):
            cp = _remote(srcs[p].at[:, 1 - c], dsts[p], send.at[p], recv.at[p], (x, y, 1 - c))
            cp.start()
            cps.append(cp)
        loc = pltpu.make_async_copy(small_ref, small_all.at[me], lsem)
        loc.start()
        scps = []
        for d in range(1, 8):
            flip = lambda v, bit: 1 - v if bit else v
            peer = (flip(x, d & 4), flip(y, d & 2), flip(c, d & 1))
            cp = _remote(small_ref, small_all.at[me], ssend.at[d - 1], srecv.at[d - 1], peer)
            cp.start()
            scps.append((cp, 4 * peer[0] + 2 * peer[1] + peer[2], peer))
        for cp in cps:
            cp.wait()
        for d, (cp, slot, peer) in enumerate(scps):
            _remote(small_ref, small_all.at[slot], ssend.at[d], srecv.at[d], peer).wait_recv()
            cp.wait_send()
        loc.wait()

    outs = pl.pallas_call(
        body,
        name="pair_exchange",
        in_specs=[ANY] * (np_ + 1),
        out_specs=[ANY] * (np_ + 1),
        out_shape=[jax.ShapeDtypeStruct((N_CHIPS, h, a.shape[2]), F32) for a, h in zip(packs, halves)]
        + [jax.ShapeDtypeStruct((2 * N_CHIPS,) + small.shape, F32)],
        scratch_shapes=[pltpu.SemaphoreType.DMA((np_,)), pltpu.SemaphoreType.DMA((np_,)),
                        pltpu.SemaphoreType.DMA((7,)), pltpu.SemaphoreType.DMA((7,)), pltpu.SemaphoreType.DMA],
    )(*split, small)
    return outs[:np_], outs[np_]


def _pair_sum(pack, got, cidx, tag):
    _, rows, w = pack.shape
    h = rows // 2

    def body(c_ref, p_ref, x_ref, o_ref):
        del c_ref
        o_ref[...] = (p_ref[...] + x_ref[...]).astype(BF16)

    return pl.pallas_call(
        body,
        name="pair_sum_" + tag,
        grid_spec=pltpu.PrefetchScalarGridSpec(
            num_scalar_prefetch=1,
            grid=(N_CHIPS,),
            in_specs=[pl.BlockSpec((None, None, h, w), lambda k, c: (k, c[0], 0, 0)),
                      pl.BlockSpec((None, h, w), lambda k, c: (k, 0, 0))],
            out_specs=pl.BlockSpec((None, h, w), lambda k, c: (k, 0, 0)),
        ),
        out_shape=jax.ShapeDtypeStruct((N_CHIPS, h, w), BF16),
        compiler_params=_params(("arbitrary",)),
    )(cidx, pack.reshape(N_CHIPS, 2, h, w), got)


def _chip_exchange(packs):
    np_ = len(packs)

    def body(*refs):
        srcs, dsts = refs[:np_], refs[np_:2 * np_]
        send, recv = refs[2 * np_:]
        x, y, c, chips = _my_place()
        j = 2 * x + y
        sends = []
        for p in range(np_):
            for nn, (kx, ky) in enumerate(chips):
                cp = _remote(srcs[p].at[2 * kx + ky], dsts[p].at[j], send.at[p, nn], recv.at[p, nn], (kx, ky, c))
                cp.start()
                sends.append(cp)
        for p in range(np_):
            for nn, (kx, ky) in enumerate(chips):
                blk = dsts[p].at[2 * kx + ky]
                _remote(blk, blk, send.at[p, nn], recv.at[p, nn], (kx, ky, c)).wait_recv()
        for cp in sends:
            cp.wait_send()

    return pl.pallas_call(
        body,
        name="chip_exchange",
        in_specs=[ANY] * np_,
        out_specs=[ANY] * np_,
        out_shape=[jax.ShapeDtypeStruct(a.shape, a.dtype) for a in packs],
        scratch_shapes=[pltpu.SemaphoreType.DMA((np_, 3)), pltpu.SemaphoreType.DMA((np_, 3))],
    )(*packs)


def _sum_chips(parts, own, place, steps, tag):
    _, rows, w = parts.shape
    tr = rows // steps

    def body(place_ref, *refs):
        p_refs, own_ref, o_ref = refs[:N_CHIPS], refs[N_CHIPS], refs[N_CHIPS + 1]
        j = place_ref[0]
        acc = None
        for k in range(N_CHIPS):
            term = jnp.where(j == k, own_ref[...], p_refs[k][...]).astype(F32)
            acc = term if acc is None else acc + term
        o_ref[...] = acc

    def other(k):
        return lambda i, pr: (jnp.where(pr[0] == k, (k + 1) % N_CHIPS, k), i, 0)

    return pl.pallas_call(
        body,
        name="sum_chips_" + tag,
        grid_spec=pltpu.PrefetchScalarGridSpec(
            num_scalar_prefetch=1,
            grid=(steps,),
            in_specs=[pl.BlockSpec((None, tr, w), other(k)) for k in range(N_CHIPS)]
            + [pl.BlockSpec((None, tr, w), lambda i, pr: (pr[0], i, 0))],
            out_specs=pl.BlockSpec((None, tr, w), lambda i, pr: (pr[1], i, 0)),
        ),
        out_shape=jax.ShapeDtypeStruct((2, rows, w), F32),
        compiler_params=_params(("arbitrary",)),
    )(place, *([parts] * N_CHIPS), own)


def _sum_slots(parts, steps, tag):
    nslot, rows, w = parts.shape
    tr = rows // steps

    def body(p_ref, o_ref):
        acc = p_ref[0].astype(F32)
        for k in range(1, nslot):
            acc = acc + p_ref[k].astype(F32)
        o_ref[...] = acc

    return pl.pallas_call(
        body,
        name="sum_slots_" + tag,
        grid=(steps,),
        in_specs=[pl.BlockSpec((nslot, tr, w), lambda i: (0, i, 0))],
        out_specs=pl.BlockSpec((tr, w), lambda i: (i, 0)),
        out_shape=jax.ShapeDtypeStruct((rows, w), F32),
        compiler_params=_params(("arbitrary",)),
    )(parts)


def _pair_gather(bufs):
    np_ = len(bufs)

    def body(*refs):
        srcs, dsts = refs[:np_], refs[np_:2 * np_]
        send, recv = refs[2 * np_:]
        x, y, c, _ = _my_place()
        cps = []
        for p in range(np_):
            cp = _remote(srcs[p].at[c], dsts[p].at[c], send.at[p], recv.at[p], (x, y, 1 - c))
            cp.start()
            cps.append(cp)
        for p, cp in enumerate(cps):
            other = dsts[p].at[1 - c]
            _remote(other, other, send.at[p], recv.at[p], (x, y, 1 - c)).wait_recv()
            cp.wait_send()

    outs = pl.pallas_call(
        body,
        name="pair_gather",
        in_specs=[ANY] * np_,
        out_specs=[ANY] * np_,
        out_shape=[jax.ShapeDtypeStruct(a.shape, F32) for a in bufs],
        input_output_aliases={p: p for p in range(np_)},
        scratch_shapes=[pltpu.SemaphoreType.DMA((np_,))] * 2,
    )(*bufs)
    return [o.reshape(2 * a.shape[1], a.shape[2]) for o, a in zip(outs, bufs)]


def _adamw(gsrc, row0, w, m, v, tr, tag):
    rows, width = w.shape
    off = row0 // tr
    bc1 = 1.0 - ADAM_B1 ** ADAM_STEP
    bc2 = 1.0 - ADAM_B2 ** ADAM_STEP

    def body(g_ref, w_ref, m_ref, v_ref, go_ref, d_ref, mo_ref, vo_ref):
        g = g_ref[...]
        m2 = ADAM_B1 * m_ref[...] + (1.0 - ADAM_B1) * g
        v2 = ADAM_B2 * v_ref[...] + (1.0 - ADAM_B2) * (g * g)
        go_ref[...] = g
        mo_ref[...] = m2
        vo_ref[...] = v2
        d_ref[...] = -ADAM_LR * ((m2 / bc1) / (jnp.sqrt(v2 / bc2) + ADAM_EPS) + ADAM_WD * w_ref[...])

    here = pl.BlockSpec((tr, width), lambda i: (i, 0))
    return pl.pallas_call(
        body,
        name="adamw_" + tag,
        grid=(rows // tr,),
        in_specs=[pl.BlockSpec((tr, width), lambda i: (off + i, 0)), here, here, here],
        out_specs=[here] * 4,
        out_shape=[jax.ShapeDtypeStruct((rows, width), F32)] * 4,
        compiler_params=_params(("arbitrary",)),
    )(gsrc, w, m, v)


def _flat_rows(a):
    return a.reshape(-1, 128)


def kernel(x, mem, mix_pre_g, w_mix_in, conv_a_w, conv_b_w, conv_b_b, ln_b_g, ln_b_b, w_mix_out, mix_post_g, xa_pre_g, mem_norm_g, w_q, w_k, w_v, w_o, xa_post_g, ffn_pre_g, w_gate, w_up, w_down, ffn_post_g, loss_target, m_mix_pre_g, m_w_mix_in, m_conv_a_w, m_conv_b_w, m_conv_b_b, m_ln_b_g, m_ln_b_b, m_w_mix_out, m_mix_post_g, m_xa_pre_g, m_mem_norm_g, m_w_q, m_w_k, m_w_v, m_w_o, m_xa_post_g, m_ffn_pre_g, m_w_gate, m_w_up, m_w_down, m_ffn_post_g, v_mix_pre_g, v_w_mix_in, v_conv_a_w, v_conv_b_w, v_conv_b_b, v_ln_b_g, v_ln_b_b, v_w_mix_out, v_mix_post_g, v_xa_pre_g, v_mem_norm_g, v_w_q, v_w_k, v_w_v, v_w_o, v_xa_post_g, v_ffn_pre_g, v_w_gate, v_w_up, v_w_down, v_ffn_post_g):
    given = dict(locals())
    names = ["mix_pre_g", "w_mix_in", "conv_a_w", "conv_b_w", "conv_b_b", "ln_b_g", "ln_b_b", "w_mix_out",
             "mix_post_g", "xa_pre_g", "mem_norm_g", "w_q", "w_k", "w_v", "w_o", "xa_post_g", "ffn_pre_g",
             "w_gate", "w_up", "w_down", "ffn_post_g"]
    row = lambda a: a.reshape(1, -1)
    cx, cy, cc = lax.axis_index("x"), lax.axis_index("y"), lax.axis_index("c")
    chip = 2 * cx + cy
    ca_blk = conv_a_w.shape[1]

    pr = jnp.concatenate([w_mix_out, w_q, w_k, w_v, w_o, w_down], axis=0).astype(BF16)
    g2 = jnp.concatenate([w_gate, w_up], axis=0).astype(BF16)
    g1 = w_mix_in.astype(BF16)
    conv_rows = CONV_A_W + CONV_B_W
    sw = jnp.concatenate([conv_a_w, conv_b_w, jnp.zeros((SMALL_W_ROWS - conv_rows, ca_blk), F32)], axis=0)
    prg, g2g, g1g, swg = _gather_weights([pr, g2, g1, sw])
    conv_full = jnp.transpose(swg[:, :conv_rows, :], (1, 0, 2)).reshape(conv_rows, N_CHIPS * ca_blk)
    wa, wb = conv_full[:CONV_A_W], conv_full[CONV_A_W:]

    xs, ms, tgt = x[0], mem[0], loss_target[0]
    x1, u, o1, z1 = _mix_fwd(xs, row(mix_pre_g), row(mix_post_g), wa, wb, row(conv_b_b), row(ln_b_g), row(ln_b_b),
                             g1g, prg)
    memn, kb, vb = _mem_kv(ms, row(mem_norm_g), prg)
    x2, q, o2 = _xattn_fwd(x1, row(xa_pre_g), row(xa_post_g), kb, vb, prg)
    h3, gs, us, do3, dx3, loss_part, d_ffn_post = _ffn_fwd(x2, tgt, row(ffn_pre_g), row(ffn_post_g), g2g, prg)

    dh3p, gg2, gpr = _ffn_bwd(h3, do3, gs, us, g2g, prg)
    dx1, d_ffn_pre, d_xa_post, d_xa_pre, dk, dv, gpr = _xattn_bwd(
        dx3, dh3p, x2, x1, o2, q, kb, vb, row(ffn_pre_g), row(xa_post_g), row(xa_pre_g), prg, gpr)
    d_mem_g, gpr = _mem_bwd(dk, dv, ms, memn, row(mem_norm_g), prg, gpr)
    dx, d_mix_pre, d_mix_post, dwa, dwb, dbb, dlng, dlnb, gg1, gpr = _mix_bwd(
        dx1, xs, o1, u, z1, row(mix_pre_g), row(mix_post_g), wa, wb, row(ln_b_g), row(ln_b_b), g1g, prg, gpr)

    small_parts = [d_mix_pre, dwa, dwb, dbb, dlng, dlnb, d_mix_post, d_xa_pre, d_mem_g, d_xa_post, d_ffn_pre,
                   d_ffn_post, loss_part]
    sizes = [p.size for p in small_parts]
    small = jnp.concatenate([p.reshape(-1) for p in small_parts])
    small_rows = -(-small.size // (8 * 128)) * 8
    small = jnp.pad(small, (0, small_rows * 128 - small.size)).reshape(small_rows, 128)
    got, small_all = _pair_exchange([gpr, gg2, gg1], small)
    cidx = cc.astype(jnp.int32).reshape(1)
    sums = [_pair_sum(p, g, cidx, t) for p, g, t in zip([gpr, gg2, gg1], got, PACK_TAGS)]
    parts = _chip_exchange(sums)
    place = jnp.stack([chip, cc]).astype(jnp.int32)
    rpr, rg2, rg1 = _pair_gather([_sum_chips(p, s, place, 2, t) for p, s, t in zip(parts, sums, PACK_TAGS)])
    small_sum = _sum_slots(small_all, 1, "small").reshape(-1)
    red, pos = [], 0
    for p, sz in zip(small_parts, sizes):
        red.append(small_sum[pos:pos + sz].reshape(p.shape))
        pos += sz
    (r_mix_pre, r_wa, r_wb, r_bb, r_lng, r_lnb, r_mix_post, r_xa_pre, r_mem_g, r_xa_post, r_ffn_pre, r_ffn_post,
     r_loss) = red
    loss = r_loss[0, 0]

    res = {}
    big = {"w_mix_out": (rpr, PR_OUT, 256), "w_q": (rpr, PR_Q, 256), "w_k": (rpr, PR_K, 256),
           "w_v": (rpr, PR_V, 256), "w_o": (rpr, PR_O, 256), "w_down": (rpr, PR_DOWN, 64),
           "w_gate": (rg2, 0, 256), "w_up": (rg2, D_MODEL, 256), "w_mix_in": (rg1, 0, 256)}
    for nm, (src, row0, tr) in big.items():
        res[nm] = _adamw(src, row0, given[nm], given["m_" + nm], given["v_" + nm], tr, nm)
    small_grads = {"mix_pre_g": r_mix_pre, "conv_b_b": r_bb, "ln_b_g": r_lng, "ln_b_b": r_lnb,
                   "mix_post_g": r_mix_post, "xa_pre_g": r_xa_pre, "mem_norm_g": r_mem_g, "xa_post_g": r_xa_post,
                   "ffn_pre_g": r_ffn_pre, "ffn_post_g": r_ffn_post,
                   "conv_a_w": lax.dynamic_slice_in_dim(r_wa, chip * ca_blk, ca_blk, axis=1),
                   "conv_b_w": lax.dynamic_slice_in_dim(r_wb, chip * ca_blk, ca_blk, axis=1)}
    small_names = list(small_grads)

    def packed(prefix, grads=None):
        flat = jnp.concatenate([(grads[nm] if grads else given[prefix + nm]).reshape(-1) for nm in small_names])
        rows8 = -(-flat.size // (8 * 128)) * 8
        return jnp.pad(flat, (0, rows8 * 128 - flat.size)).reshape(rows8, 128)

    gp = packed("", small_grads)
    outs = _adamw(gp, 0, packed(""), packed("m_"), packed("v_"), gp.shape[0], "small")
    pos = 0
    for nm in small_names:
        shape = given[nm].shape
        sz = given[nm].size
        res[nm] = [o.reshape(-1)[pos:pos + sz].reshape(shape) for o in outs]
        pos += sz

    return (loss, dx[None], *[res[nm][0] for nm in names], *[res[nm][1] for nm in names],
            *[res[nm][2] for nm in names], *[res[nm][3] for nm in names])
```

```python
import jax
import jax.numpy as jnp
from jax import lax
from jax.experimental import pallas as pl
from jax.experimental.pallas import tpu as pltpu

F32 = jnp.float32
BF16 = jnp.bfloat16
MESH = pl.DeviceIdType.MESH

RMS_EPS = 1e-6
LN_EPS = 1e-5
D_MODEL = 1024
D_A = 512
D_B = 512
D_IN_ALL = 3 * D_A + 2 * D_B
CONV_A_W = 3
CONV_B_W = 31
HALO = 32
XA_HEADS = 4
HEAD_DIM = 256
D_FF = 2816
N_CHIPS = 4
N_DEV = 8
FF_BLK = D_FF // N_CHIPS
IN_BLK = D_IN_ALL // N_CHIPS
ROW_BLK = D_MODEL // N_CHIPS

ADAM_LR = 0.001
ADAM_B1 = 0.9
ADAM_B2 = 0.999
ADAM_EPS = 1e-08
ADAM_WD = 0.01
ADAM_STEP = 10

TILE_FWD = 512
TILE_FFN = 512
TILE_BWD = 256
FFN_BWD_BLOCKS = 1
CONV_ROWS = 64
V7X_VMEM_LIMIT = 56 * 1024 * 1024

P1_Q, P1_K, P1_V, P1_O, P1_DOWN = 0, 256, 512, 768, 1024
P1_ROWS = P1_DOWN + FF_BLK
GX_Q, GX_K, GX_V, GX_O = 0, 256, 512, 768
SMALL_W_ROWS = 48

ANY = pl.BlockSpec(memory_space=pl.ANY)


def _mm(a, b):
    return lax.dot_general(a, b, (((1,), (0,)), ((), ())), preferred_element_type=F32)


def _mm_nt(a, b):
    return lax.dot_general(a, b, (((1,), (1,)), ((), ())), preferred_element_type=F32)


def _mm_tn(a, b):
    return lax.dot_general(a, b, (((0,), (0,)), ((), ())), preferred_element_type=F32)


def _sigmoid(x):
    return 1.0 / (1.0 + jnp.exp(-x))


def _rms(x):
    r = lax.rsqrt(jnp.mean(x * x, axis=-1, keepdims=True) + RMS_EPS)
    return x * r, r


def _rms_bwd(dy, xn, r, g):
    gdy = dy * g
    dx = r * (gdy - xn * jnp.mean(gdy * xn, axis=-1, keepdims=True))
    return dx, jnp.sum(dy * xn, axis=0, keepdims=True)


def _full(shape):
    return pl.BlockSpec(shape, lambda *_: (0,) * len(shape))


def _params(sem=None):
    return pltpu.CompilerParams(dimension_semantics=sem, vmem_limit_bytes=V7X_VMEM_LIMIT)


def _load_rows(dst, src_hbm, row0, rows):
    for k in range(N_CHIPS):
        pltpu.sync_copy(src_hbm.at[k, pl.ds(row0, rows), :], dst.at[pl.ds(rows * k, rows), :])


def _load_cols(dst, src_hbm, cols):
    for k in range(N_CHIPS):
        pltpu.sync_copy(src_hbm.at[k], dst.at[:, pl.ds(cols * k, cols)])


def _fill_phases(src, sh, nrows):
    for r in range(1, 8):
        sh[r, pl.ds(0, nrows), :] = src[pl.ds(r, nrows), pl.ds(D_A, D_B)]


def _phase_rows(src, sh, off, start, size):
    r = off % 8
    if r == 0:
        return src[pl.ds(off + start, size), pl.ds(D_A, D_B)]
    return sh[r, pl.ds(off - r + start, size), :]


def _my_place():
    x, y, c = lax.axis_index("x"), lax.axis_index("y"), lax.axis_index("c")
    return x, y, c, ((1 - x, y), (x, 1 - y), (1 - x, 1 - y))


def _remote(src, dst, send_sem, recv_sem, to):
    return pltpu.make_async_remote_copy(src_ref=src, dst_ref=dst, send_sem=send_sem, recv_sem=recv_sem,
                                        device_id=to, device_id_type=MESH)


def _gather_sems(np_):
    return [pltpu.SemaphoreType.DMA((np_, 3))] * 4 + [pltpu.SemaphoreType.DMA((np_,))] * 2


def _gather_start(srcs, dsts, sems):
    send, recv, _, _, osend, orecv = sems
    x, y, c, chips = _my_place()
    j = 2 * x + y
    for p in range(len(srcs)):
        _remote(srcs[p], dsts[p].at[j], osend.at[p], orecv.at[p], (x, y, 1 - c)).start()
        for nn, (kx, ky) in enumerate(chips):
            _remote(srcs[p].at[c], dsts[p].at[j, c], send.at[p, nn], recv.at[p, nn], (kx, ky, c)).start()


def _gather_forward(srcs, dsts, sems):
    send, recv, fsend, frecv, _, _ = sems
    x, y, c, chips = _my_place()
    for nn, (kx, ky) in enumerate(chips):
        for p in range(len(srcs)):
            blk = dsts[p].at[2 * kx + ky, c]
            _remote(blk, blk, send.at[p, nn], recv.at[p, nn], (kx, ky, c)).wait_recv()
            _remote(blk, blk, fsend.at[p, nn], frecv.at[p, nn], (x, y, 1 - c)).start()


def _gather_finish(srcs, dsts, sems):
    send, recv, fsend, frecv, osend, orecv = sems
    x, y, c, chips = _my_place()
    j = 2 * x + y
    for nn, (kx, ky) in enumerate(chips):
        for p in range(len(srcs)):
            other = dsts[p].at[2 * kx + ky, 1 - c]
            _remote(other, other, fsend.at[p, nn], frecv.at[p, nn], (x, y, 1 - c)).wait_recv()
    for nn, (kx, ky) in enumerate(chips):
        for p in range(len(srcs)):
            _remote(srcs[p].at[c], dsts[p].at[j, c], send.at[p, nn], recv.at[p, nn], (kx, ky, c)).wait_send()
            blk = dsts[p].at[2 * kx + ky, c]
            _remote(blk, blk, fsend.at[p, nn], frecv.at[p, nn], (x, y, 1 - c)).wait_send()
    for p in range(len(srcs)):
        _remote(srcs[p], dsts[p].at[j], osend.at[p], orecv.at[p], (x, y, 1 - c)).wait()


def _split_halves(a):
    return a.reshape(2, a.shape[0] // 2, a.shape[1])


def _gather_weights(packs):
    np_ = len(packs)
    split = [_split_halves(a) for a in packs]

    def body(*refs):
        srcs, dsts, sems = refs[:np_], refs[np_:2 * np_], refs[2 * np_:]
        _gather_start(srcs, dsts, sems)
        _gather_forward(srcs, dsts, sems)
        _gather_finish(srcs, dsts, sems)

    outs = pl.pallas_call(
        body,
        name="gather_weights",
        in_specs=[ANY] * np_,
        out_specs=[ANY] * np_,
        out_shape=[jax.ShapeDtypeStruct((N_CHIPS,) + a.shape, a.dtype) for a in split],
        scratch_shapes=_gather_sems(np_),
    )(*split)
    return [o.reshape((N_CHIPS,) + a.shape) for o, a in zip(outs, packs)]


def _peers(x, y, c):
    out = []
    for d in range(1, N_DEV):
        px = 1 - x if d & 4 else x
        py = 1 - y if d & 2 else y
        pc = 1 - c if d & 1 else c
        out.append(((px, py, pc), 4 * px + 2 * py + pc))
    return out


def _scatter_copies(srcs, dsts, send, recv):
    x, y, c, _ = _my_place()
    me = 4 * x + 2 * y + c
    out = []
    for a in range(len(srcs)):
        for d, ((px, py, pc), pidx) in enumerate(_peers(x, y, c)):
            piece = srcs[a].at[2 * px + py, pc]
            out.append((_remote(piece, dsts[a].at[me], send.at[a, d], recv.at[a, d], (px, py, pc)),
                        _remote(piece, dsts[a].at[pidx], send.at[a, d], recv.at[a, d], (px, py, pc))))
    return out


def _scatter_start(srcs, dsts, send, recv):
    for out_cp, _ in _scatter_copies(srcs, dsts, send, recv):
        out_cp.start()


def _scatter_wait(srcs, dsts, send, recv):
    for out_cp, in_cp in _scatter_copies(srcs, dsts, send, recv):
        in_cp.wait_recv()
        out_cp.wait_send()


def _scatter_sems(na):
    return [pltpu.SemaphoreType.DMA((na, N_DEV - 1))] * 2


def _slots(a):
    return jax.ShapeDtypeStruct((N_DEV,) + a.shape[2:], a.dtype)


def _by_halves(a):
    return a.reshape(a.shape[0], 2, a.shape[1] // 2, a.shape[2])


def _mix_fwd(x, gpre, gpost, wa, wb, bb, lng, lnb, g1g, pr0g, late):
    S, D = x.shape
    T = min(TILE_FWD, S)
    n = S // T
    nl = len(late)
    late_split = [_split_halves(a) for a in late]

    def body(*refs):
        (x_ref, gpre_ref, gpost_ref, wa_ref, wb_ref, bb_ref, lng_ref, lnb_ref, g1g_hbm, pr0g_hbm) = refs[:10]
        srcs = refs[10:10 + nl]
        x1_ref, u_ref, o1_ref, z1_ref = refs[10 + nl:14 + nl]
        dsts = refs[14 + nl:14 + 2 * nl]
        win_v, wout_v, ext, sh, z1buf = refs[14 + 2 * nl:19 + 2 * nl]
        sems = refs[19 + 2 * nl:]
        i = pl.program_id(0)

        @pl.when(i == 0)
        def _():
            _gather_start(srcs, dsts, sems)
            _load_cols(win_v, g1g_hbm, IN_BLK)
            _load_rows(wout_v, pr0g_hbm, 0, ROW_BLK)
            ext[pl.ds(0, HALO), :] = jnp.zeros((HALO, D_A + D_B), F32)

        @pl.when(i == max(n - 2, 0))
        def _():
            _gather_forward(srcs, dsts, sems)

        xv = x_ref[...]
        xn, _ = _rms(xv)
        h = (xn * gpre_ref[...]).astype(BF16)
        u = _mm(h, win_v[...])
        u_ref[...] = u.astype(BF16)
        b_a = u[:, 0:D_A]
        cv = u[:, D_A:2 * D_A] * u[:, 2 * D_A:3 * D_A]
        z0 = u[:, 3 * D_A:3 * D_A + D_B] * _sigmoid(u[:, 3 * D_A + D_B:])
        ext[pl.ds(HALO, T), pl.ds(0, D_A)] = cv
        ext[pl.ds(HALO, T), pl.ds(D_A, D_B)] = z0

        conv_a = ext[pl.ds(HALO - 2, T), pl.ds(0, D_A)] * wa_ref[0:1, :]
        for k in range(1, CONV_A_W):
            conv_a = conv_a + ext[pl.ds(HALO - 2 + k, T), pl.ds(0, D_A)] * wa_ref[k:k + 1, :]
        y_a = b_a * conv_a

        _fill_phases(ext, sh, T + HALO - 8)
        base = HALO - (CONV_B_W - 1)

        def chunk(ci, carry):
            start = pl.multiple_of(ci * CONV_ROWS, 8)
            acc = jnp.broadcast_to(bb_ref[...], (CONV_ROWS, D_B))
            for k in range(CONV_B_W):
                acc = acc + _phase_rows(ext, sh, base + k, start, CONV_ROWS) * wb_ref[k:k + 1, :]
            z1buf[pl.ds(start, CONV_ROWS), :] = acc
            return carry

        lax.fori_loop(0, T // CONV_ROWS, chunk, 0)
        z1 = z1buf[...]
        z1_ref[...] = z1.astype(BF16)
        mu = jnp.mean(z1, axis=-1, keepdims=True)
        zc = z1 - mu
        rstd = lax.rsqrt(jnp.mean(zc * zc, axis=-1, keepdims=True) + LN_EPS)
        l = zc * rstd * lng_ref[...] + lnb_ref[...]
        y_b = l * _sigmoid(l)
        y = jnp.concatenate([y_a, y_b], axis=-1).astype(BF16)
        o1 = _mm(y, wout_v[...])
        o1_ref[...] = o1.astype(BF16)
        o1n, _ = _rms(o1)
        x1_ref[...] = xv + o1n * gpost_ref[...]
        ext[pl.ds(0, HALO), :] = ext[pl.ds(T, HALO), :]

        @pl.when(i == n - 1)
        def _():
            _gather_finish(srcs, dsts, sems)

    tok = lambda w: pl.BlockSpec((T, w), lambda i: (i, 0))
    outs = pl.pallas_call(
        body,
        name="mix_fwd",
        grid=(n,),
        in_specs=[tok(D), _full((1, D)), _full((1, D)), _full((CONV_A_W, D_A)), _full((CONV_B_W, D_B)),
                  _full((1, D_B)), _full((1, D_B)), _full((1, D_B)), ANY, ANY] + [ANY] * nl,
        out_specs=[tok(D), tok(D_IN_ALL), tok(D), tok(D_B)] + [ANY] * nl,
        out_shape=[jax.ShapeDtypeStruct((S, D), F32), jax.ShapeDtypeStruct((S, D_IN_ALL), BF16),
                   jax.ShapeDtypeStruct((S, D), BF16), jax.ShapeDtypeStruct((S, D_B), BF16)]
        + [jax.ShapeDtypeStruct((N_CHIPS,) + a.shape, a.dtype) for a in late_split],
        scratch_shapes=[pltpu.VMEM((D, D_IN_ALL), BF16), pltpu.VMEM((D_A + D_B, D), BF16),
                        pltpu.VMEM((HALO + T, D_A + D_B), F32), pltpu.VMEM((8, HALO + T, D_B), F32),
                        pltpu.VMEM((T, D_B), F32)] + _gather_sems(nl),
        compiler_params=_params(("arbitrary",)),
    )(x, gpre, gpost, wa, wb, bb, lng, lnb, g1g, pr0g, *late_split)
    return list(outs[:4]) + [o.reshape((N_CHIPS,) + a.shape) for o, a in zip(outs[4:], late)]


def _mix_bwd(dx1, x, o1, u, z1s, gpre, gpost, wa, wb, lng, lnb, g1g, pr0g, cx):
    S, D = x.shape
    T = min(TILE_BWD, S)
    n = S // T
    hb = T // HALO
    cxs = _by_halves(cx)

    def body(dx1_ref, x_ref, o1_ref, u_ref, uh_ref, z1_ref, gpre_ref, gpost_ref, wa_ref, wb_ref, lng_ref, lnb_ref,
             g1g_hbm, pr0g_hbm, cx_hbm,
             dx_ref, dgpre_ref, dgpost_ref, dwa_ref, dwb_ref, dbb_ref, dlng_ref, dlnb_ref, cm1_hbm, cm2_hbm, yx_hbm,
             win_v, wout_v, dwin_acc, dwout_acc, ext, ext2, shf, shb, dz0buf, dwb_acc, send, recv):
        i = pl.program_id(0)

        @pl.when(i == 0)
        def _():
            _scatter_start([cx_hbm], [yx_hbm], send, recv)
            _load_cols(win_v, g1g_hbm, IN_BLK)
            _load_rows(wout_v, pr0g_hbm, 0, ROW_BLK)
            dwin_acc[...] = jnp.zeros_like(dwin_acc)
            dwout_acc[...] = jnp.zeros_like(dwout_acc)
            dwb_acc[...] = jnp.zeros_like(dwb_acc)
            ext2[pl.ds(T, HALO), :] = jnp.zeros((HALO, D_A + D_B), F32)
            for ref in (dgpre_ref, dgpost_ref, dwa_ref, dbb_ref, dlng_ref, dlnb_ref):
                ref[...] = jnp.zeros_like(ref)

        o1n, r1 = _rms(o1_ref[...].astype(F32))
        dx1v = dx1_ref[...]
        d_o1, dgp = _rms_bwd(dx1v, o1n, r1, gpost_ref[...])
        dgpost_ref[...] += dgp
        d_o1b = d_o1.astype(BF16)
        dy = _mm_nt(d_o1b, wout_v[...])

        first = (i == n - 1).astype(F32)
        uh = uh_ref[...].astype(F32) * (1.0 - first)
        ext[pl.ds(0, HALO), pl.ds(0, D_A)] = uh[:, D_A:2 * D_A] * uh[:, 2 * D_A:3 * D_A]
        ext[pl.ds(0, HALO), pl.ds(D_A, D_B)] = uh[:, 3 * D_A:3 * D_A + D_B] * _sigmoid(uh[:, 3 * D_A + D_B:])
        uf = u_ref[...].astype(F32)
        b_a = uf[:, 0:D_A]
        c_a = uf[:, D_A:2 * D_A]
        v_a = uf[:, 2 * D_A:3 * D_A]
        gv = uf[:, 3 * D_A:3 * D_A + D_B]
        sg = _sigmoid(uf[:, 3 * D_A + D_B:])
        ext[pl.ds(HALO, T), pl.ds(0, D_A)] = c_a * v_a
        ext[pl.ds(HALO, T), pl.ds(D_A, D_B)] = gv * sg
        conv_a = ext[pl.ds(HALO - 2, T), pl.ds(0, D_A)] * wa_ref[0:1, :]
        for k in range(1, CONV_A_W):
            conv_a = conv_a + ext[pl.ds(HALO - 2 + k, T), pl.ds(0, D_A)] * wa_ref[k:k + 1, :]
        z1 = z1_ref[...].astype(F32)
        mu = jnp.mean(z1, axis=-1, keepdims=True)
        zc = z1 - mu
        rstd = lax.rsqrt(jnp.mean(zc * zc, axis=-1, keepdims=True) + LN_EPS)
        zn = zc * rstd
        l = zn * lng_ref[...] + lnb_ref[...]
        sl = _sigmoid(l)
        y = jnp.concatenate([b_a * conv_a, l * sl], axis=-1).astype(BF16)
        dwout_acc[...] += _mm_tn(y, d_o1b)

        dy_a = dy[:, 0:D_A]
        dl = dy[:, D_A:] * (sl * (1.0 + l * (1.0 - sl)))
        dlng_ref[...] += jnp.sum(dl * zn, axis=0, keepdims=True)
        dlnb_ref[...] += jnp.sum(dl, axis=0, keepdims=True)
        dzn = dl * lng_ref[...]
        dz1 = rstd * (dzn - jnp.mean(dzn, axis=-1, keepdims=True) - zn * jnp.mean(dzn * zn, axis=-1, keepdims=True))
        dbb_ref[...] += jnp.sum(dz1, axis=0, keepdims=True)
        d_conv = dy_a * b_a
        ext2[pl.ds(0, T), pl.ds(0, D_A)] = d_conv
        ext2[pl.ds(0, T), pl.ds(D_A, D_B)] = dz1

        d_cv = ext2[pl.ds(CONV_A_W - 1, T), pl.ds(0, D_A)] * wa_ref[0:1, :]
        for k in range(1, CONV_A_W):
            d_cv = d_cv + ext2[pl.ds(CONV_A_W - 1 - k, T), pl.ds(0, D_A)] * wa_ref[k:k + 1, :]
        for k in range(CONV_A_W):
            dwa_ref[k:k + 1, :] += jnp.sum(d_conv * ext[pl.ds(HALO - 2 + k, T), pl.ds(0, D_A)], axis=0, keepdims=True)

        _fill_phases(ext, shf, T + HALO - 8)
        _fill_phases(ext2, shb, T + HALO - 8)
        base = HALO - (CONV_B_W - 1)

        def chunk(ci, carry):
            start = pl.multiple_of(ci * CONV_ROWS, 8)
            dzc = ext2[pl.ds(start, CONV_ROWS), pl.ds(D_A, D_B)]
            acc = jnp.zeros((CONV_ROWS, D_B), F32)
            for k in range(CONV_B_W):
                wk = wb_ref[k:k + 1, :]
                acc = acc + _phase_rows(ext2, shb, CONV_B_W - 1 - k, start, CONV_ROWS) * wk
                prod = dzc * _phase_rows(ext, shf, base + k, start, CONV_ROWS)
                part = prod[0:8, :]
                for m in range(1, CONV_ROWS // 8):
                    part = part + prod[8 * m:8 * m + 8, :]
                dwb_acc[k] += part
            dz0buf[pl.ds(start, CONV_ROWS), :] = acc
            return carry

        lax.fori_loop(0, T // CONV_ROWS, chunk, 0)
        dz0 = dz0buf[...]
        du = jnp.concatenate([dy_a * conv_a, d_cv * v_a, d_cv * c_a, dz0 * sg, dz0 * gv * sg * (1.0 - sg)],
                             axis=-1).astype(BF16)
        dh = _mm_nt(du, win_v[...])
        xv = x_ref[...]
        xn, r0 = _rms(xv)
        dwin_acc[...] += _mm_tn((xn * gpre_ref[...]).astype(BF16), du)
        dxp, dg0 = _rms_bwd(dh, xn, r0, gpre_ref[...])
        dgpre_ref[...] += dg0
        dx_ref[...] = dx1v + dxp
        ext2[pl.ds(T, HALO), :] = ext2[pl.ds(0, HALO), :]

        @pl.when(i == n - 1)
        def _():
            for k in range(CONV_B_W):
                dwb_ref[k:k + 1, :] = jnp.sum(dwb_acc[k], axis=0, keepdims=True)
            win_v[...] = dwin_acc[...].astype(BF16)
            wout_v[...] = dwout_acc[...].astype(BF16)
            for k in range(N_CHIPS):
                pltpu.sync_copy(win_v.at[:, pl.ds(IN_BLK * k, IN_BLK)], cm1_hbm.at[k])
                pltpu.sync_copy(wout_v.at[pl.ds(ROW_BLK * k, ROW_BLK), :], cm2_hbm.at[k])
            _scatter_wait([cx_hbm], [yx_hbm], send, recv)

    rev = lambda w: pl.BlockSpec((T, w), lambda i: (n - 1 - i, 0))
    halo = pl.BlockSpec((HALO, D_IN_ALL), lambda i: (jnp.maximum((n - 1 - i) * hb - 1, 0), 0))
    return pl.pallas_call(
        body,
        name="mix_bwd",
        grid=(n,),
        in_specs=[rev(D), rev(D), rev(D), rev(D_IN_ALL), halo, rev(D_B), _full((1, D)), _full((1, D)),
                  _full((CONV_A_W, D_A)), _full((CONV_B_W, D_B)), _full((1, D_B)), _full((1, D_B)), ANY, ANY, ANY],
        out_specs=[rev(D), _full((1, D)), _full((1, D)), _full((CONV_A_W, D_A)), _full((CONV_B_W, D_B)),
                   _full((1, D_B)), _full((1, D_B)), _full((1, D_B)), ANY, ANY, ANY],
        out_shape=[jax.ShapeDtypeStruct((S, D), F32), jax.ShapeDtypeStruct((1, D), F32),
                   jax.ShapeDtypeStruct((1, D), F32), jax.ShapeDtypeStruct((CONV_A_W, D_A), F32),
                   jax.ShapeDtypeStruct((CONV_B_W, D_B), F32), jax.ShapeDtypeStruct((1, D_B), F32),
                   jax.ShapeDtypeStruct((1, D_B), F32), jax.ShapeDtypeStruct((1, D_B), F32),
                   jax.ShapeDtypeStruct((N_CHIPS, D, IN_BLK), BF16),
                   jax.ShapeDtypeStruct((N_CHIPS, ROW_BLK, D), BF16), _slots(cxs)],
        scratch_shapes=[pltpu.VMEM((D, D_IN_ALL), BF16), pltpu.VMEM((D_A + D_B, D), BF16),
                        pltpu.VMEM((D, D_IN_ALL), F32), pltpu.VMEM((D_A + D_B, D), F32),
                        pltpu.VMEM((HALO + T, D_A + D_B), F32), pltpu.VMEM((HALO + T, D_A + D_B), F32),
                        pltpu.VMEM((8, HALO + T, D_B), F32), pltpu.VMEM((8, HALO + T, D_B), F32),
                        pltpu.VMEM((T, D_B), F32), pltpu.VMEM((CONV_B_W, 8, D_B), F32)] + _scatter_sems(1),
        compiler_params=_params(("arbitrary",)),
    )(dx1, x, o1, u, u, z1s, gpre, gpost, wa, wb, lng, lnb, g1g, pr0g, cxs)


def _mem_kv(mem, gmem, pr1g):
    M, D = mem.shape

    def body(mem_ref, g_ref, pr1g_hbm, memn_ref, k_ref, v_ref, wk_v, wv_v):
        _load_rows(wk_v, pr1g_hbm, P1_K, ROW_BLK)
        _load_rows(wv_v, pr1g_hbm, P1_V, ROW_BLK)
        mn, _ = _rms(mem_ref[...])
        mb = (mn * g_ref[...]).astype(BF16)
        memn_ref[...] = mb
        k_ref[...] = _mm(mb, wk_v[...]).astype(BF16)
        v_ref[...] = _mm(mb, wv_v[...]).astype(BF16)

    return pl.pallas_call(
        body,
        name="mem_kv",
        grid=(1,),
        in_specs=[_full((M, D)), _full((1, D)), ANY],
        out_specs=[_full((M, D))] * 3,
        out_shape=[jax.ShapeDtypeStruct((M, D), BF16)] * 3,
        scratch_shapes=[pltpu.VMEM((D, D), BF16), pltpu.VMEM((D, D), BF16)],
        compiler_params=_params(("arbitrary",)),
    )(mem, gmem, pr1g)


def _attend(qb, kb, vb):
    scale = HEAD_DIM ** -0.5
    ps, os_ = [], []
    for hd in range(XA_HEADS):
        cols = slice(HEAD_DIM * hd, HEAD_DIM * (hd + 1))
        s = _mm_nt(qb[:, cols], kb[:, cols]) * scale
        e = jnp.exp(s - jnp.max(s, axis=-1, keepdims=True))
        p = e / jnp.sum(e, axis=-1, keepdims=True)
        ps.append(p)
        os_.append(_mm(p.astype(BF16), vb[:, cols]))
    return ps, jnp.concatenate(os_, axis=-1).astype(BF16)


def _xattn_fwd(x1, gpre, gpost, kb, vb, pr1g):
    S, D = x1.shape
    M = kb.shape[0]
    T = min(TILE_FWD, S)
    n = S // T

    def body(x1_ref, gpre_ref, gpost_ref, k_ref, v_ref, pr1g_hbm, x2_ref, q_ref, o2_ref, wq_v, wo_v):
        @pl.when(pl.program_id(0) == 0)
        def _():
            _load_rows(wq_v, pr1g_hbm, P1_Q, ROW_BLK)
            _load_rows(wo_v, pr1g_hbm, P1_O, ROW_BLK)

        xv = x1_ref[...]
        xn, _ = _rms(xv)
        qb = _mm((xn * gpre_ref[...]).astype(BF16), wq_v[...]).astype(BF16)
        q_ref[...] = qb
        _, ob = _attend(qb, k_ref[...], v_ref[...])
        o2 = _mm(ob, wo_v[...])
        o2_ref[...] = o2.astype(BF16)
        o2n, _ = _rms(o2)
        x2_ref[...] = xv + o2n * gpost_ref[...]

    tok = lambda w: pl.BlockSpec((T, w), lambda i: (i, 0))
    return pl.pallas_call(
        body,
        name="xattn_fwd",
        grid=(n,),
        in_specs=[tok(D), _full((1, D)), _full((1, D)), _full((M, D)), _full((M, D)), ANY],
        out_specs=[tok(D), tok(D), tok(D)],
        out_shape=[jax.ShapeDtypeStruct((S, D), F32), jax.ShapeDtypeStruct((S, D), BF16),
                   jax.ShapeDtypeStruct((S, D), BF16)],
        scratch_shapes=[pltpu.VMEM((D, D), BF16), pltpu.VMEM((D, D), BF16)],
        compiler_params=_params(("arbitrary",)),
    )(x1, gpre, gpost, kb, vb, pr1g)


def _xattn_bwd(dx3, dh3p, x2, x1, o2, q, kb, vb, gffn, gpost, gpre, pr1g, cf1, cf2):
    S, D = x1.shape
    M = kb.shape[0]
    T = min(TILE_BWD, S)
    n = S // T
    scale = HEAD_DIM ** -0.5
    nparts = dh3p.shape[0]
    cfs = [_by_halves(cf1), _by_halves(cf2)]

    def body(*refs):
        dx3_ref, dh3_refs = refs[0], refs[1:1 + nparts]
        (x2_ref, x1_ref, o2_ref, q_ref, k_ref, v_ref, gffn_ref, gpost_ref, gpre_ref, pr1g_hbm, cf1_hbm, cf2_hbm,
         dx1_ref, dgffn_ref, dgpost_ref, dgpre_ref, dk_ref, dv_ref, cx_hbm, yf1_hbm, yf2_hbm,
         wq_v, wo_v, dwq_acc, dwo_acc, send, recv) = refs[1 + nparts:]
        i = pl.program_id(0)

        @pl.when(i == 0)
        def _():
            _scatter_start([cf1_hbm, cf2_hbm], [yf1_hbm, yf2_hbm], send, recv)
            _load_rows(wq_v, pr1g_hbm, P1_Q, ROW_BLK)
            _load_rows(wo_v, pr1g_hbm, P1_O, ROW_BLK)
            dwq_acc[...] = jnp.zeros_like(dwq_acc)
            dwo_acc[...] = jnp.zeros_like(dwo_acc)
            for ref in (dgffn_ref, dgpost_ref, dgpre_ref, dk_ref, dv_ref):
                ref[...] = jnp.zeros_like(ref)

        x2n, r2 = _rms(x2_ref[...])
        dh3 = dh3_refs[0][...].astype(F32)
        for ref in dh3_refs[1:]:
            dh3 = dh3 + ref[...].astype(F32)
        dxp, dg = _rms_bwd(dh3, x2n, r2, gffn_ref[...])
        dgffn_ref[...] += dg
        dx2 = dx3_ref[...] + dxp
        o2n, ro = _rms(o2_ref[...].astype(F32))
        d_o2, dg = _rms_bwd(dx2, o2n, ro, gpost_ref[...])
        dgpost_ref[...] += dg
        d_o2b = d_o2.astype(BF16)
        d_o = _mm_nt(d_o2b, wo_v[...]).astype(BF16)
        qb = q_ref[...]
        kv = k_ref[...]
        vv = v_ref[...]
        ps, ob = _attend(qb, kv, vv)
        dwo_acc[...] += _mm_tn(ob, d_o2b)
        dqs = []
        for hd in range(XA_HEADS):
            cols = slice(HEAD_DIM * hd, HEAD_DIM * (hd + 1))
            p = ps[hd]
            dp = _mm_nt(d_o[:, cols], vv[:, cols])
            dv_ref[:, cols] += _mm_tn(p.astype(BF16), d_o[:, cols])
            ds = (p * (dp - jnp.sum(p * dp, axis=-1, keepdims=True)) * scale).astype(BF16)
            dqs.append(_mm(ds, kv[:, cols]))
            dk_ref[:, cols] += _mm_tn(ds, qb[:, cols])
        dq = jnp.concatenate(dqs, axis=-1).astype(BF16)
        dh2 = _mm_nt(dq, wq_v[...])
        x1n, r1 = _rms(x1_ref[...])
        dwq_acc[...] += _mm_tn((x1n * gpre_ref[...]).astype(BF16), dq)
        dxp, dg = _rms_bwd(dh2, x1n, r1, gpre_ref[...])
        dgpre_ref[...] += dg
        dx1_ref[...] = dx2 + dxp

        @pl.when(i == n - 1)
        def _():
            wq_v[...] = dwq_acc[...].astype(BF16)
            wo_v[...] = dwo_acc[...].astype(BF16)
            for k in range(N_CHIPS):
                rows = pl.ds(ROW_BLK * k, ROW_BLK)
                pltpu.sync_copy(wq_v.at[rows, :], cx_hbm.at[k, pl.ds(GX_Q, ROW_BLK), :])
                pltpu.sync_copy(wo_v.at[rows, :], cx_hbm.at[k, pl.ds(GX_O, ROW_BLK), :])
            _scatter_wait([cf1_hbm, cf2_hbm], [yf1_hbm, yf2_hbm], send, recv)

    tok = lambda w: pl.BlockSpec((T, w), lambda i: (i, 0))
    part = lambda j: pl.BlockSpec((None, T, D), lambda i: (j, i, 0))
    return pl.pallas_call(
        body,
        name="xattn_bwd",
        grid=(n,),
        in_specs=[tok(D)] + [part(j) for j in range(nparts)] + [tok(D), tok(D), tok(D), tok(D), _full((M, D)),
                                                                 _full((M, D)), _full((1, D)), _full((1, D)),
                                                                 _full((1, D)), ANY, ANY, ANY],
        out_specs=[tok(D), _full((1, D)), _full((1, D)), _full((1, D)), _full((M, D)), _full((M, D)), ANY, ANY, ANY],
        out_shape=[jax.ShapeDtypeStruct((S, D), F32), jax.ShapeDtypeStruct((1, D), F32),
                   jax.ShapeDtypeStruct((1, D), F32), jax.ShapeDtypeStruct((1, D), F32),
                   jax.ShapeDtypeStruct((M, D), F32), jax.ShapeDtypeStruct((M, D), F32),
                   jax.ShapeDtypeStruct((N_CHIPS, D, D), BF16), _slots(cfs[0]), _slots(cfs[1])],
        scratch_shapes=[pltpu.VMEM((D, D), BF16), pltpu.VMEM((D, D), BF16),
                        pltpu.VMEM((D, D), F32), pltpu.VMEM((D, D), F32)] + _scatter_sems(2),
        compiler_params=_params(("arbitrary",)),
    )(dx3, *([dh3p] * nparts), x2, x1, o2, q, kb, vb, gffn, gpost, gpre, pr1g, *cfs)


def _mem_bwd(dk, dv, mem, memn, gmem, pr1g, cx_in):
    M, D = mem.shape

    def body(dk_ref, dv_ref, mem_ref, memn_ref, g_ref, pr1g_hbm, cx_hbm, dg_ref, cx_out, wk_v, wv_v):
        del cx_hbm
        _load_rows(wk_v, pr1g_hbm, P1_K, ROW_BLK)
        _load_rows(wv_v, pr1g_hbm, P1_V, ROW_BLK)
        dkb = dk_ref[...].astype(BF16)
        dvb = dv_ref[...].astype(BF16)
        mb = memn_ref[...]
        dmn = _mm_nt(dkb, wk_v[...]) + _mm_nt(dvb, wv_v[...])
        mn, _ = _rms(mem_ref[...])
        dg_ref[...] = jnp.sum(dmn * mn, axis=0, keepdims=True)
        wk_v[...] = _mm_tn(mb, dkb).astype(BF16)
        wv_v[...] = _mm_tn(mb, dvb).astype(BF16)
        for k in range(N_CHIPS):
            rows = pl.ds(ROW_BLK * k, ROW_BLK)
            pltpu.sync_copy(wk_v.at[rows, :], cx_out.at[k, pl.ds(GX_K, ROW_BLK), :])
            pltpu.sync_copy(wv_v.at[rows, :], cx_out.at[k, pl.ds(GX_V, ROW_BLK), :])

    return pl.pallas_call(
        body,
        name="mem_bwd",
        grid=(1,),
        in_specs=[_full((M, D)), _full((M, D)), _full((M, D)), _full((M, D)), _full((1, D)), ANY, ANY],
        out_specs=[_full((1, D)), ANY],
        out_shape=[jax.ShapeDtypeStruct((1, D), F32), jax.ShapeDtypeStruct(cx_in.shape, BF16)],
        input_output_aliases={6: 1},
        scratch_shapes=[pltpu.VMEM((D, D), BF16), pltpu.VMEM((D, D), BF16)],
        compiler_params=_params(("arbitrary",)),
    )(dk, dv, mem, memn, gmem, pr1g, cx_in)


def _ffn_fwd(x2, target, gpre, gpost, g2g, pr1g):
    S, D = x2.shape
    T = min(TILE_FFN, S)
    n = S // T

    def body(x2_ref, t_ref, gpre_ref, gpost_ref, g2g_hbm, pr1g_hbm,
             h3_ref, g_hbm, u_hbm, do3_ref, dx3_ref, loss_ref, dgpost_ref, wg_v, wu_v, wd_v, gst, ust, sem):
        i = pl.program_id(0)

        @pl.when(i == 0)
        def _():
            pltpu.sync_copy(g2g_hbm.at[:, pl.ds(0, D), :], wg_v)
            pltpu.sync_copy(g2g_hbm.at[:, pl.ds(D, D), :], wu_v)
            pltpu.sync_copy(pr1g_hbm.at[:, pl.ds(P1_DOWN, FF_BLK), :], wd_v)
            loss_ref[...] = jnp.zeros_like(loss_ref)
            dgpost_ref[...] = jnp.zeros_like(dgpost_ref)

        xv = x2_ref[...]
        xn, _ = _rms(xv)
        hb = (xn * gpre_ref[...]).astype(BF16)
        h3_ref[...] = hb
        o3 = jnp.zeros((T, D), F32)
        out = [None, None]
        for c in range(N_CHIPS):
            slot = c % 2
            if out[slot] is not None:
                for cp in out[slot]:
                    cp.wait()
            g = _mm(hb, wg_v[c])
            u = _mm(hb, wu_v[c])
            gst[slot] = g.astype(BF16)
            ust[slot] = u.astype(BF16)
            out[slot] = (pltpu.make_async_copy(gst.at[slot], g_hbm.at[c, i], sem.at[0, slot]),
                         pltpu.make_async_copy(ust.at[slot], u_hbm.at[c, i], sem.at[1, slot]))
            for cp in out[slot]:
                cp.start()
            o3 = o3 + _mm((g * _sigmoid(g) * u).astype(BF16), wd_v[c])
        for pair in out:
            for cp in pair:
                cp.wait()
        o3n, r3 = _rms(o3)
        diff = xv + o3n * gpost_ref[...] - t_ref[...]
        sq = jnp.sum(jnp.sum(diff * diff, axis=-1, keepdims=True), axis=0, keepdims=True)
        loss_ref[...] += sq * (0.5 / D)
        dx3 = diff * (1.0 / D)
        dx3_ref[...] = dx3
        d_o3, dg = _rms_bwd(dx3, o3n, r3, gpost_ref[...])
        dgpost_ref[...] += dg
        do3_ref[...] = d_o3.astype(BF16)

    tok = lambda w: pl.BlockSpec((T, w), lambda i: (i, 0))
    h3, gs, us, do3, dx3, loss, dgpost = pl.pallas_call(
        body,
        name="ffn_fwd",
        grid=(n,),
        in_specs=[tok(D), tok(D), _full((1, D)), _full((1, D)), ANY, ANY],
        out_specs=[tok(D), ANY, ANY, tok(D), tok(D), _full((1, 128)), _full((1, D))],
        out_shape=[jax.ShapeDtypeStruct((S, D), BF16), jax.ShapeDtypeStruct((N_CHIPS, n, T, FF_BLK), BF16),
                   jax.ShapeDtypeStruct((N_CHIPS, n, T, FF_BLK), BF16), jax.ShapeDtypeStruct((S, D), BF16),
                   jax.ShapeDtypeStruct((S, D), F32), jax.ShapeDtypeStruct((1, 128), F32),
                   jax.ShapeDtypeStruct((1, D), F32)],
        scratch_shapes=[pltpu.VMEM((N_CHIPS, D, FF_BLK), BF16), pltpu.VMEM((N_CHIPS, D, FF_BLK), BF16),
                        pltpu.VMEM((N_CHIPS, FF_BLK, D), BF16), pltpu.VMEM((2, T, FF_BLK), BF16),
                        pltpu.VMEM((2, T, FF_BLK), BF16), pltpu.SemaphoreType.DMA((2, 2))],
        compiler_params=_params(("arbitrary",)),
    )(x2, target, gpre, gpost, g2g, pr1g)
    return h3, gs.reshape(N_CHIPS, S, FF_BLK), us.reshape(N_CHIPS, S, FF_BLK), do3, dx3, loss, dgpost


def _ffn_bwd(h3, do3, gs, us, g2g, pr1g):
    S, D = h3.shape
    T = min(TILE_FFN, S)
    n = S // T
    NP = FFN_BWD_BLOCKS

    def body(h3_ref, do3_ref, g_ref, u_ref, g2g_hbm, pr1g_hbm, dh3_ref, cf1_hbm, cf2_hbm,
             wg_v, wu_v, wd_v, dwg_acc, dwu_acc, dwd_acc):
        jp = pl.program_id(0)
        i = pl.program_id(1)
        blocks = pl.ds(NP * jp, NP)

        @pl.when(i == 0)
        def _():
            pltpu.sync_copy(g2g_hbm.at[blocks, pl.ds(0, D), :], wg_v)
            pltpu.sync_copy(g2g_hbm.at[blocks, pl.ds(D, D), :], wu_v)
            pltpu.sync_copy(pr1g_hbm.at[blocks, pl.ds(P1_DOWN, FF_BLK), :], wd_v)
            dwg_acc[...] = jnp.zeros_like(dwg_acc)
            dwu_acc[...] = jnp.zeros_like(dwu_acc)
            dwd_acc[...] = jnp.zeros_like(dwd_acc)

        hb = h3_ref[...]
        d_o3 = do3_ref[...]
        dh = jnp.zeros((T, D), F32)
        for c in range(NP):
            g = g_ref[c].astype(F32)
            u = u_ref[c].astype(F32)
            sg = _sigmoid(g)
            sl = g * sg
            da = _mm_nt(d_o3, wd_v[c])
            dwd_acc[c] += _mm_tn((sl * u).astype(BF16), d_o3)
            dub = (da * sl).astype(BF16)
            dgb = (da * u * (sg * (1.0 + g * (1.0 - sg)))).astype(BF16)
            dwg_acc[c] += _mm_tn(hb, dgb)
            dwu_acc[c] += _mm_tn(hb, dub)
            dh = dh + _mm_nt(dgb, wg_v[c]) + _mm_nt(dub, wu_v[c])
        dh3_ref[...] = dh.astype(BF16)

        @pl.when(i == n - 1)
        def _():
            wg_v[...] = dwg_acc[...].astype(BF16)
            wu_v[...] = dwu_acc[...].astype(BF16)
            wd_v[...] = dwd_acc[...].astype(BF16)
            pltpu.sync_copy(wg_v, cf1_hbm.at[blocks, pl.ds(0, D), :])
            pltpu.sync_copy(wu_v, cf1_hbm.at[blocks, pl.ds(D, D), :])
            pltpu.sync_copy(wd_v, cf2_hbm.at[blocks])

    tok = lambda w: pl.BlockSpec((T, w), lambda jp, i: (i, 0))
    blk = pl.BlockSpec((NP, T, FF_BLK), lambda jp, i: (jp, i, 0))
    return pl.pallas_call(
        body,
        name="ffn_bwd",
        grid=(N_CHIPS // NP, n),
        in_specs=[tok(D), tok(D), blk, blk, ANY, ANY],
        out_specs=[pl.BlockSpec((None, T, D), lambda jp, i: (jp, i, 0)), ANY, ANY],
        out_shape=[jax.ShapeDtypeStruct((N_CHIPS // NP, S, D), BF16),
                   jax.ShapeDtypeStruct((N_CHIPS, 2 * D, FF_BLK), BF16),
                   jax.ShapeDtypeStruct((N_CHIPS, FF_BLK, D), BF16)],
        scratch_shapes=[pltpu.VMEM((NP, D, FF_BLK), BF16), pltpu.VMEM((NP, D, FF_BLK), BF16),
                        pltpu.VMEM((NP, FF_BLK, D), BF16), pltpu.VMEM((NP, D, FF_BLK), F32),
                        pltpu.VMEM((NP, D, FF_BLK), F32), pltpu.VMEM((NP, FF_BLK, D), F32)],
        compiler_params=_params(("arbitrary", "arbitrary")),
    )(h3, do3, gs, us, g2g, pr1g)


def _exchange_last(contribs, small):
    na = len(contribs)
    cs = [_by_halves(a) for a in contribs]

    def body(*refs):
        srcs, small_ref = refs[:na], refs[na]
        dsts, small_all = refs[na + 1:2 * na + 1], refs[2 * na + 1]
        send, recv, ssend, srecv, lsem = refs[2 * na + 2:]
        x, y, c, _ = _my_place()
        me = 4 * x + 2 * y + c
        _scatter_start(srcs, dsts, send, recv)
        loc = pltpu.make_async_copy(small_ref, small_all.at[me], lsem)
        loc.start()
        scps = []
        for d, (peer, pidx) in enumerate(_peers(x, y, c)):
            cp = _remote(small_ref, small_all.at[me], ssend.at[d], srecv.at[d], peer)
            cp.start()
            scps.append((cp, _remote(small_ref, small_all.at[pidx], ssend.at[d], srecv.at[d], peer)))
        _scatter_wait(srcs, dsts, send, recv)
        for out_cp, in_cp in scps:
            in_cp.wait_recv()
            out_cp.wait_send()
        loc.wait()

    outs = pl.pallas_call(
        body,
        name="exchange_last",
        in_specs=[ANY] * (na + 1),
        out_specs=[ANY] * (na + 1),
        out_shape=[_slots(a) for a in cs] + [jax.ShapeDtypeStruct((N_DEV,) + small.shape, F32)],
        scratch_shapes=_scatter_sems(na) + [pltpu.SemaphoreType.DMA((N_DEV - 1,))] * 2 + [pltpu.SemaphoreType.DMA],
    )(*cs, small)
    return outs[:na], outs[na]


def _sum_peers(parts, own, place, steps, tag):
    _, rows, w = parts.shape
    tr = rows // steps

    def body(place_ref, *refs):
        p_refs, own_ref, o_ref = refs[:N_DEV], refs[N_DEV], refs[N_DEV + 1]
        me = place_ref[2]
        acc = None
        for s in range(N_DEV):
            term = jnp.where(me == s, own_ref[...], p_refs[s][...]).astype(F32)
            acc = term if acc is None else acc + term
        o_ref[...] = acc

    def other(s):
        return lambda i, pr: (jnp.where(pr[2] == s, (s + 1) % N_DEV, s), i, 0)

    return pl.pallas_call(
        body,
        name="sum_peers_" + tag,
        grid_spec=pltpu.PrefetchScalarGridSpec(
            num_scalar_prefetch=1,
            grid=(steps,),
            in_specs=[pl.BlockSpec((None, tr, w), other(s)) for s in range(N_DEV)]
            + [pl.BlockSpec((None, None, tr, w), lambda i, pr: (pr[0], pr[1], i, 0))],
            out_specs=pl.BlockSpec((None, tr, w), lambda i, pr: (pr[1], i, 0)),
        ),
        out_shape=jax.ShapeDtypeStruct((2, rows, w), F32),
        compiler_params=_params(("arbitrary",)),
    )(place, *([parts] * N_DEV), _by_halves(own))


def _sum_slots(parts, tag):
    nslot, rows, w = parts.shape

    def body(p_ref, o_ref):
        acc = p_ref[0]
        for k in range(1, nslot):
            acc = acc + p_ref[k]
        o_ref[...] = acc

    return pl.pallas_call(
        body,
        name="sum_slots_" + tag,
        grid=(1,),
        in_specs=[_full((nslot, rows, w))],
        out_specs=_full((rows, w)),
        out_shape=jax.ShapeDtypeStruct((rows, w), F32),
        compiler_params=_params(("arbitrary",)),
    )(parts)


def _pair_gather(bufs):
    np_ = len(bufs)

    def body(*refs):
        srcs, dsts = refs[:np_], refs[np_:2 * np_]
        send, recv = refs[2 * np_:]
        x, y, c, _ = _my_place()
        cps = []
        for p in range(np_):
            cp = _remote(srcs[p].at[c], dsts[p].at[c], send.at[p], recv.at[p], (x, y, 1 - c))
            cp.start()
            cps.append(cp)
        for p, cp in enumerate(cps):
            other = dsts[p].at[1 - c]
            _remote(other, other, send.at[p], recv.at[p], (x, y, 1 - c)).wait_recv()
            cp.wait_send()

    outs = pl.pallas_call(
        body,
        name="pair_gather",
        in_specs=[ANY] * np_,
        out_specs=[ANY] * np_,
        out_shape=[jax.ShapeDtypeStruct(a.shape, F32) for a in bufs],
        input_output_aliases={p: p for p in range(np_)},
        scratch_shapes=[pltpu.SemaphoreType.DMA((np_,))] * 2,
    )(*bufs)
    return [o.reshape(2 * a.shape[1], a.shape[2]) for o, a in zip(outs, bufs)]


def _adamw(gsrc, row0, w, m, v, tr, tag):
    rows, width = w.shape
    off = row0 // tr
    bc1 = 1.0 - ADAM_B1 ** ADAM_STEP
    bc2 = 1.0 - ADAM_B2 ** ADAM_STEP

    def body(g_ref, w_ref, m_ref, v_ref, go_ref, d_ref, mo_ref, vo_ref):
        g = g_ref[...]
        m2 = ADAM_B1 * m_ref[...] + (1.0 - ADAM_B1) * g
        v2 = ADAM_B2 * v_ref[...] + (1.0 - ADAM_B2) * (g * g)
        go_ref[...] = g
        mo_ref[...] = m2
        vo_ref[...] = v2
        d_ref[...] = -ADAM_LR * ((m2 / bc1) / (jnp.sqrt(v2 / bc2) + ADAM_EPS) + ADAM_WD * w_ref[...])

    here = pl.BlockSpec((tr, width), lambda i: (i, 0))
    return pl.pallas_call(
        body,
        name="adamw_" + tag,
        grid=(rows // tr,),
        in_specs=[pl.BlockSpec((tr, width), lambda i: (off + i, 0)), here, here, here],
        out_specs=[here] * 4,
        out_shape=[jax.ShapeDtypeStruct((rows, width), F32)] * 4,
        compiler_params=_params(("arbitrary",)),
    )(gsrc, w, m, v)


def kernel(x, mem, mix_pre_g, w_mix_in, conv_a_w, conv_b_w, conv_b_b, ln_b_g, ln_b_b, w_mix_out, mix_post_g, xa_pre_g, mem_norm_g, w_q, w_k, w_v, w_o, xa_post_g, ffn_pre_g, w_gate, w_up, w_down, ffn_post_g, loss_target, m_mix_pre_g, m_w_mix_in, m_conv_a_w, m_conv_b_w, m_conv_b_b, m_ln_b_g, m_ln_b_b, m_w_mix_out, m_mix_post_g, m_xa_pre_g, m_mem_norm_g, m_w_q, m_w_k, m_w_v, m_w_o, m_xa_post_g, m_ffn_pre_g, m_w_gate, m_w_up, m_w_down, m_ffn_post_g, v_mix_pre_g, v_w_mix_in, v_conv_a_w, v_conv_b_w, v_conv_b_b, v_ln_b_g, v_ln_b_b, v_w_mix_out, v_mix_post_g, v_xa_pre_g, v_mem_norm_g, v_w_q, v_w_k, v_w_v, v_w_o, v_xa_post_g, v_ffn_pre_g, v_w_gate, v_w_up, v_w_down, v_ffn_post_g):
    given = dict(locals())
    names = ["mix_pre_g", "w_mix_in", "conv_a_w", "conv_b_w", "conv_b_b", "ln_b_g", "ln_b_b", "w_mix_out",
             "mix_post_g", "xa_pre_g", "mem_norm_g", "w_q", "w_k", "w_v", "w_o", "xa_post_g", "ffn_pre_g",
             "w_gate", "w_up", "w_down", "ffn_post_g"]
    row = lambda a: a.reshape(1, -1)
    cx, cy, cc = lax.axis_index("x"), lax.axis_index("y"), lax.axis_index("c")
    chip = 2 * cx + cy
    ca_blk = conv_a_w.shape[1]

    conv_rows = CONV_A_W + CONV_B_W
    sw = jnp.concatenate([conv_a_w, conv_b_w, jnp.zeros((SMALL_W_ROWS - conv_rows, ca_blk), F32)], axis=0)
    g1g, pr0g, swg = _gather_weights([w_mix_in.astype(BF16), w_mix_out.astype(BF16), sw])
    conv_full = jnp.transpose(swg[:, :conv_rows, :], (1, 0, 2)).reshape(conv_rows, N_CHIPS * ca_blk)
    wa, wb = conv_full[:CONV_A_W], conv_full[CONV_A_W:]
    pr1 = jnp.concatenate([w_q, w_k, w_v, w_o, w_down], axis=0).astype(BF16)
    g2 = jnp.concatenate([w_gate, w_up], axis=0).astype(BF16)

    xs, ms, tgt = x[0], mem[0], loss_target[0]
    x1, u, o1, z1, pr1g, g2g = _mix_fwd(xs, row(mix_pre_g), row(mix_post_g), wa, wb, row(conv_b_b), row(ln_b_g),
                                        row(ln_b_b), g1g, pr0g, [pr1, g2])
    memn, kb, vb = _mem_kv(ms, row(mem_norm_g), pr1g)
    x2, q, o2 = _xattn_fwd(x1, row(xa_pre_g), row(xa_post_g), kb, vb, pr1g)
    h3, gs, us, do3, dx3, loss_part, d_ffn_post = _ffn_fwd(x2, tgt, row(ffn_pre_g), row(ffn_post_g), g2g, pr1g)

    dh3p, cf1, cf2 = _ffn_bwd(h3, do3, gs, us, g2g, pr1g)
    dx1, d_ffn_pre, d_xa_post, d_xa_pre, dk, dv, cxa, yf1, yf2 = _xattn_bwd(
        dx3, dh3p, x2, x1, o2, q, kb, vb, row(ffn_pre_g), row(xa_post_g), row(xa_pre_g), pr1g, cf1, cf2)
    d_mem_g, cxa = _mem_bwd(dk, dv, ms, memn, row(mem_norm_g), pr1g, cxa)
    dx, d_mix_pre, d_mix_post, dwa, dwb, dbb, dlng, dlnb, cm1, cm2, yx = _mix_bwd(
        dx1, xs, o1, u, z1, row(mix_pre_g), row(mix_post_g), wa, wb, row(ln_b_g), row(ln_b_b), g1g, pr0g, cxa)

    small_parts = [d_mix_pre, dwa, dwb, dbb, dlng, dlnb, d_mix_post, d_xa_pre, d_mem_g, d_xa_post, d_ffn_pre,
                   d_ffn_post, loss_part]
    sizes = [p.size for p in small_parts]
    small = jnp.concatenate([p.reshape(-1) for p in small_parts])
    small_rows = -(-small.size // (8 * 128)) * 8
    small = jnp.pad(small, (0, small_rows * 128 - small.size)).reshape(small_rows, 128)
    (ym1, ym2), small_all = _exchange_last([cm1, cm2], small)
    place = jnp.stack([chip, cc, 2 * chip + cc]).astype(jnp.int32)
    landed = [(yf1, cf1, "gate_up"), (yf2, cf2, "down"), (yx, cxa, "attn"), (ym1, cm1, "mix_in"), (ym2, cm2, "mix_out")]
    r_gu, r_down, r_attn, r_in, r_out = _pair_gather([_sum_peers(y, c, place, 2, t) for y, c, t in landed])
    small_sum = _sum_slots(small_all, "small").reshape(-1)
    red, pos = [], 0
    for p, sz in zip(small_parts, sizes):
        red.append(small_sum[pos:pos + sz].reshape(p.shape))
        pos += sz
    (r_mix_pre, r_wa, r_wb, r_bb, r_lng, r_lnb, r_mix_post, r_xa_pre, r_mem_g, r_xa_post, r_ffn_pre, r_ffn_post,
     r_loss) = red
    loss = r_loss[0, 0]

    res = {}
    big = {"w_mix_out": (r_out, 0, 256), "w_q": (r_attn, GX_Q, 256), "w_k": (r_attn, GX_K, 256),
           "w_v": (r_attn, GX_V, 256), "w_o": (r_attn, GX_O, 256), "w_down": (r_down, 0, 64),
           "w_gate": (r_gu, 0, 256), "w_up": (r_gu, D_MODEL, 256), "w_mix_in": (r_in, 0, 256)}
    for nm, (src, row0, tr) in big.items():
        res[nm] = _adamw(src, row0, given[nm], given["m_" + nm], given["v_" + nm], tr, nm)
    small_grads = {"mix_pre_g": r_mix_pre, "conv_b_b": r_bb, "ln_b_g": r_lng, "ln_b_b": r_lnb,
                   "mix_post_g": r_mix_post, "xa_pre_g": r_xa_pre, "mem_norm_g": r_mem_g, "xa_post_g": r_xa_post,
                   "ffn_pre_g": r_ffn_pre, "ffn_post_g": r_ffn_post,
                   "conv_a_w": lax.dynamic_slice_in_dim(r_wa, chip * ca_blk, ca_blk, axis=1),
                   "conv_b_w": lax.dynamic_slice_in_dim(r_wb, chip * ca_blk, ca_blk, axis=1)}
    small_names = list(small_grads)

    def packed(prefix, grads=None):
        flat = jnp.concatenate([(grads[nm] if grads else given[prefix + nm]).reshape(-1) for nm in small_names])
        rows8 = -(-flat.size // (8 * 128)) * 8
        return jnp.pad(flat, (0, rows8 * 128 - flat.size)).reshape(rows8, 128)

    gp = packed("", small_grads)
    outs = _adamw(gp, 0, packed(""), packed("m_"), packed("v_"), gp.shape[0], "small")
    pos = 0
    for nm in small_names:
        shape = given[nm].shape
        sz = given[nm].size
        res[nm] = [o.reshape(-1)[pos:pos + sz].reshape(shape) for o in outs]
        pos += sz

    return (loss, dx[None], *[res[nm][0] for nm in names], *[res[nm][1] for nm in names],
            *[res[nm][2] for nm in names], *[res[nm][3] for nm in names])
```

```python
import jax
import jax.numpy as jnp
from jax import lax
from jax.experimental import pallas as pl
from jax.experimental.pallas import tpu as pltpu

F32 = jnp.float32
BF16 = jnp.bfloat16
MESH = pl.DeviceIdType.MESH

RMS_EPS = 1e-6
LN_EPS = 1e-5
D_MODEL = 1024
D_A = 512
D_B = 512
D_IN_ALL = 3 * D_A + 2 * D_B
CONV_A_W = 3
CONV_B_W = 31
HALO = 32
XA_HEADS = 4
HEAD_DIM = 256
D_FF = 2816
N_CHIPS = 4
N_DEV = 8
FF_BLK = D_FF // N_CHIPS
IN_BLK = D_IN_ALL // N_CHIPS
ROW_BLK = D_MODEL // N_CHIPS

ADAM_LR = 0.001
ADAM_B1 = 0.9
ADAM_B2 = 0.999
ADAM_EPS = 1e-08
ADAM_WD = 0.01
ADAM_STEP = 10

TILE_FWD = 512
TILE_FFN = 512
TILE_BWD = 256
FFN_BWD_BLOCKS = 2
CONV_ROWS = 64
V7X_VMEM_LIMIT = 56 * 1024 * 1024

P1_Q, P1_K, P1_V, P1_O, P1_DOWN = 0, 256, 512, 768, 1024
P1_ROWS = P1_DOWN + FF_BLK
GX_Q, GX_K, GX_V, GX_O = 0, 256, 512, 768
SMALL_W_ROWS = 48

ANY = pl.BlockSpec(memory_space=pl.ANY)


def _mm(a, b):
    return lax.dot_general(a, b, (((1,), (0,)), ((), ())), preferred_element_type=F32)


def _mm_nt(a, b):
    return lax.dot_general(a, b, (((1,), (1,)), ((), ())), preferred_element_type=F32)


def _mm_tn(a, b):
    return lax.dot_general(a, b, (((0,), (0,)), ((), ())), preferred_element_type=F32)


def _sigmoid(x):
    return 0.5 * jnp.tanh(0.5 * x) + 0.5


def _rms(x):
    r = lax.rsqrt(jnp.mean(x * x, axis=-1, keepdims=True) + RMS_EPS)
    return x * r, r


def _rms_bwd(dy, xn, r, g):
    gdy = dy * g
    dx = r * (gdy - xn * jnp.mean(gdy * xn, axis=-1, keepdims=True))
    return dx, jnp.sum(dy * xn, axis=0, keepdims=True)


def _full(shape):
    return pl.BlockSpec(shape, lambda *_: (0,) * len(shape))


def _params(sem=None):
    return pltpu.CompilerParams(dimension_semantics=sem, vmem_limit_bytes=V7X_VMEM_LIMIT)


def _load_rows(dst, src_hbm, row0, rows):
    for k in range(N_CHIPS):
        pltpu.sync_copy(src_hbm.at[k, pl.ds(row0, rows), :], dst.at[pl.ds(rows * k, rows), :])


def _load_cols(dst, src_hbm, cols):
    for k in range(N_CHIPS):
        pltpu.sync_copy(src_hbm.at[k], dst.at[:, pl.ds(cols * k, cols)])


def _fill_phases(src, sh, nrows):
    for r in range(1, 8):
        sh[r, pl.ds(0, nrows), :] = src[pl.ds(r, nrows), pl.ds(D_A, D_B)]


def _phase_rows(src, sh, off, start, size):
    r = off % 8
    if r == 0:
        return src[pl.ds(off + start, size), pl.ds(D_A, D_B)]
    return sh[r, pl.ds(off - r + start, size), :]


def _my_place():
    x, y, c = lax.axis_index("x"), lax.axis_index("y"), lax.axis_index("c")
    return x, y, c, ((1 - x, y), (x, 1 - y), (1 - x, 1 - y))


def _remote(src, dst, send_sem, recv_sem, to):
    return pltpu.make_async_remote_copy(src_ref=src, dst_ref=dst, send_sem=send_sem, recv_sem=recv_sem,
                                        device_id=to, device_id_type=MESH)


def _gather_sems(np_):
    return [pltpu.SemaphoreType.DMA((np_, 3))] * 4 + [pltpu.SemaphoreType.DMA((np_,))] * 2


def _gather_start(srcs, dsts, sems):
    send, recv, _, _, osend, orecv = sems
    x, y, c, chips = _my_place()
    j = 2 * x + y
    for p in range(len(srcs)):
        _remote(srcs[p], dsts[p].at[j], osend.at[p], orecv.at[p], (x, y, 1 - c)).start()
        for nn, (kx, ky) in enumerate(chips):
            _remote(srcs[p].at[c], dsts[p].at[j, c], send.at[p, nn], recv.at[p, nn], (kx, ky, c)).start()


def _gather_forward(srcs, dsts, sems):
    send, recv, fsend, frecv, _, _ = sems
    x, y, c, chips = _my_place()
    for nn, (kx, ky) in enumerate(chips):
        for p in range(len(srcs)):
            blk = dsts[p].at[2 * kx + ky, c]
            _remote(blk, blk, send.at[p, nn], recv.at[p, nn], (kx, ky, c)).wait_recv()
            _remote(blk, blk, fsend.at[p, nn], frecv.at[p, nn], (x, y, 1 - c)).start()


def _gather_finish(srcs, dsts, sems):
    send, recv, fsend, frecv, osend, orecv = sems
    x, y, c, chips = _my_place()
    j = 2 * x + y
    for nn, (kx, ky) in enumerate(chips):
        for p in range(len(srcs)):
            other = dsts[p].at[2 * kx + ky, 1 - c]
            _remote(other, other, fsend.at[p, nn], frecv.at[p, nn], (x, y, 1 - c)).wait_recv()
    for nn, (kx, ky) in enumerate(chips):
        for p in range(len(srcs)):
            _remote(srcs[p].at[c], dsts[p].at[j, c], send.at[p, nn], recv.at[p, nn], (kx, ky, c)).wait_send()
            blk = dsts[p].at[2 * kx + ky, c]
            _remote(blk, blk, fsend.at[p, nn], frecv.at[p, nn], (x, y, 1 - c)).wait_send()
    for p in range(len(srcs)):
        _remote(srcs[p], dsts[p].at[j], osend.at[p], orecv.at[p], (x, y, 1 - c)).wait()


def _split_halves(a):
    return a.reshape(2, a.shape[0] // 2, a.shape[1])


def _gather_weights(packs):
    np_ = len(packs)
    split = [_split_halves(a) for a in packs]

    def body(*refs):
        srcs, dsts, sems = refs[:np_], refs[np_:2 * np_], refs[2 * np_:]
        _gather_start(srcs, dsts, sems)
        _gather_forward(srcs, dsts, sems)
        _gather_finish(srcs, dsts, sems)

    outs = pl.pallas_call(
        body,
        name="gather_weights",
        in_specs=[ANY] * np_,
        out_specs=[ANY] * np_,
        out_shape=[jax.ShapeDtypeStruct((N_CHIPS,) + a.shape, a.dtype) for a in split],
        scratch_shapes=_gather_sems(np_),
    )(*split)
    return [o.reshape((N_CHIPS,) + a.shape) for o, a in zip(outs, packs)]


def _peers(x, y, c):
    out = []
    for d in range(1, N_DEV):
        px = 1 - x if d & 4 else x
        py = 1 - y if d & 2 else y
        pc = 1 - c if d & 1 else c
        out.append(((px, py, pc), 4 * px + 2 * py + pc))
    return out


def _scatter_copies(srcs, dsts, send, recv):
    x, y, c, _ = _my_place()
    me = 4 * x + 2 * y + c
    out = []
    for a in range(len(srcs)):
        for d, ((px, py, pc), pidx) in enumerate(_peers(x, y, c)):
            piece = srcs[a].at[2 * px + py, pc]
            out.append((_remote(piece, dsts[a].at[me], send.at[a, d], recv.at[a, d], (px, py, pc)),
                        _remote(piece, dsts[a].at[pidx], send.at[a, d], recv.at[a, d], (px, py, pc))))
    return out


def _scatter_start(srcs, dsts, send, recv):
    for out_cp, _ in _scatter_copies(srcs, dsts, send, recv):
        out_cp.start()


def _scatter_wait(srcs, dsts, send, recv):
    for out_cp, in_cp in _scatter_copies(srcs, dsts, send, recv):
        in_cp.wait_recv()
        out_cp.wait_send()


def _scatter_sems(na):
    return [pltpu.SemaphoreType.DMA((na, N_DEV - 1))] * 2


def _slots(a):
    return jax.ShapeDtypeStruct((N_DEV,) + a.shape[2:], a.dtype)


def _by_halves(a):
    return a.reshape(a.shape[0], 2, a.shape[1] // 2, a.shape[2])


def _mix_fwd(x, gpre, gpost, wa, wb, bb, lng, lnb, g1g, pr0g, late):
    S, D = x.shape
    T = min(TILE_FWD, S)
    n = S // T
    nl = len(late)
    late_split = [_split_halves(a) for a in late]

    def body(*refs):
        (x_ref, gpre_ref, gpost_ref, wa_ref, wb_ref, bb_ref, lng_ref, lnb_ref, g1g_hbm, pr0g_hbm) = refs[:10]
        srcs = refs[10:10 + nl]
        x1_ref, u_ref, o1_ref, z1_ref = refs[10 + nl:14 + nl]
        dsts = refs[14 + nl:14 + 2 * nl]
        win_v, wout_v, ext, sh, z1buf = refs[14 + 2 * nl:19 + 2 * nl]
        sems = refs[19 + 2 * nl:]
        i = pl.program_id(0)

        @pl.when(i == 0)
        def _():
            _gather_start(srcs, dsts, sems)
            _load_cols(win_v, g1g_hbm, IN_BLK)
            _load_rows(wout_v, pr0g_hbm, 0, ROW_BLK)
            ext[pl.ds(0, HALO), :] = jnp.zeros((HALO, D_A + D_B), F32)

        @pl.when(i == max(n - 2, 0))
        def _():
            _gather_forward(srcs, dsts, sems)

        xv = x_ref[...]
        xn, _ = _rms(xv)
        h = (xn * gpre_ref[...]).astype(BF16)
        u = _mm(h, win_v[...])
        u_ref[...] = u.astype(BF16)
        b_a = u[:, 0:D_A]
        cv = u[:, D_A:2 * D_A] * u[:, 2 * D_A:3 * D_A]
        z0 = u[:, 3 * D_A:3 * D_A + D_B] * _sigmoid(u[:, 3 * D_A + D_B:])
        ext[pl.ds(HALO, T), pl.ds(0, D_A)] = cv
        ext[pl.ds(HALO, T), pl.ds(D_A, D_B)] = z0

        conv_a = ext[pl.ds(HALO - 2, T), pl.ds(0, D_A)] * wa_ref[0:1, :]
        for k in range(1, CONV_A_W):
            conv_a = conv_a + ext[pl.ds(HALO - 2 + k, T), pl.ds(0, D_A)] * wa_ref[k:k + 1, :]
        y_a = b_a * conv_a

        _fill_phases(ext, sh, T + HALO - 8)
        base = HALO - (CONV_B_W - 1)

        def chunk(ci, carry):
            start = pl.multiple_of(ci * CONV_ROWS, 8)
            acc = jnp.broadcast_to(bb_ref[...], (CONV_ROWS, D_B))
            for k in range(CONV_B_W):
                acc = acc + _phase_rows(ext, sh, base + k, start, CONV_ROWS) * wb_ref[k:k + 1, :]
            z1buf[pl.ds(start, CONV_ROWS), :] = acc
            return carry

        lax.fori_loop(0, T // CONV_ROWS, chunk, 0)
        z1 = z1buf[...]
        z1_ref[...] = z1.astype(BF16)
        mu = jnp.mean(z1, axis=-1, keepdims=True)
        zc = z1 - mu
        rstd = lax.rsqrt(jnp.mean(zc * zc, axis=-1, keepdims=True) + LN_EPS)
        l = zc * rstd * lng_ref[...] + lnb_ref[...]
        y_b = l * _sigmoid(l)
        y = jnp.concatenate([y_a, y_b], axis=-1).astype(BF16)
        o1 = _mm(y, wout_v[...])
        o1_ref[...] = o1.astype(BF16)
        o1n, _ = _rms(o1)
        x1_ref[...] = xv + o1n * gpost_ref[...]
        ext[pl.ds(0, HALO), :] = ext[pl.ds(T, HALO), :]

        @pl.when(i == n - 1)
        def _():
            _gather_finish(srcs, dsts, sems)

    tok = lambda w: pl.BlockSpec((T, w), lambda i: (i, 0))
    outs = pl.pallas_call(
        body,
        name="mix_fwd",
        grid=(n,),
        in_specs=[tok(D), _full((1, D)), _full((1, D)), _full((CONV_A_W, D_A)), _full((CONV_B_W, D_B)),
                  _full((1, D_B)), _full((1, D_B)), _full((1, D_B)), ANY, ANY] + [ANY] * nl,
        out_specs=[tok(D), tok(D_IN_ALL), tok(D), tok(D_B)] + [ANY] * nl,
        out_shape=[jax.ShapeDtypeStruct((S, D), F32), jax.ShapeDtypeStruct((S, D_IN_ALL), BF16),
                   jax.ShapeDtypeStruct((S, D), BF16), jax.ShapeDtypeStruct((S, D_B), BF16)]
        + [jax.ShapeDtypeStruct((N_CHIPS,) + a.shape, a.dtype) for a in late_split],
        scratch_shapes=[pltpu.VMEM((D, D_IN_ALL), BF16), pltpu.VMEM((D_A + D_B, D), BF16),
                        pltpu.VMEM((HALO + T, D_A + D_B), F32), pltpu.VMEM((8, HALO + T, D_B), F32),
                        pltpu.VMEM((T, D_B), F32)] + _gather_sems(nl),
        compiler_params=_params(("arbitrary",)),
    )(x, gpre, gpost, wa, wb, bb, lng, lnb, g1g, pr0g, *late_split)
    return list(outs[:4]) + [o.reshape((N_CHIPS,) + a.shape) for o, a in zip(outs[4:], late)]


def _mix_bwd(dx1, x, o1, u, z1s, gpre, gpost, wa, wb, lng, lnb, g1g, pr0g, cx):
    S, D = x.shape
    T = min(TILE_BWD, S)
    n = S // T
    hb = T // HALO
    cxs = _by_halves(cx)

    def body(dx1_ref, x_ref, o1_ref, u_ref, uh_ref, z1_ref, gpre_ref, gpost_ref, wa_ref, wb_ref, lng_ref, lnb_ref,
             g1g_hbm, pr0g_hbm, cx_hbm,
             dx_ref, dgpre_ref, dgpost_ref, dwa_ref, dwb_ref, dbb_ref, dlng_ref, dlnb_ref, cm1_hbm, cm2_hbm, yx_hbm,
             win_v, wout_v, dwin_acc, dwout_acc, ext, ext2, shf, shb, dz0buf, dwb_acc, send, recv):
        i = pl.program_id(0)

        @pl.when(i == 0)
        def _():
            _scatter_start([cx_hbm], [yx_hbm], send, recv)
            _load_cols(win_v, g1g_hbm, IN_BLK)
            _load_rows(wout_v, pr0g_hbm, 0, ROW_BLK)
            dwin_acc[...] = jnp.zeros_like(dwin_acc)
            dwout_acc[...] = jnp.zeros_like(dwout_acc)
            dwb_acc[...] = jnp.zeros_like(dwb_acc)
            ext2[pl.ds(T, HALO), :] = jnp.zeros((HALO, D_A + D_B), F32)
            for ref in (dgpre_ref, dgpost_ref, dwa_ref, dbb_ref, dlng_ref, dlnb_ref):
                ref[...] = jnp.zeros_like(ref)

        o1n, r1 = _rms(o1_ref[...].astype(F32))
        dx1v = dx1_ref[...]
        d_o1, dgp = _rms_bwd(dx1v, o1n, r1, gpost_ref[...])
        dgpost_ref[...] += dgp
        d_o1b = d_o1.astype(BF16)
        dy = _mm_nt(d_o1b, wout_v[...])

        first = (i == n - 1).astype(F32)
        uh = uh_ref[...].astype(F32) * (1.0 - first)
        ext[pl.ds(0, HALO), pl.ds(0, D_A)] = uh[:, D_A:2 * D_A] * uh[:, 2 * D_A:3 * D_A]
        ext[pl.ds(0, HALO), pl.ds(D_A, D_B)] = uh[:, 3 * D_A:3 * D_A + D_B] * _sigmoid(uh[:, 3 * D_A + D_B:])
        uf = u_ref[...].astype(F32)
        b_a = uf[:, 0:D_A]
        c_a = uf[:, D_A:2 * D_A]
        v_a = uf[:, 2 * D_A:3 * D_A]
        gv = uf[:, 3 * D_A:3 * D_A + D_B]
        sg = _sigmoid(uf[:, 3 * D_A + D_B:])
        ext[pl.ds(HALO, T), pl.ds(0, D_A)] = c_a * v_a
        ext[pl.ds(HALO, T), pl.ds(D_A, D_B)] = gv * sg
        conv_a = ext[pl.ds(HALO - 2, T), pl.ds(0, D_A)] * wa_ref[0:1, :]
        for k in range(1, CONV_A_W):
            conv_a = conv_a + ext[pl.ds(HALO - 2 + k, T), pl.ds(0, D_A)] * wa_ref[k:k + 1, :]
        z1 = z1_ref[...].astype(F32)
        mu = jnp.mean(z1, axis=-1, keepdims=True)
        zc = z1 - mu
        rstd = lax.rsqrt(jnp.mean(zc * zc, axis=-1, keepdims=True) + LN_EPS)
        zn = zc * rstd
        l = zn * lng_ref[...] + lnb_ref[...]
        sl = _sigmoid(l)
        y = jnp.concatenate([b_a * conv_a, l * sl], axis=-1).astype(BF16)
        dwout_acc[...] += _mm_tn(y, d_o1b)

        dy_a = dy[:, 0:D_A]
        dl = dy[:, D_A:] * (sl * (1.0 + l * (1.0 - sl)))
        dlng_ref[...] += jnp.sum(dl * zn, axis=0, keepdims=True)
        dlnb_ref[...] += jnp.sum(dl, axis=0, keepdims=True)
        dzn = dl * lng_ref[...]
        dz1 = rstd * (dzn - jnp.mean(dzn, axis=-1, keepdims=True) - zn * jnp.mean(dzn * zn, axis=-1, keepdims=True))
        dbb_ref[...] += jnp.sum(dz1, axis=0, keepdims=True)
        d_conv = dy_a * b_a
        ext2[pl.ds(0, T), pl.ds(0, D_A)] = d_conv
        ext2[pl.ds(0, T), pl.ds(D_A, D_B)] = dz1

        d_cv = ext2[pl.ds(CONV_A_W - 1, T), pl.ds(0, D_A)] * wa_ref[0:1, :]
        for k in range(1, CONV_A_W):
            d_cv = d_cv + ext2[pl.ds(CONV_A_W - 1 - k, T), pl.ds(0, D_A)] * wa_ref[k:k + 1, :]
        for k in range(CONV_A_W):
            dwa_ref[k:k + 1, :] += jnp.sum(d_conv * ext[pl.ds(HALO - 2 + k, T), pl.ds(0, D_A)], axis=0, keepdims=True)

        _fill_phases(ext, shf, T + HALO - 8)
        _fill_phases(ext2, shb, T + HALO - 8)
        base = HALO - (CONV_B_W - 1)

        def chunk(ci, carry):
            start = pl.multiple_of(ci * CONV_ROWS, 8)
            dzc = ext2[pl.ds(start, CONV_ROWS), pl.ds(D_A, D_B)]
            acc = jnp.zeros((CONV_ROWS, D_B), F32)
            for k in range(CONV_B_W):
                wk = wb_ref[k:k + 1, :]
                acc = acc + _phase_rows(ext2, shb, CONV_B_W - 1 - k, start, CONV_ROWS) * wk
                prod = dzc * _phase_rows(ext, shf, base + k, start, CONV_ROWS)
                part = prod[0:8, :]
                for m in range(1, CONV_ROWS // 8):
                    part = part + prod[8 * m:8 * m + 8, :]
                dwb_acc[k] += part
            dz0buf[pl.ds(start, CONV_ROWS), :] = acc
            return carry

        lax.fori_loop(0, T // CONV_ROWS, chunk, 0)
        dz0 = dz0buf[...]
        du = jnp.concatenate([dy_a * conv_a, d_cv * v_a, d_cv * c_a, dz0 * sg, dz0 * gv * sg * (1.0 - sg)],
                             axis=-1).astype(BF16)
        dh = _mm_nt(du, win_v[...])
        xv = x_ref[...]
        xn, r0 = _rms(xv)
        dwin_acc[...] += _mm_tn((xn * gpre_ref[...]).astype(BF16), du)
        dxp, dg0 = _rms_bwd(dh, xn, r0, gpre_ref[...])
        dgpre_ref[...] += dg0
        dx_ref[...] = dx1v + dxp
        ext2[pl.ds(T, HALO), :] = ext2[pl.ds(0, HALO), :]

        @pl.when(i == n - 1)
        def _():
            for k in range(CONV_B_W):
                dwb_ref[k:k + 1, :] = jnp.sum(dwb_acc[k], axis=0, keepdims=True)
            win_v[...] = dwin_acc[...].astype(BF16)
            wout_v[...] = dwout_acc[...].astype(BF16)
            for k in range(N_CHIPS):
                pltpu.sync_copy(win_v.at[:, pl.ds(IN_BLK * k, IN_BLK)], cm1_hbm.at[k])
                pltpu.sync_copy(wout_v.at[pl.ds(ROW_BLK * k, ROW_BLK), :], cm2_hbm.at[k])
            _scatter_wait([cx_hbm], [yx_hbm], send, recv)

    rev = lambda w: pl.BlockSpec((T, w), lambda i: (n - 1 - i, 0))
    halo = pl.BlockSpec((HALO, D_IN_ALL), lambda i: (jnp.maximum((n - 1 - i) * hb - 1, 0), 0))
    return pl.pallas_call(
        body,
        name="mix_bwd",
        grid=(n,),
        in_specs=[rev(D), rev(D), rev(D), rev(D_IN_ALL), halo, rev(D_B), _full((1, D)), _full((1, D)),
                  _full((CONV_A_W, D_A)), _full((CONV_B_W, D_B)), _full((1, D_B)), _full((1, D_B)), ANY, ANY, ANY],
        out_specs=[rev(D), _full((1, D)), _full((1, D)), _full((CONV_A_W, D_A)), _full((CONV_B_W, D_B)),
                   _full((1, D_B)), _full((1, D_B)), _full((1, D_B)), ANY, ANY, ANY],
        out_shape=[jax.ShapeDtypeStruct((S, D), F32), jax.ShapeDtypeStruct((1, D), F32),
                   jax.ShapeDtypeStruct((1, D), F32), jax.ShapeDtypeStruct((CONV_A_W, D_A), F32),
                   jax.ShapeDtypeStruct((CONV_B_W, D_B), F32), jax.ShapeDtypeStruct((1, D_B), F32),
                   jax.ShapeDtypeStruct((1, D_B), F32), jax.ShapeDtypeStruct((1, D_B), F32),
                   jax.ShapeDtypeStruct((N_CHIPS, D, IN_BLK), BF16),
                   jax.ShapeDtypeStruct((N_CHIPS, ROW_BLK, D), BF16), _slots(cxs)],
        scratch_shapes=[pltpu.VMEM((D, D_IN_ALL), BF16), pltpu.VMEM((D_A + D_B, D), BF16),
                        pltpu.VMEM((D, D_IN_ALL), F32), pltpu.VMEM((D_A + D_B, D), F32),
                        pltpu.VMEM((HALO + T, D_A + D_B), F32), pltpu.VMEM((HALO + T, D_A + D_B), F32),
                        pltpu.VMEM((8, HALO + T, D_B), F32), pltpu.VMEM((8, HALO + T, D_B), F32),
                        pltpu.VMEM((T, D_B), F32), pltpu.VMEM((CONV_B_W, 8, D_B), F32)] + _scatter_sems(1),
        compiler_params=_params(("arbitrary",)),
    )(dx1, x, o1, u, u, z1s, gpre, gpost, wa, wb, lng, lnb, g1g, pr0g, cxs)


def _mem_kv(mem, gmem, pr1g):
    M, D = mem.shape

    def body(mem_ref, g_ref, pr1g_hbm, memn_ref, k_ref, v_ref, wk_v, wv_v):
        _load_rows(wk_v, pr1g_hbm, P1_K, ROW_BLK)
        _load_rows(wv_v, pr1g_hbm, P1_V, ROW_BLK)
        mn, _ = _rms(mem_ref[...])
        mb = (mn * g_ref[...]).astype(BF16)
        memn_ref[...] = mb
        k_ref[...] = _mm(mb, wk_v[...]).astype(BF16)
        v_ref[...] = _mm(mb, wv_v[...]).astype(BF16)

    return pl.pallas_call(
        body,
        name="mem_kv",
        grid=(1,),
        in_specs=[_full((M, D)), _full((1, D)), ANY],
        out_specs=[_full((M, D))] * 3,
        out_shape=[jax.ShapeDtypeStruct((M, D), BF16)] * 3,
        scratch_shapes=[pltpu.VMEM((D, D), BF16), pltpu.VMEM((D, D), BF16)],
        compiler_params=_params(("arbitrary",)),
    )(mem, gmem, pr1g)


def _attend(qb, kb, vb):
    scale = HEAD_DIM ** -0.5
    ps, os_ = [], []
    for hd in range(XA_HEADS):
        cols = slice(HEAD_DIM * hd, HEAD_DIM * (hd + 1))
        s = _mm_nt(qb[:, cols], kb[:, cols]) * scale
        e = jnp.exp(s - jnp.max(s, axis=-1, keepdims=True))
        p = e * (1.0 / jnp.sum(e, axis=-1, keepdims=True))
        ps.append(p)
        os_.append(_mm(p.astype(BF16), vb[:, cols]))
    return ps, jnp.concatenate(os_, axis=-1).astype(BF16)


def _xattn_fwd(x1, gpre, gpost, kb, vb, pr1g):
    S, D = x1.shape
    M = kb.shape[0]
    T = min(TILE_FWD, S)
    n = S // T

    def body(x1_ref, gpre_ref, gpost_ref, k_ref, v_ref, pr1g_hbm, x2_ref, q_ref, o2_ref, wq_v, wo_v):
        @pl.when(pl.program_id(0) == 0)
        def _():
            _load_rows(wq_v, pr1g_hbm, P1_Q, ROW_BLK)
            _load_rows(wo_v, pr1g_hbm, P1_O, ROW_BLK)

        xv = x1_ref[...]
        xn, _ = _rms(xv)
        qb = _mm((xn * gpre_ref[...]).astype(BF16), wq_v[...]).astype(BF16)
        q_ref[...] = qb
        _, ob = _attend(qb, k_ref[...], v_ref[...])
        o2 = _mm(ob, wo_v[...])
        o2_ref[...] = o2.astype(BF16)
        o2n, _ = _rms(o2)
        x2_ref[...] = xv + o2n * gpost_ref[...]

    tok = lambda w: pl.BlockSpec((T, w), lambda i: (i, 0))
    return pl.pallas_call(
        body,
        name="xattn_fwd",
        grid=(n,),
        in_specs=[tok(D), _full((1, D)), _full((1, D)), _full((M, D)), _full((M, D)), ANY],
        out_specs=[tok(D), tok(D), tok(D)],
        out_shape=[jax.ShapeDtypeStruct((S, D), F32), jax.ShapeDtypeStruct((S, D), BF16),
                   jax.ShapeDtypeStruct((S, D), BF16)],
        scratch_shapes=[pltpu.VMEM((D, D), BF16), pltpu.VMEM((D, D), BF16)],
        compiler_params=_params(("arbitrary",)),
    )(x1, gpre, gpost, kb, vb, pr1g)


def _xattn_bwd(dx3, dh3p, x2, x1, o2, q, kb, vb, gffn, gpost, gpre, pr1g, cf1, cf2):
    S, D = x1.shape
    M = kb.shape[0]
    T = min(TILE_BWD, S)
    n = S // T
    scale = HEAD_DIM ** -0.5
    nparts = dh3p.shape[0]
    cfs = [_by_halves(cf1), _by_halves(cf2)]

    def body(*refs):
        dx3_ref, dh3_refs = refs[0], refs[1:1 + nparts]
        (x2_ref, x1_ref, o2_ref, q_ref, k_ref, v_ref, gffn_ref, gpost_ref, gpre_ref, pr1g_hbm, cf1_hbm, cf2_hbm,
         dx1_ref, dgffn_ref, dgpost_ref, dgpre_ref, dk_ref, dv_ref, cx_hbm, yf1_hbm, yf2_hbm,
         wq_v, wo_v, dwq_acc, dwo_acc, send, recv) = refs[1 + nparts:]
        i = pl.program_id(0)

        @pl.when(i == 0)
        def _():
            _scatter_start([cf1_hbm, cf2_hbm], [yf1_hbm, yf2_hbm], send, recv)
            _load_rows(wq_v, pr1g_hbm, P1_Q, ROW_BLK)
            _load_rows(wo_v, pr1g_hbm, P1_O, ROW_BLK)
            dwq_acc[...] = jnp.zeros_like(dwq_acc)
            dwo_acc[...] = jnp.zeros_like(dwo_acc)
            for ref in (dgffn_ref, dgpost_ref, dgpre_ref, dk_ref, dv_ref):
                ref[...] = jnp.zeros_like(ref)

        x2n, r2 = _rms(x2_ref[...])
        dh3 = dh3_refs[0][...].astype(F32)
        for ref in dh3_refs[1:]:
            dh3 = dh3 + ref[...].astype(F32)
        dxp, dg = _rms_bwd(dh3, x2n, r2, gffn_ref[...])
        dgffn_ref[...] += dg
        dx2 = dx3_ref[...] + dxp
        o2n, ro = _rms(o2_ref[...].astype(F32))
        d_o2, dg = _rms_bwd(dx2, o2n, ro, gpost_ref[...])
        dgpost_ref[...] += dg
        d_o2b = d_o2.astype(BF16)
        d_o = _mm_nt(d_o2b, wo_v[...]).astype(BF16)
        qb = q_ref[...]
        kv = k_ref[...]
        vv = v_ref[...]
        ps, ob = _attend(qb, kv, vv)
        dwo_acc[...] += _mm_tn(ob, d_o2b)
        dqs = []
        for hd in range(XA_HEADS):
            cols = slice(HEAD_DIM * hd, HEAD_DIM * (hd + 1))
            p = ps[hd]
            dp = _mm_nt(d_o[:, cols], vv[:, cols])
            dv_ref[:, cols] += _mm_tn(p.astype(BF16), d_o[:, cols])
            ds = (p * (dp - jnp.sum(p * dp, axis=-1, keepdims=True)) * scale).astype(BF16)
            dqs.append(_mm(ds, kv[:, cols]))
            dk_ref[:, cols] += _mm_tn(ds, qb[:, cols])
        dq = jnp.concatenate(dqs, axis=-1).astype(BF16)
        dh2 = _mm_nt(dq, wq_v[...])
        x1n, r1 = _rms(x1_ref[...])
        dwq_acc[...] += _mm_tn((x1n * gpre_ref[...]).astype(BF16), dq)
        dxp, dg = _rms_bwd(dh2, x1n, r1, gpre_ref[...])
        dgpre_ref[...] += dg
        dx1_ref[...] = dx2 + dxp

        @pl.when(i == n - 1)
        def _():
            wq_v[...] = dwq_acc[...].astype(BF16)
            wo_v[...] = dwo_acc[...].astype(BF16)
            for k in range(N_CHIPS):
                rows = pl.ds(ROW_BLK * k, ROW_BLK)
                pltpu.sync_copy(wq_v.at[rows, :], cx_hbm.at[k, pl.ds(GX_Q, ROW_BLK), :])
                pltpu.sync_copy(wo_v.at[rows, :], cx_hbm.at[k, pl.ds(GX_O, ROW_BLK), :])
            _scatter_wait([cf1_hbm, cf2_hbm], [yf1_hbm, yf2_hbm], send, recv)

    tok = lambda w: pl.BlockSpec((T, w), lambda i: (i, 0))
    part = lambda j: pl.BlockSpec((None, T, D), lambda i: (j, i, 0))
    return pl.pallas_call(
        body,
        name="xattn_bwd",
        grid=(n,),
        in_specs=[tok(D)] + [part(j) for j in range(nparts)] + [tok(D), tok(D), tok(D), tok(D), _full((M, D)),
                                                                 _full((M, D)), _full((1, D)), _full((1, D)),
                                                                 _full((1, D)), ANY, ANY, ANY],
        out_specs=[tok(D), _full((1, D)), _full((1, D)), _full((1, D)), _full((M, D)), _full((M, D)), ANY, ANY, ANY],
        out_shape=[jax.ShapeDtypeStruct((S, D), F32), jax.ShapeDtypeStruct((1, D), F32),
                   jax.ShapeDtypeStruct((1, D), F32), jax.ShapeDtypeStruct((1, D), F32),
                   jax.ShapeDtypeStruct((M, D), F32), jax.ShapeDtypeStruct((M, D), F32),
                   jax.ShapeDtypeStruct((N_CHIPS, D, D), BF16), _slots(cfs[0]), _slots(cfs[1])],
        scratch_shapes=[pltpu.VMEM((D, D), BF16), pltpu.VMEM((D, D), BF16),
                        pltpu.VMEM((D, D), F32), pltpu.VMEM((D, D), F32)] + _scatter_sems(2),
        compiler_params=_params(("arbitrary",)),
    )(dx3, *([dh3p] * nparts), x2, x1, o2, q, kb, vb, gffn, gpost, gpre, pr1g, *cfs)


def _mem_bwd(dk, dv, mem, memn, gmem, pr1g, cx_in):
    M, D = mem.shape

    def body(dk_ref, dv_ref, mem_ref, memn_ref, g_ref, pr1g_hbm, cx_hbm, dg_ref, cx_out, wk_v, wv_v):
        del cx_hbm
        _load_rows(wk_v, pr1g_hbm, P1_K, ROW_BLK)
        _load_rows(wv_v, pr1g_hbm, P1_V, ROW_BLK)
        dkb = dk_ref[...].astype(BF16)
        dvb = dv_ref[...].astype(BF16)
        mb = memn_ref[...]
        dmn = _mm_nt(dkb, wk_v[...]) + _mm_nt(dvb, wv_v[...])
        mn, _ = _rms(mem_ref[...])
        dg_ref[...] = jnp.sum(dmn * mn, axis=0, keepdims=True)
        wk_v[...] = _mm_tn(mb, dkb).astype(BF16)
        wv_v[...] = _mm_tn(mb, dvb).astype(BF16)
        for k in range(N_CHIPS):
            rows = pl.ds(ROW_BLK * k, ROW_BLK)
            pltpu.sync_copy(wk_v.at[rows, :], cx_out.at[k, pl.ds(GX_K, ROW_BLK), :])
            pltpu.sync_copy(wv_v.at[rows, :], cx_out.at[k, pl.ds(GX_V, ROW_BLK), :])

    return pl.pallas_call(
        body,
        name="mem_bwd",
        grid=(1,),
        in_specs=[_full((M, D)), _full((M, D)), _full((M, D)), _full((M, D)), _full((1, D)), ANY, ANY],
        out_specs=[_full((1, D)), ANY],
        out_shape=[jax.ShapeDtypeStruct((1, D), F32), jax.ShapeDtypeStruct(cx_in.shape, BF16)],
        input_output_aliases={6: 1},
        scratch_shapes=[pltpu.VMEM((D, D), BF16), pltpu.VMEM((D, D), BF16)],
        compiler_params=_params(("arbitrary",)),
    )(dk, dv, mem, memn, gmem, pr1g, cx_in)


def _ffn_fwd(x2, target, gpre, gpost, g2g, pr1g):
    S, D = x2.shape
    T = min(TILE_FFN, S)
    n = S // T

    def body(x2_ref, t_ref, gpre_ref, gpost_ref, g2g_hbm, pr1g_hbm,
             h3_ref, g_hbm, u_hbm, do3_ref, dx3_ref, loss_ref, dgpost_ref, wg_v, wu_v, wd_v, gst, ust, sem):
        i = pl.program_id(0)

        @pl.when(i == 0)
        def _():
            pltpu.sync_copy(g2g_hbm.at[:, pl.ds(0, D), :], wg_v)
            pltpu.sync_copy(g2g_hbm.at[:, pl.ds(D, D), :], wu_v)
            pltpu.sync_copy(pr1g_hbm.at[:, pl.ds(P1_DOWN, FF_BLK), :], wd_v)
            loss_ref[...] = jnp.zeros_like(loss_ref)
            dgpost_ref[...] = jnp.zeros_like(dgpost_ref)

        xv = x2_ref[...]
        xn, _ = _rms(xv)
        hb = (xn * gpre_ref[...]).astype(BF16)
        h3_ref[...] = hb
        o3 = jnp.zeros((T, D), F32)
        out = [None, None]
        for c in range(N_CHIPS):
            slot = c % 2
            if out[slot] is not None:
                for cp in out[slot]:
                    cp.wait()
            g = _mm(hb, wg_v[c])
            u = _mm(hb, wu_v[c])
            gst[slot] = g.astype(BF16)
            ust[slot] = u.astype(BF16)
            out[slot] = (pltpu.make_async_copy(gst.at[slot], g_hbm.at[c, i], sem.at[0, slot]),
                         pltpu.make_async_copy(ust.at[slot], u_hbm.at[c, i], sem.at[1, slot]))
            for cp in out[slot]:
                cp.start()
            o3 = o3 + _mm((g * _sigmoid(g) * u).astype(BF16), wd_v[c])
        for pair in out:
            for cp in pair:
                cp.wait()
        o3n, r3 = _rms(o3)
        diff = xv + o3n * gpost_ref[...] - t_ref[...]
        sq = jnp.sum(jnp.sum(diff * diff, axis=-1, keepdims=True), axis=0, keepdims=True)
        loss_ref[...] += sq * (0.5 / D)
        dx3 = diff * (1.0 / D)
        dx3_ref[...] = dx3
        d_o3, dg = _rms_bwd(dx3, o3n, r3, gpost_ref[...])
        dgpost_ref[...] += dg
        do3_ref[...] = d_o3.astype(BF16)

    tok = lambda w: pl.BlockSpec((T, w), lambda i: (i, 0))
    h3, gs, us, do3, dx3, loss, dgpost = pl.pallas_call(
        body,
        name="ffn_fwd",
        grid=(n,),
        in_specs=[tok(D), tok(D), _full((1, D)), _full((1, D)), ANY, ANY],
        out_specs=[tok(D), ANY, ANY, tok(D), tok(D), _full((1, 128)), _full((1, D))],
        out_shape=[jax.ShapeDtypeStruct((S, D), BF16), jax.ShapeDtypeStruct((N_CHIPS, n, T, FF_BLK), BF16),
                   jax.ShapeDtypeStruct((N_CHIPS, n, T, FF_BLK), BF16), jax.ShapeDtypeStruct((S, D), BF16),
                   jax.ShapeDtypeStruct((S, D), F32), jax.ShapeDtypeStruct((1, 128), F32),
                   jax.ShapeDtypeStruct((1, D), F32)],
        scratch_shapes=[pltpu.VMEM((N_CHIPS, D, FF_BLK), BF16), pltpu.VMEM((N_CHIPS, D, FF_BLK), BF16),
                        pltpu.VMEM((N_CHIPS, FF_BLK, D), BF16), pltpu.VMEM((2, T, FF_BLK), BF16),
                        pltpu.VMEM((2, T, FF_BLK), BF16), pltpu.SemaphoreType.DMA((2, 2))],
        compiler_params=_params(("arbitrary",)),
    )(x2, target, gpre, gpost, g2g, pr1g)
    return h3, gs.reshape(N_CHIPS, S, FF_BLK), us.reshape(N_CHIPS, S, FF_BLK), do3, dx3, loss, dgpost


def _ffn_bwd(h3, do3, gs, us, g2g, pr1g):
    S, D = h3.shape
    T = min(TILE_FFN, S)
    n = S // T
    NP = FFN_BWD_BLOCKS

    def body(h3_ref, do3_ref, g_ref, u_ref, g2g_hbm, pr1g_hbm, dh3_ref, cf1_hbm, cf2_hbm,
             wg_v, wu_v, wd_v, dwg_acc, dwu_acc, dwd_acc):
        jp = pl.program_id(0)
        i = pl.program_id(1)
        blocks = pl.ds(NP * jp, NP)

        @pl.when(i == 0)
        def _():
            pltpu.sync_copy(g2g_hbm.at[blocks, pl.ds(0, D), :], wg_v)
            pltpu.sync_copy(g2g_hbm.at[blocks, pl.ds(D, D), :], wu_v)
            pltpu.sync_copy(pr1g_hbm.at[blocks, pl.ds(P1_DOWN, FF_BLK), :], wd_v)
            dwg_acc[...] = jnp.zeros_like(dwg_acc)
            dwu_acc[...] = jnp.zeros_like(dwu_acc)
            dwd_acc[...] = jnp.zeros_like(dwd_acc)

        hb = h3_ref[...]
        d_o3 = do3_ref[...]
        dh = jnp.zeros((T, D), F32)
        for c in range(NP):
            da = _mm_nt(d_o3, wd_v[c])
            g = g_ref[c].astype(F32)
            u = u_ref[c].astype(F32)
            sg = _sigmoid(g)
            sl = g * sg
            dwd_acc[c] += _mm_tn((sl * u).astype(BF16), d_o3)
            dub = (da * sl).astype(BF16)
            dgb = (da * u * (sg * (1.0 + g * (1.0 - sg)))).astype(BF16)
            dwg_acc[c] += _mm_tn(dgb, hb)
            dwu_acc[c] += _mm_tn(dub, hb)
            dh = dh + _mm_nt(dgb, wg_v[c]) + _mm_nt(dub, wu_v[c])
        dh3_ref[...] = dh.astype(BF16)

        @pl.when(i == n - 1)
        def _():
            wd_v[...] = dwg_acc[...].astype(BF16)
            pltpu.sync_copy(wd_v, cf1_hbm.at[blocks, pl.ds(0, FF_BLK), :])
            wd_v[...] = dwu_acc[...].astype(BF16)
            pltpu.sync_copy(wd_v, cf1_hbm.at[blocks, pl.ds(FF_BLK, FF_BLK), :])
            wd_v[...] = dwd_acc[...].astype(BF16)
            pltpu.sync_copy(wd_v, cf2_hbm.at[blocks])

    tok = lambda w: pl.BlockSpec((T, w), lambda jp, i: (i, 0))
    blk = pl.BlockSpec((NP, T, FF_BLK), lambda jp, i: (jp, i, 0))
    return pl.pallas_call(
        body,
        name="ffn_bwd",
        grid=(N_CHIPS // NP, n),
        in_specs=[tok(D), tok(D), blk, blk, ANY, ANY],
        out_specs=[pl.BlockSpec((None, T, D), lambda jp, i: (jp, i, 0)), ANY, ANY],
        out_shape=[jax.ShapeDtypeStruct((N_CHIPS // NP, S, D), BF16),
                   jax.ShapeDtypeStruct((N_CHIPS, 2 * FF_BLK, D), BF16),
                   jax.ShapeDtypeStruct((N_CHIPS, FF_BLK, D), BF16)],
        scratch_shapes=[pltpu.VMEM((NP, D, FF_BLK), BF16), pltpu.VMEM((NP, D, FF_BLK), BF16),
                        pltpu.VMEM((NP, FF_BLK, D), BF16), pltpu.VMEM((NP, FF_BLK, D), F32),
                        pltpu.VMEM((NP, FF_BLK, D), F32), pltpu.VMEM((NP, FF_BLK, D), F32)],
        compiler_params=_params(("arbitrary", "arbitrary")),
    )(h3, do3, gs, us, g2g, pr1g)


def _exchange_last(contribs, small):
    na = len(contribs)
    cs = [_by_halves(a) for a in contribs]

    def body(*refs):
        srcs, small_ref = refs[:na], refs[na]
        dsts, small_all = refs[na + 1:2 * na + 1], refs[2 * na + 1]
        send, recv, ssend, srecv, lsem = refs[2 * na + 2:]
        x, y, c, _ = _my_place()
        me = 4 * x + 2 * y + c
        _scatter_start(srcs, dsts, send, recv)
        loc = pltpu.make_async_copy(small_ref, small_all.at[me], lsem)
        loc.start()
        scps = []
        for d, (peer, pidx) in enumerate(_peers(x, y, c)):
            cp = _remote(small_ref, small_all.at[me], ssend.at[d], srecv.at[d], peer)
            cp.start()
            scps.append((cp, _remote(small_ref, small_all.at[pidx], ssend.at[d], srecv.at[d], peer)))
        _scatter_wait(srcs, dsts, send, recv)
        for out_cp, in_cp in scps:
            in_cp.wait_recv()
            out_cp.wait_send()
        loc.wait()

    outs = pl.pallas_call(
        body,
        name="exchange_last",
        in_specs=[ANY] * (na + 1),
        out_specs=[ANY] * (na + 1),
        out_shape=[_slots(a) for a in cs] + [jax.ShapeDtypeStruct((N_DEV,) + small.shape, F32)],
        scratch_shapes=_scatter_sems(na) + [pltpu.SemaphoreType.DMA((N_DEV - 1,))] * 2 + [pltpu.SemaphoreType.DMA],
    )(*cs, small)
    return outs[:na], outs[na]


def _sum_peers(parts, own, place, steps, tag):
    _, rows, w = parts.shape
    tr = rows // steps

    def body(place_ref, *refs):
        p_refs, own_ref, o_ref = refs[:N_DEV], refs[N_DEV], refs[N_DEV + 1]
        me = place_ref[2]
        acc = None
        for s in range(N_DEV):
            term = jnp.where(me == s, own_ref[...], p_refs[s][...]).astype(F32)
            acc = term if acc is None else acc + term
        o_ref[...] = acc

    def other(s):
        return lambda i, pr: (jnp.where(pr[2] == s, (s + 1) % N_DEV, s), i, 0)

    return pl.pallas_call(
        body,
        name="sum_peers_" + tag,
        grid_spec=pltpu.PrefetchScalarGridSpec(
            num_scalar_prefetch=1,
            grid=(steps,),
            in_specs=[pl.BlockSpec((None, tr, w), other(s)) for s in range(N_DEV)]
            + [pl.BlockSpec((None, None, tr, w), lambda i, pr: (pr[0], pr[1], i, 0))],
            out_specs=pl.BlockSpec((None, tr, w), lambda i, pr: (pr[1], i, 0)),
        ),
        out_shape=jax.ShapeDtypeStruct((2, rows, w), F32),
        compiler_params=_params(("arbitrary",)),
    )(place, *([parts] * N_DEV), _by_halves(own))


def _sum_slots(parts, tag):
    nslot, rows, w = parts.shape

    def body(p_ref, o_ref):
        acc = p_ref[0]
        for k in range(1, nslot):
            acc = acc + p_ref[k]
        o_ref[...] = acc

    return pl.pallas_call(
        body,
        name="sum_slots_" + tag,
        grid=(1,),
        in_specs=[_full((nslot, rows, w))],
        out_specs=_full((rows, w)),
        out_shape=jax.ShapeDtypeStruct((rows, w), F32),
        compiler_params=_params(("arbitrary",)),
    )(parts)


def _pair_gather(bufs):
    np_ = len(bufs)

    def body(*refs):
        srcs, dsts = refs[:np_], refs[np_:2 * np_]
        send, recv = refs[2 * np_:]
        x, y, c, _ = _my_place()
        cps = []
        for p in range(np_):
            cp = _remote(srcs[p].at[c], dsts[p].at[c], send.at[p], recv.at[p], (x, y, 1 - c))
            cp.start()
            cps.append(cp)
        for p, cp in enumerate(cps):
            other = dsts[p].at[1 - c]
            _remote(other, other, send.at[p], recv.at[p], (x, y, 1 - c)).wait_recv()
            cp.wait_send()

    outs = pl.pallas_call(
        body,
        name="pair_gather",
        in_specs=[ANY] * np_,
        out_specs=[ANY] * np_,
        out_shape=[jax.ShapeDtypeStruct(a.shape, F32) for a in bufs],
        input_output_aliases={p: p for p in range(np_)},
        scratch_shapes=[pltpu.SemaphoreType.DMA((np_,))] * 2,
    )(*bufs)
    return [o.reshape(2 * a.shape[1], a.shape[2]) for o, a in zip(outs, bufs)]


def _adamw(gsrc, row0, w, m, v, tr, tag):
    rows, width = w.shape
    off = row0 // tr
    bc1 = 1.0 - ADAM_B1 ** ADAM_STEP
    bc2 = 1.0 - ADAM_B2 ** ADAM_STEP

    def body(g_ref, w_ref, m_ref, v_ref, go_ref, d_ref, mo_ref, vo_ref):
        g = g_ref[...]
        m2 = ADAM_B1 * m_ref[...] + (1.0 - ADAM_B1) * g
        v2 = ADAM_B2 * v_ref[...] + (1.0 - ADAM_B2) * (g * g)
        go_ref[...] = g
        mo_ref[...] = m2
        vo_ref[...] = v2
        d_ref[...] = -ADAM_LR * ((m2 / bc1) / (jnp.sqrt(v2 / bc2) + ADAM_EPS) + ADAM_WD * w_ref[...])

    here = pl.BlockSpec((tr, width), lambda i: (i, 0))
    return pl.pallas_call(
        body,
        name="adamw_" + tag,
        grid=(rows // tr,),
        in_specs=[pl.BlockSpec((tr, width), lambda i: (off + i, 0)), here, here, here],
        out_specs=[here] * 4,
        out_shape=[jax.ShapeDtypeStruct((rows, width), F32)] * 4,
        compiler_params=_params(("arbitrary",)),
    )(gsrc, w, m, v)


def kernel(x, mem, mix_pre_g, w_mix_in, conv_a_w, conv_b_w, conv_b_b, ln_b_g, ln_b_b, w_mix_out, mix_post_g, xa_pre_g, mem_norm_g, w_q, w_k, w_v, w_o, xa_post_g, ffn_pre_g, w_gate, w_up, w_down, ffn_post_g, loss_target, m_mix_pre_g, m_w_mix_in, m_conv_a_w, m_conv_b_w, m_conv_b_b, m_ln_b_g, m_ln_b_b, m_w_mix_out, m_mix_post_g, m_xa_pre_g, m_mem_norm_g, m_w_q, m_w_k, m_w_v, m_w_o, m_xa_post_g, m_ffn_pre_g, m_w_gate, m_w_up, m_w_down, m_ffn_post_g, v_mix_pre_g, v_w_mix_in, v_conv_a_w, v_conv_b_w, v_conv_b_b, v_ln_b_g, v_ln_b_b, v_w_mix_out, v_mix_post_g, v_xa_pre_g, v_mem_norm_g, v_w_q, v_w_k, v_w_v, v_w_o, v_xa_post_g, v_ffn_pre_g, v_w_gate, v_w_up, v_w_down, v_ffn_post_g):
    given = dict(locals())
    names = ["mix_pre_g", "w_mix_in", "conv_a_w", "conv_b_w", "conv_b_b", "ln_b_g", "ln_b_b", "w_mix_out",
             "mix_post_g", "xa_pre_g", "mem_norm_g", "w_q", "w_k", "w_v", "w_o", "xa_post_g", "ffn_pre_g",
             "w_gate", "w_up", "w_down", "ffn_post_g"]
    row = lambda a: a.reshape(1, -1)
    cx, cy, cc = lax.axis_index("x"), lax.axis_index("y"), lax.axis_index("c")
    chip = 2 * cx + cy
    ca_blk = conv_a_w.shape[1]

    conv_rows = CONV_A_W + CONV_B_W
    sw = jnp.concatenate([conv_a_w, conv_b_w, jnp.zeros((SMALL_W_ROWS - conv_rows, ca_blk), F32)], axis=0)
    g1g, pr0g, swg = _gather_weights([w_mix_in.astype(BF16), w_mix_out.astype(BF16), sw])
    conv_full = jnp.transpose(swg[:, :conv_rows, :], (1, 0, 2)).reshape(conv_rows, N_CHIPS * ca_blk)
    wa, wb = conv_full[:CONV_A_W], conv_full[CONV_A_W:]
    pr1 = jnp.concatenate([w_q, w_k, w_v, w_o, w_down], axis=0).astype(BF16)
    g2 = jnp.concatenate([w_gate, w_up], axis=0).astype(BF16)

    xs, ms, tgt = x[0], mem[0], loss_target[0]
    x1, u, o1, z1, pr1g, g2g = _mix_fwd(xs, row(mix_pre_g), row(mix_post_g), wa, wb, row(conv_b_b), row(ln_b_g),
                                        row(ln_b_b), g1g, pr0g, [pr1, g2])
    memn, kb, vb = _mem_kv(ms, row(mem_norm_g), pr1g)
    x2, q, o2 = _xattn_fwd(x1, row(xa_pre_g), row(xa_post_g), kb, vb, pr1g)
    h3, gs, us, do3, dx3, loss_part, d_ffn_post = _ffn_fwd(x2, tgt, row(ffn_pre_g), row(ffn_post_g), g2g, pr1g)

    dh3p, cf1, cf2 = _ffn_bwd(h3, do3, gs, us, g2g, pr1g)
    dx1, d_ffn_pre, d_xa_post, d_xa_pre, dk, dv, cxa, yf1, yf2 = _xattn_bwd(
        dx3, dh3p, x2, x1, o2, q, kb, vb, row(ffn_pre_g), row(xa_post_g), row(xa_pre_g), pr1g, cf1, cf2)
    d_mem_g, cxa = _mem_bwd(dk, dv, ms, memn, row(mem_norm_g), pr1g, cxa)
    dx, d_mix_pre, d_mix_post, dwa, dwb, dbb, dlng, dlnb, cm1, cm2, yx = _mix_bwd(
        dx1, xs, o1, u, z1, row(mix_pre_g), row(mix_post_g), wa, wb, row(ln_b_g), row(ln_b_b), g1g, pr0g, cxa)

    small_parts = [d_mix_pre, dwa, dwb, dbb, dlng, dlnb, d_mix_post, d_xa_pre, d_mem_g, d_xa_post, d_ffn_pre,
                   d_ffn_post, loss_part]
    sizes = [p.size for p in small_parts]
    small = jnp.concatenate([p.reshape(-1) for p in small_parts])
    small_rows = -(-small.size // (8 * 128)) * 8
    small = jnp.pad(small, (0, small_rows * 128 - small.size)).reshape(small_rows, 128)
    (ym1, ym2), small_all = _exchange_last([cm1, cm2], small)
    place = jnp.stack([chip, cc, 2 * chip + cc]).astype(jnp.int32)
    landed = [(yf1, cf1, "gate_up"), (yf2, cf2, "down"), (yx, cxa, "attn"), (ym1, cm1, "mix_in"), (ym2, cm2, "mix_out")]
    r_gu, r_down, r_attn, r_in, r_out = _pair_gather([_sum_peers(y, c, place, 2, t) for y, c, t in landed])
    small_sum = _sum_slots(small_all, "small").reshape(-1)
    red, pos = [], 0
    for p, sz in zip(small_parts, sizes):
        red.append(small_sum[pos:pos + sz].reshape(p.shape))
        pos += sz
    (r_mix_pre, r_wa, r_wb, r_bb, r_lng, r_lnb, r_mix_post, r_xa_pre, r_mem_g, r_xa_post, r_ffn_pre, r_ffn_post,
     r_loss) = red
    loss = r_loss[0, 0]

    res = {}
    big = {"w_mix_out": (r_out, 0, 256), "w_q": (r_attn, GX_Q, 256), "w_k": (r_attn, GX_K, 256),
           "w_v": (r_attn, GX_V, 256), "w_o": (r_attn, GX_O, 256), "w_down": (r_down, 0, 64),
           "w_mix_in": (r_in, 0, 256)}
    for nm, (src, row0, tr) in big.items():
        res[nm] = _adamw(src, row0, given[nm], given["m_" + nm], given["v_" + nm], tr, nm)
    for nm, row0 in (("w_gate", 0), ("w_up", FF_BLK)):
        outs = _adamw(r_gu, row0, given[nm].T, given["m_" + nm].T, given["v_" + nm].T, 64, nm)
        res[nm] = [o.T for o in outs]
    small_grads = {"mix_pre_g": r_mix_pre, "conv_b_b": r_bb, "ln_b_g": r_lng, "ln_b_b": r_lnb,
                   "mix_post_g": r_mix_post, "xa_pre_g": r_xa_pre, "mem_norm_g": r_mem_g, "xa_post_g": r_xa_post,
                   "ffn_pre_g": r_ffn_pre, "ffn_post_g": r_ffn_post,
                   "conv_a_w": lax.dynamic_slice_in_dim(r_wa, chip * ca_blk, ca_blk, axis=1),
                   "conv_b_w": lax.dynamic_slice_in_dim(r_wb, chip * ca_blk, ca_blk, axis=1)}
    small_names = list(small_grads)

    def packed(prefix, grads=None):
        flat = jnp.concatenate([(grads[nm] if grads else given[prefix + nm]).reshape(-1) for nm in small_names])
        rows8 = -(-flat.size // (8 * 128)) * 8
        return jnp.pad(flat, (0, rows8 * 128 - flat.size)).reshape(rows8, 128)

    gp = packed("", small_grads)
    outs = _adamw(gp, 0, packed(""), packed("m_"), packed("v_"), gp.shape[0], "small")
    pos = 0
    for nm in small_names:
        shape = given[nm].shape
        sz = given[nm].size
        res[nm] = [o.reshape(-1)[pos:pos + sz].reshape(shape) for o in outs]
        pos += sz

    return (loss, dx[None], *[res[nm][0] for nm in names], *[res[nm][1] for nm in names],
            *[res[nm][2] for nm in names], *[res[nm][3] for nm in names])
```

```python
import jax
import jax.numpy as jnp
from jax import lax
from jax.experimental import pallas as pl
from jax.experimental.pallas import tpu as pltpu

F32 = jnp.float32
BF16 = jnp.bfloat16
MESH = pl.DeviceIdType.MESH

RMS_EPS = 1e-6
LN_EPS = 1e-5
D_MODEL = 1024
D_A = 512
D_B = 512
D_IN_ALL = 3 * D_A + 2 * D_B
CONV_A_W = 3
CONV_B_W = 31
HALO = 32
XA_HEADS = 4
HEAD_DIM = 256
D_FF = 2816
N_CHIPS = 4
N_DEV = 8
FF_BLK = D_FF // N_CHIPS
IN_BLK = D_IN_ALL // N_CHIPS
ROW_BLK = D_MODEL // N_CHIPS

ADAM_LR = 0.001
ADAM_B1 = 0.9
ADAM_B2 = 0.999
ADAM_EPS = 1e-08
ADAM_WD = 0.01
ADAM_STEP = 10

TILE_FWD = 512
TILE_FFN = 512
TILE_BWD = 256
FFN_BWD_BLOCKS = 2
CONV_ROWS = 64
V7X_VMEM_LIMIT = 56 * 1024 * 1024

P1_Q, P1_K, P1_V, P1_O, P1_DOWN = 0, 256, 512, 768, 1024
P1_ROWS = P1_DOWN + FF_BLK
GX_Q, GX_K, GX_V, GX_O = 0, 256, 512, 768
SMALL_W_ROWS = 48

ANY = pl.BlockSpec(memory_space=pl.ANY)


def _mm(a, b):
    return lax.dot_general(a, b, (((1,), (0,)), ((), ())), preferred_element_type=F32)


def _mm_nt(a, b):
    return lax.dot_general(a, b, (((1,), (1,)), ((), ())), preferred_element_type=F32)


def _mm_tn(a, b):
    return lax.dot_general(a, b, (((0,), (0,)), ((), ())), preferred_element_type=F32)


def _sigmoid(x):
    return 0.5 * jnp.tanh(0.5 * x) + 0.5


def _rms(x):
    r = lax.rsqrt(jnp.mean(x * x, axis=-1, keepdims=True) + RMS_EPS)
    return x * r, r


def _rms_bwd(dy, xn, r, g):
    gdy = dy * g
    dx = r * (gdy - xn * jnp.mean(gdy * xn, axis=-1, keepdims=True))
    return dx, jnp.sum(dy * xn, axis=0, keepdims=True)


def _full(shape):
    return pl.BlockSpec(shape, lambda *_: (0,) * len(shape))


def _params(sem=None):
    return pltpu.CompilerParams(dimension_semantics=sem, vmem_limit_bytes=V7X_VMEM_LIMIT)


def _load_rows(dst, src_hbm, row0, rows):
    for k in range(N_CHIPS):
        pltpu.sync_copy(src_hbm.at[k, pl.ds(row0, rows), :], dst.at[pl.ds(rows * k, rows), :])


def _load_cols(dst, src_hbm, cols):
    for k in range(N_CHIPS):
        pltpu.sync_copy(src_hbm.at[k], dst.at[:, pl.ds(cols * k, cols)])


def _fill_phases(src, sh, nrows):
    for r in range(1, 8):
        sh[r, pl.ds(0, nrows), :] = src[pl.ds(r, nrows), pl.ds(D_A, D_B)]


def _phase_rows(src, sh, off, start, size):
    r = off % 8
    if r == 0:
        return src[pl.ds(off + start, size), pl.ds(D_A, D_B)]
    return sh[r, pl.ds(off - r + start, size), :]


def _my_place():
    x, y, c = lax.axis_index("x"), lax.axis_index("y"), lax.axis_index("c")
    return x, y, c, ((1 - x, y), (x, 1 - y), (1 - x, 1 - y))


def _remote(src, dst, send_sem, recv_sem, to):
    return pltpu.make_async_remote_copy(src_ref=src, dst_ref=dst, send_sem=send_sem, recv_sem=recv_sem,
                                        device_id=to, device_id_type=MESH)


def _gather_sems(np_):
    return [pltpu.SemaphoreType.DMA((np_, 3))] * 4 + [pltpu.SemaphoreType.DMA((np_,))] * 2


def _gather_start(srcs, dsts, sems):
    send, recv, _, _, osend, orecv = sems
    x, y, c, chips = _my_place()
    j = 2 * x + y
    for p in range(len(srcs)):
        _remote(srcs[p], dsts[p].at[j], osend.at[p], orecv.at[p], (x, y, 1 - c)).start()
        for nn, (kx, ky) in enumerate(chips):
            _remote(srcs[p].at[c], dsts[p].at[j, c], send.at[p, nn], recv.at[p, nn], (kx, ky, c)).start()


def _gather_forward(srcs, dsts, sems):
    send, recv, fsend, frecv, _, _ = sems
    x, y, c, chips = _my_place()
    for nn, (kx, ky) in enumerate(chips):
        for p in range(len(srcs)):
            blk = dsts[p].at[2 * kx + ky, c]
            _remote(blk, blk, send.at[p, nn], recv.at[p, nn], (kx, ky, c)).wait_recv()
            _remote(blk, blk, fsend.at[p, nn], frecv.at[p, nn], (x, y, 1 - c)).start()


def _gather_finish(srcs, dsts, sems):
    send, recv, fsend, frecv, osend, orecv = sems
    x, y, c, chips = _my_place()
    j = 2 * x + y
    for nn, (kx, ky) in enumerate(chips):
        for p in range(len(srcs)):
            other = dsts[p].at[2 * kx + ky, 1 - c]
            _remote(other, other, fsend.at[p, nn], frecv.at[p, nn], (x, y, 1 - c)).wait_recv()
    for nn, (kx, ky) in enumerate(chips):
        for p in range(len(srcs)):
            _remote(srcs[p].at[c], dsts[p].at[j, c], send.at[p, nn], recv.at[p, nn], (kx, ky, c)).wait_send()
            blk = dsts[p].at[2 * kx + ky, c]
            _remote(blk, blk, fsend.at[p, nn], frecv.at[p, nn], (x, y, 1 - c)).wait_send()
    for p in range(len(srcs)):
        _remote(srcs[p], dsts[p].at[j], osend.at[p], orecv.at[p], (x, y, 1 - c)).wait()


def _split_halves(a):
    return a.reshape(2, a.shape[0] // 2, a.shape[1])


def _gather_weights(packs):
    np_ = len(packs)
    split = [_split_halves(a) for a in packs]

    def body(*refs):
        srcs, dsts, sems = refs[:np_], refs[np_:2 * np_], refs[2 * np_:]
        _gather_start(srcs, dsts, sems)
        _gather_forward(srcs, dsts, sems)
        _gather_finish(srcs, dsts, sems)

    outs = pl.pallas_call(
        body,
        name="gather_weights",
        in_specs=[ANY] * np_,
        out_specs=[ANY] * np_,
        out_shape=[jax.ShapeDtypeStruct((N_CHIPS,) + a.shape, a.dtype) for a in split],
        scratch_shapes=_gather_sems(np_),
    )(*split)
    return [o.reshape((N_CHIPS,) + a.shape) for o, a in zip(outs, packs)]


def _peers(x, y, c):
    out = []
    for d in range(1, N_DEV):
        px = 1 - x if d & 4 else x
        py = 1 - y if d & 2 else y
        pc = 1 - c if d & 1 else c
        out.append(((px, py, pc), 4 * px + 2 * py + pc))
    return out


def _scatter_copies(srcs, dsts, send, recv):
    x, y, c, _ = _my_place()
    me = 4 * x + 2 * y + c
    out = []
    for a in range(len(srcs)):
        for d, ((px, py, pc), pidx) in enumerate(_peers(x, y, c)):
            piece = srcs[a].at[2 * px + py, pc]
            out.append((_remote(piece, dsts[a].at[me], send.at[a, d], recv.at[a, d], (px, py, pc)),
                        _remote(piece, dsts[a].at[pidx], send.at[a, d], recv.at[a, d], (px, py, pc))))
    return out


def _scatter_start(srcs, dsts, send, recv):
    for out_cp, _ in _scatter_copies(srcs, dsts, send, recv):
        out_cp.start()


def _scatter_wait(srcs, dsts, send, recv):
    for out_cp, in_cp in _scatter_copies(srcs, dsts, send, recv):
        in_cp.wait_recv()
        out_cp.wait_send()


def _scatter_sems(na):
    return [pltpu.SemaphoreType.DMA((na, N_DEV - 1))] * 2


def _slots(a):
    return jax.ShapeDtypeStruct((N_DEV,) + a.shape[2:], a.dtype)


def _by_halves(a):
    return a.reshape(a.shape[0], 2, a.shape[1] // 2, a.shape[2])


def _mix_fwd(x, gpre, gpost, wa, wb, bb, lng, lnb, g1g, pr0g, late):
    S, D = x.shape
    T = min(TILE_FWD, S)
    n = S // T
    nl = len(late)
    late_split = [_split_halves(a) for a in late]

    def body(*refs):
        (x_ref, gpre_ref, gpost_ref, wa_ref, wb_ref, bb_ref, lng_ref, lnb_ref, g1g_hbm, pr0g_hbm) = refs[:10]
        srcs = refs[10:10 + nl]
        x1_ref, u_ref, o1_ref, z1_ref = refs[10 + nl:14 + nl]
        dsts = refs[14 + nl:14 + 2 * nl]
        win_v, wout_v, ext, sh, z1buf = refs[14 + 2 * nl:19 + 2 * nl]
        sems = refs[19 + 2 * nl:]
        i = pl.program_id(0)

        @pl.when(i == 0)
        def _():
            _gather_start(srcs, dsts, sems)
            _load_cols(win_v, g1g_hbm, IN_BLK)
            _load_rows(wout_v, pr0g_hbm, 0, ROW_BLK)
            ext[pl.ds(0, HALO), :] = jnp.zeros((HALO, D_A + D_B), F32)

        @pl.when(i == max(n - 2, 0))
        def _():
            _gather_forward(srcs, dsts, sems)

        xv = x_ref[...]
        xn, _ = _rms(xv)
        h = (xn * gpre_ref[...]).astype(BF16)
        u = _mm(h, win_v[...])
        u_ref[...] = u.astype(BF16)
        b_a = u[:, 0:D_A]
        cv = u[:, D_A:2 * D_A] * u[:, 2 * D_A:3 * D_A]
        z0 = u[:, 3 * D_A:3 * D_A + D_B] * _sigmoid(u[:, 3 * D_A + D_B:])
        ext[pl.ds(HALO, T), pl.ds(0, D_A)] = cv
        ext[pl.ds(HALO, T), pl.ds(D_A, D_B)] = z0

        conv_a = ext[pl.ds(HALO - 2, T), pl.ds(0, D_A)] * wa_ref[0:1, :]
        for k in range(1, CONV_A_W):
            conv_a = conv_a + ext[pl.ds(HALO - 2 + k, T), pl.ds(0, D_A)] * wa_ref[k:k + 1, :]
        y_a = b_a * conv_a

        _fill_phases(ext, sh, T + HALO - 8)
        base = HALO - (CONV_B_W - 1)

        def chunk(ci, carry):
            start = pl.multiple_of(ci * CONV_ROWS, 8)
            acc = jnp.broadcast_to(bb_ref[...], (CONV_ROWS, D_B))
            for k in range(CONV_B_W):
                acc = acc + _phase_rows(ext, sh, base + k, start, CONV_ROWS) * wb_ref[k:k + 1, :]
            z1buf[pl.ds(start, CONV_ROWS), :] = acc
            return carry

        lax.fori_loop(0, T // CONV_ROWS, chunk, 0)
        z1 = z1buf[...]
        z1_ref[...] = z1.astype(BF16)
        mu = jnp.mean(z1, axis=-1, keepdims=True)
        zc = z1 - mu
        rstd = lax.rsqrt(jnp.mean(zc * zc, axis=-1, keepdims=True) + LN_EPS)
        l = zc * rstd * lng_ref[...] + lnb_ref[...]
        y_b = l * _sigmoid(l)
        y = jnp.concatenate([y_a, y_b], axis=-1).astype(BF16)
        o1 = _mm(y, wout_v[...])
        o1_ref[...] = o1.astype(BF16)
        o1n, _ = _rms(o1)
        x1_ref[...] = xv + o1n * gpost_ref[...]
        ext[pl.ds(0, HALO), :] = ext[pl.ds(T, HALO), :]

        @pl.when(i == n - 1)
        def _():
            _gather_finish(srcs, dsts, sems)

    tok = lambda w: pl.BlockSpec((T, w), lambda i: (i, 0))
    outs = pl.pallas_call(
        body,
        name="mix_fwd",
        grid=(n,),
        in_specs=[tok(D), _full((1, D)), _full((1, D)), _full((CONV_A_W, D_A)), _full((CONV_B_W, D_B)),
                  _full((1, D_B)), _full((1, D_B)), _full((1, D_B)), ANY, ANY] + [ANY] * nl,
        out_specs=[tok(D), tok(D_IN_ALL), tok(D), tok(D_B)] + [ANY] * nl,
        out_shape=[jax.ShapeDtypeStruct((S, D), F32), jax.ShapeDtypeStruct((S, D_IN_ALL), BF16),
                   jax.ShapeDtypeStruct((S, D), BF16), jax.ShapeDtypeStruct((S, D_B), BF16)]
        + [jax.ShapeDtypeStruct((N_CHIPS,) + a.shape, a.dtype) for a in late_split],
        scratch_shapes=[pltpu.VMEM((D, D_IN_ALL), BF16), pltpu.VMEM((D_A + D_B, D), BF16),
                        pltpu.VMEM((HALO + T, D_A + D_B), F32), pltpu.VMEM((8, HALO + T, D_B), F32),
                        pltpu.VMEM((T, D_B), F32)] + _gather_sems(nl),
        compiler_params=_params(("arbitrary",)),
    )(x, gpre, gpost, wa, wb, bb, lng, lnb, g1g, pr0g, *late_split)
    return list(outs[:4]) + [o.reshape((N_CHIPS,) + a.shape) for o, a in zip(outs[4:], late)]


def _mix_bwd(dx1, x, o1, u, z1s, gpre, gpost, wa, wb, lng, lnb, g1g, pr0g, cx):
    S, D = x.shape
    T = min(TILE_BWD, S)
    n = S // T
    hb = T // HALO
    cxs = _by_halves(cx)

    def body(dx1_ref, x_ref, o1_ref, u_ref, uh_ref, z1_ref, gpre_ref, gpost_ref, wa_ref, wb_ref, lng_ref, lnb_ref,
             g1g_hbm, pr0g_hbm, cx_hbm,
             dx_ref, dgpre_ref, dgpost_ref, dwa_ref, dwb_ref, dbb_ref, dlng_ref, dlnb_ref, cm1_hbm, cm2_hbm, yx_hbm,
             win_v, wout_v, dwin_acc, dwout_acc, ext, ext2, shb, dz0buf, dwb_acc, send, recv):
        i = pl.program_id(0)

        @pl.when(i == 0)
        def _():
            _scatter_start([cx_hbm], [yx_hbm], send, recv)
            _load_cols(win_v, g1g_hbm, IN_BLK)
            _load_rows(wout_v, pr0g_hbm, 0, ROW_BLK)
            dwin_acc[...] = jnp.zeros_like(dwin_acc)
            dwout_acc[...] = jnp.zeros_like(dwout_acc)
            dwb_acc[...] = jnp.zeros_like(dwb_acc)
            ext2[pl.ds(T, HALO), :] = jnp.zeros((HALO, D_A + D_B), F32)
            for ref in (dgpre_ref, dgpost_ref, dwa_ref, dbb_ref, dlng_ref, dlnb_ref):
                ref[...] = jnp.zeros_like(ref)

        o1n, r1 = _rms(o1_ref[...].astype(F32))
        dx1v = dx1_ref[...]
        d_o1, dgp = _rms_bwd(dx1v, o1n, r1, gpost_ref[...])
        dgpost_ref[...] += dgp
        d_o1b = d_o1.astype(BF16)
        dy = _mm_nt(d_o1b, wout_v[...])

        first = (i == n - 1).astype(F32)
        uh = uh_ref[...].astype(F32) * (1.0 - first)
        ext[pl.ds(0, HALO), pl.ds(0, D_A)] = uh[:, D_A:2 * D_A] * uh[:, 2 * D_A:3 * D_A]
        uf = u_ref[...].astype(F32)
        b_a = uf[:, 0:D_A]
        c_a = uf[:, D_A:2 * D_A]
        v_a = uf[:, 2 * D_A:3 * D_A]
        gv = uf[:, 3 * D_A:3 * D_A + D_B]
        sg = _sigmoid(uf[:, 3 * D_A + D_B:])
        ext[pl.ds(HALO, T), pl.ds(0, D_A)] = c_a * v_a
        ext[pl.ds(HALO, T), pl.ds(D_A, D_B)] = gv * sg
        conv_a = ext[pl.ds(HALO - 2, T), pl.ds(0, D_A)] * wa_ref[0:1, :]
        for k in range(1, CONV_A_W):
            conv_a = conv_a + ext[pl.ds(HALO - 2 + k, T), pl.ds(0, D_A)] * wa_ref[k:k + 1, :]
        z1 = z1_ref[...].astype(F32)
        mu = jnp.mean(z1, axis=-1, keepdims=True)
        zc = z1 - mu
        rstd = lax.rsqrt(jnp.mean(zc * zc, axis=-1, keepdims=True) + LN_EPS)
        zn = zc * rstd
        l = zn * lng_ref[...] + lnb_ref[...]
        sl = _sigmoid(l)
        y = jnp.concatenate([b_a * conv_a, l * sl], axis=-1).astype(BF16)
        dwout_acc[...] += _mm_tn(y, d_o1b)

        dy_a = dy[:, 0:D_A]
        dl = dy[:, D_A:] * (sl * (1.0 + l * (1.0 - sl)))
        dlng_ref[...] += jnp.sum(dl * zn, axis=0, keepdims=True)
        dlnb_ref[...] += jnp.sum(dl, axis=0, keepdims=True)
        dzn = dl * lng_ref[...]
        dz1 = rstd * (dzn - jnp.mean(dzn, axis=-1, keepdims=True) - zn * jnp.mean(dzn * zn, axis=-1, keepdims=True))
        dbb_ref[...] += jnp.sum(dz1, axis=0, keepdims=True)
        d_conv = dy_a * b_a
        ext2[pl.ds(0, T), pl.ds(0, D_A)] = d_conv
        ext2[pl.ds(0, T), pl.ds(D_A, D_B)] = dz1

        d_cv = ext2[pl.ds(CONV_A_W - 1, T), pl.ds(0, D_A)] * wa_ref[0:1, :]
        for k in range(1, CONV_A_W):
            d_cv = d_cv + ext2[pl.ds(CONV_A_W - 1 - k, T), pl.ds(0, D_A)] * wa_ref[k:k + 1, :]
        for k in range(CONV_A_W):
            dwa_ref[k:k + 1, :] += jnp.sum(d_conv * ext[pl.ds(HALO - 2 + k, T), pl.ds(0, D_A)], axis=0, keepdims=True)

        _fill_phases(ext2, shb, T + HALO - 8)

        def chunk(ci, carry):
            start = pl.multiple_of(ci * CONV_ROWS, 8)
            z0c = ext[pl.ds(HALO + start, CONV_ROWS), pl.ds(D_A, D_B)]
            acc = jnp.zeros((CONV_ROWS, D_B), F32)
            for k in range(CONV_B_W):
                wk = wb_ref[k:k + 1, :]
                ahead = _phase_rows(ext2, shb, CONV_B_W - 1 - k, start, CONV_ROWS)
                acc = acc + ahead * wk
                prod = z0c * ahead
                part = prod[0:8, :]
                for m in range(1, CONV_ROWS // 8):
                    part = part + prod[8 * m:8 * m + 8, :]
                dwb_acc[k] += part
            dz0buf[pl.ds(start, CONV_ROWS), :] = acc
            return carry

        lax.fori_loop(0, T // CONV_ROWS, chunk, 0)
        dz0 = dz0buf[...]
        du = jnp.concatenate([dy_a * conv_a, d_cv * v_a, d_cv * c_a, dz0 * sg, dz0 * gv * sg * (1.0 - sg)],
                             axis=-1).astype(BF16)
        dh = _mm_nt(du, win_v[...])
        xv = x_ref[...]
        xn, r0 = _rms(xv)
        dwin_acc[...] += _mm_tn((xn * gpre_ref[...]).astype(BF16), du)
        dxp, dg0 = _rms_bwd(dh, xn, r0, gpre_ref[...])
        dgpre_ref[...] += dg0
        dx_ref[...] = dx1v + dxp
        ext2[pl.ds(T, HALO), :] = ext2[pl.ds(0, HALO), :]

        @pl.when(i == n - 1)
        def _():
            for k in range(CONV_B_W):
                dwb_ref[k:k + 1, :] = jnp.sum(dwb_acc[k], axis=0, keepdims=True)
            win_v[...] = dwin_acc[...].astype(BF16)
            wout_v[...] = dwout_acc[...].astype(BF16)
            for k in range(N_CHIPS):
                pltpu.sync_copy(win_v.at[:, pl.ds(IN_BLK * k, IN_BLK)], cm1_hbm.at[k])
                pltpu.sync_copy(wout_v.at[pl.ds(ROW_BLK * k, ROW_BLK), :], cm2_hbm.at[k])
            _scatter_wait([cx_hbm], [yx_hbm], send, recv)

    rev = lambda w: pl.BlockSpec((T, w), lambda i: (n - 1 - i, 0))
    halo = pl.BlockSpec((HALO, D_IN_ALL), lambda i: (jnp.maximum((n - 1 - i) * hb - 1, 0), 0))
    return pl.pallas_call(
        body,
        name="mix_bwd",
        grid=(n,),
        in_specs=[rev(D), rev(D), rev(D), rev(D_IN_ALL), halo, rev(D_B), _full((1, D)), _full((1, D)),
                  _full((CONV_A_W, D_A)), _full((CONV_B_W, D_B)), _full((1, D_B)), _full((1, D_B)), ANY, ANY, ANY],
        out_specs=[rev(D), _full((1, D)), _full((1, D)), _full((CONV_A_W, D_A)), _full((CONV_B_W, D_B)),
                   _full((1, D_B)), _full((1, D_B)), _full((1, D_B)), ANY, ANY, ANY],
        out_shape=[jax.ShapeDtypeStruct((S, D), F32), jax.ShapeDtypeStruct((1, D), F32),
                   jax.ShapeDtypeStruct((1, D), F32), jax.ShapeDtypeStruct((CONV_A_W, D_A), F32),
                   jax.ShapeDtypeStruct((CONV_B_W, D_B), F32), jax.ShapeDtypeStruct((1, D_B), F32),
                   jax.ShapeDtypeStruct((1, D_B), F32), jax.ShapeDtypeStruct((1, D_B), F32),
                   jax.ShapeDtypeStruct((N_CHIPS, D, IN_BLK), BF16),
                   jax.ShapeDtypeStruct((N_CHIPS, ROW_BLK, D), BF16), _slots(cxs)],
        scratch_shapes=[pltpu.VMEM((D, D_IN_ALL), BF16), pltpu.VMEM((D_A + D_B, D), BF16),
                        pltpu.VMEM((D, D_IN_ALL), F32), pltpu.VMEM((D_A + D_B, D), F32),
                        pltpu.VMEM((HALO + T, D_A + D_B), F32), pltpu.VMEM((HALO + T, D_A + D_B), F32),
                        pltpu.VMEM((8, HALO + T, D_B), F32),
                        pltpu.VMEM((T, D_B), F32), pltpu.VMEM((CONV_B_W, 8, D_B), F32)] + _scatter_sems(1),
        compiler_params=_params(("arbitrary",)),
    )(dx1, x, o1, u, u, z1s, gpre, gpost, wa, wb, lng, lnb, g1g, pr0g, cxs)


def _mem_kv(mem, gmem, pr1g):
    M, D = mem.shape

    def body(mem_ref, g_ref, pr1g_hbm, memn_ref, k_ref, v_ref, wk_v, wv_v):
        _load_rows(wk_v, pr1g_hbm, P1_K, ROW_BLK)
        _load_rows(wv_v, pr1g_hbm, P1_V, ROW_BLK)
        mn, _ = _rms(mem_ref[...])
        mb = (mn * g_ref[...]).astype(BF16)
        memn_ref[...] = mb
        k_ref[...] = _mm(mb, wk_v[...]).astype(BF16)
        v_ref[...] = _mm(mb, wv_v[...]).astype(BF16)

    return pl.pallas_call(
        body,
        name="mem_kv",
        grid=(1,),
        in_specs=[_full((M, D)), _full((1, D)), ANY],
        out_specs=[_full((M, D))] * 3,
        out_shape=[jax.ShapeDtypeStruct((M, D), BF16)] * 3,
        scratch_shapes=[pltpu.VMEM((D, D), BF16), pltpu.VMEM((D, D), BF16)],
        compiler_params=_params(("arbitrary",)),
    )(mem, gmem, pr1g)


def _attend(qb, kb, vb):
    scale = HEAD_DIM ** -0.5
    ps, os_ = [], []
    for hd in range(XA_HEADS):
        cols = slice(HEAD_DIM * hd, HEAD_DIM * (hd + 1))
        s = _mm_nt(qb[:, cols], kb[:, cols]) * scale
        e = jnp.exp(s - jnp.max(s, axis=-1, keepdims=True))
        p = e * (1.0 / jnp.sum(e, axis=-1, keepdims=True))
        ps.append(p)
        os_.append(_mm(p.astype(BF16), vb[:, cols]))
    return ps, jnp.concatenate(os_, axis=-1).astype(BF16)


def _xattn_fwd(x1, gpre, gpost, kb, vb, pr1g):
    S, D = x1.shape
    M = kb.shape[0]
    T = min(TILE_FWD, S)
    n = S // T

    def body(x1_ref, gpre_ref, gpost_ref, k_ref, v_ref, pr1g_hbm, x2_ref, q_ref, o2_ref, wq_v, wo_v):
        @pl.when(pl.program_id(0) == 0)
        def _():
            _load_rows(wq_v, pr1g_hbm, P1_Q, ROW_BLK)
            _load_rows(wo_v, pr1g_hbm, P1_O, ROW_BLK)

        xv = x1_ref[...]
        xn, _ = _rms(xv)
        qb = _mm((xn * gpre_ref[...]).astype(BF16), wq_v[...]).astype(BF16)
        q_ref[...] = qb
        _, ob = _attend(qb, k_ref[...], v_ref[...])
        o2 = _mm(ob, wo_v[...])
        o2_ref[...] = o2.astype(BF16)
        o2n, _ = _rms(o2)
        x2_ref[...] = xv + o2n * gpost_ref[...]

    tok = lambda w: pl.BlockSpec((T, w), lambda i: (i, 0))
    return pl.pallas_call(
        body,
        name="xattn_fwd",
        grid=(n,),
        in_specs=[tok(D), _full((1, D)), _full((1, D)), _full((M, D)), _full((M, D)), ANY],
        out_specs=[tok(D), tok(D), tok(D)],
        out_shape=[jax.ShapeDtypeStruct((S, D), F32), jax.ShapeDtypeStruct((S, D), BF16),
                   jax.ShapeDtypeStruct((S, D), BF16)],
        scratch_shapes=[pltpu.VMEM((D, D), BF16), pltpu.VMEM((D, D), BF16)],
        compiler_params=_params(("arbitrary",)),
    )(x1, gpre, gpost, kb, vb, pr1g)


def _xattn_bwd(dx3, dh3p, x2, x1, o2, q, kb, vb, gffn, gpost, gpre, pr1g, cf1, cf2):
    S, D = x1.shape
    M = kb.shape[0]
    T = min(TILE_BWD, S)
    n = S // T
    scale = HEAD_DIM ** -0.5
    nparts = dh3p.shape[0]
    cfs = [_by_halves(cf1), _by_halves(cf2)]

    def body(*refs):
        dx3_ref, dh3_refs = refs[0], refs[1:1 + nparts]
        (x2_ref, x1_ref, o2_ref, q_ref, k_ref, v_ref, gffn_ref, gpost_ref, gpre_ref, pr1g_hbm, cf1_hbm, cf2_hbm,
         dx1_ref, dgffn_ref, dgpost_ref, dgpre_ref, dk_ref, dv_ref, cx_hbm, yf1_hbm, yf2_hbm,
         wq_v, wo_v, dwq_acc, dwo_acc, send, recv) = refs[1 + nparts:]
        i = pl.program_id(0)

        @pl.when(i == 0)
        def _():
            _scatter_start([cf1_hbm, cf2_hbm], [yf1_hbm, yf2_hbm], send, recv)
            _load_rows(wq_v, pr1g_hbm, P1_Q, ROW_BLK)
            _load_rows(wo_v, pr1g_hbm, P1_O, ROW_BLK)
            dwq_acc[...] = jnp.zeros_like(dwq_acc)
            dwo_acc[...] = jnp.zeros_like(dwo_acc)
            for ref in (dgffn_ref, dgpost_ref, dgpre_ref, dk_ref, dv_ref):
                ref[...] = jnp.zeros_like(ref)

        x2n, r2 = _rms(x2_ref[...])
        dh3 = dh3_refs[0][...].astype(F32)
        for ref in dh3_refs[1:]:
            dh3 = dh3 + ref[...].astype(F32)
        dxp, dg = _rms_bwd(dh3, x2n, r2, gffn_ref[...])
        dgffn_ref[...] += dg
        dx2 = dx3_ref[...] + dxp
        o2n, ro = _rms(o2_ref[...].astype(F32))
        d_o2, dg = _rms_bwd(dx2, o2n, ro, gpost_ref[...])
        dgpost_ref[...] += dg
        d_o2b = d_o2.astype(BF16)
        d_o = _mm_nt(d_o2b, wo_v[...]).astype(BF16)
        qb = q_ref[...]
        kv = k_ref[...]
        vv = v_ref[...]
        ps, ob = _attend(qb, kv, vv)
        dwo_acc[...] += _mm_tn(ob, d_o2b)
        dqs = []
        for hd in range(XA_HEADS):
            cols = slice(HEAD_DIM * hd, HEAD_DIM * (hd + 1))
            p = ps[hd]
            dp = _mm_nt(d_o[:, cols], vv[:, cols])
            dv_ref[:, cols] += _mm_tn(p.astype(BF16), d_o[:, cols])
            ds = (p * (dp - jnp.sum(p * dp, axis=-1, keepdims=True)) * scale).astype(BF16)
            dqs.append(_mm(ds, kv[:, cols]))
            dk_ref[:, cols] += _mm_tn(ds, qb[:, cols])
        dq = jnp.concatenate(dqs, axis=-1).astype(BF16)
        dh2 = _mm_nt(dq, wq_v[...])
        x1n, r1 = _rms(x1_ref[...])
        dwq_acc[...] += _mm_tn((x1n * gpre_ref[...]).astype(BF16), dq)
        dxp, dg = _rms_bwd(dh2, x1n, r1, gpre_ref[...])
        dgpre_ref[...] += dg
        dx1_ref[...] = dx2 + dxp

        @pl.when(i == n - 1)
        def _():
            wq_v[...] = dwq_acc[...].astype(BF16)
            wo_v[...] = dwo_acc[...].astype(BF16)
            for k in range(N_CHIPS):
                rows = pl.ds(ROW_BLK * k, ROW_BLK)
                pltpu.sync_copy(wq_v.at[rows, :], cx_hbm.at[k, pl.ds(GX_Q, ROW_BLK), :])
                pltpu.sync_copy(wo_v.at[rows, :], cx_hbm.at[k, pl.ds(GX_O, ROW_BLK), :])
            _scatter_wait([cf1_hbm, cf2_hbm], [yf1_hbm, yf2_hbm], send, recv)

    tok = lambda w: pl.BlockSpec((T, w), lambda i: (i, 0))
    part = lambda j: pl.BlockSpec((None, T, D), lambda i: (j, i, 0))
    return pl.pallas_call(
        body,
        name="xattn_bwd",
        grid=(n,),
        in_specs=[tok(D)] + [part(j) for j in range(nparts)] + [tok(D), tok(D), tok(D), tok(D), _full((M, D)),
                                                                 _full((M, D)), _full((1, D)), _full((1, D)),
                                                                 _full((1, D)), ANY, ANY, ANY],
        out_specs=[tok(D), _full((1, D)), _full((1, D)), _full((1, D)), _full((M, D)), _full((M, D)), ANY, ANY, ANY],
        out_shape=[jax.ShapeDtypeStruct((S, D), F32), jax.ShapeDtypeStruct((1, D), F32),
                   jax.ShapeDtypeStruct((1, D), F32), jax.ShapeDtypeStruct((1, D), F32),
                   jax.ShapeDtypeStruct((M, D), F32), jax.ShapeDtypeStruct((M, D), F32),
                   jax.ShapeDtypeStruct((N_CHIPS, D, D), BF16), _slots(cfs[0]), _slots(cfs[1])],
        scratch_shapes=[pltpu.VMEM((D, D), BF16), pltpu.VMEM((D, D), BF16),
                        pltpu.VMEM((D, D), F32), pltpu.VMEM((D, D), F32)] + _scatter_sems(2),
        compiler_params=_params(("arbitrary",)),
    )(dx3, *([dh3p] * nparts), x2, x1, o2, q, kb, vb, gffn, gpost, gpre, pr1g, *cfs)


def _mem_bwd(dk, dv, mem, memn, gmem, pr1g, cx_in):
    M, D = mem.shape

    def body(dk_ref, dv_ref, mem_ref, memn_ref, g_ref, pr1g_hbm, cx_hbm, dg_ref, cx_out, wk_v, wv_v):
        del cx_hbm
        _load_rows(wk_v, pr1g_hbm, P1_K, ROW_BLK)
        _load_rows(wv_v, pr1g_hbm, P1_V, ROW_BLK)
        dkb = dk_ref[...].astype(BF16)
        dvb = dv_ref[...].astype(BF16)
        mb = memn_ref[...]
        dmn = _mm_nt(dkb, wk_v[...]) + _mm_nt(dvb, wv_v[...])
        mn, _ = _rms(mem_ref[...])
        dg_ref[...] = jnp.sum(dmn * mn, axis=0, keepdims=True)
        wk_v[...] = _mm_tn(mb, dkb).astype(BF16)
        wv_v[...] = _mm_tn(mb, dvb).astype(BF16)
        for k in range(N_CHIPS):
            rows = pl.ds(ROW_BLK * k, ROW_BLK)
            pltpu.sync_copy(wk_v.at[rows, :], cx_out.at[k, pl.ds(GX_K, ROW_BLK), :])
            pltpu.sync_copy(wv_v.at[rows, :], cx_out.at[k, pl.ds(GX_V, ROW_BLK), :])

    return pl.pallas_call(
        body,
        name="mem_bwd",
        grid=(1,),
        in_specs=[_full((M, D)), _full((M, D)), _full((M, D)), _full((M, D)), _full((1, D)), ANY, ANY],
        out_specs=[_full((1, D)), ANY],
        out_shape=[jax.ShapeDtypeStruct((1, D), F32), jax.ShapeDtypeStruct(cx_in.shape, BF16)],
        input_output_aliases={6: 1},
        scratch_shapes=[pltpu.VMEM((D, D), BF16), pltpu.VMEM((D, D), BF16)],
        compiler_params=_params(("arbitrary",)),
    )(dk, dv, mem, memn, gmem, pr1g, cx_in)


def _ffn_fwd(x2, target, gpre, gpost, g2g, pr1g):
    S, D = x2.shape
    T = min(TILE_FFN, S)
    n = S // T

    def body(x2_ref, t_ref, gpre_ref, gpost_ref, g2g_hbm, pr1g_hbm,
             h3_ref, g_hbm, u_hbm, do3_ref, dx3_ref, loss_ref, dgpost_ref, wg_v, wu_v, wd_v, gst, ust, sem):
        i = pl.program_id(0)

        @pl.when(i == 0)
        def _():
            pltpu.sync_copy(g2g_hbm.at[:, pl.ds(0, D), :], wg_v)
            pltpu.sync_copy(g2g_hbm.at[:, pl.ds(D, D), :], wu_v)
            pltpu.sync_copy(pr1g_hbm.at[:, pl.ds(P1_DOWN, FF_BLK), :], wd_v)
            loss_ref[...] = jnp.zeros_like(loss_ref)
            dgpost_ref[...] = jnp.zeros_like(dgpost_ref)

        xv = x2_ref[...]
        xn, _ = _rms(xv)
        hb = (xn * gpre_ref[...]).astype(BF16)
        h3_ref[...] = hb
        o3 = jnp.zeros((T, D), F32)
        out = [None, None]
        for c in range(N_CHIPS):
            slot = c % 2
            if out[slot] is not None:
                for cp in out[slot]:
                    cp.wait()
            g = _mm(hb, wg_v[c])
            u = _mm(hb, wu_v[c])
            gst[slot] = g.astype(BF16)
            ust[slot] = u.astype(BF16)
            out[slot] = (pltpu.make_async_copy(gst.at[slot], g_hbm.at[c, i], sem.at[0, slot]),
                         pltpu.make_async_copy(ust.at[slot], u_hbm.at[c, i], sem.at[1, slot]))
            for cp in out[slot]:
                cp.start()
            o3 = o3 + _mm((g * _sigmoid(g) * u).astype(BF16), wd_v[c])
        for pair in out:
            for cp in pair:
                cp.wait()
        o3n, r3 = _rms(o3)
        diff = xv + o3n * gpost_ref[...] - t_ref[...]
        sq = jnp.sum(jnp.sum(diff * diff, axis=-1, keepdims=True), axis=0, keepdims=True)
        loss_ref[...] += sq * (0.5 / D)
        dx3 = diff * (1.0 / D)
        dx3_ref[...] = dx3
        d_o3, dg = _rms_bwd(dx3, o3n, r3, gpost_ref[...])
        dgpost_ref[...] += dg
        do3_ref[...] = d_o3.astype(BF16)

    tok = lambda w: pl.BlockSpec((T, w), lambda i: (i, 0))
    h3, gs, us, do3, dx3, loss, dgpost = pl.pallas_call(
        body,
        name="ffn_fwd",
        grid=(n,),
        in_specs=[tok(D), tok(D), _full((1, D)), _full((1, D)), ANY, ANY],
        out_specs=[tok(D), ANY, ANY, tok(D), tok(D), _full((1, 128)), _full((1, D))],
        out_shape=[jax.ShapeDtypeStruct((S, D), BF16), jax.ShapeDtypeStruct((N_CHIPS, n, T, FF_BLK), BF16),
                   jax.ShapeDtypeStruct((N_CHIPS, n, T, FF_BLK), BF16), jax.ShapeDtypeStruct((S, D), BF16),
                   jax.ShapeDtypeStruct((S, D), F32), jax.ShapeDtypeStruct((1, 128), F32),
                   jax.ShapeDtypeStruct((1, D), F32)],
        scratch_shapes=[pltpu.VMEM((N_CHIPS, D, FF_BLK), BF16), pltpu.VMEM((N_CHIPS, D, FF_BLK), BF16),
                        pltpu.VMEM((N_CHIPS, FF_BLK, D), BF16), pltpu.VMEM((2, T, FF_BLK), BF16),
                        pltpu.VMEM((2, T, FF_BLK), BF16), pltpu.SemaphoreType.DMA((2, 2))],
        compiler_params=_params(("arbitrary",)),
    )(x2, target, gpre, gpost, g2g, pr1g)
    return h3, gs.reshape(N_CHIPS, S, FF_BLK), us.reshape(N_CHIPS, S, FF_BLK), do3, dx3, loss, dgpost


def _ffn_bwd(h3, do3, gs, us, g2g, pr1g):
    S, D = h3.shape
    T = min(TILE_FFN, S)
    n = S // T
    NP = FFN_BWD_BLOCKS

    def body(h3_ref, do3_ref, g_ref, u_ref, g2g_hbm, pr1g_hbm, dh3_ref, cf1_hbm, cf2_hbm,
             wg_v, wu_v, wd_v, dwg_acc, dwu_acc, dwd_acc):
        jp = pl.program_id(0)
        i = pl.program_id(1)
        blocks = pl.ds(NP * jp, NP)

        @pl.when(i == 0)
        def _():
            pltpu.sync_copy(g2g_hbm.at[blocks, pl.ds(0, D), :], wg_v)
            pltpu.sync_copy(g2g_hbm.at[blocks, pl.ds(D, D), :], wu_v)
            pltpu.sync_copy(pr1g_hbm.at[blocks, pl.ds(P1_DOWN, FF_BLK), :], wd_v)
            dwg_acc[...] = jnp.zeros_like(dwg_acc)
            dwu_acc[...] = jnp.zeros_like(dwu_acc)
            dwd_acc[...] = jnp.zeros_like(dwd_acc)

        hb = h3_ref[...]
        d_o3 = do3_ref[...]
        dh = jnp.zeros((T, D), F32)
        for c in range(NP):
            da = _mm_nt(d_o3, wd_v[c])
            g = g_ref[c].astype(F32)
            u = u_ref[c].astype(F32)
            sg = _sigmoid(g)
            sl = g * sg
            dwd_acc[c] += _mm_tn((sl * u).astype(BF16), d_o3)
            dub = (da * sl).astype(BF16)
            dgb = (da * u * (sg * (1.0 + g * (1.0 - sg)))).astype(BF16)
            dwg_acc[c] += _mm_tn(dgb, hb)
            dwu_acc[c] += _mm_tn(dub, hb)
            dh = dh + _mm_nt(dgb, wg_v[c]) + _mm_nt(dub, wu_v[c])
        dh3_ref[...] = dh.astype(BF16)

        @pl.when(i == n - 1)
        def _():
            wd_v[...] = dwg_acc[...].astype(BF16)
            pltpu.sync_copy(wd_v, cf1_hbm.at[blocks, pl.ds(0, FF_BLK), :])
            wd_v[...] = dwu_acc[...].astype(BF16)
            pltpu.sync_copy(wd_v, cf1_hbm.at[blocks, pl.ds(FF_BLK, FF_BLK), :])
            wd_v[...] = dwd_acc[...].astype(BF16)
            pltpu.sync_copy(wd_v, cf2_hbm.at[blocks])

    tok = lambda w: pl.BlockSpec((T, w), lambda jp, i: (i, 0))
    blk = pl.BlockSpec((NP, T, FF_BLK), lambda jp, i: (jp, i, 0))
    return pl.pallas_call(
        body,
        name="ffn_bwd",
        grid=(N_CHIPS // NP, n),
        in_specs=[tok(D), tok(D), blk, blk, ANY, ANY],
        out_specs=[pl.BlockSpec((None, T, D), lambda jp, i: (jp, i, 0)), ANY, ANY],
        out_shape=[jax.ShapeDtypeStruct((N_CHIPS // NP, S, D), BF16),
                   jax.ShapeDtypeStruct((N_CHIPS, 2 * FF_BLK, D), BF16),
                   jax.ShapeDtypeStruct((N_CHIPS, FF_BLK, D), BF16)],
        scratch_shapes=[pltpu.VMEM((NP, D, FF_BLK), BF16), pltpu.VMEM((NP, D, FF_BLK), BF16),
                        pltpu.VMEM((NP, FF_BLK, D), BF16), pltpu.VMEM((NP, FF_BLK, D), F32),
                        pltpu.VMEM((NP, FF_BLK, D), F32), pltpu.VMEM((NP, FF_BLK, D), F32)],
        compiler_params=_params(("arbitrary", "arbitrary")),
    )(h3, do3, gs, us, g2g, pr1g)


def _exchange_last(contribs, small):
    na = len(contribs)
    cs = [_by_halves(a) for a in contribs]

    def body(*refs):
        srcs, small_ref = refs[:na], refs[na]
        dsts, small_all = refs[na + 1:2 * na + 1], refs[2 * na + 1]
        send, recv, ssend, srecv, lsem = refs[2 * na + 2:]
        x, y, c, _ = _my_place()
        me = 4 * x + 2 * y + c
        _scatter_start(srcs, dsts, send, recv)
        loc = pltpu.make_async_copy(small_ref, small_all.at[me], lsem)
        loc.start()
        scps = []
        for d, (peer, pidx) in enumerate(_peers(x, y, c)):
            cp = _remote(small_ref, small_all.at[me], ssend.at[d], srecv.at[d], peer)
            cp.start()
            scps.append((cp, _remote(small_ref, small_all.at[pidx], ssend.at[d], srecv.at[d], peer)))
        _scatter_wait(srcs, dsts, send, recv)
        for out_cp, in_cp in scps:
            in_cp.wait_recv()
            out_cp.wait_send()
        loc.wait()

    outs = pl.pallas_call(
        body,
        name="exchange_last",
        in_specs=[ANY] * (na + 1),
        out_specs=[ANY] * (na + 1),
        out_shape=[_slots(a) for a in cs] + [jax.ShapeDtypeStruct((N_DEV,) + small.shape, F32)],
        scratch_shapes=_scatter_sems(na) + [pltpu.SemaphoreType.DMA((N_DEV - 1,))] * 2 + [pltpu.SemaphoreType.DMA],
    )(*cs, small)
    return outs[:na], outs[na]


def _sum_peers(parts, own, place, steps, tag):
    _, rows, w = parts.shape
    tr = rows // steps

    def body(place_ref, *refs):
        p_refs, own_ref, o_ref = refs[:N_DEV], refs[N_DEV], refs[N_DEV + 1]
        me = place_ref[2]
        acc = None
        for s in range(N_DEV):
            term = jnp.where(me == s, own_ref[...], p_refs[s][...]).astype(F32)
            acc = term if acc is None else acc + term
        o_ref[...] = acc

    def other(s):
        return lambda i, pr: (jnp.where(pr[2] == s, (s + 1) % N_DEV, s), i, 0)

    return pl.pallas_call(
        body,
        name="sum_peers_" + tag,
        grid_spec=pltpu.PrefetchScalarGridSpec(
            num_scalar_prefetch=1,
            grid=(steps,),
            in_specs=[pl.BlockSpec((None, tr, w), other(s)) for s in range(N_DEV)]
            + [pl.BlockSpec((None, None, tr, w), lambda i, pr: (pr[0], pr[1], i, 0))],
            out_specs=pl.BlockSpec((None, tr, w), lambda i, pr: (pr[1], i, 0)),
        ),
        out_shape=jax.ShapeDtypeStruct((2, rows, w), F32),
        compiler_params=_params(("arbitrary",)),
    )(place, *([parts] * N_DEV), _by_halves(own))


def _sum_slots(parts, tag):
    nslot, rows, w = parts.shape

    def body(p_ref, o_ref):
        acc = p_ref[0]
        for k in range(1, nslot):
            acc = acc + p_ref[k]
        o_ref[...] = acc

    return pl.pallas_call(
        body,
        name="sum_slots_" + tag,
        grid=(1,),
        in_specs=[_full((nslot, rows, w))],
        out_specs=_full((rows, w)),
        out_shape=jax.ShapeDtypeStruct((rows, w), F32),
        compiler_params=_params(("arbitrary",)),
    )(parts)


def _pair_gather(bufs):
    np_ = len(bufs)

    def body(*refs):
        srcs, dsts = refs[:np_], refs[np_:2 * np_]
        send, recv = refs[2 * np_:]
        x, y, c, _ = _my_place()
        cps = []
        for p in range(np_):
            cp = _remote(srcs[p].at[c], dsts[p].at[c], send.at[p], recv.at[p], (x, y, 1 - c))
            cp.start()
            cps.append(cp)
        for p, cp in enumerate(cps):
            other = dsts[p].at[1 - c]
            _remote(other, other, send.at[p], recv.at[p], (x, y, 1 - c)).wait_recv()
            cp.wait_send()

    outs = pl.pallas_call(
        body,
        name="pair_gather",
        in_specs=[ANY] * np_,
        out_specs=[ANY] * np_,
        out_shape=[jax.ShapeDtypeStruct(a.shape, F32) for a in bufs],
        input_output_aliases={p: p for p in range(np_)},
        scratch_shapes=[pltpu.SemaphoreType.DMA((np_,))] * 2,
    )(*bufs)
    return [o.reshape(2 * a.shape[1], a.shape[2]) for o, a in zip(outs, bufs)]


def _adamw(gsrc, row0, w, m, v, tr, tag):
    rows, width = w.shape
    off = row0 // tr
    bc1 = 1.0 - ADAM_B1 ** ADAM_STEP
    bc2 = 1.0 - ADAM_B2 ** ADAM_STEP

    def body(g_ref, w_ref, m_ref, v_ref, go_ref, d_ref, mo_ref, vo_ref):
        g = g_ref[...]
        m2 = ADAM_B1 * m_ref[...] + (1.0 - ADAM_B1) * g
        v2 = ADAM_B2 * v_ref[...] + (1.0 - ADAM_B2) * (g * g)
        go_ref[...] = g
        mo_ref[...] = m2
        vo_ref[...] = v2
        d_ref[...] = -ADAM_LR * ((m2 / bc1) / (jnp.sqrt(v2 / bc2) + ADAM_EPS) + ADAM_WD * w_ref[...])

    here = pl.BlockSpec((tr, width), lambda i: (i, 0))
    return pl.pallas_call(
        body,
        name="adamw_" + tag,
        grid=(rows // tr,),
        in_specs=[pl.BlockSpec((tr, width), lambda i: (off + i, 0)), here, here, here],
        out_specs=[here] * 4,
        out_shape=[jax.ShapeDtypeStruct((rows, width), F32)] * 4,
        compiler_params=_params(("arbitrary",)),
    )(gsrc, w, m, v)


def kernel(x, mem, mix_pre_g, w_mix_in, conv_a_w, conv_b_w, conv_b_b, ln_b_g, ln_b_b, w_mix_out, mix_post_g, xa_pre_g, mem_norm_g, w_q, w_k, w_v, w_o, xa_post_g, ffn_pre_g, w_gate, w_up, w_down, ffn_post_g, loss_target, m_mix_pre_g, m_w_mix_in, m_conv_a_w, m_conv_b_w, m_conv_b_b, m_ln_b_g, m_ln_b_b, m_w_mix_out, m_mix_post_g, m_xa_pre_g, m_mem_norm_g, m_w_q, m_w_k, m_w_v, m_w_o, m_xa_post_g, m_ffn_pre_g, m_w_gate, m_w_up, m_w_down, m_ffn_post_g, v_mix_pre_g, v_w_mix_in, v_conv_a_w, v_conv_b_w, v_conv_b_b, v_ln_b_g, v_ln_b_b, v_w_mix_out, v_mix_post_g, v_xa_pre_g, v_mem_norm_g, v_w_q, v_w_k, v_w_v, v_w_o, v_xa_post_g, v_ffn_pre_g, v_w_gate, v_w_up, v_w_down, v_ffn_post_g):
    given = dict(locals())
    names = ["mix_pre_g", "w_mix_in", "conv_a_w", "conv_b_w", "conv_b_b", "ln_b_g", "ln_b_b", "w_mix_out",
             "mix_post_g", "xa_pre_g", "mem_norm_g", "w_q", "w_k", "w_v", "w_o", "xa_post_g", "ffn_pre_g",
             "w_gate", "w_up", "w_down", "ffn_post_g"]
    row = lambda a: a.reshape(1, -1)
    cx, cy, cc = lax.axis_index("x"), lax.axis_index("y"), lax.axis_index("c")
    chip = 2 * cx + cy
    ca_blk = conv_a_w.shape[1]

    conv_rows = CONV_A_W + CONV_B_W
    sw = jnp.concatenate([conv_a_w, conv_b_w, jnp.zeros((SMALL_W_ROWS - conv_rows, ca_blk), F32)], axis=0)
    g1g, pr0g, swg = _gather_weights([w_mix_in.astype(BF16), w_mix_out.astype(BF16), sw])
    conv_full = jnp.transpose(swg[:, :conv_rows, :], (1, 0, 2)).reshape(conv_rows, N_CHIPS * ca_blk)
    wa, wb = conv_full[:CONV_A_W], conv_full[CONV_A_W:]
    pr1 = jnp.concatenate([w_q, w_k, w_v, w_o, w_down], axis=0).astype(BF16)
    g2 = jnp.concatenate([w_gate, w_up], axis=0).astype(BF16)

    xs, ms, tgt = x[0], mem[0], loss_target[0]
    x1, u, o1, z1, pr1g, g2g = _mix_fwd(xs, row(mix_pre_g), row(mix_post_g), wa, wb, row(conv_b_b), row(ln_b_g),
                                        row(ln_b_b), g1g, pr0g, [pr1, g2])
    memn, kb, vb = _mem_kv(ms, row(mem_norm_g), pr1g)
    x2, q, o2 = _xattn_fwd(x1, row(xa_pre_g), row(xa_post_g), kb, vb, pr1g)
    h3, gs, us, do3, dx3, loss_part, d_ffn_post = _ffn_fwd(x2, tgt, row(ffn_pre_g), row(ffn_post_g), g2g, pr1g)

    dh3p, cf1, cf2 = _ffn_bwd(h3, do3, gs, us, g2g, pr1g)
    dx1, d_ffn_pre, d_xa_post, d_xa_pre, dk, dv, cxa, yf1, yf2 = _xattn_bwd(
        dx3, dh3p, x2, x1, o2, q, kb, vb, row(ffn_pre_g), row(xa_post_g), row(xa_pre_g), pr1g, cf1, cf2)
    d_mem_g, cxa = _mem_bwd(dk, dv, ms, memn, row(mem_norm_g), pr1g, cxa)
    dx, d_mix_pre, d_mix_post, dwa, dwb, dbb, dlng, dlnb, cm1, cm2, yx = _mix_bwd(
        dx1, xs, o1, u, z1, row(mix_pre_g), row(mix_post_g), wa, wb, row(ln_b_g), row(ln_b_b), g1g, pr0g, cxa)

    small_parts = [d_mix_pre, dwa, dwb, dbb, dlng, dlnb, d_mix_post, d_xa_pre, d_mem_g, d_xa_post, d_ffn_pre,
                   d_ffn_post, loss_part]
    sizes = [p.size for p in small_parts]
    small = jnp.concatenate([p.reshape(-1) for p in small_parts])
    small_rows = -(-small.size // (8 * 128)) * 8
    small = jnp.pad(small, (0, small_rows * 128 - small.size)).reshape(small_rows, 128)
    (ym1, ym2), small_all = _exchange_last([cm1, cm2], small)
    place = jnp.stack([chip, cc, 2 * chip + cc]).astype(jnp.int32)
    landed = [(yf1, cf1, "gate_up"), (yf2, cf2, "down"), (yx, cxa, "attn"), (ym1, cm1, "mix_in"), (ym2, cm2, "mix_out")]
    r_gu, r_down, r_attn, r_in, r_out = _pair_gather([_sum_peers(y, c, place, 2, t) for y, c, t in landed])
    small_sum = _sum_slots(small_all, "small").reshape(-1)
    red, pos = [], 0
    for p, sz in zip(small_parts, sizes):
        red.append(small_sum[pos:pos + sz].reshape(p.shape))
        pos += sz
    (r_mix_pre, r_wa, r_wb, r_bb, r_lng, r_lnb, r_mix_post, r_xa_pre, r_mem_g, r_xa_post, r_ffn_pre, r_ffn_post,
     r_loss) = red
    loss = r_loss[0, 0]

    res = {}
    big = {"w_mix_out": (r_out, 0, 256), "w_q": (r_attn, GX_Q, 256), "w_k": (r_attn, GX_K, 256),
           "w_v": (r_attn, GX_V, 256), "w_o": (r_attn, GX_O, 256), "w_down": (r_down, 0, FF_BLK // 2),
           "w_mix_in": (r_in, 0, 512)}
    for nm, (src, row0, tr) in big.items():
        res[nm] = _adamw(src, row0, given[nm], given["m_" + nm], given["v_" + nm], tr, nm)
    for nm, row0 in (("w_gate", 0), ("w_up", FF_BLK)):
        outs = _adamw(r_gu, row0, given[nm].T, given["m_" + nm].T, given["v_" + nm].T, FF_BLK // 2, nm)
        res[nm] = [o.T for o in outs]
    small_grads = {"mix_pre_g": r_mix_pre, "conv_b_b": r_bb, "ln_b_g": r_lng, "ln_b_b": r_lnb,
                   "mix_post_g": r_mix_post, "xa_pre_g": r_xa_pre, "mem_norm_g": r_mem_g, "xa_post_g": r_xa_post,
                   "ffn_pre_g": r_ffn_pre, "ffn_post_g": r_ffn_post,
                   "conv_a_w": lax.dynamic_slice_in_dim(r_wa, chip * ca_blk, ca_blk, axis=1),
                   "conv_b_w": lax.dynamic_slice_in_dim(r_wb, chip * ca_blk, ca_blk, axis=1)}
    small_names = list(small_grads)

    def packed(prefix, grads=None):
        flat = jnp.concatenate([(grads[nm] if grads else given[prefix + nm]).reshape(-1) for nm in small_names])
        rows8 = -(-flat.size // (8 * 128)) * 8
        return jnp.pad(flat, (0, rows8 * 128 - flat.size)).reshape(rows8, 128)

    gp = packed("", small_grads)
    outs = _adamw(gp, 0, packed(""), packed("m_"), packed("v_"), gp.shape[0], "small")
    pos = 0
    for nm in small_names:
        shape = given[nm].shape
        sz = given[nm].size
        res[nm] = [o.reshape(-1)[pos:pos + sz].reshape(shape) for o in outs]
        pos += sz

    return (loss, dx[None], *[res[nm][0] for nm in names], *[res[nm][1] for nm in names],
            *[res[nm][2] for nm in names], *[res[nm][3] for nm in names])
```

```python
import jax
import jax.numpy as jnp
from jax import lax
from jax.experimental import pallas as pl
from jax.experimental.pallas import tpu as pltpu

F32 = jnp.float32
BF16 = jnp.bfloat16
MESH = pl.DeviceIdType.MESH

RMS_EPS = 1e-6
LN_EPS = 1e-5
D_MODEL = 1024
D_A = 512
D_B = 512
D_IN_ALL = 3 * D_A + 2 * D_B
CONV_A_W = 3
CONV_B_W = 31
HALO = 32
XA_HEADS = 4
HEAD_DIM = 256
D_FF = 2816
N_CHIPS = 4
N_DEV = 8
FF_BLK = D_FF // N_CHIPS
IN_BLK = D_IN_ALL // N_CHIPS
ROW_BLK = D_MODEL // N_CHIPS

ADAM_LR = 0.001
ADAM_B1 = 0.9
ADAM_B2 = 0.999
ADAM_EPS = 1e-08
ADAM_WD = 0.01
ADAM_STEP = 10

TILE_FWD = 512
TILE_XATTN_FWD = 1024
TILE_FFN = 512
TILE_BWD = 256
FFN_BWD_BLOCKS = 2
CONV_ROWS_FWD = 64
CONV_ROWS = 32
V7X_VMEM_LIMIT = 56 * 1024 * 1024

P1_Q, P1_K, P1_V, P1_O, P1_DOWN = 0, 256, 512, 768, 1024
P1_ROWS = P1_DOWN + FF_BLK
GX_Q, GX_K, GX_V, GX_O = 0, 256, 512, 768
SMALL_W_ROWS = 48

ANY = pl.BlockSpec(memory_space=pl.ANY)


def _mm(a, b):
    return lax.dot_general(a, b, (((1,), (0,)), ((), ())), preferred_element_type=F32)


def _mm_nt(a, b):
    return lax.dot_general(a, b, (((1,), (1,)), ((), ())), preferred_element_type=F32)


def _mm_tn(a, b):
    return lax.dot_general(a, b, (((0,), (0,)), ((), ())), preferred_element_type=F32)


def _sigmoid(x):
    return 0.5 * jnp.tanh(0.5 * x) + 0.5


def _rms(x):
    r = lax.rsqrt(jnp.mean(x * x, axis=-1, keepdims=True) + RMS_EPS)
    return x * r, r


def _rms_bwd(dy, xn, r, g):
    gdy = dy * g
    dx = r * (gdy - xn * jnp.mean(gdy * xn, axis=-1, keepdims=True))
    return dx, jnp.sum(dy * xn, axis=0, keepdims=True)


def _fold8(a):
    out = a[0:8, :]
    for m in range(1, a.shape[0] // 8):
        out = out + a[8 * m:8 * m + 8, :]
    return out


def _full(shape):
    return pl.BlockSpec(shape, lambda *_: (0,) * len(shape))


def _params(sem=None):
    return pltpu.CompilerParams(dimension_semantics=sem, vmem_limit_bytes=V7X_VMEM_LIMIT)


def _load_rows(dst, src_hbm, row0, rows):
    for k in range(N_CHIPS):
        pltpu.sync_copy(src_hbm.at[k, pl.ds(row0, rows), :], dst.at[pl.ds(rows * k, rows), :])


def _load_cols(dst, src_hbm, cols):
    for k in range(N_CHIPS):
        pltpu.sync_copy(src_hbm.at[k], dst.at[:, pl.ds(cols * k, cols)])


def _fill_phases(src, sh, nrows):
    for r in range(1, 8):
        sh[r, pl.ds(0, nrows), :] = src[pl.ds(r, nrows), pl.ds(D_A, D_B)]


def _phase_rows(src, sh, off, start, size):
    r = off % 8
    if r == 0:
        return src[pl.ds(off + start, size), pl.ds(D_A, D_B)]
    return sh[r, pl.ds(off - r + start, size), :]


def _my_place():
    x, y, c = lax.axis_index("x"), lax.axis_index("y"), lax.axis_index("c")
    return x, y, c, ((1 - x, y), (x, 1 - y), (1 - x, 1 - y))


def _remote(src, dst, send_sem, recv_sem, to):
    return pltpu.make_async_remote_copy(src_ref=src, dst_ref=dst, send_sem=send_sem, recv_sem=recv_sem,
                                        device_id=to, device_id_type=MESH)


def _gather_sems(np_):
    return [pltpu.SemaphoreType.DMA((np_, 3))] * 4 + [pltpu.SemaphoreType.DMA((np_,))] * 2


def _gather_start(srcs, dsts, sems):
    send, recv, _, _, osend, orecv = sems
    x, y, c, chips = _my_place()
    j = 2 * x + y
    for p in range(len(srcs)):
        _remote(srcs[p], dsts[p].at[j], osend.at[p], orecv.at[p], (x, y, 1 - c)).start()
        for nn, (kx, ky) in enumerate(chips):
            _remote(srcs[p].at[c], dsts[p].at[j, c], send.at[p, nn], recv.at[p, nn], (kx, ky, c)).start()


def _gather_forward(srcs, dsts, sems):
    send, recv, fsend, frecv, _, _ = sems
    x, y, c, chips = _my_place()
    for nn, (kx, ky) in enumerate(chips):
        for p in range(len(srcs)):
            blk = dsts[p].at[2 * kx + ky, c]
            _remote(blk, blk, send.at[p, nn], recv.at[p, nn], (kx, ky, c)).wait_recv()
            _remote(blk, blk, fsend.at[p, nn], frecv.at[p, nn], (x, y, 1 - c)).start()


def _gather_finish(srcs, dsts, sems):
    send, recv, fsend, frecv, osend, orecv = sems
    x, y, c, chips = _my_place()
    j = 2 * x + y
    for nn, (kx, ky) in enumerate(chips):
        for p in range(len(srcs)):
            other = dsts[p].at[2 * kx + ky, 1 - c]
            _remote(other, other, fsend.at[p, nn], frecv.at[p, nn], (x, y, 1 - c)).wait_recv()
    for nn, (kx, ky) in enumerate(chips):
        for p in range(len(srcs)):
            _remote(srcs[p].at[c], dsts[p].at[j, c], send.at[p, nn], recv.at[p, nn], (kx, ky, c)).wait_send()
            blk = dsts[p].at[2 * kx + ky, c]
            _remote(blk, blk, fsend.at[p, nn], frecv.at[p, nn], (x, y, 1 - c)).wait_send()
    for p in range(len(srcs)):
        _remote(srcs[p], dsts[p].at[j], osend.at[p], orecv.at[p], (x, y, 1 - c)).wait()


def _split_halves(a):
    return a.reshape(2, a.shape[0] // 2, a.shape[1])


def _gather_weights(packs):
    np_ = len(packs)
    split = [_split_halves(a) for a in packs]

    def body(*refs):
        srcs, dsts, sems = refs[:np_], refs[np_:2 * np_], refs[2 * np_:]
        _gather_start(srcs, dsts, sems)
        _gather_forward(srcs, dsts, sems)
        _gather_finish(srcs, dsts, sems)

    outs = pl.pallas_call(
        body,
        name="gather_weights",
        in_specs=[ANY] * np_,
        out_specs=[ANY] * np_,
        out_shape=[jax.ShapeDtypeStruct((N_CHIPS,) + a.shape, a.dtype) for a in split],
        scratch_shapes=_gather_sems(np_),
    )(*split)
    return [o.reshape((N_CHIPS,) + a.shape) for o, a in zip(outs, packs)]


def _peers(x, y, c):
    out = []
    for d in range(1, N_DEV):
        px = 1 - x if d & 4 else x
        py = 1 - y if d & 2 else y
        pc = 1 - c if d & 1 else c
        out.append(((px, py, pc), 4 * px + 2 * py + pc))
    return out


def _scatter_copies(srcs, dsts, send, recv):
    x, y, c, _ = _my_place()
    me = 4 * x + 2 * y + c
    out = []
    for a in range(len(srcs)):
        for d, ((px, py, pc), pidx) in enumerate(_peers(x, y, c)):
            piece = srcs[a].at[2 * px + py, pc]
            out.append((_remote(piece, dsts[a].at[me], send.at[a, d], recv.at[a, d], (px, py, pc)),
                        _remote(piece, dsts[a].at[pidx], send.at[a, d], recv.at[a, d], (px, py, pc))))
    return out


def _scatter_start(srcs, dsts, send, recv):
    for out_cp, _ in _scatter_copies(srcs, dsts, send, recv):
        out_cp.start()


def _scatter_wait(srcs, dsts, send, recv):
    for out_cp, in_cp in _scatter_copies(srcs, dsts, send, recv):
        in_cp.wait_recv()
        out_cp.wait_send()


def _scatter_sems(na):
    return [pltpu.SemaphoreType.DMA((na, N_DEV - 1))] * 2


def _slots(a):
    return jax.ShapeDtypeStruct((N_DEV,) + a.shape[2:], a.dtype)


def _by_halves(a):
    return a.reshape(a.shape[0], 2, a.shape[1] // 2, a.shape[2])


def _mix_fwd(x, gpre, gpost, wa, wb, bb, lng, lnb, g1g, pr0g, late):
    S, D = x.shape
    T = min(TILE_FWD, S)
    n = S // T
    nl = len(late)
    late_split = [_split_halves(a) for a in late]

    def body(*refs):
        (x_ref, gpre_ref, gpost_ref, wa_ref, wb_ref, bb_ref, lng_ref, lnb_ref, g1g_hbm, pr0g_hbm) = refs[:10]
        srcs = refs[10:10 + nl]
        x1_ref, u_ref, o1_ref, z1_ref = refs[10 + nl:14 + nl]
        dsts = refs[14 + nl:14 + 2 * nl]
        win_v, wout_v, ext, sh, z1buf = refs[14 + 2 * nl:19 + 2 * nl]
        sems = refs[19 + 2 * nl:]
        i = pl.program_id(0)

        @pl.when(i == 0)
        def _():
            _gather_start(srcs, dsts, sems)
            _load_cols(win_v, g1g_hbm, IN_BLK)
            _load_rows(wout_v, pr0g_hbm, 0, ROW_BLK)
            ext[pl.ds(0, HALO), :] = jnp.zeros((HALO, D_A + D_B), F32)

        @pl.when(i == max(n - 2, 0))
        def _():
            _gather_forward(srcs, dsts, sems)

        xv = x_ref[...]
        xn, _ = _rms(xv)
        h = (xn * gpre_ref[...]).astype(BF16)
        u = _mm(h, win_v[...])
        u_ref[...] = u.astype(BF16)
        b_a = u[:, 0:D_A]
        cv = u[:, D_A:2 * D_A] * u[:, 2 * D_A:3 * D_A]
        z0 = u[:, 3 * D_A:3 * D_A + D_B] * _sigmoid(u[:, 3 * D_A + D_B:])
        ext[pl.ds(HALO, T), pl.ds(0, D_A)] = cv
        ext[pl.ds(HALO, T), pl.ds(D_A, D_B)] = z0

        conv_a = ext[pl.ds(HALO - 2, T), pl.ds(0, D_A)] * wa_ref[0:1, :]
        for k in range(1, CONV_A_W):
            conv_a = conv_a + ext[pl.ds(HALO - 2 + k, T), pl.ds(0, D_A)] * wa_ref[k:k + 1, :]
        y_a = b_a * conv_a

        _fill_phases(ext, sh, T + HALO - 8)
        base = HALO - (CONV_B_W - 1)

        def chunk(ci, carry):
            start = pl.multiple_of(ci * CONV_ROWS_FWD, 8)
            acc = jnp.broadcast_to(bb_ref[...], (CONV_ROWS_FWD, D_B))
            for k in range(CONV_B_W):
                acc = acc + _phase_rows(ext, sh, base + k, start, CONV_ROWS_FWD) * wb_ref[k:k + 1, :]
            z1buf[pl.ds(start, CONV_ROWS_FWD), :] = acc
            return carry

        lax.fori_loop(0, T // CONV_ROWS_FWD, chunk, 0)
        z1 = z1buf[...]
        z1_ref[...] = z1.astype(BF16)
        mu = jnp.mean(z1, axis=-1, keepdims=True)
        zc = z1 - mu
        rstd = lax.rsqrt(jnp.mean(zc * zc, axis=-1, keepdims=True) + LN_EPS)
        l = zc * rstd * lng_ref[...] + lnb_ref[...]
        y_b = l * _sigmoid(l)
        y = jnp.concatenate([y_a, y_b], axis=-1).astype(BF16)
        o1 = _mm(y, wout_v[...])
        o1_ref[...] = o1.astype(BF16)
        o1n, _ = _rms(o1)
        x1_ref[...] = xv + o1n * gpost_ref[...]
        ext[pl.ds(0, HALO), :] = ext[pl.ds(T, HALO), :]

        @pl.when(i == n - 1)
        def _():
            _gather_finish(srcs, dsts, sems)

    tok = lambda w: pl.BlockSpec((T, w), lambda i: (i, 0))
    outs = pl.pallas_call(
        body,
        name="mix_fwd",
        grid=(n,),
        in_specs=[tok(D), _full((1, D)), _full((1, D)), _full((CONV_A_W, D_A)), _full((CONV_B_W, D_B)),
                  _full((1, D_B)), _full((1, D_B)), _full((1, D_B)), ANY, ANY] + [ANY] * nl,
        out_specs=[tok(D), tok(D_IN_ALL), tok(D), tok(D_B)] + [ANY] * nl,
        out_shape=[jax.ShapeDtypeStruct((S, D), F32), jax.ShapeDtypeStruct((S, D_IN_ALL), BF16),
                   jax.ShapeDtypeStruct((S, D), BF16), jax.ShapeDtypeStruct((S, D_B), BF16)]
        + [jax.ShapeDtypeStruct((N_CHIPS,) + a.shape, a.dtype) for a in late_split],
        scratch_shapes=[pltpu.VMEM((D, D_IN_ALL), BF16), pltpu.VMEM((D_A + D_B, D), BF16),
                        pltpu.VMEM((HALO + T, D_A + D_B), F32), pltpu.VMEM((8, HALO + T, D_B), F32),
                        pltpu.VMEM((T, D_B), F32)] + _gather_sems(nl),
        compiler_params=_params(("arbitrary",)),
    )(x, gpre, gpost, wa, wb, bb, lng, lnb, g1g, pr0g, *late_split)
    return list(outs[:4]) + [o.reshape((N_CHIPS,) + a.shape) for o, a in zip(outs[4:], late)]


def _mix_bwd(dx1, x, o1, u, z1s, gpre, gpost, wa, wb, lng, lnb, g1g, pr0g, cx):
    S, D = x.shape
    T = min(TILE_BWD, S)
    n = S // T
    hb = T // HALO
    cxs = _by_halves(cx)

    def body(dx1_ref, x_ref, o1_ref, u_ref, uh_ref, z1_ref, gpre_ref, gpost_ref, wa_ref, wb_ref, lng_ref, lnb_ref,
             g1g_hbm, pr0g_hbm, cx_hbm,
             dx_ref, dgpre_ref, dgpost_ref, dwa_ref, dwb_ref, dbb_ref, dlng_ref, dlnb_ref, cm1_hbm, cm2_hbm, yx_hbm,
             win_v, wout_v, dwin_acc, dwout_acc, ext, ext2, shb, dz0buf, dwb_acc, send, recv):
        i = pl.program_id(0)

        @pl.when(i == 0)
        def _():
            _scatter_start([cx_hbm], [yx_hbm], send, recv)
            _load_cols(win_v, g1g_hbm, IN_BLK)
            _load_rows(wout_v, pr0g_hbm, 0, ROW_BLK)
            dwin_acc[...] = jnp.zeros_like(dwin_acc)
            dwout_acc[...] = jnp.zeros_like(dwout_acc)
            dwb_acc[...] = jnp.zeros_like(dwb_acc)
            ext2[pl.ds(T, HALO), :] = jnp.zeros((HALO, D_A + D_B), F32)
            for ref in (dgpre_ref, dgpost_ref, dwa_ref, dbb_ref, dlng_ref, dlnb_ref):
                ref[...] = jnp.zeros_like(ref)

        o1n, r1 = _rms(o1_ref[...].astype(F32))
        dx1v = dx1_ref[...]
        d_o1, dgp = _rms_bwd(dx1v, o1n, r1, gpost_ref[...])
        dgpost_ref[...] += dgp
        d_o1b = d_o1.astype(BF16)
        dy = _mm_nt(d_o1b, wout_v[...])

        first = (i == n - 1).astype(F32)
        uh = uh_ref[...].astype(F32) * (1.0 - first)
        ext[pl.ds(0, HALO), pl.ds(0, D_A)] = uh[:, D_A:2 * D_A] * uh[:, 2 * D_A:3 * D_A]
        uf = u_ref[...].astype(F32)
        b_a = uf[:, 0:D_A]
        c_a = uf[:, D_A:2 * D_A]
        v_a = uf[:, 2 * D_A:3 * D_A]
        gv = uf[:, 3 * D_A:3 * D_A + D_B]
        sg = _sigmoid(uf[:, 3 * D_A + D_B:])
        ext[pl.ds(HALO, T), pl.ds(0, D_A)] = c_a * v_a
        ext[pl.ds(HALO, T), pl.ds(D_A, D_B)] = gv * sg
        conv_a = ext[pl.ds(HALO - 2, T), pl.ds(0, D_A)] * wa_ref[0:1, :]
        for k in range(1, CONV_A_W):
            conv_a = conv_a + ext[pl.ds(HALO - 2 + k, T), pl.ds(0, D_A)] * wa_ref[k:k + 1, :]
        z1 = z1_ref[...].astype(F32)
        mu = jnp.mean(z1, axis=-1, keepdims=True)
        zc = z1 - mu
        rstd = lax.rsqrt(jnp.mean(zc * zc, axis=-1, keepdims=True) + LN_EPS)
        zn = zc * rstd
        l = zn * lng_ref[...] + lnb_ref[...]
        sl = _sigmoid(l)
        y = jnp.concatenate([b_a * conv_a, l * sl], axis=-1).astype(BF16)
        dwout_acc[...] += _mm_tn(y, d_o1b)

        dy_a = dy[:, 0:D_A]
        dl = dy[:, D_A:] * (sl * (1.0 + l * (1.0 - sl)))
        dlng_ref[...] += jnp.sum(dl * zn, axis=0, keepdims=True)
        dlnb_ref[...] += jnp.sum(dl, axis=0, keepdims=True)
        dzn = dl * lng_ref[...]
        dz1 = rstd * (dzn - jnp.mean(dzn, axis=-1, keepdims=True) - zn * jnp.mean(dzn * zn, axis=-1, keepdims=True))
        dbb_ref[...] += jnp.sum(dz1, axis=0, keepdims=True)
        d_conv = dy_a * b_a
        ext2[pl.ds(0, T), pl.ds(0, D_A)] = d_conv
        ext2[pl.ds(0, T), pl.ds(D_A, D_B)] = dz1

        d_cv = ext2[pl.ds(CONV_A_W - 1, T), pl.ds(0, D_A)] * wa_ref[0:1, :]
        for k in range(1, CONV_A_W):
            d_cv = d_cv + ext2[pl.ds(CONV_A_W - 1 - k, T), pl.ds(0, D_A)] * wa_ref[k:k + 1, :]
        for k in range(CONV_A_W):
            dwa_ref[k:k + 1, :] += jnp.sum(d_conv * ext[pl.ds(HALO - 2 + k, T), pl.ds(0, D_A)], axis=0, keepdims=True)

        _fill_phases(ext2, shb, T + HALO - 8)

        def chunk(ci, carry):
            start = pl.multiple_of(ci * CONV_ROWS, 8)
            z0c = ext[pl.ds(HALO + start, CONV_ROWS), pl.ds(D_A, D_B)]
            acc = jnp.zeros((CONV_ROWS, D_B), F32)
            for k in range(CONV_B_W):
                ahead = _phase_rows(ext2, shb, CONV_B_W - 1 - k, start, CONV_ROWS)
                acc = acc + ahead * wb_ref[k:k + 1, :]
                dwb_acc[k] += _fold8(z0c * ahead)
            dz0buf[pl.ds(start, CONV_ROWS), :] = acc
            return carry

        lax.fori_loop(0, T // CONV_ROWS, chunk, 0)
        dz0 = dz0buf[...]
        du = jnp.concatenate([dy_a * conv_a, d_cv * v_a, d_cv * c_a, dz0 * sg, dz0 * gv * sg * (1.0 - sg)],
                             axis=-1).astype(BF16)
        dh = _mm_nt(du, win_v[...])
        xv = x_ref[...]
        xn, r0 = _rms(xv)
        dwin_acc[...] += _mm_tn((xn * gpre_ref[...]).astype(BF16), du)
        dxp, dg0 = _rms_bwd(dh, xn, r0, gpre_ref[...])
        dgpre_ref[...] += dg0
        dx_ref[...] = dx1v + dxp
        ext2[pl.ds(T, HALO), :] = ext2[pl.ds(0, HALO), :]

        @pl.when(i == n - 1)
        def _():
            for k in range(CONV_B_W):
                dwb_ref[k:k + 1, :] = jnp.sum(dwb_acc[k], axis=0, keepdims=True)
            win_v[...] = dwin_acc[...].astype(BF16)
            wout_v[...] = dwout_acc[...].astype(BF16)
            for k in range(N_CHIPS):
                pltpu.sync_copy(win_v.at[:, pl.ds(IN_BLK * k, IN_BLK)], cm1_hbm.at[k])
                pltpu.sync_copy(wout_v.at[pl.ds(ROW_BLK * k, ROW_BLK), :], cm2_hbm.at[k])
            _scatter_wait([cx_hbm], [yx_hbm], send, recv)

    rev = lambda w: pl.BlockSpec((T, w), lambda i: (n - 1 - i, 0))
    halo = pl.BlockSpec((HALO, D_IN_ALL), lambda i: (jnp.maximum((n - 1 - i) * hb - 1, 0), 0))
    return pl.pallas_call(
        body,
        name="mix_bwd",
        grid=(n,),
        in_specs=[rev(D), rev(D), rev(D), rev(D_IN_ALL), halo, rev(D_B), _full((1, D)), _full((1, D)),
                  _full((CONV_A_W, D_A)), _full((CONV_B_W, D_B)), _full((1, D_B)), _full((1, D_B)), ANY, ANY, ANY],
        out_specs=[rev(D), _full((1, D)), _full((1, D)), _full((CONV_A_W, D_A)), _full((CONV_B_W, D_B)),
                   _full((1, D_B)), _full((1, D_B)), _full((1, D_B)), ANY, ANY, ANY],
        out_shape=[jax.ShapeDtypeStruct((S, D), F32), jax.ShapeDtypeStruct((1, D), F32),
                   jax.ShapeDtypeStruct((1, D), F32), jax.ShapeDtypeStruct((CONV_A_W, D_A), F32),
                   jax.ShapeDtypeStruct((CONV_B_W, D_B), F32), jax.ShapeDtypeStruct((1, D_B), F32),
                   jax.ShapeDtypeStruct((1, D_B), F32), jax.ShapeDtypeStruct((1, D_B), F32),
                   jax.ShapeDtypeStruct((N_CHIPS, D, IN_BLK), BF16),
                   jax.ShapeDtypeStruct((N_CHIPS, ROW_BLK, D), BF16), _slots(cxs)],
        scratch_shapes=[pltpu.VMEM((D, D_IN_ALL), BF16), pltpu.VMEM((D_A + D_B, D), BF16),
                        pltpu.VMEM((D, D_IN_ALL), F32), pltpu.VMEM((D_A + D_B, D), F32),
                        pltpu.VMEM((HALO + T, D_A + D_B), F32), pltpu.VMEM((HALO + T, D_A + D_B), F32),
                        pltpu.VMEM((8, HALO + T, D_B), F32),
                        pltpu.VMEM((T, D_B), F32), pltpu.VMEM((CONV_B_W, 8, D_B), F32)] + _scatter_sems(1),
        compiler_params=_params(("arbitrary",)),
    )(dx1, x, o1, u, u, z1s, gpre, gpost, wa, wb, lng, lnb, g1g, pr0g, cxs)


def _mem_kv(mem, gmem, pr1g):
    M, D = mem.shape

    def body(mem_ref, g_ref, pr1g_hbm, memn_ref, k_ref, v_ref, wk_v, wv_v):
        _load_rows(wk_v, pr1g_hbm, P1_K, ROW_BLK)
        _load_rows(wv_v, pr1g_hbm, P1_V, ROW_BLK)
        mn, _ = _rms(mem_ref[...])
        mb = (mn * g_ref[...]).astype(BF16)
        memn_ref[...] = mb
        k_ref[...] = _mm(mb, wk_v[...]).astype(BF16)
        v_ref[...] = _mm(mb, wv_v[...]).astype(BF16)

    return pl.pallas_call(
        body,
        name="mem_kv",
        grid=(1,),
        in_specs=[_full((M, D)), _full((1, D)), ANY],
        out_specs=[_full((M, D))] * 3,
        out_shape=[jax.ShapeDtypeStruct((M, D), BF16)] * 3,
        scratch_shapes=[pltpu.VMEM((D, D), BF16), pltpu.VMEM((D, D), BF16)],
        compiler_params=_params(("arbitrary",)),
    )(mem, gmem, pr1g)


def _attend(qb, kb, vb):
    scale = HEAD_DIM ** -0.5
    ps, os_ = [], []
    for hd in range(XA_HEADS):
        cols = slice(HEAD_DIM * hd, HEAD_DIM * (hd + 1))
        s = _mm_nt(qb[:, cols], kb[:, cols]) * scale
        e = jnp.exp(s - jnp.max(s, axis=-1, keepdims=True))
        p = e * (1.0 / jnp.sum(e, axis=-1, keepdims=True))
        ps.append(p)
        os_.append(_mm(p.astype(BF16), vb[:, cols]))
    return ps, jnp.concatenate(os_, axis=-1).astype(BF16)


def _xattn_fwd(x1, gpre, gpost, kb, vb, pr1g):
    S, D = x1.shape
    M = kb.shape[0]
    T = min(TILE_XATTN_FWD, S)
    n = S // T

    def body(x1_ref, gpre_ref, gpost_ref, k_ref, v_ref, pr1g_hbm, x2_ref, q_ref, o2_ref, wq_v, wo_v):
        @pl.when(pl.program_id(0) == 0)
        def _():
            _load_rows(wq_v, pr1g_hbm, P1_Q, ROW_BLK)
            _load_rows(wo_v, pr1g_hbm, P1_O, ROW_BLK)

        xv = x1_ref[...]
        xn, _ = _rms(xv)
        qb = _mm((xn * gpre_ref[...]).astype(BF16), wq_v[...]).astype(BF16)
        q_ref[...] = qb
        _, ob = _attend(qb, k_ref[...], v_ref[...])
        o2 = _mm(ob, wo_v[...])
        o2_ref[...] = o2.astype(BF16)
        o2n, _ = _rms(o2)
        x2_ref[...] = xv + o2n * gpost_ref[...]

    tok = lambda w: pl.BlockSpec((T, w), lambda i: (i, 0))
    return pl.pallas_call(
        body,
        name="xattn_fwd",
        grid=(n,),
        in_specs=[tok(D), _full((1, D)), _full((1, D)), _full((M, D)), _full((M, D)), ANY],
        out_specs=[tok(D), tok(D), tok(D)],
        out_shape=[jax.ShapeDtypeStruct((S, D), F32), jax.ShapeDtypeStruct((S, D), BF16),
                   jax.ShapeDtypeStruct((S, D), BF16)],
        scratch_shapes=[pltpu.VMEM((D, D), BF16), pltpu.VMEM((D, D), BF16)],
        compiler_params=_params(("arbitrary",)),
    )(x1, gpre, gpost, kb, vb, pr1g)


def _xattn_bwd(dx3, dh3p, x2, x1, o2, q, kb, vb, gffn, gpost, gpre, pr1g, cf1, cf2):
    S, D = x1.shape
    M = kb.shape[0]
    T = min(TILE_BWD, S)
    n = S // T
    scale = HEAD_DIM ** -0.5
    nparts = dh3p.shape[0]
    cfs = [_by_halves(cf1), _by_halves(cf2)]

    def body(*refs):
        dx3_ref, dh3_refs = refs[0], refs[1:1 + nparts]
        (x2_ref, x1_ref, o2_ref, q_ref, k_ref, v_ref, gffn_ref, gpost_ref, gpre_ref, pr1g_hbm, cf1_hbm, cf2_hbm,
         dx1_ref, dgffn_ref, dgpost_ref, dgpre_ref, dk_ref, dv_ref, cx_hbm, yf1_hbm, yf2_hbm,
         wq_v, wo_v, dwq_acc, dwo_acc, send, recv) = refs[1 + nparts:]
        i = pl.program_id(0)

        @pl.when(i == 0)
        def _():
            _scatter_start([cf1_hbm, cf2_hbm], [yf1_hbm, yf2_hbm], send, recv)
            _load_rows(wq_v, pr1g_hbm, P1_Q, ROW_BLK)
            _load_rows(wo_v, pr1g_hbm, P1_O, ROW_BLK)
            dwq_acc[...] = jnp.zeros_like(dwq_acc)
            dwo_acc[...] = jnp.zeros_like(dwo_acc)
            for ref in (dgffn_ref, dgpost_ref, dgpre_ref, dk_ref, dv_ref):
                ref[...] = jnp.zeros_like(ref)

        x2n, r2 = _rms(x2_ref[...])
        dh3 = dh3_refs[0][...].astype(F32)
        for ref in dh3_refs[1:]:
            dh3 = dh3 + ref[...].astype(F32)
        dxp, dg = _rms_bwd(dh3, x2n, r2, gffn_ref[...])
        dgffn_ref[...] += dg
        dx2 = dx3_ref[...] + dxp
        o2n, ro = _rms(o2_ref[...].astype(F32))
        d_o2, dg = _rms_bwd(dx2, o2n, ro, gpost_ref[...])
        dgpost_ref[...] += dg
        d_o2b = d_o2.astype(BF16)
        d_o = _mm_nt(d_o2b, wo_v[...]).astype(BF16)
        qb = q_ref[...]
        kv = k_ref[...]
        vv = v_ref[...]
        ps, ob = _attend(qb, kv, vv)
        dwo_acc[...] += _mm_tn(ob, d_o2b)
        dqs = []
        for hd in range(XA_HEADS):
            cols = slice(HEAD_DIM * hd, HEAD_DIM * (hd + 1))
            p = ps[hd]
            dp = _mm_nt(d_o[:, cols], vv[:, cols])
            dv_ref[:, cols] += _mm_tn(p.astype(BF16), d_o[:, cols])
            ds = (p * (dp - jnp.sum(p * dp, axis=-1, keepdims=True)) * scale).astype(BF16)
            dqs.append(_mm(ds, kv[:, cols]))
            dk_ref[:, cols] += _mm_tn(ds, qb[:, cols])
        dq = jnp.concatenate(dqs, axis=-1).astype(BF16)
        dh2 = _mm_nt(dq, wq_v[...])
        x1n, r1 = _rms(x1_ref[...])
        dwq_acc[...] += _mm_tn((x1n * gpre_ref[...]).astype(BF16), dq)
        dxp, dg = _rms_bwd(dh2, x1n, r1, gpre_ref[...])
        dgpre_ref[...] += dg
        dx1_ref[...] = dx2 + dxp

        @pl.when(i == n - 1)
        def _():
            wq_v[...] = dwq_acc[...].astype(BF16)
            wo_v[...] = dwo_acc[...].astype(BF16)
            for k in range(N_CHIPS):
                rows = pl.ds(ROW_BLK * k, ROW_BLK)
                pltpu.sync_copy(wq_v.at[rows, :], cx_hbm.at[k, pl.ds(GX_Q, ROW_BLK), :])
                pltpu.sync_copy(wo_v.at[rows, :], cx_hbm.at[k, pl.ds(GX_O, ROW_BLK), :])
            _scatter_wait([cf1_hbm, cf2_hbm], [yf1_hbm, yf2_hbm], send, recv)

    tok = lambda w: pl.BlockSpec((T, w), lambda i: (i, 0))
    part = lambda j: pl.BlockSpec((None, T, D), lambda i: (j, i, 0))
    return pl.pallas_call(
        body,
        name="xattn_bwd",
        grid=(n,),
        in_specs=[tok(D)] + [part(j) for j in range(nparts)] + [tok(D), tok(D), tok(D), tok(D), _full((M, D)),
                                                                 _full((M, D)), _full((1, D)), _full((1, D)),
                                                                 _full((1, D)), ANY, ANY, ANY],
        out_specs=[tok(D), _full((1, D)), _full((1, D)), _full((1, D)), _full((M, D)), _full((M, D)), ANY, ANY, ANY],
        out_shape=[jax.ShapeDtypeStruct((S, D), F32), jax.ShapeDtypeStruct((1, D), F32),
                   jax.ShapeDtypeStruct((1, D), F32), jax.ShapeDtypeStruct((1, D), F32),
                   jax.ShapeDtypeStruct((M, D), F32), jax.ShapeDtypeStruct((M, D), F32),
                   jax.ShapeDtypeStruct((N_CHIPS, D, D), BF16), _slots(cfs[0]), _slots(cfs[1])],
        scratch_shapes=[pltpu.VMEM((D, D), BF16), pltpu.VMEM((D, D), BF16),
                        pltpu.VMEM((D, D), F32), pltpu.VMEM((D, D), F32)] + _scatter_sems(2),
        compiler_params=_params(("arbitrary",)),
    )(dx3, *([dh3p] * nparts), x2, x1, o2, q, kb, vb, gffn, gpost, gpre, pr1g, *cfs)


def _mem_bwd(dk, dv, mem, memn, gmem, pr1g, cx_in):
    M, D = mem.shape

    def body(dk_ref, dv_ref, mem_ref, memn_ref, g_ref, pr1g_hbm, cx_hbm, dg_ref, cx_out, wk_v, wv_v):
        del cx_hbm
        _load_rows(wk_v, pr1g_hbm, P1_K, ROW_BLK)
        _load_rows(wv_v, pr1g_hbm, P1_V, ROW_BLK)
        dkb = dk_ref[...].astype(BF16)
        dvb = dv_ref[...].astype(BF16)
        mb = memn_ref[...]
        dmn = _mm_nt(dkb, wk_v[...]) + _mm_nt(dvb, wv_v[...])
        mn, _ = _rms(mem_ref[...])
        dg_ref[...] = jnp.sum(dmn * mn, axis=0, keepdims=True)
        wk_v[...] = _mm_tn(mb, dkb).astype(BF16)
        wv_v[...] = _mm_tn(mb, dvb).astype(BF16)
        for k in range(N_CHIPS):
            rows = pl.ds(ROW_BLK * k, ROW_BLK)
            pltpu.sync_copy(wk_v.at[rows, :], cx_out.at[k, pl.ds(GX_K, ROW_BLK), :])
            pltpu.sync_copy(wv_v.at[rows, :], cx_out.at[k, pl.ds(GX_V, ROW_BLK), :])

    return pl.pallas_call(
        body,
        name="mem_bwd",
        grid=(1,),
        in_specs=[_full((M, D)), _full((M, D)), _full((M, D)), _full((M, D)), _full((1, D)), ANY, ANY],
        out_specs=[_full((1, D)), ANY],
        out_shape=[jax.ShapeDtypeStruct((1, D), F32), jax.ShapeDtypeStruct(cx_in.shape, BF16)],
        input_output_aliases={6: 1},
        scratch_shapes=[pltpu.VMEM((D, D), BF16), pltpu.VMEM((D, D), BF16)],
        compiler_params=_params(("arbitrary",)),
    )(dk, dv, mem, memn, gmem, pr1g, cx_in)


def _ffn_fwd(x2, target, gpre, gpost, g2g, pr1g):
    S, D = x2.shape
    T = min(TILE_FFN, S)
    n = S // T

    def body(x2_ref, t_ref, gpre_ref, gpost_ref, g2g_hbm, pr1g_hbm,
             h3_ref, g_hbm, u_hbm, do3_ref, dx3_ref, loss_ref, dgpost_ref, wg_v, wu_v, wd_v, gst, ust, sem):
        i = pl.program_id(0)

        @pl.when(i == 0)
        def _():
            pltpu.sync_copy(g2g_hbm.at[:, pl.ds(0, D), :], wg_v)
            pltpu.sync_copy(g2g_hbm.at[:, pl.ds(D, D), :], wu_v)
            pltpu.sync_copy(pr1g_hbm.at[:, pl.ds(P1_DOWN, FF_BLK), :], wd_v)
            loss_ref[...] = jnp.zeros_like(loss_ref)
            dgpost_ref[...] = jnp.zeros_like(dgpost_ref)

        xv = x2_ref[...]
        xn, _ = _rms(xv)
        hb = (xn * gpre_ref[...]).astype(BF16)
        h3_ref[...] = hb
        o3 = jnp.zeros((T, D), F32)
        out = [None, None]
        for c in range(N_CHIPS):
            slot = c % 2
            if out[slot] is not None:
                for cp in out[slot]:
                    cp.wait()
            g = _mm(hb, wg_v[c])
            u = _mm(hb, wu_v[c])
            gst[slot] = g.astype(BF16)
            ust[slot] = u.astype(BF16)
            out[slot] = (pltpu.make_async_copy(gst.at[slot], g_hbm.at[c, i], sem.at[0, slot]),
                         pltpu.make_async_copy(ust.at[slot], u_hbm.at[c, i], sem.at[1, slot]))
            for cp in out[slot]:
                cp.start()
            o3 = o3 + _mm((g * _sigmoid(g) * u).astype(BF16), wd_v[c])
        for pair in out:
            for cp in pair:
                cp.wait()
        o3n, r3 = _rms(o3)
        diff = xv + o3n * gpost_ref[...] - t_ref[...]
        sq = jnp.sum(jnp.sum(diff * diff, axis=-1, keepdims=True), axis=0, keepdims=True)
        loss_ref[...] += sq * (0.5 / D)
        dx3 = diff * (1.0 / D)
        dx3_ref[...] = dx3
        d_o3, dg = _rms_bwd(dx3, o3n, r3, gpost_ref[...])
        dgpost_ref[...] += dg
        do3_ref[...] = d_o3.astype(BF16)

    tok = lambda w: pl.BlockSpec((T, w), lambda i: (i, 0))
    h3, gs, us, do3, dx3, loss, dgpost = pl.pallas_call(
        body,
        name="ffn_fwd",
        grid=(n,),
        in_specs=[tok(D), tok(D), _full((1, D)), _full((1, D)), ANY, ANY],
        out_specs=[tok(D), ANY, ANY, tok(D), tok(D), _full((1, 128)), _full((1, D))],
        out_shape=[jax.ShapeDtypeStruct((S, D), BF16), jax.ShapeDtypeStruct((N_CHIPS, n, T, FF_BLK), BF16),
                   jax.ShapeDtypeStruct((N_CHIPS, n, T, FF_BLK), BF16), jax.ShapeDtypeStruct((S, D), BF16),
                   jax.ShapeDtypeStruct((S, D), F32), jax.ShapeDtypeStruct((1, 128), F32),
                   jax.ShapeDtypeStruct((1, D), F32)],
        scratch_shapes=[pltpu.VMEM((N_CHIPS, D, FF_BLK), BF16), pltpu.VMEM((N_CHIPS, D, FF_BLK), BF16),
                        pltpu.VMEM((N_CHIPS, FF_BLK, D), BF16), pltpu.VMEM((2, T, FF_BLK), BF16),
                        pltpu.VMEM((2, T, FF_BLK), BF16), pltpu.SemaphoreType.DMA((2, 2))],
        compiler_params=_params(("arbitrary",)),
    )(x2, target, gpre, gpost, g2g, pr1g)
    return h3, gs.reshape(N_CHIPS, S, FF_BLK), us.reshape(N_CHIPS, S, FF_BLK), do3, dx3, loss, dgpost


def _ffn_bwd(h3, do3, gs, us, g2g, pr1g):
    S, D = h3.shape
    T = min(TILE_FFN, S)
    n = S // T
    NP = FFN_BWD_BLOCKS

    def body(h3_ref, do3_ref, g_ref, u_ref, g2g_hbm, pr1g_hbm, dh3_ref, cf1_hbm, cf2_hbm,
             wg_v, wu_v, wd_v, dwg_acc, dwu_acc, dwd_acc):
        jp = pl.program_id(0)
        i = pl.program_id(1)
        blocks = pl.ds(NP * jp, NP)

        @pl.when(i == 0)
        def _():
            pltpu.sync_copy(g2g_hbm.at[blocks, pl.ds(0, D), :], wg_v)
            pltpu.sync_copy(g2g_hbm.at[blocks, pl.ds(D, D), :], wu_v)
            pltpu.sync_copy(pr1g_hbm.at[blocks, pl.ds(P1_DOWN, FF_BLK), :], wd_v)
            dwg_acc[...] = jnp.zeros_like(dwg_acc)
            dwu_acc[...] = jnp.zeros_like(dwu_acc)
            dwd_acc[...] = jnp.zeros_like(dwd_acc)

        hb = h3_ref[...]
        d_o3 = do3_ref[...]
        dh = jnp.zeros((T, D), F32)
        for c in range(NP):
            da = _mm_nt(d_o3, wd_v[c])
            g = g_ref[c].astype(F32)
            u = u_ref[c].astype(F32)
            sg = _sigmoid(g)
            sl = g * sg
            dwd_acc[c] += _mm_tn((sl * u).astype(BF16), d_o3)
            dub = (da * sl).astype(BF16)
            dgb = (da * u * (sg * (1.0 + g * (1.0 - sg)))).astype(BF16)
            dwg_acc[c] += _mm_tn(dgb, hb)
            dwu_acc[c] += _mm_tn(dub, hb)
            dh = dh + _mm_nt(dgb, wg_v[c]) + _mm_nt(dub, wu_v[c])
        dh3_ref[...] = dh.astype(BF16)

        @pl.when(i == n - 1)
        def _():
            wd_v[...] = dwg_acc[...].astype(BF16)
            pltpu.sync_copy(wd_v, cf1_hbm.at[blocks, pl.ds(0, FF_BLK), :])
            wd_v[...] = dwu_acc[...].astype(BF16)
            pltpu.sync_copy(wd_v, cf1_hbm.at[blocks, pl.ds(FF_BLK, FF_BLK), :])
            wd_v[...] = dwd_acc[...].astype(BF16)
            pltpu.sync_copy(wd_v, cf2_hbm.at[blocks])

    tok = lambda w: pl.BlockSpec((T, w), lambda jp, i: (i, 0))
    blk = pl.BlockSpec((NP, T, FF_BLK), lambda jp, i: (jp, i, 0))
    return pl.pallas_call(
        body,
        name="ffn_bwd",
        grid=(N_CHIPS // NP, n),
        in_specs=[tok(D), tok(D), blk, blk, ANY, ANY],
        out_specs=[pl.BlockSpec((None, T, D), lambda jp, i: (jp, i, 0)), ANY, ANY],
        out_shape=[jax.ShapeDtypeStruct((N_CHIPS // NP, S, D), BF16),
                   jax.ShapeDtypeStruct((N_CHIPS, 2 * FF_BLK, D), BF16),
                   jax.ShapeDtypeStruct((N_CHIPS, FF_BLK, D), BF16)],
        scratch_shapes=[pltpu.VMEM((NP, D, FF_BLK), BF16), pltpu.VMEM((NP, D, FF_BLK), BF16),
                        pltpu.VMEM((NP, FF_BLK, D), BF16), pltpu.VMEM((NP, FF_BLK, D), F32),
                        pltpu.VMEM((NP, FF_BLK, D), F32), pltpu.VMEM((NP, FF_BLK, D), F32)],
        compiler_params=_params(("arbitrary", "arbitrary")),
    )(h3, do3, gs, us, g2g, pr1g)


def _exchange_last(contribs, small):
    na = len(contribs)
    cs = [_by_halves(a) for a in contribs]

    def body(*refs):
        srcs, small_ref = refs[:na], refs[na]
        dsts, small_all = refs[na + 1:2 * na + 1], refs[2 * na + 1]
        send, recv, ssend, srecv, lsem = refs[2 * na + 2:]
        x, y, c, _ = _my_place()
        me = 4 * x + 2 * y + c
        _scatter_start(srcs, dsts, send, recv)
        loc = pltpu.make_async_copy(small_ref, small_all.at[me], lsem)
        loc.start()
        scps = []
        for d, (peer, pidx) in enumerate(_peers(x, y, c)):
            cp = _remote(small_ref, small_all.at[me], ssend.at[d], srecv.at[d], peer)
            cp.start()
            scps.append((cp, _remote(small_ref, small_all.at[pidx], ssend.at[d], srecv.at[d], peer)))
        _scatter_wait(srcs, dsts, send, recv)
        for out_cp, in_cp in scps:
            in_cp.wait_recv()
            out_cp.wait_send()
        loc.wait()

    outs = pl.pallas_call(
        body,
        name="exchange_last",
        in_specs=[ANY] * (na + 1),
        out_specs=[ANY] * (na + 1),
        out_shape=[_slots(a) for a in cs] + [jax.ShapeDtypeStruct((N_DEV,) + small.shape, F32)],
        scratch_shapes=_scatter_sems(na) + [pltpu.SemaphoreType.DMA((N_DEV - 1,))] * 2 + [pltpu.SemaphoreType.DMA],
    )(*cs, small)
    return outs[:na], outs[na]


def _sum_peers(parts, own, place, steps, tag):
    _, rows, w = parts.shape
    tr = rows // steps

    def body(place_ref, *refs):
        p_refs, own_ref, o_ref = refs[:N_DEV], refs[N_DEV], refs[N_DEV + 1]
        me = place_ref[2]
        acc = None
        for s in range(N_DEV):
            term = jnp.where(me == s, own_ref[...], p_refs[s][...]).astype(F32)
            acc = term if acc is None else acc + term
        o_ref[...] = acc

    def other(s):
        return lambda i, pr: (jnp.where(pr[2] == s, (s + 1) % N_DEV, s), i, 0)

    return pl.pallas_call(
        body,
        name="sum_peers_" + tag,
        grid_spec=pltpu.PrefetchScalarGridSpec(
            num_scalar_prefetch=1,
            grid=(steps,),
            in_specs=[pl.BlockSpec((None, tr, w), other(s)) for s in range(N_DEV)]
            + [pl.BlockSpec((None, None, tr, w), lambda i, pr: (pr[0], pr[1], i, 0))],
            out_specs=pl.BlockSpec((None, tr, w), lambda i, pr: (pr[1], i, 0)),
        ),
        out_shape=jax.ShapeDtypeStruct((2, rows, w), F32),
        compiler_params=_params(("arbitrary",)),
    )(place, *([parts] * N_DEV), _by_halves(own))


def _sum_slots(parts, tag):
    nslot, rows, w = parts.shape

    def body(p_ref, o_ref):
        acc = p_ref[0]
        for k in range(1, nslot):
            acc = acc + p_ref[k]
        o_ref[...] = acc

    return pl.pallas_call(
        body,
        name="sum_slots_" + tag,
        grid=(1,),
        in_specs=[_full((nslot, rows, w))],
        out_specs=_full((rows, w)),
        out_shape=jax.ShapeDtypeStruct((rows, w), F32),
        compiler_params=_params(("arbitrary",)),
    )(parts)


def _pair_gather(bufs):
    np_ = len(bufs)

    def body(*refs):
        srcs, dsts = refs[:np_], refs[np_:2 * np_]
        send, recv = refs[2 * np_:]
        x, y, c, _ = _my_place()
        cps = []
        for p in range(np_):
            cp = _remote(srcs[p].at[c], dsts[p].at[c], send.at[p], recv.at[p], (x, y, 1 - c))
            cp.start()
            cps.append(cp)
        for p, cp in enumerate(cps):
            other = dsts[p].at[1 - c]
            _remote(other, other, send.at[p], recv.at[p], (x, y, 1 - c)).wait_recv()
            cp.wait_send()

    outs = pl.pallas_call(
        body,
        name="pair_gather",
        in_specs=[ANY] * np_,
        out_specs=[ANY] * np_,
        out_shape=[jax.ShapeDtypeStruct(a.shape, F32) for a in bufs],
        input_output_aliases={p: p for p in range(np_)},
        scratch_shapes=[pltpu.SemaphoreType.DMA((np_,))] * 2,
    )(*bufs)
    return [o.reshape(2 * a.shape[1], a.shape[2]) for o, a in zip(outs, bufs)]


def _adamw(gsrc, row0, w, m, v, tr, tag):
    rows, width = w.shape
    off = row0 // tr
    bc1 = 1.0 - ADAM_B1 ** ADAM_STEP
    bc2 = 1.0 - ADAM_B2 ** ADAM_STEP

    def body(g_ref, w_ref, m_ref, v_ref, go_ref, d_ref, mo_ref, vo_ref):
        g = g_ref[...]
        m2 = ADAM_B1 * m_ref[...] + (1.0 - ADAM_B1) * g
        v2 = ADAM_B2 * v_ref[...] + (1.0 - ADAM_B2) * (g * g)
        go_ref[...] = g
        mo_ref[...] = m2
        vo_ref[...] = v2
        d_ref[...] = -ADAM_LR * ((m2 / bc1) / (jnp.sqrt(v2 / bc2) + ADAM_EPS) + ADAM_WD * w_ref[...])

    here = pl.BlockSpec((tr, width), lambda i: (i, 0))
    return pl.pallas_call(
        body,
        name="adamw_" + tag,
        grid=(rows // tr,),
        in_specs=[pl.BlockSpec((tr, width), lambda i: (off + i, 0)), here, here, here],
        out_specs=[here] * 4,
        out_shape=[jax.ShapeDtypeStruct((rows, width), F32)] * 4,
        compiler_params=_params(("arbitrary",)),
    )(gsrc, w, m, v)


def kernel(x, mem, mix_pre_g, w_mix_in, conv_a_w, conv_b_w, conv_b_b, ln_b_g, ln_b_b, w_mix_out, mix_post_g, xa_pre_g, mem_norm_g, w_q, w_k, w_v, w_o, xa_post_g, ffn_pre_g, w_gate, w_up, w_down, ffn_post_g, loss_target, m_mix_pre_g, m_w_mix_in, m_conv_a_w, m_conv_b_w, m_conv_b_b, m_ln_b_g, m_ln_b_b, m_w_mix_out, m_mix_post_g, m_xa_pre_g, m_mem_norm_g, m_w_q, m_w_k, m_w_v, m_w_o, m_xa_post_g, m_ffn_pre_g, m_w_gate, m_w_up, m_w_down, m_ffn_post_g, v_mix_pre_g, v_w_mix_in, v_conv_a_w, v_conv_b_w, v_conv_b_b, v_ln_b_g, v_ln_b_b, v_w_mix_out, v_mix_post_g, v_xa_pre_g, v_mem_norm_g, v_w_q, v_w_k, v_w_v, v_w_o, v_xa_post_g, v_ffn_pre_g, v_w_gate, v_w_up, v_w_down, v_ffn_post_g):
    given = dict(locals())
    names = ["mix_pre_g", "w_mix_in", "conv_a_w", "conv_b_w", "conv_b_b", "ln_b_g", "ln_b_b", "w_mix_out",
             "mix_post_g", "xa_pre_g", "mem_norm_g", "w_q", "w_k", "w_v", "w_o", "xa_post_g", "ffn_pre_g",
             "w_gate", "w_up", "w_down", "ffn_post_g"]
    row = lambda a: a.reshape(1, -1)
    cx, cy, cc = lax.axis_index("x"), lax.axis_index("y"), lax.axis_index("c")
    chip = 2 * cx + cy
    ca_blk = conv_a_w.shape[1]

    conv_rows = CONV_A_W + CONV_B_W
    sw = jnp.concatenate([conv_a_w, conv_b_w, jnp.zeros((SMALL_W_ROWS - conv_rows, ca_blk), F32)], axis=0)
    g1g, pr0g, swg = _gather_weights([w_mix_in.astype(BF16), w_mix_out.astype(BF16), sw])
    conv_full = jnp.transpose(swg[:, :conv_rows, :], (1, 0, 2)).reshape(conv_rows, N_CHIPS * ca_blk)
    wa, wb = conv_full[:CONV_A_W], conv_full[CONV_A_W:]
    pr1 = jnp.concatenate([w_q, w_k, w_v, w_o, w_down], axis=0).astype(BF16)
    g2 = jnp.concatenate([w_gate, w_up], axis=0).astype(BF16)

    xs, ms, tgt = x[0], mem[0], loss_target[0]
    x1, u, o1, z1, pr1g, g2g = _mix_fwd(xs, row(mix_pre_g), row(mix_post_g), wa, wb, row(conv_b_b), row(ln_b_g),
                                        row(ln_b_b), g1g, pr0g, [pr1, g2])
    memn, kb, vb = _mem_kv(ms, row(mem_norm_g), pr1g)
    x2, q, o2 = _xattn_fwd(x1, row(xa_pre_g), row(xa_post_g), kb, vb, pr1g)
    h3, gs, us, do3, dx3, loss_part, d_ffn_post = _ffn_fwd(x2, tgt, row(ffn_pre_g), row(ffn_post_g), g2g, pr1g)

    dh3p, cf1, cf2 = _ffn_bwd(h3, do3, gs, us, g2g, pr1g)
    dx1, d_ffn_pre, d_xa_post, d_xa_pre, dk, dv, cxa, yf1, yf2 = _xattn_bwd(
        dx3, dh3p, x2, x1, o2, q, kb, vb, row(ffn_pre_g), row(xa_post_g), row(xa_pre_g), pr1g, cf1, cf2)
    d_mem_g, cxa = _mem_bwd(dk, dv, ms, memn, row(mem_norm_g), pr1g, cxa)
    dx, d_mix_pre, d_mix_post, dwa, dwb, dbb, dlng, dlnb, cm1, cm2, yx = _mix_bwd(
        dx1, xs, o1, u, z1, row(mix_pre_g), row(mix_post_g), wa, wb, row(ln_b_g), row(ln_b_b), g1g, pr0g, cxa)

    small_parts = [d_mix_pre, dwa, dwb, dbb, dlng, dlnb, d_mix_post, d_xa_pre, d_mem_g, d_xa_post, d_ffn_pre,
                   d_ffn_post, loss_part]
    sizes = [p.size for p in small_parts]
    small = jnp.concatenate([p.reshape(-1) for p in small_parts])
    small_rows = -(-small.size // (8 * 128)) * 8
    small = jnp.pad(small, (0, small_rows * 128 - small.size)).reshape(small_rows, 128)
    (ym1, ym2), small_all = _exchange_last([cm1, cm2], small)
    place = jnp.stack([chip, cc, 2 * chip + cc]).astype(jnp.int32)
    landed = [(yf1, cf1, "gate_up"), (yf2, cf2, "down"), (yx, cxa, "attn"), (ym1, cm1, "mix_in"), (ym2, cm2, "mix_out")]
    r_gu, r_down, r_attn, r_in, r_out = _pair_gather([_sum_peers(y, c, place, 2, t) for y, c, t in landed])
    small_sum = _sum_slots(small_all, "small").reshape(-1)
    red, pos = [], 0
    for p, sz in zip(small_parts, sizes):
        red.append(small_sum[pos:pos + sz].reshape(p.shape))
        pos += sz
    (r_mix_pre, r_wa, r_wb, r_bb, r_lng, r_lnb, r_mix_post, r_xa_pre, r_mem_g, r_xa_post, r_ffn_pre, r_ffn_post,
     r_loss) = red
    loss = r_loss[0, 0]

    res = {}
    big = {"w_mix_out": (r_out, 0, 256), "w_q": (r_attn, GX_Q, 256), "w_k": (r_attn, GX_K, 256),
           "w_v": (r_attn, GX_V, 256), "w_o": (r_attn, GX_O, 256), "w_down": (r_down, 0, FF_BLK // 2),
           "w_mix_in": (r_in, 0, 512)}
    for nm, (src, row0, tr) in big.items():
        res[nm] = _adamw(src, row0, given[nm], given["m_" + nm], given["v_" + nm], tr, nm)
    for nm, row0 in (("w_gate", 0), ("w_up", FF_BLK)):
        outs = _adamw(r_gu, row0, given[nm].T, given["m_" + nm].T, given["v_" + nm].T, FF_BLK // 2, nm)
        res[nm] = [o.T for o in outs]
    small_grads = {"mix_pre_g": r_mix_pre, "conv_b_b": r_bb, "ln_b_g": r_lng, "ln_b_b": r_lnb,
                   "mix_post_g": r_mix_post, "xa_pre_g": r_xa_pre, "mem_norm_g": r_mem_g, "xa_post_g": r_xa_post,
                   "ffn_pre_g": r_ffn_pre, "ffn_post_g": r_ffn_post,
                   "conv_a_w": lax.dynamic_slice_in_dim(r_wa, chip * ca_blk, ca_blk, axis=1),
                   "conv_b_w": lax.dynamic_slice_in_dim(r_wb, chip * ca_blk, ca_blk, axis=1)}
    small_names = list(small_grads)

    def packed(prefix, grads=None):
        flat = jnp.concatenate([(grads[nm] if grads else given[prefix + nm]).reshape(-1) for nm in small_names])
        rows8 = -(-flat.size // (8 * 128)) * 8
        return jnp.pad(flat, (0, rows8 * 128 - flat.size)).reshape(rows8, 128)

    gp = packed("", small_grads)
    outs = _adamw(gp, 0, packed(""), packed("m_"), packed("v_"), gp.shape[0], "small")
    pos = 0
    for nm in small_names:
        shape = given[nm].shape
        sz = given[nm].size
        res[nm] = [o.reshape(-1)[pos:pos + sz].reshape(shape) for o in outs]
        pos += sz

    return (loss, dx[None], *[res[nm][0] for nm in names], *[res[nm][1] for nm in names],
            *[res[nm][2] for nm in names], *[res[nm][3] for nm in names])
```

```python
import jax
import jax.numpy as jnp
from jax import lax
from jax.experimental import pallas as pl
from jax.experimental.pallas import tpu as pltpu

F32 = jnp.float32
BF16 = jnp.bfloat16
MESH = pl.DeviceIdType.MESH

RMS_EPS = 1e-6
LN_EPS = 1e-5
D_MODEL = 1024
D_A = 512
D_B = 512
D_IN_ALL = 3 * D_A + 2 * D_B
CONV_A_W = 3
CONV_B_W = 31
HALO = 32
XA_HEADS = 4
HEAD_DIM = 256
D_FF = 2816
N_CHIPS = 4
N_DEV = 8
FF_BLK = D_FF // N_CHIPS
IN_BLK = D_IN_ALL // N_CHIPS
ROW_BLK = D_MODEL // N_CHIPS

ADAM_LR = 0.001
ADAM_B1 = 0.9
ADAM_B2 = 0.999
ADAM_EPS = 1e-08
ADAM_WD = 0.01
ADAM_STEP = 10

TILE_FWD = 512
TILE_XATTN_FWD = 1024
TILE_FFN = 512
TILE_BWD = 256
FFN_BWD_BLOCKS = 2
CONV_ROWS_FWD = 64
CONV_ROWS = 32
V7X_VMEM_LIMIT = 56 * 1024 * 1024

P1_Q, P1_K, P1_V, P1_O, P1_DOWN = 0, 256, 512, 768, 1024
P1_ROWS = P1_DOWN + FF_BLK
GX_Q, GX_K, GX_V, GX_O = 0, 256, 512, 768
SMALL_W_ROWS = 48

ANY = pl.BlockSpec(memory_space=pl.ANY)


def _mm(a, b):
    return lax.dot_general(a, b, (((1,), (0,)), ((), ())), preferred_element_type=F32)


def _mm_nt(a, b):
    return lax.dot_general(a, b, (((1,), (1,)), ((), ())), preferred_element_type=F32)


def _mm_tn(a, b):
    return lax.dot_general(a, b, (((0,), (0,)), ((), ())), preferred_element_type=F32)


def _sigmoid(x):
    return 0.5 * jnp.tanh(0.5 * x) + 0.5


def _rms(x):
    r = lax.rsqrt(jnp.mean(x * x, axis=-1, keepdims=True) + RMS_EPS)
    return x * r, r


def _rms_bwd(dy, xn, r, g):
    gdy = dy * g
    dx = r * (gdy - xn * jnp.mean(gdy * xn, axis=-1, keepdims=True))
    return dx, jnp.sum(dy * xn, axis=0, keepdims=True)


def _fold8(a):
    out = a[0:8, :]
    for m in range(1, a.shape[0] // 8):
        out = out + a[8 * m:8 * m + 8, :]
    return out


def _full(shape):
    return pl.BlockSpec(shape, lambda *_: (0,) * len(shape))


def _params(sem=None):
    return pltpu.CompilerParams(dimension_semantics=sem, vmem_limit_bytes=V7X_VMEM_LIMIT)


def _load_rows(dst, src_hbm, row0, rows):
    for k in range(N_CHIPS):
        pltpu.sync_copy(src_hbm.at[k, pl.ds(row0, rows), :], dst.at[pl.ds(rows * k, rows), :])


def _load_cols(dst, src_hbm, cols):
    for k in range(N_CHIPS):
        pltpu.sync_copy(src_hbm.at[k], dst.at[:, pl.ds(cols * k, cols)])


def _fill_phases(src, sh, nrows):
    for r in range(1, 8):
        sh[r, pl.ds(0, nrows), :] = src[pl.ds(r, nrows), pl.ds(D_A, D_B)]


def _phase_rows(src, sh, off, start, size):
    r = off % 8
    if r == 0:
        return src[pl.ds(off + start, size), pl.ds(D_A, D_B)]
    return sh[r, pl.ds(off - r + start, size), :]


def _my_place():
    x, y, c = lax.axis_index("x"), lax.axis_index("y"), lax.axis_index("c")
    return x, y, c, ((1 - x, y), (x, 1 - y), (1 - x, 1 - y))


def _remote(src, dst, send_sem, recv_sem, to):
    return pltpu.make_async_remote_copy(src_ref=src, dst_ref=dst, send_sem=send_sem, recv_sem=recv_sem,
                                        device_id=to, device_id_type=MESH)


def _gather_sems(np_):
    return [pltpu.SemaphoreType.DMA((np_, 3))] * 4 + [pltpu.SemaphoreType.DMA((np_,))] * 2


def _gather_start(srcs, dsts, sems):
    send, recv, _, _, osend, orecv = sems
    x, y, c, chips = _my_place()
    j = 2 * x + y
    for p in range(len(srcs)):
        _remote(srcs[p], dsts[p].at[j], osend.at[p], orecv.at[p], (x, y, 1 - c)).start()
        for nn, (kx, ky) in enumerate(chips):
            _remote(srcs[p].at[c], dsts[p].at[j, c], send.at[p, nn], recv.at[p, nn], (kx, ky, c)).start()


def _gather_forward(srcs, dsts, sems):
    send, recv, fsend, frecv, _, _ = sems
    x, y, c, chips = _my_place()
    for nn, (kx, ky) in enumerate(chips):
        for p in range(len(srcs)):
            blk = dsts[p].at[2 * kx + ky, c]
            _remote(blk, blk, send.at[p, nn], recv.at[p, nn], (kx, ky, c)).wait_recv()
            _remote(blk, blk, fsend.at[p, nn], frecv.at[p, nn], (x, y, 1 - c)).start()


def _gather_finish(srcs, dsts, sems):
    send, recv, fsend, frecv, osend, orecv = sems
    x, y, c, chips = _my_place()
    j = 2 * x + y
    for nn, (kx, ky) in enumerate(chips):
        for p in range(len(srcs)):
            other = dsts[p].at[2 * kx + ky, 1 - c]
            _remote(other, other, fsend.at[p, nn], frecv.at[p, nn], (x, y, 1 - c)).wait_recv()
    for nn, (kx, ky) in enumerate(chips):
        for p in range(len(srcs)):
            _remote(srcs[p].at[c], dsts[p].at[j, c], send.at[p, nn], recv.at[p, nn], (kx, ky, c)).wait_send()
            blk = dsts[p].at[2 * kx + ky, c]
            _remote(blk, blk, fsend.at[p, nn], frecv.at[p, nn], (x, y, 1 - c)).wait_send()
    for p in range(len(srcs)):
        _remote(srcs[p], dsts[p].at[j], osend.at[p], orecv.at[p], (x, y, 1 - c)).wait()


def _split_halves(a):
    return a.reshape(2, a.shape[0] // 2, a.shape[1])


def _gather_weights(packs):
    np_ = len(packs)
    split = [_split_halves(a) for a in packs]

    def body(*refs):
        srcs, dsts, sems = refs[:np_], refs[np_:2 * np_], refs[2 * np_:]
        _gather_start(srcs, dsts, sems)
        _gather_forward(srcs, dsts, sems)
        _gather_finish(srcs, dsts, sems)

    outs = pl.pallas_call(
        body,
        name="gather_weights",
        in_specs=[ANY] * np_,
        out_specs=[ANY] * np_,
        out_shape=[jax.ShapeDtypeStruct((N_CHIPS,) + a.shape, a.dtype) for a in split],
        scratch_shapes=_gather_sems(np_),
    )(*split)
    return [o.reshape((N_CHIPS,) + a.shape) for o, a in zip(outs, packs)]


def _peers(x, y, c):
    out = []
    for d in range(1, N_DEV):
        px = 1 - x if d & 4 else x
        py = 1 - y if d & 2 else y
        pc = 1 - c if d & 1 else c
        out.append(((px, py, pc), 4 * px + 2 * py + pc))
    return out


def _scatter_copies(srcs, dsts, send, recv):
    x, y, c, _ = _my_place()
    me = 4 * x + 2 * y + c
    out = []
    for a in range(len(srcs)):
        for d, ((px, py, pc), pidx) in enumerate(_peers(x, y, c)):
            piece = srcs[a].at[2 * px + py, pc]
            out.append((_remote(piece, dsts[a].at[me], send.at[a, d], recv.at[a, d], (px, py, pc)),
                        _remote(piece, dsts[a].at[pidx], send.at[a, d], recv.at[a, d], (px, py, pc))))
    return out


def _scatter_start(srcs, dsts, send, recv):
    for out_cp, _ in _scatter_copies(srcs, dsts, send, recv):
        out_cp.start()


def _scatter_wait(srcs, dsts, send, recv):
    for out_cp, in_cp in _scatter_copies(srcs, dsts, send, recv):
        in_cp.wait_recv()
        out_cp.wait_send()


def _scatter_sems(na):
    return [pltpu.SemaphoreType.DMA((na, N_DEV - 1))] * 2


def _slots(a):
    return jax.ShapeDtypeStruct((N_DEV,) + a.shape[2:], a.dtype)


def _by_halves(a):
    return a.reshape(a.shape[0], 2, a.shape[1] // 2, a.shape[2])


def _mix_fwd(x, gpre, gpost, wa, wb, bb, lng, lnb, g1g, pr0g, late):
    S, D = x.shape
    T = min(TILE_FWD, S)
    n = S // T
    nl = len(late)
    late_split = [_split_halves(a) for a in late]

    def body(*refs):
        (x_ref, gpre_ref, gpost_ref, wa_ref, wb_ref, bb_ref, lng_ref, lnb_ref, g1g_hbm, pr0g_hbm) = refs[:10]
        srcs = refs[10:10 + nl]
        x1_ref, u_ref, o1_ref, z1_ref = refs[10 + nl:14 + nl]
        dsts = refs[14 + nl:14 + 2 * nl]
        win_v, wout_v, ext, sh, z1buf = refs[14 + 2 * nl:19 + 2 * nl]
        sems = refs[19 + 2 * nl:]
        i = pl.program_id(0)

        @pl.when(i == 0)
        def _():
            _gather_start(srcs, dsts, sems)
            _load_cols(win_v, g1g_hbm, IN_BLK)
            _load_rows(wout_v, pr0g_hbm, 0, ROW_BLK)
            ext[pl.ds(0, HALO), :] = jnp.zeros((HALO, D_A + D_B), F32)

        @pl.when(i == max(n - 2, 0))
        def _():
            _gather_forward(srcs, dsts, sems)

        xv = x_ref[...]
        xn, _ = _rms(xv)
        h = (xn * gpre_ref[...]).astype(BF16)
        u = _mm(h, win_v[...])
        u_ref[...] = u.astype(BF16)
        b_a = u[:, 0:D_A]
        cv = u[:, D_A:2 * D_A] * u[:, 2 * D_A:3 * D_A]
        z0 = u[:, 3 * D_A:3 * D_A + D_B] * _sigmoid(u[:, 3 * D_A + D_B:])
        ext[pl.ds(HALO, T), pl.ds(0, D_A)] = cv
        ext[pl.ds(HALO, T), pl.ds(D_A, D_B)] = z0

        conv_a = ext[pl.ds(HALO - 2, T), pl.ds(0, D_A)] * wa_ref[0:1, :]
        for k in range(1, CONV_A_W):
            conv_a = conv_a + ext[pl.ds(HALO - 2 + k, T), pl.ds(0, D_A)] * wa_ref[k:k + 1, :]
        y_a = b_a * conv_a

        _fill_phases(ext, sh, T + HALO - 8)
        base = HALO - (CONV_B_W - 1)

        def chunk(ci, carry):
            start = pl.multiple_of(ci * CONV_ROWS_FWD, 8)
            acc = jnp.broadcast_to(bb_ref[...], (CONV_ROWS_FWD, D_B))
            for k in range(CONV_B_W):
                acc = acc + _phase_rows(ext, sh, base + k, start, CONV_ROWS_FWD) * wb_ref[k:k + 1, :]
            z1buf[pl.ds(start, CONV_ROWS_FWD), :] = acc
            return carry

        lax.fori_loop(0, T // CONV_ROWS_FWD, chunk, 0)
        z1 = z1buf[...]
        z1_ref[...] = z1.astype(BF16)
        mu = jnp.mean(z1, axis=-1, keepdims=True)
        zc = z1 - mu
        rstd = lax.rsqrt(jnp.mean(zc * zc, axis=-1, keepdims=True) + LN_EPS)
        l = zc * rstd * lng_ref[...] + lnb_ref[...]
        y_b = l * _sigmoid(l)
        y = jnp.concatenate([y_a, y_b], axis=-1).astype(BF16)
        o1 = _mm(y, wout_v[...])
        o1_ref[...] = o1.astype(BF16)
        o1n, _ = _rms(o1)
        x1_ref[...] = xv + o1n * gpost_ref[...]
        ext[pl.ds(0, HALO), :] = ext[pl.ds(T, HALO), :]

        @pl.when(i == n - 1)
        def _():
            _gather_finish(srcs, dsts, sems)

    tok = lambda w: pl.BlockSpec((T, w), lambda i: (i, 0))
    outs = pl.pallas_call(
        body,
        name="mix_fwd",
        grid=(n,),
        in_specs=[tok(D), _full((1, D)), _full((1, D)), _full((CONV_A_W, D_A)), _full((CONV_B_W, D_B)),
                  _full((1, D_B)), _full((1, D_B)), _full((1, D_B)), ANY, ANY] + [ANY] * nl,
        out_specs=[tok(D), tok(D_IN_ALL), tok(D), tok(D_B)] + [ANY] * nl,
        out_shape=[jax.ShapeDtypeStruct((S, D), F32), jax.ShapeDtypeStruct((S, D_IN_ALL), BF16),
                   jax.ShapeDtypeStruct((S, D), BF16), jax.ShapeDtypeStruct((S, D_B), BF16)]
        + [jax.ShapeDtypeStruct((N_CHIPS,) + a.shape, a.dtype) for a in late_split],
        scratch_shapes=[pltpu.VMEM((D, D_IN_ALL), BF16), pltpu.VMEM((D_A + D_B, D), BF16),
                        pltpu.VMEM((HALO + T, D_A + D_B), F32), pltpu.VMEM((8, HALO + T, D_B), F32),
                        pltpu.VMEM((T, D_B), F32)] + _gather_sems(nl),
        compiler_params=_params(("arbitrary",)),
    )(x, gpre, gpost, wa, wb, bb, lng, lnb, g1g, pr0g, *late_split)
    return list(outs[:4]) + [o.reshape((N_CHIPS,) + a.shape) for o, a in zip(outs[4:], late)]


def _mix_bwd(dx1, x, o1, u, z1s, gpre, gpost, wa, wb, lng, lnb, g1g, pr0g, cx):
    S, D = x.shape
    T = min(TILE_BWD, S)
    n = S // T
    hb = T // HALO
    cxs = _by_halves(cx)

    def body(dx1_ref, x_ref, o1_ref, u_ref, uh_ref, z1_ref, gpre_ref, gpost_ref, wa_ref, wb_ref, lng_ref, lnb_ref,
             g1g_hbm, pr0g_hbm, cx_hbm,
             dx_ref, dgpre_ref, dgpost_ref, dwa_ref, dwb_ref, dbb_ref, dlng_ref, dlnb_ref, cm1_hbm, cm2_hbm, yx_hbm,
             win_v, wout_v, dwin_acc, dwout_acc, ext, ext2, shb, dz0buf, dwb_acc, send, recv):
        i = pl.program_id(0)

        @pl.when(i == 0)
        def _():
            _scatter_start([cx_hbm], [yx_hbm], send, recv)
            _load_cols(win_v, g1g_hbm, IN_BLK)
            _load_rows(wout_v, pr0g_hbm, 0, ROW_BLK)
            dwin_acc[...] = jnp.zeros_like(dwin_acc)
            dwout_acc[...] = jnp.zeros_like(dwout_acc)
            dwb_acc[...] = jnp.zeros_like(dwb_acc)
            ext2[pl.ds(T, HALO), :] = jnp.zeros((HALO, D_A + D_B), F32)
            for ref in (dgpre_ref, dgpost_ref, dwa_ref, dbb_ref, dlng_ref, dlnb_ref):
                ref[...] = jnp.zeros_like(ref)

        o1n, r1 = _rms(o1_ref[...].astype(F32))
        dx1v = dx1_ref[...]
        d_o1, dgp = _rms_bwd(dx1v, o1n, r1, gpost_ref[...])
        dgpost_ref[...] += dgp
        d_o1b = d_o1.astype(BF16)
        dy = _mm_nt(d_o1b, wout_v[...])

        first = (i == n - 1).astype(F32)
        uh = uh_ref[...].astype(F32) * (1.0 - first)
        ext[pl.ds(0, HALO), pl.ds(0, D_A)] = uh[:, D_A:2 * D_A] * uh[:, 2 * D_A:3 * D_A]
        uf = u_ref[...].astype(F32)
        b_a = uf[:, 0:D_A]
        c_a = uf[:, D_A:2 * D_A]
        v_a = uf[:, 2 * D_A:3 * D_A]
        gv = uf[:, 3 * D_A:3 * D_A + D_B]
        sg = _sigmoid(uf[:, 3 * D_A + D_B:])
        ext[pl.ds(HALO, T), pl.ds(0, D_A)] = c_a * v_a
        ext[pl.ds(HALO, T), pl.ds(D_A, D_B)] = gv * sg
        conv_a = ext[pl.ds(HALO - 2, T), pl.ds(0, D_A)] * wa_ref[0:1, :]
        for k in range(1, CONV_A_W):
            conv_a = conv_a + ext[pl.ds(HALO - 2 + k, T), pl.ds(0, D_A)] * wa_ref[k:k + 1, :]
        z1 = z1_ref[...].astype(F32)
        mu = jnp.mean(z1, axis=-1, keepdims=True)
        zc = z1 - mu
        rstd = lax.rsqrt(jnp.mean(zc * zc, axis=-1, keepdims=True) + LN_EPS)
        zn = zc * rstd
        l = zn * lng_ref[...] + lnb_ref[...]
        sl = _sigmoid(l)
        y = jnp.concatenate([b_a * conv_a, l * sl], axis=-1).astype(BF16)
        dwout_acc[...] += _mm_tn(y, d_o1b)

        dy_a = dy[:, 0:D_A]
        dl = dy[:, D_A:] * (sl * (1.0 + l * (1.0 - sl)))
        dlng_ref[...] += jnp.sum(dl * zn, axis=0, keepdims=True)
        dlnb_ref[...] += jnp.sum(dl, axis=0, keepdims=True)
        dzn = dl * lng_ref[...]
        dz1 = rstd * (dzn - jnp.mean(dzn, axis=-1, keepdims=True) - zn * jnp.mean(dzn * zn, axis=-1, keepdims=True))
        dbb_ref[...] += jnp.sum(dz1, axis=0, keepdims=True)
        d_conv = dy_a * b_a
        ext2[pl.ds(0, T), pl.ds(0, D_A)] = d_conv
        ext2[pl.ds(0, T), pl.ds(D_A, D_B)] = dz1

        d_cv = ext2[pl.ds(CONV_A_W - 1, T), pl.ds(0, D_A)] * wa_ref[0:1, :]
        for k in range(1, CONV_A_W):
            d_cv = d_cv + ext2[pl.ds(CONV_A_W - 1 - k, T), pl.ds(0, D_A)] * wa_ref[k:k + 1, :]
        for k in range(CONV_A_W):
            dwa_ref[k:k + 1, :] += jnp.sum(d_conv * ext[pl.ds(HALO - 2 + k, T), pl.ds(0, D_A)], axis=0, keepdims=True)

        _fill_phases(ext2, shb, T + HALO - 8)

        def chunk(ci, carry):
            start = pl.multiple_of(ci * CONV_ROWS, 8)
            z0c = ext[pl.ds(HALO + start, CONV_ROWS), pl.ds(D_A, D_B)]
            acc = jnp.zeros((CONV_ROWS, D_B), F32)
            for k in range(CONV_B_W):
                ahead = _phase_rows(ext2, shb, CONV_B_W - 1 - k, start, CONV_ROWS)
                acc = acc + ahead * wb_ref[k:k + 1, :]
                dwb_acc[k] += _fold8(z0c * ahead)
            dz0buf[pl.ds(start, CONV_ROWS), :] = acc
            return carry

        lax.fori_loop(0, T // CONV_ROWS, chunk, 0)
        dz0 = dz0buf[...]
        du = jnp.concatenate([dy_a * conv_a, d_cv * v_a, d_cv * c_a, dz0 * sg, dz0 * gv * sg * (1.0 - sg)],
                             axis=-1).astype(BF16)
        dh = _mm_nt(du, win_v[...])
        xv = x_ref[...]
        xn, r0 = _rms(xv)
        dwin_acc[...] += _mm_tn((xn * gpre_ref[...]).astype(BF16), du)
        dxp, dg0 = _rms_bwd(dh, xn, r0, gpre_ref[...])
        dgpre_ref[...] += dg0
        dx_ref[...] = dx1v + dxp
        ext2[pl.ds(T, HALO), :] = ext2[pl.ds(0, HALO), :]

        @pl.when(i == n - 1)
        def _():
            for k in range(CONV_B_W):
                dwb_ref[k:k + 1, :] = jnp.sum(dwb_acc[k], axis=0, keepdims=True)
            win_v[...] = dwin_acc[...].astype(BF16)
            wout_v[...] = dwout_acc[...].astype(BF16)
            for k in range(N_CHIPS):
                pltpu.sync_copy(win_v.at[:, pl.ds(IN_BLK * k, IN_BLK)], cm1_hbm.at[k])
                pltpu.sync_copy(wout_v.at[pl.ds(ROW_BLK * k, ROW_BLK), :], cm2_hbm.at[k])
            _scatter_wait([cx_hbm], [yx_hbm], send, recv)

    rev = lambda w: pl.BlockSpec((T, w), lambda i: (n - 1 - i, 0))
    halo = pl.BlockSpec((HALO, D_IN_ALL), lambda i: (jnp.maximum((n - 1 - i) * hb - 1, 0), 0))
    return pl.pallas_call(
        body,
        name="mix_bwd",
        grid=(n,),
        in_specs=[rev(D), rev(D), rev(D), rev(D_IN_ALL), halo, rev(D_B), _full((1, D)), _full((1, D)),
                  _full((CONV_A_W, D_A)), _full((CONV_B_W, D_B)), _full((1, D_B)), _full((1, D_B)), ANY, ANY, ANY],
        out_specs=[rev(D), _full((1, D)), _full((1, D)), _full((CONV_A_W, D_A)), _full((CONV_B_W, D_B)),
                   _full((1, D_B)), _full((1, D_B)), _full((1, D_B)), ANY, ANY, ANY],
        out_shape=[jax.ShapeDtypeStruct((S, D), F32), jax.ShapeDtypeStruct((1, D), F32),
                   jax.ShapeDtypeStruct((1, D), F32), jax.ShapeDtypeStruct((CONV_A_W, D_A), F32),
                   jax.ShapeDtypeStruct((CONV_B_W, D_B), F32), jax.ShapeDtypeStruct((1, D_B), F32),
                   jax.ShapeDtypeStruct((1, D_B), F32), jax.ShapeDtypeStruct((1, D_B), F32),
                   jax.ShapeDtypeStruct((N_CHIPS, D, IN_BLK), BF16),
                   jax.ShapeDtypeStruct((N_CHIPS, ROW_BLK, D), BF16), _slots(cxs)],
        scratch_shapes=[pltpu.VMEM((D, D_IN_ALL), BF16), pltpu.VMEM((D_A + D_B, D), BF16),
                        pltpu.VMEM((D, D_IN_ALL), F32), pltpu.VMEM((D_A + D_B, D), F32),
                        pltpu.VMEM((HALO + T, D_A + D_B), F32), pltpu.VMEM((HALO + T, D_A + D_B), F32),
                        pltpu.VMEM((8, HALO + T, D_B), F32),
                        pltpu.VMEM((T, D_B), F32), pltpu.VMEM((CONV_B_W, 8, D_B), F32)] + _scatter_sems(1),
        compiler_params=_params(("arbitrary",)),
    )(dx1, x, o1, u, u, z1s, gpre, gpost, wa, wb, lng, lnb, g1g, pr0g, cxs)


def _mem_kv(mem, gmem, pr1g):
    M, D = mem.shape

    def body(mem_ref, g_ref, pr1g_hbm, memn_ref, k_ref, v_ref, wk_v, wv_v):
        _load_rows(wk_v, pr1g_hbm, P1_K, ROW_BLK)
        _load_rows(wv_v, pr1g_hbm, P1_V, ROW_BLK)
        mn, _ = _rms(mem_ref[...])
        mb = (mn * g_ref[...]).astype(BF16)
        memn_ref[...] = mb
        k_ref[...] = _mm(mb, wk_v[...]).astype(BF16)
        v_ref[...] = _mm(mb, wv_v[...]).astype(BF16)

    return pl.pallas_call(
        body,
        name="mem_kv",
        grid=(1,),
        in_specs=[_full((M, D)), _full((1, D)), ANY],
        out_specs=[_full((M, D))] * 3,
        out_shape=[jax.ShapeDtypeStruct((M, D), BF16)] * 3,
        scratch_shapes=[pltpu.VMEM((D, D), BF16), pltpu.VMEM((D, D), BF16)],
        compiler_params=_params(("arbitrary",)),
    )(mem, gmem, pr1g)


def _attend(qb, kb, vb):
    scale = HEAD_DIM ** -0.5
    ps, os_ = [], []
    for hd in range(XA_HEADS):
        cols = slice(HEAD_DIM * hd, HEAD_DIM * (hd + 1))
        s = _mm_nt(qb[:, cols], kb[:, cols]) * scale
        e = jnp.exp(s - jnp.max(s, axis=-1, keepdims=True))
        p = e * (1.0 / jnp.sum(e, axis=-1, keepdims=True))
        ps.append(p)
        os_.append(_mm(p.astype(BF16), vb[:, cols]))
    return ps, jnp.concatenate(os_, axis=-1).astype(BF16)


def _xattn_fwd(x1, gpre, gpost, kb, vb, pr1g):
    S, D = x1.shape
    M = kb.shape[0]
    T = min(TILE_XATTN_FWD, S)
    n = S // T

    def body(x1_ref, gpre_ref, gpost_ref, k_ref, v_ref, pr1g_hbm, x2_ref, q_ref, o2_ref, wq_v, wo_v):
        @pl.when(pl.program_id(0) == 0)
        def _():
            _load_rows(wq_v, pr1g_hbm, P1_Q, ROW_BLK)
            _load_rows(wo_v, pr1g_hbm, P1_O, ROW_BLK)

        xv = x1_ref[...]
        xn, _ = _rms(xv)
        qb = _mm((xn * gpre_ref[...]).astype(BF16), wq_v[...]).astype(BF16)
        q_ref[...] = qb
        _, ob = _attend(qb, k_ref[...], v_ref[...])
        o2 = _mm(ob, wo_v[...])
        o2_ref[...] = o2.astype(BF16)
        o2n, _ = _rms(o2)
        x2_ref[...] = xv + o2n * gpost_ref[...]

    tok = lambda w: pl.BlockSpec((T, w), lambda i: (i, 0))
    return pl.pallas_call(
        body,
        name="xattn_fwd",
        grid=(n,),
        in_specs=[tok(D), _full((1, D)), _full((1, D)), _full((M, D)), _full((M, D)), ANY],
        out_specs=[tok(D), tok(D), tok(D)],
        out_shape=[jax.ShapeDtypeStruct((S, D), F32), jax.ShapeDtypeStruct((S, D), BF16),
                   jax.ShapeDtypeStruct((S, D), BF16)],
        scratch_shapes=[pltpu.VMEM((D, D), BF16), pltpu.VMEM((D, D), BF16)],
        compiler_params=_params(("arbitrary",)),
    )(x1, gpre, gpost, kb, vb, pr1g)


def _xattn_bwd(dx3, dh3p, x2, x1, o2, q, kb, vb, gffn, gpost, gpre, pr1g, cf1, cf2):
    S, D = x1.shape
    M = kb.shape[0]
    T = min(TILE_BWD, S)
    n = S // T
    scale = HEAD_DIM ** -0.5
    nparts = dh3p.shape[0]
    cfs = [_by_halves(cf1), _by_halves(cf2)]

    def body(*refs):
        dx3_ref, dh3_refs = refs[0], refs[1:1 + nparts]
        (x2_ref, x1_ref, o2_ref, q_ref, k_ref, v_ref, gffn_ref, gpost_ref, gpre_ref, pr1g_hbm, cf1_hbm, cf2_hbm,
         dx1_ref, dgffn_ref, dgpost_ref, dgpre_ref, dk_ref, dv_ref, cx_hbm, yf1_hbm, yf2_hbm,
         wq_v, wo_v, dwq_acc, dwo_acc, send, recv) = refs[1 + nparts:]
        i = pl.program_id(0)

        @pl.when(i == 0)
        def _():
            _scatter_start([cf1_hbm, cf2_hbm], [yf1_hbm, yf2_hbm], send, recv)
            _load_rows(wq_v, pr1g_hbm, P1_Q, ROW_BLK)
            _load_rows(wo_v, pr1g_hbm, P1_O, ROW_BLK)
            dwq_acc[...] = jnp.zeros_like(dwq_acc)
            dwo_acc[...] = jnp.zeros_like(dwo_acc)
            for ref in (dgffn_ref, dgpost_ref, dgpre_ref, dk_ref, dv_ref):
                ref[...] = jnp.zeros_like(ref)

        x2n, r2 = _rms(x2_ref[...])
        dh3 = dh3_refs[0][...].astype(F32)
        for ref in dh3_refs[1:]:
            dh3 = dh3 + ref[...].astype(F32)
        dxp, dg = _rms_bwd(dh3, x2n, r2, gffn_ref[...])
        dgffn_ref[...] += dg
        dx2 = dx3_ref[...] + dxp
        o2n, ro = _rms(o2_ref[...].astype(F32))
        d_o2, dg = _rms_bwd(dx2, o2n, ro, gpost_ref[...])
        dgpost_ref[...] += dg
        d_o2b = d_o2.astype(BF16)
        d_o = _mm_nt(d_o2b, wo_v[...]).astype(BF16)
        qb = q_ref[...]
        kv = k_ref[...]
        vv = v_ref[...]
        ps, ob = _attend(qb, kv, vv)
        dwo_acc[...] += _mm_tn(ob, d_o2b)
        dqs = []
        for hd in range(XA_HEADS):
            cols = slice(HEAD_DIM * hd, HEAD_DIM * (hd + 1))
            p = ps[hd]
            dp = _mm_nt(d_o[:, cols], vv[:, cols])
            dv_ref[:, cols] += _mm_tn(p.astype(BF16), d_o[:, cols])
            ds = (p * (dp - jnp.sum(p * dp, axis=-1, keepdims=True)) * scale).astype(BF16)
            dqs.append(_mm(ds, kv[:, cols]))
            dk_ref[:, cols] += _mm_tn(ds, qb[:, cols])
        dq = jnp.concatenate(dqs, axis=-1).astype(BF16)
        dh2 = _mm_nt(dq, wq_v[...])
        x1n, r1 = _rms(x1_ref[...])
        dwq_acc[...] += _mm_tn((x1n * gpre_ref[...]).astype(BF16), dq)
        dxp, dg = _rms_bwd(dh2, x1n, r1, gpre_ref[...])
        dgpre_ref[...] += dg
        dx1_ref[...] = dx2 + dxp

        @pl.when(i == n - 1)
        def _():
            wq_v[...] = dwq_acc[...].astype(BF16)
            wo_v[...] = dwo_acc[...].astype(BF16)
            for k in range(N_CHIPS):
                rows = pl.ds(ROW_BLK * k, ROW_BLK)
                pltpu.sync_copy(wq_v.at[rows, :], cx_hbm.at[k, pl.ds(GX_Q, ROW_BLK), :])
                pltpu.sync_copy(wo_v.at[rows, :], cx_hbm.at[k, pl.ds(GX_O, ROW_BLK), :])
            _scatter_wait([cf1_hbm, cf2_hbm], [yf1_hbm, yf2_hbm], send, recv)

    tok = lambda w: pl.BlockSpec((T, w), lambda i: (i, 0))
    part = lambda j: pl.BlockSpec((None, T, D), lambda i: (j, i, 0))
    return pl.pallas_call(
        body,
        name="xattn_bwd",
        grid=(n,),
        in_specs=[tok(D)] + [part(j) for j in range(nparts)] + [tok(D), tok(D), tok(D), tok(D), _full((M, D)),
                                                                 _full((M, D)), _full((1, D)), _full((1, D)),
                                                                 _full((1, D)), ANY, ANY, ANY],
        out_specs=[tok(D), _full((1, D)), _full((1, D)), _full((1, D)), _full((M, D)), _full((M, D)), ANY, ANY, ANY],
        out_shape=[jax.ShapeDtypeStruct((S, D), F32), jax.ShapeDtypeStruct((1, D), F32),
                   jax.ShapeDtypeStruct((1, D), F32), jax.ShapeDtypeStruct((1, D), F32),
                   jax.ShapeDtypeStruct((M, D), F32), jax.ShapeDtypeStruct((M, D), F32),
                   jax.ShapeDtypeStruct((N_CHIPS, D, D), BF16), _slots(cfs[0]), _slots(cfs[1])],
        scratch_shapes=[pltpu.VMEM((D, D), BF16), pltpu.VMEM((D, D), BF16),
                        pltpu.VMEM((D, D), F32), pltpu.VMEM((D, D), F32)] + _scatter_sems(2),
        compiler_params=_params(("arbitrary",)),
    )(dx3, *([dh3p] * nparts), x2, x1, o2, q, kb, vb, gffn, gpost, gpre, pr1g, *cfs)


def _mem_bwd(dk, dv, mem, memn, gmem, pr1g, cx_in):
    M, D = mem.shape

    def body(dk_ref, dv_ref, mem_ref, memn_ref, g_ref, pr1g_hbm, cx_hbm, dg_ref, cx_out, wk_v, wv_v):
        del cx_hbm
        _load_rows(wk_v, pr1g_hbm, P1_K, ROW_BLK)
        _load_rows(wv_v, pr1g_hbm, P1_V, ROW_BLK)
        dkb = dk_ref[...].astype(BF16)
        dvb = dv_ref[...].astype(BF16)
        mb = memn_ref[...]
        dmn = _mm_nt(dkb, wk_v[...]) + _mm_nt(dvb, wv_v[...])
        mn, _ = _rms(mem_ref[...])
        dg_ref[...] = jnp.sum(dmn * mn, axis=0, keepdims=True)
        wk_v[...] = _mm_tn(mb, dkb).astype(BF16)
        wv_v[...] = _mm_tn(mb, dvb).astype(BF16)
        for k in range(N_CHIPS):
            rows = pl.ds(ROW_BLK * k, ROW_BLK)
            pltpu.sync_copy(wk_v.at[rows, :], cx_out.at[k, pl.ds(GX_K, ROW_BLK), :])
            pltpu.sync_copy(wv_v.at[rows, :], cx_out.at[k, pl.ds(GX_V, ROW_BLK), :])

    return pl.pallas_call(
        body,
        name="mem_bwd",
        grid=(1,),
        in_specs=[_full((M, D)), _full((M, D)), _full((M, D)), _full((M, D)), _full((1, D)), ANY, ANY],
        out_specs=[_full((1, D)), ANY],
        out_shape=[jax.ShapeDtypeStruct((1, D), F32), jax.ShapeDtypeStruct(cx_in.shape, BF16)],
        input_output_aliases={6: 1},
        scratch_shapes=[pltpu.VMEM((D, D), BF16), pltpu.VMEM((D, D), BF16)],
        compiler_params=_params(("arbitrary",)),
    )(dk, dv, mem, memn, gmem, pr1g, cx_in)


def _ffn_fwd(x2, target, gpre, gpost, g2g, pr1g):
    S, D = x2.shape
    T = min(TILE_FFN, S)
    n = S // T

    def body(x2_ref, t_ref, gpre_ref, gpost_ref, g2g_hbm, pr1g_hbm,
             h3_ref, g_hbm, u_hbm, do3_ref, dx3_ref, loss_ref, dgpost_ref, wg_v, wu_v, wd_v, gst, ust, sem):
        i = pl.program_id(0)

        @pl.when(i == 0)
        def _():
            pltpu.sync_copy(g2g_hbm.at[:, pl.ds(0, D), :], wg_v)
            pltpu.sync_copy(g2g_hbm.at[:, pl.ds(D, D), :], wu_v)
            pltpu.sync_copy(pr1g_hbm.at[:, pl.ds(P1_DOWN, FF_BLK), :], wd_v)
            loss_ref[...] = jnp.zeros_like(loss_ref)
            dgpost_ref[...] = jnp.zeros_like(dgpost_ref)

        xv = x2_ref[...]
        xn, _ = _rms(xv)
        hb = (xn * gpre_ref[...]).astype(BF16)
        h3_ref[...] = hb
        o3 = jnp.zeros((T, D), F32)
        out = [None, None]
        for c in range(N_CHIPS):
            slot = c % 2
            if out[slot] is not None:
                for cp in out[slot]:
                    cp.wait()
            g = _mm(hb, wg_v[c])
            u = _mm(hb, wu_v[c])
            gst[slot] = g.astype(BF16)
            ust[slot] = u.astype(BF16)
            out[slot] = (pltpu.make_async_copy(gst.at[slot], g_hbm.at[c, i], sem.at[0, slot]),
                         pltpu.make_async_copy(ust.at[slot], u_hbm.at[c, i], sem.at[1, slot]))
            for cp in out[slot]:
                cp.start()
            o3 = o3 + _mm((g * _sigmoid(g) * u).astype(BF16), wd_v[c])
        for pair in out:
            for cp in pair:
                cp.wait()
        o3n, r3 = _rms(o3)
        diff = xv + o3n * gpost_ref[...] - t_ref[...]
        sq = jnp.sum(jnp.sum(diff * diff, axis=-1, keepdims=True), axis=0, keepdims=True)
        loss_ref[...] += sq * (0.5 / D)
        dx3 = diff * (1.0 / D)
        dx3_ref[...] = dx3
        d_o3, dg = _rms_bwd(dx3, o3n, r3, gpost_ref[...])
        dgpost_ref[...] += dg
        do3_ref[...] = d_o3.astype(BF16)

    tok = lambda w: pl.BlockSpec((T, w), lambda i: (i, 0))
    h3, gs, us, do3, dx3, loss, dgpost = pl.pallas_call(
        body,
        name="ffn_fwd",
        grid=(n,),
        in_specs=[tok(D), tok(D), _full((1, D)), _full((1, D)), ANY, ANY],
        out_specs=[tok(D), ANY, ANY, tok(D), tok(D), _full((1, 128)), _full((1, D))],
        out_shape=[jax.ShapeDtypeStruct((S, D), BF16), jax.ShapeDtypeStruct((N_CHIPS, n, T, FF_BLK), BF16),
                   jax.ShapeDtypeStruct((N_CHIPS, n, T, FF_BLK), BF16), jax.ShapeDtypeStruct((S, D), BF16),
                   jax.ShapeDtypeStruct((S, D), F32), jax.ShapeDtypeStruct((1, 128), F32),
                   jax.ShapeDtypeStruct((1, D), F32)],
        scratch_shapes=[pltpu.VMEM((N_CHIPS, D, FF_BLK), BF16), pltpu.VMEM((N_CHIPS, D, FF_BLK), BF16),
                        pltpu.VMEM((N_CHIPS, FF_BLK, D), BF16), pltpu.VMEM((2, T, FF_BLK), BF16),
                        pltpu.VMEM((2, T, FF_BLK), BF16), pltpu.SemaphoreType.DMA((2, 2))],
        compiler_params=_params(("arbitrary",)),
    )(x2, target, gpre, gpost, g2g, pr1g)
    return h3, gs.reshape(N_CHIPS, S, FF_BLK), us.reshape(N_CHIPS, S, FF_BLK), do3, dx3, loss, dgpost


def _ffn_bwd(h3, do3, gs, us, g2g, pr1g):
    S, D = h3.shape
    T = min(TILE_FFN, S)
    n = S // T
    NP = FFN_BWD_BLOCKS

    def body(h3_ref, do3_ref, g_ref, u_ref, g2g_hbm, pr1g_hbm, dh3_ref, cf1_hbm, cf2_hbm,
             wg_v, wu_v, wd_v, dwg_acc, dwu_acc, dwd_acc):
        jp = pl.program_id(0)
        i = pl.program_id(1)
        blocks = pl.ds(NP * jp, NP)

        @pl.when(i == 0)
        def _():
            pltpu.sync_copy(g2g_hbm.at[blocks, pl.ds(0, D), :], wg_v)
            pltpu.sync_copy(g2g_hbm.at[blocks, pl.ds(D, D), :], wu_v)
            pltpu.sync_copy(pr1g_hbm.at[blocks, pl.ds(P1_DOWN, FF_BLK), :], wd_v)
            dwg_acc[...] = jnp.zeros_like(dwg_acc)
            dwu_acc[...] = jnp.zeros_like(dwu_acc)
            dwd_acc[...] = jnp.zeros_like(dwd_acc)

        hb = h3_ref[...]
        d_o3 = do3_ref[...]
        dh = jnp.zeros((T, D), F32)
        for c in range(NP):
            da = _mm_nt(d_o3, wd_v[c])
            g = g_ref[c].astype(F32)
            u = u_ref[c].astype(F32)
            sg = _sigmoid(g)
            sl = g * sg
            dwd_acc[c] += _mm_tn((sl * u).astype(BF16), d_o3)
            dub = (da * sl).astype(BF16)
            dgb = (da * u * (sg * (1.0 + g * (1.0 - sg)))).astype(BF16)
            dwg_acc[c] += _mm_tn(dgb, hb)
            dwu_acc[c] += _mm_tn(dub, hb)
            dh = dh + _mm_nt(dgb, wg_v[c]) + _mm_nt(dub, wu_v[c])
        dh3_ref[...] = dh.astype(BF16)

        @pl.when(i == n - 1)
        def _():
            wd_v[...] = dwg_acc[...].astype(BF16)
            pltpu.sync_copy(wd_v, cf1_hbm.at[blocks, pl.ds(0, FF_BLK), :])
            wd_v[...] = dwu_acc[...].astype(BF16)
            pltpu.sync_copy(wd_v, cf1_hbm.at[blocks, pl.ds(FF_BLK, FF_BLK), :])
            wd_v[...] = dwd_acc[...].astype(BF16)
            pltpu.sync_copy(wd_v, cf2_hbm.at[blocks])

    tok = lambda w: pl.BlockSpec((T, w), lambda jp, i: (i, 0))
    blk = pl.BlockSpec((NP, T, FF_BLK), lambda jp, i: (jp, i, 0))
    return pl.pallas_call(
        body,
        name="ffn_bwd",
        grid=(N_CHIPS // NP, n),
        in_specs=[tok(D), tok(D), blk, blk, ANY, ANY],
        out_specs=[pl.BlockSpec((None, T, D), lambda jp, i: (jp, i, 0)), ANY, ANY],
        out_shape=[jax.ShapeDtypeStruct((N_CHIPS // NP, S, D), BF16),
                   jax.ShapeDtypeStruct((N_CHIPS, 2 * FF_BLK, D), BF16),
                   jax.ShapeDtypeStruct((N_CHIPS, FF_BLK, D), BF16)],
        scratch_shapes=[pltpu.VMEM((NP, D, FF_BLK), BF16), pltpu.VMEM((NP, D, FF_BLK), BF16),
                        pltpu.VMEM((NP, FF_BLK, D), BF16), pltpu.VMEM((NP, FF_BLK, D), F32),
                        pltpu.VMEM((NP, FF_BLK, D), F32), pltpu.VMEM((NP, FF_BLK, D), F32)],
        compiler_params=_params(("arbitrary", "arbitrary")),
    )(h3, do3, gs, us, g2g, pr1g)


IN_HBM = pl.BlockSpec(memory_space=pltpu.HBM)
IN_SEMAPHORES = pl.BlockSpec(memory_space=pltpu.SEMAPHORE)


def _last_copies(srcs, lands, small_ref, small_all, sems):
    send, recv = sems
    x, y, c, _ = _my_place()
    me = 4 * x + 2 * y + c
    pairs = []
    for d, ((px, py, pc), pidx) in enumerate(_peers(x, y, c)):
        for a in range(len(srcs) + 1):
            k = d * (len(srcs) + 1) + a
            src = srcs[a].at[2 * px + py, pc] if a < len(srcs) else small_ref
            land = lands[a] if a < len(srcs) else small_all
            pairs.append((_remote(src, land.at[me], send.at[k], recv.at[k], (px, py, pc)),
                          _remote(src, land.at[pidx], send.at[k], recv.at[k], (px, py, pc))))
    return pairs


def _exchange_start(contribs, small):
    na = len(contribs)
    cs = [_by_halves(a) for a in contribs]
    flying = cs + [small] + [lax.empty(_slots(a).shape, a.dtype) for a in cs] + [
        lax.empty((N_DEV,) + small.shape, F32)]
    nf = len(flying)

    def body(*refs):
        srcs, small_ref, lands, small_all = refs[:na], refs[na], refs[na + 1:2 * na + 1], refs[2 * na + 1]
        sems, token = refs[nf:nf + 2], refs[-1]
        for mine, _ in _last_copies(srcs, lands, small_ref, small_all, sems):
            mine.start()
        token[...] = jnp.zeros_like(token)

    ncopies = (N_DEV - 1) * (na + 1)
    outs = pl.pallas_call(
        body,
        name="exchange_start",
        in_specs=[IN_HBM] * nf,
        out_specs=[IN_SEMAPHORES] * 2 + [IN_HBM] * nf + [pl.BlockSpec(memory_space=pltpu.VMEM)],
        out_shape=[pltpu.SemaphoreType.DMA((ncopies,)), pltpu.SemaphoreType.DMA((ncopies,))]
        + [pltpu.HBM(a.shape, a.dtype) for a in flying] + [jax.ShapeDtypeStruct((8, 128), F32)],
        input_output_aliases={k: 2 + k for k in range(nf)},
        compiler_params=pltpu.CompilerParams(has_side_effects=pltpu.SideEffectType.DATAFLOW_SIDE_EFFECTING),
    )(*[pltpu.with_memory_space_constraint(a, pltpu.HBM) for a in flying])
    return outs[:2], outs[2:2 + nf], outs[-1]


def _exchange_wait(sems, flying, na, after):
    nf = len(flying)

    def body(*refs):
        srcs, small_ref, lands, small_all = refs[:na], refs[na], refs[na + 1:2 * na + 1], refs[2 * na + 1]
        for mine, theirs in _last_copies(srcs, lands, small_ref, small_all, refs[nf:nf + 2]):
            theirs.wait_recv()
            mine.wait_send()

    outs = pl.pallas_call(
        body,
        name="exchange_wait",
        in_specs=[IN_HBM] * nf + [IN_SEMAPHORES] * 2 + [ANY] * len(after),
        out_specs=[IN_HBM] * nf,
        out_shape=[pltpu.HBM(a.shape, a.dtype) for a in flying],
        input_output_aliases={k: k for k in range(nf)},
        compiler_params=pltpu.CompilerParams(has_side_effects=pltpu.SideEffectType.DATAFLOW_SIDE_EFFECTING),
    )(*flying, *sems, *after)
    return outs[na + 1:2 * na + 1], outs[2 * na + 1]


def _sum_peers(parts, own, place, steps, tag, after=()):
    _, rows, w = parts.shape
    tr = rows // steps
    if own.ndim == 3:
        own = _by_halves(own)

    def body(place_ref, *refs):
        p_refs, own_ref, o_ref = refs[:N_DEV], refs[N_DEV], refs[-1]
        me = place_ref[2]
        acc = None
        for s in range(N_DEV):
            term = jnp.where(me == s, own_ref[...], p_refs[s][...]).astype(F32)
            acc = term if acc is None else acc + term
        o_ref[...] = acc

    def other(s):
        return lambda i, pr: (jnp.where(pr[2] == s, (s + 1) % N_DEV, s), i, 0)

    own_spec = (pl.BlockSpec((None, None, tr, w), lambda i, pr: (pr[0], pr[1], i, 0)) if own.ndim == 4 else
                pl.BlockSpec((tr, w), lambda i, pr: (i, 0)))
    out_spec = (pl.BlockSpec((None, tr, w), lambda i, pr: (pr[1], i, 0)) if own.ndim == 4 else
                pl.BlockSpec((tr, w), lambda i, pr: (i, 0)))
    return pl.pallas_call(
        body,
        name="sum_peers_" + tag,
        grid_spec=pltpu.PrefetchScalarGridSpec(
            num_scalar_prefetch=1,
            grid=(steps,),
            in_specs=[pl.BlockSpec((None, tr, w), other(s)) for s in range(N_DEV)] + [own_spec] + [ANY] * len(after),
            out_specs=out_spec,
        ),
        out_shape=jax.ShapeDtypeStruct((2, rows, w) if own.ndim == 4 else (rows, w), F32),
        compiler_params=_params(("arbitrary",)),
    )(place, *([parts] * N_DEV), own, *after)


def _pair_gather(bufs, tag):
    np_ = len(bufs)

    def body(*refs):
        srcs, dsts = refs[:np_], refs[np_:2 * np_]
        send, recv = refs[2 * np_:]
        x, y, c, _ = _my_place()
        cps = []
        for p in range(np_):
            cp = _remote(srcs[p].at[c], dsts[p].at[c], send.at[p], recv.at[p], (x, y, 1 - c))
            cp.start()
            cps.append(cp)
        for p, cp in enumerate(cps):
            other = dsts[p].at[1 - c]
            _remote(other, other, send.at[p], recv.at[p], (x, y, 1 - c)).wait_recv()
            cp.wait_send()

    outs = pl.pallas_call(
        body,
        name="pair_gather_" + tag,
        in_specs=[ANY] * np_,
        out_specs=[ANY] * np_,
        out_shape=[jax.ShapeDtypeStruct(a.shape, F32) for a in bufs],
        input_output_aliases={p: p for p in range(np_)},
        scratch_shapes=[pltpu.SemaphoreType.DMA((np_,))] * 2,
    )(*bufs)
    return [o.reshape(2 * a.shape[1], a.shape[2]) for o, a in zip(outs, bufs)]


def _adamw(gsrc, row0, w, m, v, tr, tag):
    rows, width = w.shape
    off = row0 // tr
    bc1 = 1.0 - ADAM_B1 ** ADAM_STEP
    bc2 = 1.0 - ADAM_B2 ** ADAM_STEP

    def body(g_ref, w_ref, m_ref, v_ref, go_ref, d_ref, mo_ref, vo_ref):
        g = g_ref[...]
        m2 = ADAM_B1 * m_ref[...] + (1.0 - ADAM_B1) * g
        v2 = ADAM_B2 * v_ref[...] + (1.0 - ADAM_B2) * (g * g)
        go_ref[...] = g
        mo_ref[...] = m2
        vo_ref[...] = v2
        d_ref[...] = -ADAM_LR * ((m2 / bc1) / (jnp.sqrt(v2 / bc2) + ADAM_EPS) + ADAM_WD * w_ref[...])

    here = pl.BlockSpec((tr, width), lambda i: (i, 0))
    return pl.pallas_call(
        body,
        name="adamw_" + tag,
        grid=(rows // tr,),
        in_specs=[pl.BlockSpec((tr, width), lambda i: (off + i, 0)), here, here, here],
        out_specs=[here] * 4,
        out_shape=[jax.ShapeDtypeStruct((rows, width), F32)] * 4,
        compiler_params=_params(("arbitrary",)),
    )(gsrc, w, m, v)


def kernel(x, mem, mix_pre_g, w_mix_in, conv_a_w, conv_b_w, conv_b_b, ln_b_g, ln_b_b, w_mix_out, mix_post_g, xa_pre_g, mem_norm_g, w_q, w_k, w_v, w_o, xa_post_g, ffn_pre_g, w_gate, w_up, w_down, ffn_post_g, loss_target, m_mix_pre_g, m_w_mix_in, m_conv_a_w, m_conv_b_w, m_conv_b_b, m_ln_b_g, m_ln_b_b, m_w_mix_out, m_mix_post_g, m_xa_pre_g, m_mem_norm_g, m_w_q, m_w_k, m_w_v, m_w_o, m_xa_post_g, m_ffn_pre_g, m_w_gate, m_w_up, m_w_down, m_ffn_post_g, v_mix_pre_g, v_w_mix_in, v_conv_a_w, v_conv_b_w, v_conv_b_b, v_ln_b_g, v_ln_b_b, v_w_mix_out, v_mix_post_g, v_xa_pre_g, v_mem_norm_g, v_w_q, v_w_k, v_w_v, v_w_o, v_xa_post_g, v_ffn_pre_g, v_w_gate, v_w_up, v_w_down, v_ffn_post_g):
    given = dict(locals())
    names = ["mix_pre_g", "w_mix_in", "conv_a_w", "conv_b_w", "conv_b_b", "ln_b_g", "ln_b_b", "w_mix_out",
             "mix_post_g", "xa_pre_g", "mem_norm_g", "w_q", "w_k", "w_v", "w_o", "xa_post_g", "ffn_pre_g",
             "w_gate", "w_up", "w_down", "ffn_post_g"]
    row = lambda a: a.reshape(1, -1)
    cx, cy, cc = lax.axis_index("x"), lax.axis_index("y"), lax.axis_index("c")
    chip = 2 * cx + cy
    ca_blk = conv_a_w.shape[1]

    conv_rows = CONV_A_W + CONV_B_W
    sw = jnp.concatenate([conv_a_w, conv_b_w, jnp.zeros((SMALL_W_ROWS - conv_rows, ca_blk), F32)], axis=0)
    g1g, pr0g, swg = _gather_weights([w_mix_in.astype(BF16), w_mix_out.astype(BF16), sw])
    conv_full = jnp.transpose(swg[:, :conv_rows, :], (1, 0, 2)).reshape(conv_rows, N_CHIPS * ca_blk)
    wa, wb = conv_full[:CONV_A_W], conv_full[CONV_A_W:]
    pr1 = jnp.concatenate([w_q, w_k, w_v, w_o, w_down], axis=0).astype(BF16)
    g2 = jnp.concatenate([w_gate, w_up], axis=0).astype(BF16)

    xs, ms, tgt = x[0], mem[0], loss_target[0]
    x1, u, o1, z1, pr1g, g2g = _mix_fwd(xs, row(mix_pre_g), row(mix_post_g), wa, wb, row(conv_b_b), row(ln_b_g),
                                        row(ln_b_b), g1g, pr0g, [pr1, g2])
    memn, kb, vb = _mem_kv(ms, row(mem_norm_g), pr1g)
    x2, q, o2 = _xattn_fwd(x1, row(xa_pre_g), row(xa_post_g), kb, vb, pr1g)
    h3, gs, us, do3, dx3, loss_part, d_ffn_post = _ffn_fwd(x2, tgt, row(ffn_pre_g), row(ffn_post_g), g2g, pr1g)

    dh3p, cf1, cf2 = _ffn_bwd(h3, do3, gs, us, g2g, pr1g)
    dx1, d_ffn_pre, d_xa_post, d_xa_pre, dk, dv, cxa, yf1, yf2 = _xattn_bwd(
        dx3, dh3p, x2, x1, o2, q, kb, vb, row(ffn_pre_g), row(xa_post_g), row(xa_pre_g), pr1g, cf1, cf2)
    d_mem_g, cxa = _mem_bwd(dk, dv, ms, memn, row(mem_norm_g), pr1g, cxa)
    dx, d_mix_pre, d_mix_post, dwa, dwb, dbb, dlng, dlnb, cm1, cm2, yx = _mix_bwd(
        dx1, xs, o1, u, z1, row(mix_pre_g), row(mix_post_g), wa, wb, row(ln_b_g), row(ln_b_b), g1g, pr0g, cxa)

    small_parts = [d_mix_pre, dwa, dwb, dbb, dlng, dlnb, d_mix_post, d_xa_pre, d_mem_g, d_xa_post, d_ffn_pre,
                   d_ffn_post, loss_part]
    sizes = [p.size for p in small_parts]
    small = jnp.concatenate([p.reshape(-1) for p in small_parts])
    small_rows = -(-small.size // (8 * 128)) * 8
    small = jnp.pad(small, (0, small_rows * 128 - small.size)).reshape(small_rows, 128)
    sems, flying, token = _exchange_start([cm1, cm2], small)
    place = jnp.stack([chip, cc, 2 * chip + cc]).astype(jnp.int32)

    res = {}

    def update(nm, src, row0, tr, transposed=False):
        view = (lambda a: a.T) if transposed else (lambda a: a)
        outs = _adamw(src, row0, view(given[nm]), view(given["m_" + nm]), view(given["v_" + nm]), tr, nm)
        res[nm] = [view(o) for o in outs]

    early = [(yf1, cf1, "gate_up"), (yf2, cf2, "down"), (yx, cxa, "attn")]
    r_gu, r_down, r_attn = _pair_gather([_sum_peers(y, c, place, 2, t, after=(token,)) for y, c, t in early], "early")
    update("w_gate", r_gu, 0, FF_BLK // 2, transposed=True)
    update("w_up", r_gu, FF_BLK, FF_BLK // 2, transposed=True)
    update("w_down", r_down, 0, FF_BLK // 2)
    for nm, row0 in (("w_q", GX_Q), ("w_k", GX_K), ("w_v", GX_V), ("w_o", GX_O)):
        update(nm, r_attn, row0, ROW_BLK)
    done = [res[nm][1] for nm in ("w_gate", "w_up", "w_down", "w_q", "w_k", "w_v", "w_o")]

    (ym1, ym2), small_all = _exchange_wait(sems, flying, 2, done)
    r_in, r_out = _pair_gather([_sum_peers(y, c, place, 2, t) for y, c, t in
                                [(ym1, cm1, "mix_in"), (ym2, cm2, "mix_out")]], "late")
    update("w_mix_in", r_in, 0, 512)
    update("w_mix_out", r_out, 0, ROW_BLK)
    small_sum = _sum_peers(small_all, small, place, 1, "small").reshape(-1)
    red, pos = [], 0
    for p, sz in zip(small_parts, sizes):
        red.append(small_sum[pos:pos + sz].reshape(p.shape))
        pos += sz
    (r_mix_pre, r_wa, r_wb, r_bb, r_lng, r_lnb, r_mix_post, r_xa_pre, r_mem_g, r_xa_post, r_ffn_pre, r_ffn_post,
     r_loss) = red
    loss = r_loss[0, 0]

    small_grads = {"mix_pre_g": r_mix_pre, "conv_b_b": r_bb, "ln_b_g": r_lng, "ln_b_b": r_lnb,
                   "mix_post_g": r_mix_post, "xa_pre_g": r_xa_pre, "mem_norm_g": r_mem_g, "xa_post_g": r_xa_post,
                   "ffn_pre_g": r_ffn_pre, "ffn_post_g": r_ffn_post,
                   "conv_a_w": lax.dynamic_slice_in_dim(r_wa, chip * ca_blk, ca_blk, axis=1),
                   "conv_b_w": lax.dynamic_slice_in_dim(r_wb, chip * ca_blk, ca_blk, axis=1)}
    small_names = list(small_grads)

    def packed(prefix, grads=None):
        flat = jnp.concatenate([(grads[nm] if grads else given[prefix + nm]).reshape(-1) for nm in small_names])
        rows8 = -(-flat.size // (8 * 128)) * 8
        return jnp.pad(flat, (0, rows8 * 128 - flat.size)).reshape(rows8, 128)

    gp = packed("", small_grads)
    outs = _adamw(gp, 0, packed(""), packed("m_"), packed("v_"), gp.shape[0], "small")
    pos = 0
    for nm in small_names:
        shape = given[nm].shape
        sz = given[nm].size
        res[nm] = [o.reshape(-1)[pos:pos + sz].reshape(shape) for o in outs]
        pos += sz

    return (loss, dx[None], *[res[nm][0] for nm in names], *[res[nm][1] for nm in names],
            *[res[nm][2] for nm in names], *[res[nm][3] for nm in names])
```

```python
import jax
import jax.numpy as jnp
from jax import lax
from jax.experimental import pallas as pl
from jax.experimental.pallas import tpu as pltpu

F32 = jnp.float32
BF16 = jnp.bfloat16
MESH = pl.DeviceIdType.MESH

RMS_EPS = 1e-6
LN_EPS = 1e-5
D_MODEL = 1024
D_A = 512
D_B = 512
D_IN_ALL = 3 * D_A + 2 * D_B
CONV_A_W = 3
CONV_B_W = 31
HALO = 32
XA_HEADS = 4
HEAD_DIM = 256
D_FF = 2816
N_CHIPS = 4
N_DEV = 8
FF_BLK = D_FF // N_CHIPS
IN_BLK = D_IN_ALL // N_CHIPS
ROW_BLK = D_MODEL // N_CHIPS

ADAM_LR = 0.001
ADAM_B1 = 0.9
ADAM_B2 = 0.999
ADAM_EPS = 1e-08
ADAM_WD = 0.01
ADAM_STEP = 10

TILE_FWD = 512
TILE_XATTN_FWD = 1024
TILE_FFN = 512
TILE_BWD = 256
FFN_BWD_BLOCKS = 2
CONV_ROWS_FWD = 64
CONV_ROWS = 32
V7X_VMEM_LIMIT = 56 * 1024 * 1024

P1_Q, P1_K, P1_V, P1_O, P1_DOWN = 0, 256, 512, 768, 1024
P1_ROWS = P1_DOWN + FF_BLK
GX_Q, GX_K, GX_V, GX_O = 0, 256, 512, 768
SMALL_W_ROWS = 48

ANY = pl.BlockSpec(memory_space=pl.ANY)


def _mm(a, b):
    return lax.dot_general(a, b, (((1,), (0,)), ((), ())), preferred_element_type=F32)


def _mm_nt(a, b):
    return lax.dot_general(a, b, (((1,), (1,)), ((), ())), preferred_element_type=F32)


def _mm_tn(a, b):
    return lax.dot_general(a, b, (((0,), (0,)), ((), ())), preferred_element_type=F32)


def _sigmoid(x):
    return 0.5 * jnp.tanh(0.5 * x) + 0.5


def _rms(x):
    r = lax.rsqrt(jnp.mean(x * x, axis=-1, keepdims=True) + RMS_EPS)
    return x * r, r


def _rms_bwd(dy, xn, r, g):
    gdy = dy * g
    dx = r * (gdy - xn * jnp.mean(gdy * xn, axis=-1, keepdims=True))
    return dx, jnp.sum(dy * xn, axis=0, keepdims=True)


def _fold8(a):
    out = a[0:8, :]
    for m in range(1, a.shape[0] // 8):
        out = out + a[8 * m:8 * m + 8, :]
    return out


def _full(shape):
    return pl.BlockSpec(shape, lambda *_: (0,) * len(shape))


def _params(sem=None):
    return pltpu.CompilerParams(dimension_semantics=sem, vmem_limit_bytes=V7X_VMEM_LIMIT)


def _load_rows(dst, src_hbm, row0, rows):
    for k in range(N_CHIPS):
        pltpu.sync_copy(src_hbm.at[k, pl.ds(row0, rows), :], dst.at[pl.ds(rows * k, rows), :])


def _load_cols(dst, src_hbm, cols):
    for k in range(N_CHIPS):
        pltpu.sync_copy(src_hbm.at[k], dst.at[:, pl.ds(cols * k, cols)])


def _fill_phases(src, sh, nrows):
    for r in range(1, 8):
        sh[r, pl.ds(0, nrows), :] = src[pl.ds(r, nrows), pl.ds(D_A, D_B)]


def _phase_rows(src, sh, off, start, size):
    r = off % 8
    if r == 0:
        return src[pl.ds(off + start, size), pl.ds(D_A, D_B)]
    return sh[r, pl.ds(off - r + start, size), :]


def _my_place():
    x, y, c = lax.axis_index("x"), lax.axis_index("y"), lax.axis_index("c")
    return x, y, c, ((1 - x, y), (x, 1 - y), (1 - x, 1 - y))


def _remote(src, dst, send_sem, recv_sem, to):
    return pltpu.make_async_remote_copy(src_ref=src, dst_ref=dst, send_sem=send_sem, recv_sem=recv_sem,
                                        device_id=to, device_id_type=MESH)


def _gather_sems(np_):
    return [pltpu.SemaphoreType.DMA((np_, 3))] * 4 + [pltpu.SemaphoreType.DMA((np_,))] * 2


def _gather_start(srcs, dsts, sems):
    send, recv, _, _, osend, orecv = sems
    x, y, c, chips = _my_place()
    j = 2 * x + y
    for p in range(len(srcs)):
        _remote(srcs[p], dsts[p].at[j], osend.at[p], orecv.at[p], (x, y, 1 - c)).start()
        for nn, (kx, ky) in enumerate(chips):
            _remote(srcs[p].at[c], dsts[p].at[j, c], send.at[p, nn], recv.at[p, nn], (kx, ky, c)).start()


def _gather_forward(srcs, dsts, sems):
    send, recv, fsend, frecv, _, _ = sems
    x, y, c, chips = _my_place()
    for nn, (kx, ky) in enumerate(chips):
        for p in range(len(srcs)):
            blk = dsts[p].at[2 * kx + ky, c]
            _remote(blk, blk, send.at[p, nn], recv.at[p, nn], (kx, ky, c)).wait_recv()
            _remote(blk, blk, fsend.at[p, nn], frecv.at[p, nn], (x, y, 1 - c)).start()


def _gather_finish(srcs, dsts, sems):
    send, recv, fsend, frecv, osend, orecv = sems
    x, y, c, chips = _my_place()
    j = 2 * x + y
    for nn, (kx, ky) in enumerate(chips):
        for p in range(len(srcs)):
            other = dsts[p].at[2 * kx + ky, 1 - c]
            _remote(other, other, fsend.at[p, nn], frecv.at[p, nn], (x, y, 1 - c)).wait_recv()
    for nn, (kx, ky) in enumerate(chips):
        for p in range(len(srcs)):
            _remote(srcs[p].at[c], dsts[p].at[j, c], send.at[p, nn], recv.at[p, nn], (kx, ky, c)).wait_send()
            blk = dsts[p].at[2 * kx + ky, c]
            _remote(blk, blk, fsend.at[p, nn], frecv.at[p, nn], (x, y, 1 - c)).wait_send()
    for p in range(len(srcs)):
        _remote(srcs[p], dsts[p].at[j], osend.at[p], orecv.at[p], (x, y, 1 - c)).wait()


def _split_halves(a):
    return a.reshape(2, a.shape[0] // 2, a.shape[1])


def _gather_weights(packs):
    np_ = len(packs)
    split = [_split_halves(a) for a in packs]

    def body(*refs):
        srcs, dsts, sems = refs[:np_], refs[np_:2 * np_], refs[2 * np_:]
        _gather_start(srcs, dsts, sems)
        _gather_forward(srcs, dsts, sems)
        _gather_finish(srcs, dsts, sems)

    outs = pl.pallas_call(
        body,
        name="gather_weights",
        in_specs=[ANY] * np_,
        out_specs=[ANY] * np_,
        out_shape=[jax.ShapeDtypeStruct((N_CHIPS,) + a.shape, a.dtype) for a in split],
        scratch_shapes=_gather_sems(np_),
    )(*split)
    return [o.reshape((N_CHIPS,) + a.shape) for o, a in zip(outs, packs)]


def _peers(x, y, c):
    out = []
    for d in range(1, N_DEV):
        px = 1 - x if d & 4 else x
        py = 1 - y if d & 2 else y
        pc = 1 - c if d & 1 else c
        out.append(((px, py, pc), 4 * px + 2 * py + pc))
    return out


def _scatter_copies(srcs, dsts, send, recv):
    x, y, c, _ = _my_place()
    me = 4 * x + 2 * y + c
    out = []
    for a in range(len(srcs)):
        for d, ((px, py, pc), pidx) in enumerate(_peers(x, y, c)):
            piece = srcs[a].at[2 * px + py, pc]
            out.append((_remote(piece, dsts[a].at[me], send.at[a, d], recv.at[a, d], (px, py, pc)),
                        _remote(piece, dsts[a].at[pidx], send.at[a, d], recv.at[a, d], (px, py, pc))))
    return out


def _scatter_start(srcs, dsts, send, recv):
    for out_cp, _ in _scatter_copies(srcs, dsts, send, recv):
        out_cp.start()


def _scatter_wait(srcs, dsts, send, recv):
    for out_cp, in_cp in _scatter_copies(srcs, dsts, send, recv):
        in_cp.wait_recv()
        out_cp.wait_send()


def _scatter_sems(na):
    return [pltpu.SemaphoreType.DMA((na, N_DEV - 1))] * 2


def _slots(a):
    return jax.ShapeDtypeStruct((N_DEV,) + a.shape[2:], a.dtype)


def _by_halves(a):
    return a.reshape(a.shape[0], 2, a.shape[1] // 2, a.shape[2])


def _mix_fwd(x, gpre, gpost, wa, wb, bb, lng, lnb, g1g, pr0g, late):
    S, D = x.shape
    T = min(TILE_FWD, S)
    n = S // T
    nl = len(late)
    late_split = [_split_halves(a) for a in late]

    def body(*refs):
        (x_ref, gpre_ref, gpost_ref, wa_ref, wb_ref, bb_ref, lng_ref, lnb_ref, g1g_hbm, pr0g_hbm) = refs[:10]
        srcs = refs[10:10 + nl]
        x1_ref, u_ref, o1_ref, z1_ref = refs[10 + nl:14 + nl]
        dsts = refs[14 + nl:14 + 2 * nl]
        win_v, wout_v, ext, sh, z1buf = refs[14 + 2 * nl:19 + 2 * nl]
        sems = refs[19 + 2 * nl:]
        i = pl.program_id(0)

        @pl.when(i == 0)
        def _():
            _gather_start(srcs, dsts, sems)
            _load_cols(win_v, g1g_hbm, IN_BLK)
            _load_rows(wout_v, pr0g_hbm, 0, ROW_BLK)
            ext[pl.ds(0, HALO), :] = jnp.zeros((HALO, D_A + D_B), F32)

        @pl.when(i == max(n - 2, 0))
        def _():
            _gather_forward(srcs, dsts, sems)

        xv = x_ref[...]
        xn, _ = _rms(xv)
        h = (xn * gpre_ref[...]).astype(BF16)
        u = _mm(h, win_v[...])
        u_ref[...] = u.astype(BF16)
        b_a = u[:, 0:D_A]
        cv = u[:, D_A:2 * D_A] * u[:, 2 * D_A:3 * D_A]
        z0 = u[:, 3 * D_A:3 * D_A + D_B] * _sigmoid(u[:, 3 * D_A + D_B:])
        ext[pl.ds(HALO, T), pl.ds(0, D_A)] = cv
        ext[pl.ds(HALO, T), pl.ds(D_A, D_B)] = z0

        conv_a = ext[pl.ds(HALO - 2, T), pl.ds(0, D_A)] * wa_ref[0:1, :]
        for k in range(1, CONV_A_W):
            conv_a = conv_a + ext[pl.ds(HALO - 2 + k, T), pl.ds(0, D_A)] * wa_ref[k:k + 1, :]
        y_a = b_a * conv_a

        _fill_phases(ext, sh, T + HALO - 8)
        base = HALO - (CONV_B_W - 1)

        def chunk(ci, carry):
            start = pl.multiple_of(ci * CONV_ROWS_FWD, 8)
            acc = jnp.broadcast_to(bb_ref[...], (CONV_ROWS_FWD, D_B))
            for k in range(CONV_B_W):
                acc = acc + _phase_rows(ext, sh, base + k, start, CONV_ROWS_FWD) * wb_ref[k:k + 1, :]
            z1buf[pl.ds(start, CONV_ROWS_FWD), :] = acc
            return carry

        lax.fori_loop(0, T // CONV_ROWS_FWD, chunk, 0)
        z1 = z1buf[...]
        z1_ref[...] = z1.astype(BF16)
        mu = jnp.mean(z1, axis=-1, keepdims=True)
        zc = z1 - mu
        rstd = lax.rsqrt(jnp.mean(zc * zc, axis=-1, keepdims=True) + LN_EPS)
        l = zc * rstd * lng_ref[...] + lnb_ref[...]
        y_b = l * _sigmoid(l)
        y = jnp.concatenate([y_a, y_b], axis=-1).astype(BF16)
        o1 = _mm(y, wout_v[...])
        o1_ref[...] = o1.astype(BF16)
        o1n, _ = _rms(o1)
        x1_ref[...] = xv + o1n * gpost_ref[...]
        ext[pl.ds(0, HALO), :] = ext[pl.ds(T, HALO), :]

        @pl.when(i == n - 1)
        def _():
            _gather_finish(srcs, dsts, sems)

    tok = lambda w: pl.BlockSpec((T, w), lambda i: (i, 0))
    outs = pl.pallas_call(
        body,
        name="mix_fwd",
        grid=(n,),
        in_specs=[tok(D), _full((1, D)), _full((1, D)), _full((CONV_A_W, D_A)), _full((CONV_B_W, D_B)),
                  _full((1, D_B)), _full((1, D_B)), _full((1, D_B)), ANY, ANY] + [ANY] * nl,
        out_specs=[tok(D), tok(D_IN_ALL), tok(D), tok(D_B)] + [ANY] * nl,
        out_shape=[jax.ShapeDtypeStruct((S, D), F32), jax.ShapeDtypeStruct((S, D_IN_ALL), BF16),
                   jax.ShapeDtypeStruct((S, D), BF16), jax.ShapeDtypeStruct((S, D_B), BF16)]
        + [jax.ShapeDtypeStruct((N_CHIPS,) + a.shape, a.dtype) for a in late_split],
        scratch_shapes=[pltpu.VMEM((D, D_IN_ALL), BF16), pltpu.VMEM((D_A + D_B, D), BF16),
                        pltpu.VMEM((HALO + T, D_A + D_B), F32), pltpu.VMEM((8, HALO + T, D_B), F32),
                        pltpu.VMEM((T, D_B), F32)] + _gather_sems(nl),
        compiler_params=_params(("arbitrary",)),
    )(x, gpre, gpost, wa, wb, bb, lng, lnb, g1g, pr0g, *late_split)
    return list(outs[:4]) + [o.reshape((N_CHIPS,) + a.shape) for o, a in zip(outs[4:], late)]


def _mix_bwd(dx1, x, o1, u, z1s, gpre, gpost, wa, wb, lng, lnb, g1g, pr0g, after):
    S, D = x.shape
    T = min(TILE_BWD, S)
    n = S // T
    hb = T // HALO

    def body(dx1_ref, x_ref, o1_ref, u_ref, uh_ref, z1_ref, gpre_ref, gpost_ref, wa_ref, wb_ref, lng_ref, lnb_ref,
             g1g_hbm, pr0g_hbm, _,
             dx_ref, dgpre_ref, dgpost_ref, dwa_ref, dwb_ref, dbb_ref, dlng_ref, dlnb_ref, cm1_hbm, cm2_hbm,
             win_v, wout_v, dwin_acc, dwout_acc, ext, ext2, shb, dz0buf, dwb_acc):
        i = pl.program_id(0)

        @pl.when(i == 0)
        def _():
            _load_cols(win_v, g1g_hbm, IN_BLK)
            _load_rows(wout_v, pr0g_hbm, 0, ROW_BLK)
            dwin_acc[...] = jnp.zeros_like(dwin_acc)
            dwout_acc[...] = jnp.zeros_like(dwout_acc)
            dwb_acc[...] = jnp.zeros_like(dwb_acc)
            ext2[pl.ds(T, HALO), :] = jnp.zeros((HALO, D_A + D_B), F32)
            for ref in (dgpre_ref, dgpost_ref, dwa_ref, dbb_ref, dlng_ref, dlnb_ref):
                ref[...] = jnp.zeros_like(ref)

        o1n, r1 = _rms(o1_ref[...].astype(F32))
        dx1v = dx1_ref[...]
        d_o1, dgp = _rms_bwd(dx1v, o1n, r1, gpost_ref[...])
        dgpost_ref[...] += dgp
        d_o1b = d_o1.astype(BF16)
        dy = _mm_nt(d_o1b, wout_v[...])

        first = (i == n - 1).astype(F32)
        uh = uh_ref[...].astype(F32) * (1.0 - first)
        ext[pl.ds(0, HALO), pl.ds(0, D_A)] = uh[:, D_A:2 * D_A] * uh[:, 2 * D_A:3 * D_A]
        uf = u_ref[...].astype(F32)
        b_a = uf[:, 0:D_A]
        c_a = uf[:, D_A:2 * D_A]
        v_a = uf[:, 2 * D_A:3 * D_A]
        gv = uf[:, 3 * D_A:3 * D_A + D_B]
        sg = _sigmoid(uf[:, 3 * D_A + D_B:])
        ext[pl.ds(HALO, T), pl.ds(0, D_A)] = c_a * v_a
        ext[pl.ds(HALO, T), pl.ds(D_A, D_B)] = gv * sg
        conv_a = ext[pl.ds(HALO - 2, T), pl.ds(0, D_A)] * wa_ref[0:1, :]
        for k in range(1, CONV_A_W):
            conv_a = conv_a + ext[pl.ds(HALO - 2 + k, T), pl.ds(0, D_A)] * wa_ref[k:k + 1, :]
        z1 = z1_ref[...].astype(F32)
        mu = jnp.mean(z1, axis=-1, keepdims=True)
        zc = z1 - mu
        rstd = lax.rsqrt(jnp.mean(zc * zc, axis=-1, keepdims=True) + LN_EPS)
        zn = zc * rstd
        l = zn * lng_ref[...] + lnb_ref[...]
        sl = _sigmoid(l)
        y = jnp.concatenate([b_a * conv_a, l * sl], axis=-1).astype(BF16)
        dwout_acc[...] += _mm_tn(y, d_o1b)

        dy_a = dy[:, 0:D_A]
        dl = dy[:, D_A:] * (sl * (1.0 + l * (1.0 - sl)))
        dlng_ref[...] += jnp.sum(dl * zn, axis=0, keepdims=True)
        dlnb_ref[...] += jnp.sum(dl, axis=0, keepdims=True)
        dzn = dl * lng_ref[...]
        dz1 = rstd * (dzn - jnp.mean(dzn, axis=-1, keepdims=True) - zn * jnp.mean(dzn * zn, axis=-1, keepdims=True))
        dbb_ref[...] += jnp.sum(dz1, axis=0, keepdims=True)
        d_conv = dy_a * b_a
        ext2[pl.ds(0, T), pl.ds(0, D_A)] = d_conv
        ext2[pl.ds(0, T), pl.ds(D_A, D_B)] = dz1

        d_cv = ext2[pl.ds(CONV_A_W - 1, T), pl.ds(0, D_A)] * wa_ref[0:1, :]
        for k in range(1, CONV_A_W):
            d_cv = d_cv + ext2[pl.ds(CONV_A_W - 1 - k, T), pl.ds(0, D_A)] * wa_ref[k:k + 1, :]
        for k in range(CONV_A_W):
            dwa_ref[k:k + 1, :] += jnp.sum(d_conv * ext[pl.ds(HALO - 2 + k, T), pl.ds(0, D_A)], axis=0, keepdims=True)

        _fill_phases(ext2, shb, T + HALO - 8)

        def chunk(ci, carry):
            start = pl.multiple_of(ci * CONV_ROWS, 8)
            z0c = ext[pl.ds(HALO + start, CONV_ROWS), pl.ds(D_A, D_B)]
            acc = jnp.zeros((CONV_ROWS, D_B), F32)
            for k in range(CONV_B_W):
                ahead = _phase_rows(ext2, shb, CONV_B_W - 1 - k, start, CONV_ROWS)
                acc = acc + ahead * wb_ref[k:k + 1, :]
                dwb_acc[k] += _fold8(z0c * ahead)
            dz0buf[pl.ds(start, CONV_ROWS), :] = acc
            return carry

        lax.fori_loop(0, T // CONV_ROWS, chunk, 0)
        dz0 = dz0buf[...]
        du = jnp.concatenate([dy_a * conv_a, d_cv * v_a, d_cv * c_a, dz0 * sg, dz0 * gv * sg * (1.0 - sg)],
                             axis=-1).astype(BF16)
        dh = _mm_nt(du, win_v[...])
        xv = x_ref[...]
        xn, r0 = _rms(xv)
        dwin_acc[...] += _mm_tn((xn * gpre_ref[...]).astype(BF16), du)
        dxp, dg0 = _rms_bwd(dh, xn, r0, gpre_ref[...])
        dgpre_ref[...] += dg0
        dx_ref[...] = dx1v + dxp
        ext2[pl.ds(T, HALO), :] = ext2[pl.ds(0, HALO), :]

        @pl.when(i == n - 1)
        def _():
            for k in range(CONV_B_W):
                dwb_ref[k:k + 1, :] = jnp.sum(dwb_acc[k], axis=0, keepdims=True)
            win_v[...] = dwin_acc[...].astype(BF16)
            wout_v[...] = dwout_acc[...].astype(BF16)
            for k in range(N_CHIPS):
                pltpu.sync_copy(win_v.at[:, pl.ds(IN_BLK * k, IN_BLK)], cm1_hbm.at[k])
                pltpu.sync_copy(wout_v.at[pl.ds(ROW_BLK * k, ROW_BLK), :], cm2_hbm.at[k])

    rev = lambda w: pl.BlockSpec((T, w), lambda i: (n - 1 - i, 0))
    halo = pl.BlockSpec((HALO, D_IN_ALL), lambda i: (jnp.maximum((n - 1 - i) * hb - 1, 0), 0))
    return pl.pallas_call(
        body,
        name="mix_bwd",
        grid=(n,),
        in_specs=[rev(D), rev(D), rev(D), rev(D_IN_ALL), halo, rev(D_B), _full((1, D)), _full((1, D)),
                  _full((CONV_A_W, D_A)), _full((CONV_B_W, D_B)), _full((1, D_B)), _full((1, D_B)), ANY, ANY, ANY],
        out_specs=[rev(D), _full((1, D)), _full((1, D)), _full((CONV_A_W, D_A)), _full((CONV_B_W, D_B)),
                   _full((1, D_B)), _full((1, D_B)), _full((1, D_B)), ANY, ANY],
        out_shape=[jax.ShapeDtypeStruct((S, D), F32), jax.ShapeDtypeStruct((1, D), F32),
                   jax.ShapeDtypeStruct((1, D), F32), jax.ShapeDtypeStruct((CONV_A_W, D_A), F32),
                   jax.ShapeDtypeStruct((CONV_B_W, D_B), F32), jax.ShapeDtypeStruct((1, D_B), F32),
                   jax.ShapeDtypeStruct((1, D_B), F32), jax.ShapeDtypeStruct((1, D_B), F32),
                   jax.ShapeDtypeStruct((N_CHIPS, D, IN_BLK), BF16),
                   jax.ShapeDtypeStruct((N_CHIPS, ROW_BLK, D), BF16)],
        scratch_shapes=[pltpu.VMEM((D, D_IN_ALL), BF16), pltpu.VMEM((D_A + D_B, D), BF16),
                        pltpu.VMEM((D, D_IN_ALL), F32), pltpu.VMEM((D_A + D_B, D), F32),
                        pltpu.VMEM((HALO + T, D_A + D_B), F32), pltpu.VMEM((HALO + T, D_A + D_B), F32),
                        pltpu.VMEM((8, HALO + T, D_B), F32),
                        pltpu.VMEM((T, D_B), F32), pltpu.VMEM((CONV_B_W, 8, D_B), F32)],
        compiler_params=_params(("arbitrary",)),
    )(dx1, x, o1, u, u, z1s, gpre, gpost, wa, wb, lng, lnb, g1g, pr0g, after)


def _mem_kv(mem, gmem, pr1g):
    M, D = mem.shape

    def body(mem_ref, g_ref, pr1g_hbm, memn_ref, k_ref, v_ref, wk_v, wv_v):
        _load_rows(wk_v, pr1g_hbm, P1_K, ROW_BLK)
        _load_rows(wv_v, pr1g_hbm, P1_V, ROW_BLK)
        mn, _ = _rms(mem_ref[...])
        mb = (mn * g_ref[...]).astype(BF16)
        memn_ref[...] = mb
        k_ref[...] = _mm(mb, wk_v[...]).astype(BF16)
        v_ref[...] = _mm(mb, wv_v[...]).astype(BF16)

    return pl.pallas_call(
        body,
        name="mem_kv",
        grid=(1,),
        in_specs=[_full((M, D)), _full((1, D)), ANY],
        out_specs=[_full((M, D))] * 3,
        out_shape=[jax.ShapeDtypeStruct((M, D), BF16)] * 3,
        scratch_shapes=[pltpu.VMEM((D, D), BF16), pltpu.VMEM((D, D), BF16)],
        compiler_params=_params(("arbitrary",)),
    )(mem, gmem, pr1g)


def _attend(qb, kb, vb):
    scale = HEAD_DIM ** -0.5
    ps, os_ = [], []
    for hd in range(XA_HEADS):
        cols = slice(HEAD_DIM * hd, HEAD_DIM * (hd + 1))
        s = _mm_nt(qb[:, cols], kb[:, cols]) * scale
        e = jnp.exp(s - jnp.max(s, axis=-1, keepdims=True))
        p = e * (1.0 / jnp.sum(e, axis=-1, keepdims=True))
        ps.append(p)
        os_.append(_mm(p.astype(BF16), vb[:, cols]))
    return ps, jnp.concatenate(os_, axis=-1).astype(BF16)


def _xattn_fwd(x1, gpre, gpost, kb, vb, pr1g):
    S, D = x1.shape
    M = kb.shape[0]
    T = min(TILE_XATTN_FWD, S)
    n = S // T

    def body(x1_ref, gpre_ref, gpost_ref, k_ref, v_ref, pr1g_hbm, x2_ref, q_ref, o2_ref, wq_v, wo_v):
        @pl.when(pl.program_id(0) == 0)
        def _():
            _load_rows(wq_v, pr1g_hbm, P1_Q, ROW_BLK)
            _load_rows(wo_v, pr1g_hbm, P1_O, ROW_BLK)

        xv = x1_ref[...]
        xn, _ = _rms(xv)
        qb = _mm((xn * gpre_ref[...]).astype(BF16), wq_v[...]).astype(BF16)
        q_ref[...] = qb
        _, ob = _attend(qb, k_ref[...], v_ref[...])
        o2 = _mm(ob, wo_v[...])
        o2_ref[...] = o2.astype(BF16)
        o2n, _ = _rms(o2)
        x2_ref[...] = xv + o2n * gpost_ref[...]

    tok = lambda w: pl.BlockSpec((T, w), lambda i: (i, 0))
    return pl.pallas_call(
        body,
        name="xattn_fwd",
        grid=(n,),
        in_specs=[tok(D), _full((1, D)), _full((1, D)), _full((M, D)), _full((M, D)), ANY],
        out_specs=[tok(D), tok(D), tok(D)],
        out_shape=[jax.ShapeDtypeStruct((S, D), F32), jax.ShapeDtypeStruct((S, D), BF16),
                   jax.ShapeDtypeStruct((S, D), BF16)],
        scratch_shapes=[pltpu.VMEM((D, D), BF16), pltpu.VMEM((D, D), BF16)],
        compiler_params=_params(("arbitrary",)),
    )(x1, gpre, gpost, kb, vb, pr1g)


def _xattn_bwd(dx3, dh3p, x2, x1, o2, q, kb, vb, gffn, gpost, gpre, pr1g, cf1, cf2):
    S, D = x1.shape
    M = kb.shape[0]
    T = min(TILE_BWD, S)
    n = S // T
    scale = HEAD_DIM ** -0.5
    nparts = dh3p.shape[0]
    cfs = [_by_halves(cf1), _by_halves(cf2)]

    def body(*refs):
        dx3_ref, dh3_refs = refs[0], refs[1:1 + nparts]
        (x2_ref, x1_ref, o2_ref, q_ref, k_ref, v_ref, gffn_ref, gpost_ref, gpre_ref, pr1g_hbm, cf1_hbm, cf2_hbm,
         dx1_ref, dgffn_ref, dgpost_ref, dgpre_ref, dk_ref, dv_ref, cx_hbm, yf1_hbm, yf2_hbm,
         wq_v, wo_v, dwq_acc, dwo_acc, send, recv) = refs[1 + nparts:]
        i = pl.program_id(0)

        @pl.when(i == 0)
        def _():
            _scatter_start([cf1_hbm, cf2_hbm], [yf1_hbm, yf2_hbm], send, recv)
            _load_rows(wq_v, pr1g_hbm, P1_Q, ROW_BLK)
            _load_rows(wo_v, pr1g_hbm, P1_O, ROW_BLK)
            dwq_acc[...] = jnp.zeros_like(dwq_acc)
            dwo_acc[...] = jnp.zeros_like(dwo_acc)
            for ref in (dgffn_ref, dgpost_ref, dgpre_ref, dk_ref, dv_ref):
                ref[...] = jnp.zeros_like(ref)

        x2n, r2 = _rms(x2_ref[...])
        dh3 = dh3_refs[0][...].astype(F32)
        for ref in dh3_refs[1:]:
            dh3 = dh3 + ref[...].astype(F32)
        dxp, dg = _rms_bwd(dh3, x2n, r2, gffn_ref[...])
        dgffn_ref[...] += dg
        dx2 = dx3_ref[...] + dxp
        o2n, ro = _rms(o2_ref[...].astype(F32))
        d_o2, dg = _rms_bwd(dx2, o2n, ro, gpost_ref[...])
        dgpost_ref[...] += dg
        d_o2b = d_o2.astype(BF16)
        d_o = _mm_nt(d_o2b, wo_v[...]).astype(BF16)
        qb = q_ref[...]
        kv = k_ref[...]
        vv = v_ref[...]
        ps, ob = _attend(qb, kv, vv)
        dwo_acc[...] += _mm_tn(ob, d_o2b)
        dqs = []
        for hd in range(XA_HEADS):
            cols = slice(HEAD_DIM * hd, HEAD_DIM * (hd + 1))
            p = ps[hd]
            dp = _mm_nt(d_o[:, cols], vv[:, cols])
            dv_ref[:, cols] += _mm_tn(p.astype(BF16), d_o[:, cols])
            ds = (p * (dp - jnp.sum(p * dp, axis=-1, keepdims=True)) * scale).astype(BF16)
            dqs.append(_mm(ds, kv[:, cols]))
            dk_ref[:, cols] += _mm_tn(ds, qb[:, cols])
        dq = jnp.concatenate(dqs, axis=-1).astype(BF16)
        dh2 = _mm_nt(dq, wq_v[...])
        x1n, r1 = _rms(x1_ref[...])
        dwq_acc[...] += _mm_tn((x1n * gpre_ref[...]).astype(BF16), dq)
        dxp, dg = _rms_bwd(dh2, x1n, r1, gpre_ref[...])
        dgpre_ref[...] += dg
        dx1_ref[...] = dx2 + dxp

        @pl.when(i == n - 1)
        def _():
            wq_v[...] = dwq_acc[...].astype(BF16)
            wo_v[...] = dwo_acc[...].astype(BF16)
            for k in range(N_CHIPS):
                rows = pl.ds(ROW_BLK * k, ROW_BLK)
                pltpu.sync_copy(wq_v.at[rows, :], cx_hbm.at[k, pl.ds(GX_Q, ROW_BLK), :])
                pltpu.sync_copy(wo_v.at[rows, :], cx_hbm.at[k, pl.ds(GX_O, ROW_BLK), :])
            _scatter_wait([cf1_hbm, cf2_hbm], [yf1_hbm, yf2_hbm], send, recv)

    tok = lambda w: pl.BlockSpec((T, w), lambda i: (i, 0))
    part = lambda j: pl.BlockSpec((None, T, D), lambda i: (j, i, 0))
    return pl.pallas_call(
        body,
        name="xattn_bwd",
        grid=(n,),
        in_specs=[tok(D)] + [part(j) for j in range(nparts)] + [tok(D), tok(D), tok(D), tok(D), _full((M, D)),
                                                                 _full((M, D)), _full((1, D)), _full((1, D)),
                                                                 _full((1, D)), ANY, ANY, ANY],
        out_specs=[tok(D), _full((1, D)), _full((1, D)), _full((1, D)), _full((M, D)), _full((M, D)), ANY, ANY, ANY],
        out_shape=[jax.ShapeDtypeStruct((S, D), F32), jax.ShapeDtypeStruct((1, D), F32),
                   jax.ShapeDtypeStruct((1, D), F32), jax.ShapeDtypeStruct((1, D), F32),
                   jax.ShapeDtypeStruct((M, D), F32), jax.ShapeDtypeStruct((M, D), F32),
                   jax.ShapeDtypeStruct((N_CHIPS, D, D), BF16), _slots(cfs[0]), _slots(cfs[1])],
        scratch_shapes=[pltpu.VMEM((D, D), BF16), pltpu.VMEM((D, D), BF16),
                        pltpu.VMEM((D, D), F32), pltpu.VMEM((D, D), F32)] + _scatter_sems(2),
        compiler_params=_params(("arbitrary",)),
    )(dx3, *([dh3p] * nparts), x2, x1, o2, q, kb, vb, gffn, gpost, gpre, pr1g, *cfs)


def _mem_bwd(dk, dv, mem, memn, gmem, pr1g, cx_in):
    M, D = mem.shape

    def body(dk_ref, dv_ref, mem_ref, memn_ref, g_ref, pr1g_hbm, cx_hbm, dg_ref, cx_out, wk_v, wv_v):
        del cx_hbm
        _load_rows(wk_v, pr1g_hbm, P1_K, ROW_BLK)
        _load_rows(wv_v, pr1g_hbm, P1_V, ROW_BLK)
        dkb = dk_ref[...].astype(BF16)
        dvb = dv_ref[...].astype(BF16)
        mb = memn_ref[...]
        dmn = _mm_nt(dkb, wk_v[...]) + _mm_nt(dvb, wv_v[...])
        mn, _ = _rms(mem_ref[...])
        dg_ref[...] = jnp.sum(dmn * mn, axis=0, keepdims=True)
        wk_v[...] = _mm_tn(mb, dkb).astype(BF16)
        wv_v[...] = _mm_tn(mb, dvb).astype(BF16)
        for k in range(N_CHIPS):
            rows = pl.ds(ROW_BLK * k, ROW_BLK)
            pltpu.sync_copy(wk_v.at[rows, :], cx_out.at[k, pl.ds(GX_K, ROW_BLK), :])
            pltpu.sync_copy(wv_v.at[rows, :], cx_out.at[k, pl.ds(GX_V, ROW_BLK), :])

    return pl.pallas_call(
        body,
        name="mem_bwd",
        grid=(1,),
        in_specs=[_full((M, D)), _full((M, D)), _full((M, D)), _full((M, D)), _full((1, D)), ANY, ANY],
        out_specs=[_full((1, D)), ANY],
        out_shape=[jax.ShapeDtypeStruct((1, D), F32), jax.ShapeDtypeStruct(cx_in.shape, BF16)],
        input_output_aliases={6: 1},
        scratch_shapes=[pltpu.VMEM((D, D), BF16), pltpu.VMEM((D, D), BF16)],
        compiler_params=_params(("arbitrary",)),
    )(dk, dv, mem, memn, gmem, pr1g, cx_in)


def _ffn_fwd(x2, target, gpre, gpost, g2g, pr1g):
    S, D = x2.shape
    T = min(TILE_FFN, S)
    n = S // T

    def body(x2_ref, t_ref, gpre_ref, gpost_ref, g2g_hbm, pr1g_hbm,
             h3_ref, g_hbm, u_hbm, do3_ref, dx3_ref, loss_ref, dgpost_ref, wg_v, wu_v, wd_v, gst, ust, sem):
        i = pl.program_id(0)

        @pl.when(i == 0)
        def _():
            pltpu.sync_copy(g2g_hbm.at[:, pl.ds(0, D), :], wg_v)
            pltpu.sync_copy(g2g_hbm.at[:, pl.ds(D, D), :], wu_v)
            pltpu.sync_copy(pr1g_hbm.at[:, pl.ds(P1_DOWN, FF_BLK), :], wd_v)
            loss_ref[...] = jnp.zeros_like(loss_ref)
            dgpost_ref[...] = jnp.zeros_like(dgpost_ref)

        xv = x2_ref[...]
        xn, _ = _rms(xv)
        hb = (xn * gpre_ref[...]).astype(BF16)
        h3_ref[...] = hb
        o3 = jnp.zeros((T, D), F32)
        out = [None, None]
        for c in range(N_CHIPS):
            slot = c % 2
            if out[slot] is not None:
                for cp in out[slot]:
                    cp.wait()
            g = _mm(hb, wg_v[c])
            u = _mm(hb, wu_v[c])
            gst[slot] = g.astype(BF16)
            ust[slot] = u.astype(BF16)
            out[slot] = (pltpu.make_async_copy(gst.at[slot], g_hbm.at[c, i], sem.at[0, slot]),
                         pltpu.make_async_copy(ust.at[slot], u_hbm.at[c, i], sem.at[1, slot]))
            for cp in out[slot]:
                cp.start()
            o3 = o3 + _mm((g * _sigmoid(g) * u).astype(BF16), wd_v[c])
        for pair in out:
            for cp in pair:
                cp.wait()
        o3n, r3 = _rms(o3)
        diff = xv + o3n * gpost_ref[...] - t_ref[...]
        sq = jnp.sum(jnp.sum(diff * diff, axis=-1, keepdims=True), axis=0, keepdims=True)
        loss_ref[...] += sq * (0.5 / D)
        dx3 = diff * (1.0 / D)
        dx3_ref[...] = dx3
        d_o3, dg = _rms_bwd(dx3, o3n, r3, gpost_ref[...])
        dgpost_ref[...] += dg
        do3_ref[...] = d_o3.astype(BF16)

    tok = lambda w: pl.BlockSpec((T, w), lambda i: (i, 0))
    h3, gs, us, do3, dx3, loss, dgpost = pl.pallas_call(
        body,
        name="ffn_fwd",
        grid=(n,),
        in_specs=[tok(D), tok(D), _full((1, D)), _full((1, D)), ANY, ANY],
        out_specs=[tok(D), ANY, ANY, tok(D), tok(D), _full((1, 128)), _full((1, D))],
        out_shape=[jax.ShapeDtypeStruct((S, D), BF16), jax.ShapeDtypeStruct((N_CHIPS, n, T, FF_BLK), BF16),
                   jax.ShapeDtypeStruct((N_CHIPS, n, T, FF_BLK), BF16), jax.ShapeDtypeStruct((S, D), BF16),
                   jax.ShapeDtypeStruct((S, D), F32), jax.ShapeDtypeStruct((1, 128), F32),
                   jax.ShapeDtypeStruct((1, D), F32)],
        scratch_shapes=[pltpu.VMEM((N_CHIPS, D, FF_BLK), BF16), pltpu.VMEM((N_CHIPS, D, FF_BLK), BF16),
                        pltpu.VMEM((N_CHIPS, FF_BLK, D), BF16), pltpu.VMEM((2, T, FF_BLK), BF16),
                        pltpu.VMEM((2, T, FF_BLK), BF16), pltpu.SemaphoreType.DMA((2, 2))],
        compiler_params=_params(("arbitrary",)),
    )(x2, target, gpre, gpost, g2g, pr1g)
    return h3, gs.reshape(N_CHIPS, S, FF_BLK), us.reshape(N_CHIPS, S, FF_BLK), do3, dx3, loss, dgpost


def _ffn_bwd(h3, do3, gs, us, g2g, pr1g):
    S, D = h3.shape
    T = min(TILE_FFN, S)
    n = S // T
    NP = FFN_BWD_BLOCKS

    def body(h3_ref, do3_ref, g_ref, u_ref, g2g_hbm, pr1g_hbm, dh3_ref, cf1_hbm, cf2_hbm,
             wg_v, wu_v, wd_v, dwg_acc, dwu_acc, dwd_acc):
        jp = pl.program_id(0)
        i = pl.program_id(1)
        blocks = pl.ds(NP * jp, NP)

        @pl.when(i == 0)
        def _():
            pltpu.sync_copy(g2g_hbm.at[blocks, pl.ds(0, D), :], wg_v)
            pltpu.sync_copy(g2g_hbm.at[blocks, pl.ds(D, D), :], wu_v)
            pltpu.sync_copy(pr1g_hbm.at[blocks, pl.ds(P1_DOWN, FF_BLK), :], wd_v)
            dwg_acc[...] = jnp.zeros_like(dwg_acc)
            dwu_acc[...] = jnp.zeros_like(dwu_acc)
            dwd_acc[...] = jnp.zeros_like(dwd_acc)

        hb = h3_ref[...]
        d_o3 = do3_ref[...]
        dh = jnp.zeros((T, D), F32)
        for c in range(NP):
            da = _mm_nt(d_o3, wd_v[c])
            g = g_ref[c].astype(F32)
            u = u_ref[c].astype(F32)
            sg = _sigmoid(g)
            sl = g * sg
            dwd_acc[c] += _mm_tn((sl * u).astype(BF16), d_o3)
            dub = (da * sl).astype(BF16)
            dgb = (da * u * (sg * (1.0 + g * (1.0 - sg)))).astype(BF16)
            dwg_acc[c] += _mm_tn(dgb, hb)
            dwu_acc[c] += _mm_tn(dub, hb)
            dh = dh + _mm_nt(dgb, wg_v[c]) + _mm_nt(dub, wu_v[c])
        dh3_ref[...] = dh.astype(BF16)

        @pl.when(i == n - 1)
        def _():
            wd_v[...] = dwg_acc[...].astype(BF16)
            pltpu.sync_copy(wd_v, cf1_hbm.at[blocks, pl.ds(0, FF_BLK), :])
            wd_v[...] = dwu_acc[...].astype(BF16)
            pltpu.sync_copy(wd_v, cf1_hbm.at[blocks, pl.ds(FF_BLK, FF_BLK), :])
            wd_v[...] = dwd_acc[...].astype(BF16)
            pltpu.sync_copy(wd_v, cf2_hbm.at[blocks])

    tok = lambda w: pl.BlockSpec((T, w), lambda jp, i: (i, 0))
    blk = pl.BlockSpec((NP, T, FF_BLK), lambda jp, i: (jp, i, 0))
    return pl.pallas_call(
        body,
        name="ffn_bwd",
        grid=(N_CHIPS // NP, n),
        in_specs=[tok(D), tok(D), blk, blk, ANY, ANY],
        out_specs=[pl.BlockSpec((None, T, D), lambda jp, i: (jp, i, 0)), ANY, ANY],
        out_shape=[jax.ShapeDtypeStruct((N_CHIPS // NP, S, D), BF16),
                   jax.ShapeDtypeStruct((N_CHIPS, 2 * FF_BLK, D), BF16),
                   jax.ShapeDtypeStruct((N_CHIPS, FF_BLK, D), BF16)],
        scratch_shapes=[pltpu.VMEM((NP, D, FF_BLK), BF16), pltpu.VMEM((NP, D, FF_BLK), BF16),
                        pltpu.VMEM((NP, FF_BLK, D), BF16), pltpu.VMEM((NP, FF_BLK, D), F32),
                        pltpu.VMEM((NP, FF_BLK, D), F32), pltpu.VMEM((NP, FF_BLK, D), F32)],
        compiler_params=_params(("arbitrary", "arbitrary")),
    )(h3, do3, gs, us, g2g, pr1g)


IN_HBM = pl.BlockSpec(memory_space=pltpu.HBM)
IN_SEMAPHORES = pl.BlockSpec(memory_space=pltpu.SEMAPHORE)


def _split_copies(refs, na, nw, sems):
    srcs, lands = refs[:na + nw], refs[na + nw:2 * (na + nw)]
    send, recv = sems
    x, y, c, _ = _my_place()
    me = 4 * x + 2 * y + c
    pairs = []
    for d, ((px, py, pc), pidx) in enumerate(_peers(x, y, c)):
        for a in range(na + nw):
            k = d * (na + nw) + a
            src = srcs[a].at[2 * px + py, pc] if a < na else srcs[a]
            pairs.append((_remote(src, lands[a].at[me], send.at[k], recv.at[k], (px, py, pc)),
                          _remote(src, lands[a].at[pidx], send.at[k], recv.at[k], (px, py, pc))))
    return pairs


def _exchange_start(contribs, whole, tag):
    na, nw = len(contribs), len(whole)
    cs = [_by_halves(a) for a in contribs]
    flying = cs + list(whole) + [lax.empty(_slots(a).shape, a.dtype) for a in cs] + [
        lax.empty((N_DEV,) + a.shape, a.dtype) for a in whole]
    nf = len(flying)

    def body(*refs):
        for mine, _ in _split_copies(refs[:nf], na, nw, refs[nf:nf + 2]):
            mine.start()
        refs[-1][...] = jnp.zeros_like(refs[-1])

    ncopies = (N_DEV - 1) * (na + nw)
    outs = pl.pallas_call(
        body,
        name="exchange_start_" + tag,
        in_specs=[IN_HBM] * nf,
        out_specs=[IN_SEMAPHORES] * 2 + [IN_HBM] * nf + [pl.BlockSpec(memory_space=pltpu.VMEM)],
        out_shape=[pltpu.SemaphoreType.DMA((ncopies,)), pltpu.SemaphoreType.DMA((ncopies,))]
        + [pltpu.HBM(a.shape, a.dtype) for a in flying] + [jax.ShapeDtypeStruct((8, 128), F32)],
        input_output_aliases={k: 2 + k for k in range(nf)},
        compiler_params=pltpu.CompilerParams(has_side_effects=pltpu.SideEffectType.DATAFLOW_SIDE_EFFECTING),
    )(*[pltpu.with_memory_space_constraint(a, pltpu.HBM) for a in flying])
    return outs[:2], outs[2:2 + nf], outs[-1]


def _exchange_wait(sems, flying, na, nw, after, tag):
    nf = len(flying)

    def body(*refs):
        for mine, theirs in _split_copies(refs[:nf], na, nw, refs[nf:nf + 2]):
            theirs.wait_recv()
            mine.wait_send()

    outs = pl.pallas_call(
        body,
        name="exchange_wait_" + tag,
        in_specs=[IN_HBM] * nf + [IN_SEMAPHORES] * 2 + [ANY] * len(after),
        out_specs=[IN_HBM] * nf,
        out_shape=[pltpu.HBM(a.shape, a.dtype) for a in flying],
        input_output_aliases={k: k for k in range(nf)},
        compiler_params=pltpu.CompilerParams(has_side_effects=pltpu.SideEffectType.DATAFLOW_SIDE_EFFECTING),
    )(*flying, *sems, *after)
    return outs[na + nw:]


def _sum_peers(parts, own, place, steps, tag, after=()):
    _, rows, w = parts.shape
    tr = rows // steps
    if own.ndim == 3:
        own = _by_halves(own)

    def body(place_ref, *refs):
        p_refs, own_ref, o_ref = refs[:N_DEV], refs[N_DEV], refs[-1]
        me = place_ref[2]
        acc = None
        for s in range(N_DEV):
            term = jnp.where(me == s, own_ref[...], p_refs[s][...]).astype(F32)
            acc = term if acc is None else acc + term
        o_ref[...] = acc

    def other(s):
        return lambda i, pr: (jnp.where(pr[2] == s, (s + 1) % N_DEV, s), i, 0)

    own_spec = (pl.BlockSpec((None, None, tr, w), lambda i, pr: (pr[0], pr[1], i, 0)) if own.ndim == 4 else
                pl.BlockSpec((tr, w), lambda i, pr: (i, 0)))
    out_spec = (pl.BlockSpec((None, tr, w), lambda i, pr: (pr[1], i, 0)) if own.ndim == 4 else
                pl.BlockSpec((tr, w), lambda i, pr: (i, 0)))
    return pl.pallas_call(
        body,
        name="sum_peers_" + tag,
        grid_spec=pltpu.PrefetchScalarGridSpec(
            num_scalar_prefetch=1,
            grid=(steps,),
            in_specs=[pl.BlockSpec((None, tr, w), other(s)) for s in range(N_DEV)] + [own_spec] + [ANY] * len(after),
            out_specs=out_spec,
        ),
        out_shape=jax.ShapeDtypeStruct((2, rows, w) if own.ndim == 4 else (rows, w), F32),
        compiler_params=_params(("arbitrary",)),
    )(place, *([parts] * N_DEV), own, *after)


def _pair_gather(bufs, tag):
    np_ = len(bufs)

    def body(*refs):
        srcs, dsts = refs[:np_], refs[np_:2 * np_]
        send, recv = refs[2 * np_:]
        x, y, c, _ = _my_place()
        cps = []
        for p in range(np_):
            cp = _remote(srcs[p].at[c], dsts[p].at[c], send.at[p], recv.at[p], (x, y, 1 - c))
            cp.start()
            cps.append(cp)
        for p, cp in enumerate(cps):
            other = dsts[p].at[1 - c]
            _remote(other, other, send.at[p], recv.at[p], (x, y, 1 - c)).wait_recv()
            cp.wait_send()

    outs = pl.pallas_call(
        body,
        name="pair_gather_" + tag,
        in_specs=[ANY] * np_,
        out_specs=[ANY] * np_,
        out_shape=[jax.ShapeDtypeStruct(a.shape, F32) for a in bufs],
        input_output_aliases={p: p for p in range(np_)},
        scratch_shapes=[pltpu.SemaphoreType.DMA((np_,))] * 2,
    )(*bufs)
    return [o.reshape(2 * a.shape[1], a.shape[2]) for o, a in zip(outs, bufs)]


def _adamw(gsrc, row0, w, m, v, tr, tag):
    rows, width = w.shape
    off = row0 // tr
    bc1 = 1.0 - ADAM_B1 ** ADAM_STEP
    bc2 = 1.0 - ADAM_B2 ** ADAM_STEP

    def body(g_ref, w_ref, m_ref, v_ref, go_ref, d_ref, mo_ref, vo_ref):
        g = g_ref[...]
        m2 = ADAM_B1 * m_ref[...] + (1.0 - ADAM_B1) * g
        v2 = ADAM_B2 * v_ref[...] + (1.0 - ADAM_B2) * (g * g)
        go_ref[...] = g
        mo_ref[...] = m2
        vo_ref[...] = v2
        d_ref[...] = -ADAM_LR * ((m2 / bc1) / (jnp.sqrt(v2 / bc2) + ADAM_EPS) + ADAM_WD * w_ref[...])

    here = pl.BlockSpec((tr, width), lambda i: (i, 0))
    return pl.pallas_call(
        body,
        name="adamw_" + tag,
        grid=(rows // tr,),
        in_specs=[pl.BlockSpec((tr, width), lambda i: (off + i, 0)), here, here, here],
        out_specs=[here] * 4,
        out_shape=[jax.ShapeDtypeStruct((rows, width), F32)] * 4,
        compiler_params=_params(("arbitrary",)),
    )(gsrc, w, m, v)


def kernel(x, mem, mix_pre_g, w_mix_in, conv_a_w, conv_b_w, conv_b_b, ln_b_g, ln_b_b, w_mix_out, mix_post_g, xa_pre_g, mem_norm_g, w_q, w_k, w_v, w_o, xa_post_g, ffn_pre_g, w_gate, w_up, w_down, ffn_post_g, loss_target, m_mix_pre_g, m_w_mix_in, m_conv_a_w, m_conv_b_w, m_conv_b_b, m_ln_b_g, m_ln_b_b, m_w_mix_out, m_mix_post_g, m_xa_pre_g, m_mem_norm_g, m_w_q, m_w_k, m_w_v, m_w_o, m_xa_post_g, m_ffn_pre_g, m_w_gate, m_w_up, m_w_down, m_ffn_post_g, v_mix_pre_g, v_w_mix_in, v_conv_a_w, v_conv_b_w, v_conv_b_b, v_ln_b_g, v_ln_b_b, v_w_mix_out, v_mix_post_g, v_xa_pre_g, v_mem_norm_g, v_w_q, v_w_k, v_w_v, v_w_o, v_xa_post_g, v_ffn_pre_g, v_w_gate, v_w_up, v_w_down, v_ffn_post_g):
    given = dict(locals())
    names = ["mix_pre_g", "w_mix_in", "conv_a_w", "conv_b_w", "conv_b_b", "ln_b_g", "ln_b_b", "w_mix_out",
             "mix_post_g", "xa_pre_g", "mem_norm_g", "w_q", "w_k", "w_v", "w_o", "xa_post_g", "ffn_pre_g",
             "w_gate", "w_up", "w_down", "ffn_post_g"]
    row = lambda a: a.reshape(1, -1)
    cx, cy, cc = lax.axis_index("x"), lax.axis_index("y"), lax.axis_index("c")
    chip = 2 * cx + cy
    ca_blk = conv_a_w.shape[1]

    conv_rows = CONV_A_W + CONV_B_W
    sw = jnp.concatenate([conv_a_w, conv_b_w, jnp.zeros((SMALL_W_ROWS - conv_rows, ca_blk), F32)], axis=0)
    g1g, pr0g, swg = _gather_weights([w_mix_in.astype(BF16), w_mix_out.astype(BF16), sw])
    conv_full = jnp.transpose(swg[:, :conv_rows, :], (1, 0, 2)).reshape(conv_rows, N_CHIPS * ca_blk)
    wa, wb = conv_full[:CONV_A_W], conv_full[CONV_A_W:]
    pr1 = jnp.concatenate([w_q, w_k, w_v, w_o, w_down], axis=0).astype(BF16)
    g2 = jnp.concatenate([w_gate, w_up], axis=0).astype(BF16)

    xs, ms, tgt = x[0], mem[0], loss_target[0]
    x1, u, o1, z1, pr1g, g2g = _mix_fwd(xs, row(mix_pre_g), row(mix_post_g), wa, wb, row(conv_b_b), row(ln_b_g),
                                        row(ln_b_b), g1g, pr0g, [pr1, g2])
    memn, kb, vb = _mem_kv(ms, row(mem_norm_g), pr1g)
    x2, q, o2 = _xattn_fwd(x1, row(xa_pre_g), row(xa_post_g), kb, vb, pr1g)
    h3, gs, us, do3, dx3, loss_part, d_ffn_post = _ffn_fwd(x2, tgt, row(ffn_pre_g), row(ffn_post_g), g2g, pr1g)

    dh3p, cf1, cf2 = _ffn_bwd(h3, do3, gs, us, g2g, pr1g)
    dx1, d_ffn_pre, d_xa_post, d_xa_pre, dk, dv, cxa, yf1, yf2 = _xattn_bwd(
        dx3, dh3p, x2, x1, o2, q, kb, vb, row(ffn_pre_g), row(xa_post_g), row(xa_pre_g), pr1g, cf1, cf2)
    d_mem_g, cxa = _mem_bwd(dk, dv, ms, memn, row(mem_norm_g), pr1g, cxa)
    sems_x, flying_x, token_x = _exchange_start([cxa], [], "attn")
    dx, d_mix_pre, d_mix_post, dwa, dwb, dbb, dlng, dlnb, cm1, cm2 = _mix_bwd(
        dx1, xs, o1, u, z1, row(mix_pre_g), row(mix_post_g), wa, wb, row(ln_b_g), row(ln_b_b), g1g, pr0g, token_x)
    (yx,) = _exchange_wait(sems_x, flying_x, 1, 0, [d_mix_pre], "attn")

    small_parts = [d_mix_pre, dwa, dwb, dbb, dlng, dlnb, d_mix_post, d_xa_pre, d_mem_g, d_xa_post, d_ffn_pre,
                   d_ffn_post, loss_part]
    sizes = [p.size for p in small_parts]
    small = jnp.concatenate([p.reshape(-1) for p in small_parts])
    small_rows = -(-small.size // (8 * 128)) * 8
    small = jnp.pad(small, (0, small_rows * 128 - small.size)).reshape(small_rows, 128)
    sems, flying, token = _exchange_start([cm1, cm2], [small], "mix")
    place = jnp.stack([chip, cc, 2 * chip + cc]).astype(jnp.int32)

    res = {}

    def update(nm, src, row0, tr, transposed=False):
        view = (lambda a: a.T) if transposed else (lambda a: a)
        outs = _adamw(src, row0, view(given[nm]), view(given["m_" + nm]), view(given["v_" + nm]), tr, nm)
        res[nm] = [view(o) for o in outs]

    early = [(yf1, cf1, "gate_up"), (yf2, cf2, "down"), (yx, cxa, "attn")]
    r_gu, r_down, r_attn = _pair_gather([_sum_peers(y, c, place, 2, t, after=(token,)) for y, c, t in early], "early")
    update("w_gate", r_gu, 0, FF_BLK // 2, transposed=True)
    update("w_up", r_gu, FF_BLK, FF_BLK // 2, transposed=True)
    update("w_down", r_down, 0, FF_BLK // 2)
    for nm, row0 in (("w_q", GX_Q), ("w_k", GX_K), ("w_v", GX_V), ("w_o", GX_O)):
        update(nm, r_attn, row0, ROW_BLK)
    done = [res[nm][1] for nm in ("w_gate", "w_up", "w_down", "w_q", "w_k", "w_v", "w_o")]

    ym1, ym2, small_all = _exchange_wait(sems, flying, 2, 1, done, "mix")
    r_in, r_out = _pair_gather([_sum_peers(y, c, place, 2, t) for y, c, t in
                                [(ym1, cm1, "mix_in"), (ym2, cm2, "mix_out")]], "late")
    update("w_mix_in", r_in, 0, 512)
    update("w_mix_out", r_out, 0, ROW_BLK)
    small_sum = _sum_peers(small_all, small, place, 1, "small").reshape(-1)
    red, pos = [], 0
    for p, sz in zip(small_parts, sizes):
        red.append(small_sum[pos:pos + sz].reshape(p.shape))
        pos += sz
    (r_mix_pre, r_wa, r_wb, r_bb, r_lng, r_lnb, r_mix_post, r_xa_pre, r_mem_g, r_xa_post, r_ffn_pre, r_ffn_post,
     r_loss) = red
    loss = r_loss[0, 0]

    small_grads = {"mix_pre_g": r_mix_pre, "conv_b_b": r_bb, "ln_b_g": r_lng, "ln_b_b": r_lnb,
                   "mix_post_g": r_mix_post, "xa_pre_g": r_xa_pre, "mem_norm_g": r_mem_g, "xa_post_g": r_xa_post,
                   "ffn_pre_g": r_ffn_pre, "ffn_post_g": r_ffn_post,
                   "conv_a_w": lax.dynamic_slice_in_dim(r_wa, chip * ca_blk, ca_blk, axis=1),
                   "conv_b_w": lax.dynamic_slice_in_dim(r_wb, chip * ca_blk, ca_blk, axis=1)}
    small_names = list(small_grads)

    def packed(prefix, grads=None):
        flat = jnp.concatenate([(grads[nm] if grads else given[prefix + nm]).reshape(-1) for nm in small_names])
        rows8 = -(-flat.size // (8 * 128)) * 8
        return jnp.pad(flat, (0, rows8 * 128 - flat.size)).reshape(rows8, 128)

    gp = packed("", small_grads)
    outs = _adamw(gp, 0, packed(""), packed("m_"), packed("v_"), gp.shape[0], "small")
    pos = 0
    for nm in small_names:
        shape = given[nm].shape
        sz = given[nm].size
        res[nm] = [o.reshape(-1)[pos:pos + sz].reshape(shape) for o in outs]
        pos += sz

    return (loss, dx[None], *[res[nm][0] for nm in names], *[res[nm][1] for nm in names],
            *[res[nm][2] for nm in names], *[res[nm][3] for nm in names])
```

```python
import jax
import jax.numpy as jnp
from jax import lax
from jax.experimental import pallas as pl
from jax.experimental.pallas import tpu as pltpu

F32 = jnp.float32
BF16 = jnp.bfloat16
MESH = pl.DeviceIdType.MESH

RMS_EPS = 1e-6
LN_EPS = 1e-5
D_MODEL = 1024
D_A = 512
D_B = 512
D_IN_ALL = 3 * D_A + 2 * D_B
CONV_A_W = 3
CONV_B_W = 31
HALO = 32
XA_HEADS = 4
HEAD_DIM = 256
D_FF = 2816
N_CHIPS = 4
N_DEV = 8
FF_BLK = D_FF // N_CHIPS
IN_BLK = D_IN_ALL // N_CHIPS
ROW_BLK = D_MODEL // N_CHIPS

ADAM_LR = 0.001
ADAM_B1 = 0.9
ADAM_B2 = 0.999
ADAM_EPS = 1e-08
ADAM_WD = 0.01
ADAM_STEP = 10

TILE_FWD = 512
TILE_XATTN_FWD = 1024
TILE_FFN = 512
TILE_BWD = 256
FFN_BWD_BLOCKS = 2
CONV_ROWS_FWD = 64
CONV_ROWS = 32
V7X_VMEM_LIMIT = 56 * 1024 * 1024

P1_Q, P1_K, P1_V, P1_O, P1_DOWN = 0, 256, 512, 768, 1024
P1_ROWS = P1_DOWN + FF_BLK
GX_Q, GX_K, GX_V, GX_O = 0, 256, 512, 768
SMALL_W_ROWS = 48

ANY = pl.BlockSpec(memory_space=pl.ANY)


def _mm(a, b):
    return lax.dot_general(a, b, (((1,), (0,)), ((), ())), preferred_element_type=F32)


def _mm_nt(a, b):
    return lax.dot_general(a, b, (((1,), (1,)), ((), ())), preferred_element_type=F32)


def _mm_tn(a, b):
    return lax.dot_general(a, b, (((0,), (0,)), ((), ())), preferred_element_type=F32)


def _sigmoid(x):
    return 0.5 * jnp.tanh(0.5 * x) + 0.5


def _rms(x):
    r = lax.rsqrt(jnp.mean(x * x, axis=-1, keepdims=True) + RMS_EPS)
    return x * r, r


def _rms_bwd(dy, xn, r, g):
    gdy = dy * g
    dx = r * (gdy - xn * jnp.mean(gdy * xn, axis=-1, keepdims=True))
    return dx, jnp.sum(dy * xn, axis=0, keepdims=True)


def _fold8(a):
    out = a[0:8, :]
    for m in range(1, a.shape[0] // 8):
        out = out + a[8 * m:8 * m + 8, :]
    return out


def _full(shape):
    return pl.BlockSpec(shape, lambda *_: (0,) * len(shape))


def _params(sem=None):
    return pltpu.CompilerParams(dimension_semantics=sem, vmem_limit_bytes=V7X_VMEM_LIMIT)


def _load_rows(dst, src_hbm, row0, rows):
    for k in range(N_CHIPS):
        pltpu.sync_copy(src_hbm.at[k, pl.ds(row0, rows), :], dst.at[pl.ds(rows * k, rows), :])


def _load_cols(dst, src_hbm, cols):
    for k in range(N_CHIPS):
        pltpu.sync_copy(src_hbm.at[k], dst.at[:, pl.ds(cols * k, cols)])


def _fill_phases(src, sh, nrows):
    for r in range(1, 8):
        sh[r, pl.ds(0, nrows), :] = src[pl.ds(r, nrows), pl.ds(D_A, D_B)]


def _phase_rows(src, sh, off, start, size):
    r = off % 8
    if r == 0:
        return src[pl.ds(off + start, size), pl.ds(D_A, D_B)]
    return sh[r, pl.ds(off - r + start, size), :]


def _my_place():
    x, y, c = lax.axis_index("x"), lax.axis_index("y"), lax.axis_index("c")
    return x, y, c, ((1 - x, y), (x, 1 - y), (1 - x, 1 - y))


def _remote(src, dst, send_sem, recv_sem, to):
    return pltpu.make_async_remote_copy(src_ref=src, dst_ref=dst, send_sem=send_sem, recv_sem=recv_sem,
                                        device_id=to, device_id_type=MESH)


def _gather_sems(np_):
    return [pltpu.SemaphoreType.DMA((np_, 3))] * 4 + [pltpu.SemaphoreType.DMA((np_,))] * 2


def _gather_start(srcs, dsts, sems):
    send, recv, _, _, osend, orecv = sems
    x, y, c, chips = _my_place()
    j = 2 * x + y
    for p in range(len(srcs)):
        _remote(srcs[p], dsts[p].at[j], osend.at[p], orecv.at[p], (x, y, 1 - c)).start()
        for nn, (kx, ky) in enumerate(chips):
            _remote(srcs[p].at[c], dsts[p].at[j, c], send.at[p, nn], recv.at[p, nn], (kx, ky, c)).start()


def _gather_forward(srcs, dsts, sems):
    send, recv, fsend, frecv, _, _ = sems
    x, y, c, chips = _my_place()
    for nn, (kx, ky) in enumerate(chips):
        for p in range(len(srcs)):
            blk = dsts[p].at[2 * kx + ky, c]
            _remote(blk, blk, send.at[p, nn], recv.at[p, nn], (kx, ky, c)).wait_recv()
            _remote(blk, blk, fsend.at[p, nn], frecv.at[p, nn], (x, y, 1 - c)).start()


def _gather_finish(srcs, dsts, sems):
    send, recv, fsend, frecv, osend, orecv = sems
    x, y, c, chips = _my_place()
    j = 2 * x + y
    for nn, (kx, ky) in enumerate(chips):
        for p in range(len(srcs)):
            other = dsts[p].at[2 * kx + ky, 1 - c]
            _remote(other, other, fsend.at[p, nn], frecv.at[p, nn], (x, y, 1 - c)).wait_recv()
    for nn, (kx, ky) in enumerate(chips):
        for p in range(len(srcs)):
            _remote(srcs[p].at[c], dsts[p].at[j, c], send.at[p, nn], recv.at[p, nn], (kx, ky, c)).wait_send()
            blk = dsts[p].at[2 * kx + ky, c]
            _remote(blk, blk, fsend.at[p, nn], frecv.at[p, nn], (x, y, 1 - c)).wait_send()
    for p in range(len(srcs)):
        _remote(srcs[p], dsts[p].at[j], osend.at[p], orecv.at[p], (x, y, 1 - c)).wait()


def _split_halves(a):
    return a.reshape(2, a.shape[0] // 2, a.shape[1])


def _gather_weights(packs):
    np_ = len(packs)
    split = [_split_halves(a) for a in packs]

    def body(*refs):
        srcs, dsts, sems = refs[:np_], refs[np_:2 * np_], refs[2 * np_:]
        _gather_start(srcs, dsts, sems)
        _gather_forward(srcs, dsts, sems)
        _gather_finish(srcs, dsts, sems)

    outs = pl.pallas_call(
        body,
        name="gather_weights",
        in_specs=[ANY] * np_,
        out_specs=[ANY] * np_,
        out_shape=[jax.ShapeDtypeStruct((N_CHIPS,) + a.shape, a.dtype) for a in split],
        scratch_shapes=_gather_sems(np_),
    )(*split)
    return [o.reshape((N_CHIPS,) + a.shape) for o, a in zip(outs, packs)]


def _peers(x, y, c):
    out = []
    for d in range(1, N_DEV):
        px = 1 - x if d & 4 else x
        py = 1 - y if d & 2 else y
        pc = 1 - c if d & 1 else c
        out.append(((px, py, pc), 4 * px + 2 * py + pc))
    return out


def _scatter_copies(srcs, dsts, send, recv):
    x, y, c, _ = _my_place()
    me = 4 * x + 2 * y + c
    out = []
    for a in range(len(srcs)):
        for d, ((px, py, pc), pidx) in enumerate(_peers(x, y, c)):
            piece = srcs[a].at[2 * px + py, pc]
            out.append((_remote(piece, dsts[a].at[me], send.at[a, d], recv.at[a, d], (px, py, pc)),
                        _remote(piece, dsts[a].at[pidx], send.at[a, d], recv.at[a, d], (px, py, pc))))
    return out


def _scatter_start(srcs, dsts, send, recv):
    for out_cp, _ in _scatter_copies(srcs, dsts, send, recv):
        out_cp.start()


def _scatter_wait(srcs, dsts, send, recv):
    for out_cp, in_cp in _scatter_copies(srcs, dsts, send, recv):
        in_cp.wait_recv()
        out_cp.wait_send()


def _scatter_sems(na):
    return [pltpu.SemaphoreType.DMA((na, N_DEV - 1))] * 2


def _slots(a):
    return jax.ShapeDtypeStruct((N_DEV,) + a.shape[2:], a.dtype)


def _by_halves(a):
    return a.reshape(a.shape[0], 2, a.shape[1] // 2, a.shape[2])


def _mix_fwd(x, gpre, gpost, wa, wb, bb, lng, lnb, g1g, pr0g, late):
    S, D = x.shape
    T = min(TILE_FWD, S)
    n = S // T
    nl = len(late)
    late_split = [_split_halves(a) for a in late]

    def body(*refs):
        (x_ref, gpre_ref, gpost_ref, wa_ref, wb_ref, bb_ref, lng_ref, lnb_ref, g1g_hbm, pr0g_hbm) = refs[:10]
        srcs = refs[10:10 + nl]
        x1_ref, u_ref, o1_ref, z1_ref, rx_ref, ro_ref = refs[10 + nl:16 + nl]
        dsts = refs[16 + nl:16 + 2 * nl]
        win_v, wout_v, ext, sh, z1buf = refs[16 + 2 * nl:21 + 2 * nl]
        sems = refs[21 + 2 * nl:]
        i = pl.program_id(0)

        @pl.when(i == 0)
        def _():
            _gather_start(srcs, dsts, sems)
            _load_cols(win_v, g1g_hbm, IN_BLK)
            _load_rows(wout_v, pr0g_hbm, 0, ROW_BLK)
            ext[pl.ds(0, HALO), :] = jnp.zeros((HALO, D_A + D_B), F32)

        @pl.when(i == max(n - 2, 0))
        def _():
            _gather_forward(srcs, dsts, sems)

        xv = x_ref[...]
        xn, rx = _rms(xv)
        rx_ref[...] = rx
        h = (xn * gpre_ref[...]).astype(BF16)
        u = _mm(h, win_v[...])
        u_ref[...] = u.astype(BF16)
        b_a = u[:, 0:D_A]
        cv = u[:, D_A:2 * D_A] * u[:, 2 * D_A:3 * D_A]
        z0 = u[:, 3 * D_A:3 * D_A + D_B] * _sigmoid(u[:, 3 * D_A + D_B:])
        ext[pl.ds(HALO, T), pl.ds(0, D_A)] = cv
        ext[pl.ds(HALO, T), pl.ds(D_A, D_B)] = z0

        conv_a = ext[pl.ds(HALO - 2, T), pl.ds(0, D_A)] * wa_ref[0:1, :]
        for k in range(1, CONV_A_W):
            conv_a = conv_a + ext[pl.ds(HALO - 2 + k, T), pl.ds(0, D_A)] * wa_ref[k:k + 1, :]
        y_a = b_a * conv_a

        _fill_phases(ext, sh, T + HALO - 8)
        base = HALO - (CONV_B_W - 1)

        def chunk(ci, carry):
            start = pl.multiple_of(ci * CONV_ROWS_FWD, 8)
            acc = jnp.broadcast_to(bb_ref[...], (CONV_ROWS_FWD, D_B))
            for k in range(CONV_B_W):
                acc = acc + _phase_rows(ext, sh, base + k, start, CONV_ROWS_FWD) * wb_ref[k:k + 1, :]
            z1buf[pl.ds(start, CONV_ROWS_FWD), :] = acc
            return carry

        lax.fori_loop(0, T // CONV_ROWS_FWD, chunk, 0)
        z1 = z1buf[...]
        z1_ref[...] = z1.astype(BF16)
        mu = jnp.mean(z1, axis=-1, keepdims=True)
        zc = z1 - mu
        rstd = lax.rsqrt(jnp.mean(zc * zc, axis=-1, keepdims=True) + LN_EPS)
        l = zc * rstd * lng_ref[...] + lnb_ref[...]
        y_b = l * _sigmoid(l)
        y = jnp.concatenate([y_a, y_b], axis=-1).astype(BF16)
        o1 = _mm(y, wout_v[...])
        o1_ref[...] = o1.astype(BF16)
        o1n, ro = _rms(o1)
        ro_ref[...] = ro
        x1_ref[...] = xv + o1n * gpost_ref[...]
        ext[pl.ds(0, HALO), :] = ext[pl.ds(T, HALO), :]

        @pl.when(i == n - 1)
        def _():
            _gather_finish(srcs, dsts, sems)

    tok = lambda w: pl.BlockSpec((T, w), lambda i: (i, 0))
    outs = pl.pallas_call(
        body,
        name="mix_fwd",
        grid=(n,),
        in_specs=[tok(D), _full((1, D)), _full((1, D)), _full((CONV_A_W, D_A)), _full((CONV_B_W, D_B)),
                  _full((1, D_B)), _full((1, D_B)), _full((1, D_B)), ANY, ANY] + [ANY] * nl,
        out_specs=[tok(D), tok(D_IN_ALL), tok(D), tok(D_B), tok(1), tok(1)] + [ANY] * nl,
        out_shape=[jax.ShapeDtypeStruct((S, D), F32), jax.ShapeDtypeStruct((S, D_IN_ALL), BF16),
                   jax.ShapeDtypeStruct((S, D), BF16), jax.ShapeDtypeStruct((S, D_B), BF16),
                   jax.ShapeDtypeStruct((S, 1), F32), jax.ShapeDtypeStruct((S, 1), F32)]
        + [jax.ShapeDtypeStruct((N_CHIPS,) + a.shape, a.dtype) for a in late_split],
        scratch_shapes=[pltpu.VMEM((D, D_IN_ALL), BF16), pltpu.VMEM((D_A + D_B, D), BF16),
                        pltpu.VMEM((HALO + T, D_A + D_B), F32), pltpu.VMEM((8, HALO + T, D_B), F32),
                        pltpu.VMEM((T, D_B), F32)] + _gather_sems(nl),
        compiler_params=_params(("arbitrary",)),
    )(x, gpre, gpost, wa, wb, bb, lng, lnb, g1g, pr0g, *late_split)
    return list(outs[:6]) + [o.reshape((N_CHIPS,) + a.shape) for o, a in zip(outs[6:], late)]


def _mix_bwd(dx1, x, o1, u, z1s, ro1, rx, gpre, gpost, wa, wb, lng, lnb, g1g, pr0g, after):
    S, D = x.shape
    T = min(TILE_BWD, S)
    n = S // T
    hb = T // HALO

    def body(dx1_ref, x_ref, o1_ref, u_ref, uh_ref, z1_ref, ro_ref, rx_ref, gpre_ref, gpost_ref, wa_ref, wb_ref,
             lng_ref, lnb_ref, g1g_hbm, pr0g_hbm, _,
             dx_ref, dgpre_ref, dgpost_ref, dwa_ref, dwb_ref, dbb_ref, dlng_ref, dlnb_ref, cm1_hbm, cm2_hbm,
             win_v, wout_v, dwin_acc, dwout_acc, ext, ext2, shb, dz0buf, dwb_acc):
        i = pl.program_id(0)

        @pl.when(i == 0)
        def _():
            _load_cols(win_v, g1g_hbm, IN_BLK)
            _load_rows(wout_v, pr0g_hbm, 0, ROW_BLK)
            dwin_acc[...] = jnp.zeros_like(dwin_acc)
            dwout_acc[...] = jnp.zeros_like(dwout_acc)
            dwb_acc[...] = jnp.zeros_like(dwb_acc)
            ext2[pl.ds(T, HALO), :] = jnp.zeros((HALO, D_A + D_B), F32)
            for ref in (dgpre_ref, dgpost_ref, dwa_ref, dbb_ref, dlng_ref, dlnb_ref):
                ref[...] = jnp.zeros_like(ref)

        r1 = ro_ref[...]
        o1n = o1_ref[...].astype(F32) * r1
        dx1v = dx1_ref[...]
        d_o1, dgp = _rms_bwd(dx1v, o1n, r1, gpost_ref[...])
        dgpost_ref[...] += dgp
        d_o1b = d_o1.astype(BF16)
        dy = _mm_nt(d_o1b, wout_v[...])

        first = (i == n - 1).astype(F32)
        uh = uh_ref[...].astype(F32) * (1.0 - first)
        ext[pl.ds(0, HALO), pl.ds(0, D_A)] = uh[:, D_A:2 * D_A] * uh[:, 2 * D_A:3 * D_A]
        uf = u_ref[...].astype(F32)
        b_a = uf[:, 0:D_A]
        c_a = uf[:, D_A:2 * D_A]
        v_a = uf[:, 2 * D_A:3 * D_A]
        gv = uf[:, 3 * D_A:3 * D_A + D_B]
        sg = _sigmoid(uf[:, 3 * D_A + D_B:])
        ext[pl.ds(HALO, T), pl.ds(0, D_A)] = c_a * v_a
        ext[pl.ds(HALO, T), pl.ds(D_A, D_B)] = gv * sg
        conv_a = ext[pl.ds(HALO - 2, T), pl.ds(0, D_A)] * wa_ref[0:1, :]
        for k in range(1, CONV_A_W):
            conv_a = conv_a + ext[pl.ds(HALO - 2 + k, T), pl.ds(0, D_A)] * wa_ref[k:k + 1, :]
        z1 = z1_ref[...].astype(F32)
        mu = jnp.mean(z1, axis=-1, keepdims=True)
        zc = z1 - mu
        rstd = lax.rsqrt(jnp.mean(zc * zc, axis=-1, keepdims=True) + LN_EPS)
        zn = zc * rstd
        l = zn * lng_ref[...] + lnb_ref[...]
        sl = _sigmoid(l)
        y = jnp.concatenate([b_a * conv_a, l * sl], axis=-1).astype(BF16)
        dwout_acc[...] += _mm_tn(y, d_o1b)

        dy_a = dy[:, 0:D_A]
        dl = dy[:, D_A:] * (sl * (1.0 + l * (1.0 - sl)))
        dlng_ref[...] += jnp.sum(dl * zn, axis=0, keepdims=True)
        dlnb_ref[...] += jnp.sum(dl, axis=0, keepdims=True)
        dzn = dl * lng_ref[...]
        dz1 = rstd * (dzn - jnp.mean(dzn, axis=-1, keepdims=True) - zn * jnp.mean(dzn * zn, axis=-1, keepdims=True))
        dbb_ref[...] += jnp.sum(dz1, axis=0, keepdims=True)
        d_conv = dy_a * b_a
        ext2[pl.ds(0, T), pl.ds(0, D_A)] = d_conv
        ext2[pl.ds(0, T), pl.ds(D_A, D_B)] = dz1

        d_cv = ext2[pl.ds(CONV_A_W - 1, T), pl.ds(0, D_A)] * wa_ref[0:1, :]
        for k in range(1, CONV_A_W):
            d_cv = d_cv + ext2[pl.ds(CONV_A_W - 1 - k, T), pl.ds(0, D_A)] * wa_ref[k:k + 1, :]
        for k in range(CONV_A_W):
            dwa_ref[k:k + 1, :] += jnp.sum(d_conv * ext[pl.ds(HALO - 2 + k, T), pl.ds(0, D_A)], axis=0, keepdims=True)

        _fill_phases(ext2, shb, T + HALO - 8)

        def chunk(ci, carry):
            start = pl.multiple_of(ci * CONV_ROWS, 8)
            z0c = ext[pl.ds(HALO + start, CONV_ROWS), pl.ds(D_A, D_B)]
            acc = jnp.zeros((CONV_ROWS, D_B), F32)
            for k in range(CONV_B_W):
                ahead = _phase_rows(ext2, shb, CONV_B_W - 1 - k, start, CONV_ROWS)
                acc = acc + ahead * wb_ref[k:k + 1, :]
                dwb_acc[k] += _fold8(z0c * ahead)
            dz0buf[pl.ds(start, CONV_ROWS), :] = acc
            return carry

        lax.fori_loop(0, T // CONV_ROWS, chunk, 0)
        dz0 = dz0buf[...]
        du = jnp.concatenate([dy_a * conv_a, d_cv * v_a, d_cv * c_a, dz0 * sg, dz0 * gv * sg * (1.0 - sg)],
                             axis=-1).astype(BF16)
        dh = _mm_nt(du, win_v[...])
        r0 = rx_ref[...]
        xn = x_ref[...] * r0
        dwin_acc[...] += _mm_tn((xn * gpre_ref[...]).astype(BF16), du)
        dxp, dg0 = _rms_bwd(dh, xn, r0, gpre_ref[...])
        dgpre_ref[...] += dg0
        dx_ref[...] = dx1v + dxp
        ext2[pl.ds(T, HALO), :] = ext2[pl.ds(0, HALO), :]

        @pl.when(i == n - 1)
        def _():
            for k in range(CONV_B_W):
                dwb_ref[k:k + 1, :] = jnp.sum(dwb_acc[k], axis=0, keepdims=True)
            win_v[...] = dwin_acc[...].astype(BF16)
            wout_v[...] = dwout_acc[...].astype(BF16)
            for k in range(N_CHIPS):
                pltpu.sync_copy(win_v.at[:, pl.ds(IN_BLK * k, IN_BLK)], cm1_hbm.at[k])
                pltpu.sync_copy(wout_v.at[pl.ds(ROW_BLK * k, ROW_BLK), :], cm2_hbm.at[k])

    rev = lambda w: pl.BlockSpec((T, w), lambda i: (n - 1 - i, 0))
    halo = pl.BlockSpec((HALO, D_IN_ALL), lambda i: (jnp.maximum((n - 1 - i) * hb - 1, 0), 0))
    return pl.pallas_call(
        body,
        name="mix_bwd",
        grid=(n,),
        in_specs=[rev(D), rev(D), rev(D), rev(D_IN_ALL), halo, rev(D_B), rev(1), rev(1), _full((1, D)),
                  _full((1, D)), _full((CONV_A_W, D_A)), _full((CONV_B_W, D_B)), _full((1, D_B)), _full((1, D_B)),
                  ANY, ANY, ANY],
        out_specs=[rev(D), _full((1, D)), _full((1, D)), _full((CONV_A_W, D_A)), _full((CONV_B_W, D_B)),
                   _full((1, D_B)), _full((1, D_B)), _full((1, D_B)), ANY, ANY],
        out_shape=[jax.ShapeDtypeStruct((S, D), F32), jax.ShapeDtypeStruct((1, D), F32),
                   jax.ShapeDtypeStruct((1, D), F32), jax.ShapeDtypeStruct((CONV_A_W, D_A), F32),
                   jax.ShapeDtypeStruct((CONV_B_W, D_B), F32), jax.ShapeDtypeStruct((1, D_B), F32),
                   jax.ShapeDtypeStruct((1, D_B), F32), jax.ShapeDtypeStruct((1, D_B), F32),
                   jax.ShapeDtypeStruct((N_CHIPS, D, IN_BLK), BF16),
                   jax.ShapeDtypeStruct((N_CHIPS, ROW_BLK, D), BF16)],
        scratch_shapes=[pltpu.VMEM((D, D_IN_ALL), BF16), pltpu.VMEM((D_A + D_B, D), BF16),
                        pltpu.VMEM((D, D_IN_ALL), F32), pltpu.VMEM((D_A + D_B, D), F32),
                        pltpu.VMEM((HALO + T, D_A + D_B), F32), pltpu.VMEM((HALO + T, D_A + D_B), F32),
                        pltpu.VMEM((8, HALO + T, D_B), F32),
                        pltpu.VMEM((T, D_B), F32), pltpu.VMEM((CONV_B_W, 8, D_B), F32)],
        compiler_params=_params(("arbitrary",)),
    )(dx1, x, o1, u, u, z1s, ro1, rx, gpre, gpost, wa, wb, lng, lnb, g1g, pr0g, after)


def _mem_kv(mem, gmem, pr1g):
    M, D = mem.shape

    def body(mem_ref, g_ref, pr1g_hbm, memn_ref, k_ref, v_ref, wk_v, wv_v):
        _load_rows(wk_v, pr1g_hbm, P1_K, ROW_BLK)
        _load_rows(wv_v, pr1g_hbm, P1_V, ROW_BLK)
        mn, _ = _rms(mem_ref[...])
        mb = (mn * g_ref[...]).astype(BF16)
        memn_ref[...] = mb
        k_ref[...] = _mm(mb, wk_v[...]).astype(BF16)
        v_ref[...] = _mm(mb, wv_v[...]).astype(BF16)

    return pl.pallas_call(
        body,
        name="mem_kv",
        grid=(1,),
        in_specs=[_full((M, D)), _full((1, D)), ANY],
        out_specs=[_full((M, D))] * 3,
        out_shape=[jax.ShapeDtypeStruct((M, D), BF16)] * 3,
        scratch_shapes=[pltpu.VMEM((D, D), BF16), pltpu.VMEM((D, D), BF16)],
        compiler_params=_params(("arbitrary",)),
    )(mem, gmem, pr1g)


def _attend(qb, kb, vb):
    scale = HEAD_DIM ** -0.5
    ps, os_ = [], []
    for hd in range(XA_HEADS):
        cols = slice(HEAD_DIM * hd, HEAD_DIM * (hd + 1))
        s = _mm_nt(qb[:, cols], kb[:, cols]) * scale
        e = jnp.exp(s - jnp.max(s, axis=-1, keepdims=True))
        p = e * (1.0 / jnp.sum(e, axis=-1, keepdims=True))
        ps.append(p)
        os_.append(_mm(p.astype(BF16), vb[:, cols]))
    return ps, jnp.concatenate(os_, axis=-1).astype(BF16)


def _xattn_fwd(x1, gpre, gpost, kb, vb, pr1g):
    S, D = x1.shape
    M = kb.shape[0]
    T = min(TILE_XATTN_FWD, S)
    n = S // T

    def body(x1_ref, gpre_ref, gpost_ref, k_ref, v_ref, pr1g_hbm, x2_ref, q_ref, o2_ref, rx_ref, ro_ref, wq_v, wo_v):
        @pl.when(pl.program_id(0) == 0)
        def _():
            _load_rows(wq_v, pr1g_hbm, P1_Q, ROW_BLK)
            _load_rows(wo_v, pr1g_hbm, P1_O, ROW_BLK)

        xv = x1_ref[...]
        xn, rx = _rms(xv)
        rx_ref[...] = rx
        qb = _mm((xn * gpre_ref[...]).astype(BF16), wq_v[...]).astype(BF16)
        q_ref[...] = qb
        _, ob = _attend(qb, k_ref[...], v_ref[...])
        o2 = _mm(ob, wo_v[...])
        o2_ref[...] = o2.astype(BF16)
        o2n, ro = _rms(o2)
        ro_ref[...] = ro
        x2_ref[...] = xv + o2n * gpost_ref[...]

    tok = lambda w: pl.BlockSpec((T, w), lambda i: (i, 0))
    return pl.pallas_call(
        body,
        name="xattn_fwd",
        grid=(n,),
        in_specs=[tok(D), _full((1, D)), _full((1, D)), _full((M, D)), _full((M, D)), ANY],
        out_specs=[tok(D), tok(D), tok(D), tok(1), tok(1)],
        out_shape=[jax.ShapeDtypeStruct((S, D), F32), jax.ShapeDtypeStruct((S, D), BF16),
                   jax.ShapeDtypeStruct((S, D), BF16), jax.ShapeDtypeStruct((S, 1), F32),
                   jax.ShapeDtypeStruct((S, 1), F32)],
        scratch_shapes=[pltpu.VMEM((D, D), BF16), pltpu.VMEM((D, D), BF16)],
        compiler_params=_params(("arbitrary",)),
    )(x1, gpre, gpost, kb, vb, pr1g)


def _xattn_bwd(dx3, dh3p, x2, x1, o2, q, rx2, ro2, rx1, kb, vb, gffn, gpost, gpre, pr1g, cf1, cf2):
    S, D = x1.shape
    M = kb.shape[0]
    T = min(TILE_BWD, S)
    n = S // T
    scale = HEAD_DIM ** -0.5
    nparts = dh3p.shape[0]
    cfs = [_by_halves(cf1), _by_halves(cf2)]

    def body(*refs):
        dx3_ref, dh3_refs = refs[0], refs[1:1 + nparts]
        (x2_ref, x1_ref, o2_ref, q_ref, rx2_ref, ro2_ref, rx1_ref, k_ref, v_ref, gffn_ref, gpost_ref, gpre_ref,
         pr1g_hbm, cf1_hbm, cf2_hbm,
         dx1_ref, dgffn_ref, dgpost_ref, dgpre_ref, dk_ref, dv_ref, cx_hbm, yf1_hbm, yf2_hbm,
         wq_v, wo_v, dwq_acc, dwo_acc, send, recv) = refs[1 + nparts:]
        i = pl.program_id(0)

        @pl.when(i == 0)
        def _():
            _scatter_start([cf1_hbm, cf2_hbm], [yf1_hbm, yf2_hbm], send, recv)
            _load_rows(wq_v, pr1g_hbm, P1_Q, ROW_BLK)
            _load_rows(wo_v, pr1g_hbm, P1_O, ROW_BLK)
            dwq_acc[...] = jnp.zeros_like(dwq_acc)
            dwo_acc[...] = jnp.zeros_like(dwo_acc)
            for ref in (dgffn_ref, dgpost_ref, dgpre_ref, dk_ref, dv_ref):
                ref[...] = jnp.zeros_like(ref)

        r2 = rx2_ref[...]
        x2n = x2_ref[...] * r2
        dh3 = dh3_refs[0][...].astype(F32)
        for ref in dh3_refs[1:]:
            dh3 = dh3 + ref[...].astype(F32)
        dxp, dg = _rms_bwd(dh3, x2n, r2, gffn_ref[...])
        dgffn_ref[...] += dg
        dx2 = dx3_ref[...] + dxp
        ro = ro2_ref[...]
        o2n = o2_ref[...].astype(F32) * ro
        d_o2, dg = _rms_bwd(dx2, o2n, ro, gpost_ref[...])
        dgpost_ref[...] += dg
        d_o2b = d_o2.astype(BF16)
        d_o = _mm_nt(d_o2b, wo_v[...]).astype(BF16)
        qb = q_ref[...]
        kv = k_ref[...]
        vv = v_ref[...]
        ps, ob = _attend(qb, kv, vv)
        dwo_acc[...] += _mm_tn(ob, d_o2b)
        dqs = []
        for hd in range(XA_HEADS):
            cols = slice(HEAD_DIM * hd, HEAD_DIM * (hd + 1))
            p = ps[hd]
            dp = _mm_nt(d_o[:, cols], vv[:, cols])
            dv_ref[:, cols] += _mm_tn(p.astype(BF16), d_o[:, cols])
            ds = (p * (dp - jnp.sum(p * dp, axis=-1, keepdims=True)) * scale).astype(BF16)
            dqs.append(_mm(ds, kv[:, cols]))
            dk_ref[:, cols] += _mm_tn(ds, qb[:, cols])
        dq = jnp.concatenate(dqs, axis=-1).astype(BF16)
        dh2 = _mm_nt(dq, wq_v[...])
        r1 = rx1_ref[...]
        x1n = x1_ref[...] * r1
        dwq_acc[...] += _mm_tn((x1n * gpre_ref[...]).astype(BF16), dq)
        dxp, dg = _rms_bwd(dh2, x1n, r1, gpre_ref[...])
        dgpre_ref[...] += dg
        dx1_ref[...] = dx2 + dxp

        @pl.when(i == n - 1)
        def _():
            wq_v[...] = dwq_acc[...].astype(BF16)
            wo_v[...] = dwo_acc[...].astype(BF16)
            for k in range(N_CHIPS):
                rows = pl.ds(ROW_BLK * k, ROW_BLK)
                pltpu.sync_copy(wq_v.at[rows, :], cx_hbm.at[k, pl.ds(GX_Q, ROW_BLK), :])
                pltpu.sync_copy(wo_v.at[rows, :], cx_hbm.at[k, pl.ds(GX_O, ROW_BLK), :])
            _scatter_wait([cf1_hbm, cf2_hbm], [yf1_hbm, yf2_hbm], send, recv)

    tok = lambda w: pl.BlockSpec((T, w), lambda i: (i, 0))
    part = lambda j: pl.BlockSpec((None, T, D), lambda i: (j, i, 0))
    return pl.pallas_call(
        body,
        name="xattn_bwd",
        grid=(n,),
        in_specs=[tok(D)] + [part(j) for j in range(nparts)] + [tok(D), tok(D), tok(D), tok(D), tok(1), tok(1), tok(1),
                                                                 _full((M, D)), _full((M, D)), _full((1, D)),
                                                                 _full((1, D)), _full((1, D)), ANY, ANY, ANY],
        out_specs=[tok(D), _full((1, D)), _full((1, D)), _full((1, D)), _full((M, D)), _full((M, D)), ANY, ANY, ANY],
        out_shape=[jax.ShapeDtypeStruct((S, D), F32), jax.ShapeDtypeStruct((1, D), F32),
                   jax.ShapeDtypeStruct((1, D), F32), jax.ShapeDtypeStruct((1, D), F32),
                   jax.ShapeDtypeStruct((M, D), F32), jax.ShapeDtypeStruct((M, D), F32),
                   jax.ShapeDtypeStruct((N_CHIPS, D, D), BF16), _slots(cfs[0]), _slots(cfs[1])],
        scratch_shapes=[pltpu.VMEM((D, D), BF16), pltpu.VMEM((D, D), BF16),
                        pltpu.VMEM((D, D), F32), pltpu.VMEM((D, D), F32)] + _scatter_sems(2),
        compiler_params=_params(("arbitrary",)),
    )(dx3, *([dh3p] * nparts), x2, x1, o2, q, rx2, ro2, rx1, kb, vb, gffn, gpost, gpre, pr1g, *cfs)


def _mem_bwd(dk, dv, mem, memn, gmem, pr1g, cx_in):
    M, D = mem.shape

    def body(dk_ref, dv_ref, mem_ref, memn_ref, g_ref, pr1g_hbm, cx_hbm, dg_ref, cx_out, wk_v, wv_v):
        del cx_hbm
        _load_rows(wk_v, pr1g_hbm, P1_K, ROW_BLK)
        _load_rows(wv_v, pr1g_hbm, P1_V, ROW_BLK)
        dkb = dk_ref[...].astype(BF16)
        dvb = dv_ref[...].astype(BF16)
        mb = memn_ref[...]
        dmn = _mm_nt(dkb, wk_v[...]) + _mm_nt(dvb, wv_v[...])
        mn, _ = _rms(mem_ref[...])
        dg_ref[...] = jnp.sum(dmn * mn, axis=0, keepdims=True)
        wk_v[...] = _mm_tn(mb, dkb).astype(BF16)
        wv_v[...] = _mm_tn(mb, dvb).astype(BF16)
        for k in range(N_CHIPS):
            rows = pl.ds(ROW_BLK * k, ROW_BLK)
            pltpu.sync_copy(wk_v.at[rows, :], cx_out.at[k, pl.ds(GX_K, ROW_BLK), :])
            pltpu.sync_copy(wv_v.at[rows, :], cx_out.at[k, pl.ds(GX_V, ROW_BLK), :])

    return pl.pallas_call(
        body,
        name="mem_bwd",
        grid=(1,),
        in_specs=[_full((M, D)), _full((M, D)), _full((M, D)), _full((M, D)), _full((1, D)), ANY, ANY],
        out_specs=[_full((1, D)), ANY],
        out_shape=[jax.ShapeDtypeStruct((1, D), F32), jax.ShapeDtypeStruct(cx_in.shape, BF16)],
        input_output_aliases={6: 1},
        scratch_shapes=[pltpu.VMEM((D, D), BF16), pltpu.VMEM((D, D), BF16)],
        compiler_params=_params(("arbitrary",)),
    )(dk, dv, mem, memn, gmem, pr1g, cx_in)


def _ffn_fwd(x2, target, gpre, gpost, g2g, pr1g):
    S, D = x2.shape
    T = min(TILE_FFN, S)
    n = S // T

    def body(x2_ref, t_ref, gpre_ref, gpost_ref, g2g_hbm, pr1g_hbm,
             h3_ref, g_hbm, u_hbm, do3_ref, dx3_ref, loss_ref, dgpost_ref, rx_ref, wg_v, wu_v, wd_v, gst, ust, sem):
        i = pl.program_id(0)

        @pl.when(i == 0)
        def _():
            pltpu.sync_copy(g2g_hbm.at[:, pl.ds(0, D), :], wg_v)
            pltpu.sync_copy(g2g_hbm.at[:, pl.ds(D, D), :], wu_v)
            pltpu.sync_copy(pr1g_hbm.at[:, pl.ds(P1_DOWN, FF_BLK), :], wd_v)
            loss_ref[...] = jnp.zeros_like(loss_ref)
            dgpost_ref[...] = jnp.zeros_like(dgpost_ref)

        xv = x2_ref[...]
        xn, rx = _rms(xv)
        rx_ref[...] = rx
        hb = (xn * gpre_ref[...]).astype(BF16)
        h3_ref[...] = hb
        o3 = jnp.zeros((T, D), F32)
        out = [None, None]
        for c in range(N_CHIPS):
            slot = c % 2
            if out[slot] is not None:
                for cp in out[slot]:
                    cp.wait()
            g = _mm(hb, wg_v[c])
            u = _mm(hb, wu_v[c])
            gst[slot] = g.astype(BF16)
            ust[slot] = u.astype(BF16)
            out[slot] = (pltpu.make_async_copy(gst.at[slot], g_hbm.at[c, i], sem.at[0, slot]),
                         pltpu.make_async_copy(ust.at[slot], u_hbm.at[c, i], sem.at[1, slot]))
            for cp in out[slot]:
                cp.start()
            o3 = o3 + _mm((g * _sigmoid(g) * u).astype(BF16), wd_v[c])
        for pair in out:
            for cp in pair:
                cp.wait()
        o3n, r3 = _rms(o3)
        diff = xv + o3n * gpost_ref[...] - t_ref[...]
        sq = jnp.sum(jnp.sum(diff * diff, axis=-1, keepdims=True), axis=0, keepdims=True)
        loss_ref[...] += sq * (0.5 / D)
        dx3 = diff * (1.0 / D)
        dx3_ref[...] = dx3
        d_o3, dg = _rms_bwd(dx3, o3n, r3, gpost_ref[...])
        dgpost_ref[...] += dg
        do3_ref[...] = d_o3.astype(BF16)

    tok = lambda w: pl.BlockSpec((T, w), lambda i: (i, 0))
    h3, gs, us, do3, dx3, loss, dgpost, rx2 = pl.pallas_call(
        body,
        name="ffn_fwd",
        grid=(n,),
        in_specs=[tok(D), tok(D), _full((1, D)), _full((1, D)), ANY, ANY],
        out_specs=[tok(D), ANY, ANY, tok(D), tok(D), _full((1, 128)), _full((1, D)), tok(1)],
        out_shape=[jax.ShapeDtypeStruct((S, D), BF16), jax.ShapeDtypeStruct((N_CHIPS, n, T, FF_BLK), BF16),
                   jax.ShapeDtypeStruct((N_CHIPS, n, T, FF_BLK), BF16), jax.ShapeDtypeStruct((S, D), BF16),
                   jax.ShapeDtypeStruct((S, D), F32), jax.ShapeDtypeStruct((1, 128), F32),
                   jax.ShapeDtypeStruct((1, D), F32), jax.ShapeDtypeStruct((S, 1), F32)],
        scratch_shapes=[pltpu.VMEM((N_CHIPS, D, FF_BLK), BF16), pltpu.VMEM((N_CHIPS, D, FF_BLK), BF16),
                        pltpu.VMEM((N_CHIPS, FF_BLK, D), BF16), pltpu.VMEM((2, T, FF_BLK), BF16),
                        pltpu.VMEM((2, T, FF_BLK), BF16), pltpu.SemaphoreType.DMA((2, 2))],
        compiler_params=_params(("arbitrary",)),
    )(x2, target, gpre, gpost, g2g, pr1g)
    return h3, gs.reshape(N_CHIPS, S, FF_BLK), us.reshape(N_CHIPS, S, FF_BLK), do3, dx3, loss, dgpost, rx2


def _ffn_bwd(h3, do3, gs, us, g2g, pr1g):
    S, D = h3.shape
    T = min(TILE_FFN, S)
    n = S // T
    NP = FFN_BWD_BLOCKS

    def body(h3_ref, do3_ref, g_ref, u_ref, g2g_hbm, pr1g_hbm, dh3_ref, cf1_hbm, cf2_hbm,
             wg_v, wu_v, wd_v, dwg_acc, dwu_acc, dwd_acc):
        jp = pl.program_id(0)
        i = pl.program_id(1)
        blocks = pl.ds(NP * jp, NP)

        @pl.when(i == 0)
        def _():
            pltpu.sync_copy(g2g_hbm.at[blocks, pl.ds(0, D), :], wg_v)
            pltpu.sync_copy(g2g_hbm.at[blocks, pl.ds(D, D), :], wu_v)
            pltpu.sync_copy(pr1g_hbm.at[blocks, pl.ds(P1_DOWN, FF_BLK), :], wd_v)
            dwg_acc[...] = jnp.zeros_like(dwg_acc)
            dwu_acc[...] = jnp.zeros_like(dwu_acc)
            dwd_acc[...] = jnp.zeros_like(dwd_acc)

        hb = h3_ref[...]
        d_o3 = do3_ref[...]
        dh = jnp.zeros((T, D), F32)
        for c in range(NP):
            da = _mm_nt(d_o3, wd_v[c])
            g = g_ref[c].astype(F32)
            u = u_ref[c].astype(F32)
            sg = _sigmoid(g)
            sl = g * sg
            dwd_acc[c] += _mm_tn((sl * u).astype(BF16), d_o3)
            dub = (da * sl).astype(BF16)
            dgb = (da * u * (sg * (1.0 + g * (1.0 - sg)))).astype(BF16)
            dwg_acc[c] += _mm_tn(dgb, hb)
            dwu_acc[c] += _mm_tn(dub, hb)
            dh = dh + _mm_nt(dgb, wg_v[c]) + _mm_nt(dub, wu_v[c])
        dh3_ref[...] = dh.astype(BF16)

        @pl.when(i == n - 1)
        def _():
            wd_v[...] = dwg_acc[...].astype(BF16)
            pltpu.sync_copy(wd_v, cf1_hbm.at[blocks, pl.ds(0, FF_BLK), :])
            wd_v[...] = dwu_acc[...].astype(BF16)
            pltpu.sync_copy(wd_v, cf1_hbm.at[blocks, pl.ds(FF_BLK, FF_BLK), :])
            wd_v[...] = dwd_acc[...].astype(BF16)
            pltpu.sync_copy(wd_v, cf2_hbm.at[blocks])

    tok = lambda w: pl.BlockSpec((T, w), lambda jp, i: (i, 0))
    blk = pl.BlockSpec((NP, T, FF_BLK), lambda jp, i: (jp, i, 0))
    return pl.pallas_call(
        body,
        name="ffn_bwd",
        grid=(N_CHIPS // NP, n),
        in_specs=[tok(D), tok(D), blk, blk, ANY, ANY],
        out_specs=[pl.BlockSpec((None, T, D), lambda jp, i: (jp, i, 0)), ANY, ANY],
        out_shape=[jax.ShapeDtypeStruct((N_CHIPS // NP, S, D), BF16),
                   jax.ShapeDtypeStruct((N_CHIPS, 2 * FF_BLK, D), BF16),
                   jax.ShapeDtypeStruct((N_CHIPS, FF_BLK, D), BF16)],
        scratch_shapes=[pltpu.VMEM((NP, D, FF_BLK), BF16), pltpu.VMEM((NP, D, FF_BLK), BF16),
                        pltpu.VMEM((NP, FF_BLK, D), BF16), pltpu.VMEM((NP, FF_BLK, D), F32),
                        pltpu.VMEM((NP, FF_BLK, D), F32), pltpu.VMEM((NP, FF_BLK, D), F32)],
        compiler_params=_params(("arbitrary", "arbitrary")),
    )(h3, do3, gs, us, g2g, pr1g)


IN_HBM = pl.BlockSpec(memory_space=pltpu.HBM)
IN_SEMAPHORES = pl.BlockSpec(memory_space=pltpu.SEMAPHORE)


def _split_copies(refs, na, nw, sems):
    srcs, lands = refs[:na + nw], refs[na + nw:2 * (na + nw)]
    send, recv = sems
    x, y, c, _ = _my_place()
    me = 4 * x + 2 * y + c
    pairs = []
    for d, ((px, py, pc), pidx) in enumerate(_peers(x, y, c)):
        for a in range(na + nw):
            k = d * (na + nw) + a
            src = srcs[a].at[2 * px + py, pc] if a < na else srcs[a]
            pairs.append((_remote(src, lands[a].at[me], send.at[k], recv.at[k], (px, py, pc)),
                          _remote(src, lands[a].at[pidx], send.at[k], recv.at[k], (px, py, pc))))
    return pairs


def _exchange_start(contribs, whole, tag):
    na, nw = len(contribs), len(whole)
    cs = [_by_halves(a) for a in contribs]
    flying = cs + list(whole) + [lax.empty(_slots(a).shape, a.dtype) for a in cs] + [
        lax.empty((N_DEV,) + a.shape, a.dtype) for a in whole]
    nf = len(flying)

    def body(*refs):
        for mine, _ in _split_copies(refs[:nf], na, nw, refs[nf:nf + 2]):
            mine.start()
        refs[-1][...] = jnp.zeros_like(refs[-1])

    ncopies = (N_DEV - 1) * (na + nw)
    outs = pl.pallas_call(
        body,
        name="exchange_start_" + tag,
        in_specs=[IN_HBM] * nf,
        out_specs=[IN_SEMAPHORES] * 2 + [IN_HBM] * nf + [pl.BlockSpec(memory_space=pltpu.VMEM)],
        out_shape=[pltpu.SemaphoreType.DMA((ncopies,)), pltpu.SemaphoreType.DMA((ncopies,))]
        + [pltpu.HBM(a.shape, a.dtype) for a in flying] + [jax.ShapeDtypeStruct((8, 128), F32)],
        input_output_aliases={k: 2 + k for k in range(nf)},
        compiler_params=pltpu.CompilerParams(has_side_effects=pltpu.SideEffectType.DATAFLOW_SIDE_EFFECTING),
    )(*[pltpu.with_memory_space_constraint(a, pltpu.HBM) for a in flying])
    return outs[:2], outs[2:2 + nf], outs[-1]


def _exchange_wait(sems, flying, na, nw, after, tag):
    nf = len(flying)

    def body(*refs):
        for mine, theirs in _split_copies(refs[:nf], na, nw, refs[nf:nf + 2]):
            theirs.wait_recv()
            mine.wait_send()

    outs = pl.pallas_call(
        body,
        name="exchange_wait_" + tag,
        in_specs=[IN_HBM] * nf + [IN_SEMAPHORES] * 2 + [ANY] * len(after),
        out_specs=[IN_HBM] * nf,
        out_shape=[pltpu.HBM(a.shape, a.dtype) for a in flying],
        input_output_aliases={k: k for k in range(nf)},
        compiler_params=pltpu.CompilerParams(has_side_effects=pltpu.SideEffectType.DATAFLOW_SIDE_EFFECTING),
    )(*flying, *sems, *after)
    return outs[na + nw:]


def _sum_peers(parts, own, place, steps, tag, after=()):
    _, rows, w = parts.shape
    tr = rows // steps
    if own.ndim == 3:
        own = _by_halves(own)

    def body(place_ref, *refs):
        p_refs, own_ref, o_ref = refs[:N_DEV], refs[N_DEV], refs[-1]
        me = place_ref[2]
        acc = None
        for s in range(N_DEV):
            term = jnp.where(me == s, own_ref[...], p_refs[s][...]).astype(F32)
            acc = term if acc is None else acc + term
        o_ref[...] = acc

    def other(s):
        return lambda i, pr: (jnp.where(pr[2] == s, (s + 1) % N_DEV, s), i, 0)

    own_spec = (pl.BlockSpec((None, None, tr, w), lambda i, pr: (pr[0], pr[1], i, 0)) if own.ndim == 4 else
                pl.BlockSpec((tr, w), lambda i, pr: (i, 0)))
    out_spec = (pl.BlockSpec((None, tr, w), lambda i, pr: (pr[1], i, 0)) if own.ndim == 4 else
                pl.BlockSpec((tr, w), lambda i, pr: (i, 0)))
    return pl.pallas_call(
        body,
        name="sum_peers_" + tag,
        grid_spec=pltpu.PrefetchScalarGridSpec(
            num_scalar_prefetch=1,
            grid=(steps,),
            in_specs=[pl.BlockSpec((None, tr, w), other(s)) for s in range(N_DEV)] + [own_spec] + [ANY] * len(after),
            out_specs=out_spec,
        ),
        out_shape=jax.ShapeDtypeStruct((2, rows, w) if own.ndim == 4 else (rows, w), F32),
        compiler_params=_params(("arbitrary",)),
    )(place, *([parts] * N_DEV), own, *after)


def _pair_gather(bufs, tag):
    np_ = len(bufs)

    def body(*refs):
        srcs, dsts = refs[:np_], refs[np_:2 * np_]
        send, recv = refs[2 * np_:]
        x, y, c, _ = _my_place()
        cps = []
        for p in range(np_):
            cp = _remote(srcs[p].at[c], dsts[p].at[c], send.at[p], recv.at[p], (x, y, 1 - c))
            cp.start()
            cps.append(cp)
        for p, cp in enumerate(cps):
            other = dsts[p].at[1 - c]
            _remote(other, other, send.at[p], recv.at[p], (x, y, 1 - c)).wait_recv()
            cp.wait_send()

    outs = pl.pallas_call(
        body,
        name="pair_gather_" + tag,
        in_specs=[ANY] * np_,
        out_specs=[ANY] * np_,
        out_shape=[jax.ShapeDtypeStruct(a.shape, F32) for a in bufs],
        input_output_aliases={p: p for p in range(np_)},
        scratch_shapes=[pltpu.SemaphoreType.DMA((np_,))] * 2,
    )(*bufs)
    return [o.reshape(2 * a.shape[1], a.shape[2]) for o, a in zip(outs, bufs)]


def _adamw(gsrc, row0, w, m, v, tr, tag):
    rows, width = w.shape
    off = row0 // tr
    bc1 = 1.0 - ADAM_B1 ** ADAM_STEP
    bc2 = 1.0 - ADAM_B2 ** ADAM_STEP

    def body(g_ref, w_ref, m_ref, v_ref, go_ref, d_ref, mo_ref, vo_ref):
        g = g_ref[...]
        m2 = ADAM_B1 * m_ref[...] + (1.0 - ADAM_B1) * g
        v2 = ADAM_B2 * v_ref[...] + (1.0 - ADAM_B2) * (g * g)
        go_ref[...] = g
        mo_ref[...] = m2
        vo_ref[...] = v2
        d_ref[...] = -ADAM_LR * ((m2 / bc1) / (jnp.sqrt(v2 / bc2) + ADAM_EPS) + ADAM_WD * w_ref[...])

    here = pl.BlockSpec((tr, width), lambda i: (i, 0))
    return pl.pallas_call(
        body,
        name="adamw_" + tag,
        grid=(rows // tr,),
        in_specs=[pl.BlockSpec((tr, width), lambda i: (off + i, 0)), here, here, here],
        out_specs=[here] * 4,
        out_shape=[jax.ShapeDtypeStruct((rows, width), F32)] * 4,
        compiler_params=_params(("arbitrary",)),
    )(gsrc, w, m, v)


def kernel(x, mem, mix_pre_g, w_mix_in, conv_a_w, conv_b_w, conv_b_b, ln_b_g, ln_b_b, w_mix_out, mix_post_g, xa_pre_g, mem_norm_g, w_q, w_k, w_v, w_o, xa_post_g, ffn_pre_g, w_gate, w_up, w_down, ffn_post_g, loss_target, m_mix_pre_g, m_w_mix_in, m_conv_a_w, m_conv_b_w, m_conv_b_b, m_ln_b_g, m_ln_b_b, m_w_mix_out, m_mix_post_g, m_xa_pre_g, m_mem_norm_g, m_w_q, m_w_k, m_w_v, m_w_o, m_xa_post_g, m_ffn_pre_g, m_w_gate, m_w_up, m_w_down, m_ffn_post_g, v_mix_pre_g, v_w_mix_in, v_conv_a_w, v_conv_b_w, v_conv_b_b, v_ln_b_g, v_ln_b_b, v_w_mix_out, v_mix_post_g, v_xa_pre_g, v_mem_norm_g, v_w_q, v_w_k, v_w_v, v_w_o, v_xa_post_g, v_ffn_pre_g, v_w_gate, v_w_up, v_w_down, v_ffn_post_g):
    given = dict(locals())
    names = ["mix_pre_g", "w_mix_in", "conv_a_w", "conv_b_w", "conv_b_b", "ln_b_g", "ln_b_b", "w_mix_out",
             "mix_post_g", "xa_pre_g", "mem_norm_g", "w_q", "w_k", "w_v", "w_o", "xa_post_g", "ffn_pre_g",
             "w_gate", "w_up", "w_down", "ffn_post_g"]
    row = lambda a: a.reshape(1, -1)
    cx, cy, cc = lax.axis_index("x"), lax.axis_index("y"), lax.axis_index("c")
    chip = 2 * cx + cy
    ca_blk = conv_a_w.shape[1]

    conv_rows = CONV_A_W + CONV_B_W
    sw = jnp.concatenate([conv_a_w, conv_b_w, jnp.zeros((SMALL_W_ROWS - conv_rows, ca_blk), F32)], axis=0)
    g1g, pr0g, swg = _gather_weights([w_mix_in.astype(BF16), w_mix_out.astype(BF16), sw])
    conv_full = jnp.transpose(swg[:, :conv_rows, :], (1, 0, 2)).reshape(conv_rows, N_CHIPS * ca_blk)
    wa, wb = conv_full[:CONV_A_W], conv_full[CONV_A_W:]
    pr1 = jnp.concatenate([w_q, w_k, w_v, w_o, w_down], axis=0).astype(BF16)
    g2 = jnp.concatenate([w_gate, w_up], axis=0).astype(BF16)

    xs, ms, tgt = x[0], mem[0], loss_target[0]
    x1, u, o1, z1, rx, ro1, pr1g, g2g = _mix_fwd(xs, row(mix_pre_g), row(mix_post_g), wa, wb, row(conv_b_b),
                                                 row(ln_b_g), row(ln_b_b), g1g, pr0g, [pr1, g2])
    memn, kb, vb = _mem_kv(ms, row(mem_norm_g), pr1g)
    x2, q, o2, rx1, ro2 = _xattn_fwd(x1, row(xa_pre_g), row(xa_post_g), kb, vb, pr1g)
    h3, gs, us, do3, dx3, loss_part, d_ffn_post, rx2 = _ffn_fwd(x2, tgt, row(ffn_pre_g), row(ffn_post_g), g2g, pr1g)

    dh3p, cf1, cf2 = _ffn_bwd(h3, do3, gs, us, g2g, pr1g)
    dx1, d_ffn_pre, d_xa_post, d_xa_pre, dk, dv, cxa, yf1, yf2 = _xattn_bwd(
        dx3, dh3p, x2, x1, o2, q, rx2, ro2, rx1, kb, vb, row(ffn_pre_g), row(xa_post_g), row(xa_pre_g), pr1g, cf1, cf2)
    d_mem_g, cxa = _mem_bwd(dk, dv, ms, memn, row(mem_norm_g), pr1g, cxa)
    sems_x, flying_x, token_x = _exchange_start([cxa], [], "attn")
    dx, d_mix_pre, d_mix_post, dwa, dwb, dbb, dlng, dlnb, cm1, cm2 = _mix_bwd(
        dx1, xs, o1, u, z1, ro1, rx, row(mix_pre_g), row(mix_post_g), wa, wb, row(ln_b_g), row(ln_b_b), g1g, pr0g,
        token_x)
    (yx,) = _exchange_wait(sems_x, flying_x, 1, 0, [d_mix_pre], "attn")

    small_parts = [d_mix_pre, dwa, dwb, dbb, dlng, dlnb, d_mix_post, d_xa_pre, d_mem_g, d_xa_post, d_ffn_pre,
                   d_ffn_post, loss_part]
    sizes = [p.size for p in small_parts]
    small = jnp.concatenate([p.reshape(-1) for p in small_parts])
    small_rows = -(-small.size // (8 * 128)) * 8
    small = jnp.pad(small, (0, small_rows * 128 - small.size)).reshape(small_rows, 128)
    sems, flying, token = _exchange_start([cm1, cm2], [small], "mix")
    place = jnp.stack([chip, cc, 2 * chip + cc]).astype(jnp.int32)

    res = {}

    def update(nm, src, row0, tr, transposed=False):
        view = (lambda a: a.T) if transposed else (lambda a: a)
        outs = _adamw(src, row0, view(given[nm]), view(given["m_" + nm]), view(given["v_" + nm]), tr, nm)
        res[nm] = [view(o) for o in outs]

    early = [(yf1, cf1, "gate_up"), (yf2, cf2, "down"), (yx, cxa, "attn")]
    r_gu, r_down, r_attn = _pair_gather([_sum_peers(y, c, place, 2, t, after=(token,)) for y, c, t in early], "early")
    update("w_gate", r_gu, 0, FF_BLK // 2, transposed=True)
    update("w_up", r_gu, FF_BLK, FF_BLK // 2, transposed=True)
    update("w_down", r_down, 0, FF_BLK // 2)
    for nm, row0 in (("w_q", GX_Q), ("w_k", GX_K), ("w_v", GX_V), ("w_o", GX_O)):
        update(nm, r_attn, row0, ROW_BLK)
    done = [res[nm][1] for nm in ("w_gate", "w_up", "w_down", "w_q", "w_k", "w_v", "w_o")]

    ym1, ym2, small_all = _exchange_wait(sems, flying, 2, 1, done, "mix")
    r_in, r_out = _pair_gather([_sum_peers(y, c, place, 2, t) for y, c, t in
                                [(ym1, cm1, "mix_in"), (ym2, cm2, "mix_out")]], "late")
    update("w_mix_in", r_in, 0, 512)
    update("w_mix_out", r_out, 0, ROW_BLK)
    small_sum = _sum_peers(small_all, small, place, 1, "small").reshape(-1)
    red, pos = [], 0
    for p, sz in zip(small_parts, sizes):
        red.append(small_sum[pos:pos + sz].reshape(p.shape))
        pos += sz
    (r_mix_pre, r_wa, r_wb, r_bb, r_lng, r_lnb, r_mix_post, r_xa_pre, r_mem_g, r_xa_post, r_ffn_pre, r_ffn_post,
     r_loss) = red
    loss = r_loss[0, 0]

    small_grads = {"mix_pre_g": r_mix_pre, "conv_b_b": r_bb, "ln_b_g": r_lng, "ln_b_b": r_lnb,
                   "mix_post_g": r_mix_post, "xa_pre_g": r_xa_pre, "mem_norm_g": r_mem_g, "xa_post_g": r_xa_post,
                   "ffn_pre_g": r_ffn_pre, "ffn_post_g": r_ffn_post,
                   "conv_a_w": lax.dynamic_slice_in_dim(r_wa, chip * ca_blk, ca_blk, axis=1),
                   "conv_b_w": lax.dynamic_slice_in_dim(r_wb, chip * ca_blk, ca_blk, axis=1)}
    small_names = list(small_grads)

    def packed(prefix, grads=None):
        flat = jnp.concatenate([(grads[nm] if grads else given[prefix + nm]).reshape(-1) for nm in small_names])
        rows8 = -(-flat.size // (8 * 128)) * 8
        return jnp.pad(flat, (0, rows8 * 128 - flat.size)).reshape(rows8, 128)

    gp = packed("", small_grads)
    outs = _adamw(gp, 0, packed(""), packed("m_"), packed("v_"), gp.shape[0], "small")
    pos = 0
    for nm in small_names:
        shape = given[nm].shape
        sz = given[nm].size
        res[nm] = [o.reshape(-1)[pos:pos + sz].reshape(shape) for o in outs]
        pos += sz

    return (loss, dx[None], *[res[nm][0] for nm in names], *[res[nm][1] for nm in names],
            *[res[nm][2] for nm in names], *[res[nm][3] for nm in names])
```

```python
import jax
import jax.numpy as jnp
from jax import lax
from jax.experimental import pallas as pl
from jax.experimental.pallas import tpu as pltpu

F32 = jnp.float32
BF16 = jnp.bfloat16
MESH = pl.DeviceIdType.MESH

RMS_EPS = 1e-6
LN_EPS = 1e-5
D_MODEL = 1024
D_A = 512
D_B = 512
D_IN_ALL = 3 * D_A + 2 * D_B
CONV_A_W = 3
CONV_B_W = 31
HALO = 32
XA_HEADS = 4
HEAD_DIM = 256
D_FF = 2816
N_CHIPS = 4
N_DEV = 8
FF_BLK = D_FF // N_CHIPS
IN_BLK = D_IN_ALL // N_CHIPS
ROW_BLK = D_MODEL // N_CHIPS

ADAM_LR = 0.001
ADAM_B1 = 0.9
ADAM_B2 = 0.999
ADAM_EPS = 1e-08
ADAM_WD = 0.01
ADAM_STEP = 10

TILE_FWD = 512
TILE_XATTN_FWD = 1024
TILE_FFN = 512
TILE_FFN_BWD = 512
TILE_XATTN_BWD = 512
TILE_BWD = 256
FFN_BWD_BLOCKS = 2
CONV_ROWS_FWD = 64
CONV_ROWS = 32
V7X_VMEM_LIMIT = 56 * 1024 * 1024

P1_Q, P1_K, P1_V, P1_O, P1_DOWN = 0, 256, 512, 768, 1024
P1_ROWS = P1_DOWN + FF_BLK
GX_Q, GX_K, GX_V, GX_O = 0, 256, 512, 768
SMALL_W_ROWS = 48

ANY = pl.BlockSpec(memory_space=pl.ANY)


def _mm(a, b):
    return lax.dot_general(a, b, (((1,), (0,)), ((), ())), preferred_element_type=F32)


def _mm_nt(a, b):
    return lax.dot_general(a, b, (((1,), (1,)), ((), ())), preferred_element_type=F32)


def _mm_tn(a, b):
    return lax.dot_general(a, b, (((0,), (0,)), ((), ())), preferred_element_type=F32)


def _sigmoid(x):
    return 0.5 * jnp.tanh(0.5 * x) + 0.5


def _rms(x):
    r = lax.rsqrt(jnp.mean(x * x, axis=-1, keepdims=True) + RMS_EPS)
    return x * r, r


def _rms_bwd(dy, xn, r, g):
    gdy = dy * g
    dx = r * (gdy - xn * jnp.mean(gdy * xn, axis=-1, keepdims=True))
    return dx, jnp.sum(dy * xn, axis=0, keepdims=True)


def _fold8(a):
    out = a[0:8, :]
    for m in range(1, a.shape[0] // 8):
        out = out + a[8 * m:8 * m + 8, :]
    return out


def _full(shape):
    return pl.BlockSpec(shape, lambda *_: (0,) * len(shape))


def _params(sem=None):
    return pltpu.CompilerParams(dimension_semantics=sem, vmem_limit_bytes=V7X_VMEM_LIMIT)


def _load_rows(dst, src_hbm, row0, rows):
    for k in range(N_CHIPS):
        pltpu.sync_copy(src_hbm.at[k, pl.ds(row0, rows), :], dst.at[pl.ds(rows * k, rows), :])


def _load_cols(dst, src_hbm, cols):
    for k in range(N_CHIPS):
        pltpu.sync_copy(src_hbm.at[k], dst.at[:, pl.ds(cols * k, cols)])


def _fill_phases(src, sh, nrows):
    for r in range(1, 8):
        sh[r, pl.ds(0, nrows), :] = src[pl.ds(r, nrows), pl.ds(D_A, D_B)]


def _phase_rows(src, sh, off, start, size):
    r = off % 8
    if r == 0:
        return src[pl.ds(off + start, size), pl.ds(D_A, D_B)]
    return sh[r, pl.ds(off - r + start, size), :]


def _my_place():
    x, y, c = lax.axis_index("x"), lax.axis_index("y"), lax.axis_index("c")
    return x, y, c, ((1 - x, y), (x, 1 - y), (1 - x, 1 - y))


def _remote(src, dst, send_sem, recv_sem, to):
    return pltpu.make_async_remote_copy(src_ref=src, dst_ref=dst, send_sem=send_sem, recv_sem=recv_sem,
                                        device_id=to, device_id_type=MESH)


def _gather_sems(np_):
    return [pltpu.SemaphoreType.DMA((np_, 3))] * 4 + [pltpu.SemaphoreType.DMA((np_,))] * 2


def _gather_start(srcs, dsts, sems):
    send, recv, _, _, osend, orecv = sems
    x, y, c, chips = _my_place()
    j = 2 * x + y
    for p in range(len(srcs)):
        _remote(srcs[p], dsts[p].at[j], osend.at[p], orecv.at[p], (x, y, 1 - c)).start()
        for nn, (kx, ky) in enumerate(chips):
            _remote(srcs[p].at[c], dsts[p].at[j, c], send.at[p, nn], recv.at[p, nn], (kx, ky, c)).start()


def _gather_forward(srcs, dsts, sems):
    send, recv, fsend, frecv, _, _ = sems
    x, y, c, chips = _my_place()
    for nn, (kx, ky) in enumerate(chips):
        for p in range(len(srcs)):
            blk = dsts[p].at[2 * kx + ky, c]
            _remote(blk, blk, send.at[p, nn], recv.at[p, nn], (kx, ky, c)).wait_recv()
            _remote(blk, blk, fsend.at[p, nn], frecv.at[p, nn], (x, y, 1 - c)).start()


def _gather_finish(srcs, dsts, sems):
    send, recv, fsend, frecv, osend, orecv = sems
    x, y, c, chips = _my_place()
    j = 2 * x + y
    for nn, (kx, ky) in enumerate(chips):
        for p in range(len(srcs)):
            other = dsts[p].at[2 * kx + ky, 1 - c]
            _remote(other, other, fsend.at[p, nn], frecv.at[p, nn], (x, y, 1 - c)).wait_recv()
    for nn, (kx, ky) in enumerate(chips):
        for p in range(len(srcs)):
            _remote(srcs[p].at[c], dsts[p].at[j, c], send.at[p, nn], recv.at[p, nn], (kx, ky, c)).wait_send()
            blk = dsts[p].at[2 * kx + ky, c]
            _remote(blk, blk, fsend.at[p, nn], frecv.at[p, nn], (x, y, 1 - c)).wait_send()
    for p in range(len(srcs)):
        _remote(srcs[p], dsts[p].at[j], osend.at[p], orecv.at[p], (x, y, 1 - c)).wait()


def _split_halves(a):
    return a.reshape(2, a.shape[0] // 2, a.shape[1])


def _gather_weights(packs):
    np_ = len(packs)
    split = [_split_halves(a) for a in packs]

    def body(*refs):
        srcs, dsts, sems = refs[:np_], refs[np_:2 * np_], refs[2 * np_:]
        _gather_start(srcs, dsts, sems)
        _gather_forward(srcs, dsts, sems)
        _gather_finish(srcs, dsts, sems)

    outs = pl.pallas_call(
        body,
        name="gather_weights",
        in_specs=[ANY] * np_,
        out_specs=[ANY] * np_,
        out_shape=[jax.ShapeDtypeStruct((N_CHIPS,) + a.shape, a.dtype) for a in split],
        scratch_shapes=_gather_sems(np_),
    )(*split)
    return [o.reshape((N_CHIPS,) + a.shape) for o, a in zip(outs, packs)]


def _peers(x, y, c):
    out = []
    for d in range(1, N_DEV):
        px = 1 - x if d & 4 else x
        py = 1 - y if d & 2 else y
        pc = 1 - c if d & 1 else c
        out.append(((px, py, pc), 4 * px + 2 * py + pc))
    return out


def _scatter_copies(srcs, dsts, send, recv):
    x, y, c, _ = _my_place()
    me = 4 * x + 2 * y + c
    out = []
    for a in range(len(srcs)):
        for d, ((px, py, pc), pidx) in enumerate(_peers(x, y, c)):
            piece = srcs[a].at[2 * px + py, pc]
            out.append((_remote(piece, dsts[a].at[me], send.at[a, d], recv.at[a, d], (px, py, pc)),
                        _remote(piece, dsts[a].at[pidx], send.at[a, d], recv.at[a, d], (px, py, pc))))
    return out


def _scatter_start(srcs, dsts, send, recv):
    for out_cp, _ in _scatter_copies(srcs, dsts, send, recv):
        out_cp.start()


def _scatter_wait(srcs, dsts, send, recv):
    for out_cp, in_cp in _scatter_copies(srcs, dsts, send, recv):
        in_cp.wait_recv()
        out_cp.wait_send()


def _scatter_sems(na):
    return [pltpu.SemaphoreType.DMA((na, N_DEV - 1))] * 2


def _slots(a):
    return jax.ShapeDtypeStruct((N_DEV,) + a.shape[2:], a.dtype)


def _by_halves(a):
    return a.reshape(a.shape[0], 2, a.shape[1] // 2, a.shape[2])


def _mix_fwd(x, gpre, gpost, wa, wb, bb, lng, lnb, g1g, pr0g, late):
    S, D = x.shape
    T = min(TILE_FWD, S)
    n = S // T
    nl = len(late)
    late_split = [_split_halves(a) for a in late]

    def body(*refs):
        (x_ref, gpre_ref, gpost_ref, wa_ref, wb_ref, bb_ref, lng_ref, lnb_ref, g1g_hbm, pr0g_hbm) = refs[:10]
        srcs = refs[10:10 + nl]
        x1_ref, u_ref, o1_ref, z1_ref, rx_ref, ro_ref = refs[10 + nl:16 + nl]
        dsts = refs[16 + nl:16 + 2 * nl]
        win_v, wout_v, ext, sh, z1buf = refs[16 + 2 * nl:21 + 2 * nl]
        sems = refs[21 + 2 * nl:]
        i = pl.program_id(0)

        @pl.when(i == 0)
        def _():
            _gather_start(srcs, dsts, sems)
            _load_cols(win_v, g1g_hbm, IN_BLK)
            _load_rows(wout_v, pr0g_hbm, 0, ROW_BLK)
            ext[pl.ds(0, HALO), :] = jnp.zeros((HALO, D_A + D_B), F32)

        @pl.when(i == max(n - 2, 0))
        def _():
            _gather_forward(srcs, dsts, sems)

        xv = x_ref[...]
        xn, rx = _rms(xv)
        rx_ref[...] = rx
        h = (xn * gpre_ref[...]).astype(BF16)
        u = _mm(h, win_v[...])
        u_ref[...] = u.astype(BF16)
        b_a = u[:, 0:D_A]
        cv = u[:, D_A:2 * D_A] * u[:, 2 * D_A:3 * D_A]
        z0 = u[:, 3 * D_A:3 * D_A + D_B] * _sigmoid(u[:, 3 * D_A + D_B:])
        ext[pl.ds(HALO, T), pl.ds(0, D_A)] = cv
        ext[pl.ds(HALO, T), pl.ds(D_A, D_B)] = z0

        conv_a = ext[pl.ds(HALO - 2, T), pl.ds(0, D_A)] * wa_ref[0:1, :]
        for k in range(1, CONV_A_W):
            conv_a = conv_a + ext[pl.ds(HALO - 2 + k, T), pl.ds(0, D_A)] * wa_ref[k:k + 1, :]
        y_a = b_a * conv_a

        _fill_phases(ext, sh, T + HALO - 8)
        base = HALO - (CONV_B_W - 1)

        def chunk(ci, carry):
            start = pl.multiple_of(ci * CONV_ROWS_FWD, 8)
            acc = jnp.broadcast_to(bb_ref[...], (CONV_ROWS_FWD, D_B))
            for k in range(CONV_B_W):
                acc = acc + _phase_rows(ext, sh, base + k, start, CONV_ROWS_FWD) * wb_ref[k:k + 1, :]
            z1buf[pl.ds(start, CONV_ROWS_FWD), :] = acc
            return carry

        lax.fori_loop(0, T // CONV_ROWS_FWD, chunk, 0)
        z1 = z1buf[...]
        z1_ref[...] = z1.astype(BF16)
        mu = jnp.mean(z1, axis=-1, keepdims=True)
        zc = z1 - mu
        rstd = lax.rsqrt(jnp.mean(zc * zc, axis=-1, keepdims=True) + LN_EPS)
        l = zc * rstd * lng_ref[...] + lnb_ref[...]
        y_b = l * _sigmoid(l)
        y = jnp.concatenate([y_a, y_b], axis=-1).astype(BF16)
        o1 = _mm(y, wout_v[...])
        o1_ref[...] = o1.astype(BF16)
        o1n, ro = _rms(o1)
        ro_ref[...] = ro
        x1_ref[...] = xv + o1n * gpost_ref[...]
        ext[pl.ds(0, HALO), :] = ext[pl.ds(T, HALO), :]

        @pl.when(i == n - 1)
        def _():
            _gather_finish(srcs, dsts, sems)

    tok = lambda w: pl.BlockSpec((T, w), lambda i: (i, 0))
    outs = pl.pallas_call(
        body,
        name="mix_fwd",
        grid=(n,),
        in_specs=[tok(D), _full((1, D)), _full((1, D)), _full((CONV_A_W, D_A)), _full((CONV_B_W, D_B)),
                  _full((1, D_B)), _full((1, D_B)), _full((1, D_B)), ANY, ANY] + [ANY] * nl,
        out_specs=[tok(D), tok(D_IN_ALL), tok(D), tok(D_B), tok(1), tok(1)] + [ANY] * nl,
        out_shape=[jax.ShapeDtypeStruct((S, D), F32), jax.ShapeDtypeStruct((S, D_IN_ALL), BF16),
                   jax.ShapeDtypeStruct((S, D), BF16), jax.ShapeDtypeStruct((S, D_B), BF16),
                   jax.ShapeDtypeStruct((S, 1), F32), jax.ShapeDtypeStruct((S, 1), F32)]
        + [jax.ShapeDtypeStruct((N_CHIPS,) + a.shape, a.dtype) for a in late_split],
        scratch_shapes=[pltpu.VMEM((D, D_IN_ALL), BF16), pltpu.VMEM((D_A + D_B, D), BF16),
                        pltpu.VMEM((HALO + T, D_A + D_B), F32), pltpu.VMEM((8, HALO + T, D_B), F32),
                        pltpu.VMEM((T, D_B), F32)] + _gather_sems(nl),
        compiler_params=_params(("arbitrary",)),
    )(x, gpre, gpost, wa, wb, bb, lng, lnb, g1g, pr0g, *late_split)
    return list(outs[:6]) + [o.reshape((N_CHIPS,) + a.shape) for o, a in zip(outs[6:], late)]


def _mix_bwd(dx1, x, o1, u, z1s, ro1, rx, gpre, gpost, wa, wb, lng, lnb, g1g, pr0g, after):
    S, D = x.shape
    T = min(TILE_BWD, S)
    n = S // T
    hb = T // HALO

    def body(dx1_ref, x_ref, o1_ref, u_ref, uh_ref, z1_ref, ro_ref, rx_ref, gpre_ref, gpost_ref, wa_ref, wb_ref,
             lng_ref, lnb_ref, g1g_hbm, pr0g_hbm, _,
             dx_ref, dgpre_ref, dgpost_ref, dwa_ref, dwb_ref, dbb_ref, dlng_ref, dlnb_ref, cm1_hbm, cm2_hbm,
             win_v, wout_v, dwin_acc, dwout_acc, ext, ext2, shb, dz0buf, dwb_acc):
        i = pl.program_id(0)

        @pl.when(i == 0)
        def _():
            _load_cols(win_v, g1g_hbm, IN_BLK)
            _load_rows(wout_v, pr0g_hbm, 0, ROW_BLK)
            dwin_acc[...] = jnp.zeros_like(dwin_acc)
            dwout_acc[...] = jnp.zeros_like(dwout_acc)
            dwb_acc[...] = jnp.zeros_like(dwb_acc)
            ext2[pl.ds(T, HALO), :] = jnp.zeros((HALO, D_A + D_B), F32)
            for ref in (dgpre_ref, dgpost_ref, dwa_ref, dbb_ref, dlng_ref, dlnb_ref):
                ref[...] = jnp.zeros_like(ref)

        r1 = ro_ref[...]
        o1n = o1_ref[...].astype(F32) * r1
        dx1v = dx1_ref[...]
        d_o1, dgp = _rms_bwd(dx1v, o1n, r1, gpost_ref[...])
        dgpost_ref[...] += dgp
        d_o1b = d_o1.astype(BF16)
        dy = _mm_nt(d_o1b, wout_v[...])

        first = (i == n - 1).astype(F32)
        uh = uh_ref[...].astype(F32) * (1.0 - first)
        ext[pl.ds(0, HALO), pl.ds(0, D_A)] = uh[:, D_A:2 * D_A] * uh[:, 2 * D_A:3 * D_A]
        uf = u_ref[...].astype(F32)
        b_a = uf[:, 0:D_A]
        c_a = uf[:, D_A:2 * D_A]
        v_a = uf[:, 2 * D_A:3 * D_A]
        gv = uf[:, 3 * D_A:3 * D_A + D_B]
        sg = _sigmoid(uf[:, 3 * D_A + D_B:])
        ext[pl.ds(HALO, T), pl.ds(0, D_A)] = c_a * v_a
        ext[pl.ds(HALO, T), pl.ds(D_A, D_B)] = gv * sg
        conv_a = ext[pl.ds(HALO - 2, T), pl.ds(0, D_A)] * wa_ref[0:1, :]
        for k in range(1, CONV_A_W):
            conv_a = conv_a + ext[pl.ds(HALO - 2 + k, T), pl.ds(0, D_A)] * wa_ref[k:k + 1, :]
        z1 = z1_ref[...].astype(F32)
        mu = jnp.mean(z1, axis=-1, keepdims=True)
        zc = z1 - mu
        rstd = lax.rsqrt(jnp.mean(zc * zc, axis=-1, keepdims=True) + LN_EPS)
        zn = zc * rstd
        l = zn * lng_ref[...] + lnb_ref[...]
        sl = _sigmoid(l)
        y = jnp.concatenate([b_a * conv_a, l * sl], axis=-1).astype(BF16)
        dwout_acc[...] += _mm_tn(y, d_o1b)

        dy_a = dy[:, 0:D_A]
        dl = dy[:, D_A:] * (sl * (1.0 + l * (1.0 - sl)))
        dlng_ref[...] += jnp.sum(dl * zn, axis=0, keepdims=True)
        dlnb_ref[...] += jnp.sum(dl, axis=0, keepdims=True)
        dzn = dl * lng_ref[...]
        dz1 = rstd * (dzn - jnp.mean(dzn, axis=-1, keepdims=True) - zn * jnp.mean(dzn * zn, axis=-1, keepdims=True))
        dbb_ref[...] += jnp.sum(dz1, axis=0, keepdims=True)
        d_conv = dy_a * b_a
        ext2[pl.ds(0, T), pl.ds(0, D_A)] = d_conv
        ext2[pl.ds(0, T), pl.ds(D_A, D_B)] = dz1

        d_cv = ext2[pl.ds(CONV_A_W - 1, T), pl.ds(0, D_A)] * wa_ref[0:1, :]
        for k in range(1, CONV_A_W):
            d_cv = d_cv + ext2[pl.ds(CONV_A_W - 1 - k, T), pl.ds(0, D_A)] * wa_ref[k:k + 1, :]
        for k in range(CONV_A_W):
            dwa_ref[k:k + 1, :] += jnp.sum(d_conv * ext[pl.ds(HALO - 2 + k, T), pl.ds(0, D_A)], axis=0, keepdims=True)

        _fill_phases(ext2, shb, T + HALO - 8)

        def chunk(ci, carry):
            start = pl.multiple_of(ci * CONV_ROWS, 8)
            z0c = ext[pl.ds(HALO + start, CONV_ROWS), pl.ds(D_A, D_B)]
            acc = jnp.zeros((CONV_ROWS, D_B), F32)
            for k in range(CONV_B_W):
                ahead = _phase_rows(ext2, shb, CONV_B_W - 1 - k, start, CONV_ROWS)
                acc = acc + ahead * wb_ref[k:k + 1, :]
                dwb_acc[k] += _fold8(z0c * ahead)
            dz0buf[pl.ds(start, CONV_ROWS), :] = acc
            return carry

        lax.fori_loop(0, T // CONV_ROWS, chunk, 0)
        dz0 = dz0buf[...]
        du = jnp.concatenate([dy_a * conv_a, d_cv * v_a, d_cv * c_a, dz0 * sg, dz0 * gv * sg * (1.0 - sg)],
                             axis=-1).astype(BF16)
        dh = _mm_nt(du, win_v[...])
        r0 = rx_ref[...]
        xn = x_ref[...] * r0
        dwin_acc[...] += _mm_tn((xn * gpre_ref[...]).astype(BF16), du)
        dxp, dg0 = _rms_bwd(dh, xn, r0, gpre_ref[...])
        dgpre_ref[...] += dg0
        dx_ref[...] = dx1v + dxp
        ext2[pl.ds(T, HALO), :] = ext2[pl.ds(0, HALO), :]

        @pl.when(i == n - 1)
        def _():
            for k in range(CONV_B_W):
                dwb_ref[k:k + 1, :] = jnp.sum(dwb_acc[k], axis=0, keepdims=True)
            win_v[...] = dwin_acc[...].astype(BF16)
            wout_v[...] = dwout_acc[...].astype(BF16)
            for k in range(N_CHIPS):
                pltpu.sync_copy(win_v.at[:, pl.ds(IN_BLK * k, IN_BLK)], cm1_hbm.at[k])
                pltpu.sync_copy(wout_v.at[pl.ds(ROW_BLK * k, ROW_BLK), :], cm2_hbm.at[k])

    rev = lambda w: pl.BlockSpec((T, w), lambda i: (n - 1 - i, 0))
    halo = pl.BlockSpec((HALO, D_IN_ALL), lambda i: (jnp.maximum((n - 1 - i) * hb - 1, 0), 0))
    return pl.pallas_call(
        body,
        name="mix_bwd",
        grid=(n,),
        in_specs=[rev(D), rev(D), rev(D), rev(D_IN_ALL), halo, rev(D_B), rev(1), rev(1), _full((1, D)),
                  _full((1, D)), _full((CONV_A_W, D_A)), _full((CONV_B_W, D_B)), _full((1, D_B)), _full((1, D_B)),
                  ANY, ANY, ANY],
        out_specs=[rev(D), _full((1, D)), _full((1, D)), _full((CONV_A_W, D_A)), _full((CONV_B_W, D_B)),
                   _full((1, D_B)), _full((1, D_B)), _full((1, D_B)), ANY, ANY],
        out_shape=[jax.ShapeDtypeStruct((S, D), F32), jax.ShapeDtypeStruct((1, D), F32),
                   jax.ShapeDtypeStruct((1, D), F32), jax.ShapeDtypeStruct((CONV_A_W, D_A), F32),
                   jax.ShapeDtypeStruct((CONV_B_W, D_B), F32), jax.ShapeDtypeStruct((1, D_B), F32),
                   jax.ShapeDtypeStruct((1, D_B), F32), jax.ShapeDtypeStruct((1, D_B), F32),
                   jax.ShapeDtypeStruct((N_CHIPS, D, IN_BLK), BF16),
                   jax.ShapeDtypeStruct((N_CHIPS, ROW_BLK, D), BF16)],
        scratch_shapes=[pltpu.VMEM((D, D_IN_ALL), BF16), pltpu.VMEM((D_A + D_B, D), BF16),
                        pltpu.VMEM((D, D_IN_ALL), F32), pltpu.VMEM((D_A + D_B, D), F32),
                        pltpu.VMEM((HALO + T, D_A + D_B), F32), pltpu.VMEM((HALO + T, D_A + D_B), F32),
                        pltpu.VMEM((8, HALO + T, D_B), F32),
                        pltpu.VMEM((T, D_B), F32), pltpu.VMEM((CONV_B_W, 8, D_B), F32)],
        compiler_params=_params(("arbitrary",)),
    )(dx1, x, o1, u, u, z1s, ro1, rx, gpre, gpost, wa, wb, lng, lnb, g1g, pr0g, after)


def _mem_kv(mem, gmem, pr1g):
    M, D = mem.shape

    def body(mem_ref, g_ref, pr1g_hbm, memn_ref, k_ref, v_ref, wk_v, wv_v):
        _load_rows(wk_v, pr1g_hbm, P1_K, ROW_BLK)
        _load_rows(wv_v, pr1g_hbm, P1_V, ROW_BLK)
        mn, _ = _rms(mem_ref[...])
        mb = (mn * g_ref[...]).astype(BF16)
        memn_ref[...] = mb
        k_ref[...] = _mm(mb, wk_v[...]).astype(BF16)
        v_ref[...] = _mm(mb, wv_v[...]).astype(BF16)

    return pl.pallas_call(
        body,
        name="mem_kv",
        grid=(1,),
        in_specs=[_full((M, D)), _full((1, D)), ANY],
        out_specs=[_full((M, D))] * 3,
        out_shape=[jax.ShapeDtypeStruct((M, D), BF16)] * 3,
        scratch_shapes=[pltpu.VMEM((D, D), BF16), pltpu.VMEM((D, D), BF16)],
        compiler_params=_params(("arbitrary",)),
    )(mem, gmem, pr1g)


def _attend(qb, kb, vb):
    scale = HEAD_DIM ** -0.5
    ps, os_ = [], []
    for hd in range(XA_HEADS):
        cols = slice(HEAD_DIM * hd, HEAD_DIM * (hd + 1))
        s = _mm_nt(qb[:, cols], kb[:, cols]) * scale
        e = jnp.exp(s - jnp.max(s, axis=-1, keepdims=True))
        p = e * (1.0 / jnp.sum(e, axis=-1, keepdims=True))
        ps.append(p)
        os_.append(_mm(p.astype(BF16), vb[:, cols]))
    return ps, jnp.concatenate(os_, axis=-1).astype(BF16)


def _xattn_fwd(x1, gpre, gpost, kb, vb, pr1g):
    S, D = x1.shape
    M = kb.shape[0]
    T = min(TILE_XATTN_FWD, S)
    n = S // T

    def body(x1_ref, gpre_ref, gpost_ref, k_ref, v_ref, pr1g_hbm, x2_ref, q_ref, o2_ref, rx_ref, ro_ref, wq_v, wo_v):
        @pl.when(pl.program_id(0) == 0)
        def _():
            _load_rows(wq_v, pr1g_hbm, P1_Q, ROW_BLK)
            _load_rows(wo_v, pr1g_hbm, P1_O, ROW_BLK)

        xv = x1_ref[...]
        xn, rx = _rms(xv)
        rx_ref[...] = rx
        qb = _mm((xn * gpre_ref[...]).astype(BF16), wq_v[...]).astype(BF16)
        q_ref[...] = qb
        _, ob = _attend(qb, k_ref[...], v_ref[...])
        o2 = _mm(ob, wo_v[...])
        o2_ref[...] = o2.astype(BF16)
        o2n, ro = _rms(o2)
        ro_ref[...] = ro
        x2_ref[...] = xv + o2n * gpost_ref[...]

    tok = lambda w: pl.BlockSpec((T, w), lambda i: (i, 0))
    return pl.pallas_call(
        body,
        name="xattn_fwd",
        grid=(n,),
        in_specs=[tok(D), _full((1, D)), _full((1, D)), _full((M, D)), _full((M, D)), ANY],
        out_specs=[tok(D), tok(D), tok(D), tok(1), tok(1)],
        out_shape=[jax.ShapeDtypeStruct((S, D), F32), jax.ShapeDtypeStruct((S, D), BF16),
                   jax.ShapeDtypeStruct((S, D), BF16), jax.ShapeDtypeStruct((S, 1), F32),
                   jax.ShapeDtypeStruct((S, 1), F32)],
        scratch_shapes=[pltpu.VMEM((D, D), BF16), pltpu.VMEM((D, D), BF16)],
        compiler_params=_params(("arbitrary",)),
    )(x1, gpre, gpost, kb, vb, pr1g)


def _xattn_bwd(dx3, dh3p, x2, x1, o2, q, rx2, ro2, rx1, kb, vb, gffn, gpost, gpre, pr1g, cf1, cf2):
    S, D = x1.shape
    M = kb.shape[0]
    T = min(TILE_XATTN_BWD, S)
    n = S // T
    scale = HEAD_DIM ** -0.5
    nparts = dh3p.shape[0]
    cfs = [_by_halves(cf1), _by_halves(cf2)]

    def body(*refs):
        dx3_ref, dh3_refs = refs[0], refs[1:1 + nparts]
        (x2_ref, x1_ref, o2_ref, q_ref, rx2_ref, ro2_ref, rx1_ref, k_ref, v_ref, gffn_ref, gpost_ref, gpre_ref,
         pr1g_hbm, cf1_hbm, cf2_hbm,
         dx1_ref, dgffn_ref, dgpost_ref, dgpre_ref, dk_ref, dv_ref, cx_hbm, yf1_hbm, yf2_hbm,
         wq_v, wo_v, dwq_acc, dwo_acc, send, recv) = refs[1 + nparts:]
        i = pl.program_id(0)

        @pl.when(i == 0)
        def _():
            _scatter_start([cf1_hbm, cf2_hbm], [yf1_hbm, yf2_hbm], send, recv)
            _load_rows(wq_v, pr1g_hbm, P1_Q, ROW_BLK)
            _load_rows(wo_v, pr1g_hbm, P1_O, ROW_BLK)
            dwq_acc[...] = jnp.zeros_like(dwq_acc)
            dwo_acc[...] = jnp.zeros_like(dwo_acc)
            for ref in (dgffn_ref, dgpost_ref, dgpre_ref, dk_ref, dv_ref):
                ref[...] = jnp.zeros_like(ref)

        r2 = rx2_ref[...]
        x2n = x2_ref[...] * r2
        dh3 = dh3_refs[0][...].astype(F32)
        for ref in dh3_refs[1:]:
            dh3 = dh3 + ref[...].astype(F32)
        dxp, dg = _rms_bwd(dh3, x2n, r2, gffn_ref[...])
        dgffn_ref[...] += dg
        dx2 = dx3_ref[...] + dxp
        ro = ro2_ref[...]
        o2n = o2_ref[...].astype(F32) * ro
        d_o2, dg = _rms_bwd(dx2, o2n, ro, gpost_ref[...])
        dgpost_ref[...] += dg
        d_o2b = d_o2.astype(BF16)
        d_o = _mm_nt(d_o2b, wo_v[...]).astype(BF16)
        qb = q_ref[...]
        kv = k_ref[...]
        vv = v_ref[...]
        ps, ob = _attend(qb, kv, vv)
        dwo_acc[...] += _mm_tn(ob, d_o2b)
        dqs = []
        for hd in range(XA_HEADS):
            cols = slice(HEAD_DIM * hd, HEAD_DIM * (hd + 1))
            p = ps[hd]
            dp = _mm_nt(d_o[:, cols], vv[:, cols])
            dv_ref[:, cols] += _mm_tn(p.astype(BF16), d_o[:, cols])
            ds = (p * (dp - jnp.sum(p * dp, axis=-1, keepdims=True)) * scale).astype(BF16)
            dqs.append(_mm(ds, kv[:, cols]))
            dk_ref[:, cols] += _mm_tn(ds, qb[:, cols])
        dq = jnp.concatenate(dqs, axis=-1).astype(BF16)
        dh2 = _mm_nt(dq, wq_v[...])
        r1 = rx1_ref[...]
        x1n = x1_ref[...] * r1
        dwq_acc[...] += _mm_tn((x1n * gpre_ref[...]).astype(BF16), dq)
        dxp, dg = _rms_bwd(dh2, x1n, r1, gpre_ref[...])
        dgpre_ref[...] += dg
        dx1_ref[...] = dx2 + dxp

        @pl.when(i == n - 1)
        def _():
            wq_v[...] = dwq_acc[...].astype(BF16)
            wo_v[...] = dwo_acc[...].astype(BF16)
            for k in range(N_CHIPS):
                rows = pl.ds(ROW_BLK * k, ROW_BLK)
                pltpu.sync_copy(wq_v.at[rows, :], cx_hbm.at[k, pl.ds(GX_Q, ROW_BLK), :])
                pltpu.sync_copy(wo_v.at[rows, :], cx_hbm.at[k, pl.ds(GX_O, ROW_BLK), :])
            _scatter_wait([cf1_hbm, cf2_hbm], [yf1_hbm, yf2_hbm], send, recv)

    tok = lambda w: pl.BlockSpec((T, w), lambda i: (i, 0))
    part = lambda j: pl.BlockSpec((None, T, D), lambda i: (j, i, 0))
    return pl.pallas_call(
        body,
        name="xattn_bwd",
        grid=(n,),
        in_specs=[tok(D)] + [part(j) for j in range(nparts)] + [tok(D), tok(D), tok(D), tok(D), tok(1), tok(1), tok(1),
                                                                 _full((M, D)), _full((M, D)), _full((1, D)),
                                                                 _full((1, D)), _full((1, D)), ANY, ANY, ANY],
        out_specs=[tok(D), _full((1, D)), _full((1, D)), _full((1, D)), _full((M, D)), _full((M, D)), ANY, ANY, ANY],
        out_shape=[jax.ShapeDtypeStruct((S, D), F32), jax.ShapeDtypeStruct((1, D), F32),
                   jax.ShapeDtypeStruct((1, D), F32), jax.ShapeDtypeStruct((1, D), F32),
                   jax.ShapeDtypeStruct((M, D), F32), jax.ShapeDtypeStruct((M, D), F32),
                   jax.ShapeDtypeStruct((N_CHIPS, D, D), BF16), _slots(cfs[0]), _slots(cfs[1])],
        scratch_shapes=[pltpu.VMEM((D, D), BF16), pltpu.VMEM((D, D), BF16),
                        pltpu.VMEM((D, D), F32), pltpu.VMEM((D, D), F32)] + _scatter_sems(2),
        compiler_params=_params(("arbitrary",)),
    )(dx3, *([dh3p] * nparts), x2, x1, o2, q, rx2, ro2, rx1, kb, vb, gffn, gpost, gpre, pr1g, *cfs)


def _mem_bwd(dk, dv, mem, memn, gmem, pr1g, cx_in):
    M, D = mem.shape

    def body(dk_ref, dv_ref, mem_ref, memn_ref, g_ref, pr1g_hbm, cx_hbm, dg_ref, cx_out, wk_v, wv_v):
        del cx_hbm
        _load_rows(wk_v, pr1g_hbm, P1_K, ROW_BLK)
        _load_rows(wv_v, pr1g_hbm, P1_V, ROW_BLK)
        dkb = dk_ref[...].astype(BF16)
        dvb = dv_ref[...].astype(BF16)
        mb = memn_ref[...]
        dmn = _mm_nt(dkb, wk_v[...]) + _mm_nt(dvb, wv_v[...])
        mn, _ = _rms(mem_ref[...])
        dg_ref[...] = jnp.sum(dmn * mn, axis=0, keepdims=True)
        wk_v[...] = _mm_tn(mb, dkb).astype(BF16)
        wv_v[...] = _mm_tn(mb, dvb).astype(BF16)
        for k in range(N_CHIPS):
            rows = pl.ds(ROW_BLK * k, ROW_BLK)
            pltpu.sync_copy(wk_v.at[rows, :], cx_out.at[k, pl.ds(GX_K, ROW_BLK), :])
            pltpu.sync_copy(wv_v.at[rows, :], cx_out.at[k, pl.ds(GX_V, ROW_BLK), :])

    return pl.pallas_call(
        body,
        name="mem_bwd",
        grid=(1,),
        in_specs=[_full((M, D)), _full((M, D)), _full((M, D)), _full((M, D)), _full((1, D)), ANY, ANY],
        out_specs=[_full((1, D)), ANY],
        out_shape=[jax.ShapeDtypeStruct((1, D), F32), jax.ShapeDtypeStruct(cx_in.shape, BF16)],
        input_output_aliases={6: 1},
        scratch_shapes=[pltpu.VMEM((D, D), BF16), pltpu.VMEM((D, D), BF16)],
        compiler_params=_params(("arbitrary",)),
    )(dk, dv, mem, memn, gmem, pr1g, cx_in)


def _ffn_fwd(x2, target, gpre, gpost, g2g, pr1g):
    S, D = x2.shape
    T = min(TILE_FFN, S)
    n = S // T

    def body(x2_ref, t_ref, gpre_ref, gpost_ref, g2g_hbm, pr1g_hbm,
             h3_ref, g_hbm, u_hbm, do3_ref, dx3_ref, loss_ref, dgpost_ref, rx_ref, wg_v, wu_v, wd_v, gst, ust, sem):
        i = pl.program_id(0)

        @pl.when(i == 0)
        def _():
            pltpu.sync_copy(g2g_hbm.at[:, pl.ds(0, D), :], wg_v)
            pltpu.sync_copy(g2g_hbm.at[:, pl.ds(D, D), :], wu_v)
            pltpu.sync_copy(pr1g_hbm.at[:, pl.ds(P1_DOWN, FF_BLK), :], wd_v)
            loss_ref[...] = jnp.zeros_like(loss_ref)
            dgpost_ref[...] = jnp.zeros_like(dgpost_ref)

        xv = x2_ref[...]
        xn, rx = _rms(xv)
        rx_ref[...] = rx
        hb = (xn * gpre_ref[...]).astype(BF16)
        h3_ref[...] = hb
        o3 = jnp.zeros((T, D), F32)
        out = [None, None]
        for c in range(N_CHIPS):
            slot = c % 2
            if out[slot] is not None:
                for cp in out[slot]:
                    cp.wait()
            g = _mm(hb, wg_v[c])
            u = _mm(hb, wu_v[c])
            gst[slot] = g.astype(BF16)
            ust[slot] = u.astype(BF16)
            out[slot] = (pltpu.make_async_copy(gst.at[slot], g_hbm.at[c, i], sem.at[0, slot]),
                         pltpu.make_async_copy(ust.at[slot], u_hbm.at[c, i], sem.at[1, slot]))
            for cp in out[slot]:
                cp.start()
            o3 = o3 + _mm((g * _sigmoid(g) * u).astype(BF16), wd_v[c])
        for pair in out:
            for cp in pair:
                cp.wait()
        o3n, r3 = _rms(o3)
        diff = xv + o3n * gpost_ref[...] - t_ref[...]
        sq = jnp.sum(jnp.sum(diff * diff, axis=-1, keepdims=True), axis=0, keepdims=True)
        loss_ref[...] += sq * (0.5 / D)
        dx3 = diff * (1.0 / D)
        dx3_ref[...] = dx3
        d_o3, dg = _rms_bwd(dx3, o3n, r3, gpost_ref[...])
        dgpost_ref[...] += dg
        do3_ref[...] = d_o3.astype(BF16)

    tok = lambda w: pl.BlockSpec((T, w), lambda i: (i, 0))
    h3, gs, us, do3, dx3, loss, dgpost, rx2 = pl.pallas_call(
        body,
        name="ffn_fwd",
        grid=(n,),
        in_specs=[tok(D), tok(D), _full((1, D)), _full((1, D)), ANY, ANY],
        out_specs=[tok(D), ANY, ANY, tok(D), tok(D), _full((1, 128)), _full((1, D)), tok(1)],
        out_shape=[jax.ShapeDtypeStruct((S, D), BF16), jax.ShapeDtypeStruct((N_CHIPS, n, T, FF_BLK), BF16),
                   jax.ShapeDtypeStruct((N_CHIPS, n, T, FF_BLK), BF16), jax.ShapeDtypeStruct((S, D), BF16),
                   jax.ShapeDtypeStruct((S, D), F32), jax.ShapeDtypeStruct((1, 128), F32),
                   jax.ShapeDtypeStruct((1, D), F32), jax.ShapeDtypeStruct((S, 1), F32)],
        scratch_shapes=[pltpu.VMEM((N_CHIPS, D, FF_BLK), BF16), pltpu.VMEM((N_CHIPS, D, FF_BLK), BF16),
                        pltpu.VMEM((N_CHIPS, FF_BLK, D), BF16), pltpu.VMEM((2, T, FF_BLK), BF16),
                        pltpu.VMEM((2, T, FF_BLK), BF16), pltpu.SemaphoreType.DMA((2, 2))],
        compiler_params=_params(("arbitrary",)),
    )(x2, target, gpre, gpost, g2g, pr1g)
    return h3, gs.reshape(N_CHIPS, S, FF_BLK), us.reshape(N_CHIPS, S, FF_BLK), do3, dx3, loss, dgpost, rx2


def _ffn_bwd(h3, do3, gs, us, g2g, pr1g):
    S, D = h3.shape
    T = min(TILE_FFN_BWD, S)
    n = S // T
    NP = FFN_BWD_BLOCKS

    def body(h3_ref, do3_ref, g_ref, u_ref, g2g_hbm, pr1g_hbm, dh3_ref, cf1_hbm, cf2_hbm,
             wg_v, wu_v, wd_v, dwg_acc, dwu_acc, dwd_acc):
        jp = pl.program_id(0)
        i = pl.program_id(1)
        blocks = pl.ds(NP * jp, NP)

        @pl.when(i == 0)
        def _():
            pltpu.sync_copy(g2g_hbm.at[blocks, pl.ds(0, D), :], wg_v)
            pltpu.sync_copy(g2g_hbm.at[blocks, pl.ds(D, D), :], wu_v)
            pltpu.sync_copy(pr1g_hbm.at[blocks, pl.ds(P1_DOWN, FF_BLK), :], wd_v)
            dwg_acc[...] = jnp.zeros_like(dwg_acc)
            dwu_acc[...] = jnp.zeros_like(dwu_acc)
            dwd_acc[...] = jnp.zeros_like(dwd_acc)

        hb = h3_ref[...]
        d_o3 = do3_ref[...]
        dh = jnp.zeros((T, D), F32)
        for c in range(NP):
            da = _mm_nt(d_o3, wd_v[c])
            g = g_ref[c].astype(F32)
            u = u_ref[c].astype(F32)
            sg = _sigmoid(g)
            sl = g * sg
            dwd_acc[c] += _mm_tn((sl * u).astype(BF16), d_o3)
            dub = (da * sl).astype(BF16)
            dgb = (da * u * (sg * (1.0 + g * (1.0 - sg)))).astype(BF16)
            dwg_acc[c] += _mm_tn(dgb, hb)
            dwu_acc[c] += _mm_tn(dub, hb)
            dh = dh + _mm_nt(dgb, wg_v[c]) + _mm_nt(dub, wu_v[c])
        dh3_ref[...] = dh.astype(BF16)

        @pl.when(i == n - 1)
        def _():
            wd_v[...] = dwg_acc[...].astype(BF16)
            pltpu.sync_copy(wd_v, cf1_hbm.at[blocks, pl.ds(0, FF_BLK), :])
            wd_v[...] = dwu_acc[...].astype(BF16)
            pltpu.sync_copy(wd_v, cf1_hbm.at[blocks, pl.ds(FF_BLK, FF_BLK), :])
            wd_v[...] = dwd_acc[...].astype(BF16)
            pltpu.sync_copy(wd_v, cf2_hbm.at[blocks])

    tok = lambda w: pl.BlockSpec((T, w), lambda jp, i: (i, 0))
    blk = pl.BlockSpec((NP, T, FF_BLK), lambda jp, i: (jp, i, 0))
    return pl.pallas_call(
        body,
        name="ffn_bwd",
        grid=(N_CHIPS // NP, n),
        in_specs=[tok(D), tok(D), blk, blk, ANY, ANY],
        out_specs=[pl.BlockSpec((None, T, D), lambda jp, i: (jp, i, 0)), ANY, ANY],
        out_shape=[jax.ShapeDtypeStruct((N_CHIPS // NP, S, D), BF16),
                   jax.ShapeDtypeStruct((N_CHIPS, 2 * FF_BLK, D), BF16),
                   jax.ShapeDtypeStruct((N_CHIPS, FF_BLK, D), BF16)],
        scratch_shapes=[pltpu.VMEM((NP, D, FF_BLK), BF16), pltpu.VMEM((NP, D, FF_BLK), BF16),
                        pltpu.VMEM((NP, FF_BLK, D), BF16), pltpu.VMEM((NP, FF_BLK, D), F32),
                        pltpu.VMEM((NP, FF_BLK, D), F32), pltpu.VMEM((NP, FF_BLK, D), F32)],
        compiler_params=_params(("arbitrary", "arbitrary")),
    )(h3, do3, gs, us, g2g, pr1g)


IN_HBM = pl.BlockSpec(memory_space=pltpu.HBM)
IN_SEMAPHORES = pl.BlockSpec(memory_space=pltpu.SEMAPHORE)


def _split_copies(refs, na, nw, sems):
    srcs, lands = refs[:na + nw], refs[na + nw:2 * (na + nw)]
    send, recv = sems
    x, y, c, _ = _my_place()
    me = 4 * x + 2 * y + c
    pairs = []
    for d, ((px, py, pc), pidx) in enumerate(_peers(x, y, c)):
        for a in range(na + nw):
            k = d * (na + nw) + a
            src = srcs[a].at[2 * px + py, pc] if a < na else srcs[a]
            pairs.append((_remote(src, lands[a].at[me], send.at[k], recv.at[k], (px, py, pc)),
                          _remote(src, lands[a].at[pidx], send.at[k], recv.at[k], (px, py, pc))))
    return pairs


def _exchange_start(contribs, whole, tag):
    na, nw = len(contribs), len(whole)
    cs = [_by_halves(a) for a in contribs]
    flying = cs + list(whole) + [lax.empty(_slots(a).shape, a.dtype) for a in cs] + [
        lax.empty((N_DEV,) + a.shape, a.dtype) for a in whole]
    nf = len(flying)

    def body(*refs):
        for mine, _ in _split_copies(refs[:nf], na, nw, refs[nf:nf + 2]):
            mine.start()
        refs[-1][...] = jnp.zeros_like(refs[-1])

    ncopies = (N_DEV - 1) * (na + nw)
    outs = pl.pallas_call(
        body,
        name="exchange_start_" + tag,
        in_specs=[IN_HBM] * nf,
        out_specs=[IN_SEMAPHORES] * 2 + [IN_HBM] * nf + [pl.BlockSpec(memory_space=pltpu.VMEM)],
        out_shape=[pltpu.SemaphoreType.DMA((ncopies,)), pltpu.SemaphoreType.DMA((ncopies,))]
        + [pltpu.HBM(a.shape, a.dtype) for a in flying] + [jax.ShapeDtypeStruct((8, 128), F32)],
        input_output_aliases={k: 2 + k for k in range(nf)},
        compiler_params=pltpu.CompilerParams(has_side_effects=pltpu.SideEffectType.DATAFLOW_SIDE_EFFECTING),
    )(*[pltpu.with_memory_space_constraint(a, pltpu.HBM) for a in flying])
    return outs[:2], outs[2:2 + nf], outs[-1]


def _exchange_wait(sems, flying, na, nw, after, tag):
    nf = len(flying)

    def body(*refs):
        for mine, theirs in _split_copies(refs[:nf], na, nw, refs[nf:nf + 2]):
            theirs.wait_recv()
            mine.wait_send()

    outs = pl.pallas_call(
        body,
        name="exchange_wait_" + tag,
        in_specs=[IN_HBM] * nf + [IN_SEMAPHORES] * 2 + [ANY] * len(after),
        out_specs=[IN_HBM] * nf,
        out_shape=[pltpu.HBM(a.shape, a.dtype) for a in flying],
        input_output_aliases={k: k for k in range(nf)},
        compiler_params=pltpu.CompilerParams(has_side_effects=pltpu.SideEffectType.DATAFLOW_SIDE_EFFECTING),
    )(*flying, *sems, *after)
    return outs[na + nw:]


def _sum_peers(parts, own, place, steps, tag, after=()):
    _, rows, w = parts.shape
    tr = rows // steps
    if own.ndim == 3:
        own = _by_halves(own)

    def body(place_ref, *refs):
        p_refs, own_ref, o_ref = refs[:N_DEV], refs[N_DEV], refs[-1]
        me = place_ref[2]
        acc = None
        for s in range(N_DEV):
            term = jnp.where(me == s, own_ref[...], p_refs[s][...]).astype(F32)
            acc = term if acc is None else acc + term
        o_ref[...] = acc

    def other(s):
        return lambda i, pr: (jnp.where(pr[2] == s, (s + 1) % N_DEV, s), i, 0)

    own_spec = (pl.BlockSpec((None, None, tr, w), lambda i, pr: (pr[0], pr[1], i, 0)) if own.ndim == 4 else
                pl.BlockSpec((tr, w), lambda i, pr: (i, 0)))
    out_spec = (pl.BlockSpec((None, tr, w), lambda i, pr: (pr[1], i, 0)) if own.ndim == 4 else
                pl.BlockSpec((tr, w), lambda i, pr: (i, 0)))
    return pl.pallas_call(
        body,
        name="sum_peers_" + tag,
        grid_spec=pltpu.PrefetchScalarGridSpec(
            num_scalar_prefetch=1,
            grid=(steps,),
            in_specs=[pl.BlockSpec((None, tr, w), other(s)) for s in range(N_DEV)] + [own_spec] + [ANY] * len(after),
            out_specs=out_spec,
        ),
        out_shape=jax.ShapeDtypeStruct((2, rows, w) if own.ndim == 4 else (rows, w), F32),
        compiler_params=_params(("arbitrary",)),
    )(place, *([parts] * N_DEV), own, *after)


def _pair_gather(bufs, tag):
    np_ = len(bufs)

    def body(*refs):
        srcs, dsts = refs[:np_], refs[np_:2 * np_]
        send, recv = refs[2 * np_:]
        x, y, c, _ = _my_place()
        cps = []
        for p in range(np_):
            cp = _remote(srcs[p].at[c], dsts[p].at[c], send.at[p], recv.at[p], (x, y, 1 - c))
            cp.start()
            cps.append(cp)
        for p, cp in enumerate(cps):
            other = dsts[p].at[1 - c]
            _remote(other, other, send.at[p], recv.at[p], (x, y, 1 - c)).wait_recv()
            cp.wait_send()

    outs = pl.pallas_call(
        body,
        name="pair_gather_" + tag,
        in_specs=[ANY] * np_,
        out_specs=[ANY] * np_,
        out_shape=[jax.ShapeDtypeStruct(a.shape, F32) for a in bufs],
        input_output_aliases={p: p for p in range(np_)},
        scratch_shapes=[pltpu.SemaphoreType.DMA((np_,))] * 2,
    )(*bufs)
    return [o.reshape(2 * a.shape[1], a.shape[2]) for o, a in zip(outs, bufs)]


def _adamw(gsrc, row0, w, m, v, tr, tag):
    rows, width = w.shape
    off = row0 // tr
    bc1 = 1.0 - ADAM_B1 ** ADAM_STEP
    bc2 = 1.0 - ADAM_B2 ** ADAM_STEP

    def body(g_ref, w_ref, m_ref, v_ref, go_ref, d_ref, mo_ref, vo_ref):
        g = g_ref[...]
        m2 = ADAM_B1 * m_ref[...] + (1.0 - ADAM_B1) * g
        v2 = ADAM_B2 * v_ref[...] + (1.0 - ADAM_B2) * (g * g)
        go_ref[...] = g
        mo_ref[...] = m2
        vo_ref[...] = v2
        d_ref[...] = -ADAM_LR * ((m2 / bc1) / (jnp.sqrt(v2 / bc2) + ADAM_EPS) + ADAM_WD * w_ref[...])

    here = pl.BlockSpec((tr, width), lambda i: (i, 0))
    return pl.pallas_call(
        body,
        name="adamw_" + tag,
        grid=(rows // tr,),
        in_specs=[pl.BlockSpec((tr, width), lambda i: (off + i, 0)), here, here, here],
        out_specs=[here] * 4,
        out_shape=[jax.ShapeDtypeStruct((rows, width), F32)] * 4,
        compiler_params=_params(("arbitrary",)),
    )(gsrc, w, m, v)


def kernel(x, mem, mix_pre_g, w_mix_in, conv_a_w, conv_b_w, conv_b_b, ln_b_g, ln_b_b, w_mix_out, mix_post_g, xa_pre_g, mem_norm_g, w_q, w_k, w_v, w_o, xa_post_g, ffn_pre_g, w_gate, w_up, w_down, ffn_post_g, loss_target, m_mix_pre_g, m_w_mix_in, m_conv_a_w, m_conv_b_w, m_conv_b_b, m_ln_b_g, m_ln_b_b, m_w_mix_out, m_mix_post_g, m_xa_pre_g, m_mem_norm_g, m_w_q, m_w_k, m_w_v, m_w_o, m_xa_post_g, m_ffn_pre_g, m_w_gate, m_w_up, m_w_down, m_ffn_post_g, v_mix_pre_g, v_w_mix_in, v_conv_a_w, v_conv_b_w, v_conv_b_b, v_ln_b_g, v_ln_b_b, v_w_mix_out, v_mix_post_g, v_xa_pre_g, v_mem_norm_g, v_w_q, v_w_k, v_w_v, v_w_o, v_xa_post_g, v_ffn_pre_g, v_w_gate, v_w_up, v_w_down, v_ffn_post_g):
    given = dict(locals())
    names = ["mix_pre_g", "w_mix_in", "conv_a_w", "conv_b_w", "conv_b_b", "ln_b_g", "ln_b_b", "w_mix_out",
             "mix_post_g", "xa_pre_g", "mem_norm_g", "w_q", "w_k", "w_v", "w_o", "xa_post_g", "ffn_pre_g",
             "w_gate", "w_up", "w_down", "ffn_post_g"]
    row = lambda a: a.reshape(1, -1)
    cx, cy, cc = lax.axis_index("x"), lax.axis_index("y"), lax.axis_index("c")
    chip = 2 * cx + cy
    ca_blk = conv_a_w.shape[1]

    conv_rows = CONV_A_W + CONV_B_W
    sw = jnp.concatenate([conv_a_w, conv_b_w, jnp.zeros((SMALL_W_ROWS - conv_rows, ca_blk), F32)], axis=0)
    g1g, pr0g, swg = _gather_weights([w_mix_in.astype(BF16), w_mix_out.astype(BF16), sw])
    conv_full = jnp.transpose(swg[:, :conv_rows, :], (1, 0, 2)).reshape(conv_rows, N_CHIPS * ca_blk)
    wa, wb = conv_full[:CONV_A_W], conv_full[CONV_A_W:]
    pr1 = jnp.concatenate([w_q, w_k, w_v, w_o, w_down], axis=0).astype(BF16)
    g2 = jnp.concatenate([w_gate, w_up], axis=0).astype(BF16)

    xs, ms, tgt = x[0], mem[0], loss_target[0]
    x1, u, o1, z1, rx, ro1, pr1g, g2g = _mix_fwd(xs, row(mix_pre_g), row(mix_post_g), wa, wb, row(conv_b_b),
                                                 row(ln_b_g), row(ln_b_b), g1g, pr0g, [pr1, g2])
    memn, kb, vb = _mem_kv(ms, row(mem_norm_g), pr1g)
    x2, q, o2, rx1, ro2 = _xattn_fwd(x1, row(xa_pre_g), row(xa_post_g), kb, vb, pr1g)
    h3, gs, us, do3, dx3, loss_part, d_ffn_post, rx2 = _ffn_fwd(x2, tgt, row(ffn_pre_g), row(ffn_post_g), g2g, pr1g)

    dh3p, cf1, cf2 = _ffn_bwd(h3, do3, gs, us, g2g, pr1g)
    dx1, d_ffn_pre, d_xa_post, d_xa_pre, dk, dv, cxa, yf1, yf2 = _xattn_bwd(
        dx3, dh3p, x2, x1, o2, q, rx2, ro2, rx1, kb, vb, row(ffn_pre_g), row(xa_post_g), row(xa_pre_g), pr1g, cf1, cf2)
    d_mem_g, cxa = _mem_bwd(dk, dv, ms, memn, row(mem_norm_g), pr1g, cxa)
    sems_x, flying_x, token_x = _exchange_start([cxa], [], "attn")
    dx, d_mix_pre, d_mix_post, dwa, dwb, dbb, dlng, dlnb, cm1, cm2 = _mix_bwd(
        dx1, xs, o1, u, z1, ro1, rx, row(mix_pre_g), row(mix_post_g), wa, wb, row(ln_b_g), row(ln_b_b), g1g, pr0g,
        token_x)
    (yx,) = _exchange_wait(sems_x, flying_x, 1, 0, [d_mix_pre], "attn")

    small_parts = [d_mix_pre, dwa, dwb, dbb, dlng, dlnb, d_mix_post, d_xa_pre, d_mem_g, d_xa_post, d_ffn_pre,
                   d_ffn_post, loss_part]
    sizes = [p.size for p in small_parts]
    small = jnp.concatenate([p.reshape(-1) for p in small_parts])
    small_rows = -(-small.size // (8 * 128)) * 8
    small = jnp.pad(small, (0, small_rows * 128 - small.size)).reshape(small_rows, 128)
    sems, flying, token = _exchange_start([cm1, cm2], [small], "mix")
    place = jnp.stack([chip, cc, 2 * chip + cc]).astype(jnp.int32)

    res = {}

    def update(nm, src, row0, tr, transposed=False):
        view = (lambda a: a.T) if transposed else (lambda a: a)
        outs = _adamw(src, row0, view(given[nm]), view(given["m_" + nm]), view(given["v_" + nm]), tr, nm)
        res[nm] = [view(o) for o in outs]

    early = [(yf1, cf1, "gate_up"), (yf2, cf2, "down"), (yx, cxa, "attn")]
    r_gu, r_down, r_attn = _pair_gather([_sum_peers(y, c, place, 2, t, after=(token,)) for y, c, t in early], "early")
    update("w_gate", r_gu, 0, FF_BLK // 2, transposed=True)
    update("w_up", r_gu, FF_BLK, FF_BLK // 2, transposed=True)
    update("w_down", r_down, 0, FF_BLK // 2)
    for nm, row0 in (("w_q", GX_Q), ("w_k", GX_K), ("w_v", GX_V), ("w_o", GX_O)):
        update(nm, r_attn, row0, ROW_BLK)
    done = [res[nm][1] for nm in ("w_gate", "w_up", "w_down", "w_q", "w_k", "w_v", "w_o")]

    ym1, ym2, small_all = _exchange_wait(sems, flying, 2, 1, done, "mix")
    r_in, r_out = _pair_gather([_sum_peers(y, c, place, 2, t) for y, c, t in
                                [(ym1, cm1, "mix_in"), (ym2, cm2, "mix_out")]], "late")
    update("w_mix_in", r_in, 0, 512)
    update("w_mix_out", r_out, 0, ROW_BLK)
    small_sum = _sum_peers(small_all, small, place, 1, "small").reshape(-1)
    red, pos = [], 0
    for p, sz in zip(small_parts, sizes):
        red.append(small_sum[pos:pos + sz].reshape(p.shape))
        pos += sz
    (r_mix_pre, r_wa, r_wb, r_bb, r_lng, r_lnb, r_mix_post, r_xa_pre, r_mem_g, r_xa_post, r_ffn_pre, r_ffn_post,
     r_loss) = red
    loss = r_loss[0, 0]

    small_grads = {"mix_pre_g": r_mix_pre, "conv_b_b": r_bb, "ln_b_g": r_lng, "ln_b_b": r_lnb,
                   "mix_post_g": r_mix_post, "xa_pre_g": r_xa_pre, "mem_norm_g": r_mem_g, "xa_post_g": r_xa_post,
                   "ffn_pre_g": r_ffn_pre, "ffn_post_g": r_ffn_post,
                   "conv_a_w": lax.dynamic_slice_in_dim(r_wa, chip * ca_blk, ca_blk, axis=1),
                   "conv_b_w": lax.dynamic_slice_in_dim(r_wb, chip * ca_blk, ca_blk, axis=1)}
    small_names = list(small_grads)

    def packed(prefix, grads=None):
        flat = jnp.concatenate([(grads[nm] if grads else given[prefix + nm]).reshape(-1) for nm in small_names])
        rows8 = -(-flat.size // (8 * 128)) * 8
        return jnp.pad(flat, (0, rows8 * 128 - flat.size)).reshape(rows8, 128)

    gp = packed("", small_grads)
    outs = _adamw(gp, 0, packed(""), packed("m_"), packed("v_"), gp.shape[0], "small")
    pos = 0
    for nm in small_names:
        shape = given[nm].shape
        sz = given[nm].size
        res[nm] = [o.reshape(-1)[pos:pos + sz].reshape(shape) for o in outs]
        pos += sz

    return (loss, dx[None], *[res[nm][0] for nm in names], *[res[nm][1] for nm in names],
            *[res[nm][2] for nm in names], *[res[nm][3] for nm in names])
```

```python
import jax
import jax.numpy as jnp
from jax import lax
from jax.experimental import pallas as pl
from jax.experimental.pallas import tpu as pltpu

F32 = jnp.float32
BF16 = jnp.bfloat16
MESH = pl.DeviceIdType.MESH

RMS_EPS = 1e-6
LN_EPS = 1e-5
D_MODEL = 1024
D_A = 512
D_B = 512
D_IN_ALL = 3 * D_A + 2 * D_B
CONV_A_W = 3
CONV_B_W = 31
HALO = 32
XA_HEADS = 4
HEAD_DIM = 256
D_FF = 2816
N_CHIPS = 4
N_DEV = 8
FF_BLK = D_FF // N_CHIPS
IN_BLK = D_IN_ALL // N_CHIPS
ROW_BLK = D_MODEL // N_CHIPS

ADAM_LR = 0.001
ADAM_B1 = 0.9
ADAM_B2 = 0.999
ADAM_EPS = 1e-08
ADAM_WD = 0.01
ADAM_STEP = 10

TILE_FWD = 512
TILE_XATTN_FWD = 1024
TILE_FFN = 512
TILE_FFN_BWD = 512
TILE_XATTN_BWD = 512
TILE_BWD = 256
PHASE_ROWS = 256
FFN_BWD_BLOCKS = 2
CONV_ROWS_FWD = 64
CONV_ROWS = 32
V7X_VMEM_LIMIT = 56 * 1024 * 1024

P1_Q, P1_K, P1_V, P1_O, P1_DOWN = 0, 256, 512, 768, 1024
P1_ROWS = P1_DOWN + FF_BLK
GX_Q, GX_K, GX_V, GX_O = 0, 256, 512, 768
SMALL_W_ROWS = 48

ANY = pl.BlockSpec(memory_space=pl.ANY)


def _mm(a, b):
    return lax.dot_general(a, b, (((1,), (0,)), ((), ())), preferred_element_type=F32)


def _mm_nt(a, b):
    return lax.dot_general(a, b, (((1,), (1,)), ((), ())), preferred_element_type=F32)


def _mm_tn(a, b):
    return lax.dot_general(a, b, (((0,), (0,)), ((), ())), preferred_element_type=F32)


def _sigmoid(x):
    return 0.5 * jnp.tanh(0.5 * x) + 0.5


def _rms(x):
    r = lax.rsqrt(jnp.mean(x * x, axis=-1, keepdims=True) + RMS_EPS)
    return x * r, r


def _rms_bwd(dy, xn, r, g):
    gdy = dy * g
    dx = r * (gdy - xn * jnp.mean(gdy * xn, axis=-1, keepdims=True))
    return dx, jnp.sum(dy * xn, axis=0, keepdims=True)


def _fold8(a):
    out = a[0:8, :]
    for m in range(1, a.shape[0] // 8):
        out = out + a[8 * m:8 * m + 8, :]
    return out


def _full(shape):
    return pl.BlockSpec(shape, lambda *_: (0,) * len(shape))


def _params(sem=None):
    return pltpu.CompilerParams(dimension_semantics=sem, vmem_limit_bytes=V7X_VMEM_LIMIT)


def _copy_all(pairs):
    def scoped(sems):
        copies = [pltpu.make_async_copy(src, dst, sems.at[k]) for k, (src, dst) in enumerate(pairs)]
        for cp in copies:
            cp.start()
        for cp in copies:
            cp.wait()

    pl.run_scoped(scoped, pltpu.SemaphoreType.DMA((len(pairs),)))


def _rows(dst, src_hbm, row0, rows):
    return [(src_hbm.at[k, pl.ds(row0, rows), :], dst.at[pl.ds(rows * k, rows), :]) for k in range(N_CHIPS)]


def _cols(dst, src_hbm, cols):
    return [(src_hbm.at[k], dst.at[:, pl.ds(cols * k, cols)]) for k in range(N_CHIPS)]


def _fill_phases(src, sh, nrows, row0=0):
    for r in range(1, 8):
        sh[r - 1, pl.ds(0, nrows), :] = src[pl.ds(row0 + r, nrows), pl.ds(D_A, D_B)]


def _phase_rows(src, sh, off, start, size, row0=0):
    r = off % 8
    if r == 0:
        return src[pl.ds(off + start, size), pl.ds(D_A, D_B)]
    return sh[r - 1, pl.ds(off - r + start - row0, size), :]


def _my_place():
    x, y, c = lax.axis_index("x"), lax.axis_index("y"), lax.axis_index("c")
    return x, y, c, ((1 - x, y), (x, 1 - y), (1 - x, 1 - y))


def _remote(src, dst, send_sem, recv_sem, to):
    return pltpu.make_async_remote_copy(src_ref=src, dst_ref=dst, send_sem=send_sem, recv_sem=recv_sem,
                                        device_id=to, device_id_type=MESH)


def _gather_sems(np_):
    return [pltpu.SemaphoreType.DMA((np_, 3))] * 4 + [pltpu.SemaphoreType.DMA((np_,))] * 2


def _gather_start(srcs, dsts, sems):
    send, recv, _, _, osend, orecv = sems
    x, y, c, chips = _my_place()
    j = 2 * x + y
    for p in range(len(srcs)):
        _remote(srcs[p], dsts[p].at[j], osend.at[p], orecv.at[p], (x, y, 1 - c)).start()
        for nn, (kx, ky) in enumerate(chips):
            _remote(srcs[p].at[c], dsts[p].at[j, c], send.at[p, nn], recv.at[p, nn], (kx, ky, c)).start()


def _gather_forward(srcs, dsts, sems):
    send, recv, fsend, frecv, _, _ = sems
    x, y, c, chips = _my_place()
    for nn, (kx, ky) in enumerate(chips):
        for p in range(len(srcs)):
            blk = dsts[p].at[2 * kx + ky, c]
            _remote(blk, blk, send.at[p, nn], recv.at[p, nn], (kx, ky, c)).wait_recv()
            _remote(blk, blk, fsend.at[p, nn], frecv.at[p, nn], (x, y, 1 - c)).start()


def _gather_finish(srcs, dsts, sems):
    send, recv, fsend, frecv, osend, orecv = sems
    x, y, c, chips = _my_place()
    j = 2 * x + y
    for nn, (kx, ky) in enumerate(chips):
        for p in range(len(srcs)):
            other = dsts[p].at[2 * kx + ky, 1 - c]
            _remote(other, other, fsend.at[p, nn], frecv.at[p, nn], (x, y, 1 - c)).wait_recv()
    for nn, (kx, ky) in enumerate(chips):
        for p in range(len(srcs)):
            _remote(srcs[p].at[c], dsts[p].at[j, c], send.at[p, nn], recv.at[p, nn], (kx, ky, c)).wait_send()
            blk = dsts[p].at[2 * kx + ky, c]
            _remote(blk, blk, fsend.at[p, nn], frecv.at[p, nn], (x, y, 1 - c)).wait_send()
    for p in range(len(srcs)):
        _remote(srcs[p], dsts[p].at[j], osend.at[p], orecv.at[p], (x, y, 1 - c)).wait()


def _split_halves(a):
    return a.reshape(2, a.shape[0] // 2, a.shape[1])


def _gather_weights(packs):
    np_ = len(packs)
    split = [_split_halves(a) for a in packs]

    def body(*refs):
        srcs, dsts, sems = refs[:np_], refs[np_:2 * np_], refs[2 * np_:]
        _gather_start(srcs, dsts, sems)
        _gather_forward(srcs, dsts, sems)
        _gather_finish(srcs, dsts, sems)

    outs = pl.pallas_call(
        body,
        name="gather_weights",
        in_specs=[ANY] * np_,
        out_specs=[ANY] * np_,
        out_shape=[jax.ShapeDtypeStruct((N_CHIPS,) + a.shape, a.dtype) for a in split],
        scratch_shapes=_gather_sems(np_),
    )(*split)
    return [o.reshape((N_CHIPS,) + a.shape) for o, a in zip(outs, packs)]


def _peers(x, y, c):
    out = []
    for d in range(1, N_DEV):
        px = 1 - x if d & 4 else x
        py = 1 - y if d & 2 else y
        pc = 1 - c if d & 1 else c
        out.append(((px, py, pc), 4 * px + 2 * py + pc))
    return out


def _scatter_copies(srcs, dsts, send, recv):
    x, y, c, _ = _my_place()
    me = 4 * x + 2 * y + c
    out = []
    for a in range(len(srcs)):
        for d, ((px, py, pc), pidx) in enumerate(_peers(x, y, c)):
            piece = srcs[a].at[2 * px + py, pc]
            out.append((_remote(piece, dsts[a].at[me], send.at[a, d], recv.at[a, d], (px, py, pc)),
                        _remote(piece, dsts[a].at[pidx], send.at[a, d], recv.at[a, d], (px, py, pc))))
    return out


def _scatter_start(srcs, dsts, send, recv):
    for out_cp, _ in _scatter_copies(srcs, dsts, send, recv):
        out_cp.start()


def _scatter_wait(srcs, dsts, send, recv):
    for out_cp, in_cp in _scatter_copies(srcs, dsts, send, recv):
        in_cp.wait_recv()
        out_cp.wait_send()


def _scatter_sems(na):
    return [pltpu.SemaphoreType.DMA((na, N_DEV - 1))] * 2


def _slots(a):
    return jax.ShapeDtypeStruct((N_DEV,) + a.shape[2:], a.dtype)


def _by_halves(a):
    return a.reshape(a.shape[0], 2, a.shape[1] // 2, a.shape[2])


def _mix_fwd(x, gpre, gpost, wa, wb, bb, lng, lnb, g1g, pr0g, late):
    S, D = x.shape
    T = min(TILE_FWD, S)
    n = S // T
    nl = len(late)
    late_split = [_split_halves(a) for a in late]

    def body(*refs):
        (x_ref, gpre_ref, gpost_ref, wa_ref, wb_ref, bb_ref, lng_ref, lnb_ref, g1g_hbm, pr0g_hbm) = refs[:10]
        srcs = refs[10:10 + nl]
        x1_ref, u_ref, o1_ref, z1_ref = refs[10 + nl:14 + nl]
        dsts = refs[14 + nl:14 + 2 * nl]
        win_v, wout_v, ext, sh, z1buf = refs[14 + 2 * nl:19 + 2 * nl]
        sems = refs[19 + 2 * nl:]
        i = pl.program_id(0)

        @pl.when(i == 0)
        def _():
            _gather_start(srcs, dsts, sems)
            _copy_all(_cols(win_v, g1g_hbm, IN_BLK) + _rows(wout_v, pr0g_hbm, 0, ROW_BLK))
            ext[pl.ds(0, HALO), :] = jnp.zeros((HALO, D_A + D_B), F32)

        @pl.when(i == max(n - 2, 0))
        def _():
            _gather_forward(srcs, dsts, sems)

        xv = x_ref[...]
        xn, _ = _rms(xv)
        h = (xn * gpre_ref[...]).astype(BF16)
        u = _mm(h, win_v[...])
        u_ref[...] = u.astype(BF16)
        b_a = u[:, 0:D_A]
        cv = u[:, D_A:2 * D_A] * u[:, 2 * D_A:3 * D_A]
        z0 = u[:, 3 * D_A:3 * D_A + D_B] * _sigmoid(u[:, 3 * D_A + D_B:])
        ext[pl.ds(HALO, T), pl.ds(0, D_A)] = cv
        ext[pl.ds(HALO, T), pl.ds(D_A, D_B)] = z0

        conv_a = ext[pl.ds(HALO - 2, T), pl.ds(0, D_A)] * wa_ref[0:1, :]
        for k in range(1, CONV_A_W):
            conv_a = conv_a + ext[pl.ds(HALO - 2 + k, T), pl.ds(0, D_A)] * wa_ref[k:k + 1, :]
        y_a = b_a * conv_a

        _fill_phases(ext, sh, T + HALO - 8)
        base = HALO - (CONV_B_W - 1)

        def chunk(ci, carry):
            start = pl.multiple_of(ci * CONV_ROWS_FWD, 8)
            acc = jnp.broadcast_to(bb_ref[...], (CONV_ROWS_FWD, D_B))
            for k in range(CONV_B_W):
                acc = acc + _phase_rows(ext, sh, base + k, start, CONV_ROWS_FWD) * wb_ref[k:k + 1, :]
            z1buf[pl.ds(start, CONV_ROWS_FWD), :] = acc
            return carry

        lax.fori_loop(0, T // CONV_ROWS_FWD, chunk, 0)
        z1 = z1buf[...]
        z1_ref[...] = z1.astype(BF16)
        mu = jnp.mean(z1, axis=-1, keepdims=True)
        zc = z1 - mu
        rstd = lax.rsqrt(jnp.mean(zc * zc, axis=-1, keepdims=True) + LN_EPS)
        l = zc * rstd * lng_ref[...] + lnb_ref[...]
        y_b = l * _sigmoid(l)
        y = jnp.concatenate([y_a, y_b], axis=-1).astype(BF16)
        o1 = _mm(y, wout_v[...])
        o1_ref[...] = o1.astype(BF16)
        o1n, _ = _rms(o1)
        x1_ref[...] = xv + o1n * gpost_ref[...]
        ext[pl.ds(0, HALO), :] = ext[pl.ds(T, HALO), :]

        @pl.when(i == n - 1)
        def _():
            _gather_finish(srcs, dsts, sems)

    tok = lambda w: pl.BlockSpec((T, w), lambda i: (i, 0))
    outs = pl.pallas_call(
        body,
        name="mix_fwd",
        grid=(n,),
        in_specs=[tok(D), _full((1, D)), _full((1, D)), _full((CONV_A_W, D_A)), _full((CONV_B_W, D_B)),
                  _full((1, D_B)), _full((1, D_B)), _full((1, D_B)), ANY, ANY] + [ANY] * nl,
        out_specs=[tok(D), tok(D_IN_ALL), tok(D), tok(D_B)] + [ANY] * nl,
        out_shape=[jax.ShapeDtypeStruct((S, D), F32), jax.ShapeDtypeStruct((S, D_IN_ALL), BF16),
                   jax.ShapeDtypeStruct((S, D), BF16), jax.ShapeDtypeStruct((S, D_B), BF16)]
        + [jax.ShapeDtypeStruct((N_CHIPS,) + a.shape, a.dtype) for a in late_split],
        scratch_shapes=[pltpu.VMEM((D, D_IN_ALL), BF16), pltpu.VMEM((D_A + D_B, D), BF16),
                        pltpu.VMEM((HALO + T, D_A + D_B), F32), pltpu.VMEM((7, HALO + T, D_B), F32),
                        pltpu.VMEM((T, D_B), F32)] + _gather_sems(nl),
        compiler_params=_params(("arbitrary",)),
    )(x, gpre, gpost, wa, wb, bb, lng, lnb, g1g, pr0g, *late_split)
    return list(outs[:4]) + [o.reshape((N_CHIPS,) + a.shape) for o, a in zip(outs[4:], late)]


def _mix_bwd(dx1, x, o1, u, z1s, gpre, gpost, wa, wb, lng, lnb, g1g, pr0g, after):
    S, D = x.shape
    T = min(TILE_BWD, S)
    n = S // T
    hb = T // HALO
    phase_rows = min(T, PHASE_ROWS)

    def body(dx1_ref, x_ref, o1_ref, u_ref, uh_ref, z1_ref, gpre_ref, gpost_ref, wa_ref, wb_ref,
             lng_ref, lnb_ref, g1g_hbm, pr0g_hbm, _,
             dx_ref, dgpre_ref, dgpost_ref, dwa_ref, dwb_ref, dbb_ref, dlng_ref, dlnb_ref, cm1_hbm, cm2_hbm,
             win_v, wout_v, dwin_acc, dwout_acc, ext, ext2, shb, dwb_acc):
        i = pl.program_id(0)

        @pl.when(i == 0)
        def _():
            _copy_all(_cols(win_v, g1g_hbm, IN_BLK) + _rows(wout_v, pr0g_hbm, 0, ROW_BLK))
            dwin_acc[...] = jnp.zeros_like(dwin_acc)
            dwout_acc[...] = jnp.zeros_like(dwout_acc)
            dwb_acc[...] = jnp.zeros_like(dwb_acc)
            ext2[pl.ds(T, HALO), :] = jnp.zeros((HALO, D_A + D_B), F32)
            for ref in (dgpre_ref, dgpost_ref, dwa_ref, dbb_ref, dlng_ref, dlnb_ref):
                ref[...] = jnp.zeros_like(ref)

        o1n, r1 = _rms(o1_ref[...].astype(F32))
        dx1v = dx1_ref[...]
        d_o1, dgp = _rms_bwd(dx1v, o1n, r1, gpost_ref[...])
        dgpost_ref[...] += dgp
        d_o1b = d_o1.astype(BF16)
        dy = _mm_nt(d_o1b, wout_v[...])

        first = (i == n - 1).astype(F32)
        uh = uh_ref[...].astype(F32) * (1.0 - first)
        ext[pl.ds(0, HALO), pl.ds(0, D_A)] = uh[:, D_A:2 * D_A] * uh[:, 2 * D_A:3 * D_A]
        uf = u_ref[...].astype(F32)
        b_a = uf[:, 0:D_A]
        c_a = uf[:, D_A:2 * D_A]
        v_a = uf[:, 2 * D_A:3 * D_A]
        gv = uf[:, 3 * D_A:3 * D_A + D_B]
        sg = _sigmoid(uf[:, 3 * D_A + D_B:])
        ext[pl.ds(HALO, T), pl.ds(0, D_A)] = c_a * v_a
        ext[pl.ds(HALO, T), pl.ds(D_A, D_B)] = gv * sg
        conv_a = ext[pl.ds(HALO - 2, T), pl.ds(0, D_A)] * wa_ref[0:1, :]
        for k in range(1, CONV_A_W):
            conv_a = conv_a + ext[pl.ds(HALO - 2 + k, T), pl.ds(0, D_A)] * wa_ref[k:k + 1, :]
        z1 = z1_ref[...].astype(F32)
        mu = jnp.mean(z1, axis=-1, keepdims=True)
        zc = z1 - mu
        rstd = lax.rsqrt(jnp.mean(zc * zc, axis=-1, keepdims=True) + LN_EPS)
        zn = zc * rstd
        l = zn * lng_ref[...] + lnb_ref[...]
        sl = _sigmoid(l)
        y = jnp.concatenate([b_a * conv_a, l * sl], axis=-1).astype(BF16)
        dwout_acc[...] += _mm_tn(y, d_o1b)

        dy_a = dy[:, 0:D_A]
        dl = dy[:, D_A:] * (sl * (1.0 + l * (1.0 - sl)))
        dlng_ref[...] += jnp.sum(dl * zn, axis=0, keepdims=True)
        dlnb_ref[...] += jnp.sum(dl, axis=0, keepdims=True)
        dzn = dl * lng_ref[...]
        dz1 = rstd * (dzn - jnp.mean(dzn, axis=-1, keepdims=True) - zn * jnp.mean(dzn * zn, axis=-1, keepdims=True))
        dbb_ref[...] += jnp.sum(dz1, axis=0, keepdims=True)
        d_conv = dy_a * b_a
        ext2[pl.ds(0, T), pl.ds(0, D_A)] = d_conv
        ext2[pl.ds(0, T), pl.ds(D_A, D_B)] = dz1

        d_cv = ext2[pl.ds(CONV_A_W - 1, T), pl.ds(0, D_A)] * wa_ref[0:1, :]
        for k in range(1, CONV_A_W):
            d_cv = d_cv + ext2[pl.ds(CONV_A_W - 1 - k, T), pl.ds(0, D_A)] * wa_ref[k:k + 1, :]
        for k in range(CONV_A_W):
            dwa_ref[k:k + 1, :] += jnp.sum(d_conv * ext[pl.ds(HALO - 2 + k, T), pl.ds(0, D_A)], axis=0, keepdims=True)

        for row0 in range(0, T, phase_rows):
            _fill_phases(ext2, shb, phase_rows + HALO - 8, row0)

            def chunk(ci, carry, row0=row0):
                start = pl.multiple_of(row0 + ci * CONV_ROWS, 8)
                z0c = ext[pl.ds(HALO + start, CONV_ROWS), pl.ds(D_A, D_B)]
                acc = jnp.zeros((CONV_ROWS, D_B), F32)
                for k in range(CONV_B_W):
                    ahead = _phase_rows(ext2, shb, CONV_B_W - 1 - k, start, CONV_ROWS, row0)
                    acc = acc + ahead * wb_ref[k:k + 1, :]
                    dwb_acc[k] += _fold8(z0c * ahead)
                ext[pl.ds(HALO + start, CONV_ROWS), pl.ds(D_A, D_B)] = acc
                return carry

            lax.fori_loop(0, phase_rows // CONV_ROWS, chunk, 0)
        dz0 = ext[pl.ds(HALO, T), pl.ds(D_A, D_B)]
        du = jnp.concatenate([dy_a * conv_a, d_cv * v_a, d_cv * c_a, dz0 * sg, dz0 * gv * sg * (1.0 - sg)],
                             axis=-1).astype(BF16)
        dh = _mm_nt(du, win_v[...])
        xn, r0 = _rms(x_ref[...])
        dwin_acc[...] += _mm_tn((xn * gpre_ref[...]).astype(BF16), du)
        dxp, dg0 = _rms_bwd(dh, xn, r0, gpre_ref[...])
        dgpre_ref[...] += dg0
        dx_ref[...] = dx1v + dxp
        ext2[pl.ds(T, HALO), :] = ext2[pl.ds(0, HALO), :]

        @pl.when(i == n - 1)
        def _():
            for k in range(CONV_B_W):
                dwb_ref[k:k + 1, :] = jnp.sum(dwb_acc[k], axis=0, keepdims=True)
            win_v[...] = dwin_acc[...].astype(BF16)
            wout_v[...] = dwout_acc[...].astype(BF16)
            _copy_all([(win_v.at[:, pl.ds(IN_BLK * k, IN_BLK)], cm1_hbm.at[k]) for k in range(N_CHIPS)]
                      + [(wout_v.at[pl.ds(ROW_BLK * k, ROW_BLK), :], cm2_hbm.at[k]) for k in range(N_CHIPS)])

    rev = lambda w: pl.BlockSpec((T, w), lambda i: (n - 1 - i, 0))
    halo = pl.BlockSpec((HALO, D_IN_ALL), lambda i: (jnp.maximum((n - 1 - i) * hb - 1, 0), 0))
    return pl.pallas_call(
        body,
        name="mix_bwd",
        grid=(n,),
        in_specs=[rev(D), rev(D), rev(D), rev(D_IN_ALL), halo, rev(D_B), _full((1, D)),
                  _full((1, D)), _full((CONV_A_W, D_A)), _full((CONV_B_W, D_B)), _full((1, D_B)), _full((1, D_B)),
                  ANY, ANY, ANY],
        out_specs=[rev(D), _full((1, D)), _full((1, D)), _full((CONV_A_W, D_A)), _full((CONV_B_W, D_B)),
                   _full((1, D_B)), _full((1, D_B)), _full((1, D_B)), ANY, ANY],
        out_shape=[jax.ShapeDtypeStruct((S, D), F32), jax.ShapeDtypeStruct((1, D), F32),
                   jax.ShapeDtypeStruct((1, D), F32), jax.ShapeDtypeStruct((CONV_A_W, D_A), F32),
                   jax.ShapeDtypeStruct((CONV_B_W, D_B), F32), jax.ShapeDtypeStruct((1, D_B), F32),
                   jax.ShapeDtypeStruct((1, D_B), F32), jax.ShapeDtypeStruct((1, D_B), F32),
                   jax.ShapeDtypeStruct((N_CHIPS, D, IN_BLK), BF16),
                   jax.ShapeDtypeStruct((N_CHIPS, ROW_BLK, D), BF16)],
        scratch_shapes=[pltpu.VMEM((D, D_IN_ALL), BF16), pltpu.VMEM((D_A + D_B, D), BF16),
                        pltpu.VMEM((D, D_IN_ALL), F32), pltpu.VMEM((D_A + D_B, D), F32),
                        pltpu.VMEM((HALO + T, D_A + D_B), F32), pltpu.VMEM((HALO + T, D_A + D_B), F32),
                        pltpu.VMEM((7, HALO + phase_rows, D_B), F32), pltpu.VMEM((CONV_B_W, 8, D_B), F32)],
        compiler_params=_params(("arbitrary",)),
    )(dx1, x, o1, u, u, z1s, gpre, gpost, wa, wb, lng, lnb, g1g, pr0g, after)


def _mem_kv(mem, gmem, pr1g):
    M, D = mem.shape

    def body(mem_ref, g_ref, pr1g_hbm, memn_ref, k_ref, v_ref, wk_v, wv_v):
        _copy_all(_rows(wk_v, pr1g_hbm, P1_K, ROW_BLK) + _rows(wv_v, pr1g_hbm, P1_V, ROW_BLK))
        mn, _ = _rms(mem_ref[...])
        mb = (mn * g_ref[...]).astype(BF16)
        memn_ref[...] = mb
        k_ref[...] = _mm(mb, wk_v[...]).astype(BF16)
        v_ref[...] = _mm(mb, wv_v[...]).astype(BF16)

    return pl.pallas_call(
        body,
        name="mem_kv",
        grid=(1,),
        in_specs=[_full((M, D)), _full((1, D)), ANY],
        out_specs=[_full((M, D))] * 3,
        out_shape=[jax.ShapeDtypeStruct((M, D), BF16)] * 3,
        scratch_shapes=[pltpu.VMEM((D, D), BF16), pltpu.VMEM((D, D), BF16)],
        compiler_params=_params(("arbitrary",)),
    )(mem, gmem, pr1g)


def _attend(qb, kb, vb):
    scale = HEAD_DIM ** -0.5
    ps, os_ = [], []
    for hd in range(XA_HEADS):
        cols = slice(HEAD_DIM * hd, HEAD_DIM * (hd + 1))
        s = _mm_nt(qb[:, cols], kb[:, cols]) * scale
        e = jnp.exp(s - jnp.max(s, axis=-1, keepdims=True))
        p = e * (1.0 / jnp.sum(e, axis=-1, keepdims=True))
        ps.append(p)
        os_.append(_mm(p.astype(BF16), vb[:, cols]))
    return ps, jnp.concatenate(os_, axis=-1).astype(BF16)


def _xattn_fwd(x1, gpre, gpost, kb, vb, pr1g):
    S, D = x1.shape
    M = kb.shape[0]
    T = min(TILE_XATTN_FWD, S)
    n = S // T

    def body(x1_ref, gpre_ref, gpost_ref, k_ref, v_ref, pr1g_hbm, x2_ref, q_ref, o2_ref, rx_ref, ro_ref, wq_v, wo_v):
        @pl.when(pl.program_id(0) == 0)
        def _():
            _copy_all(_rows(wq_v, pr1g_hbm, P1_Q, ROW_BLK) + _rows(wo_v, pr1g_hbm, P1_O, ROW_BLK))

        xv = x1_ref[...]
        xn, rx = _rms(xv)
        rx_ref[...] = rx
        qb = _mm((xn * gpre_ref[...]).astype(BF16), wq_v[...]).astype(BF16)
        q_ref[...] = qb
        _, ob = _attend(qb, k_ref[...], v_ref[...])
        o2 = _mm(ob, wo_v[...])
        o2_ref[...] = o2.astype(BF16)
        o2n, ro = _rms(o2)
        ro_ref[...] = ro
        x2_ref[...] = xv + o2n * gpost_ref[...]

    tok = lambda w: pl.BlockSpec((T, w), lambda i: (i, 0))
    return pl.pallas_call(
        body,
        name="xattn_fwd",
        grid=(n,),
        in_specs=[tok(D), _full((1, D)), _full((1, D)), _full((M, D)), _full((M, D)), ANY],
        out_specs=[tok(D), tok(D), tok(D), tok(1), tok(1)],
        out_shape=[jax.ShapeDtypeStruct((S, D), F32), jax.ShapeDtypeStruct((S, D), BF16),
                   jax.ShapeDtypeStruct((S, D), BF16), jax.ShapeDtypeStruct((S, 1), F32),
                   jax.ShapeDtypeStruct((S, 1), F32)],
        scratch_shapes=[pltpu.VMEM((D, D), BF16), pltpu.VMEM((D, D), BF16)],
        compiler_params=_params(("arbitrary",)),
    )(x1, gpre, gpost, kb, vb, pr1g)


def _xattn_bwd(dx3, dh3p, x2, x1, o2, q, rx2, ro2, rx1, kb, vb, gffn, gpost, gpre, pr1g, cf1, cf2):
    S, D = x1.shape
    M = kb.shape[0]
    T = min(TILE_XATTN_BWD, S)
    n = S // T
    scale = HEAD_DIM ** -0.5
    nparts = dh3p.shape[0]
    cfs = [_by_halves(cf1), _by_halves(cf2)]

    def body(*refs):
        dx3_ref, dh3_refs = refs[0], refs[1:1 + nparts]
        (x2_ref, x1_ref, o2_ref, q_ref, rx2_ref, ro2_ref, rx1_ref, k_ref, v_ref, gffn_ref, gpost_ref, gpre_ref,
         pr1g_hbm, cf1_hbm, cf2_hbm,
         dx1_ref, dgffn_ref, dgpost_ref, dgpre_ref, dk_ref, dv_ref, cx_hbm, yf1_hbm, yf2_hbm,
         wq_v, wo_v, dwq_acc, dwo_acc, send, recv) = refs[1 + nparts:]
        i = pl.program_id(0)

        @pl.when(i == 0)
        def _():
            _scatter_start([cf1_hbm, cf2_hbm], [yf1_hbm, yf2_hbm], send, recv)
            _copy_all(_rows(wq_v, pr1g_hbm, P1_Q, ROW_BLK) + _rows(wo_v, pr1g_hbm, P1_O, ROW_BLK))
            dwq_acc[...] = jnp.zeros_like(dwq_acc)
            dwo_acc[...] = jnp.zeros_like(dwo_acc)
            for ref in (dgffn_ref, dgpost_ref, dgpre_ref, dk_ref, dv_ref):
                ref[...] = jnp.zeros_like(ref)

        r2 = rx2_ref[...]
        x2n = x2_ref[...] * r2
        dh3 = dh3_refs[0][...].astype(F32)
        for ref in dh3_refs[1:]:
            dh3 = dh3 + ref[...].astype(F32)
        dxp, dg = _rms_bwd(dh3, x2n, r2, gffn_ref[...])
        dgffn_ref[...] += dg
        dx2 = dx3_ref[...] + dxp
        ro = ro2_ref[...]
        o2n = o2_ref[...].astype(F32) * ro
        d_o2, dg = _rms_bwd(dx2, o2n, ro, gpost_ref[...])
        dgpost_ref[...] += dg
        d_o2b = d_o2.astype(BF16)
        d_o = _mm_nt(d_o2b, wo_v[...]).astype(BF16)
        qb = q_ref[...]
        kv = k_ref[...]
        vv = v_ref[...]
        ps, ob = _attend(qb, kv, vv)
        dwo_acc[...] += _mm_tn(ob, d_o2b)
        dqs = []
        for hd in range(XA_HEADS):
            cols = slice(HEAD_DIM * hd, HEAD_DIM * (hd + 1))
            p = ps[hd]
            dp = _mm_nt(d_o[:, cols], vv[:, cols])
            dv_ref[:, cols] += _mm_tn(p.astype(BF16), d_o[:, cols])
            ds = (p * (dp - jnp.sum(p * dp, axis=-1, keepdims=True)) * scale).astype(BF16)
            dqs.append(_mm(ds, kv[:, cols]))
            dk_ref[:, cols] += _mm_tn(ds, qb[:, cols])
        dq = jnp.concatenate(dqs, axis=-1).astype(BF16)
        dh2 = _mm_nt(dq, wq_v[...])
        r1 = rx1_ref[...]
        x1n = x1_ref[...] * r1
        dwq_acc[...] += _mm_tn((x1n * gpre_ref[...]).astype(BF16), dq)
        dxp, dg = _rms_bwd(dh2, x1n, r1, gpre_ref[...])
        dgpre_ref[...] += dg
        dx1_ref[...] = dx2 + dxp

        @pl.when(i == n - 1)
        def _():
            wq_v[...] = dwq_acc[...].astype(BF16)
            wo_v[...] = dwo_acc[...].astype(BF16)
            _copy_all([(w.at[pl.ds(ROW_BLK * k, ROW_BLK), :], cx_hbm.at[k, pl.ds(row0, ROW_BLK), :])
                       for w, row0 in ((wq_v, GX_Q), (wo_v, GX_O)) for k in range(N_CHIPS)])
            _scatter_wait([cf1_hbm, cf2_hbm], [yf1_hbm, yf2_hbm], send, recv)

    tok = lambda w: pl.BlockSpec((T, w), lambda i: (i, 0))
    part = lambda j: pl.BlockSpec((None, T, D), lambda i: (j, i, 0))
    return pl.pallas_call(
        body,
        name="xattn_bwd",
        grid=(n,),
        in_specs=[tok(D)] + [part(j) for j in range(nparts)] + [tok(D), tok(D), tok(D), tok(D), tok(1), tok(1), tok(1),
                                                                 _full((M, D)), _full((M, D)), _full((1, D)),
                                                                 _full((1, D)), _full((1, D)), ANY, ANY, ANY],
        out_specs=[tok(D), _full((1, D)), _full((1, D)), _full((1, D)), _full((M, D)), _full((M, D)), ANY, ANY, ANY],
        out_shape=[jax.ShapeDtypeStruct((S, D), F32), jax.ShapeDtypeStruct((1, D), F32),
                   jax.ShapeDtypeStruct((1, D), F32), jax.ShapeDtypeStruct((1, D), F32),
                   jax.ShapeDtypeStruct((M, D), F32), jax.ShapeDtypeStruct((M, D), F32),
                   jax.ShapeDtypeStruct((N_CHIPS, D, D), BF16), _slots(cfs[0]), _slots(cfs[1])],
        scratch_shapes=[pltpu.VMEM((D, D), BF16), pltpu.VMEM((D, D), BF16),
                        pltpu.VMEM((D, D), F32), pltpu.VMEM((D, D), F32)] + _scatter_sems(2),
        compiler_params=_params(("arbitrary",)),
    )(dx3, *([dh3p] * nparts), x2, x1, o2, q, rx2, ro2, rx1, kb, vb, gffn, gpost, gpre, pr1g, *cfs)


def _mem_bwd(dk, dv, mem, memn, gmem, pr1g, cx_in):
    M, D = mem.shape

    def body(dk_ref, dv_ref, mem_ref, memn_ref, g_ref, pr1g_hbm, cx_hbm, dg_ref, cx_out, wk_v, wv_v):
        del cx_hbm
        _copy_all(_rows(wk_v, pr1g_hbm, P1_K, ROW_BLK) + _rows(wv_v, pr1g_hbm, P1_V, ROW_BLK))
        dkb = dk_ref[...].astype(BF16)
        dvb = dv_ref[...].astype(BF16)
        mb = memn_ref[...]
        dmn = _mm_nt(dkb, wk_v[...]) + _mm_nt(dvb, wv_v[...])
        mn, _ = _rms(mem_ref[...])
        dg_ref[...] = jnp.sum(dmn * mn, axis=0, keepdims=True)
        wk_v[...] = _mm_tn(mb, dkb).astype(BF16)
        wv_v[...] = _mm_tn(mb, dvb).astype(BF16)
        _copy_all([(w.at[pl.ds(ROW_BLK * k, ROW_BLK), :], cx_out.at[k, pl.ds(row0, ROW_BLK), :])
                   for w, row0 in ((wk_v, GX_K), (wv_v, GX_V)) for k in range(N_CHIPS)])

    return pl.pallas_call(
        body,
        name="mem_bwd",
        grid=(1,),
        in_specs=[_full((M, D)), _full((M, D)), _full((M, D)), _full((M, D)), _full((1, D)), ANY, ANY],
        out_specs=[_full((1, D)), ANY],
        out_shape=[jax.ShapeDtypeStruct((1, D), F32), jax.ShapeDtypeStruct(cx_in.shape, BF16)],
        input_output_aliases={6: 1},
        scratch_shapes=[pltpu.VMEM((D, D), BF16), pltpu.VMEM((D, D), BF16)],
        compiler_params=_params(("arbitrary",)),
    )(dk, dv, mem, memn, gmem, pr1g, cx_in)


def _ffn_fwd(x2, target, gpre, gpost, g2g, pr1g):
    S, D = x2.shape
    T = min(TILE_FFN, S)
    n = S // T

    def body(x2_ref, t_ref, gpre_ref, gpost_ref, g2g_hbm, pr1g_hbm,
             h3_ref, g_hbm, u_hbm, do3_ref, dx3_ref, loss_ref, dgpost_ref, rx_ref, wg_v, wu_v, wd_v, gst, ust, sem):
        i = pl.program_id(0)

        @pl.when(i == 0)
        def _():
            _copy_all([(g2g_hbm.at[:, pl.ds(0, D), :], wg_v), (g2g_hbm.at[:, pl.ds(D, D), :], wu_v),
                       (pr1g_hbm.at[:, pl.ds(P1_DOWN, FF_BLK), :], wd_v)])
            loss_ref[...] = jnp.zeros_like(loss_ref)
            dgpost_ref[...] = jnp.zeros_like(dgpost_ref)

        xv = x2_ref[...]
        xn, rx = _rms(xv)
        rx_ref[...] = rx
        hb = (xn * gpre_ref[...]).astype(BF16)
        h3_ref[...] = hb
        o3 = jnp.zeros((T, D), F32)
        out = [None, None]
        for c in range(N_CHIPS):
            slot = c % 2
            if out[slot] is not None:
                for cp in out[slot]:
                    cp.wait()
            g = _mm(hb, wg_v[c])
            u = _mm(hb, wu_v[c])
            gst[slot] = g.astype(BF16)
            ust[slot] = u.astype(BF16)
            out[slot] = (pltpu.make_async_copy(gst.at[slot], g_hbm.at[c, i], sem.at[0, slot]),
                         pltpu.make_async_copy(ust.at[slot], u_hbm.at[c, i], sem.at[1, slot]))
            for cp in out[slot]:
                cp.start()
            o3 = o3 + _mm((g * _sigmoid(g) * u).astype(BF16), wd_v[c])
        for pair in out:
            for cp in pair:
                cp.wait()
        o3n, r3 = _rms(o3)
        diff = xv + o3n * gpost_ref[...] - t_ref[...]
        sq = jnp.sum(jnp.sum(diff * diff, axis=-1, keepdims=True), axis=0, keepdims=True)
        loss_ref[...] += sq * (0.5 / D)
        dx3 = diff * (1.0 / D)
        dx3_ref[...] = dx3
        d_o3, dg = _rms_bwd(dx3, o3n, r3, gpost_ref[...])
        dgpost_ref[...] += dg
        do3_ref[...] = d_o3.astype(BF16)

    tok = lambda w: pl.BlockSpec((T, w), lambda i: (i, 0))
    h3, gs, us, do3, dx3, loss, dgpost, rx2 = pl.pallas_call(
        body,
        name="ffn_fwd",
        grid=(n,),
        in_specs=[tok(D), tok(D), _full((1, D)), _full((1, D)), ANY, ANY],
        out_specs=[tok(D), ANY, ANY, tok(D), tok(D), _full((1, 128)), _full((1, D)), tok(1)],
        out_shape=[jax.ShapeDtypeStruct((S, D), BF16), jax.ShapeDtypeStruct((N_CHIPS, n, T, FF_BLK), BF16),
                   jax.ShapeDtypeStruct((N_CHIPS, n, T, FF_BLK), BF16), jax.ShapeDtypeStruct((S, D), BF16),
                   jax.ShapeDtypeStruct((S, D), F32), jax.ShapeDtypeStruct((1, 128), F32),
                   jax.ShapeDtypeStruct((1, D), F32), jax.ShapeDtypeStruct((S, 1), F32)],
        scratch_shapes=[pltpu.VMEM((N_CHIPS, D, FF_BLK), BF16), pltpu.VMEM((N_CHIPS, D, FF_BLK), BF16),
                        pltpu.VMEM((N_CHIPS, FF_BLK, D), BF16), pltpu.VMEM((2, T, FF_BLK), BF16),
                        pltpu.VMEM((2, T, FF_BLK), BF16), pltpu.SemaphoreType.DMA((2, 2))],
        compiler_params=_params(("arbitrary",)),
    )(x2, target, gpre, gpost, g2g, pr1g)
    return h3, gs.reshape(N_CHIPS, S, FF_BLK), us.reshape(N_CHIPS, S, FF_BLK), do3, dx3, loss, dgpost, rx2


def _ffn_bwd(h3, do3, gs, us, g2g, pr1g):
    S, D = h3.shape
    T = min(TILE_FFN_BWD, S)
    n = S // T
    NP = FFN_BWD_BLOCKS

    def body(h3_ref, do3_ref, g_ref, u_ref, g2g_hbm, pr1g_hbm, dh3_ref, cf1_hbm, cf2_hbm,
             wg_v, wu_v, wd_v, dwg_acc, dwu_acc, dwd_acc):
        jp = pl.program_id(0)
        i = pl.program_id(1)
        blocks = pl.ds(NP * jp, NP)

        @pl.when(i == 0)
        def _():
            _copy_all([(g2g_hbm.at[blocks, pl.ds(0, D), :], wg_v), (g2g_hbm.at[blocks, pl.ds(D, D), :], wu_v),
                       (pr1g_hbm.at[blocks, pl.ds(P1_DOWN, FF_BLK), :], wd_v)])
            dwg_acc[...] = jnp.zeros_like(dwg_acc)
            dwu_acc[...] = jnp.zeros_like(dwu_acc)
            dwd_acc[...] = jnp.zeros_like(dwd_acc)

        hb = h3_ref[...]
        d_o3 = do3_ref[...]
        dh = jnp.zeros((T, D), F32)
        for c in range(NP):
            da = _mm_nt(d_o3, wd_v[c])
            g = g_ref[c].astype(F32)
            u = u_ref[c].astype(F32)
            sg = _sigmoid(g)
            sl = g * sg
            dwd_acc[c] += _mm_tn((sl * u).astype(BF16), d_o3)
            dub = (da * sl).astype(BF16)
            dgb = (da * u * (sg * (1.0 + g * (1.0 - sg)))).astype(BF16)
            dwg_acc[c] += _mm_tn(dgb, hb)
            dwu_acc[c] += _mm_tn(dub, hb)
            dh = dh + _mm_nt(dgb, wg_v[c]) + _mm_nt(dub, wu_v[c])
        dh3_ref[...] = dh.astype(BF16)

        @pl.when(i == n - 1)
        def _():
            wd_v[...] = dwg_acc[...].astype(BF16)
            pltpu.sync_copy(wd_v, cf1_hbm.at[blocks, pl.ds(0, FF_BLK), :])
            wd_v[...] = dwu_acc[...].astype(BF16)
            pltpu.sync_copy(wd_v, cf1_hbm.at[blocks, pl.ds(FF_BLK, FF_BLK), :])
            wd_v[...] = dwd_acc[...].astype(BF16)
            pltpu.sync_copy(wd_v, cf2_hbm.at[blocks])

    tok = lambda w: pl.BlockSpec((T, w), lambda jp, i: (i, 0))
    blk = pl.BlockSpec((NP, T, FF_BLK), lambda jp, i: (jp, i, 0))
    return pl.pallas_call(
        body,
        name="ffn_bwd",
        grid=(N_CHIPS // NP, n),
        in_specs=[tok(D), tok(D), blk, blk, ANY, ANY],
        out_specs=[pl.BlockSpec((None, T, D), lambda jp, i: (jp, i, 0)), ANY, ANY],
        out_shape=[jax.ShapeDtypeStruct((N_CHIPS // NP, S, D), BF16),
                   jax.ShapeDtypeStruct((N_CHIPS, 2 * FF_BLK, D), BF16),
                   jax.ShapeDtypeStruct((N_CHIPS, FF_BLK, D), BF16)],
        scratch_shapes=[pltpu.VMEM((NP, D, FF_BLK), BF16), pltpu.VMEM((NP, D, FF_BLK), BF16),
                        pltpu.VMEM((NP, FF_BLK, D), BF16), pltpu.VMEM((NP, FF_BLK, D), F32),
                        pltpu.VMEM((NP, FF_BLK, D), F32), pltpu.VMEM((NP, FF_BLK, D), F32)],
        compiler_params=_params(("arbitrary", "arbitrary")),
    )(h3, do3, gs, us, g2g, pr1g)


IN_HBM = pl.BlockSpec(memory_space=pltpu.HBM)
IN_SEMAPHORES = pl.BlockSpec(memory_space=pltpu.SEMAPHORE)


def _split_copies(refs, na, nw, sems):
    srcs, lands = refs[:na + nw], refs[na + nw:2 * (na + nw)]
    send, recv = sems
    x, y, c, _ = _my_place()
    me = 4 * x + 2 * y + c
    pairs = []
    for d, ((px, py, pc), pidx) in enumerate(_peers(x, y, c)):
        for a in range(na + nw):
            k = d * (na + nw) + a
            src = srcs[a].at[2 * px + py, pc] if a < na else srcs[a]
            pairs.append((_remote(src, lands[a].at[me], send.at[k], recv.at[k], (px, py, pc)),
                          _remote(src, lands[a].at[pidx], send.at[k], recv.at[k], (px, py, pc))))
    return pairs


def _exchange_start(contribs, whole, tag):
    na, nw = len(contribs), len(whole)
    cs = [_by_halves(a) for a in contribs]
    flying = cs + list(whole) + [lax.empty(_slots(a).shape, a.dtype) for a in cs] + [
        lax.empty((N_DEV,) + a.shape, a.dtype) for a in whole]
    nf = len(flying)

    def body(*refs):
        for mine, _ in _split_copies(refs[:nf], na, nw, refs[nf:nf + 2]):
            mine.start()
        refs[-1][...] = jnp.zeros_like(refs[-1])

    ncopies = (N_DEV - 1) * (na + nw)
    outs = pl.pallas_call(
        body,
        name="exchange_start_" + tag,
        in_specs=[IN_HBM] * nf,
        out_specs=[IN_SEMAPHORES] * 2 + [IN_HBM] * nf + [pl.BlockSpec(memory_space=pltpu.VMEM)],
        out_shape=[pltpu.SemaphoreType.DMA((ncopies,)), pltpu.SemaphoreType.DMA((ncopies,))]
        + [pltpu.HBM(a.shape, a.dtype) for a in flying] + [jax.ShapeDtypeStruct((8, 128), F32)],
        input_output_aliases={k: 2 + k for k in range(nf)},
        compiler_params=pltpu.CompilerParams(has_side_effects=pltpu.SideEffectType.DATAFLOW_SIDE_EFFECTING),
    )(*[pltpu.with_memory_space_constraint(a, pltpu.HBM) for a in flying])
    return outs[:2], outs[2:2 + nf], outs[-1]


def _exchange_wait(sems, flying, na, nw, after, tag):
    nf = len(flying)

    def body(*refs):
        for mine, theirs in _split_copies(refs[:nf], na, nw, refs[nf:nf + 2]):
            theirs.wait_recv()
            mine.wait_send()

    outs = pl.pallas_call(
        body,
        name="exchange_wait_" + tag,
        in_specs=[IN_HBM] * nf + [IN_SEMAPHORES] * 2 + [ANY] * len(after),
        out_specs=[IN_HBM] * nf,
        out_shape=[pltpu.HBM(a.shape, a.dtype) for a in flying],
        input_output_aliases={k: k for k in range(nf)},
        compiler_params=pltpu.CompilerParams(has_side_effects=pltpu.SideEffectType.DATAFLOW_SIDE_EFFECTING),
    )(*flying, *sems, *after)
    return outs[na + nw:]


def _sum_peers(parts, own, place, steps, tag, after=()):
    _, rows, w = parts.shape
    tr = rows // steps
    if own.ndim == 3:
        own = _by_halves(own)

    def body(place_ref, *refs):
        p_refs, own_ref, o_ref = refs[:N_DEV], refs[N_DEV], refs[-1]
        me = place_ref[2]
        acc = None
        for s in range(N_DEV):
            term = jnp.where(me == s, own_ref[...], p_refs[s][...]).astype(F32)
            acc = term if acc is None else acc + term
        o_ref[...] = acc

    def other(s):
        return lambda i, pr: (jnp.where(pr[2] == s, (s + 1) % N_DEV, s), i, 0)

    own_spec = (pl.BlockSpec((None, None, tr, w), lambda i, pr: (pr[0], pr[1], i, 0)) if own.ndim == 4 else
                pl.BlockSpec((tr, w), lambda i, pr: (i, 0)))
    out_spec = (pl.BlockSpec((None, tr, w), lambda i, pr: (pr[1], i, 0)) if own.ndim == 4 else
                pl.BlockSpec((tr, w), lambda i, pr: (i, 0)))
    return pl.pallas_call(
        body,
        name="sum_peers_" + tag,
        grid_spec=pltpu.PrefetchScalarGridSpec(
            num_scalar_prefetch=1,
            grid=(steps,),
            in_specs=[pl.BlockSpec((None, tr, w), other(s)) for s in range(N_DEV)] + [own_spec] + [ANY] * len(after),
            out_specs=out_spec,
        ),
        out_shape=jax.ShapeDtypeStruct((2, rows, w) if own.ndim == 4 else (rows, w), F32),
        compiler_params=_params(("arbitrary",)),
    )(place, *([parts] * N_DEV), own, *after)


def _pair_gather(bufs, tag):
    np_ = len(bufs)

    def body(*refs):
        srcs, dsts = refs[:np_], refs[np_:2 * np_]
        send, recv = refs[2 * np_:]
        x, y, c, _ = _my_place()
        cps = []
        for p in range(np_):
            cp = _remote(srcs[p].at[c], dsts[p].at[c], send.at[p], recv.at[p], (x, y, 1 - c))
            cp.start()
            cps.append(cp)
        for p, cp in enumerate(cps):
            other = dsts[p].at[1 - c]
            _remote(other, other, send.at[p], recv.at[p], (x, y, 1 - c)).wait_recv()
            cp.wait_send()

    outs = pl.pallas_call(
        body,
        name="pair_gather_" + tag,
        in_specs=[ANY] * np_,
        out_specs=[ANY] * np_,
        out_shape=[jax.ShapeDtypeStruct(a.shape, F32) for a in bufs],
        input_output_aliases={p: p for p in range(np_)},
        scratch_shapes=[pltpu.SemaphoreType.DMA((np_,))] * 2,
    )(*bufs)
    return [o.reshape(2 * a.shape[1], a.shape[2]) for o, a in zip(outs, bufs)]


def _adamw(gsrc, row0, w, m, v, tr, tag):
    rows, width = w.shape
    off = row0 // tr
    bc1 = 1.0 - ADAM_B1 ** ADAM_STEP
    bc2 = 1.0 - ADAM_B2 ** ADAM_STEP

    def body(g_ref, w_ref, m_ref, v_ref, go_ref, d_ref, mo_ref, vo_ref):
        g = g_ref[...]
        m2 = ADAM_B1 * m_ref[...] + (1.0 - ADAM_B1) * g
        v2 = ADAM_B2 * v_ref[...] + (1.0 - ADAM_B2) * (g * g)
        go_ref[...] = g
        mo_ref[...] = m2
        vo_ref[...] = v2
        d_ref[...] = -ADAM_LR * ((m2 / bc1) / (jnp.sqrt(v2 / bc2) + ADAM_EPS) + ADAM_WD * w_ref[...])

    here = pl.BlockSpec((tr, width), lambda i: (i, 0))
    return pl.pallas_call(
        body,
        name="adamw_" + tag,
        grid=(rows // tr,),
        in_specs=[pl.BlockSpec((tr, width), lambda i: (off + i, 0)), here, here, here],
        out_specs=[here] * 4,
        out_shape=[jax.ShapeDtypeStruct((rows, width), F32)] * 4,
        compiler_params=_params(("arbitrary",)),
    )(gsrc, w, m, v)


def kernel(x, mem, mix_pre_g, w_mix_in, conv_a_w, conv_b_w, conv_b_b, ln_b_g, ln_b_b, w_mix_out, mix_post_g, xa_pre_g, mem_norm_g, w_q, w_k, w_v, w_o, xa_post_g, ffn_pre_g, w_gate, w_up, w_down, ffn_post_g, loss_target, m_mix_pre_g, m_w_mix_in, m_conv_a_w, m_conv_b_w, m_conv_b_b, m_ln_b_g, m_ln_b_b, m_w_mix_out, m_mix_post_g, m_xa_pre_g, m_mem_norm_g, m_w_q, m_w_k, m_w_v, m_w_o, m_xa_post_g, m_ffn_pre_g, m_w_gate, m_w_up, m_w_down, m_ffn_post_g, v_mix_pre_g, v_w_mix_in, v_conv_a_w, v_conv_b_w, v_conv_b_b, v_ln_b_g, v_ln_b_b, v_w_mix_out, v_mix_post_g, v_xa_pre_g, v_mem_norm_g, v_w_q, v_w_k, v_w_v, v_w_o, v_xa_post_g, v_ffn_pre_g, v_w_gate, v_w_up, v_w_down, v_ffn_post_g):
    given = dict(locals())
    names = ["mix_pre_g", "w_mix_in", "conv_a_w", "conv_b_w", "conv_b_b", "ln_b_g", "ln_b_b", "w_mix_out",
             "mix_post_g", "xa_pre_g", "mem_norm_g", "w_q", "w_k", "w_v", "w_o", "xa_post_g", "ffn_pre_g",
             "w_gate", "w_up", "w_down", "ffn_post_g"]
    row = lambda a: a.reshape(1, -1)
    cx, cy, cc = lax.axis_index("x"), lax.axis_index("y"), lax.axis_index("c")
    chip = 2 * cx + cy
    ca_blk = conv_a_w.shape[1]

    conv_rows = CONV_A_W + CONV_B_W
    sw = jnp.concatenate([conv_a_w, conv_b_w, jnp.zeros((SMALL_W_ROWS - conv_rows, ca_blk), F32)], axis=0)
    g1g, pr0g, swg = _gather_weights([w_mix_in.astype(BF16), w_mix_out.astype(BF16), sw])
    conv_full = jnp.transpose(swg[:, :conv_rows, :], (1, 0, 2)).reshape(conv_rows, N_CHIPS * ca_blk)
    wa, wb = conv_full[:CONV_A_W], conv_full[CONV_A_W:]
    pr1 = jnp.concatenate([w_q, w_k, w_v, w_o, w_down], axis=0).astype(BF16)
    g2 = jnp.concatenate([w_gate, w_up], axis=0).astype(BF16)

    xs, ms, tgt = x[0], mem[0], loss_target[0]
    x1, u, o1, z1, pr1g, g2g = _mix_fwd(xs, row(mix_pre_g), row(mix_post_g), wa, wb, row(conv_b_b), row(ln_b_g),
                                        row(ln_b_b), g1g, pr0g, [pr1, g2])
    memn, kb, vb = _mem_kv(ms, row(mem_norm_g), pr1g)
    x2, q, o2, rx1, ro2 = _xattn_fwd(x1, row(xa_pre_g), row(xa_post_g), kb, vb, pr1g)
    h3, gs, us, do3, dx3, loss_part, d_ffn_post, rx2 = _ffn_fwd(x2, tgt, row(ffn_pre_g), row(ffn_post_g), g2g, pr1g)

    dh3p, cf1, cf2 = _ffn_bwd(h3, do3, gs, us, g2g, pr1g)
    dx1, d_ffn_pre, d_xa_post, d_xa_pre, dk, dv, cxa, yf1, yf2 = _xattn_bwd(
        dx3, dh3p, x2, x1, o2, q, rx2, ro2, rx1, kb, vb, row(ffn_pre_g), row(xa_post_g), row(xa_pre_g), pr1g, cf1, cf2)
    d_mem_g, cxa = _mem_bwd(dk, dv, ms, memn, row(mem_norm_g), pr1g, cxa)
    sems_x, flying_x, token_x = _exchange_start([cxa], [], "attn")
    dx, d_mix_pre, d_mix_post, dwa, dwb, dbb, dlng, dlnb, cm1, cm2 = _mix_bwd(
        dx1, xs, o1, u, z1, row(mix_pre_g), row(mix_post_g), wa, wb, row(ln_b_g), row(ln_b_b), g1g, pr0g, token_x)
    (yx,) = _exchange_wait(sems_x, flying_x, 1, 0, [d_mix_pre], "attn")

    small_parts = [d_mix_pre, dwa, dwb, dbb, dlng, dlnb, d_mix_post, d_xa_pre, d_mem_g, d_xa_post, d_ffn_pre,
                   d_ffn_post, loss_part]
    sizes = [p.size for p in small_parts]
    small = jnp.concatenate([p.reshape(-1) for p in small_parts])
    small_rows = -(-small.size // (8 * 128)) * 8
    small = jnp.pad(small, (0, small_rows * 128 - small.size)).reshape(small_rows, 128)
    sems, flying, token = _exchange_start([cm1, cm2], [small], "mix")
    place = jnp.stack([chip, cc, 2 * chip + cc]).astype(jnp.int32)

    res = {}

    def update(nm, src, row0, tr, transposed=False):
        view = (lambda a: a.T) if transposed else (lambda a: a)
        outs = _adamw(src, row0, view(given[nm]), view(given["m_" + nm]), view(given["v_" + nm]), tr, nm)
        res[nm] = [view(o) for o in outs]

    early = [(yf1, cf1, "gate_up"), (yf2, cf2, "down"), (yx, cxa, "attn")]
    r_gu, r_down, r_attn = _pair_gather([_sum_peers(y, c, place, 2, t, after=(token,)) for y, c, t in early], "early")
    update("w_gate", r_gu, 0, FF_BLK // 2, transposed=True)
    update("w_up", r_gu, FF_BLK, FF_BLK // 2, transposed=True)
    update("w_down", r_down, 0, FF_BLK // 2)
    for nm, row0 in (("w_q", GX_Q), ("w_k", GX_K), ("w_v", GX_V), ("w_o", GX_O)):
        update(nm, r_attn, row0, ROW_BLK)
    done = [res[nm][1] for nm in ("w_gate", "w_up", "w_down", "w_q", "w_k", "w_v", "w_o")]

    ym1, ym2, small_all = _exchange_wait(sems, flying, 2, 1, done, "mix")
    r_in, r_out = _pair_gather([_sum_peers(y, c, place, 2, t) for y, c, t in
                                [(ym1, cm1, "mix_in"), (ym2, cm2, "mix_out")]], "late")
    update("w_mix_in", r_in, 0, 512)
    update("w_mix_out", r_out, 0, ROW_BLK)
    small_sum = _sum_peers(small_all, small, place, 1, "small").reshape(-1)
    red, pos = [], 0
    for p, sz in zip(small_parts, sizes):
        red.append(small_sum[pos:pos + sz].reshape(p.shape))
        pos += sz
    (r_mix_pre, r_wa, r_wb, r_bb, r_lng, r_lnb, r_mix_post, r_xa_pre, r_mem_g, r_xa_post, r_ffn_pre, r_ffn_post,
     r_loss) = red
    loss = r_loss[0, 0]

    small_grads = {"mix_pre_g": r_mix_pre, "conv_b_b": r_bb, "ln_b_g": r_lng, "ln_b_b": r_lnb,
                   "mix_post_g": r_mix_post, "xa_pre_g": r_xa_pre, "mem_norm_g": r_mem_g, "xa_post_g": r_xa_post,
                   "ffn_pre_g": r_ffn_pre, "ffn_post_g": r_ffn_post,
                   "conv_a_w": lax.dynamic_slice_in_dim(r_wa, chip * ca_blk, ca_blk, axis=1),
                   "conv_b_w": lax.dynamic_slice_in_dim(r_wb, chip * ca_blk, ca_blk, axis=1)}
    small_names = list(small_grads)

    def packed(prefix, grads=None):
        flat = jnp.concatenate([(grads[nm] if grads else given[prefix + nm]).reshape(-1) for nm in small_names])
        rows8 = -(-flat.size // (8 * 128)) * 8
        return jnp.pad(flat, (0, rows8 * 128 - flat.size)).reshape(rows8, 128)

    gp = packed("", small_grads)
    outs = _adamw(gp, 0, packed(""), packed("m_"), packed("v_"), gp.shape[0], "small")
    pos = 0
    for nm in small_names:
        shape = given[nm].shape
        sz = given[nm].size
        res[nm] = [o.reshape(-1)[pos:pos + sz].reshape(shape) for o in outs]
        pos += sz

    return (loss, dx[None], *[res[nm][0] for nm in names], *[res[nm][1] for nm in names],
            *[res[nm][2] for nm in names], *[res[nm][3] for nm in names])
```

```python
import jax
import jax.numpy as jnp
from jax import lax
from jax.experimental import pallas as pl
from jax.experimental.pallas import tpu as pltpu

F32 = jnp.float32
BF16 = jnp.bfloat16
MESH = pl.DeviceIdType.MESH

RMS_EPS = 1e-6
LN_EPS = 1e-5
D_MODEL = 1024
D_A = 512
D_B = 512
D_IN_ALL = 3 * D_A + 2 * D_B
CONV_A_W = 3
CONV_B_W = 31
HALO = 32
XA_HEADS = 4
HEAD_DIM = 256
D_FF = 2816
N_CHIPS = 4
N_DEV = 8
FF_BLK = D_FF // N_CHIPS
IN_BLK = D_IN_ALL // N_CHIPS
ROW_BLK = D_MODEL // N_CHIPS

ADAM_LR = 0.001
ADAM_B1 = 0.9
ADAM_B2 = 0.999
ADAM_EPS = 1e-08
ADAM_WD = 0.01
ADAM_STEP = 10

TILE_FWD = 512
TILE_XATTN_FWD = 1024
TILE_FFN = 512
TILE_FFN_BWD = 512
TILE_XATTN_BWD = 512
TILE_BWD = 512
PHASE_ROWS = 256
FFN_BWD_BLOCKS = 2
CONV_ROWS_FWD = 64
CONV_ROWS = 32
V7X_VMEM_LIMIT = 56 * 1024 * 1024
V7X_VMEM_LIMIT_HIGH = 63 * 1024 * 1024

P1_Q, P1_K, P1_V, P1_O, P1_DOWN = 0, 256, 512, 768, 1024
P1_ROWS = P1_DOWN + FF_BLK
GX_Q, GX_K, GX_V, GX_O = 0, 256, 512, 768
SMALL_W_ROWS = 48

ANY = pl.BlockSpec(memory_space=pl.ANY)


def _mm(a, b):
    return lax.dot_general(a, b, (((1,), (0,)), ((), ())), preferred_element_type=F32)


def _mm_nt(a, b):
    return lax.dot_general(a, b, (((1,), (1,)), ((), ())), preferred_element_type=F32)


def _mm_tn(a, b):
    return lax.dot_general(a, b, (((0,), (0,)), ((), ())), preferred_element_type=F32)


def _sigmoid(x):
    return 0.5 * jnp.tanh(0.5 * x) + 0.5


def _rms(x):
    r = lax.rsqrt(jnp.mean(x * x, axis=-1, keepdims=True) + RMS_EPS)
    return x * r, r


def _rms_bwd(dy, xn, r, g):
    gdy = dy * g
    dx = r * (gdy - xn * jnp.mean(gdy * xn, axis=-1, keepdims=True))
    return dx, jnp.sum(dy * xn, axis=0, keepdims=True)


def _fold8(a):
    out = a[0:8, :]
    for m in range(1, a.shape[0] // 8):
        out = out + a[8 * m:8 * m + 8, :]
    return out


def _full(shape):
    return pl.BlockSpec(shape, lambda *_: (0,) * len(shape))


def _params(sem=None, vmem_limit=V7X_VMEM_LIMIT):
    return pltpu.CompilerParams(dimension_semantics=sem, vmem_limit_bytes=vmem_limit)


def _copy_all(pairs):
    def scoped(sems):
        copies = [pltpu.make_async_copy(src, dst, sems.at[k]) for k, (src, dst) in enumerate(pairs)]
        for cp in copies:
            cp.start()
        for cp in copies:
            cp.wait()

    pl.run_scoped(scoped, pltpu.SemaphoreType.DMA((len(pairs),)))


def _rows(dst, src_hbm, row0, rows):
    return [(src_hbm.at[k, pl.ds(row0, rows), :], dst.at[pl.ds(rows * k, rows), :]) for k in range(N_CHIPS)]


def _cols(dst, src_hbm, cols):
    return [(src_hbm.at[k], dst.at[:, pl.ds(cols * k, cols)]) for k in range(N_CHIPS)]


def _fill_phases(src, sh, nrows, row0=0):
    for r in range(1, 8):
        sh[r - 1, pl.ds(0, nrows), :] = src[pl.ds(row0 + r, nrows), pl.ds(D_A, D_B)]


def _phase_rows(src, sh, off, start, size, row0=0):
    r = off % 8
    if r == 0:
        return src[pl.ds(off + start, size), pl.ds(D_A, D_B)]
    return sh[r - 1, pl.ds(off - r + start - row0, size), :]


def _my_place():
    x, y, c = lax.axis_index("x"), lax.axis_index("y"), lax.axis_index("c")
    return x, y, c, ((1 - x, y), (x, 1 - y), (1 - x, 1 - y))


def _remote(src, dst, send_sem, recv_sem, to):
    return pltpu.make_async_remote_copy(src_ref=src, dst_ref=dst, send_sem=send_sem, recv_sem=recv_sem,
                                        device_id=to, device_id_type=MESH)


def _gather_sems(np_):
    return [pltpu.SemaphoreType.DMA((np_, 3))] * 4 + [pltpu.SemaphoreType.DMA((np_,))] * 2


def _gather_start(srcs, dsts, sems):
    send, recv, _, _, osend, orecv = sems
    x, y, c, chips = _my_place()
    j = 2 * x + y
    for p in range(len(srcs)):
        _remote(srcs[p], dsts[p].at[j], osend.at[p], orecv.at[p], (x, y, 1 - c)).start()
        for nn, (kx, ky) in enumerate(chips):
            _remote(srcs[p].at[c], dsts[p].at[j, c], send.at[p, nn], recv.at[p, nn], (kx, ky, c)).start()


def _gather_forward(srcs, dsts, sems):
    send, recv, fsend, frecv, _, _ = sems
    x, y, c, chips = _my_place()
    for nn, (kx, ky) in enumerate(chips):
        for p in range(len(srcs)):
            blk = dsts[p].at[2 * kx + ky, c]
            _remote(blk, blk, send.at[p, nn], recv.at[p, nn], (kx, ky, c)).wait_recv()
            _remote(blk, blk, fsend.at[p, nn], frecv.at[p, nn], (x, y, 1 - c)).start()


def _gather_finish(srcs, dsts, sems):
    send, recv, fsend, frecv, osend, orecv = sems
    x, y, c, chips = _my_place()
    j = 2 * x + y
    for nn, (kx, ky) in enumerate(chips):
        for p in range(len(srcs)):
            other = dsts[p].at[2 * kx + ky, 1 - c]
            _remote(other, other, fsend.at[p, nn], frecv.at[p, nn], (x, y, 1 - c)).wait_recv()
    for nn, (kx, ky) in enumerate(chips):
        for p in range(len(srcs)):
            _remote(srcs[p].at[c], dsts[p].at[j, c], send.at[p, nn], recv.at[p, nn], (kx, ky, c)).wait_send()
            blk = dsts[p].at[2 * kx + ky, c]
            _remote(blk, blk, fsend.at[p, nn], frecv.at[p, nn], (x, y, 1 - c)).wait_send()
    for p in range(len(srcs)):
        _remote(srcs[p], dsts[p].at[j], osend.at[p], orecv.at[p], (x, y, 1 - c)).wait()


def _split_halves(a):
    return a.reshape(2, a.shape[0] // 2, a.shape[1])


def _gather_weights(packs):
    np_ = len(packs)
    split = [_split_halves(a) for a in packs]

    def body(*refs):
        srcs, dsts, sems = refs[:np_], refs[np_:2 * np_], refs[2 * np_:]
        _gather_start(srcs, dsts, sems)
        _gather_forward(srcs, dsts, sems)
        _gather_finish(srcs, dsts, sems)

    outs = pl.pallas_call(
        body,
        name="gather_weights",
        in_specs=[ANY] * np_,
        out_specs=[ANY] * np_,
        out_shape=[jax.ShapeDtypeStruct((N_CHIPS,) + a.shape, a.dtype) for a in split],
        scratch_shapes=_gather_sems(np_),
    )(*split)
    return [o.reshape((N_CHIPS,) + a.shape) for o, a in zip(outs, packs)]


def _peers(x, y, c):
    out = []
    for d in range(1, N_DEV):
        px = 1 - x if d & 4 else x
        py = 1 - y if d & 2 else y
        pc = 1 - c if d & 1 else c
        out.append(((px, py, pc), 4 * px + 2 * py + pc))
    return out


def _scatter_copies(srcs, dsts, send, recv):
    x, y, c, _ = _my_place()
    me = 4 * x + 2 * y + c
    out = []
    for a in range(len(srcs)):
        for d, ((px, py, pc), pidx) in enumerate(_peers(x, y, c)):
            piece = srcs[a].at[2 * px + py, pc]
            out.append((_remote(piece, dsts[a].at[me], send.at[a, d], recv.at[a, d], (px, py, pc)),
                        _remote(piece, dsts[a].at[pidx], send.at[a, d], recv.at[a, d], (px, py, pc))))
    return out


def _scatter_start(srcs, dsts, send, recv):
    for out_cp, _ in _scatter_copies(srcs, dsts, send, recv):
        out_cp.start()


def _scatter_wait(srcs, dsts, send, recv):
    for out_cp, in_cp in _scatter_copies(srcs, dsts, send, recv):
        in_cp.wait_recv()
        out_cp.wait_send()


def _scatter_sems(na):
    return [pltpu.SemaphoreType.DMA((na, N_DEV - 1))] * 2


def _slots(a):
    return jax.ShapeDtypeStruct((N_DEV,) + a.shape[2:], a.dtype)


def _by_halves(a):
    return a.reshape(a.shape[0], 2, a.shape[1] // 2, a.shape[2])


def _mix_fwd(x, gpre, gpost, wa, wb, bb, lng, lnb, g1g, pr0g, late):
    S, D = x.shape
    T = min(TILE_FWD, S)
    n = S // T
    nl = len(late)
    late_split = [_split_halves(a) for a in late]

    def body(*refs):
        (x_ref, gpre_ref, gpost_ref, wa_ref, wb_ref, bb_ref, lng_ref, lnb_ref, g1g_hbm, pr0g_hbm) = refs[:10]
        srcs = refs[10:10 + nl]
        x1_ref, u_ref, o1_ref, z1_ref = refs[10 + nl:14 + nl]
        dsts = refs[14 + nl:14 + 2 * nl]
        win_v, wout_v, ext, sh, z1buf = refs[14 + 2 * nl:19 + 2 * nl]
        sems = refs[19 + 2 * nl:]
        i = pl.program_id(0)

        @pl.when(i == 0)
        def _():
            _gather_start(srcs, dsts, sems)
            _copy_all(_cols(win_v, g1g_hbm, IN_BLK) + _rows(wout_v, pr0g_hbm, 0, ROW_BLK))
            ext[pl.ds(0, HALO), :] = jnp.zeros((HALO, D_A + D_B), F32)

        @pl.when(i == max(n - 2, 0))
        def _():
            _gather_forward(srcs, dsts, sems)

        xv = x_ref[...]
        xn, _ = _rms(xv)
        h = (xn * gpre_ref[...]).astype(BF16)
        u = _mm(h, win_v[...])
        u_ref[...] = u.astype(BF16)
        b_a = u[:, 0:D_A]
        cv = u[:, D_A:2 * D_A] * u[:, 2 * D_A:3 * D_A]
        z0 = u[:, 3 * D_A:3 * D_A + D_B] * _sigmoid(u[:, 3 * D_A + D_B:])
        ext[pl.ds(HALO, T), pl.ds(0, D_A)] = cv
        ext[pl.ds(HALO, T), pl.ds(D_A, D_B)] = z0

        conv_a = ext[pl.ds(HALO - 2, T), pl.ds(0, D_A)] * wa_ref[0:1, :]
        for k in range(1, CONV_A_W):
            conv_a = conv_a + ext[pl.ds(HALO - 2 + k, T), pl.ds(0, D_A)] * wa_ref[k:k + 1, :]
        y_a = b_a * conv_a

        _fill_phases(ext, sh, T + HALO - 8)
        base = HALO - (CONV_B_W - 1)

        def chunk(ci, carry):
            start = pl.multiple_of(ci * CONV_ROWS_FWD, 8)
            acc = jnp.broadcast_to(bb_ref[...], (CONV_ROWS_FWD, D_B))
            for k in range(CONV_B_W):
                acc = acc + _phase_rows(ext, sh, base + k, start, CONV_ROWS_FWD) * wb_ref[k:k + 1, :]
            z1buf[pl.ds(start, CONV_ROWS_FWD), :] = acc
            return carry

        lax.fori_loop(0, T // CONV_ROWS_FWD, chunk, 0)
        z1 = z1buf[...]
        z1_ref[...] = z1.astype(BF16)
        mu = jnp.mean(z1, axis=-1, keepdims=True)
        zc = z1 - mu
        rstd = lax.rsqrt(jnp.mean(zc * zc, axis=-1, keepdims=True) + LN_EPS)
        l = zc * rstd * lng_ref[...] + lnb_ref[...]
        y_b = l * _sigmoid(l)
        y = jnp.concatenate([y_a, y_b], axis=-1).astype(BF16)
        o1 = _mm(y, wout_v[...])
        o1_ref[...] = o1.astype(BF16)
        o1n, _ = _rms(o1)
        x1_ref[...] = xv + o1n * gpost_ref[...]
        ext[pl.ds(0, HALO), :] = ext[pl.ds(T, HALO), :]

        @pl.when(i == n - 1)
        def _():
            _gather_finish(srcs, dsts, sems)

    tok = lambda w: pl.BlockSpec((T, w), lambda i: (i, 0))
    outs = pl.pallas_call(
        body,
        name="mix_fwd",
        grid=(n,),
        in_specs=[tok(D), _full((1, D)), _full((1, D)), _full((CONV_A_W, D_A)), _full((CONV_B_W, D_B)),
                  _full((1, D_B)), _full((1, D_B)), _full((1, D_B)), ANY, ANY] + [ANY] * nl,
        out_specs=[tok(D), tok(D_IN_ALL), tok(D), tok(D_B)] + [ANY] * nl,
        out_shape=[jax.ShapeDtypeStruct((S, D), F32), jax.ShapeDtypeStruct((S, D_IN_ALL), BF16),
                   jax.ShapeDtypeStruct((S, D), BF16), jax.ShapeDtypeStruct((S, D_B), BF16)]
        + [jax.ShapeDtypeStruct((N_CHIPS,) + a.shape, a.dtype) for a in late_split],
        scratch_shapes=[pltpu.VMEM((D, D_IN_ALL), BF16), pltpu.VMEM((D_A + D_B, D), BF16),
                        pltpu.VMEM((HALO + T, D_A + D_B), F32), pltpu.VMEM((7, HALO + T, D_B), F32),
                        pltpu.VMEM((T, D_B), F32)] + _gather_sems(nl),
        compiler_params=_params(("arbitrary",)),
    )(x, gpre, gpost, wa, wb, bb, lng, lnb, g1g, pr0g, *late_split)
    return list(outs[:4]) + [o.reshape((N_CHIPS,) + a.shape) for o, a in zip(outs[4:], late)]


def _mix_bwd(dx1, x, o1, u, z1s, gpre, gpost, wa, wb, lng, lnb, g1g, pr0g, after):
    S, D = x.shape
    T = min(TILE_BWD, S)
    n = S // T
    hb = T // HALO
    phase_rows = min(T, PHASE_ROWS)

    def body(dx1_ref, x_ref, o1_ref, u_ref, uh_ref, z1_ref, gpre_ref, gpost_ref, wa_ref, wb_ref,
             lng_ref, lnb_ref, g1g_hbm, pr0g_hbm, _,
             dx_ref, dgpre_ref, dgpost_ref, dwa_ref, dwb_ref, dbb_ref, dlng_ref, dlnb_ref, cm1_hbm, cm2_hbm,
             win_v, wout_v, dwin_acc, dwout_acc, ext, ext2, shb, dwb_acc):
        i = pl.program_id(0)

        @pl.when(i == 0)
        def _():
            _copy_all(_cols(win_v, g1g_hbm, IN_BLK) + _rows(wout_v, pr0g_hbm, 0, ROW_BLK))
            dwin_acc[...] = jnp.zeros_like(dwin_acc)
            dwout_acc[...] = jnp.zeros_like(dwout_acc)
            dwb_acc[...] = jnp.zeros_like(dwb_acc)
            ext2[pl.ds(T, HALO), :] = jnp.zeros((HALO, D_A + D_B), F32)
            for ref in (dgpre_ref, dgpost_ref, dwa_ref, dbb_ref, dlng_ref, dlnb_ref):
                ref[...] = jnp.zeros_like(ref)

        o1n, r1 = _rms(o1_ref[...].astype(F32))
        dx1v = dx1_ref[...]
        d_o1, dgp = _rms_bwd(dx1v, o1n, r1, gpost_ref[...])
        dgpost_ref[...] += dgp
        d_o1b = d_o1.astype(BF16)
        dy = _mm_nt(d_o1b, wout_v[...])

        first = (i == n - 1).astype(F32)
        uh = uh_ref[...].astype(F32) * (1.0 - first)
        ext[pl.ds(0, HALO), pl.ds(0, D_A)] = uh[:, D_A:2 * D_A] * uh[:, 2 * D_A:3 * D_A]
        uf = u_ref[...].astype(F32)
        b_a = uf[:, 0:D_A]
        c_a = uf[:, D_A:2 * D_A]
        v_a = uf[:, 2 * D_A:3 * D_A]
        gv = uf[:, 3 * D_A:3 * D_A + D_B]
        sg = _sigmoid(uf[:, 3 * D_A + D_B:])
        ext[pl.ds(HALO, T), pl.ds(0, D_A)] = c_a * v_a
        ext[pl.ds(HALO, T), pl.ds(D_A, D_B)] = gv * sg
        conv_a = ext[pl.ds(HALO - 2, T), pl.ds(0, D_A)] * wa_ref[0:1, :]
        for k in range(1, CONV_A_W):
            conv_a = conv_a + ext[pl.ds(HALO - 2 + k, T), pl.ds(0, D_A)] * wa_ref[k:k + 1, :]
        z1 = z1_ref[...].astype(F32)
        mu = jnp.mean(z1, axis=-1, keepdims=True)
        zc = z1 - mu
        rstd = lax.rsqrt(jnp.mean(zc * zc, axis=-1, keepdims=True) + LN_EPS)
        zn = zc * rstd
        l = zn * lng_ref[...] + lnb_ref[...]
        sl = _sigmoid(l)
        y = jnp.concatenate([b_a * conv_a, l * sl], axis=-1).astype(BF16)
        dwout_acc[...] += _mm_tn(y, d_o1b)

        dy_a = dy[:, 0:D_A]
        dl = dy[:, D_A:] * (sl * (1.0 + l * (1.0 - sl)))
        dlng_ref[...] += jnp.sum(dl * zn, axis=0, keepdims=True)
        dlnb_ref[...] += jnp.sum(dl, axis=0, keepdims=True)
        dzn = dl * lng_ref[...]
        dz1 = rstd * (dzn - jnp.mean(dzn, axis=-1, keepdims=True) - zn * jnp.mean(dzn * zn, axis=-1, keepdims=True))
        dbb_ref[...] += jnp.sum(dz1, axis=0, keepdims=True)
        d_conv = dy_a * b_a
        ext2[pl.ds(0, T), pl.ds(0, D_A)] = d_conv
        ext2[pl.ds(0, T), pl.ds(D_A, D_B)] = dz1

        d_cv = ext2[pl.ds(CONV_A_W - 1, T), pl.ds(0, D_A)] * wa_ref[0:1, :]
        for k in range(1, CONV_A_W):
            d_cv = d_cv + ext2[pl.ds(CONV_A_W - 1 - k, T), pl.ds(0, D_A)] * wa_ref[k:k + 1, :]
        for k in range(CONV_A_W):
            dwa_ref[k:k + 1, :] += jnp.sum(d_conv * ext[pl.ds(HALO - 2 + k, T), pl.ds(0, D_A)], axis=0, keepdims=True)

        for row0 in range(0, T, phase_rows):
            _fill_phases(ext2, shb, phase_rows + HALO - 8, row0)

            def chunk(ci, carry, row0=row0):
                start = pl.multiple_of(row0 + ci * CONV_ROWS, 8)
                z0c = ext[pl.ds(HALO + start, CONV_ROWS), pl.ds(D_A, D_B)]
                acc = jnp.zeros((CONV_ROWS, D_B), F32)
                for k in range(CONV_B_W):
                    ahead = _phase_rows(ext2, shb, CONV_B_W - 1 - k, start, CONV_ROWS, row0)
                    acc = acc + ahead * wb_ref[k:k + 1, :]
                    dwb_acc[k] += _fold8(z0c * ahead)
                ext[pl.ds(HALO + start, CONV_ROWS), pl.ds(D_A, D_B)] = acc
                return carry

            lax.fori_loop(0, phase_rows // CONV_ROWS, chunk, 0)
        dz0 = ext[pl.ds(HALO, T), pl.ds(D_A, D_B)]
        du = jnp.concatenate([dy_a * conv_a, d_cv * v_a, d_cv * c_a, dz0 * sg, dz0 * gv * sg * (1.0 - sg)],
                             axis=-1).astype(BF16)
        dh = _mm_nt(du, win_v[...])
        xn, r0 = _rms(x_ref[...])
        dwin_acc[...] += _mm_tn((xn * gpre_ref[...]).astype(BF16), du)
        dxp, dg0 = _rms_bwd(dh, xn, r0, gpre_ref[...])
        dgpre_ref[...] += dg0
        dx_ref[...] = dx1v + dxp
        ext2[pl.ds(T, HALO), :] = ext2[pl.ds(0, HALO), :]

        @pl.when(i == n - 1)
        def _():
            for k in range(CONV_B_W):
                dwb_ref[k:k + 1, :] = jnp.sum(dwb_acc[k], axis=0, keepdims=True)
            win_v[...] = dwin_acc[...].astype(BF16)
            wout_v[...] = dwout_acc[...].astype(BF16)
            _copy_all([(win_v.at[:, pl.ds(IN_BLK * k, IN_BLK)], cm1_hbm.at[k]) for k in range(N_CHIPS)]
                      + [(wout_v.at[pl.ds(ROW_BLK * k, ROW_BLK), :], cm2_hbm.at[k]) for k in range(N_CHIPS)])

    rev = lambda w: pl.BlockSpec((T, w), lambda i: (n - 1 - i, 0))
    halo = pl.BlockSpec((HALO, D_IN_ALL), lambda i: (jnp.maximum((n - 1 - i) * hb - 1, 0), 0))
    return pl.pallas_call(
        body,
        name="mix_bwd",
        grid=(n,),
        in_specs=[rev(D), rev(D), rev(D), rev(D_IN_ALL), halo, rev(D_B), _full((1, D)),
                  _full((1, D)), _full((CONV_A_W, D_A)), _full((CONV_B_W, D_B)), _full((1, D_B)), _full((1, D_B)),
                  ANY, ANY, ANY],
        out_specs=[rev(D), _full((1, D)), _full((1, D)), _full((CONV_A_W, D_A)), _full((CONV_B_W, D_B)),
                   _full((1, D_B)), _full((1, D_B)), _full((1, D_B)), ANY, ANY],
        out_shape=[jax.ShapeDtypeStruct((S, D), F32), jax.ShapeDtypeStruct((1, D), F32),
                   jax.ShapeDtypeStruct((1, D), F32), jax.ShapeDtypeStruct((CONV_A_W, D_A), F32),
                   jax.ShapeDtypeStruct((CONV_B_W, D_B), F32), jax.ShapeDtypeStruct((1, D_B), F32),
                   jax.ShapeDtypeStruct((1, D_B), F32), jax.ShapeDtypeStruct((1, D_B), F32),
                   jax.ShapeDtypeStruct((N_CHIPS, D, IN_BLK), BF16),
                   jax.ShapeDtypeStruct((N_CHIPS, ROW_BLK, D), BF16)],
        scratch_shapes=[pltpu.VMEM((D, D_IN_ALL), BF16), pltpu.VMEM((D_A + D_B, D), BF16),
                        pltpu.VMEM((D, D_IN_ALL), F32), pltpu.VMEM((D_A + D_B, D), F32),
                        pltpu.VMEM((HALO + T, D_A + D_B), F32), pltpu.VMEM((HALO + T, D_A + D_B), F32),
                        pltpu.VMEM((7, HALO + phase_rows, D_B), F32), pltpu.VMEM((CONV_B_W, 8, D_B), F32)],
        compiler_params=_params(("arbitrary",), V7X_VMEM_LIMIT_HIGH),
    )(dx1, x, o1, u, u, z1s, gpre, gpost, wa, wb, lng, lnb, g1g, pr0g, after)


def _mem_kv(mem, gmem, pr1g):
    M, D = mem.shape

    def body(mem_ref, g_ref, pr1g_hbm, memn_ref, k_ref, v_ref, wk_v, wv_v):
        _copy_all(_rows(wk_v, pr1g_hbm, P1_K, ROW_BLK) + _rows(wv_v, pr1g_hbm, P1_V, ROW_BLK))
        mn, _ = _rms(mem_ref[...])
        mb = (mn * g_ref[...]).astype(BF16)
        memn_ref[...] = mb
        k_ref[...] = _mm(mb, wk_v[...]).astype(BF16)
        v_ref[...] = _mm(mb, wv_v[...]).astype(BF16)

    return pl.pallas_call(
        body,
        name="mem_kv",
        grid=(1,),
        in_specs=[_full((M, D)), _full((1, D)), ANY],
        out_specs=[_full((M, D))] * 3,
        out_shape=[jax.ShapeDtypeStruct((M, D), BF16)] * 3,
        scratch_shapes=[pltpu.VMEM((D, D), BF16), pltpu.VMEM((D, D), BF16)],
        compiler_params=_params(("arbitrary",)),
    )(mem, gmem, pr1g)


def _attend(qb, kb, vb):
    scale = HEAD_DIM ** -0.5
    ps, os_ = [], []
    for hd in range(XA_HEADS):
        cols = slice(HEAD_DIM * hd, HEAD_DIM * (hd + 1))
        s = _mm_nt(qb[:, cols], kb[:, cols]) * scale
        e = jnp.exp(s - jnp.max(s, axis=-1, keepdims=True))
        p = e * (1.0 / jnp.sum(e, axis=-1, keepdims=True))
        ps.append(p)
        os_.append(_mm(p.astype(BF16), vb[:, cols]))
    return ps, jnp.concatenate(os_, axis=-1).astype(BF16)


def _xattn_fwd(x1, gpre, gpost, kb, vb, pr1g):
    S, D = x1.shape
    M = kb.shape[0]
    T = min(TILE_XATTN_FWD, S)
    n = S // T

    def body(x1_ref, gpre_ref, gpost_ref, k_ref, v_ref, pr1g_hbm, x2_ref, q_ref, o2_ref, rx_ref, ro_ref, wq_v, wo_v):
        @pl.when(pl.program_id(0) == 0)
        def _():
            _copy_all(_rows(wq_v, pr1g_hbm, P1_Q, ROW_BLK) + _rows(wo_v, pr1g_hbm, P1_O, ROW_BLK))

        xv = x1_ref[...]
        xn, rx = _rms(xv)
        rx_ref[...] = rx
        qb = _mm((xn * gpre_ref[...]).astype(BF16), wq_v[...]).astype(BF16)
        q_ref[...] = qb
        _, ob = _attend(qb, k_ref[...], v_ref[...])
        o2 = _mm(ob, wo_v[...])
        o2_ref[...] = o2.astype(BF16)
        o2n, ro = _rms(o2)
        ro_ref[...] = ro
        x2_ref[...] = xv + o2n * gpost_ref[...]

    tok = lambda w: pl.BlockSpec((T, w), lambda i: (i, 0))
    return pl.pallas_call(
        body,
        name="xattn_fwd",
        grid=(n,),
        in_specs=[tok(D), _full((1, D)), _full((1, D)), _full((M, D)), _full((M, D)), ANY],
        out_specs=[tok(D), tok(D), tok(D), tok(1), tok(1)],
        out_shape=[jax.ShapeDtypeStruct((S, D), F32), jax.ShapeDtypeStruct((S, D), BF16),
                   jax.ShapeDtypeStruct((S, D), BF16), jax.ShapeDtypeStruct((S, 1), F32),
                   jax.ShapeDtypeStruct((S, 1), F32)],
        scratch_shapes=[pltpu.VMEM((D, D), BF16), pltpu.VMEM((D, D), BF16)],
        compiler_params=_params(("arbitrary",)),
    )(x1, gpre, gpost, kb, vb, pr1g)


def _xattn_bwd(dx3, dh3p, x2, x1, o2, q, rx2, ro2, rx1, kb, vb, gffn, gpost, gpre, pr1g, cf1, cf2):
    S, D = x1.shape
    M = kb.shape[0]
    T = min(TILE_XATTN_BWD, S)
    n = S // T
    scale = HEAD_DIM ** -0.5
    nparts = dh3p.shape[0]
    cfs = [_by_halves(cf1), _by_halves(cf2)]

    def body(*refs):
        dx3_ref, dh3_refs = refs[0], refs[1:1 + nparts]
        (x2_ref, x1_ref, o2_ref, q_ref, rx2_ref, ro2_ref, rx1_ref, k_ref, v_ref, gffn_ref, gpost_ref, gpre_ref,
         pr1g_hbm, cf1_hbm, cf2_hbm,
         dx1_ref, dgffn_ref, dgpost_ref, dgpre_ref, dk_ref, dv_ref, cx_hbm, yf1_hbm, yf2_hbm,
         wq_v, wo_v, dwq_acc, dwo_acc, send, recv) = refs[1 + nparts:]
        i = pl.program_id(0)

        @pl.when(i == 0)
        def _():
            _scatter_start([cf1_hbm, cf2_hbm], [yf1_hbm, yf2_hbm], send, recv)
            _copy_all(_rows(wq_v, pr1g_hbm, P1_Q, ROW_BLK) + _rows(wo_v, pr1g_hbm, P1_O, ROW_BLK))
            dwq_acc[...] = jnp.zeros_like(dwq_acc)
            dwo_acc[...] = jnp.zeros_like(dwo_acc)
            for ref in (dgffn_ref, dgpost_ref, dgpre_ref, dk_ref, dv_ref):
                ref[...] = jnp.zeros_like(ref)

        r2 = rx2_ref[...]
        x2n = x2_ref[...] * r2
        dh3 = dh3_refs[0][...].astype(F32)
        for ref in dh3_refs[1:]:
            dh3 = dh3 + ref[...].astype(F32)
        dxp, dg = _rms_bwd(dh3, x2n, r2, gffn_ref[...])
        dgffn_ref[...] += dg
        dx2 = dx3_ref[...] + dxp
        ro = ro2_ref[...]
        o2n = o2_ref[...].astype(F32) * ro
        d_o2, dg = _rms_bwd(dx2, o2n, ro, gpost_ref[...])
        dgpost_ref[...] += dg
        d_o2b = d_o2.astype(BF16)
        d_o = _mm_nt(d_o2b, wo_v[...]).astype(BF16)
        qb = q_ref[...]
        kv = k_ref[...]
        vv = v_ref[...]
        ps, ob = _attend(qb, kv, vv)
        dwo_acc[...] += _mm_tn(ob, d_o2b)
        dqs = []
        for hd in range(XA_HEADS):
            cols = slice(HEAD_DIM * hd, HEAD_DIM * (hd + 1))
            p = ps[hd]
            dp = _mm_nt(d_o[:, cols], vv[:, cols])
            dv_ref[:, cols] += _mm_tn(p.astype(BF16), d_o[:, cols])
            ds = (p * (dp - jnp.sum(p * dp, axis=-1, keepdims=True)) * scale).astype(BF16)
            dqs.append(_mm(ds, kv[:, cols]))
            dk_ref[:, cols] += _mm_tn(ds, qb[:, cols])
        dq = jnp.concatenate(dqs, axis=-1).astype(BF16)
        dh2 = _mm_nt(dq, wq_v[...])
        r1 = rx1_ref[...]
        x1n = x1_ref[...] * r1
        dwq_acc[...] += _mm_tn((x1n * gpre_ref[...]).astype(BF16), dq)
        dxp, dg = _rms_bwd(dh2, x1n, r1, gpre_ref[...])
        dgpre_ref[...] += dg
        dx1_ref[...] = dx2 + dxp

        @pl.when(i == n - 1)
        def _():
            wq_v[...] = dwq_acc[...].astype(BF16)
            wo_v[...] = dwo_acc[...].astype(BF16)
            _copy_all([(w.at[pl.ds(ROW_BLK * k, ROW_BLK), :], cx_hbm.at[k, pl.ds(row0, ROW_BLK), :])
                       for w, row0 in ((wq_v, GX_Q), (wo_v, GX_O)) for k in range(N_CHIPS)])
            _scatter_wait([cf1_hbm, cf2_hbm], [yf1_hbm, yf2_hbm], send, recv)

    tok = lambda w: pl.BlockSpec((T, w), lambda i: (i, 0))
    part = lambda j: pl.BlockSpec((None, T, D), lambda i: (j, i, 0))
    return pl.pallas_call(
        body,
        name="xattn_bwd",
        grid=(n,),
        in_specs=[tok(D)] + [part(j) for j in range(nparts)] + [tok(D), tok(D), tok(D), tok(D), tok(1), tok(1), tok(1),
                                                                 _full((M, D)), _full((M, D)), _full((1, D)),
                                                                 _full((1, D)), _full((1, D)), ANY, ANY, ANY],
        out_specs=[tok(D), _full((1, D)), _full((1, D)), _full((1, D)), _full((M, D)), _full((M, D)), ANY, ANY, ANY],
        out_shape=[jax.ShapeDtypeStruct((S, D), F32), jax.ShapeDtypeStruct((1, D), F32),
                   jax.ShapeDtypeStruct((1, D), F32), jax.ShapeDtypeStruct((1, D), F32),
                   jax.ShapeDtypeStruct((M, D), F32), jax.ShapeDtypeStruct((M, D), F32),
                   jax.ShapeDtypeStruct((N_CHIPS, D, D), BF16), _slots(cfs[0]), _slots(cfs[1])],
        scratch_shapes=[pltpu.VMEM((D, D), BF16), pltpu.VMEM((D, D), BF16),
                        pltpu.VMEM((D, D), F32), pltpu.VMEM((D, D), F32)] + _scatter_sems(2),
        compiler_params=_params(("arbitrary",)),
    )(dx3, *([dh3p] * nparts), x2, x1, o2, q, rx2, ro2, rx1, kb, vb, gffn, gpost, gpre, pr1g, *cfs)


def _mem_bwd(dk, dv, mem, memn, gmem, pr1g, cx_in):
    M, D = mem.shape

    def body(dk_ref, dv_ref, mem_ref, memn_ref, g_ref, pr1g_hbm, cx_hbm, dg_ref, cx_out, wk_v, wv_v):
        del cx_hbm
        _copy_all(_rows(wk_v, pr1g_hbm, P1_K, ROW_BLK) + _rows(wv_v, pr1g_hbm, P1_V, ROW_BLK))
        dkb = dk_ref[...].astype(BF16)
        dvb = dv_ref[...].astype(BF16)
        mb = memn_ref[...]
        dmn = _mm_nt(dkb, wk_v[...]) + _mm_nt(dvb, wv_v[...])
        mn, _ = _rms(mem_ref[...])
        dg_ref[...] = jnp.sum(dmn * mn, axis=0, keepdims=True)
        wk_v[...] = _mm_tn(mb, dkb).astype(BF16)
        wv_v[...] = _mm_tn(mb, dvb).astype(BF16)
        _copy_all([(w.at[pl.ds(ROW_BLK * k, ROW_BLK), :], cx_out.at[k, pl.ds(row0, ROW_BLK), :])
                   for w, row0 in ((wk_v, GX_K), (wv_v, GX_V)) for k in range(N_CHIPS)])

    return pl.pallas_call(
        body,
        name="mem_bwd",
        grid=(1,),
        in_specs=[_full((M, D)), _full((M, D)), _full((M, D)), _full((M, D)), _full((1, D)), ANY, ANY],
        out_specs=[_full((1, D)), ANY],
        out_shape=[jax.ShapeDtypeStruct((1, D), F32), jax.ShapeDtypeStruct(cx_in.shape, BF16)],
        input_output_aliases={6: 1},
        scratch_shapes=[pltpu.VMEM((D, D), BF16), pltpu.VMEM((D, D), BF16)],
        compiler_params=_params(("arbitrary",)),
    )(dk, dv, mem, memn, gmem, pr1g, cx_in)


def _ffn_fwd(x2, target, gpre, gpost, g2g, pr1g):
    S, D = x2.shape
    T = min(TILE_FFN, S)
    n = S // T

    def body(x2_ref, t_ref, gpre_ref, gpost_ref, g2g_hbm, pr1g_hbm,
             h3_ref, g_hbm, u_hbm, do3_ref, dx3_ref, loss_ref, dgpost_ref, rx_ref, wg_v, wu_v, wd_v, gst, ust, sem):
        i = pl.program_id(0)

        @pl.when(i == 0)
        def _():
            _copy_all([(g2g_hbm.at[:, pl.ds(0, D), :], wg_v), (g2g_hbm.at[:, pl.ds(D, D), :], wu_v),
                       (pr1g_hbm.at[:, pl.ds(P1_DOWN, FF_BLK), :], wd_v)])
            loss_ref[...] = jnp.zeros_like(loss_ref)
            dgpost_ref[...] = jnp.zeros_like(dgpost_ref)

        xv = x2_ref[...]
        xn, rx = _rms(xv)
        rx_ref[...] = rx
        hb = (xn * gpre_ref[...]).astype(BF16)
        h3_ref[...] = hb
        o3 = jnp.zeros((T, D), F32)
        out = [None, None]
        for c in range(N_CHIPS):
            slot = c % 2
            if out[slot] is not None:
                for cp in out[slot]:
                    cp.wait()
            g = _mm(hb, wg_v[c])
            u = _mm(hb, wu_v[c])
            gst[slot] = g.astype(BF16)
            ust[slot] = u.astype(BF16)
            out[slot] = (pltpu.make_async_copy(gst.at[slot], g_hbm.at[c, i], sem.at[0, slot]),
                         pltpu.make_async_copy(ust.at[slot], u_hbm.at[c, i], sem.at[1, slot]))
            for cp in out[slot]:
                cp.start()
            o3 = o3 + _mm((g * _sigmoid(g) * u).astype(BF16), wd_v[c])
        for pair in out:
            for cp in pair:
                cp.wait()
        o3n, r3 = _rms(o3)
        diff = xv + o3n * gpost_ref[...] - t_ref[...]
        sq = jnp.sum(jnp.sum(diff * diff, axis=-1, keepdims=True), axis=0, keepdims=True)
        loss_ref[...] += sq * (0.5 / D)
        dx3 = diff * (1.0 / D)
        dx3_ref[...] = dx3
        d_o3, dg = _rms_bwd(dx3, o3n, r3, gpost_ref[...])
        dgpost_ref[...] += dg
        do3_ref[...] = d_o3.astype(BF16)

    tok = lambda w: pl.BlockSpec((T, w), lambda i: (i, 0))
    h3, gs, us, do3, dx3, loss, dgpost, rx2 = pl.pallas_call(
        body,
        name="ffn_fwd",
        grid=(n,),
        in_specs=[tok(D), tok(D), _full((1, D)), _full((1, D)), ANY, ANY],
        out_specs=[tok(D), ANY, ANY, tok(D), tok(D), _full((1, 128)), _full((1, D)), tok(1)],
        out_shape=[jax.ShapeDtypeStruct((S, D), BF16), jax.ShapeDtypeStruct((N_CHIPS, n, T, FF_BLK), BF16),
                   jax.ShapeDtypeStruct((N_CHIPS, n, T, FF_BLK), BF16), jax.ShapeDtypeStruct((S, D), BF16),
                   jax.ShapeDtypeStruct((S, D), F32), jax.ShapeDtypeStruct((1, 128), F32),
                   jax.ShapeDtypeStruct((1, D), F32), jax.ShapeDtypeStruct((S, 1), F32)],
        scratch_shapes=[pltpu.VMEM((N_CHIPS, D, FF_BLK), BF16), pltpu.VMEM((N_CHIPS, D, FF_BLK), BF16),
                        pltpu.VMEM((N_CHIPS, FF_BLK, D), BF16), pltpu.VMEM((2, T, FF_BLK), BF16),
                        pltpu.VMEM((2, T, FF_BLK), BF16), pltpu.SemaphoreType.DMA((2, 2))],
        compiler_params=_params(("arbitrary",)),
    )(x2, target, gpre, gpost, g2g, pr1g)
    return h3, gs.reshape(N_CHIPS, S, FF_BLK), us.reshape(N_CHIPS, S, FF_BLK), do3, dx3, loss, dgpost, rx2


def _ffn_bwd(h3, do3, gs, us, g2g, pr1g):
    S, D = h3.shape
    T = min(TILE_FFN_BWD, S)
    n = S // T
    NP = FFN_BWD_BLOCKS

    def body(h3_ref, do3_ref, g_ref, u_ref, g2g_hbm, pr1g_hbm, dh3_ref, cf1_hbm, cf2_hbm,
             wg_v, wu_v, wd_v, dwg_acc, dwu_acc, dwd_acc):
        jp = pl.program_id(0)
        i = pl.program_id(1)
        blocks = pl.ds(NP * jp, NP)

        @pl.when(i == 0)
        def _():
            _copy_all([(g2g_hbm.at[blocks, pl.ds(0, D), :], wg_v), (g2g_hbm.at[blocks, pl.ds(D, D), :], wu_v),
                       (pr1g_hbm.at[blocks, pl.ds(P1_DOWN, FF_BLK), :], wd_v)])
            dwg_acc[...] = jnp.zeros_like(dwg_acc)
            dwu_acc[...] = jnp.zeros_like(dwu_acc)
            dwd_acc[...] = jnp.zeros_like(dwd_acc)

        hb = h3_ref[...]
        d_o3 = do3_ref[...]
        dh = jnp.zeros((T, D), F32)
        for c in range(NP):
            da = _mm_nt(d_o3, wd_v[c])
            g = g_ref[c].astype(F32)
            u = u_ref[c].astype(F32)
            sg = _sigmoid(g)
            sl = g * sg
            dwd_acc[c] += _mm_tn((sl * u).astype(BF16), d_o3)
            dub = (da * sl).astype(BF16)
            dgb = (da * u * (sg * (1.0 + g * (1.0 - sg)))).astype(BF16)
            dwg_acc[c] += _mm_tn(dgb, hb)
            dwu_acc[c] += _mm_tn(dub, hb)
            dh = dh + _mm_nt(dgb, wg_v[c]) + _mm_nt(dub, wu_v[c])
        dh3_ref[...] = dh.astype(BF16)

        @pl.when(i == n - 1)
        def _():
            wd_v[...] = dwg_acc[...].astype(BF16)
            pltpu.sync_copy(wd_v, cf1_hbm.at[blocks, pl.ds(0, FF_BLK), :])
            wd_v[...] = dwu_acc[...].astype(BF16)
            pltpu.sync_copy(wd_v, cf1_hbm.at[blocks, pl.ds(FF_BLK, FF_BLK), :])
            wd_v[...] = dwd_acc[...].astype(BF16)
            pltpu.sync_copy(wd_v, cf2_hbm.at[blocks])

    tok = lambda w: pl.BlockSpec((T, w), lambda jp, i: (i, 0))
    blk = pl.BlockSpec((NP, T, FF_BLK), lambda jp, i: (jp, i, 0))
    return pl.pallas_call(
        body,
        name="ffn_bwd",
        grid=(N_CHIPS // NP, n),
        in_specs=[tok(D), tok(D), blk, blk, ANY, ANY],
        out_specs=[pl.BlockSpec((None, T, D), lambda jp, i: (jp, i, 0)), ANY, ANY],
        out_shape=[jax.ShapeDtypeStruct((N_CHIPS // NP, S, D), BF16),
                   jax.ShapeDtypeStruct((N_CHIPS, 2 * FF_BLK, D), BF16),
                   jax.ShapeDtypeStruct((N_CHIPS, FF_BLK, D), BF16)],
        scratch_shapes=[pltpu.VMEM((NP, D, FF_BLK), BF16), pltpu.VMEM((NP, D, FF_BLK), BF16),
                        pltpu.VMEM((NP, FF_BLK, D), BF16), pltpu.VMEM((NP, FF_BLK, D), F32),
                        pltpu.VMEM((NP, FF_BLK, D), F32), pltpu.VMEM((NP, FF_BLK, D), F32)],
        compiler_params=_params(("arbitrary", "arbitrary")),
    )(h3, do3, gs, us, g2g, pr1g)


IN_HBM = pl.BlockSpec(memory_space=pltpu.HBM)
IN_SEMAPHORES = pl.BlockSpec(memory_space=pltpu.SEMAPHORE)


def _split_copies(refs, na, nw, sems):
    srcs, lands = refs[:na + nw], refs[na + nw:2 * (na + nw)]
    send, recv = sems
    x, y, c, _ = _my_place()
    me = 4 * x + 2 * y + c
    pairs = []
    for d, ((px, py, pc), pidx) in enumerate(_peers(x, y, c)):
        for a in range(na + nw):
            k = d * (na + nw) + a
            src = srcs[a].at[2 * px + py, pc] if a < na else srcs[a]
            pairs.append((_remote(src, lands[a].at[me], send.at[k], recv.at[k], (px, py, pc)),
                          _remote(src, lands[a].at[pidx], send.at[k], recv.at[k], (px, py, pc))))
    return pairs


def _exchange_start(contribs, whole, tag):
    na, nw = len(contribs), len(whole)
    cs = [_by_halves(a) for a in contribs]
    flying = cs + list(whole) + [lax.empty(_slots(a).shape, a.dtype) for a in cs] + [
        lax.empty((N_DEV,) + a.shape, a.dtype) for a in whole]
    nf = len(flying)

    def body(*refs):
        for mine, _ in _split_copies(refs[:nf], na, nw, refs[nf:nf + 2]):
            mine.start()
        refs[-1][...] = jnp.zeros_like(refs[-1])

    ncopies = (N_DEV - 1) * (na + nw)
    outs = pl.pallas_call(
        body,
        name="exchange_start_" + tag,
        in_specs=[IN_HBM] * nf,
        out_specs=[IN_SEMAPHORES] * 2 + [IN_HBM] * nf + [pl.BlockSpec(memory_space=pltpu.VMEM)],
        out_shape=[pltpu.SemaphoreType.DMA((ncopies,)), pltpu.SemaphoreType.DMA((ncopies,))]
        + [pltpu.HBM(a.shape, a.dtype) for a in flying] + [jax.ShapeDtypeStruct((8, 128), F32)],
        input_output_aliases={k: 2 + k for k in range(nf)},
        compiler_params=pltpu.CompilerParams(has_side_effects=pltpu.SideEffectType.DATAFLOW_SIDE_EFFECTING),
    )(*[pltpu.with_memory_space_constraint(a, pltpu.HBM) for a in flying])
    return outs[:2], outs[2:2 + nf], outs[-1]


def _exchange_wait(sems, flying, na, nw, after, tag):
    nf = len(flying)

    def body(*refs):
        for mine, theirs in _split_copies(refs[:nf], na, nw, refs[nf:nf + 2]):
            theirs.wait_recv()
            mine.wait_send()

    outs = pl.pallas_call(
        body,
        name="exchange_wait_" + tag,
        in_specs=[IN_HBM] * nf + [IN_SEMAPHORES] * 2 + [ANY] * len(after),
        out_specs=[IN_HBM] * nf,
        out_shape=[pltpu.HBM(a.shape, a.dtype) for a in flying],
        input_output_aliases={k: k for k in range(nf)},
        compiler_params=pltpu.CompilerParams(has_side_effects=pltpu.SideEffectType.DATAFLOW_SIDE_EFFECTING),
    )(*flying, *sems, *after)
    return outs[na + nw:]


def _sum_peers(parts, own, place, steps, tag, after=()):
    _, rows, w = parts.shape
    tr = rows // steps
    if own.ndim == 3:
        own = _by_halves(own)

    def body(place_ref, *refs):
        p_refs, own_ref, o_ref = refs[:N_DEV], refs[N_DEV], refs[-1]
        me = place_ref[2]
        acc = None
        for s in range(N_DEV):
            term = jnp.where(me == s, own_ref[...], p_refs[s][...]).astype(F32)
            acc = term if acc is None else acc + term
        o_ref[...] = acc

    def other(s):
        return lambda i, pr: (jnp.where(pr[2] == s, (s + 1) % N_DEV, s), i, 0)

    own_spec = (pl.BlockSpec((None, None, tr, w), lambda i, pr: (pr[0], pr[1], i, 0)) if own.ndim == 4 else
                pl.BlockSpec((tr, w), lambda i, pr: (i, 0)))
    out_spec = (pl.BlockSpec((None, tr, w), lambda i, pr: (pr[1], i, 0)) if own.ndim == 4 else
                pl.BlockSpec((tr, w), lambda i, pr: (i, 0)))
    return pl.pallas_call(
        body,
        name="sum_peers_" + tag,
        grid_spec=pltpu.PrefetchScalarGridSpec(
            num_scalar_prefetch=1,
            grid=(steps,),
            in_specs=[pl.BlockSpec((None, tr, w), other(s)) for s in range(N_DEV)] + [own_spec] + [ANY] * len(after),
            out_specs=out_spec,
        ),
        out_shape=jax.ShapeDtypeStruct((2, rows, w) if own.ndim == 4 else (rows, w), F32),
        compiler_params=_params(("arbitrary",)),
    )(place, *([parts] * N_DEV), own, *after)


def _pair_gather(bufs, tag):
    np_ = len(bufs)

    def body(*refs):
        srcs, dsts = refs[:np_], refs[np_:2 * np_]
        send, recv = refs[2 * np_:]
        x, y, c, _ = _my_place()
        cps = []
        for p in range(np_):
            cp = _remote(srcs[p].at[c], dsts[p].at[c], send.at[p], recv.at[p], (x, y, 1 - c))
            cp.start()
            cps.append(cp)
        for p, cp in enumerate(cps):
            other = dsts[p].at[1 - c]
            _remote(other, other, send.at[p], recv.at[p], (x, y, 1 - c)).wait_recv()
            cp.wait_send()

    outs = pl.pallas_call(
        body,
        name="pair_gather_" + tag,
        in_specs=[ANY] * np_,
        out_specs=[ANY] * np_,
        out_shape=[jax.ShapeDtypeStruct(a.shape, F32) for a in bufs],
        input_output_aliases={p: p for p in range(np_)},
        scratch_shapes=[pltpu.SemaphoreType.DMA((np_,))] * 2,
    )(*bufs)
    return [o.reshape(2 * a.shape[1], a.shape[2]) for o, a in zip(outs, bufs)]


def _adamw(gsrc, row0, w, m, v, tr, tag):
    rows, width = w.shape
    off = row0 // tr
    bc1 = 1.0 - ADAM_B1 ** ADAM_STEP
    bc2 = 1.0 - ADAM_B2 ** ADAM_STEP

    def body(g_ref, w_ref, m_ref, v_ref, go_ref, d_ref, mo_ref, vo_ref):
        g = g_ref[...]
        m2 = ADAM_B1 * m_ref[...] + (1.0 - ADAM_B1) * g
        v2 = ADAM_B2 * v_ref[...] + (1.0 - ADAM_B2) * (g * g)
        go_ref[...] = g
        mo_ref[...] = m2
        vo_ref[...] = v2
        d_ref[...] = -ADAM_LR * ((m2 / bc1) / (jnp.sqrt(v2 / bc2) + ADAM_EPS) + ADAM_WD * w_ref[...])

    here = pl.BlockSpec((tr, width), lambda i: (i, 0))
    return pl.pallas_call(
        body,
        name="adamw_" + tag,
        grid=(rows // tr,),
        in_specs=[pl.BlockSpec((tr, width), lambda i: (off + i, 0)), here, here, here],
        out_specs=[here] * 4,
        out_shape=[jax.ShapeDtypeStruct((rows, width), F32)] * 4,
        compiler_params=_params(("arbitrary",)),
    )(gsrc, w, m, v)


def kernel(x, mem, mix_pre_g, w_mix_in, conv_a_w, conv_b_w, conv_b_b, ln_b_g, ln_b_b, w_mix_out, mix_post_g, xa_pre_g, mem_norm_g, w_q, w_k, w_v, w_o, xa_post_g, ffn_pre_g, w_gate, w_up, w_down, ffn_post_g, loss_target, m_mix_pre_g, m_w_mix_in, m_conv_a_w, m_conv_b_w, m_conv_b_b, m_ln_b_g, m_ln_b_b, m_w_mix_out, m_mix_post_g, m_xa_pre_g, m_mem_norm_g, m_w_q, m_w_k, m_w_v, m_w_o, m_xa_post_g, m_ffn_pre_g, m_w_gate, m_w_up, m_w_down, m_ffn_post_g, v_mix_pre_g, v_w_mix_in, v_conv_a_w, v_conv_b_w, v_conv_b_b, v_ln_b_g, v_ln_b_b, v_w_mix_out, v_mix_post_g, v_xa_pre_g, v_mem_norm_g, v_w_q, v_w_k, v_w_v, v_w_o, v_xa_post_g, v_ffn_pre_g, v_w_gate, v_w_up, v_w_down, v_ffn_post_g):
    given = dict(locals())
    names = ["mix_pre_g", "w_mix_in", "conv_a_w", "conv_b_w", "conv_b_b", "ln_b_g", "ln_b_b", "w_mix_out",
             "mix_post_g", "xa_pre_g", "mem_norm_g", "w_q", "w_k", "w_v", "w_o", "xa_post_g", "ffn_pre_g",
             "w_gate", "w_up", "w_down", "ffn_post_g"]
    row = lambda a: a.reshape(1, -1)
    cx, cy, cc = lax.axis_index("x"), lax.axis_index("y"), lax.axis_index("c")
    chip = 2 * cx + cy
    ca_blk = conv_a_w.shape[1]

    conv_rows = CONV_A_W + CONV_B_W
    sw = jnp.concatenate([conv_a_w, conv_b_w, jnp.zeros((SMALL_W_ROWS - conv_rows, ca_blk), F32)], axis=0)
    g1g, pr0g, swg = _gather_weights([w_mix_in.astype(BF16), w_mix_out.astype(BF16), sw])
    conv_full = jnp.transpose(swg[:, :conv_rows, :], (1, 0, 2)).reshape(conv_rows, N_CHIPS * ca_blk)
    wa, wb = conv_full[:CONV_A_W], conv_full[CONV_A_W:]
    pr1 = jnp.concatenate([w_q, w_k, w_v, w_o, w_down], axis=0).astype(BF16)
    g2 = jnp.concatenate([w_gate, w_up], axis=0).astype(BF16)

    xs, ms, tgt = x[0], mem[0], loss_target[0]
    x1, u, o1, z1, pr1g, g2g = _mix_fwd(xs, row(mix_pre_g), row(mix_post_g), wa, wb, row(conv_b_b), row(ln_b_g),
                                        row(ln_b_b), g1g, pr0g, [pr1, g2])
    memn, kb, vb = _mem_kv(ms, row(mem_norm_g), pr1g)
    x2, q, o2, rx1, ro2 = _xattn_fwd(x1, row(xa_pre_g), row(xa_post_g), kb, vb, pr1g)
    h3, gs, us, do3, dx3, loss_part, d_ffn_post, rx2 = _ffn_fwd(x2, tgt, row(ffn_pre_g), row(ffn_post_g), g2g, pr1g)

    dh3p, cf1, cf2 = _ffn_bwd(h3, do3, gs, us, g2g, pr1g)
    dx1, d_ffn_pre, d_xa_post, d_xa_pre, dk, dv, cxa, yf1, yf2 = _xattn_bwd(
        dx3, dh3p, x2, x1, o2, q, rx2, ro2, rx1, kb, vb, row(ffn_pre_g), row(xa_post_g), row(xa_pre_g), pr1g, cf1, cf2)
    d_mem_g, cxa = _mem_bwd(dk, dv, ms, memn, row(mem_norm_g), pr1g, cxa)
    sems_x, flying_x, token_x = _exchange_start([cxa], [], "attn")
    dx, d_mix_pre, d_mix_post, dwa, dwb, dbb, dlng, dlnb, cm1, cm2 = _mix_bwd(
        dx1, xs, o1, u, z1, row(mix_pre_g), row(mix_post_g), wa, wb, row(ln_b_g), row(ln_b_b), g1g, pr0g, token_x)
    (yx,) = _exchange_wait(sems_x, flying_x, 1, 0, [d_mix_pre], "attn")

    small_parts = [d_mix_pre, dwa, dwb, dbb, dlng, dlnb, d_mix_post, d_xa_pre, d_mem_g, d_xa_post, d_ffn_pre,
                   d_ffn_post, loss_part]
    sizes = [p.size for p in small_parts]
    small = jnp.concatenate([p.reshape(-1) for p in small_parts])
    small_rows = -(-small.size // (8 * 128)) * 8
    small = jnp.pad(small, (0, small_rows * 128 - small.size)).reshape(small_rows, 128)
    sems, flying, token = _exchange_start([cm1, cm2], [small], "mix")
    place = jnp.stack([chip, cc, 2 * chip + cc]).astype(jnp.int32)

    res = {}

    def update(nm, src, row0, tr, transposed=False):
        view = (lambda a: a.T) if transposed else (lambda a: a)
        outs = _adamw(src, row0, view(given[nm]), view(given["m_" + nm]), view(given["v_" + nm]), tr, nm)
        res[nm] = [view(o) for o in outs]

    early = [(yf1, cf1, "gate_up"), (yf2, cf2, "down"), (yx, cxa, "attn")]
    r_gu, r_down, r_attn = _pair_gather([_sum_peers(y, c, place, 2, t, after=(token,)) for y, c, t in early], "early")
    update("w_gate", r_gu, 0, FF_BLK // 2, transposed=True)
    update("w_up", r_gu, FF_BLK, FF_BLK // 2, transposed=True)
    update("w_down", r_down, 0, FF_BLK // 2)
    for nm, row0 in (("w_q", GX_Q), ("w_k", GX_K), ("w_v", GX_V), ("w_o", GX_O)):
        update(nm, r_attn, row0, ROW_BLK)
    done = [res[nm][1] for nm in ("w_gate", "w_up", "w_down", "w_q", "w_k", "w_v", "w_o")]

    ym1, ym2, small_all = _exchange_wait(sems, flying, 2, 1, done, "mix")
    r_in, r_out = _pair_gather([_sum_peers(y, c, place, 2, t) for y, c, t in
                                [(ym1, cm1, "mix_in"), (ym2, cm2, "mix_out")]], "late")
    update("w_mix_in", r_in, 0, D_MODEL // 2)
    update("w_mix_out", r_out, 0, ROW_BLK)
    small_sum = _sum_peers(small_all, small, place, 1, "small").reshape(-1)
    red, pos = [], 0
    for p, sz in zip(small_parts, sizes):
        red.append(small_sum[pos:pos + sz].reshape(p.shape))
        pos += sz
    (r_mix_pre, r_wa, r_wb, r_bb, r_lng, r_lnb, r_mix_post, r_xa_pre, r_mem_g, r_xa_post, r_ffn_pre, r_ffn_post,
     r_loss) = red
    loss = r_loss[0, 0]

    small_grads = {"mix_pre_g": r_mix_pre, "conv_b_b": r_bb, "ln_b_g": r_lng, "ln_b_b": r_lnb,
                   "mix_post_g": r_mix_post, "xa_pre_g": r_xa_pre, "mem_norm_g": r_mem_g, "xa_post_g": r_xa_post,
                   "ffn_pre_g": r_ffn_pre, "ffn_post_g": r_ffn_post,
                   "conv_a_w": lax.dynamic_slice_in_dim(r_wa, chip * ca_blk, ca_blk, axis=1),
                   "conv_b_w": lax.dynamic_slice_in_dim(r_wb, chip * ca_blk, ca_blk, axis=1)}
    small_names = list(small_grads)

    def packed(prefix, grads=None):
        flat = jnp.concatenate([(grads[nm] if grads else given[prefix + nm]).reshape(-1) for nm in small_names])
        rows8 = -(-flat.size // (8 * 128)) * 8
        return jnp.pad(flat, (0, rows8 * 128 - flat.size)).reshape(rows8, 128)

    gp = packed("", small_grads)
    outs = _adamw(gp, 0, packed(""), packed("m_"), packed("v_"), gp.shape[0], "small")
    pos = 0
    for nm in small_names:
        shape = given[nm].shape
        sz = given[nm].size
        res[nm] = [o.reshape(-1)[pos:pos + sz].reshape(shape) for o in outs]
        pos += sz

    return (loss, dx[None], *[res[nm][0] for nm in names], *[res[nm][1] for nm in names],
            *[res[nm][2] for nm in names], *[res[nm][3] for nm in names])
```

```python
import jax
import jax.numpy as jnp
from jax import lax
from jax.experimental import pallas as pl
from jax.experimental.pallas import tpu as pltpu

F32 = jnp.float32
BF16 = jnp.bfloat16
MESH = pl.DeviceIdType.MESH

RMS_EPS = 1e-6
LN_EPS = 1e-5
D_MODEL = 1024
D_A = 512
D_B = 512
D_IN_ALL = 3 * D_A + 2 * D_B
CONV_A_W = 3
CONV_B_W = 31
HALO = 32
XA_HEADS = 4
HEAD_DIM = 256
D_FF = 2816
N_CHIPS = 4
N_DEV = 8
FF_BLK = D_FF // N_CHIPS
IN_BLK = D_IN_ALL // N_CHIPS
ROW_BLK = D_MODEL // N_CHIPS

ADAM_LR = 0.001
ADAM_B1 = 0.9
ADAM_B2 = 0.999
ADAM_EPS = 1e-08
ADAM_WD = 0.01
ADAM_STEP = 10

TILE_FWD = 1024
TILE_XATTN_FWD = 1024
TILE_FFN = 512
TILE_FFN_BWD = 512
TILE_XATTN_BWD = 512
TILE_BWD = 512
PHASE_ROWS = 256
FFN_BWD_BLOCKS = 2
CONV_ROWS_FWD = 64
CONV_ROWS = 32
V7X_VMEM_LIMIT = 56 * 1024 * 1024
V7X_VMEM_LIMIT_HIGH = 63 * 1024 * 1024

P1_Q, P1_K, P1_V, P1_O, P1_DOWN = 0, 256, 512, 768, 1024
P1_ROWS = P1_DOWN + FF_BLK
GX_Q, GX_K, GX_V, GX_O = 0, 256, 512, 768
SMALL_W_ROWS = 48

ANY = pl.BlockSpec(memory_space=pl.ANY)


def _mm(a, b):
    return lax.dot_general(a, b, (((1,), (0,)), ((), ())), preferred_element_type=F32)


def _mm_nt(a, b):
    return lax.dot_general(a, b, (((1,), (1,)), ((), ())), preferred_element_type=F32)


def _mm_tn(a, b):
    return lax.dot_general(a, b, (((0,), (0,)), ((), ())), preferred_element_type=F32)


def _sigmoid(x):
    return 0.5 * jnp.tanh(0.5 * x) + 0.5


def _rms(x):
    r = lax.rsqrt(jnp.mean(x * x, axis=-1, keepdims=True) + RMS_EPS)
    return x * r, r


def _rms_bwd(dy, xn, r, g):
    gdy = dy * g
    dx = r * (gdy - xn * jnp.mean(gdy * xn, axis=-1, keepdims=True))
    return dx, jnp.sum(dy * xn, axis=0, keepdims=True)


def _fold8(a):
    out = a[0:8, :]
    for m in range(1, a.shape[0] // 8):
        out = out + a[8 * m:8 * m + 8, :]
    return out


def _full(shape):
    return pl.BlockSpec(shape, lambda *_: (0,) * len(shape))


def _params(sem=None, vmem_limit=V7X_VMEM_LIMIT):
    return pltpu.CompilerParams(dimension_semantics=sem, vmem_limit_bytes=vmem_limit)


def _copy_all(pairs):
    def scoped(sems):
        copies = [pltpu.make_async_copy(src, dst, sems.at[k]) for k, (src, dst) in enumerate(pairs)]
        for cp in copies:
            cp.start()
        for cp in copies:
            cp.wait()

    pl.run_scoped(scoped, pltpu.SemaphoreType.DMA((len(pairs),)))


def _rows(dst, src_hbm, row0, rows):
    return [(src_hbm.at[k, pl.ds(row0, rows), :], dst.at[pl.ds(rows * k, rows), :]) for k in range(N_CHIPS)]


def _cols(dst, src_hbm, cols):
    return [(src_hbm.at[k], dst.at[:, pl.ds(cols * k, cols)]) for k in range(N_CHIPS)]


def _fill_phases(src, sh, nrows, row0=0):
    for r in range(1, 8):
        sh[r - 1, pl.ds(0, nrows), :] = src[pl.ds(row0 + r, nrows), pl.ds(D_A, D_B)]


def _phase_rows(src, sh, off, start, size, row0=0):
    r = off % 8
    if r == 0:
        return src[pl.ds(off + start, size), pl.ds(D_A, D_B)]
    return sh[r - 1, pl.ds(off - r + start - row0, size), :]


def _my_place():
    x, y, c = lax.axis_index("x"), lax.axis_index("y"), lax.axis_index("c")
    return x, y, c, ((1 - x, y), (x, 1 - y), (1 - x, 1 - y))


def _remote(src, dst, send_sem, recv_sem, to):
    return pltpu.make_async_remote_copy(src_ref=src, dst_ref=dst, send_sem=send_sem, recv_sem=recv_sem,
                                        device_id=to, device_id_type=MESH)


def _gather_sems(np_):
    return [pltpu.SemaphoreType.DMA((np_, 3))] * 4 + [pltpu.SemaphoreType.DMA((np_,))] * 2


def _gather_start(srcs, dsts, sems):
    send, recv, _, _, osend, orecv = sems
    x, y, c, chips = _my_place()
    j = 2 * x + y
    for p in range(len(srcs)):
        _remote(srcs[p], dsts[p].at[j], osend.at[p], orecv.at[p], (x, y, 1 - c)).start()
        for nn, (kx, ky) in enumerate(chips):
            _remote(srcs[p].at[c], dsts[p].at[j, c], send.at[p, nn], recv.at[p, nn], (kx, ky, c)).start()


def _gather_forward(srcs, dsts, sems):
    send, recv, fsend, frecv, _, _ = sems
    x, y, c, chips = _my_place()
    for nn, (kx, ky) in enumerate(chips):
        for p in range(len(srcs)):
            blk = dsts[p].at[2 * kx + ky, c]
            _remote(blk, blk, send.at[p, nn], recv.at[p, nn], (kx, ky, c)).wait_recv()
            _remote(blk, blk, fsend.at[p, nn], frecv.at[p, nn], (x, y, 1 - c)).start()


def _gather_finish(srcs, dsts, sems):
    send, recv, fsend, frecv, osend, orecv = sems
    x, y, c, chips = _my_place()
    j = 2 * x + y
    for nn, (kx, ky) in enumerate(chips):
        for p in range(len(srcs)):
            other = dsts[p].at[2 * kx + ky, 1 - c]
            _remote(other, other, fsend.at[p, nn], frecv.at[p, nn], (x, y, 1 - c)).wait_recv()
    for nn, (kx, ky) in enumerate(chips):
        for p in range(len(srcs)):
            _remote(srcs[p].at[c], dsts[p].at[j, c], send.at[p, nn], recv.at[p, nn], (kx, ky, c)).wait_send()
            blk = dsts[p].at[2 * kx + ky, c]
            _remote(blk, blk, fsend.at[p, nn], frecv.at[p, nn], (x, y, 1 - c)).wait_send()
    for p in range(len(srcs)):
        _remote(srcs[p], dsts[p].at[j], osend.at[p], orecv.at[p], (x, y, 1 - c)).wait()


def _split_halves(a):
    return a.reshape(2, a.shape[0] // 2, a.shape[1])


def _gather_weights(packs):
    np_ = len(packs)
    split = [_split_halves(a) for a in packs]

    def body(*refs):
        srcs, dsts, sems = refs[:np_], refs[np_:2 * np_], refs[2 * np_:]
        _gather_start(srcs, dsts, sems)
        _gather_forward(srcs, dsts, sems)
        _gather_finish(srcs, dsts, sems)

    outs = pl.pallas_call(
        body,
        name="gather_weights",
        in_specs=[ANY] * np_,
        out_specs=[ANY] * np_,
        out_shape=[jax.ShapeDtypeStruct((N_CHIPS,) + a.shape, a.dtype) for a in split],
        scratch_shapes=_gather_sems(np_),
    )(*split)
    return [o.reshape((N_CHIPS,) + a.shape) for o, a in zip(outs, packs)]


def _peers(x, y, c):
    out = []
    for d in range(1, N_DEV):
        px = 1 - x if d & 4 else x
        py = 1 - y if d & 2 else y
        pc = 1 - c if d & 1 else c
        out.append(((px, py, pc), 4 * px + 2 * py + pc))
    return out


def _scatter_copies(srcs, dsts, send, recv):
    x, y, c, _ = _my_place()
    me = 4 * x + 2 * y + c
    out = []
    for a in range(len(srcs)):
        for d, ((px, py, pc), pidx) in enumerate(_peers(x, y, c)):
            piece = srcs[a].at[2 * px + py, pc]
            out.append((_remote(piece, dsts[a].at[me], send.at[a, d], recv.at[a, d], (px, py, pc)),
                        _remote(piece, dsts[a].at[pidx], send.at[a, d], recv.at[a, d], (px, py, pc))))
    return out


def _scatter_start(srcs, dsts, send, recv):
    for out_cp, _ in _scatter_copies(srcs, dsts, send, recv):
        out_cp.start()


def _scatter_wait(srcs, dsts, send, recv):
    for out_cp, in_cp in _scatter_copies(srcs, dsts, send, recv):
        in_cp.wait_recv()
        out_cp.wait_send()


def _scatter_sems(na):
    return [pltpu.SemaphoreType.DMA((na, N_DEV - 1))] * 2


def _slots(a):
    return jax.ShapeDtypeStruct((N_DEV,) + a.shape[2:], a.dtype)


def _by_halves(a):
    return a.reshape(a.shape[0], 2, a.shape[1] // 2, a.shape[2])


def _mix_fwd(x, gpre, gpost, wa, wb, bb, lng, lnb, g1g, pr0g, late):
    S, D = x.shape
    T = min(TILE_FWD, S)
    n = S // T
    nl = len(late)
    phase_rows = min(T, PHASE_ROWS)
    late_split = [_split_halves(a) for a in late]

    def body(*refs):
        (x_ref, gpre_ref, gpost_ref, wa_ref, wb_ref, bb_ref, lng_ref, lnb_ref, g1g_hbm, pr0g_hbm) = refs[:10]
        srcs = refs[10:10 + nl]
        x1_ref, u_ref, o1_ref, z1_ref = refs[10 + nl:14 + nl]
        dsts = refs[14 + nl:14 + 2 * nl]
        win_v, wout_v, ext, sh, z1buf = refs[14 + 2 * nl:19 + 2 * nl]
        sems = refs[19 + 2 * nl:]
        i = pl.program_id(0)

        @pl.when(i == 0)
        def _():
            _gather_start(srcs, dsts, sems)
            _copy_all(_cols(win_v, g1g_hbm, IN_BLK) + _rows(wout_v, pr0g_hbm, 0, ROW_BLK))
            ext[pl.ds(0, HALO), :] = jnp.zeros((HALO, D_A + D_B), F32)

        @pl.when(i == max(n - 2, 0))
        def _():
            _gather_forward(srcs, dsts, sems)

        xv = x_ref[...]
        xn, _ = _rms(xv)
        h = (xn * gpre_ref[...]).astype(BF16)
        u = _mm(h, win_v[...])
        u_ref[...] = u.astype(BF16)
        b_a = u[:, 0:D_A]
        cv = u[:, D_A:2 * D_A] * u[:, 2 * D_A:3 * D_A]
        z0 = u[:, 3 * D_A:3 * D_A + D_B] * _sigmoid(u[:, 3 * D_A + D_B:])
        ext[pl.ds(HALO, T), pl.ds(0, D_A)] = cv
        ext[pl.ds(HALO, T), pl.ds(D_A, D_B)] = z0

        conv_a = ext[pl.ds(HALO - 2, T), pl.ds(0, D_A)] * wa_ref[0:1, :]
        for k in range(1, CONV_A_W):
            conv_a = conv_a + ext[pl.ds(HALO - 2 + k, T), pl.ds(0, D_A)] * wa_ref[k:k + 1, :]
        y_a = b_a * conv_a

        base = HALO - (CONV_B_W - 1)
        for row0 in range(0, T, phase_rows):
            _fill_phases(ext, sh, phase_rows + HALO - 8, row0)

            def chunk(ci, carry, row0=row0):
                start = pl.multiple_of(row0 + ci * CONV_ROWS_FWD, 8)
                acc = jnp.broadcast_to(bb_ref[...], (CONV_ROWS_FWD, D_B))
                for k in range(CONV_B_W):
                    acc = acc + _phase_rows(ext, sh, base + k, start, CONV_ROWS_FWD, row0) * wb_ref[k:k + 1, :]
                z1buf[pl.ds(start, CONV_ROWS_FWD), :] = acc
                return carry

            lax.fori_loop(0, phase_rows // CONV_ROWS_FWD, chunk, 0)
        z1 = z1buf[...]
        z1_ref[...] = z1.astype(BF16)
        mu = jnp.mean(z1, axis=-1, keepdims=True)
        zc = z1 - mu
        rstd = lax.rsqrt(jnp.mean(zc * zc, axis=-1, keepdims=True) + LN_EPS)
        l = zc * rstd * lng_ref[...] + lnb_ref[...]
        y_b = l * _sigmoid(l)
        y = jnp.concatenate([y_a, y_b], axis=-1).astype(BF16)
        o1 = _mm(y, wout_v[...])
        o1_ref[...] = o1.astype(BF16)
        o1n, _ = _rms(o1)
        x1_ref[...] = xv + o1n * gpost_ref[...]
        ext[pl.ds(0, HALO), :] = ext[pl.ds(T, HALO), :]

        @pl.when(i == n - 1)
        def _():
            _gather_finish(srcs, dsts, sems)

    tok = lambda w: pl.BlockSpec((T, w), lambda i: (i, 0))
    outs = pl.pallas_call(
        body,
        name="mix_fwd",
        grid=(n,),
        in_specs=[tok(D), _full((1, D)), _full((1, D)), _full((CONV_A_W, D_A)), _full((CONV_B_W, D_B)),
                  _full((1, D_B)), _full((1, D_B)), _full((1, D_B)), ANY, ANY] + [ANY] * nl,
        out_specs=[tok(D), tok(D_IN_ALL), tok(D), tok(D_B)] + [ANY] * nl,
        out_shape=[jax.ShapeDtypeStruct((S, D), F32), jax.ShapeDtypeStruct((S, D_IN_ALL), BF16),
                   jax.ShapeDtypeStruct((S, D), BF16), jax.ShapeDtypeStruct((S, D_B), BF16)]
        + [jax.ShapeDtypeStruct((N_CHIPS,) + a.shape, a.dtype) for a in late_split],
        scratch_shapes=[pltpu.VMEM((D, D_IN_ALL), BF16), pltpu.VMEM((D_A + D_B, D), BF16),
                        pltpu.VMEM((HALO + T, D_A + D_B), F32), pltpu.VMEM((7, HALO + phase_rows, D_B), F32),
                        pltpu.VMEM((T, D_B), F32)] + _gather_sems(nl),
        compiler_params=_params(("arbitrary",), V7X_VMEM_LIMIT_HIGH),
    )(x, gpre, gpost, wa, wb, bb, lng, lnb, g1g, pr0g, *late_split)
    return list(outs[:4]) + [o.reshape((N_CHIPS,) + a.shape) for o, a in zip(outs[4:], late)]


def _mix_bwd(dx1, x, o1, u, z1s, gpre, gpost, wa, wb, lng, lnb, g1g, pr0g, after):
    S, D = x.shape
    T = min(TILE_BWD, S)
    n = S // T
    hb = T // HALO
    phase_rows = min(T, PHASE_ROWS)

    def body(dx1_ref, x_ref, o1_ref, u_ref, uh_ref, z1_ref, gpre_ref, gpost_ref, wa_ref, wb_ref,
             lng_ref, lnb_ref, g1g_hbm, pr0g_hbm, _,
             dx_ref, dgpre_ref, dgpost_ref, dwa_ref, dwb_ref, dbb_ref, dlng_ref, dlnb_ref, cm1_hbm, cm2_hbm,
             win_v, wout_v, dwin_acc, dwout_acc, ext, ext2, shb, dwb_acc):
        i = pl.program_id(0)

        @pl.when(i == 0)
        def _():
            _copy_all(_cols(win_v, g1g_hbm, IN_BLK) + _rows(wout_v, pr0g_hbm, 0, ROW_BLK))
            dwin_acc[...] = jnp.zeros_like(dwin_acc)
            dwout_acc[...] = jnp.zeros_like(dwout_acc)
            dwb_acc[...] = jnp.zeros_like(dwb_acc)
            ext2[pl.ds(T, HALO), :] = jnp.zeros((HALO, D_A + D_B), F32)
            for ref in (dgpre_ref, dgpost_ref, dwa_ref, dbb_ref, dlng_ref, dlnb_ref):
                ref[...] = jnp.zeros_like(ref)

        o1n, r1 = _rms(o1_ref[...].astype(F32))
        dx1v = dx1_ref[...]
        d_o1, dgp = _rms_bwd(dx1v, o1n, r1, gpost_ref[...])
        dgpost_ref[...] += dgp
        d_o1b = d_o1.astype(BF16)
        dy = _mm_nt(d_o1b, wout_v[...])

        first = (i == n - 1).astype(F32)
        uh = uh_ref[...].astype(F32) * (1.0 - first)
        ext[pl.ds(0, HALO), pl.ds(0, D_A)] = uh[:, D_A:2 * D_A] * uh[:, 2 * D_A:3 * D_A]
        uf = u_ref[...].astype(F32)
        b_a = uf[:, 0:D_A]
        c_a = uf[:, D_A:2 * D_A]
        v_a = uf[:, 2 * D_A:3 * D_A]
        gv = uf[:, 3 * D_A:3 * D_A + D_B]
        sg = _sigmoid(uf[:, 3 * D_A + D_B:])
        ext[pl.ds(HALO, T), pl.ds(0, D_A)] = c_a * v_a
        ext[pl.ds(HALO, T), pl.ds(D_A, D_B)] = gv * sg
        conv_a = ext[pl.ds(HALO - 2, T), pl.ds(0, D_A)] * wa_ref[0:1, :]
        for k in range(1, CONV_A_W):
            conv_a = conv_a + ext[pl.ds(HALO - 2 + k, T), pl.ds(0, D_A)] * wa_ref[k:k + 1, :]
        z1 = z1_ref[...].astype(F32)
        mu = jnp.mean(z1, axis=-1, keepdims=True)
        zc = z1 - mu
        rstd = lax.rsqrt(jnp.mean(zc * zc, axis=-1, keepdims=True) + LN_EPS)
        zn = zc * rstd
        l = zn * lng_ref[...] + lnb_ref[...]
        sl = _sigmoid(l)
        y = jnp.concatenate([b_a * conv_a, l * sl], axis=-1).astype(BF16)
        dwout_acc[...] += _mm_tn(y, d_o1b)

        dy_a = dy[:, 0:D_A]
        dl = dy[:, D_A:] * (sl * (1.0 + l * (1.0 - sl)))
        dlng_ref[...] += jnp.sum(dl * zn, axis=0, keepdims=True)
        dlnb_ref[...] += jnp.sum(dl, axis=0, keepdims=True)
        dzn = dl * lng_ref[...]
        dz1 = rstd * (dzn - jnp.mean(dzn, axis=-1, keepdims=True) - zn * jnp.mean(dzn * zn, axis=-1, keepdims=True))
        dbb_ref[...] += jnp.sum(dz1, axis=0, keepdims=True)
        d_conv = dy_a * b_a
        ext2[pl.ds(0, T), pl.ds(0, D_A)] = d_conv
        ext2[pl.ds(0, T), pl.ds(D_A, D_B)] = dz1

        d_cv = ext2[pl.ds(CONV_A_W - 1, T), pl.ds(0, D_A)] * wa_ref[0:1, :]
        for k in range(1, CONV_A_W):
            d_cv = d_cv + ext2[pl.ds(CONV_A_W - 1 - k, T), pl.ds(0, D_A)] * wa_ref[k:k + 1, :]
        for k in range(CONV_A_W):
            dwa_ref[k:k + 1, :] += jnp.sum(d_conv * ext[pl.ds(HALO - 2 + k, T), pl.ds(0, D_A)], axis=0, keepdims=True)

        for row0 in range(0, T, phase_rows):
            _fill_phases(ext2, shb, phase_rows + HALO - 8, row0)

            def chunk(ci, carry, row0=row0):
                start = pl.multiple_of(row0 + ci * CONV_ROWS, 8)
                z0c = ext[pl.ds(HALO + start, CONV_ROWS), pl.ds(D_A, D_B)]
                acc = jnp.zeros((CONV_ROWS, D_B), F32)
                for k in range(CONV_B_W):
                    ahead = _phase_rows(ext2, shb, CONV_B_W - 1 - k, start, CONV_ROWS, row0)
                    acc = acc + ahead * wb_ref[k:k + 1, :]
                    dwb_acc[k] += _fold8(z0c * ahead)
                ext[pl.ds(HALO + start, CONV_ROWS), pl.ds(D_A, D_B)] = acc
                return carry

            lax.fori_loop(0, phase_rows // CONV_ROWS, chunk, 0)
        dz0 = ext[pl.ds(HALO, T), pl.ds(D_A, D_B)]
        du = jnp.concatenate([dy_a * conv_a, d_cv * v_a, d_cv * c_a, dz0 * sg, dz0 * gv * sg * (1.0 - sg)],
                             axis=-1).astype(BF16)
        dh = _mm_nt(du, win_v[...])
        xn, r0 = _rms(x_ref[...])
        dwin_acc[...] += _mm_tn((xn * gpre_ref[...]).astype(BF16), du)
        dxp, dg0 = _rms_bwd(dh, xn, r0, gpre_ref[...])
        dgpre_ref[...] += dg0
        dx_ref[...] = dx1v + dxp
        ext2[pl.ds(T, HALO), :] = ext2[pl.ds(0, HALO), :]

        @pl.when(i == n - 1)
        def _():
            for k in range(CONV_B_W):
                dwb_ref[k:k + 1, :] = jnp.sum(dwb_acc[k], axis=0, keepdims=True)
            win_v[...] = dwin_acc[...].astype(BF16)
            wout_v[...] = dwout_acc[...].astype(BF16)
            _copy_all([(win_v.at[:, pl.ds(IN_BLK * k, IN_BLK)], cm1_hbm.at[k]) for k in range(N_CHIPS)]
                      + [(wout_v.at[pl.ds(ROW_BLK * k, ROW_BLK), :], cm2_hbm.at[k]) for k in range(N_CHIPS)])

    rev = lambda w: pl.BlockSpec((T, w), lambda i: (n - 1 - i, 0))
    halo = pl.BlockSpec((HALO, D_IN_ALL), lambda i: (jnp.maximum((n - 1 - i) * hb - 1, 0), 0))
    return pl.pallas_call(
        body,
        name="mix_bwd",
        grid=(n,),
        in_specs=[rev(D), rev(D), rev(D), rev(D_IN_ALL), halo, rev(D_B), _full((1, D)),
                  _full((1, D)), _full((CONV_A_W, D_A)), _full((CONV_B_W, D_B)), _full((1, D_B)), _full((1, D_B)),
                  ANY, ANY, ANY],
        out_specs=[rev(D), _full((1, D)), _full((1, D)), _full((CONV_A_W, D_A)), _full((CONV_B_W, D_B)),
                   _full((1, D_B)), _full((1, D_B)), _full((1, D_B)), ANY, ANY],
        out_shape=[jax.ShapeDtypeStruct((S, D), F32), jax.ShapeDtypeStruct((1, D), F32),
                   jax.ShapeDtypeStruct((1, D), F32), jax.ShapeDtypeStruct((CONV_A_W, D_A), F32),
                   jax.ShapeDtypeStruct((CONV_B_W, D_B), F32), jax.ShapeDtypeStruct((1, D_B), F32),
                   jax.ShapeDtypeStruct((1, D_B), F32), jax.ShapeDtypeStruct((1, D_B), F32),
                   jax.ShapeDtypeStruct((N_CHIPS, D, IN_BLK), BF16),
                   jax.ShapeDtypeStruct((N_CHIPS, ROW_BLK, D), BF16)],
        scratch_shapes=[pltpu.VMEM((D, D_IN_ALL), BF16), pltpu.VMEM((D_A + D_B, D), BF16),
                        pltpu.VMEM((D, D_IN_ALL), F32), pltpu.VMEM((D_A + D_B, D), F32),
                        pltpu.VMEM((HALO + T, D_A + D_B), F32), pltpu.VMEM((HALO + T, D_A + D_B), F32),
                        pltpu.VMEM((7, HALO + phase_rows, D_B), F32), pltpu.VMEM((CONV_B_W, 8, D_B), F32)],
        compiler_params=_params(("arbitrary",), V7X_VMEM_LIMIT_HIGH),
    )(dx1, x, o1, u, u, z1s, gpre, gpost, wa, wb, lng, lnb, g1g, pr0g, after)


def _mem_kv(mem, gmem, pr1g):
    M, D = mem.shape

    def body(mem_ref, g_ref, pr1g_hbm, memn_ref, k_ref, v_ref, wk_v, wv_v):
        _copy_all(_rows(wk_v, pr1g_hbm, P1_K, ROW_BLK) + _rows(wv_v, pr1g_hbm, P1_V, ROW_BLK))
        mn, _ = _rms(mem_ref[...])
        mb = (mn * g_ref[...]).astype(BF16)
        memn_ref[...] = mb
        k_ref[...] = _mm(mb, wk_v[...]).astype(BF16)
        v_ref[...] = _mm(mb, wv_v[...]).astype(BF16)

    return pl.pallas_call(
        body,
        name="mem_kv",
        grid=(1,),
        in_specs=[_full((M, D)), _full((1, D)), ANY],
        out_specs=[_full((M, D))] * 3,
        out_shape=[jax.ShapeDtypeStruct((M, D), BF16)] * 3,
        scratch_shapes=[pltpu.VMEM((D, D), BF16), pltpu.VMEM((D, D), BF16)],
        compiler_params=_params(("arbitrary",)),
    )(mem, gmem, pr1g)


def _attend(qb, kb, vb):
    scale = HEAD_DIM ** -0.5
    ps, os_ = [], []
    for hd in range(XA_HEADS):
        cols = slice(HEAD_DIM * hd, HEAD_DIM * (hd + 1))
        s = _mm_nt(qb[:, cols], kb[:, cols]) * scale
        e = jnp.exp(s - jnp.max(s, axis=-1, keepdims=True))
        p = e * (1.0 / jnp.sum(e, axis=-1, keepdims=True))
        ps.append(p)
        os_.append(_mm(p.astype(BF16), vb[:, cols]))
    return ps, jnp.concatenate(os_, axis=-1).astype(BF16)


def _xattn_fwd(x1, gpre, gpost, kb, vb, pr1g):
    S, D = x1.shape
    M = kb.shape[0]
    T = min(TILE_XATTN_FWD, S)
    n = S // T

    def body(x1_ref, gpre_ref, gpost_ref, k_ref, v_ref, pr1g_hbm, x2_ref, q_ref, o2_ref, rx_ref, ro_ref, wq_v, wo_v):
        @pl.when(pl.program_id(0) == 0)
        def _():
            _copy_all(_rows(wq_v, pr1g_hbm, P1_Q, ROW_BLK) + _rows(wo_v, pr1g_hbm, P1_O, ROW_BLK))

        xv = x1_ref[...]
        xn, rx = _rms(xv)
        rx_ref[...] = rx
        qb = _mm((xn * gpre_ref[...]).astype(BF16), wq_v[...]).astype(BF16)
        q_ref[...] = qb
        _, ob = _attend(qb, k_ref[...], v_ref[...])
        o2 = _mm(ob, wo_v[...])
        o2_ref[...] = o2.astype(BF16)
        o2n, ro = _rms(o2)
        ro_ref[...] = ro
        x2_ref[...] = xv + o2n * gpost_ref[...]

    tok = lambda w: pl.BlockSpec((T, w), lambda i: (i, 0))
    return pl.pallas_call(
        body,
        name="xattn_fwd",
        grid=(n,),
        in_specs=[tok(D), _full((1, D)), _full((1, D)), _full((M, D)), _full((M, D)), ANY],
        out_specs=[tok(D), tok(D), tok(D), tok(1), tok(1)],
        out_shape=[jax.ShapeDtypeStruct((S, D), F32), jax.ShapeDtypeStruct((S, D), BF16),
                   jax.ShapeDtypeStruct((S, D), BF16), jax.ShapeDtypeStruct((S, 1), F32),
                   jax.ShapeDtypeStruct((S, 1), F32)],
        scratch_shapes=[pltpu.VMEM((D, D), BF16), pltpu.VMEM((D, D), BF16)],
        compiler_params=_params(("arbitrary",)),
    )(x1, gpre, gpost, kb, vb, pr1g)


def _xattn_bwd(dx3, dh3p, x2, x1, o2, q, rx2, ro2, rx1, kb, vb, gffn, gpost, gpre, pr1g, cf1, cf2):
    S, D = x1.shape
    M = kb.shape[0]
    T = min(TILE_XATTN_BWD, S)
    n = S // T
    scale = HEAD_DIM ** -0.5
    nparts = dh3p.shape[0]
    cfs = [_by_halves(cf1), _by_halves(cf2)]

    def body(*refs):
        dx3_ref, dh3_refs = refs[0], refs[1:1 + nparts]
        (x2_ref, x1_ref, o2_ref, q_ref, rx2_ref, ro2_ref, rx1_ref, k_ref, v_ref, gffn_ref, gpost_ref, gpre_ref,
         pr1g_hbm, cf1_hbm, cf2_hbm,
         dx1_ref, dgffn_ref, dgpost_ref, dgpre_ref, dk_ref, dv_ref, cx_hbm, yf1_hbm, yf2_hbm,
         wq_v, wo_v, dwq_acc, dwo_acc, send, recv) = refs[1 + nparts:]
        i = pl.program_id(0)

        @pl.when(i == 0)
        def _():
            _scatter_start([cf1_hbm, cf2_hbm], [yf1_hbm, yf2_hbm], send, recv)
            _copy_all(_rows(wq_v, pr1g_hbm, P1_Q, ROW_BLK) + _rows(wo_v, pr1g_hbm, P1_O, ROW_BLK))
            dwq_acc[...] = jnp.zeros_like(dwq_acc)
            dwo_acc[...] = jnp.zeros_like(dwo_acc)
            for ref in (dgffn_ref, dgpost_ref, dgpre_ref, dk_ref, dv_ref):
                ref[...] = jnp.zeros_like(ref)

        r2 = rx2_ref[...]
        x2n = x2_ref[...] * r2
        dh3 = dh3_refs[0][...].astype(F32)
        for ref in dh3_refs[1:]:
            dh3 = dh3 + ref[...].astype(F32)
        dxp, dg = _rms_bwd(dh3, x2n, r2, gffn_ref[...])
        dgffn_ref[...] += dg
        dx2 = dx3_ref[...] + dxp
        ro = ro2_ref[...]
        o2n = o2_ref[...].astype(F32) * ro
        d_o2, dg = _rms_bwd(dx2, o2n, ro, gpost_ref[...])
        dgpost_ref[...] += dg
        d_o2b = d_o2.astype(BF16)
        d_o = _mm_nt(d_o2b, wo_v[...]).astype(BF16)
        qb = q_ref[...]
        kv = k_ref[...]
        vv = v_ref[...]
        ps, ob = _attend(qb, kv, vv)
        dwo_acc[...] += _mm_tn(ob, d_o2b)
        dqs = []
        for hd in range(XA_HEADS):
            cols = slice(HEAD_DIM * hd, HEAD_DIM * (hd + 1))
            p = ps[hd]
            dp = _mm_nt(d_o[:, cols], vv[:, cols])
            dv_ref[:, cols] += _mm_tn(p.astype(BF16), d_o[:, cols])
            ds = (p * (dp - jnp.sum(p * dp, axis=-1, keepdims=True)) * scale).astype(BF16)
            dqs.append(_mm(ds, kv[:, cols]))
            dk_ref[:, cols] += _mm_tn(ds, qb[:, cols])
        dq = jnp.concatenate(dqs, axis=-1).astype(BF16)
        dh2 = _mm_nt(dq, wq_v[...])
        r1 = rx1_ref[...]
        x1n = x1_ref[...] * r1
        dwq_acc[...] += _mm_tn((x1n * gpre_ref[...]).astype(BF16), dq)
        dxp, dg = _rms_bwd(dh2, x1n, r1, gpre_ref[...])
        dgpre_ref[...] += dg
        dx1_ref[...] = dx2 + dxp

        @pl.when(i == n - 1)
        def _():
            wq_v[...] = dwq_acc[...].astype(BF16)
            wo_v[...] = dwo_acc[...].astype(BF16)
            _copy_all([(w.at[pl.ds(ROW_BLK * k, ROW_BLK), :], cx_hbm.at[k, pl.ds(row0, ROW_BLK), :])
                       for w, row0 in ((wq_v, GX_Q), (wo_v, GX_O)) for k in range(N_CHIPS)])
            _scatter_wait([cf1_hbm, cf2_hbm], [yf1_hbm, yf2_hbm], send, recv)

    tok = lambda w: pl.BlockSpec((T, w), lambda i: (i, 0))
    part = lambda j: pl.BlockSpec((None, T, D), lambda i: (j, i, 0))
    return pl.pallas_call(
        body,
        name="xattn_bwd",
        grid=(n,),
        in_specs=[tok(D)] + [part(j) for j in range(nparts)] + [tok(D), tok(D), tok(D), tok(D), tok(1), tok(1), tok(1),
                                                                 _full((M, D)), _full((M, D)), _full((1, D)),
                                                                 _full((1, D)), _full((1, D)), ANY, ANY, ANY],
        out_specs=[tok(D), _full((1, D)), _full((1, D)), _full((1, D)), _full((M, D)), _full((M, D)), ANY, ANY, ANY],
        out_shape=[jax.ShapeDtypeStruct((S, D), F32), jax.ShapeDtypeStruct((1, D), F32),
                   jax.ShapeDtypeStruct((1, D), F32), jax.ShapeDtypeStruct((1, D), F32),
                   jax.ShapeDtypeStruct((M, D), F32), jax.ShapeDtypeStruct((M, D), F32),
                   jax.ShapeDtypeStruct((N_CHIPS, D, D), BF16), _slots(cfs[0]), _slots(cfs[1])],
        scratch_shapes=[pltpu.VMEM((D, D), BF16), pltpu.VMEM((D, D), BF16),
                        pltpu.VMEM((D, D), F32), pltpu.VMEM((D, D), F32)] + _scatter_sems(2),
        compiler_params=_params(("arbitrary",)),
    )(dx3, *([dh3p] * nparts), x2, x1, o2, q, rx2, ro2, rx1, kb, vb, gffn, gpost, gpre, pr1g, *cfs)


def _mem_bwd(dk, dv, mem, memn, gmem, pr1g, cx_in):
    M, D = mem.shape

    def body(dk_ref, dv_ref, mem_ref, memn_ref, g_ref, pr1g_hbm, cx_hbm, dg_ref, cx_out, wk_v, wv_v):
        del cx_hbm
        _copy_all(_rows(wk_v, pr1g_hbm, P1_K, ROW_BLK) + _rows(wv_v, pr1g_hbm, P1_V, ROW_BLK))
        dkb = dk_ref[...].astype(BF16)
        dvb = dv_ref[...].astype(BF16)
        mb = memn_ref[...]
        dmn = _mm_nt(dkb, wk_v[...]) + _mm_nt(dvb, wv_v[...])
        mn, _ = _rms(mem_ref[...])
        dg_ref[...] = jnp.sum(dmn * mn, axis=0, keepdims=True)
        wk_v[...] = _mm_tn(mb, dkb).astype(BF16)
        wv_v[...] = _mm_tn(mb, dvb).astype(BF16)
        _copy_all([(w.at[pl.ds(ROW_BLK * k, ROW_BLK), :], cx_out.at[k, pl.ds(row0, ROW_BLK), :])
                   for w, row0 in ((wk_v, GX_K), (wv_v, GX_V)) for k in range(N_CHIPS)])

    return pl.pallas_call(
        body,
        name="mem_bwd",
        grid=(1,),
        in_specs=[_full((M, D)), _full((M, D)), _full((M, D)), _full((M, D)), _full((1, D)), ANY, ANY],
        out_specs=[_full((1, D)), ANY],
        out_shape=[jax.ShapeDtypeStruct((1, D), F32), jax.ShapeDtypeStruct(cx_in.shape, BF16)],
        input_output_aliases={6: 1},
        scratch_shapes=[pltpu.VMEM((D, D), BF16), pltpu.VMEM((D, D), BF16)],
        compiler_params=_params(("arbitrary",)),
    )(dk, dv, mem, memn, gmem, pr1g, cx_in)


def _ffn_fwd(x2, target, gpre, gpost, g2g, pr1g):
    S, D = x2.shape
    T = min(TILE_FFN, S)
    n = S // T

    def body(x2_ref, t_ref, gpre_ref, gpost_ref, g2g_hbm, pr1g_hbm,
             h3_ref, g_hbm, u_hbm, do3_ref, dx3_ref, loss_ref, dgpost_ref, rx_ref, wg_v, wu_v, wd_v, gst, ust, sem):
        i = pl.program_id(0)

        @pl.when(i == 0)
        def _():
            _copy_all([(g2g_hbm.at[:, pl.ds(0, D), :], wg_v), (g2g_hbm.at[:, pl.ds(D, D), :], wu_v),
                       (pr1g_hbm.at[:, pl.ds(P1_DOWN, FF_BLK), :], wd_v)])
            loss_ref[...] = jnp.zeros_like(loss_ref)
            dgpost_ref[...] = jnp.zeros_like(dgpost_ref)

        xv = x2_ref[...]
        xn, rx = _rms(xv)
        rx_ref[...] = rx
        hb = (xn * gpre_ref[...]).astype(BF16)
        h3_ref[...] = hb
        o3 = jnp.zeros((T, D), F32)
        out = [None, None]
        for c in range(N_CHIPS):
            slot = c % 2
            if out[slot] is not None:
                for cp in out[slot]:
                    cp.wait()
            g = _mm(hb, wg_v[c])
            u = _mm(hb, wu_v[c])
            gst[slot] = g.astype(BF16)
            ust[slot] = u.astype(BF16)
            out[slot] = (pltpu.make_async_copy(gst.at[slot], g_hbm.at[c, i], sem.at[0, slot]),
                         pltpu.make_async_copy(ust.at[slot], u_hbm.at[c, i], sem.at[1, slot]))
            for cp in out[slot]:
                cp.start()
            o3 = o3 + _mm((g * _sigmoid(g) * u).astype(BF16), wd_v[c])
        for pair in out:
            for cp in pair:
                cp.wait()
        o3n, r3 = _rms(o3)
        diff = xv + o3n * gpost_ref[...] - t_ref[...]
        sq = jnp.sum(jnp.sum(diff * diff, axis=-1, keepdims=True), axis=0, keepdims=True)
        loss_ref[...] += sq * (0.5 / D)
        dx3 = diff * (1.0 / D)
        dx3_ref[...] = dx3
        d_o3, dg = _rms_bwd(dx3, o3n, r3, gpost_ref[...])
        dgpost_ref[...] += dg
        do3_ref[...] = d_o3.astype(BF16)

    tok = lambda w: pl.BlockSpec((T, w), lambda i: (i, 0))
    h3, gs, us, do3, dx3, loss, dgpost, rx2 = pl.pallas_call(
        body,
        name="ffn_fwd",
        grid=(n,),
        in_specs=[tok(D), tok(D), _full((1, D)), _full((1, D)), ANY, ANY],
        out_specs=[tok(D), ANY, ANY, tok(D), tok(D), _full((1, 128)), _full((1, D)), tok(1)],
        out_shape=[jax.ShapeDtypeStruct((S, D), BF16), jax.ShapeDtypeStruct((N_CHIPS, n, T, FF_BLK), BF16),
                   jax.ShapeDtypeStruct((N_CHIPS, n, T, FF_BLK), BF16), jax.ShapeDtypeStruct((S, D), BF16),
                   jax.ShapeDtypeStruct((S, D), F32), jax.ShapeDtypeStruct((1, 128), F32),
                   jax.ShapeDtypeStruct((1, D), F32), jax.ShapeDtypeStruct((S, 1), F32)],
        scratch_shapes=[pltpu.VMEM((N_CHIPS, D, FF_BLK), BF16), pltpu.VMEM((N_CHIPS, D, FF_BLK), BF16),
                        pltpu.VMEM((N_CHIPS, FF_BLK, D), BF16), pltpu.VMEM((2, T, FF_BLK), BF16),
                        pltpu.VMEM((2, T, FF_BLK), BF16), pltpu.SemaphoreType.DMA((2, 2))],
        compiler_params=_params(("arbitrary",)),
    )(x2, target, gpre, gpost, g2g, pr1g)
    return h3, gs.reshape(N_CHIPS, S, FF_BLK), us.reshape(N_CHIPS, S, FF_BLK), do3, dx3, loss, dgpost, rx2


def _ffn_bwd(h3, do3, gs, us, g2g, pr1g):
    S, D = h3.shape
    T = min(TILE_FFN_BWD, S)
    n = S // T
    NP = FFN_BWD_BLOCKS

    def body(h3_ref, do3_ref, g_ref, u_ref, g2g_hbm, pr1g_hbm, dh3_ref, cf1_hbm, cf2_hbm,
             wg_v, wu_v, wd_v, dwg_acc, dwu_acc, dwd_acc):
        jp = pl.program_id(0)
        i = pl.program_id(1)
        blocks = pl.ds(NP * jp, NP)

        @pl.when(i == 0)
        def _():
            _copy_all([(g2g_hbm.at[blocks, pl.ds(0, D), :], wg_v), (g2g_hbm.at[blocks, pl.ds(D, D), :], wu_v),
                       (pr1g_hbm.at[blocks, pl.ds(P1_DOWN, FF_BLK), :], wd_v)])
            dwg_acc[...] = jnp.zeros_like(dwg_acc)
            dwu_acc[...] = jnp.zeros_like(dwu_acc)
            dwd_acc[...] = jnp.zeros_like(dwd_acc)

        hb = h3_ref[...]
        d_o3 = do3_ref[...]
        dh = jnp.zeros((T, D), F32)
        for c in range(NP):
            da = _mm_nt(d_o3, wd_v[c])
            g = g_ref[c].astype(F32)
            u = u_ref[c].astype(F32)
            sg = _sigmoid(g)
            sl = g * sg
            dwd_acc[c] += _mm_tn((sl * u).astype(BF16), d_o3)
            dub = (da * sl).astype(BF16)
            dgb = (da * u * (sg * (1.0 + g * (1.0 - sg)))).astype(BF16)
            dwg_acc[c] += _mm_tn(dgb, hb)
            dwu_acc[c] += _mm_tn(dub, hb)
            dh = dh + _mm_nt(dgb, wg_v[c]) + _mm_nt(dub, wu_v[c])
        dh3_ref[...] = dh.astype(BF16)

        @pl.when(i == n - 1)
        def _():
            wd_v[...] = dwg_acc[...].astype(BF16)
            pltpu.sync_copy(wd_v, cf1_hbm.at[blocks, pl.ds(0, FF_BLK), :])
            wd_v[...] = dwu_acc[...].astype(BF16)
            pltpu.sync_copy(wd_v, cf1_hbm.at[blocks, pl.ds(FF_BLK, FF_BLK), :])
            wd_v[...] = dwd_acc[...].astype(BF16)
            pltpu.sync_copy(wd_v, cf2_hbm.at[blocks])

    tok = lambda w: pl.BlockSpec((T, w), lambda jp, i: (i, 0))
    blk = pl.BlockSpec((NP, T, FF_BLK), lambda jp, i: (jp, i, 0))
    return pl.pallas_call(
        body,
        name="ffn_bwd",
        grid=(N_CHIPS // NP, n),
        in_specs=[tok(D), tok(D), blk, blk, ANY, ANY],
        out_specs=[pl.BlockSpec((None, T, D), lambda jp, i: (jp, i, 0)), ANY, ANY],
        out_shape=[jax.ShapeDtypeStruct((N_CHIPS // NP, S, D), BF16),
                   jax.ShapeDtypeStruct((N_CHIPS, 2 * FF_BLK, D), BF16),
                   jax.ShapeDtypeStruct((N_CHIPS, FF_BLK, D), BF16)],
        scratch_shapes=[pltpu.VMEM((NP, D, FF_BLK), BF16), pltpu.VMEM((NP, D, FF_BLK), BF16),
                        pltpu.VMEM((NP, FF_BLK, D), BF16), pltpu.VMEM((NP, FF_BLK, D), F32),
                        pltpu.VMEM((NP, FF_BLK, D), F32), pltpu.VMEM((NP, FF_BLK, D), F32)],
        compiler_params=_params(("arbitrary", "arbitrary")),
    )(h3, do3, gs, us, g2g, pr1g)


IN_HBM = pl.BlockSpec(memory_space=pltpu.HBM)
IN_SEMAPHORES = pl.BlockSpec(memory_space=pltpu.SEMAPHORE)


def _split_copies(refs, na, nw, sems):
    srcs, lands = refs[:na + nw], refs[na + nw:2 * (na + nw)]
    send, recv = sems
    x, y, c, _ = _my_place()
    me = 4 * x + 2 * y + c
    pairs = []
    for d, ((px, py, pc), pidx) in enumerate(_peers(x, y, c)):
        for a in range(na + nw):
            k = d * (na + nw) + a
            src = srcs[a].at[2 * px + py, pc] if a < na else srcs[a]
            pairs.append((_remote(src, lands[a].at[me], send.at[k], recv.at[k], (px, py, pc)),
                          _remote(src, lands[a].at[pidx], send.at[k], recv.at[k], (px, py, pc))))
    return pairs


def _exchange_start(contribs, whole, tag):
    na, nw = len(contribs), len(whole)
    cs = [_by_halves(a) for a in contribs]
    flying = cs + list(whole) + [lax.empty(_slots(a).shape, a.dtype) for a in cs] + [
        lax.empty((N_DEV,) + a.shape, a.dtype) for a in whole]
    nf = len(flying)

    def body(*refs):
        for mine, _ in _split_copies(refs[:nf], na, nw, refs[nf:nf + 2]):
            mine.start()
        refs[-1][...] = jnp.zeros_like(refs[-1])

    ncopies = (N_DEV - 1) * (na + nw)
    outs = pl.pallas_call(
        body,
        name="exchange_start_" + tag,
        in_specs=[IN_HBM] * nf,
        out_specs=[IN_SEMAPHORES] * 2 + [IN_HBM] * nf + [pl.BlockSpec(memory_space=pltpu.VMEM)],
        out_shape=[pltpu.SemaphoreType.DMA((ncopies,)), pltpu.SemaphoreType.DMA((ncopies,))]
        + [pltpu.HBM(a.shape, a.dtype) for a in flying] + [jax.ShapeDtypeStruct((8, 128), F32)],
        input_output_aliases={k: 2 + k for k in range(nf)},
        compiler_params=pltpu.CompilerParams(has_side_effects=pltpu.SideEffectType.DATAFLOW_SIDE_EFFECTING),
    )(*[pltpu.with_memory_space_constraint(a, pltpu.HBM) for a in flying])
    return outs[:2], outs[2:2 + nf], outs[-1]


def _exchange_wait(sems, flying, na, nw, after, tag):
    nf = len(flying)

    def body(*refs):
        for mine, theirs in _split_copies(refs[:nf], na, nw, refs[nf:nf + 2]):
            theirs.wait_recv()
            mine.wait_send()

    outs = pl.pallas_call(
        body,
        name="exchange_wait_" + tag,
        in_specs=[IN_HBM] * nf + [IN_SEMAPHORES] * 2 + [ANY] * len(after),
        out_specs=[IN_HBM] * nf,
        out_shape=[pltpu.HBM(a.shape, a.dtype) for a in flying],
        input_output_aliases={k: k for k in range(nf)},
        compiler_params=pltpu.CompilerParams(has_side_effects=pltpu.SideEffectType.DATAFLOW_SIDE_EFFECTING),
    )(*flying, *sems, *after)
    return outs[na + nw:]


def _sum_peers(parts, own, place, steps, tag, after=()):
    _, rows, w = parts.shape
    tr = rows // steps
    if own.ndim == 3:
        own = _by_halves(own)

    def body(place_ref, *refs):
        p_refs, own_ref, o_ref = refs[:N_DEV], refs[N_DEV], refs[-1]
        me = place_ref[2]
        acc = None
        for s in range(N_DEV):
            term = jnp.where(me == s, own_ref[...], p_refs[s][...]).astype(F32)
            acc = term if acc is None else acc + term
        o_ref[...] = acc

    def other(s):
        return lambda i, pr: (jnp.where(pr[2] == s, (s + 1) % N_DEV, s), i, 0)

    own_spec = (pl.BlockSpec((None, None, tr, w), lambda i, pr: (pr[0], pr[1], i, 0)) if own.ndim == 4 else
                pl.BlockSpec((tr, w), lambda i, pr: (i, 0)))
    out_spec = (pl.BlockSpec((None, tr, w), lambda i, pr: (pr[1], i, 0)) if own.ndim == 4 else
                pl.BlockSpec((tr, w), lambda i, pr: (i, 0)))
    return pl.pallas_call(
        body,
        name="sum_peers_" + tag,
        grid_spec=pltpu.PrefetchScalarGridSpec(
            num_scalar_prefetch=1,
            grid=(steps,),
            in_specs=[pl.BlockSpec((None, tr, w), other(s)) for s in range(N_DEV)] + [own_spec] + [ANY] * len(after),
            out_specs=out_spec,
        ),
        out_shape=jax.ShapeDtypeStruct((2, rows, w) if own.ndim == 4 else (rows, w), F32),
        compiler_params=_params(("arbitrary",)),
    )(place, *([parts] * N_DEV), own, *after)


def _pair_gather(bufs, tag):
    np_ = len(bufs)

    def body(*refs):
        srcs, dsts = refs[:np_], refs[np_:2 * np_]
        send, recv = refs[2 * np_:]
        x, y, c, _ = _my_place()
        cps = []
        for p in range(np_):
            cp = _remote(srcs[p].at[c], dsts[p].at[c], send.at[p], recv.at[p], (x, y, 1 - c))
            cp.start()
            cps.append(cp)
        for p, cp in enumerate(cps):
            other = dsts[p].at[1 - c]
            _remote(other, other, send.at[p], recv.at[p], (x, y, 1 - c)).wait_recv()
            cp.wait_send()

    outs = pl.pallas_call(
        body,
        name="pair_gather_" + tag,
        in_specs=[ANY] * np_,
        out_specs=[ANY] * np_,
        out_shape=[jax.ShapeDtypeStruct(a.shape, F32) for a in bufs],
        input_output_aliases={p: p for p in range(np_)},
        scratch_shapes=[pltpu.SemaphoreType.DMA((np_,))] * 2,
    )(*bufs)
    return [o.reshape(2 * a.shape[1], a.shape[2]) for o, a in zip(outs, bufs)]


def _adamw(gsrc, row0, w, m, v, tr, tag):
    rows, width = w.shape
    off = row0 // tr
    bc1 = 1.0 - ADAM_B1 ** ADAM_STEP
    bc2 = 1.0 - ADAM_B2 ** ADAM_STEP

    def body(g_ref, w_ref, m_ref, v_ref, go_ref, d_ref, mo_ref, vo_ref):
        g = g_ref[...]
        m2 = ADAM_B1 * m_ref[...] + (1.0 - ADAM_B1) * g
        v2 = ADAM_B2 * v_ref[...] + (1.0 - ADAM_B2) * (g * g)
        go_ref[...] = g
        mo_ref[...] = m2
        vo_ref[...] = v2
        d_ref[...] = -ADAM_LR * ((m2 / bc1) / (jnp.sqrt(v2 / bc2) + ADAM_EPS) + ADAM_WD * w_ref[...])

    here = pl.BlockSpec((tr, width), lambda i: (i, 0))
    return pl.pallas_call(
        body,
        name="adamw_" + tag,
        grid=(rows // tr,),
        in_specs=[pl.BlockSpec((tr, width), lambda i: (off + i, 0)), here, here, here],
        out_specs=[here] * 4,
        out_shape=[jax.ShapeDtypeStruct((rows, width), F32)] * 4,
        compiler_params=_params(("arbitrary",)),
    )(gsrc, w, m, v)


def kernel(x, mem, mix_pre_g, w_mix_in, conv_a_w, conv_b_w, conv_b_b, ln_b_g, ln_b_b, w_mix_out, mix_post_g, xa_pre_g, mem_norm_g, w_q, w_k, w_v, w_o, xa_post_g, ffn_pre_g, w_gate, w_up, w_down, ffn_post_g, loss_target, m_mix_pre_g, m_w_mix_in, m_conv_a_w, m_conv_b_w, m_conv_b_b, m_ln_b_g, m_ln_b_b, m_w_mix_out, m_mix_post_g, m_xa_pre_g, m_mem_norm_g, m_w_q, m_w_k, m_w_v, m_w_o, m_xa_post_g, m_ffn_pre_g, m_w_gate, m_w_up, m_w_down, m_ffn_post_g, v_mix_pre_g, v_w_mix_in, v_conv_a_w, v_conv_b_w, v_conv_b_b, v_ln_b_g, v_ln_b_b, v_w_mix_out, v_mix_post_g, v_xa_pre_g, v_mem_norm_g, v_w_q, v_w_k, v_w_v, v_w_o, v_xa_post_g, v_ffn_pre_g, v_w_gate, v_w_up, v_w_down, v_ffn_post_g):
    given = dict(locals())
    names = ["mix_pre_g", "w_mix_in", "conv_a_w", "conv_b_w", "conv_b_b", "ln_b_g", "ln_b_b", "w_mix_out",
             "mix_post_g", "xa_pre_g", "mem_norm_g", "w_q", "w_k", "w_v", "w_o", "xa_post_g", "ffn_pre_g",
             "w_gate", "w_up", "w_down", "ffn_post_g"]
    row = lambda a: a.reshape(1, -1)
    cx, cy, cc = lax.axis_index("x"), lax.axis_index("y"), lax.axis_index("c")
    chip = 2 * cx + cy
    ca_blk = conv_a_w.shape[1]

    conv_rows = CONV_A_W + CONV_B_W
    sw = jnp.concatenate([conv_a_w, conv_b_w, jnp.zeros((SMALL_W_ROWS - conv_rows, ca_blk), F32)], axis=0)
    g1g, pr0g, swg = _gather_weights([w_mix_in.astype(BF16), w_mix_out.astype(BF16), sw])
    conv_full = jnp.transpose(swg[:, :conv_rows, :], (1, 0, 2)).reshape(conv_rows, N_CHIPS * ca_blk)
    wa, wb = conv_full[:CONV_A_W], conv_full[CONV_A_W:]
    pr1 = jnp.concatenate([w_q, w_k, w_v, w_o, w_down], axis=0).astype(BF16)
    g2 = jnp.concatenate([w_gate, w_up], axis=0).astype(BF16)

    xs, ms, tgt = x[0], mem[0], loss_target[0]
    x1, u, o1, z1, pr1g, g2g = _mix_fwd(xs, row(mix_pre_g), row(mix_post_g), wa, wb, row(conv_b_b), row(ln_b_g),
                                        row(ln_b_b), g1g, pr0g, [pr1, g2])
    memn, kb, vb = _mem_kv(ms, row(mem_norm_g), pr1g)
    x2, q, o2, rx1, ro2 = _xattn_fwd(x1, row(xa_pre_g), row(xa_post_g), kb, vb, pr1g)
    h3, gs, us, do3, dx3, loss_part, d_ffn_post, rx2 = _ffn_fwd(x2, tgt, row(ffn_pre_g), row(ffn_post_g), g2g, pr1g)

    dh3p, cf1, cf2 = _ffn_bwd(h3, do3, gs, us, g2g, pr1g)
    dx1, d_ffn_pre, d_xa_post, d_xa_pre, dk, dv, cxa, yf1, yf2 = _xattn_bwd(
        dx3, dh3p, x2, x1, o2, q, rx2, ro2, rx1, kb, vb, row(ffn_pre_g), row(xa_post_g), row(xa_pre_g), pr1g, cf1, cf2)
    d_mem_g, cxa = _mem_bwd(dk, dv, ms, memn, row(mem_norm_g), pr1g, cxa)
    sems_x, flying_x, token_x = _exchange_start([cxa], [], "attn")
    dx, d_mix_pre, d_mix_post, dwa, dwb, dbb, dlng, dlnb, cm1, cm2 = _mix_bwd(
        dx1, xs, o1, u, z1, row(mix_pre_g), row(mix_post_g), wa, wb, row(ln_b_g), row(ln_b_b), g1g, pr0g, token_x)
    (yx,) = _exchange_wait(sems_x, flying_x, 1, 0, [d_mix_pre], "attn")

    small_parts = [d_mix_pre, dwa, dwb, dbb, dlng, dlnb, d_mix_post, d_xa_pre, d_mem_g, d_xa_post, d_ffn_pre,
                   d_ffn_post, loss_part]
    sizes = [p.size for p in small_parts]
    small = jnp.concatenate([p.reshape(-1) for p in small_parts])
    small_rows = -(-small.size // (8 * 128)) * 8
    small = jnp.pad(small, (0, small_rows * 128 - small.size)).reshape(small_rows, 128)
    sems, flying, token = _exchange_start([cm1, cm2], [small], "mix")
    place = jnp.stack([chip, cc, 2 * chip + cc]).astype(jnp.int32)

    res = {}

    def update(nm, src, row0, tr, transposed=False):
        view = (lambda a: a.T) if transposed else (lambda a: a)
        outs = _adamw(src, row0, view(given[nm]), view(given["m_" + nm]), view(given["v_" + nm]), tr, nm)
        res[nm] = [view(o) for o in outs]

    early = [(yf1, cf1, "gate_up"), (yf2, cf2, "down"), (yx, cxa, "attn")]
    r_gu, r_down, r_attn = _pair_gather([_sum_peers(y, c, place, 2, t, after=(token,)) for y, c, t in early], "early")
    update("w_gate", r_gu, 0, FF_BLK // 2, transposed=True)
    update("w_up", r_gu, FF_BLK, FF_BLK // 2, transposed=True)
    update("w_down", r_down, 0, FF_BLK // 2)
    for nm, row0 in (("w_q", GX_Q), ("w_k", GX_K), ("w_v", GX_V), ("w_o", GX_O)):
        update(nm, r_attn, row0, ROW_BLK)
    done = [res[nm][1] for nm in ("w_gate", "w_up", "w_down", "w_q", "w_k", "w_v", "w_o")]

    ym1, ym2, small_all = _exchange_wait(sems, flying, 2, 1, done, "mix")
    r_in, r_out = _pair_gather([_sum_peers(y, c, place, 2, t) for y, c, t in
                                [(ym1, cm1, "mix_in"), (ym2, cm2, "mix_out")]], "late")
    update("w_mix_in", r_in, 0, D_MODEL // 2)
    update("w_mix_out", r_out, 0, ROW_BLK)
    small_sum = _sum_peers(small_all, small, place, 1, "small").reshape(-1)
    red, pos = [], 0
    for p, sz in zip(small_parts, sizes):
        red.append(small_sum[pos:pos + sz].reshape(p.shape))
        pos += sz
    (r_mix_pre, r_wa, r_wb, r_bb, r_lng, r_lnb, r_mix_post, r_xa_pre, r_mem_g, r_xa_post, r_ffn_pre, r_ffn_post,
     r_loss) = red
    loss = r_loss[0, 0]

    small_grads = {"mix_pre_g": r_mix_pre, "conv_b_b": r_bb, "ln_b_g": r_lng, "ln_b_b": r_lnb,
                   "mix_post_g": r_mix_post, "xa_pre_g": r_xa_pre, "mem_norm_g": r_mem_g, "xa_post_g": r_xa_post,
                   "ffn_pre_g": r_ffn_pre, "ffn_post_g": r_ffn_post,
                   "conv_a_w": lax.dynamic_slice_in_dim(r_wa, chip * ca_blk, ca_blk, axis=1),
                   "conv_b_w": lax.dynamic_slice_in_dim(r_wb, chip * ca_blk, ca_blk, axis=1)}
    small_names = list(small_grads)

    def packed(prefix, grads=None):
        flat = jnp.concatenate([(grads[nm] if grads else given[prefix + nm]).reshape(-1) for nm in small_names])
        rows8 = -(-flat.size // (8 * 128)) * 8
        return jnp.pad(flat, (0, rows8 * 128 - flat.size)).reshape(rows8, 128)

    gp = packed("", small_grads)
    outs = _adamw(gp, 0, packed(""), packed("m_"), packed("v_"), gp.shape[0], "small")
    pos = 0
    for nm in small_names:
        shape = given[nm].shape
        sz = given[nm].size
        res[nm] = [o.reshape(-1)[pos:pos + sz].reshape(shape) for o in outs]
        pos += sz

    return (loss, dx[None], *[res[nm][0] for nm in names], *[res[nm][1] for nm in names],
            *[res[nm][2] for nm in names], *[res[nm][3] for nm in names])
```

```python
import jax
import jax.numpy as jnp
from jax import lax
from jax.experimental import pallas as pl
from jax.experimental.pallas import tpu as pltpu

F32 = jnp.float32
BF16 = jnp.bfloat16
MESH = pl.DeviceIdType.MESH

RMS_EPS = 1e-6
LN_EPS = 1e-5
D_MODEL = 1024
D_A = 512
D_B = 512
D_IN_ALL = 3 * D_A + 2 * D_B
CONV_A_W = 3
CONV_B_W = 31
HALO = 32
XA_HEADS = 4
HEAD_DIM = 256
D_FF = 2816
N_CHIPS = 4
N_DEV = 8
FF_BLK = D_FF // N_CHIPS
IN_BLK = D_IN_ALL // N_CHIPS
ROW_BLK = D_MODEL // N_CHIPS

ADAM_LR = 0.001
ADAM_B1 = 0.9
ADAM_B2 = 0.999
ADAM_EPS = 1e-08
ADAM_WD = 0.01
ADAM_STEP = 10

TILE_FWD = 1024
TILE_XATTN_FWD = 1024
TILE_FFN = 512
TILE_FFN_BWD = 512
TILE_XATTN_BWD = 512
TILE_BWD = 512
PHASE_ROWS = 256
FFN_BWD_BLOCKS = 2
CONV_ROWS_FWD = 64
CONV_ROWS = 32
V7X_VMEM_LIMIT = 56 * 1024 * 1024
V7X_VMEM_LIMIT_HIGH = 63 * 1024 * 1024
DMA_PRIORITIES = 2

P1_Q, P1_K, P1_V, P1_O, P1_DOWN = 0, 256, 512, 768, 1024
P1_ROWS = P1_DOWN + FF_BLK
GX_Q, GX_K, GX_V, GX_O = 0, 256, 512, 768
SMALL_W_ROWS = 48

ANY = pl.BlockSpec(memory_space=pl.ANY)


def _mm(a, b):
    return lax.dot_general(a, b, (((1,), (0,)), ((), ())), preferred_element_type=F32)


def _mm_nt(a, b):
    return lax.dot_general(a, b, (((1,), (1,)), ((), ())), preferred_element_type=F32)


def _mm_tn(a, b):
    return lax.dot_general(a, b, (((0,), (0,)), ((), ())), preferred_element_type=F32)


def _sigmoid(x):
    return 0.5 * jnp.tanh(0.5 * x) + 0.5


def _rms(x):
    r = lax.rsqrt(jnp.mean(x * x, axis=-1, keepdims=True) + RMS_EPS)
    return x * r, r


def _rms_bwd(dy, xn, r, g):
    gdy = dy * g
    dx = r * (gdy - xn * jnp.mean(gdy * xn, axis=-1, keepdims=True))
    return dx, jnp.sum(dy * xn, axis=0, keepdims=True)


def _fold8(a):
    out = a[0:8, :]
    for m in range(1, a.shape[0] // 8):
        out = out + a[8 * m:8 * m + 8, :]
    return out


def _full(shape):
    return pl.BlockSpec(shape, lambda *_: (0,) * len(shape))


def _params(sem=None, vmem_limit=V7X_VMEM_LIMIT):
    return pltpu.CompilerParams(dimension_semantics=sem, vmem_limit_bytes=vmem_limit)


def _copy_all(pairs):
    def scoped(sems):
        copies = [pltpu.make_async_copy(src, dst, sems.at[k]) for k, (src, dst) in enumerate(pairs)]
        for k, cp in enumerate(copies):
            cp.start(priority=k % DMA_PRIORITIES)
        for cp in copies:
            cp.wait()

    pl.run_scoped(scoped, pltpu.SemaphoreType.DMA((len(pairs),)))


def _rows(dst, src_hbm, row0, rows):
    return [(src_hbm.at[k, pl.ds(row0, rows), :], dst.at[pl.ds(rows * k, rows), :]) for k in range(N_CHIPS)]


def _cols(dst, src_hbm, cols):
    return [(src_hbm.at[k], dst.at[:, pl.ds(cols * k, cols)]) for k in range(N_CHIPS)]


def _fill_phases(src, sh, nrows, row0=0):
    for r in range(1, 8):
        sh[r - 1, pl.ds(0, nrows), :] = src[pl.ds(row0 + r, nrows), pl.ds(D_A, D_B)]


def _phase_rows(src, sh, off, start, size, row0=0):
    r = off % 8
    if r == 0:
        return src[pl.ds(off + start, size), pl.ds(D_A, D_B)]
    return sh[r - 1, pl.ds(off - r + start - row0, size), :]


def _my_place():
    x, y, c = lax.axis_index("x"), lax.axis_index("y"), lax.axis_index("c")
    return x, y, c, ((1 - x, y), (x, 1 - y), (1 - x, 1 - y))


def _remote(src, dst, send_sem, recv_sem, to):
    return pltpu.make_async_remote_copy(src_ref=src, dst_ref=dst, send_sem=send_sem, recv_sem=recv_sem,
                                        device_id=to, device_id_type=MESH)


def _gather_sems(np_):
    return [pltpu.SemaphoreType.DMA((np_, 3))] * 4 + [pltpu.SemaphoreType.DMA((np_,))] * 2


def _gather_start(srcs, dsts, sems):
    send, recv, _, _, osend, orecv = sems
    x, y, c, chips = _my_place()
    j = 2 * x + y
    for p in range(len(srcs)):
        _remote(srcs[p], dsts[p].at[j], osend.at[p], orecv.at[p], (x, y, 1 - c)).start()
        for nn, (kx, ky) in enumerate(chips):
            _remote(srcs[p].at[c], dsts[p].at[j, c], send.at[p, nn], recv.at[p, nn], (kx, ky, c)).start()


def _gather_forward(srcs, dsts, sems):
    send, recv, fsend, frecv, _, _ = sems
    x, y, c, chips = _my_place()
    for nn, (kx, ky) in enumerate(chips):
        for p in range(len(srcs)):
            blk = dsts[p].at[2 * kx + ky, c]
            _remote(blk, blk, send.at[p, nn], recv.at[p, nn], (kx, ky, c)).wait_recv()
            _remote(blk, blk, fsend.at[p, nn], frecv.at[p, nn], (x, y, 1 - c)).start()


def _gather_finish(srcs, dsts, sems):
    send, recv, fsend, frecv, osend, orecv = sems
    x, y, c, chips = _my_place()
    j = 2 * x + y
    for nn, (kx, ky) in enumerate(chips):
        for p in range(len(srcs)):
            other = dsts[p].at[2 * kx + ky, 1 - c]
            _remote(other, other, fsend.at[p, nn], frecv.at[p, nn], (x, y, 1 - c)).wait_recv()
    for nn, (kx, ky) in enumerate(chips):
        for p in range(len(srcs)):
            _remote(srcs[p].at[c], dsts[p].at[j, c], send.at[p, nn], recv.at[p, nn], (kx, ky, c)).wait_send()
            blk = dsts[p].at[2 * kx + ky, c]
            _remote(blk, blk, fsend.at[p, nn], frecv.at[p, nn], (x, y, 1 - c)).wait_send()
    for p in range(len(srcs)):
        _remote(srcs[p], dsts[p].at[j], osend.at[p], orecv.at[p], (x, y, 1 - c)).wait()


def _split_halves(a):
    return a.reshape(2, a.shape[0] // 2, a.shape[1])


def _gather_weights(packs):
    np_ = len(packs)
    split = [_split_halves(a) for a in packs]

    def body(*refs):
        srcs, dsts, sems = refs[:np_], refs[np_:2 * np_], refs[2 * np_:]
        _gather_start(srcs, dsts, sems)
        _gather_forward(srcs, dsts, sems)
        _gather_finish(srcs, dsts, sems)

    outs = pl.pallas_call(
        body,
        name="gather_weights",
        in_specs=[ANY] * np_,
        out_specs=[ANY] * np_,
        out_shape=[jax.ShapeDtypeStruct((N_CHIPS,) + a.shape, a.dtype) for a in split],
        scratch_shapes=_gather_sems(np_),
    )(*split)
    return [o.reshape((N_CHIPS,) + a.shape) for o, a in zip(outs, packs)]


def _peers(x, y, c):
    out = []
    for d in range(1, N_DEV):
        px = 1 - x if d & 4 else x
        py = 1 - y if d & 2 else y
        pc = 1 - c if d & 1 else c
        out.append(((px, py, pc), 4 * px + 2 * py + pc))
    return out


def _scatter_copies(srcs, dsts, send, recv):
    x, y, c, _ = _my_place()
    me = 4 * x + 2 * y + c
    out = []
    for a in range(len(srcs)):
        for d, ((px, py, pc), pidx) in enumerate(_peers(x, y, c)):
            piece = srcs[a].at[2 * px + py, pc]
            out.append((_remote(piece, dsts[a].at[me], send.at[a, d], recv.at[a, d], (px, py, pc)),
                        _remote(piece, dsts[a].at[pidx], send.at[a, d], recv.at[a, d], (px, py, pc))))
    return out


def _scatter_start(srcs, dsts, send, recv):
    for out_cp, _ in _scatter_copies(srcs, dsts, send, recv):
        out_cp.start()


def _scatter_wait(srcs, dsts, send, recv):
    for out_cp, in_cp in _scatter_copies(srcs, dsts, send, recv):
        in_cp.wait_recv()
        out_cp.wait_send()


def _scatter_sems(na):
    return [pltpu.SemaphoreType.DMA((na, N_DEV - 1))] * 2


def _slots(a):
    return jax.ShapeDtypeStruct((N_DEV,) + a.shape[2:], a.dtype)


def _by_halves(a):
    return a.reshape(a.shape[0], 2, a.shape[1] // 2, a.shape[2])


def _mix_fwd(x, gpre, gpost, wa, wb, bb, lng, lnb, g1g, pr0g, late):
    S, D = x.shape
    T = min(TILE_FWD, S)
    n = S // T
    nl = len(late)
    phase_rows = min(T, PHASE_ROWS)
    late_split = [_split_halves(a) for a in late]

    def body(*refs):
        (x_ref, gpre_ref, gpost_ref, wa_ref, wb_ref, bb_ref, lng_ref, lnb_ref, g1g_hbm, pr0g_hbm) = refs[:10]
        srcs = refs[10:10 + nl]
        x1_ref, u_ref, o1_ref, z1_ref = refs[10 + nl:14 + nl]
        dsts = refs[14 + nl:14 + 2 * nl]
        win_v, wout_v, ext, sh, z1buf = refs[14 + 2 * nl:19 + 2 * nl]
        sems = refs[19 + 2 * nl:]
        i = pl.program_id(0)

        @pl.when(i == 0)
        def _():
            _gather_start(srcs, dsts, sems)
            _copy_all(_cols(win_v, g1g_hbm, IN_BLK) + _rows(wout_v, pr0g_hbm, 0, ROW_BLK))
            ext[pl.ds(0, HALO), :] = jnp.zeros((HALO, D_A + D_B), F32)

        @pl.when(i == max(n - 2, 0))
        def _():
            _gather_forward(srcs, dsts, sems)

        xv = x_ref[...]
        xn, _ = _rms(xv)
        h = (xn * gpre_ref[...]).astype(BF16)
        u = _mm(h, win_v[...])
        u_ref[...] = u.astype(BF16)
        b_a = u[:, 0:D_A]
        cv = u[:, D_A:2 * D_A] * u[:, 2 * D_A:3 * D_A]
        z0 = u[:, 3 * D_A:3 * D_A + D_B] * _sigmoid(u[:, 3 * D_A + D_B:])
        ext[pl.ds(HALO, T), pl.ds(0, D_A)] = cv
        ext[pl.ds(HALO, T), pl.ds(D_A, D_B)] = z0

        conv_a = ext[pl.ds(HALO - 2, T), pl.ds(0, D_A)] * wa_ref[0:1, :]
        for k in range(1, CONV_A_W):
            conv_a = conv_a + ext[pl.ds(HALO - 2 + k, T), pl.ds(0, D_A)] * wa_ref[k:k + 1, :]
        y_a = b_a * conv_a

        base = HALO - (CONV_B_W - 1)
        for row0 in range(0, T, phase_rows):
            _fill_phases(ext, sh, phase_rows + HALO - 8, row0)

            def chunk(ci, carry, row0=row0):
                start = pl.multiple_of(row0 + ci * CONV_ROWS_FWD, 8)
                acc = jnp.broadcast_to(bb_ref[...], (CONV_ROWS_FWD, D_B))
                for k in range(CONV_B_W):
                    acc = acc + _phase_rows(ext, sh, base + k, start, CONV_ROWS_FWD, row0) * wb_ref[k:k + 1, :]
                z1buf[pl.ds(start, CONV_ROWS_FWD), :] = acc
                return carry

            lax.fori_loop(0, phase_rows // CONV_ROWS_FWD, chunk, 0)
        z1 = z1buf[...]
        z1_ref[...] = z1.astype(BF16)
        mu = jnp.mean(z1, axis=-1, keepdims=True)
        zc = z1 - mu
        rstd = lax.rsqrt(jnp.mean(zc * zc, axis=-1, keepdims=True) + LN_EPS)
        l = zc * rstd * lng_ref[...] + lnb_ref[...]
        y_b = l * _sigmoid(l)
        y = jnp.concatenate([y_a, y_b], axis=-1).astype(BF16)
        o1 = _mm(y, wout_v[...])
        o1_ref[...] = o1.astype(BF16)
        o1n, _ = _rms(o1)
        x1_ref[...] = xv + o1n * gpost_ref[...]
        ext[pl.ds(0, HALO), :] = ext[pl.ds(T, HALO), :]

        @pl.when(i == n - 1)
        def _():
            _gather_finish(srcs, dsts, sems)

    tok = lambda w: pl.BlockSpec((T, w), lambda i: (i, 0))
    outs = pl.pallas_call(
        body,
        name="mix_fwd",
        grid=(n,),
        in_specs=[tok(D), _full((1, D)), _full((1, D)), _full((CONV_A_W, D_A)), _full((CONV_B_W, D_B)),
                  _full((1, D_B)), _full((1, D_B)), _full((1, D_B)), ANY, ANY] + [ANY] * nl,
        out_specs=[tok(D), tok(D_IN_ALL), tok(D), tok(D_B)] + [ANY] * nl,
        out_shape=[jax.ShapeDtypeStruct((S, D), F32), jax.ShapeDtypeStruct((S, D_IN_ALL), BF16),
                   jax.ShapeDtypeStruct((S, D), BF16), jax.ShapeDtypeStruct((S, D_B), BF16)]
        + [jax.ShapeDtypeStruct((N_CHIPS,) + a.shape, a.dtype) for a in late_split],
        scratch_shapes=[pltpu.VMEM((D, D_IN_ALL), BF16), pltpu.VMEM((D_A + D_B, D), BF16),
                        pltpu.VMEM((HALO + T, D_A + D_B), F32), pltpu.VMEM((7, HALO + phase_rows, D_B), F32),
                        pltpu.VMEM((T, D_B), F32)] + _gather_sems(nl),
        compiler_params=_params(("arbitrary",), V7X_VMEM_LIMIT_HIGH),
    )(x, gpre, gpost, wa, wb, bb, lng, lnb, g1g, pr0g, *late_split)
    return list(outs[:4]) + [o.reshape((N_CHIPS,) + a.shape) for o, a in zip(outs[4:], late)]


def _mix_bwd(dx1, x, o1, u, z1s, gpre, gpost, wa, wb, lng, lnb, g1g, pr0g, after):
    S, D = x.shape
    T = min(TILE_BWD, S)
    n = S // T
    hb = T // HALO
    phase_rows = min(T, PHASE_ROWS)

    def body(dx1_ref, x_ref, o1_ref, u_ref, uh_ref, z1_ref, gpre_ref, gpost_ref, wa_ref, wb_ref,
             lng_ref, lnb_ref, g1g_hbm, pr0g_hbm, _,
             dx_ref, dgpre_ref, dgpost_ref, dwa_ref, dwb_ref, dbb_ref, dlng_ref, dlnb_ref, cm1_hbm, cm2_hbm,
             win_v, wout_v, dwin_acc, dwout_acc, ext, ext2, shb, dwb_acc):
        i = pl.program_id(0)

        @pl.when(i == 0)
        def _():
            _copy_all(_cols(win_v, g1g_hbm, IN_BLK) + _rows(wout_v, pr0g_hbm, 0, ROW_BLK))
            dwin_acc[...] = jnp.zeros_like(dwin_acc)
            dwout_acc[...] = jnp.zeros_like(dwout_acc)
            dwb_acc[...] = jnp.zeros_like(dwb_acc)
            ext2[pl.ds(T, HALO), :] = jnp.zeros((HALO, D_A + D_B), F32)
            for ref in (dgpre_ref, dgpost_ref, dwa_ref, dbb_ref, dlng_ref, dlnb_ref):
                ref[...] = jnp.zeros_like(ref)

        o1n, r1 = _rms(o1_ref[...].astype(F32))
        dx1v = dx1_ref[...]
        d_o1, dgp = _rms_bwd(dx1v, o1n, r1, gpost_ref[...])
        dgpost_ref[...] += dgp
        d_o1b = d_o1.astype(BF16)
        dy = _mm_nt(d_o1b, wout_v[...])

        first = (i == n - 1).astype(F32)
        uh = uh_ref[...].astype(F32) * (1.0 - first)
        ext[pl.ds(0, HALO), pl.ds(0, D_A)] = uh[:, D_A:2 * D_A] * uh[:, 2 * D_A:3 * D_A]
        uf = u_ref[...].astype(F32)
        b_a = uf[:, 0:D_A]
        c_a = uf[:, D_A:2 * D_A]
        v_a = uf[:, 2 * D_A:3 * D_A]
        gv = uf[:, 3 * D_A:3 * D_A + D_B]
        sg = _sigmoid(uf[:, 3 * D_A + D_B:])
        ext[pl.ds(HALO, T), pl.ds(0, D_A)] = c_a * v_a
        ext[pl.ds(HALO, T), pl.ds(D_A, D_B)] = gv * sg
        conv_a = ext[pl.ds(HALO - 2, T), pl.ds(0, D_A)] * wa_ref[0:1, :]
        for k in range(1, CONV_A_W):
            conv_a = conv_a + ext[pl.ds(HALO - 2 + k, T), pl.ds(0, D_A)] * wa_ref[k:k + 1, :]
        z1 = z1_ref[...].astype(F32)
        mu = jnp.mean(z1, axis=-1, keepdims=True)
        zc = z1 - mu
        rstd = lax.rsqrt(jnp.mean(zc * zc, axis=-1, keepdims=True) + LN_EPS)
        zn = zc * rstd
        l = zn * lng_ref[...] + lnb_ref[...]
        sl = _sigmoid(l)
        y = jnp.concatenate([b_a * conv_a, l * sl], axis=-1).astype(BF16)
        dwout_acc[...] += _mm_tn(y, d_o1b)

        dy_a = dy[:, 0:D_A]
        dl = dy[:, D_A:] * (sl * (1.0 + l * (1.0 - sl)))
        dlng_ref[...] += jnp.sum(dl * zn, axis=0, keepdims=True)
        dlnb_ref[...] += jnp.sum(dl, axis=0, keepdims=True)
        dzn = dl * lng_ref[...]
        dz1 = rstd * (dzn - jnp.mean(dzn, axis=-1, keepdims=True) - zn * jnp.mean(dzn * zn, axis=-1, keepdims=True))
        dbb_ref[...] += jnp.sum(dz1, axis=0, keepdims=True)
        d_conv = dy_a * b_a
        ext2[pl.ds(0, T), pl.ds(0, D_A)] = d_conv
        ext2[pl.ds(0, T), pl.ds(D_A, D_B)] = dz1

        d_cv = ext2[pl.ds(CONV_A_W - 1, T), pl.ds(0, D_A)] * wa_ref[0:1, :]
        for k in range(1, CONV_A_W):
            d_cv = d_cv + ext2[pl.ds(CONV_A_W - 1 - k, T), pl.ds(0, D_A)] * wa_ref[k:k + 1, :]
        for k in range(CONV_A_W):
            dwa_ref[k:k + 1, :] += jnp.sum(d_conv * ext[pl.ds(HALO - 2 + k, T), pl.ds(0, D_A)], axis=0, keepdims=True)

        for row0 in range(0, T, phase_rows):
            _fill_phases(ext2, shb, phase_rows + HALO - 8, row0)

            def chunk(ci, carry, row0=row0):
                start = pl.multiple_of(row0 + ci * CONV_ROWS, 8)
                z0c = ext[pl.ds(HALO + start, CONV_ROWS), pl.ds(D_A, D_B)]
                acc = jnp.zeros((CONV_ROWS, D_B), F32)
                for k in range(CONV_B_W):
                    ahead = _phase_rows(ext2, shb, CONV_B_W - 1 - k, start, CONV_ROWS, row0)
                    acc = acc + ahead * wb_ref[k:k + 1, :]
                    dwb_acc[k] += _fold8(z0c * ahead)
                ext[pl.ds(HALO + start, CONV_ROWS), pl.ds(D_A, D_B)] = acc
                return carry

            lax.fori_loop(0, phase_rows // CONV_ROWS, chunk, 0)
        dz0 = ext[pl.ds(HALO, T), pl.ds(D_A, D_B)]
        du = jnp.concatenate([dy_a * conv_a, d_cv * v_a, d_cv * c_a, dz0 * sg, dz0 * gv * sg * (1.0 - sg)],
                             axis=-1).astype(BF16)
        dh = _mm_nt(du, win_v[...])
        xn, r0 = _rms(x_ref[...])
        dwin_acc[...] += _mm_tn((xn * gpre_ref[...]).astype(BF16), du)
        dxp, dg0 = _rms_bwd(dh, xn, r0, gpre_ref[...])
        dgpre_ref[...] += dg0
        dx_ref[...] = dx1v + dxp
        ext2[pl.ds(T, HALO), :] = ext2[pl.ds(0, HALO), :]

        @pl.when(i == n - 1)
        def _():
            for k in range(CONV_B_W):
                dwb_ref[k:k + 1, :] = jnp.sum(dwb_acc[k], axis=0, keepdims=True)
            win_v[...] = dwin_acc[...].astype(BF16)
            wout_v[...] = dwout_acc[...].astype(BF16)
            _copy_all([(win_v.at[:, pl.ds(IN_BLK * k, IN_BLK)], cm1_hbm.at[k]) for k in range(N_CHIPS)]
                      + [(wout_v.at[pl.ds(ROW_BLK * k, ROW_BLK), :], cm2_hbm.at[k]) for k in range(N_CHIPS)])

    rev = lambda w: pl.BlockSpec((T, w), lambda i: (n - 1 - i, 0))
    halo = pl.BlockSpec((HALO, D_IN_ALL), lambda i: (jnp.maximum((n - 1 - i) * hb - 1, 0), 0))
    return pl.pallas_call(
        body,
        name="mix_bwd",
        grid=(n,),
        in_specs=[rev(D), rev(D), rev(D), rev(D_IN_ALL), halo, rev(D_B), _full((1, D)),
                  _full((1, D)), _full((CONV_A_W, D_A)), _full((CONV_B_W, D_B)), _full((1, D_B)), _full((1, D_B)),
                  ANY, ANY, ANY],
        out_specs=[rev(D), _full((1, D)), _full((1, D)), _full((CONV_A_W, D_A)), _full((CONV_B_W, D_B)),
                   _full((1, D_B)), _full((1, D_B)), _full((1, D_B)), ANY, ANY],
        out_shape=[jax.ShapeDtypeStruct((S, D), F32), jax.ShapeDtypeStruct((1, D), F32),
                   jax.ShapeDtypeStruct((1, D), F32), jax.ShapeDtypeStruct((CONV_A_W, D_A), F32),
                   jax.ShapeDtypeStruct((CONV_B_W, D_B), F32), jax.ShapeDtypeStruct((1, D_B), F32),
                   jax.ShapeDtypeStruct((1, D_B), F32), jax.ShapeDtypeStruct((1, D_B), F32),
                   jax.ShapeDtypeStruct((N_CHIPS, D, IN_BLK), BF16),
                   jax.ShapeDtypeStruct((N_CHIPS, ROW_BLK, D), BF16)],
        scratch_shapes=[pltpu.VMEM((D, D_IN_ALL), BF16), pltpu.VMEM((D_A + D_B, D), BF16),
                        pltpu.VMEM((D, D_IN_ALL), F32), pltpu.VMEM((D_A + D_B, D), F32),
                        pltpu.VMEM((HALO + T, D_A + D_B), F32), pltpu.VMEM((HALO + T, D_A + D_B), F32),
                        pltpu.VMEM((7, HALO + phase_rows, D_B), F32), pltpu.VMEM((CONV_B_W, 8, D_B), F32)],
        compiler_params=_params(("arbitrary",), V7X_VMEM_LIMIT_HIGH),
    )(dx1, x, o1, u, u, z1s, gpre, gpost, wa, wb, lng, lnb, g1g, pr0g, after)


def _mem_kv(mem, gmem, pr1g):
    M, D = mem.shape

    def body(mem_ref, g_ref, pr1g_hbm, memn_ref, k_ref, v_ref, wk_v, wv_v):
        _copy_all(_rows(wk_v, pr1g_hbm, P1_K, ROW_BLK) + _rows(wv_v, pr1g_hbm, P1_V, ROW_BLK))
        mn, _ = _rms(mem_ref[...])
        mb = (mn * g_ref[...]).astype(BF16)
        memn_ref[...] = mb
        k_ref[...] = _mm(mb, wk_v[...]).astype(BF16)
        v_ref[...] = _mm(mb, wv_v[...]).astype(BF16)

    return pl.pallas_call(
        body,
        name="mem_kv",
        grid=(1,),
        in_specs=[_full((M, D)), _full((1, D)), ANY],
        out_specs=[_full((M, D))] * 3,
        out_shape=[jax.ShapeDtypeStruct((M, D), BF16)] * 3,
        scratch_shapes=[pltpu.VMEM((D, D), BF16), pltpu.VMEM((D, D), BF16)],
        compiler_params=_params(("arbitrary",)),
    )(mem, gmem, pr1g)


def _attend(qb, kb, vb):
    scale = HEAD_DIM ** -0.5
    ps, os_ = [], []
    for hd in range(XA_HEADS):
        cols = slice(HEAD_DIM * hd, HEAD_DIM * (hd + 1))
        s = _mm_nt(qb[:, cols], kb[:, cols]) * scale
        e = jnp.exp(s - jnp.max(s, axis=-1, keepdims=True))
        p = e * (1.0 / jnp.sum(e, axis=-1, keepdims=True))
        ps.append(p)
        os_.append(_mm(p.astype(BF16), vb[:, cols]))
    return ps, jnp.concatenate(os_, axis=-1).astype(BF16)


def _xattn_fwd(x1, gpre, gpost, kb, vb, pr1g):
    S, D = x1.shape
    M = kb.shape[0]
    T = min(TILE_XATTN_FWD, S)
    n = S // T

    def body(x1_ref, gpre_ref, gpost_ref, k_ref, v_ref, pr1g_hbm, x2_ref, q_ref, o2_ref, rx_ref, ro_ref, wq_v, wo_v):
        @pl.when(pl.program_id(0) == 0)
        def _():
            _copy_all(_rows(wq_v, pr1g_hbm, P1_Q, ROW_BLK) + _rows(wo_v, pr1g_hbm, P1_O, ROW_BLK))

        xv = x1_ref[...]
        xn, rx = _rms(xv)
        rx_ref[...] = rx
        qb = _mm((xn * gpre_ref[...]).astype(BF16), wq_v[...]).astype(BF16)
        q_ref[...] = qb
        _, ob = _attend(qb, k_ref[...], v_ref[...])
        o2 = _mm(ob, wo_v[...])
        o2_ref[...] = o2.astype(BF16)
        o2n, ro = _rms(o2)
        ro_ref[...] = ro
        x2_ref[...] = xv + o2n * gpost_ref[...]

    tok = lambda w: pl.BlockSpec((T, w), lambda i: (i, 0))
    return pl.pallas_call(
        body,
        name="xattn_fwd",
        grid=(n,),
        in_specs=[tok(D), _full((1, D)), _full((1, D)), _full((M, D)), _full((M, D)), ANY],
        out_specs=[tok(D), tok(D), tok(D), tok(1), tok(1)],
        out_shape=[jax.ShapeDtypeStruct((S, D), F32), jax.ShapeDtypeStruct((S, D), BF16),
                   jax.ShapeDtypeStruct((S, D), BF16), jax.ShapeDtypeStruct((S, 1), F32),
                   jax.ShapeDtypeStruct((S, 1), F32)],
        scratch_shapes=[pltpu.VMEM((D, D), BF16), pltpu.VMEM((D, D), BF16)],
        compiler_params=_params(("arbitrary",)),
    )(x1, gpre, gpost, kb, vb, pr1g)


def _xattn_bwd(dx3, dh3p, x2, x1, o2, q, rx2, ro2, rx1, kb, vb, gffn, gpost, gpre, pr1g, cf1, cf2):
    S, D = x1.shape
    M = kb.shape[0]
    T = min(TILE_XATTN_BWD, S)
    n = S // T
    scale = HEAD_DIM ** -0.5
    nparts = dh3p.shape[0]
    cfs = [_by_halves(cf1), _by_halves(cf2)]

    def body(*refs):
        dx3_ref, dh3_refs = refs[0], refs[1:1 + nparts]
        (x2_ref, x1_ref, o2_ref, q_ref, rx2_ref, ro2_ref, rx1_ref, k_ref, v_ref, gffn_ref, gpost_ref, gpre_ref,
         pr1g_hbm, cf1_hbm, cf2_hbm,
         dx1_ref, dgffn_ref, dgpost_ref, dgpre_ref, dk_ref, dv_ref, cx_hbm, yf1_hbm, yf2_hbm,
         wq_v, wo_v, dwq_acc, dwo_acc, send, recv) = refs[1 + nparts:]
        i = pl.program_id(0)

        @pl.when(i == 0)
        def _():
            _scatter_start([cf1_hbm, cf2_hbm], [yf1_hbm, yf2_hbm], send, recv)
            _copy_all(_rows(wq_v, pr1g_hbm, P1_Q, ROW_BLK) + _rows(wo_v, pr1g_hbm, P1_O, ROW_BLK))
            dwq_acc[...] = jnp.zeros_like(dwq_acc)
            dwo_acc[...] = jnp.zeros_like(dwo_acc)
            for ref in (dgffn_ref, dgpost_ref, dgpre_ref, dk_ref, dv_ref):
                ref[...] = jnp.zeros_like(ref)

        r2 = rx2_ref[...]
        x2n = x2_ref[...] * r2
        dh3 = dh3_refs[0][...].astype(F32)
        for ref in dh3_refs[1:]:
            dh3 = dh3 + ref[...].astype(F32)
        dxp, dg = _rms_bwd(dh3, x2n, r2, gffn_ref[...])
        dgffn_ref[...] += dg
        dx2 = dx3_ref[...] + dxp
        ro = ro2_ref[...]
        o2n = o2_ref[...].astype(F32) * ro
        d_o2, dg = _rms_bwd(dx2, o2n, ro, gpost_ref[...])
        dgpost_ref[...] += dg
        d_o2b = d_o2.astype(BF16)
        d_o = _mm_nt(d_o2b, wo_v[...]).astype(BF16)
        qb = q_ref[...]
        kv = k_ref[...]
        vv = v_ref[...]
        ps, ob = _attend(qb, kv, vv)
        dwo_acc[...] += _mm_tn(ob, d_o2b)
        dqs = []
        for hd in range(XA_HEADS):
            cols = slice(HEAD_DIM * hd, HEAD_DIM * (hd + 1))
            p = ps[hd]
            dp = _mm_nt(d_o[:, cols], vv[:, cols])
            dv_ref[:, cols] += _mm_tn(p.astype(BF16), d_o[:, cols])
            ds = (p * (dp - jnp.sum(p * dp, axis=-1, keepdims=True)) * scale).astype(BF16)
            dqs.append(_mm(ds, kv[:, cols]))
            dk_ref[:, cols] += _mm_tn(ds, qb[:, cols])
        dq = jnp.concatenate(dqs, axis=-1).astype(BF16)
        dh2 = _mm_nt(dq, wq_v[...])
        r1 = rx1_ref[...]
        x1n = x1_ref[...] * r1
        dwq_acc[...] += _mm_tn((x1n * gpre_ref[...]).astype(BF16), dq)
        dxp, dg = _rms_bwd(dh2, x1n, r1, gpre_ref[...])
        dgpre_ref[...] += dg
        dx1_ref[...] = dx2 + dxp

        @pl.when(i == n - 1)
        def _():
            wq_v[...] = dwq_acc[...].astype(BF16)
            wo_v[...] = dwo_acc[...].astype(BF16)
            _copy_all([(w.at[pl.ds(ROW_BLK * k, ROW_BLK), :], cx_hbm.at[k, pl.ds(row0, ROW_BLK), :])
                       for w, row0 in ((wq_v, GX_Q), (wo_v, GX_O)) for k in range(N_CHIPS)])
            _scatter_wait([cf1_hbm, cf2_hbm], [yf1_hbm, yf2_hbm], send, recv)

    tok = lambda w: pl.BlockSpec((T, w), lambda i: (i, 0))
    part = lambda j: pl.BlockSpec((None, T, D), lambda i: (j, i, 0))
    return pl.pallas_call(
        body,
        name="xattn_bwd",
        grid=(n,),
        in_specs=[tok(D)] + [part(j) for j in range(nparts)] + [tok(D), tok(D), tok(D), tok(D), tok(1), tok(1), tok(1),
                                                                 _full((M, D)), _full((M, D)), _full((1, D)),
                                                                 _full((1, D)), _full((1, D)), ANY, ANY, ANY],
        out_specs=[tok(D), _full((1, D)), _full((1, D)), _full((1, D)), _full((M, D)), _full((M, D)), ANY, ANY, ANY],
        out_shape=[jax.ShapeDtypeStruct((S, D), F32), jax.ShapeDtypeStruct((1, D), F32),
                   jax.ShapeDtypeStruct((1, D), F32), jax.ShapeDtypeStruct((1, D), F32),
                   jax.ShapeDtypeStruct((M, D), F32), jax.ShapeDtypeStruct((M, D), F32),
                   jax.ShapeDtypeStruct((N_CHIPS, D, D), BF16), _slots(cfs[0]), _slots(cfs[1])],
        scratch_shapes=[pltpu.VMEM((D, D), BF16), pltpu.VMEM((D, D), BF16),
                        pltpu.VMEM((D, D), F32), pltpu.VMEM((D, D), F32)] + _scatter_sems(2),
        compiler_params=_params(("arbitrary",)),
    )(dx3, *([dh3p] * nparts), x2, x1, o2, q, rx2, ro2, rx1, kb, vb, gffn, gpost, gpre, pr1g, *cfs)


def _mem_bwd(dk, dv, mem, memn, gmem, pr1g, cx_in):
    M, D = mem.shape

    def body(dk_ref, dv_ref, mem_ref, memn_ref, g_ref, pr1g_hbm, cx_hbm, dg_ref, cx_out, wk_v, wv_v):
        del cx_hbm
        _copy_all(_rows(wk_v, pr1g_hbm, P1_K, ROW_BLK) + _rows(wv_v, pr1g_hbm, P1_V, ROW_BLK))
        dkb = dk_ref[...].astype(BF16)
        dvb = dv_ref[...].astype(BF16)
        mb = memn_ref[...]
        dmn = _mm_nt(dkb, wk_v[...]) + _mm_nt(dvb, wv_v[...])
        mn, _ = _rms(mem_ref[...])
        dg_ref[...] = jnp.sum(dmn * mn, axis=0, keepdims=True)
        wk_v[...] = _mm_tn(mb, dkb).astype(BF16)
        wv_v[...] = _mm_tn(mb, dvb).astype(BF16)
        _copy_all([(w.at[pl.ds(ROW_BLK * k, ROW_BLK), :], cx_out.at[k, pl.ds(row0, ROW_BLK), :])
                   for w, row0 in ((wk_v, GX_K), (wv_v, GX_V)) for k in range(N_CHIPS)])

    return pl.pallas_call(
        body,
        name="mem_bwd",
        grid=(1,),
        in_specs=[_full((M, D)), _full((M, D)), _full((M, D)), _full((M, D)), _full((1, D)), ANY, ANY],
        out_specs=[_full((1, D)), ANY],
        out_shape=[jax.ShapeDtypeStruct((1, D), F32), jax.ShapeDtypeStruct(cx_in.shape, BF16)],
        input_output_aliases={6: 1},
        scratch_shapes=[pltpu.VMEM((D, D), BF16), pltpu.VMEM((D, D), BF16)],
        compiler_params=_params(("arbitrary",)),
    )(dk, dv, mem, memn, gmem, pr1g, cx_in)


def _ffn_fwd(x2, target, gpre, gpost, g2g, pr1g):
    S, D = x2.shape
    T = min(TILE_FFN, S)
    n = S // T

    def body(x2_ref, t_ref, gpre_ref, gpost_ref, g2g_hbm, pr1g_hbm,
             h3_ref, g_hbm, u_hbm, do3_ref, dx3_ref, loss_ref, dgpost_ref, rx_ref, wg_v, wu_v, wd_v, gst, ust, sem):
        i = pl.program_id(0)

        @pl.when(i == 0)
        def _():
            _copy_all([(g2g_hbm.at[:, pl.ds(0, D), :], wg_v), (g2g_hbm.at[:, pl.ds(D, D), :], wu_v),
                       (pr1g_hbm.at[:, pl.ds(P1_DOWN, FF_BLK), :], wd_v)])
            loss_ref[...] = jnp.zeros_like(loss_ref)
            dgpost_ref[...] = jnp.zeros_like(dgpost_ref)

        xv = x2_ref[...]
        xn, rx = _rms(xv)
        rx_ref[...] = rx
        hb = (xn * gpre_ref[...]).astype(BF16)
        h3_ref[...] = hb
        o3 = jnp.zeros((T, D), F32)
        out = [None, None]
        for c in range(N_CHIPS):
            slot = c % 2
            if out[slot] is not None:
                for cp in out[slot]:
                    cp.wait()
            g = _mm(hb, wg_v[c])
            u = _mm(hb, wu_v[c])
            gst[slot] = g.astype(BF16)
            ust[slot] = u.astype(BF16)
            out[slot] = (pltpu.make_async_copy(gst.at[slot], g_hbm.at[c, i], sem.at[0, slot]),
                         pltpu.make_async_copy(ust.at[slot], u_hbm.at[c, i], sem.at[1, slot]))
            for cp in out[slot]:
                cp.start()
            o3 = o3 + _mm((g * _sigmoid(g) * u).astype(BF16), wd_v[c])
        for pair in out:
            for cp in pair:
                cp.wait()
        o3n, r3 = _rms(o3)
        diff = xv + o3n * gpost_ref[...] - t_ref[...]
        sq = jnp.sum(jnp.sum(diff * diff, axis=-1, keepdims=True), axis=0, keepdims=True)
        loss_ref[...] += sq * (0.5 / D)
        dx3 = diff * (1.0 / D)
        dx3_ref[...] = dx3
        d_o3, dg = _rms_bwd(dx3, o3n, r3, gpost_ref[...])
        dgpost_ref[...] += dg
        do3_ref[...] = d_o3.astype(BF16)

    tok = lambda w: pl.BlockSpec((T, w), lambda i: (i, 0))
    h3, gs, us, do3, dx3, loss, dgpost, rx2 = pl.pallas_call(
        body,
        name="ffn_fwd",
        grid=(n,),
        in_specs=[tok(D), tok(D), _full((1, D)), _full((1, D)), ANY, ANY],
        out_specs=[tok(D), ANY, ANY, tok(D), tok(D), _full((1, 128)), _full((1, D)), tok(1)],
        out_shape=[jax.ShapeDtypeStruct((S, D), BF16), jax.ShapeDtypeStruct((N_CHIPS, n, T, FF_BLK), BF16),
                   jax.ShapeDtypeStruct((N_CHIPS, n, T, FF_BLK), BF16), jax.ShapeDtypeStruct((S, D), BF16),
                   jax.ShapeDtypeStruct((S, D), F32), jax.ShapeDtypeStruct((1, 128), F32),
                   jax.ShapeDtypeStruct((1, D), F32), jax.ShapeDtypeStruct((S, 1), F32)],
        scratch_shapes=[pltpu.VMEM((N_CHIPS, D, FF_BLK), BF16), pltpu.VMEM((N_CHIPS, D, FF_BLK), BF16),
                        pltpu.VMEM((N_CHIPS, FF_BLK, D), BF16), pltpu.VMEM((2, T, FF_BLK), BF16),
                        pltpu.VMEM((2, T, FF_BLK), BF16), pltpu.SemaphoreType.DMA((2, 2))],
        compiler_params=_params(("arbitrary",)),
    )(x2, target, gpre, gpost, g2g, pr1g)
    return h3, gs.reshape(N_CHIPS, S, FF_BLK), us.reshape(N_CHIPS, S, FF_BLK), do3, dx3, loss, dgpost, rx2


def _ffn_bwd(h3, do3, gs, us, g2g, pr1g):
    S, D = h3.shape
    T = min(TILE_FFN_BWD, S)
    n = S // T
    NP = FFN_BWD_BLOCKS

    def body(h3_ref, do3_ref, g_ref, u_ref, g2g_hbm, pr1g_hbm, dh3_ref, cf1_hbm, cf2_hbm,
             wg_v, wu_v, wd_v, dwg_acc, dwu_acc, dwd_acc):
        jp = pl.program_id(0)
        i = pl.program_id(1)
        blocks = pl.ds(NP * jp, NP)

        @pl.when(i == 0)
        def _():
            _copy_all([(g2g_hbm.at[blocks, pl.ds(0, D), :], wg_v), (g2g_hbm.at[blocks, pl.ds(D, D), :], wu_v),
                       (pr1g_hbm.at[blocks, pl.ds(P1_DOWN, FF_BLK), :], wd_v)])
            dwg_acc[...] = jnp.zeros_like(dwg_acc)
            dwu_acc[...] = jnp.zeros_like(dwu_acc)
            dwd_acc[...] = jnp.zeros_like(dwd_acc)

        hb = h3_ref[...]
        d_o3 = do3_ref[...]
        dh = jnp.zeros((T, D), F32)
        for c in range(NP):
            da = _mm_nt(d_o3, wd_v[c])
            g = g_ref[c].astype(F32)
            u = u_ref[c].astype(F32)
            sg = _sigmoid(g)
            sl = g * sg
            dwd_acc[c] += _mm_tn((sl * u).astype(BF16), d_o3)
            dub = (da * sl).astype(BF16)
            dgb = (da * u * (sg * (1.0 + g * (1.0 - sg)))).astype(BF16)
            dwg_acc[c] += _mm_tn(dgb, hb)
            dwu_acc[c] += _mm_tn(dub, hb)
            dh = dh + _mm_nt(dgb, wg_v[c]) + _mm_nt(dub, wu_v[c])
        dh3_ref[...] = dh.astype(BF16)

        @pl.when(i == n - 1)
        def _():
            wd_v[...] = dwg_acc[...].astype(BF16)
            pltpu.sync_copy(wd_v, cf1_hbm.at[blocks, pl.ds(0, FF_BLK), :])
            wd_v[...] = dwu_acc[...].astype(BF16)
            pltpu.sync_copy(wd_v, cf1_hbm.at[blocks, pl.ds(FF_BLK, FF_BLK), :])
            wd_v[...] = dwd_acc[...].astype(BF16)
            pltpu.sync_copy(wd_v, cf2_hbm.at[blocks])

    tok = lambda w: pl.BlockSpec((T, w), lambda jp, i: (i, 0))
    blk = pl.BlockSpec((NP, T, FF_BLK), lambda jp, i: (jp, i, 0))
    return pl.pallas_call(
        body,
        name="ffn_bwd",
        grid=(N_CHIPS // NP, n),
        in_specs=[tok(D), tok(D), blk, blk, ANY, ANY],
        out_specs=[pl.BlockSpec((None, T, D), lambda jp, i: (jp, i, 0)), ANY, ANY],
        out_shape=[jax.ShapeDtypeStruct((N_CHIPS // NP, S, D), BF16),
                   jax.ShapeDtypeStruct((N_CHIPS, 2 * FF_BLK, D), BF16),
                   jax.ShapeDtypeStruct((N_CHIPS, FF_BLK, D), BF16)],
        scratch_shapes=[pltpu.VMEM((NP, D, FF_BLK), BF16), pltpu.VMEM((NP, D, FF_BLK), BF16),
                        pltpu.VMEM((NP, FF_BLK, D), BF16), pltpu.VMEM((NP, FF_BLK, D), F32),
                        pltpu.VMEM((NP, FF_BLK, D), F32), pltpu.VMEM((NP, FF_BLK, D), F32)],
        compiler_params=_params(("arbitrary", "arbitrary")),
    )(h3, do3, gs, us, g2g, pr1g)


IN_HBM = pl.BlockSpec(memory_space=pltpu.HBM)
IN_SEMAPHORES = pl.BlockSpec(memory_space=pltpu.SEMAPHORE)


def _split_copies(refs, na, nw, sems):
    srcs, lands = refs[:na + nw], refs[na + nw:2 * (na + nw)]
    send, recv = sems
    x, y, c, _ = _my_place()
    me = 4 * x + 2 * y + c
    pairs = []
    for d, ((px, py, pc), pidx) in enumerate(_peers(x, y, c)):
        for a in range(na + nw):
            k = d * (na + nw) + a
            src = srcs[a].at[2 * px + py, pc] if a < na else srcs[a]
            pairs.append((_remote(src, lands[a].at[me], send.at[k], recv.at[k], (px, py, pc)),
                          _remote(src, lands[a].at[pidx], send.at[k], recv.at[k], (px, py, pc))))
    return pairs


def _exchange_start(contribs, whole, tag):
    na, nw = len(contribs), len(whole)
    cs = [_by_halves(a) for a in contribs]
    flying = cs + list(whole) + [lax.empty(_slots(a).shape, a.dtype) for a in cs] + [
        lax.empty((N_DEV,) + a.shape, a.dtype) for a in whole]
    nf = len(flying)

    def body(*refs):
        for mine, _ in _split_copies(refs[:nf], na, nw, refs[nf:nf + 2]):
            mine.start()
        refs[-1][...] = jnp.zeros_like(refs[-1])

    ncopies = (N_DEV - 1) * (na + nw)
    outs = pl.pallas_call(
        body,
        name="exchange_start_" + tag,
        in_specs=[IN_HBM] * nf,
        out_specs=[IN_SEMAPHORES] * 2 + [IN_HBM] * nf + [pl.BlockSpec(memory_space=pltpu.VMEM)],
        out_shape=[pltpu.SemaphoreType.DMA((ncopies,)), pltpu.SemaphoreType.DMA((ncopies,))]
        + [pltpu.HBM(a.shape, a.dtype) for a in flying] + [jax.ShapeDtypeStruct((8, 128), F32)],
        input_output_aliases={k: 2 + k for k in range(nf)},
        compiler_params=pltpu.CompilerParams(has_side_effects=pltpu.SideEffectType.DATAFLOW_SIDE_EFFECTING),
    )(*[pltpu.with_memory_space_constraint(a, pltpu.HBM) for a in flying])
    return outs[:2], outs[2:2 + nf], outs[-1]


def _exchange_wait(sems, flying, na, nw, after, tag):
    nf = len(flying)

    def body(*refs):
        for mine, theirs in _split_copies(refs[:nf], na, nw, refs[nf:nf + 2]):
            theirs.wait_recv()
            mine.wait_send()

    outs = pl.pallas_call(
        body,
        name="exchange_wait_" + tag,
        in_specs=[IN_HBM] * nf + [IN_SEMAPHORES] * 2 + [ANY] * len(after),
        out_specs=[IN_HBM] * nf,
        out_shape=[pltpu.HBM(a.shape, a.dtype) for a in flying],
        input_output_aliases={k: k for k in range(nf)},
        compiler_params=pltpu.CompilerParams(has_side_effects=pltpu.SideEffectType.DATAFLOW_SIDE_EFFECTING),
    )(*flying, *sems, *after)
    return outs[na + nw:]


def _sum_peers(parts, own, place, steps, tag, after=()):
    _, rows, w = parts.shape
    tr = rows // steps
    if own.ndim == 3:
        own = _by_halves(own)

    def body(place_ref, *refs):
        p_refs, own_ref, o_ref = refs[:N_DEV], refs[N_DEV], refs[-1]
        me = place_ref[2]
        acc = None
        for s in range(N_DEV):
            term = jnp.where(me == s, own_ref[...], p_refs[s][...]).astype(F32)
            acc = term if acc is None else acc + term
        o_ref[...] = acc

    def other(s):
        return lambda i, pr: (jnp.where(pr[2] == s, (s + 1) % N_DEV, s), i, 0)

    own_spec = (pl.BlockSpec((None, None, tr, w), lambda i, pr: (pr[0], pr[1], i, 0)) if own.ndim == 4 else
                pl.BlockSpec((tr, w), lambda i, pr: (i, 0)))
    out_spec = (pl.BlockSpec((None, tr, w), lambda i, pr: (pr[1], i, 0)) if own.ndim == 4 else
                pl.BlockSpec((tr, w), lambda i, pr: (i, 0)))
    return pl.pallas_call(
        body,
        name="sum_peers_" + tag,
        grid_spec=pltpu.PrefetchScalarGridSpec(
            num_scalar_prefetch=1,
            grid=(steps,),
            in_specs=[pl.BlockSpec((None, tr, w), other(s)) for s in range(N_DEV)] + [own_spec] + [ANY] * len(after),
            out_specs=out_spec,
        ),
        out_shape=jax.ShapeDtypeStruct((2, rows, w) if own.ndim == 4 else (rows, w), F32),
        compiler_params=_params(("arbitrary",)),
    )(place, *([parts] * N_DEV), own, *after)


def _pair_gather(bufs, tag):
    np_ = len(bufs)

    def body(*refs):
        srcs, dsts = refs[:np_], refs[np_:2 * np_]
        send, recv = refs[2 * np_:]
        x, y, c, _ = _my_place()
        cps = []
        for p in range(np_):
            cp = _remote(srcs[p].at[c], dsts[p].at[c], send.at[p], recv.at[p], (x, y, 1 - c))
            cp.start()
            cps.append(cp)
        for p, cp in enumerate(cps):
            other = dsts[p].at[1 - c]
            _remote(other, other, send.at[p], recv.at[p], (x, y, 1 - c)).wait_recv()
            cp.wait_send()

    outs = pl.pallas_call(
        body,
        name="pair_gather_" + tag,
        in_specs=[ANY] * np_,
        out_specs=[ANY] * np_,
        out_shape=[jax.ShapeDtypeStruct(a.shape, F32) for a in bufs],
        input_output_aliases={p: p for p in range(np_)},
        scratch_shapes=[pltpu.SemaphoreType.DMA((np_,))] * 2,
    )(*bufs)
    return [o.reshape(2 * a.shape[1], a.shape[2]) for o, a in zip(outs, bufs)]


def _adamw(gsrc, row0, w, m, v, tr, tag):
    rows, width = w.shape
    off = row0 // tr
    bc1 = 1.0 - ADAM_B1 ** ADAM_STEP
    bc2 = 1.0 - ADAM_B2 ** ADAM_STEP

    def body(g_ref, w_ref, m_ref, v_ref, go_ref, d_ref, mo_ref, vo_ref):
        g = g_ref[...]
        m2 = ADAM_B1 * m_ref[...] + (1.0 - ADAM_B1) * g
        v2 = ADAM_B2 * v_ref[...] + (1.0 - ADAM_B2) * (g * g)
        go_ref[...] = g
        mo_ref[...] = m2
        vo_ref[...] = v2
        d_ref[...] = -ADAM_LR * ((m2 / bc1) / (jnp.sqrt(v2 / bc2) + ADAM_EPS) + ADAM_WD * w_ref[...])

    here = pl.BlockSpec((tr, width), lambda i: (i, 0))
    return pl.pallas_call(
        body,
        name="adamw_" + tag,
        grid=(rows // tr,),
        in_specs=[pl.BlockSpec((tr, width), lambda i: (off + i, 0)), here, here, here],
        out_specs=[here] * 4,
        out_shape=[jax.ShapeDtypeStruct((rows, width), F32)] * 4,
        compiler_params=_params(("arbitrary",)),
    )(gsrc, w, m, v)


def kernel(x, mem, mix_pre_g, w_mix_in, conv_a_w, conv_b_w, conv_b_b, ln_b_g, ln_b_b, w_mix_out, mix_post_g, xa_pre_g, mem_norm_g, w_q, w_k, w_v, w_o, xa_post_g, ffn_pre_g, w_gate, w_up, w_down, ffn_post_g, loss_target, m_mix_pre_g, m_w_mix_in, m_conv_a_w, m_conv_b_w, m_conv_b_b, m_ln_b_g, m_ln_b_b, m_w_mix_out, m_mix_post_g, m_xa_pre_g, m_mem_norm_g, m_w_q, m_w_k, m_w_v, m_w_o, m_xa_post_g, m_ffn_pre_g, m_w_gate, m_w_up, m_w_down, m_ffn_post_g, v_mix_pre_g, v_w_mix_in, v_conv_a_w, v_conv_b_w, v_conv_b_b, v_ln_b_g, v_ln_b_b, v_w_mix_out, v_mix_post_g, v_xa_pre_g, v_mem_norm_g, v_w_q, v_w_k, v_w_v, v_w_o, v_xa_post_g, v_ffn_pre_g, v_w_gate, v_w_up, v_w_down, v_ffn_post_g):
    given = dict(locals())
    names = ["mix_pre_g", "w_mix_in", "conv_a_w", "conv_b_w", "conv_b_b", "ln_b_g", "ln_b_b", "w_mix_out",
             "mix_post_g", "xa_pre_g", "mem_norm_g", "w_q", "w_k", "w_v", "w_o", "xa_post_g", "ffn_pre_g",
             "w_gate", "w_up", "w_down", "ffn_post_g"]
    row = lambda a: a.reshape(1, -1)
    cx, cy, cc = lax.axis_index("x"), lax.axis_index("y"), lax.axis_index("c")
    chip = 2 * cx + cy
    ca_blk = conv_a_w.shape[1]

    conv_rows = CONV_A_W + CONV_B_W
    sw = jnp.concatenate([conv_a_w, conv_b_w, jnp.zeros((SMALL_W_ROWS - conv_rows, ca_blk), F32)], axis=0)
    g1g, pr0g, swg = _gather_weights([w_mix_in.astype(BF16), w_mix_out.astype(BF16), sw])
    conv_full = jnp.transpose(swg[:, :conv_rows, :], (1, 0, 2)).reshape(conv_rows, N_CHIPS * ca_blk)
    wa, wb = conv_full[:CONV_A_W], conv_full[CONV_A_W:]
    pr1 = jnp.concatenate([w_q, w_k, w_v, w_o, w_down], axis=0).astype(BF16)
    g2 = jnp.concatenate([w_gate, w_up], axis=0).astype(BF16)

    xs, ms, tgt = x[0], mem[0], loss_target[0]
    x1, u, o1, z1, pr1g, g2g = _mix_fwd(xs, row(mix_pre_g), row(mix_post_g), wa, wb, row(conv_b_b), row(ln_b_g),
                                        row(ln_b_b), g1g, pr0g, [pr1, g2])
    memn, kb, vb = _mem_kv(ms, row(mem_norm_g), pr1g)
    x2, q, o2, rx1, ro2 = _xattn_fwd(x1, row(xa_pre_g), row(xa_post_g), kb, vb, pr1g)
    h3, gs, us, do3, dx3, loss_part, d_ffn_post, rx2 = _ffn_fwd(x2, tgt, row(ffn_pre_g), row(ffn_post_g), g2g, pr1g)

    dh3p, cf1, cf2 = _ffn_bwd(h3, do3, gs, us, g2g, pr1g)
    dx1, d_ffn_pre, d_xa_post, d_xa_pre, dk, dv, cxa, yf1, yf2 = _xattn_bwd(
        dx3, dh3p, x2, x1, o2, q, rx2, ro2, rx1, kb, vb, row(ffn_pre_g), row(xa_post_g), row(xa_pre_g), pr1g, cf1, cf2)
    d_mem_g, cxa = _mem_bwd(dk, dv, ms, memn, row(mem_norm_g), pr1g, cxa)
    sems_x, flying_x, token_x = _exchange_start([cxa], [], "attn")
    dx, d_mix_pre, d_mix_post, dwa, dwb, dbb, dlng, dlnb, cm1, cm2 = _mix_bwd(
        dx1, xs, o1, u, z1, row(mix_pre_g), row(mix_post_g), wa, wb, row(ln_b_g), row(ln_b_b), g1g, pr0g, token_x)
    (yx,) = _exchange_wait(sems_x, flying_x, 1, 0, [d_mix_pre], "attn")

    small_parts = [d_mix_pre, dwa, dwb, dbb, dlng, dlnb, d_mix_post, d_xa_pre, d_mem_g, d_xa_post, d_ffn_pre,
                   d_ffn_post, loss_part]
    sizes = [p.size for p in small_parts]
    small = jnp.concatenate([p.reshape(-1) for p in small_parts])
    small_rows = -(-small.size // (8 * 128)) * 8
    small = jnp.pad(small, (0, small_rows * 128 - small.size)).reshape(small_rows, 128)
    sems, flying, token = _exchange_start([cm1, cm2], [small], "mix")
    place = jnp.stack([chip, cc, 2 * chip + cc]).astype(jnp.int32)

    res = {}

    def update(nm, src, row0, tr, transposed=False):
        view = (lambda a: a.T) if transposed else (lambda a: a)
        outs = _adamw(src, row0, view(given[nm]), view(given["m_" + nm]), view(given["v_" + nm]), tr, nm)
        res[nm] = [view(o) for o in outs]

    early = [(yf1, cf1, "gate_up"), (yf2, cf2, "down"), (yx, cxa, "attn")]
    r_gu, r_down, r_attn = _pair_gather([_sum_peers(y, c, place, 2, t, after=(token,)) for y, c, t in early], "early")
    update("w_gate", r_gu, 0, FF_BLK // 2, transposed=True)
    update("w_up", r_gu, FF_BLK, FF_BLK // 2, transposed=True)
    update("w_down", r_down, 0, FF_BLK // 2)
    for nm, row0 in (("w_q", GX_Q), ("w_k", GX_K), ("w_v", GX_V), ("w_o", GX_O)):
        update(nm, r_attn, row0, ROW_BLK)
    done = [res[nm][1] for nm in ("w_gate", "w_up", "w_down", "w_q", "w_k", "w_v", "w_o")]

    ym1, ym2, small_all = _exchange_wait(sems, flying, 2, 1, done, "mix")
    r_in, r_out = _pair_gather([_sum_peers(y, c, place, 2, t) for y, c, t in
                                [(ym1, cm1, "mix_in"), (ym2, cm2, "mix_out")]], "late")
    update("w_mix_in", r_in, 0, D_MODEL // 2)
    update("w_mix_out", r_out, 0, ROW_BLK)
    small_sum = _sum_peers(small_all, small, place, 1, "small").reshape(-1)
    red, pos = [], 0
    for p, sz in zip(small_parts, sizes):
        red.append(small_sum[pos:pos + sz].reshape(p.shape))
        pos += sz
    (r_mix_pre, r_wa, r_wb, r_bb, r_lng, r_lnb, r_mix_post, r_xa_pre, r_mem_g, r_xa_post, r_ffn_pre, r_ffn_post,
     r_loss) = red
    loss = r_loss[0, 0]

    small_grads = {"mix_pre_g": r_mix_pre, "conv_b_b": r_bb, "ln_b_g": r_lng, "ln_b_b": r_lnb,
                   "mix_post_g": r_mix_post, "xa_pre_g": r_xa_pre, "mem_norm_g": r_mem_g, "xa_post_g": r_xa_post,
                   "ffn_pre_g": r_ffn_pre, "ffn_post_g": r_ffn_post,
                   "conv_a_w": lax.dynamic_slice_in_dim(r_wa, chip * ca_blk, ca_blk, axis=1),
                   "conv_b_w": lax.dynamic_slice_in_dim(r_wb, chip * ca_blk, ca_blk, axis=1)}
    small_names = list(small_grads)

    def packed(prefix, grads=None):
        flat = jnp.concatenate([(grads[nm] if grads else given[prefix + nm]).reshape(-1) for nm in small_names])
        rows8 = -(-flat.size // (8 * 128)) * 8
        return jnp.pad(flat, (0, rows8 * 128 - flat.size)).reshape(rows8, 128)

    gp = packed("", small_grads)
    outs = _adamw(gp, 0, packed(""), packed("m_"), packed("v_"), gp.shape[0], "small")
    pos = 0
    for nm in small_names:
        shape = given[nm].shape
        sz = given[nm].size
        res[nm] = [o.reshape(-1)[pos:pos + sz].reshape(shape) for o in outs]
        pos += sz

    return (loss, dx[None], *[res[nm][0] for nm in names], *[res[nm][1] for nm in names],
            *[res[nm][2] for nm in names], *[res[nm][3] for nm in names])
```
